```python
import math
import jax, jax.numpy as jnp
from jax import lax
import numpy as np


D_MODEL = 1024
BATCH = 4
SEQ = 8192
DEPTH = 2

GRID_W = 64
CTX_LEN = 256
D_SSM = 512
SSM_GROUP = 16
N_SSM_GROUPS = D_SSM // SSM_GROUP
SSM_STATE = 64
N_HEADS = 8
QK_NOPE = 64
QK_ROPE = 32
V_HEAD = 64
Q_LORA = 256
KV_LORA = 128
D_QK = QK_NOPE + QK_ROPE
D_ATT = N_HEADS * V_HEAD
D_MIX = D_SSM + D_ATT
D_IN = D_SSM + Q_LORA + KV_LORA + QK_ROPE
AXIS_ROPE = QK_ROPE // 2
ROPE_FREQS = AXIS_ROPE // 2
ROPE_BASE = 10000.0
ATT_SCALE = 1.0 / math.sqrt(D_QK)
Q_BLOCK = 128
D_FF = 2816
N_EXPERTS = 8
TOP_K = 2
D_FF_EXPERT = 1408
N_DENSE = (DEPTH + 1) // 2
N_MOE = DEPTH // 2
EPS = 1e-6

kernel_name = 'hymba_s5_mla_moe_dit_prefix'


def rmsnorm(x, g):
    xf = x.astype(jnp.float32)
    y = xf * lax.rsqrt(jnp.mean(xf * xf, axis=-1, keepdims=True) + EPS)
    return (y * g.astype(jnp.float32)).astype(x.dtype)


def modulate(h, shift, scale):
    return h * (1 + scale) + shift


def axial_rope_tables(L):
    rows = L // GRID_W
    t = jnp.arange(L)
    row = jnp.repeat(jnp.arange(rows), GRID_W, total_repeat_length=L).astype(jnp.float32)
    col = (t % GRID_W).astype(jnp.float32)
    inv_freq = ROPE_BASE ** (-2.0 * jnp.arange(ROPE_FREQS, dtype=jnp.float32) / AXIS_ROPE)
    ang = jnp.stack([row[:, None] * inv_freq, col[:, None] * inv_freq], axis=1)
    return jnp.cos(ang), jnp.sin(ang)


def axial_rope(x, cos, sin):
    xs = x.reshape(x.shape[:-1] + (2, 2, ROPE_FREQS))
    x1, x2 = xs[..., 0, :], xs[..., 1, :]
    shape = (1, cos.shape[0]) + (1,) * (x.ndim - 3) + (2, ROPE_FREQS)
    cs, sn = cos.reshape(shape), sin.reshape(shape)
    out = jnp.stack([x1 * cs - x2 * sn, x2 * cs + x1 * sn], axis=-2)
    return out.reshape(x.shape)


def s5_discretize(a_re, a_im, log_dt, b_re, b_im):
    a_re, a_im = a_re.astype(jnp.float32), a_im.astype(jnp.float32)
    dt = jnp.exp(log_dt.astype(jnp.float32))[..., None]
    mag = jnp.exp(dt * a_re)
    ab_re, ab_im = mag * jnp.cos(dt * a_im), mag * jnp.sin(dt * a_im)
    den = a_re * a_re + a_im * a_im
    num_re = ab_re - 1.0
    f_re = (num_re * a_re + ab_im * a_im) / den
    f_im = (ab_im * a_re - num_re * a_im) / den
    b_re, b_im = b_re.astype(jnp.float32), b_im.astype(jnp.float32)
    bb_re = f_re[..., None] * b_re - f_im[..., None] * b_im
    bb_im = f_re[..., None] * b_im + f_im[..., None] * b_re
    return ab_re, ab_im, bb_re, bb_im


def s5_scan(ab_re, ab_im, bb_re, bb_im, ug, h0_re, h0_im, reverse):
    L = ug.shape[1]
    bu_re = jnp.einsum('blgc,gpc->lbgp', ug, bb_re)
    bu_im = jnp.einsum('blgc,gpc->lbgp', ug, bb_im)
    a_re = jnp.broadcast_to(ab_re, (L, 1) + ab_re.shape)
    a_im = jnp.broadcast_to(ab_im, (L, 1) + ab_im.shape)

    def combine(e1, e2):
        a1r, a1i, b1r, b1i = e1
        a2r, a2i, b2r, b2i = e2
        return (a2r * a1r - a2i * a1i, a2r * a1i + a2i * a1r,
                a2r * b1r - a2i * b1i + b2r, a2r * b1i + a2i * b1r + b2i)

    pa_re, pa_im, h_re, h_im = lax.associative_scan(combine, (a_re, a_im, bu_re, bu_im),
                                                    reverse=reverse, axis=0)
    h_re = h_re + pa_re * h0_re[None] - pa_im * h0_im[None]
    h_im = h_im + pa_re * h0_im[None] + pa_im * h0_re[None]
    return h_re, h_im


def s5_readout(h_re, h_im, c_re, c_im):
    return (jnp.einsum('lbgp,gcp->blgc', h_re, c_re)
            - jnp.einsum('lbgp,gcp->blgc', h_im, c_im))


def s5_glu(y, w_glu, b_glu):
    z = jax.nn.gelu(y)
    return z * jax.nn.sigmoid(z @ w_glu.astype(jnp.float32) + b_glu.astype(jnp.float32))


def s5_mixer(u, uc, a_re, a_im, log_dt, b_re, b_im, c_re, c_im, d_skip, w_glu, b_glu, need_ctx):
    B, L, _ = u.shape
    Lc = uc.shape[1]
    ab_re, ab_im, bb_re, bb_im = s5_discretize(a_re, a_im, log_dt, b_re, b_im)
    c_re, c_im = c_re.astype(jnp.float32), c_im.astype(jnp.float32)
    ug = u.astype(jnp.float32).reshape(B, L, N_SSM_GROUPS, SSM_GROUP)
    ucg = uc.astype(jnp.float32).reshape(B, Lc, N_SSM_GROUPS, SSM_GROUP)
    zero = jnp.zeros((B, N_SSM_GROUPS, SSM_STATE), jnp.float32)
    d = d_skip.astype(jnp.float32).reshape(N_SSM_GROUPS, SSM_GROUP)
    y = d * ug
    yc = d * ucg if need_ctx else None
    for dr, rev in enumerate((False, True)):
        hc_re, hc_im = s5_scan(ab_re[dr], ab_im[dr], bb_re[dr], bb_im[dr], ucg, zero, zero, rev)
        last = 0 if rev else Lc - 1
        h_re, h_im = s5_scan(ab_re[dr], ab_im[dr], bb_re[dr], bb_im[dr], ug,
                             hc_re[last], hc_im[last], rev)
        y = y + s5_readout(h_re, h_im, c_re[dr], c_im[dr])
        if need_ctx:
            yc = yc + s5_readout(hc_re, hc_im, c_re[dr], c_im[dr])
    out = s5_glu(y.reshape(B, L, D_SSM), w_glu, b_glu).astype(u.dtype)
    out_c = s5_glu(yc.reshape(B, Lc, D_SSM), w_glu, b_glu).astype(u.dtype) if need_ctx else None
    return out, out_c


def mla_queries(cq, q_norm, w_uq):
    B, L, _ = cq.shape
    return (rmsnorm(cq, q_norm) @ w_uq).reshape(B, L, N_HEADS, D_QK)


def mla_keys_values(ckv, kr, kv_norm, w_ukv):
    B, L, _ = ckv.shape
    kv = (rmsnorm(ckv, kv_norm) @ w_ukv).reshape(B, L, N_HEADS, QK_NOPE + V_HEAD)
    k_rope = jnp.broadcast_to(kr[:, :, None, :], (B, L, N_HEADS, QK_ROPE))
    k = jnp.concatenate([kv[..., :QK_NOPE], k_rope], axis=-1)
    return k, kv[..., QK_NOPE:]


def softmax_attend(q, k, v):
    s = jnp.einsum('bqhd,bkhd->bhqk', q, k).astype(jnp.float32) * ATT_SCALE
    p = jax.nn.softmax(s, axis=-1).astype(v.dtype)
    return jnp.einsum('bhqk,bkhd->bqhd', p, v)


def mla_mixer(cq, ckv, kr, cq_c, ckv_c, kr_c, q_norm, kv_norm, w_uq, w_ukv, cos, sin, need_ctx):
    B, L, _ = cq.shape
    Lc = ckv_c.shape[1]
    q = mla_queries(cq, q_norm, w_uq)
    q = jnp.concatenate([q[..., :QK_NOPE], axial_rope(q[..., QK_NOPE:], cos, sin)], axis=-1)
    k, v = mla_keys_values(ckv, axial_rope(kr, cos, sin), kv_norm, w_ukv)
    k_c, v_c = mla_keys_values(ckv_c, kr_c, kv_norm, w_ukv)
    k_all = jnp.concatenate([k, k_c], axis=1)
    v_all = jnp.concatenate([v, v_c], axis=1)
    nb = L // Q_BLOCK
    q_blocks = q.reshape(B, nb, Q_BLOCK, N_HEADS, D_QK).swapaxes(0, 1)
    out = lax.map(lambda qb: softmax_attend(qb, k_all, v_all), q_blocks)
    out = out.swapaxes(0, 1).reshape(B, L, D_ATT)
    out_c = None
    if need_ctx:
        q_c = mla_queries(cq_c, q_norm, w_uq)
        out_c = softmax_attend(q_c, k_c, v_c).reshape(B, Lc, D_ATT)
    return out, out_c


def swiglu(h, w1, w3, w2):
    return (jax.nn.silu(h @ w1) * (h @ w3)) @ w2


def moe_swiglu(h, router, w1, w3, w2):
    logits = (h @ router).astype(jnp.float32)
    top_val, top_idx = lax.top_k(logits, TOP_K)
    weights = jax.nn.softmax(top_val, axis=-1)
    gate = jnp.einsum('blk,blke->ble', weights,
                      jax.nn.one_hot(top_idx, N_EXPERTS, dtype=jnp.float32)).astype(h.dtype)
    y = jnp.zeros_like(h)
    for e in range(N_EXPERTS):
        y = y + gate[..., e:e + 1] * swiglu(h, w1[e], w3[e], w2[e])
    return y


def setup_inputs(seed: int = 0) -> dict:
    key = jax.random.key(seed)
    keys = iter(jax.random.split(key, 40))

    def nrm(shape, scale):
        return scale * jax.random.normal(next(keys), shape, jnp.float32)

    def gain(shape):
        return 1.0 + nrm(shape, 0.02)

    G, P, Hc = N_SSM_GROUPS, SSM_STATE, SSM_GROUP
    n_idx = jnp.arange(P, dtype=jnp.float32)
    return {
        'x': nrm((BATCH, SEQ, D_MODEL), 1.0),
        'c': nrm((BATCH, D_MODEL), 1.0),
        'ctx': nrm((BATCH, CTX_LEN, D_MODEL), 1.0),
        'c_ctx': nrm((D_MODEL,), 1.0),
        'w_ada': nrm((DEPTH, D_MODEL, 6 * D_MODEL), 0.5 * D_MODEL ** -0.5),
        'b_ada': nrm((DEPTH, 6 * D_MODEL), 0.02),
        'norm_mix': gain((DEPTH, D_MODEL)),
        'norm_ffn': gain((DEPTH, D_MODEL)),
        'w_in': nrm((DEPTH, D_MODEL, D_IN), D_MODEL ** -0.5),
        'q_norm': gain((DEPTH, Q_LORA)),
        'kv_norm': gain((DEPTH, KV_LORA)),
        'w_uq': nrm((DEPTH, Q_LORA, N_HEADS * D_QK), Q_LORA ** -0.5),
        'w_ukv': nrm((DEPTH, KV_LORA, N_HEADS * (QK_NOPE + V_HEAD)), KV_LORA ** -0.5),
        'ssm_a_re': -0.5 + nrm((DEPTH, 2, G, P), 0.01),
        'ssm_a_im': math.pi * n_idx + nrm((DEPTH, 2, G, P), 0.01),
        'ssm_log_dt': jax.random.uniform(next(keys), (DEPTH, 2, G), jnp.float32,
                                         math.log(1e-3), math.log(1e-1)),
        'ssm_b_re': nrm((DEPTH, 2, G, P, Hc), (2 * Hc) ** -0.5),
        'ssm_b_im': nrm((DEPTH, 2, G, P, Hc), (2 * Hc) ** -0.5),
        'ssm_c_re': nrm((DEPTH, 2, G, Hc, P), P ** -0.5),
        'ssm_c_im': nrm((DEPTH, 2, G, Hc, P), P ** -0.5),
        'ssm_d': nrm((DEPTH, D_SSM), 1.0),
        'w_glu': nrm((DEPTH, D_SSM, D_SSM), D_SSM ** -0.5),
        'b_glu': nrm((DEPTH, D_SSM), 0.02),
        'w_out': nrm((DEPTH, D_MIX, D_MODEL), D_MIX ** -0.5),
        'ffn_w1': nrm((N_DENSE, D_MODEL, D_FF), D_MODEL ** -0.5),
        'ffn_w3': nrm((N_DENSE, D_MODEL, D_FF), D_MODEL ** -0.5),
        'ffn_w2': nrm((N_DENSE, D_FF, D_MODEL), D_FF ** -0.5),
        'moe_router': nrm((N_MOE, D_MODEL, N_EXPERTS), D_MODEL ** -0.5),
        'moe_w1': nrm((N_MOE, N_EXPERTS, D_MODEL, D_FF_EXPERT), D_MODEL ** -0.5),
        'moe_w3': nrm((N_MOE, N_EXPERTS, D_MODEL, D_FF_EXPERT), D_MODEL ** -0.5),
        'moe_w2': nrm((N_MOE, N_EXPERTS, D_FF_EXPERT, D_MODEL), D_FF_EXPERT ** -0.5),
        'final_norm': gain((D_MODEL,)),
    }


def reference(x, c, ctx, c_ctx, w_ada, b_ada, norm_mix, norm_ffn, w_in, q_norm, kv_norm,
              w_uq, w_ukv, ssm_a_re, ssm_a_im, ssm_log_dt, ssm_b_re, ssm_b_im, ssm_c_re,
              ssm_c_im, ssm_d, w_glu, b_glu, w_out, ffn_w1, ffn_w3, ffn_w2, moe_router,
              moe_w1, moe_w3, moe_w2, final_norm):
    L = x.shape[1]
    cos, sin = axial_rope_tables(L)
    cos, sin = cos.astype(x.dtype), sin.astype(x.dtype)
    silu_c = jax.nn.silu(c)
    silu_cc = jax.nn.silu(c_ctx)
    splits = (D_SSM, D_SSM + Q_LORA, D_SSM + Q_LORA + KV_LORA)
    xc = ctx
    for i in range(DEPTH):
        need_ctx = i < DEPTH - 1
        mod = (silu_c @ w_ada[i] + b_ada[i])[:, None, :]
        mod_c = silu_cc @ w_ada[i] + b_ada[i]
        sh1, sc1, g1, sh2, sc2, g2 = jnp.split(mod, 6, axis=-1)
        sh1c, sc1c, g1c, sh2c, sc2c, g2c = jnp.split(mod_c, 6, axis=-1)
        z = modulate(rmsnorm(x, norm_mix[i]), sh1, sc1) @ w_in[i]
        zc = modulate(rmsnorm(xc, norm_mix[i]), sh1c, sc1c) @ w_in[i]
        u, cq, ckv, kr = jnp.split(z, splits, axis=-1)
        uc, cq_c, ckv_c, kr_c = jnp.split(zc, splits, axis=-1)
        y_ssm, yc_ssm = s5_mixer(u, uc, ssm_a_re[i], ssm_a_im[i], ssm_log_dt[i], ssm_b_re[i],
                                 ssm_b_im[i], ssm_c_re[i], ssm_c_im[i], ssm_d[i], w_glu[i],
                                 b_glu[i], need_ctx)
        y_att, yc_att = mla_mixer(cq, ckv, kr, cq_c, ckv_c, kr_c, q_norm[i], kv_norm[i],
                                  w_uq[i], w_ukv[i], cos, sin, need_ctx)
        x = x + g1 * (jnp.concatenate([y_ssm, y_att], axis=-1) @ w_out[i])
        if need_ctx:
            xc = xc + g1c * (jnp.concatenate([yc_ssm, yc_att], axis=-1) @ w_out[i])
        j = i // 2
        if i % 2 == 0:
            ffn = lambda t: swiglu(t, ffn_w1[j], ffn_w3[j], ffn_w2[j])
        else:
            ffn = lambda t: moe_swiglu(t, moe_router[j], moe_w1[j], moe_w3[j], moe_w2[j])
        x = x + g2 * ffn(modulate(rmsnorm(x, norm_ffn[i]), sh2, sc2))
        if need_ctx:
            xc = xc + g2c * ffn(modulate(rmsnorm(xc, norm_ffn[i]), sh2c, sc2c))
    return rmsnorm(x, final_norm)
```

```python
import functools
import math

import jax
import jax.numpy as jnp
from jax import lax
from jax.experimental import pallas as pl
from jax.experimental.pallas import tpu as pltpu

D_MODEL = 1024
BATCH = 4
SEQ = 8192
DEPTH = 2
GRID_W = 64
CTX_LEN = 256
D_SSM = 512
SSM_GROUP = 16
N_SSM_GROUPS = D_SSM // SSM_GROUP
SSM_STATE = 64
N_HEADS = 8
QK_NOPE = 64
QK_ROPE = 32
V_HEAD = 64
Q_LORA = 256
KV_LORA = 128
D_QK = QK_NOPE + QK_ROPE
D_ATT = N_HEADS * V_HEAD
D_MIX = D_SSM + D_ATT
D_IN = D_SSM + Q_LORA + KV_LORA + QK_ROPE
AXIS_ROPE = QK_ROPE // 2
ROPE_FREQS = AXIS_ROPE // 2
ROPE_BASE = 10000.0
ATT_SCALE = 1.0 / math.sqrt(D_QK)
D_FF = 2816
N_EXPERTS = 8
D_FF_EXPERT = 1408
EPS = 1e-6

N_CTX = BATCH * CTX_LEN
N_LAT = BATCH * SEQ
N_TOT = N_CTX + N_LAT

LANES = 128
HEAD_PAD = 128
TM = 512
LAT_TILES = N_LAT // TM
SEQ_TILES = SEQ // TM
TQ = 512
TK = 512
SSM_T = 128
N_CHUNK_LAT = N_LAT // SSM_T
N_CHUNK = N_TOT // SSM_T
SSM_K = SSM_GROUP * SSM_T
FF_CHUNK = 256
N_FF_CHUNKS = D_FF // FF_CHUNK
MOD_ROWS = 8
VMEM_LIMIT = 52 * 1024 * 1024

F32 = jnp.float32
BF16 = jnp.bfloat16
HI = lax.Precision.HIGHEST


def _cparams(*sem):
    return pltpu.CompilerParams(dimension_semantics=sem, vmem_limit_bytes=VMEM_LIMIT)


def _const_spec(shape):
    nd = len(shape)
    return pl.BlockSpec(shape, lambda *_: (0,) * nd, pipeline_mode=pl.Buffered(1))


def _mod_row(i):
    return jnp.where(i < LAT_TILES, i // SEQ_TILES, BATCH)


def _pos_tile(i):
    return jnp.where(i < LAT_TILES, i % SEQ_TILES, SEQ_TILES + i - LAT_TILES)


def _rms(x, g):
    ms = jnp.mean(x * x, axis=-1, keepdims=True)
    return x * lax.rsqrt(ms + EPS) * g


ADA_TN = 1536


def _ada_kernel(c_ref, w_ref, b_ref, o_ref):
    c = c_ref[...]
    s = c * jax.nn.sigmoid(c)
    o_ref[0] = jnp.dot(s, w_ref[0], precision=HI, preferred_element_type=F32) + b_ref[0]


def _ada_call(cond, w_ada, b_ada):
    n_col = 6 * D_MODEL // ADA_TN
    return pl.pallas_call(
        _ada_kernel,
        grid=(DEPTH, n_col),
        in_specs=[
            pl.BlockSpec((MOD_ROWS, D_MODEL), lambda l, j: (0, 0)),
            pl.BlockSpec((1, D_MODEL, ADA_TN), lambda l, j: (l, 0, j)),
            pl.BlockSpec((1, 1, ADA_TN), lambda l, j: (l, 0, j)),
        ],
        out_specs=pl.BlockSpec((1, MOD_ROWS, ADA_TN), lambda l, j: (l, 0, j)),
        out_shape=jax.ShapeDtypeStruct((DEPTH, MOD_ROWS, 6 * D_MODEL), F32),
        compiler_params=_cparams("arbitrary", "arbitrary"),
        name="ada_mod",
    )(cond, w_ada, b_ada.reshape(DEPTH, 1, 6 * D_MODEL))


def _inproj_kernel(x_ref, mod_ref, g_ref, win_ref, qg_ref, kvg_ref, wq1_ref, wq2_ref,
                   wkk_ref, wv_ref, cq_ref, sq_ref, cs_ref, u_ref, q_ref, k_ref, v_ref):
    x = x_ref[...]
    sh = mod_ref[0, 0:1, :]
    sc = mod_ref[0, 1:2, :]
    xm = (_rms(x, g_ref[...]) * (1.0 + sc) + sh).astype(BF16)
    z = jnp.dot(xm, win_ref[...], preferred_element_type=F32)
    u_ref[...] = z[:, :D_SSM].astype(u_ref.dtype)
    qn = _rms(z[:, D_SSM:D_SSM + Q_LORA], qg_ref[...]).astype(BF16)
    kvn = _rms(z[:, D_SSM + Q_LORA:D_SSM + Q_LORA + KV_LORA], kvg_ref[...]).astype(BF16)
    krr = (z[:, D_SSM + Q_LORA + KV_LORA:] * cs_ref[...]).astype(BF16)
    q1 = jnp.dot(qn, wq1_ref[...], preferred_element_type=F32)
    q2 = jnp.dot(qn, wq2_ref[...], preferred_element_type=F32)
    cq = cq_ref[...]
    sq = sq_ref[...]
    for h in range(N_HEADS):
        sl = slice(h * HEAD_PAD, (h + 1) * HEAD_PAD)
        q_ref[:, sl] = (q1[:, sl] * cq + q2[:, sl] * sq).astype(q_ref.dtype)
    kin = jnp.concatenate([kvn, krr], axis=-1)
    k_ref[...] = jnp.dot(kin, wkk_ref[...], preferred_element_type=F32).astype(k_ref.dtype)
    v_ref[...] = jnp.dot(kvn, wv_ref[...], preferred_element_type=F32).astype(v_ref.dtype)


def _inproj_call(x_all, mod, g_mix, win, qg, kvg, wq1, wq2, wkk, wv, cq_t, sq_t, cs_t):
    n_tiles = N_TOT // TM
    row = lambda i: (i, 0)
    pos = lambda i: (_pos_tile(i), 0)
    return pl.pallas_call(
        _inproj_kernel,
        grid=(n_tiles,),
        in_specs=[
            pl.BlockSpec((TM, D_MODEL), row),
            pl.BlockSpec((1, 6, D_MODEL), lambda i: (_mod_row(i), 0, 0)),
            _const_spec((1, D_MODEL)),
            _const_spec((D_MODEL, D_MODEL)),
            _const_spec((1, Q_LORA)),
            _const_spec((1, KV_LORA)),
            _const_spec((Q_LORA, N_HEADS * HEAD_PAD)),
            _const_spec((Q_LORA, N_HEADS * HEAD_PAD)),
            _const_spec((2 * KV_LORA, N_HEADS * HEAD_PAD)),
            _const_spec((KV_LORA, D_ATT)),
            pl.BlockSpec((TM, LANES), pos),
            pl.BlockSpec((TM, LANES), pos),
            pl.BlockSpec((TM, LANES), pos),
        ],
        out_specs=[
            pl.BlockSpec((TM, D_SSM), row),
            pl.BlockSpec((TM, N_HEADS * HEAD_PAD), row),
            pl.BlockSpec((TM, N_HEADS * HEAD_PAD), row),
            pl.BlockSpec((TM, D_ATT), row),
        ],
        out_shape=[
            jax.ShapeDtypeStruct((N_TOT, D_SSM), BF16),
            jax.ShapeDtypeStruct((N_TOT, N_HEADS * HEAD_PAD), BF16),
            jax.ShapeDtypeStruct((N_TOT, N_HEADS * HEAD_PAD), BF16),
            jax.ShapeDtypeStruct((N_TOT, D_ATT), BF16),
        ],
        compiler_params=_cparams("parallel"),
        name="in_proj",
    )(x_all, mod, g_mix, win, qg, kvg, wq1, wq2, wkk, wv, cq_t, sq_t, cs_t)


def _softmax_step(q, k, v, m, l, acc):
    s = lax.dot_general(q, k, (((1,), (1,)), ((), ())), preferred_element_type=F32)
    m_new = jnp.maximum(m, jnp.max(s, axis=-1, keepdims=True))
    alpha = jnp.exp(m - m_new)
    p = jnp.exp(s - m_new)
    l = alpha * l + jnp.sum(p, axis=-1, keepdims=True)
    acc = alpha * acc + jnp.dot(p.astype(BF16), v, preferred_element_type=F32)
    return m_new, l, acc


def _attn_kernel(*refs, n_chunks, tq):
    if n_chunks:
        q_ref, k_ref, v_ref, kc_ref, vc_ref, o_ref = refs
    else:
        q_ref, kc_ref, vc_ref, o_ref = refs
    outs = []
    for hh in range(2):
        sl = slice(hh * HEAD_PAD, (hh + 1) * HEAD_PAD)
        q = q_ref[:, sl]
        carry = (jnp.full((tq, 1), -jnp.inf, F32), jnp.zeros((tq, 1), F32),
                 jnp.zeros((tq, 2 * V_HEAD), F32))
        if n_chunks:
            def body(c, carry, sl=sl, q=q):
                off = pl.multiple_of(c * TK, TK)
                return _softmax_step(q, k_ref[pl.ds(off, TK), sl], v_ref[pl.ds(off, TK), :], *carry)
            carry = lax.fori_loop(0, n_chunks, body, carry)
        m, l, acc = _softmax_step(q, kc_ref[:, sl], vc_ref[...], *carry)
        outs.append(acc / l)
    lane = lax.broadcasted_iota(jnp.int32, (tq, 2 * V_HEAD), 1)
    o_ref[...] = jnp.where(lane < V_HEAD, outs[0], outs[1]).astype(o_ref.dtype)


def _attn_latent_call(q, k, v):
    qt = SEQ // TQ
    ctx0 = N_LAT // CTX_LEN
    return pl.pallas_call(
        functools.partial(_attn_kernel, n_chunks=SEQ // TK, tq=TQ),
        grid=(BATCH, N_HEADS // 2, qt),
        in_specs=[
            pl.BlockSpec((TQ, 2 * HEAD_PAD), lambda b, h, i: (b * qt + i, h)),
            pl.BlockSpec((SEQ, 2 * HEAD_PAD), lambda b, h, i: (b, h)),
            pl.BlockSpec((SEQ, 2 * V_HEAD), lambda b, h, i: (b, h)),
            pl.BlockSpec((CTX_LEN, 2 * HEAD_PAD), lambda b, h, i: (ctx0 + b, h)),
            pl.BlockSpec((CTX_LEN, 2 * V_HEAD), lambda b, h, i: (ctx0 + b, h)),
        ],
        out_specs=pl.BlockSpec((TQ, 2 * V_HEAD), lambda b, h, i: (b * qt + i, h)),
        out_shape=jax.ShapeDtypeStruct((N_LAT, D_ATT), BF16),
        compiler_params=_cparams("parallel", "parallel", "arbitrary"),
        name="attn_latent",
    )(q, k, v, k, v)


def _attn_ctx_call(q, k, v):
    ctx0 = N_LAT // CTX_LEN
    return pl.pallas_call(
        functools.partial(_attn_kernel, n_chunks=0, tq=CTX_LEN),
        grid=(BATCH, N_HEADS // 2),
        in_specs=[
            pl.BlockSpec((CTX_LEN, 2 * HEAD_PAD), lambda b, h: (ctx0 + b, h)),
            pl.BlockSpec((CTX_LEN, 2 * HEAD_PAD), lambda b, h: (ctx0 + b, h)),
            pl.BlockSpec((CTX_LEN, 2 * V_HEAD), lambda b, h: (ctx0 + b, h)),
        ],
        out_specs=pl.BlockSpec((CTX_LEN, 2 * V_HEAD), lambda b, h: (b, h)),
        out_shape=jax.ShapeDtypeStruct((N_CTX, D_ATT), BF16),
        compiler_params=_cparams("parallel", "parallel"),
        name="attn_ctx",
    )(q, k, v)


def _ssm_kernel(u_ref, cc_ref, ab_ref, ws_ref, wc_ref, at_ref, dv_ref, y_ref,
                wl_scr, m_scr, s_scr, h_scr):
    wl_scr[...] = jnp.dot(cc_ref[0], ab_ref[0], precision=HI, preferred_element_type=F32)

    def build(ci, _):
        lane0 = pl.multiple_of(ci * 2 * SSM_T, 2 * SSM_T)
        row0 = pl.multiple_of(ci * SSM_T, SSM_T)
        for co in range(SSM_GROUP):
            w = wl_scr[co:co + 1, pl.ds(lane0, 2 * SSM_T)]
            wb = jnp.broadcast_to(w, (SSM_T, 2 * SSM_T))
            toe = pltpu.roll(wb, 0, 1, stride=1, stride_axis=0)
            m_scr[pl.ds(row0, SSM_T), co * SSM_T:(co + 1) * SSM_T] = toe[:, :SSM_T].astype(BF16)
        return 0

    lax.fori_loop(0, SSM_GROUP, build, 0)

    u = u_ref[0]
    y = jnp.dot(u, m_scr[...], preferred_element_type=F32)
    s_scr[...] = jnp.dot(u, ws_ref[0], preferred_element_type=F32)

    def advance(h, a_row, s_rows):
        return h * at_ref[0, a_row:a_row + 1, :] + \
            pltpu.roll(h, SSM_STATE, 1) * at_ref[0, a_row + 1:a_row + 2, :] + s_rows

    n_lat_chunks = SEQ // SSM_T
    ctx_chunks = CTX_LEN // SSM_T
    fwd_rows = [N_CHUNK_LAT + BATCH * c for c in range(ctx_chunks)] + \
               [BATCH * k for k in range(n_lat_chunks)]
    rev_rows = [N_CHUNK_LAT + BATCH * c for c in reversed(range(ctx_chunks))] + \
               [BATCH * k for k in reversed(range(n_lat_chunks))]
    hf = jnp.zeros((BATCH, 2 * SSM_STATE), F32)
    hr = jnp.zeros((BATCH, 2 * SSM_STATE), F32)
    for rf, rr in zip(fwd_rows, rev_rows):
        h_scr[rf:rf + BATCH, 0:LANES] = hf
        hf = advance(hf, 0, s_scr[rf:rf + BATCH, 0:LANES])
        h_scr[rr:rr + BATCH, LANES:2 * LANES] = hr
        hr = advance(hr, 2, s_scr[rr:rr + BATCH, LANES:2 * LANES])

    y = y + jnp.dot(h_scr[...].astype(BF16), wc_ref[0], preferred_element_type=F32)
    y_ref[0] = y + u.astype(F32) * dv_ref[0]


def _ssm_call(u_t, cc, ab, ws, wc, at, dv):
    g3 = lambda g: (g, 0, 0)
    return pl.pallas_call(
        _ssm_kernel,
        grid=(N_SSM_GROUPS,),
        in_specs=[
            pl.BlockSpec((1, N_CHUNK, SSM_K), g3),
            pl.BlockSpec((1, SSM_GROUP, 4 * SSM_STATE), g3),
            pl.BlockSpec((1, 4 * SSM_STATE, SSM_GROUP * 2 * SSM_T), g3),
            pl.BlockSpec((1, SSM_K, 4 * SSM_STATE), g3),
            pl.BlockSpec((1, 4 * SSM_STATE, SSM_K), g3),
            pl.BlockSpec((1, 4, 2 * SSM_STATE), g3),
            pl.BlockSpec((1, 1, SSM_K), g3),
        ],
        out_specs=pl.BlockSpec((1, N_CHUNK, SSM_K), g3),
        out_shape=jax.ShapeDtypeStruct((N_SSM_GROUPS, N_CHUNK, SSM_K), F32),
        scratch_shapes=[
            pltpu.VMEM((SSM_GROUP, SSM_GROUP * 2 * SSM_T), F32),
            pltpu.VMEM((SSM_K, SSM_K), BF16),
            pltpu.VMEM((N_CHUNK, 4 * SSM_STATE), F32),
            pltpu.VMEM((N_CHUNK, 4 * SSM_STATE), F32),
        ],
        compiler_params=_cparams("parallel"),
        name="s5_mixer",
    )(u_t, cc, ab, ws, wc, at, dv)


def _ssm_tables(a_re, a_im, log_dt, b_re, b_im, c_re, c_im, d_skip):
    G, P, H, T = N_SSM_GROUPS, SSM_STATE, SSM_GROUP, SSM_T
    a_re, a_im = a_re.astype(F32), a_im.astype(F32)
    dt = jnp.exp(log_dt.astype(F32))[..., None]
    den = a_re * a_re + a_im * a_im
    mag1 = jnp.exp(dt * a_re)
    ab_re, ab_im = mag1 * jnp.cos(dt * a_im), mag1 * jnp.sin(dt * a_im)
    num_re = ab_re - 1.0
    f_re = (num_re * a_re + ab_im * a_im) / den
    f_im = (ab_im * a_re - num_re * a_im) / den
    b_re, b_im = b_re.astype(F32), b_im.astype(F32)
    bb_re = f_re[..., None] * b_re - f_im[..., None] * b_im
    bb_im = f_re[..., None] * b_im + f_im[..., None] * b_re
    n = jnp.arange(T + 1, dtype=F32)
    mag = jnp.exp((dt * a_re)[..., None] * n)
    ph = (dt * a_im)[..., None] * n
    pw_re, pw_im = mag * jnp.cos(ph), mag * jnp.sin(ph)
    c_re, c_im = c_re.astype(F32), c_im.astype(F32)

    def cmul(xr, xi, yr, yi):
        return xr * yr - xi * yi, xr * yi + xi * yr

    pb_re, pb_im = cmul(pw_re[:, :, :, None, :], pw_im[:, :, :, None, :],
                        bb_re[..., None], bb_im[..., None])
    zeros = jnp.zeros((G, P, H, T), F32)
    f_lag_re = jnp.concatenate([pb_re[0, ..., :T], zeros], axis=-1)
    f_lag_im = jnp.concatenate([pb_im[0, ..., :T], zeros], axis=-1)
    r_rev_re = pb_re[1, ..., 1:T][..., ::-1]
    r_rev_im = pb_im[1, ..., 1:T][..., ::-1]
    z1 = jnp.zeros((G, P, H, T), F32)
    r_lag_re = jnp.concatenate([pb_re[1, ..., :1], z1, r_rev_re], axis=-1)
    r_lag_im = jnp.concatenate([pb_im[1, ..., :1], z1, r_rev_im], axis=-1)
    ab = jnp.concatenate([f_lag_re, f_lag_im, r_lag_re, r_lag_im], axis=1)
    ab = ab.reshape(G, 4 * P, H * 2 * T)
    cc = jnp.concatenate([c_re[0], -c_im[0], c_re[1], -c_im[1]], axis=-1)

    wsf_re, wsf_im = pb_re[0, ..., :T][..., ::-1], pb_im[0, ..., :T][..., ::-1]
    wsr_re, wsr_im = pb_re[1, ..., :T], pb_im[1, ..., :T]
    ws = jnp.concatenate([wsf_re, wsf_im, wsr_re, wsr_im], axis=1)
    ws = ws.reshape(G, 4 * P, H * T).transpose(0, 2, 1).astype(BF16)

    def readout(d, pr, pi):
        cr = c_re[d].transpose(0, 2, 1)[..., None]
        ci = c_im[d].transpose(0, 2, 1)[..., None]
        rr, ri = cmul(cr, ci, pr[:, :, None, :], pi[:, :, None, :])
        return rr, -ri
    wcf_re, wcf_im = readout(0, pw_re[0, ..., 1:T + 1], pw_im[0, ..., 1:T + 1])
    wcr_re, wcr_im = readout(1, pw_re[1, ..., 1:T + 1][..., ::-1], pw_im[1, ..., 1:T + 1][..., ::-1])
    wc = jnp.concatenate([wcf_re, wcf_im, wcr_re, wcr_im], axis=1)
    wc = wc.reshape(G, 4 * P, H * T).astype(BF16)

    t_re, t_im = pw_re[..., T], pw_im[..., T]
    at = jnp.stack([jnp.concatenate([t_re[0], t_re[0]], -1), jnp.concatenate([-t_im[0], t_im[0]], -1),
                    jnp.concatenate([t_re[1], t_re[1]], -1), jnp.concatenate([-t_im[1], t_im[1]], -1)],
                   axis=1)
    dv = jnp.repeat(d_skip.astype(F32).reshape(G, H), T, axis=-1).reshape(G, 1, H * T)
    return cc, ab, ws, wc, at, dv


def _to_group_major(u):
    G, H, T = N_SSM_GROUPS, SSM_GROUP, SSM_T
    ul = u[:N_LAT].reshape(BATCH, SEQ // T, T, G, H).transpose(3, 1, 0, 4, 2)
    uc = u[N_LAT:].reshape(BATCH, CTX_LEN // T, T, G, H).transpose(3, 1, 0, 4, 2)
    return jnp.concatenate([ul.reshape(G, N_CHUNK_LAT, H * T),
                            uc.reshape(G, N_CHUNK - N_CHUNK_LAT, H * T)], axis=1)


def _from_group_major(y):
    G, H, T = N_SSM_GROUPS, SSM_GROUP, SSM_T
    yl = y[:, :N_CHUNK_LAT].reshape(G, SEQ // T, BATCH, H, T).transpose(2, 1, 4, 0, 3)
    yc = y[:, N_CHUNK_LAT:].reshape(G, CTX_LEN // T, BATCH, H, T).transpose(2, 1, 4, 0, 3)
    return jnp.concatenate([yl.reshape(N_LAT, D_SSM), yc.reshape(N_CTX, D_SSM)], axis=0)


def _top2_gates(logits):
    lane = lax.broadcasted_iota(jnp.int32, logits.shape, 1)
    valid = lane < N_EXPERTS
    lg = jnp.where(valid, logits, -jnp.inf)
    m1 = jnp.max(lg, axis=-1, keepdims=True)
    i1 = jnp.min(jnp.where(lg == m1, lane, LANES), axis=-1, keepdims=True)
    lg2 = jnp.where(lane == i1, -jnp.inf, lg)
    m2 = jnp.max(lg2, axis=-1, keepdims=True)
    i2 = jnp.min(jnp.where(lg2 == m2, lane, LANES), axis=-1, keepdims=True)
    e2 = jnp.exp(m2 - m1)
    w1 = 1.0 / (1.0 + e2)
    return jnp.where(lane == i1, w1, 0.0) + jnp.where(lane == i2, e2 * w1, 0.0)


def _mix_kernel(*refs, with_router):
    if with_router:
        (x_ref, ys_ref, ya_ref, mod_ref, wglu_ref, bglu_ref, wout_ref, gffn_ref, router_ref,
         x1_ref, h2_ref, gate_ref) = refs
    else:
        (x_ref, ys_ref, ya_ref, mod_ref, wglu_ref, bglu_ref, wout_ref, gffn_ref,
         x1_ref, h2_ref) = refs
    z = jax.nn.gelu(ys_ref[...], approximate=True)
    gl = z * jax.nn.sigmoid(
        jnp.dot(z.astype(BF16), wglu_ref[...], preferred_element_type=F32) + bglu_ref[...])
    mix = jnp.concatenate([gl.astype(BF16), ya_ref[...]], axis=-1)
    o = jnp.dot(mix, wout_ref[...], preferred_element_type=F32)
    x1 = x_ref[...] + mod_ref[0, 2:3, :] * o
    x1_ref[...] = x1
    h2 = _rms(x1, gffn_ref[...]) * (1.0 + mod_ref[0, 4:5, :]) + mod_ref[0, 3:4, :]
    h2_ref[...] = h2.astype(h2_ref.dtype)
    if with_router:
        logits = jnp.dot(h2, router_ref[...], precision=HI, preferred_element_type=F32)
        gate_ref[...] = _top2_gates(logits)


def _mix_call(x_all, y_ssm, y_att, mod, wglu, bglu, wout, gffn, router, *, n_tiles):
    row = lambda i: (i, 0)
    out_row = row
    with_router = router is not None
    in_specs = [
        pl.BlockSpec((TM, D_MODEL), row),
        pl.BlockSpec((TM, D_SSM), row),
        pl.BlockSpec((TM, D_ATT), row),
        pl.BlockSpec((1, 6, D_MODEL), lambda i: (_mod_row(i), 0, 0)),
        _const_spec((D_SSM, D_SSM)),
        _const_spec((1, D_SSM)),
        _const_spec((D_MIX, D_MODEL)),
        _const_spec((1, D_MODEL)),
    ]
    args = [x_all, y_ssm, y_att, mod, wglu, bglu, wout, gffn]
    out_specs = [pl.BlockSpec((TM, D_MODEL), out_row), pl.BlockSpec((TM, D_MODEL), out_row)]
    out_shape = [jax.ShapeDtypeStruct((n_tiles * TM, D_MODEL), F32),
                 jax.ShapeDtypeStruct((n_tiles * TM, D_MODEL), BF16)]
    if with_router:
        in_specs.append(_const_spec((D_MODEL, LANES)))
        args.append(router)
        out_specs.append(pl.BlockSpec((TM, LANES), out_row))
        out_shape.append(jax.ShapeDtypeStruct((n_tiles * TM, LANES), F32))
    return pl.pallas_call(
        functools.partial(_mix_kernel, with_router=with_router),
        grid=(n_tiles,),
        in_specs=in_specs,
        out_specs=out_specs,
        out_shape=out_shape,
        compiler_params=_cparams("parallel"),
        name="mix_out",
    )(*args)


def _ffn_kernel(h_ref, x1_ref, mod_ref, w1_ref, w3_ref, w2_ref, o_ref):
    h = h_ref[...]
    acc = jnp.zeros((TM, D_MODEL), F32)
    for j in range(N_FF_CHUNKS):
        a = jnp.dot(h, w1_ref[j], preferred_element_type=F32)
        b = jnp.dot(h, w3_ref[j], preferred_element_type=F32)
        g = (a * jax.nn.sigmoid(a) * b).astype(BF16)
        acc = acc + jnp.dot(g, w2_ref[j], preferred_element_type=F32)
    o_ref[...] = x1_ref[...] + mod_ref[0, 5:6, :] * acc


def _ffn_call(h2, x1, mod, w1, w3, w2):
    n_tiles = N_TOT // TM
    row = lambda i: (i, 0)
    return pl.pallas_call(
        _ffn_kernel,
        grid=(n_tiles,),
        in_specs=[
            pl.BlockSpec((TM, D_MODEL), row),
            pl.BlockSpec((TM, D_MODEL), row),
            pl.BlockSpec((1, 6, D_MODEL), lambda i: (_mod_row(i), 0, 0)),
            _const_spec((N_FF_CHUNKS, D_MODEL, FF_CHUNK)),
            _const_spec((N_FF_CHUNKS, D_MODEL, FF_CHUNK)),
            _const_spec((N_FF_CHUNKS, FF_CHUNK, D_MODEL)),
        ],
        out_specs=pl.BlockSpec((TM, D_MODEL), row),
        out_shape=jax.ShapeDtypeStruct((N_TOT, D_MODEL), F32),
        compiler_params=_cparams("parallel"),
        name="ffn_dense",
    )(h2, x1, mod, w1, w3, w2)


def _moe_kernel(h_ref, x1_ref, mod_ref, gate_ref, w1_ref, w3_ref, w2_ref, fg_ref, o_ref, acc_scr):
    e = pl.program_id(1)

    @pl.when(e == 0)
    def _():
        acc_scr[...] = jnp.zeros_like(acc_scr)

    h = h_ref[...]
    a = jnp.dot(h, w1_ref[0], preferred_element_type=F32)
    b = jnp.dot(h, w3_ref[0], preferred_element_type=F32)
    g = (a * jax.nn.sigmoid(a) * b).astype(BF16)
    y = jnp.dot(g, w2_ref[0], preferred_element_type=F32)
    lane = lax.broadcasted_iota(jnp.int32, (TM, LANES), 1)
    gate_e = jnp.sum(jnp.where(lane == e, gate_ref[...], 0.0), axis=-1, keepdims=True)
    acc_scr[...] += gate_e * y

    @pl.when(e == N_EXPERTS - 1)
    def _():
        x2 = x1_ref[...] + mod_ref[0, 5:6, :] * acc_scr[...]
        o_ref[...] = _rms(x2, fg_ref[...])


def _moe_call(h2, x1, mod, gate, w1, w3, w2, fg):
    n_tiles = N_LAT // TM
    row = lambda i, e: (i, 0)
    return pl.pallas_call(
        _moe_kernel,
        grid=(n_tiles, N_EXPERTS),
        in_specs=[
            pl.BlockSpec((TM, D_MODEL), row),
            pl.BlockSpec((TM, D_MODEL), row),
            pl.BlockSpec((1, 6, D_MODEL), lambda i, e: (i // SEQ_TILES, 0, 0)),
            pl.BlockSpec((TM, LANES), row),
            pl.BlockSpec((1, D_MODEL, D_FF_EXPERT), lambda i, e: (e, 0, 0)),
            pl.BlockSpec((1, D_MODEL, D_FF_EXPERT), lambda i, e: (e, 0, 0)),
            pl.BlockSpec((1, D_FF_EXPERT, D_MODEL), lambda i, e: (e, 0, 0)),
            pl.BlockSpec((1, D_MODEL), lambda i, e: (0, 0)),
        ],
        out_specs=pl.BlockSpec((TM, D_MODEL), row),
        out_shape=jax.ShapeDtypeStruct((N_LAT, D_MODEL), F32),
        scratch_shapes=[pltpu.VMEM((TM, D_MODEL), F32)],
        compiler_params=_cparams("parallel", "arbitrary"),
        name="moe_experts",
    )(h2, x1, mod, gate, w1, w3, w2, fg)


def _rope_partner_perm():
    perm, sign = [], []
    for j in range(QK_ROPE):
        first_half = (j % AXIS_ROPE) < ROPE_FREQS
        perm.append(j + ROPE_FREQS if first_half else j - ROPE_FREQS)
        sign.append(-1.0 if first_half else 1.0)
    return jnp.array(perm, jnp.int32), jnp.array(sign, F32)


def _rope_tables():
    t = jnp.arange(SEQ)
    row = (t // GRID_W).astype(F32)
    col = (t % GRID_W).astype(F32)
    inv_freq = ROPE_BASE ** (-2.0 * jnp.arange(ROPE_FREQS, dtype=F32) / AXIS_ROPE)
    ang = jnp.concatenate([row[:, None] * inv_freq, row[:, None] * inv_freq,
                           col[:, None] * inv_freq, col[:, None] * inv_freq], axis=1)
    cos = jnp.concatenate([jnp.cos(ang), jnp.ones((N_CTX, QK_ROPE), F32)], axis=0)
    sin = jnp.concatenate([jnp.sin(ang), jnp.zeros((N_CTX, QK_ROPE), F32)], axis=0)
    n = N_CTX + SEQ
    pad32 = jnp.zeros((n, HEAD_PAD - D_QK), F32)
    cq = jnp.concatenate([jnp.full((n, QK_NOPE), ATT_SCALE, F32), ATT_SCALE * cos, pad32], axis=1)
    sq = jnp.concatenate([jnp.zeros((n, QK_NOPE), F32), ATT_SCALE * sin, pad32], axis=1)
    cs = jnp.concatenate([cos, sin, jnp.zeros((n, LANES - 2 * QK_ROPE), F32)], axis=1)
    return cq, sq, cs


def _layer_weights(w_in, w_uq, w_ukv):
    perm, sign = _rope_partner_perm()
    s0 = D_SSM + Q_LORA + KV_LORA
    kr_w = w_in[:, s0:s0 + QK_ROPE]
    win = jnp.concatenate([w_in[:, :s0], kr_w, kr_w[:, perm] * sign,
                           jnp.zeros((D_MODEL, LANES - 2 * QK_ROPE), F32)], axis=1).astype(BF16)
    uq = w_uq.reshape(Q_LORA, N_HEADS, D_QK)
    nope, rope = uq[..., :QK_NOPE], uq[..., QK_NOPE:]
    zpad = jnp.zeros((Q_LORA, N_HEADS, HEAD_PAD - D_QK), F32)
    wq1 = jnp.concatenate([nope, rope, zpad], axis=-1).reshape(Q_LORA, N_HEADS * HEAD_PAD).astype(BF16)
    wq2 = jnp.concatenate([jnp.zeros_like(nope), rope[..., perm] * sign, zpad], axis=-1)
    wq2 = wq2.reshape(Q_LORA, N_HEADS * HEAD_PAD).astype(BF16)
    ukv = w_ukv.reshape(KV_LORA, N_HEADS, QK_NOPE + V_HEAD)
    wk = jnp.concatenate([ukv[..., :QK_NOPE], jnp.zeros((KV_LORA, N_HEADS, HEAD_PAD - QK_NOPE), F32)],
                         axis=-1).reshape(KV_LORA, N_HEADS * HEAD_PAD)
    eye = jnp.eye(QK_ROPE, dtype=F32)
    place = jnp.concatenate([jnp.zeros((QK_ROPE, QK_NOPE), F32), eye,
                             jnp.zeros((QK_ROPE, HEAD_PAD - D_QK), F32)], axis=1)
    place = jnp.tile(place, (1, N_HEADS))
    spread = jnp.concatenate([place, place, jnp.zeros((LANES - 2 * QK_ROPE, N_HEADS * HEAD_PAD), F32)], 0)
    wkk = jnp.concatenate([wk, spread], axis=0).astype(BF16)
    wv = ukv[..., QK_NOPE:].reshape(KV_LORA, D_ATT).astype(BF16)
    return win, wq1, wq2, wkk, wv


def kernel(x, c, ctx, c_ctx, w_ada, b_ada, norm_mix, norm_ffn, w_in, q_norm, kv_norm, w_uq, w_ukv,
           ssm_a_re, ssm_a_im, ssm_log_dt, ssm_b_re, ssm_b_im, ssm_c_re, ssm_c_im, ssm_d, w_glu,
           b_glu, w_out, ffn_w1, ffn_w3, ffn_w2, moe_router, moe_w1, moe_w3, moe_w2, final_norm):
    assert x.shape == (BATCH, SEQ, D_MODEL) and ctx.shape == (BATCH, CTX_LEN, D_MODEL)
    cond = jnp.concatenate([c, c_ctx[None, :], jnp.zeros((MOD_ROWS - BATCH - 1, D_MODEL), F32)], axis=0)
    mod_all = _ada_call(cond, w_ada, b_ada).reshape(DEPTH, MOD_ROWS, 6, D_MODEL)
    cq_t, sq_t, cs_t = _rope_tables()
    x_all = jnp.concatenate([x.reshape(N_LAT, D_MODEL), ctx.reshape(N_CTX, D_MODEL)], axis=0)

    out = None
    for i in range(DEPTH):
        last = i == DEPTH - 1
        mod = mod_all[i]
        win, wq1, wq2, wkk, wv = _layer_weights(w_in[i], w_uq[i], w_ukv[i])
        u, q, k, v = _inproj_call(x_all, mod, norm_mix[i][None, :], win, q_norm[i][None, :],
                                  kv_norm[i][None, :], wq1, wq2, wkk, wv, cq_t, sq_t, cs_t)
        tabs = _ssm_tables(ssm_a_re[i], ssm_a_im[i], ssm_log_dt[i], ssm_b_re[i], ssm_b_im[i],
                           ssm_c_re[i], ssm_c_im[i], ssm_d[i])
        y_ssm = _from_group_major(_ssm_call(_to_group_major(u), *tabs))
        y_att = _attn_latent_call(q, k, v)
        if last:
            n_tiles = LAT_TILES
        else:
            y_att = jnp.concatenate([y_att, _attn_ctx_call(q, k, v)], axis=0)
            n_tiles = N_TOT // TM
        j = i // 2
        if i % 2 == 0:
            router = None
        else:
            router = jnp.concatenate([moe_router[j], jnp.zeros((D_MODEL, LANES - N_EXPERTS), F32)], axis=1)
        res = _mix_call(x_all, y_ssm, y_att, mod, w_glu[i].astype(BF16), b_glu[i][None, :],
                        w_out[i].astype(BF16), norm_ffn[i][None, :], router, n_tiles=n_tiles)
        if i % 2 == 0:
            assert not last
            x1, h2 = res
            w1 = ffn_w1[j].reshape(D_MODEL, N_FF_CHUNKS, FF_CHUNK).transpose(1, 0, 2).astype(BF16)
            w3 = ffn_w3[j].reshape(D_MODEL, N_FF_CHUNKS, FF_CHUNK).transpose(1, 0, 2).astype(BF16)
            w2 = ffn_w2[j].reshape(N_FF_CHUNKS, FF_CHUNK, D_MODEL).astype(BF16)
            x_all = _ffn_call(h2, x1, mod, w1, w3, w2)
        else:
            assert last
            x1, h2, gate = res
            out = _moe_call(h2, x1, mod, gate, moe_w1[j].astype(BF16), moe_w3[j].astype(BF16),
                            moe_w2[j].astype(BF16), final_norm[None, :])
    return out.reshape(BATCH, SEQ, D_MODEL)
```

```python
import functools
import math

import jax
import jax.numpy as jnp
from jax import lax
from jax.experimental import pallas as pl
from jax.experimental.pallas import tpu as pltpu

D_MODEL = 1024
BATCH = 4
SEQ = 8192
DEPTH = 2
GRID_W = 64
CTX_LEN = 256
D_SSM = 512
SSM_GROUP = 16
N_SSM_GROUPS = D_SSM // SSM_GROUP
SSM_STATE = 64
N_HEADS = 8
QK_NOPE = 64
QK_ROPE = 32
V_HEAD = 64
Q_LORA = 256
KV_LORA = 128
D_QK = QK_NOPE + QK_ROPE
D_ATT = N_HEADS * V_HEAD
D_MIX = D_SSM + D_ATT
D_IN = D_SSM + Q_LORA + KV_LORA + QK_ROPE
AXIS_ROPE = QK_ROPE // 2
ROPE_FREQS = AXIS_ROPE // 2
ROPE_BASE = 10000.0
ATT_SCALE = 1.0 / math.sqrt(D_QK)
D_FF = 2816
N_EXPERTS = 8
D_FF_EXPERT = 1408
EPS = 1e-6

N_CTX = BATCH * CTX_LEN
N_LAT = BATCH * SEQ
N_TOT = N_CTX + N_LAT

LANES = 128
HEAD_PAD = 128
TM = 512
LAT_TILES = N_LAT // TM
SEQ_TILES = SEQ // TM
TQ = 512
TK = 512
SSM_T = 128
N_CHUNK_LAT = N_LAT // SSM_T
N_CHUNK = N_TOT // SSM_T
SSM_K = SSM_GROUP * SSM_T
FF_CHUNK = 256
N_FF_CHUNKS = D_FF // FF_CHUNK
MOD_ROWS = 8
VMEM_LIMIT = 52 * 1024 * 1024

F32 = jnp.float32
BF16 = jnp.bfloat16
HI = lax.Precision.HIGHEST


def _cparams(*sem):
    return pltpu.CompilerParams(dimension_semantics=sem, vmem_limit_bytes=VMEM_LIMIT)


def _const_spec(shape):
    nd = len(shape)
    return pl.BlockSpec(shape, lambda *_: (0,) * nd, pipeline_mode=pl.Buffered(1))


def _mod_row(i):
    return jnp.where(i < LAT_TILES, i // SEQ_TILES, BATCH)


def _pos_tile(i):
    return jnp.where(i < LAT_TILES, i % SEQ_TILES, SEQ_TILES + i - LAT_TILES)


def _rms(x, g):
    ms = jnp.mean(x * x, axis=-1, keepdims=True)
    return x * lax.rsqrt(ms + EPS) * g


ADA_TN = 1536


def _ada_kernel(c_ref, w_ref, b_ref, o_ref):
    c = c_ref[...]
    s = c * jax.nn.sigmoid(c)
    o_ref[0] = jnp.dot(s, w_ref[0], precision=HI, preferred_element_type=F32) + b_ref[0]


def _ada_call(cond, w_ada, b_ada):
    n_col = 6 * D_MODEL // ADA_TN
    return pl.pallas_call(
        _ada_kernel,
        grid=(DEPTH, n_col),
        in_specs=[
            pl.BlockSpec((MOD_ROWS, D_MODEL), lambda l, j: (0, 0)),
            pl.BlockSpec((1, D_MODEL, ADA_TN), lambda l, j: (l, 0, j)),
            pl.BlockSpec((1, 1, ADA_TN), lambda l, j: (l, 0, j)),
        ],
        out_specs=pl.BlockSpec((1, MOD_ROWS, ADA_TN), lambda l, j: (l, 0, j)),
        out_shape=jax.ShapeDtypeStruct((DEPTH, MOD_ROWS, 6 * D_MODEL), F32),
        compiler_params=_cparams("arbitrary", "arbitrary"),
        name="ada_mod",
    )(cond, w_ada, b_ada.reshape(DEPTH, 1, 6 * D_MODEL))


def _inproj_kernel(x_ref, mod_ref, g_ref, win_ref, qg_ref, kvg_ref, wq1_ref, wq2_ref,
                   wkk_ref, wv_ref, cq_ref, sq_ref, cs_ref, u_ref, q_ref, k_ref, v_ref):
    x = x_ref[...]
    sh = mod_ref[0, 0:1, :]
    sc = mod_ref[0, 1:2, :]
    xm = (_rms(x, g_ref[...]) * (1.0 + sc) + sh).astype(BF16)
    z = jnp.dot(xm, win_ref[...], preferred_element_type=F32)
    u_ref[...] = z[:, :D_SSM].astype(u_ref.dtype)
    qn = _rms(z[:, D_SSM:D_SSM + Q_LORA], qg_ref[...]).astype(BF16)
    kvn = _rms(z[:, D_SSM + Q_LORA:D_SSM + Q_LORA + KV_LORA], kvg_ref[...]).astype(BF16)
    krr = (z[:, D_SSM + Q_LORA + KV_LORA:] * cs_ref[...]).astype(BF16)
    q1 = jnp.dot(qn, wq1_ref[...], preferred_element_type=F32)
    q2 = jnp.dot(qn, wq2_ref[...], preferred_element_type=F32)
    cq = cq_ref[...]
    sq = sq_ref[...]
    for h in range(N_HEADS):
        sl = slice(h * HEAD_PAD, (h + 1) * HEAD_PAD)
        q_ref[:, sl] = (q1[:, sl] * cq + q2[:, sl] * sq).astype(q_ref.dtype)
    kin = jnp.concatenate([kvn, krr], axis=-1)
    k_ref[...] = jnp.dot(kin, wkk_ref[...], preferred_element_type=F32).astype(k_ref.dtype)
    vv = jnp.dot(kvn, wv_ref[...], preferred_element_type=F32)
    lane = lax.broadcasted_iota(jnp.int32, vv.shape, 1)
    v_ref[...] = jnp.where(lane % HEAD_PAD == V_HEAD, 1.0, vv).astype(v_ref.dtype)


def _inproj_call(x_all, mod, g_mix, win, qg, kvg, wq1, wq2, wkk, wv, cq_t, sq_t, cs_t):
    n_tiles = N_TOT // TM
    row = lambda i: (i, 0)
    pos = lambda i: (_pos_tile(i), 0)
    return pl.pallas_call(
        _inproj_kernel,
        grid=(n_tiles,),
        in_specs=[
            pl.BlockSpec((TM, D_MODEL), row),
            pl.BlockSpec((1, 6, D_MODEL), lambda i: (_mod_row(i), 0, 0)),
            _const_spec((1, D_MODEL)),
            _const_spec((D_MODEL, D_MODEL)),
            _const_spec((1, Q_LORA)),
            _const_spec((1, KV_LORA)),
            _const_spec((Q_LORA, N_HEADS * HEAD_PAD)),
            _const_spec((Q_LORA, N_HEADS * HEAD_PAD)),
            _const_spec((2 * KV_LORA, N_HEADS * HEAD_PAD)),
            _const_spec((KV_LORA, N_HEADS * HEAD_PAD)),
            pl.BlockSpec((TM, LANES), pos),
            pl.BlockSpec((TM, LANES), pos),
            pl.BlockSpec((TM, LANES), pos),
        ],
        out_specs=[
            pl.BlockSpec((TM, D_SSM), row),
            pl.BlockSpec((TM, N_HEADS * HEAD_PAD), row),
            pl.BlockSpec((TM, N_HEADS * HEAD_PAD), row),
            pl.BlockSpec((TM, N_HEADS * HEAD_PAD), row),
        ],
        out_shape=[
            jax.ShapeDtypeStruct((N_TOT, D_SSM), BF16),
            jax.ShapeDtypeStruct((N_TOT, N_HEADS * HEAD_PAD), BF16),
            jax.ShapeDtypeStruct((N_TOT, N_HEADS * HEAD_PAD), BF16),
            jax.ShapeDtypeStruct((N_TOT, N_HEADS * HEAD_PAD), BF16),
        ],
        compiler_params=_cparams("parallel"),
        name="in_proj",
    )(x_all, mod, g_mix, win, qg, kvg, wq1, wq2, wkk, wv, cq_t, sq_t, cs_t)


def _attn_kernel(*refs, n_chunks, tq):
    if n_chunks:
        q_ref, k_ref, v_ref, kc_ref, vc_ref, o_ref, s_scr = refs
    else:
        q_ref, kc_ref, vc_ref, o_ref, s_scr = refs
    heads = [slice(hh * HEAD_PAD, (hh + 1) * HEAD_PAD) for hh in range(2)]
    qs = [q_ref[:, sl] for sl in heads]

    def put_scores(slot, k_at, width):
        for hh in range(2):
            s_scr[slot, hh, :, :width] = lax.dot_general(
                qs[hh], k_at(heads[hh]), (((1,), (1,)), ((), ())), preferred_element_type=F32)

    def consume(carry, slot, v_at, width):
        new = []
        for hh in range(2):
            m, acc = carry[hh]
            s = s_scr[slot, hh, :, :width]
            m_new = jnp.maximum(m, jnp.max(s, axis=-1, keepdims=True))
            alpha = jnp.exp2(m - m_new)
            p = jnp.exp2(s - m_new).astype(BF16)
            acc = alpha * acc + jnp.dot(p, v_at(heads[hh]), preferred_element_type=F32)
            new.append((m_new, acc))
        return tuple(new)

    def chunk(c):
        rows = pl.ds(pl.multiple_of(c * TK, TK), TK)
        return (lambda sl: k_ref[rows, sl]), (lambda sl: v_ref[rows, sl])

    ctx_k, ctx_v = (lambda sl: kc_ref[:, sl]), (lambda sl: vc_ref[:, sl])
    carry = tuple((jnp.full((tq, 1), -jnp.inf, F32), jnp.zeros((tq, HEAD_PAD), F32))
                  for _ in range(2))
    if n_chunks:
        assert n_chunks % 2 == 0
        put_scores(0, chunk(0)[0], TK)

        def body(j, carry):
            c0 = 2 * j
            put_scores(1, chunk(c0 + 1)[0], TK)
            carry = consume(carry, 0, chunk(c0)[1], TK)
            put_scores(0, chunk(c0 + 2)[0], TK)
            return consume(carry, 1, chunk(c0 + 1)[1], TK)

        carry = lax.fori_loop(0, n_chunks // 2 - 1, body, carry)
        put_scores(1, chunk(n_chunks - 1)[0], TK)
        carry = consume(carry, 0, chunk(n_chunks - 2)[1], TK)
        put_scores(0, ctx_k, CTX_LEN)
        carry = consume(carry, 1, chunk(n_chunks - 1)[1], TK)
    else:
        put_scores(0, ctx_k, CTX_LEN)
    carry = consume(carry, 0, ctx_v, CTX_LEN)
    outs = [acc[:, :V_HEAD] / acc[:, V_HEAD:V_HEAD + 1] for _, acc in carry]
    o_ref[...] = jnp.concatenate(outs, axis=-1).astype(o_ref.dtype)


def _attn_latent_call(q, k, v):
    qt = SEQ // TQ
    ctx0 = N_LAT // CTX_LEN
    return pl.pallas_call(
        functools.partial(_attn_kernel, n_chunks=SEQ // TK, tq=TQ),
        grid=(BATCH, N_HEADS // 2, qt),
        in_specs=[
            pl.BlockSpec((TQ, 2 * HEAD_PAD), lambda b, h, i: (b * qt + i, h)),
            pl.BlockSpec((SEQ, 2 * HEAD_PAD), lambda b, h, i: (b, h)),
            pl.BlockSpec((SEQ, 2 * HEAD_PAD), lambda b, h, i: (b, h)),
            pl.BlockSpec((CTX_LEN, 2 * HEAD_PAD), lambda b, h, i: (ctx0 + b, h)),
            pl.BlockSpec((CTX_LEN, 2 * HEAD_PAD), lambda b, h, i: (ctx0 + b, h)),
        ],
        out_specs=pl.BlockSpec((TQ, 2 * V_HEAD), lambda b, h, i: (b * qt + i, h)),
        out_shape=jax.ShapeDtypeStruct((N_LAT, D_ATT), BF16),
        scratch_shapes=[pltpu.VMEM((2, 2, TQ, TK), F32)],
        compiler_params=_cparams("parallel", "parallel", "arbitrary"),
        name="attn_latent",
    )(q, k, v, k, v)


def _attn_ctx_call(q, k, v):
    ctx0 = N_LAT // CTX_LEN
    return pl.pallas_call(
        functools.partial(_attn_kernel, n_chunks=0, tq=CTX_LEN),
        grid=(BATCH, N_HEADS // 2),
        in_specs=[
            pl.BlockSpec((CTX_LEN, 2 * HEAD_PAD), lambda b, h: (ctx0 + b, h)),
            pl.BlockSpec((CTX_LEN, 2 * HEAD_PAD), lambda b, h: (ctx0 + b, h)),
            pl.BlockSpec((CTX_LEN, 2 * HEAD_PAD), lambda b, h: (ctx0 + b, h)),
        ],
        out_specs=pl.BlockSpec((CTX_LEN, 2 * V_HEAD), lambda b, h: (b, h)),
        out_shape=jax.ShapeDtypeStruct((N_CTX, D_ATT), BF16),
        scratch_shapes=[pltpu.VMEM((1, 2, CTX_LEN, CTX_LEN), F32)],
        compiler_params=_cparams("parallel", "parallel"),
        name="attn_ctx",
    )(q, k, v)


def _ssm_kernel(u_ref, cc_ref, ab_ref, ws_ref, wc_ref, at_ref, dv_ref, y_ref,
                wl_scr, m_scr, s_scr, h_scr):
    wl_scr[...] = jnp.dot(cc_ref[0], ab_ref[0], precision=HI, preferred_element_type=F32)

    def build(ci, _):
        lane0 = pl.multiple_of(ci * 2 * SSM_T, 2 * SSM_T)
        row0 = pl.multiple_of(ci * SSM_T, SSM_T)
        for co in range(SSM_GROUP):
            w = wl_scr[co:co + 1, pl.ds(lane0, 2 * SSM_T)]
            wb = jnp.broadcast_to(w, (SSM_T, 2 * SSM_T))
            toe = pltpu.roll(wb, 0, 1, stride=1, stride_axis=0)
            m_scr[pl.ds(row0, SSM_T), co * SSM_T:(co + 1) * SSM_T] = toe[:, :SSM_T].astype(BF16)
        return 0

    lax.fori_loop(0, SSM_GROUP, build, 0)

    u = u_ref[0]
    y = jnp.dot(u, m_scr[...], preferred_element_type=F32)
    s_scr[...] = jnp.dot(u, ws_ref[0], preferred_element_type=F32)

    def advance(h, a_row, s_rows):
        return h * at_ref[0, a_row:a_row + 1, :] + \
            pltpu.roll(h, SSM_STATE, 1) * at_ref[0, a_row + 1:a_row + 2, :] + s_rows

    n_lat_chunks = SEQ // SSM_T
    ctx_chunks = CTX_LEN // SSM_T
    fwd_rows = [N_CHUNK_LAT + BATCH * c for c in range(ctx_chunks)] + \
               [BATCH * k for k in range(n_lat_chunks)]
    rev_rows = [N_CHUNK_LAT + BATCH * c for c in reversed(range(ctx_chunks))] + \
               [BATCH * k for k in reversed(range(n_lat_chunks))]
    hf = jnp.zeros((BATCH, 2 * SSM_STATE), F32)
    hr = jnp.zeros((BATCH, 2 * SSM_STATE), F32)
    for rf, rr in zip(fwd_rows, rev_rows):
        h_scr[rf:rf + BATCH, 0:LANES] = hf
        hf = advance(hf, 0, s_scr[rf:rf + BATCH, 0:LANES])
        h_scr[rr:rr + BATCH, LANES:2 * LANES] = hr
        hr = advance(hr, 2, s_scr[rr:rr + BATCH, LANES:2 * LANES])

    y = y + jnp.dot(h_scr[...].astype(BF16), wc_ref[0], preferred_element_type=F32)
    y_ref[0] = y + u.astype(F32) * dv_ref[0]


def _ssm_call(u_t, cc, ab, ws, wc, at, dv):
    g3 = lambda g: (g, 0, 0)
    return pl.pallas_call(
        _ssm_kernel,
        grid=(N_SSM_GROUPS,),
        in_specs=[
            pl.BlockSpec((1, N_CHUNK, SSM_K), g3),
            pl.BlockSpec((1, SSM_GROUP, 4 * SSM_STATE), g3),
            pl.BlockSpec((1, 4 * SSM_STATE, SSM_GROUP * 2 * SSM_T), g3),
            pl.BlockSpec((1, SSM_K, 4 * SSM_STATE), g3),
            pl.BlockSpec((1, 4 * SSM_STATE, SSM_K), g3),
            pl.BlockSpec((1, 4, 2 * SSM_STATE), g3),
            pl.BlockSpec((1, 1, SSM_K), g3),
        ],
        out_specs=pl.BlockSpec((1, N_CHUNK, SSM_K), g3),
        out_shape=jax.ShapeDtypeStruct((N_SSM_GROUPS, N_CHUNK, SSM_K), F32),
        scratch_shapes=[
            pltpu.VMEM((SSM_GROUP, SSM_GROUP * 2 * SSM_T), F32),
            pltpu.VMEM((SSM_K, SSM_K), BF16),
            pltpu.VMEM((N_CHUNK, 4 * SSM_STATE), F32),
            pltpu.VMEM((N_CHUNK, 4 * SSM_STATE), F32),
        ],
        compiler_params=_cparams("parallel"),
        name="s5_mixer",
    )(u_t, cc, ab, ws, wc, at, dv)


def _ssm_tables(a_re, a_im, log_dt, b_re, b_im, c_re, c_im, d_skip):
    G, P, H, T = N_SSM_GROUPS, SSM_STATE, SSM_GROUP, SSM_T
    a_re, a_im = a_re.astype(F32), a_im.astype(F32)
    dt = jnp.exp(log_dt.astype(F32))[..., None]
    den = a_re * a_re + a_im * a_im
    mag1 = jnp.exp(dt * a_re)
    ab_re, ab_im = mag1 * jnp.cos(dt * a_im), mag1 * jnp.sin(dt * a_im)
    num_re = ab_re - 1.0
    f_re = (num_re * a_re + ab_im * a_im) / den
    f_im = (ab_im * a_re - num_re * a_im) / den
    b_re, b_im = b_re.astype(F32), b_im.astype(F32)
    bb_re = f_re[..., None] * b_re - f_im[..., None] * b_im
    bb_im = f_re[..., None] * b_im + f_im[..., None] * b_re
    c_re, c_im = c_re.astype(F32), c_im.astype(F32)
    la, th = dt * a_re, dt * a_im

    def cmul(xr, xi, yr, yi):
        return xr * yr - xi * yi, xr * yi + xi * yr

    def powers(d, n):
        mag = jnp.exp(la[d][..., None] * n)
        ph = th[d][..., None] * n
        return mag * jnp.cos(ph), mag * jnp.sin(ph)

    def powers_b(d, n):
        pr, pi = powers(d, n)
        return cmul(pr[:, :, None, :], pi[:, :, None, :], bb_re[d][..., None], bb_im[d][..., None])

    lane = jnp.arange(2 * T)
    f_exp = jnp.where(lane < T, lane, 0).astype(F32)
    r_exp = jnp.where(lane > T, 2 * T - lane, 0).astype(F32)
    f_on = (lane < T).astype(F32)
    r_on = ((lane == 0) | (lane > T)).astype(F32)
    f_lag_re, f_lag_im = powers_b(0, f_exp)
    r_lag_re, r_lag_im = powers_b(1, r_exp)
    ab = jnp.concatenate([f_lag_re * f_on, f_lag_im * f_on, r_lag_re * r_on, r_lag_im * r_on], axis=1)
    ab = ab.reshape(G, 4 * P, H * 2 * T)
    cc = jnp.concatenate([c_re[0], -c_im[0], c_re[1], -c_im[1]], axis=-1)

    s_idx = jnp.arange(T, dtype=F32)
    wsf_re, wsf_im = powers_b(0, T - 1.0 - s_idx)
    wsr_re, wsr_im = powers_b(1, s_idx)
    ws = jnp.concatenate([wsf_re, wsf_im, wsr_re, wsr_im], axis=1)
    ws = ws.reshape(G, 4 * P, H * T).transpose(0, 2, 1).astype(BF16)

    def readout(d, n):
        pr, pi = powers(d, n)
        cr = c_re[d].transpose(0, 2, 1)[..., None]
        ci = c_im[d].transpose(0, 2, 1)[..., None]
        rr, ri = cmul(cr, ci, pr[:, :, None, :], pi[:, :, None, :])
        return rr, -ri
    wcf_re, wcf_im = readout(0, s_idx + 1.0)
    wcr_re, wcr_im = readout(1, T - s_idx)
    wc = jnp.concatenate([wcf_re, wcf_im, wcr_re, wcr_im], axis=1)
    wc = wc.reshape(G, 4 * P, H * T).astype(BF16)

    t_exp = jnp.full((1,), float(T), F32)
    (f_re_t, f_im_t), (r_re_t, r_im_t) = powers(0, t_exp), powers(1, t_exp)
    f_re_t, f_im_t, r_re_t, r_im_t = (v[..., 0] for v in (f_re_t, f_im_t, r_re_t, r_im_t))
    at = jnp.stack([jnp.concatenate([f_re_t, f_re_t], -1), jnp.concatenate([-f_im_t, f_im_t], -1),
                    jnp.concatenate([r_re_t, r_re_t], -1), jnp.concatenate([-r_im_t, r_im_t], -1)],
                   axis=1)
    dv = jnp.repeat(d_skip.astype(F32).reshape(G, H), T, axis=-1).reshape(G, 1, H * T)
    return cc, ab, ws, wc, at, dv


def _to_group_major(u):
    G, H, T = N_SSM_GROUPS, SSM_GROUP, SSM_T
    ul = u[:N_LAT].reshape(BATCH, SEQ // T, T, G, H).transpose(3, 1, 0, 4, 2)
    uc = u[N_LAT:].reshape(BATCH, CTX_LEN // T, T, G, H).transpose(3, 1, 0, 4, 2)
    return jnp.concatenate([ul.reshape(G, N_CHUNK_LAT, H * T),
                            uc.reshape(G, N_CHUNK - N_CHUNK_LAT, H * T)], axis=1)


def _from_group_major(y):
    G, H, T = N_SSM_GROUPS, SSM_GROUP, SSM_T
    yl = y[:, :N_CHUNK_LAT].reshape(G, SEQ // T, BATCH, H, T).transpose(2, 1, 4, 0, 3)
    yc = y[:, N_CHUNK_LAT:].reshape(G, CTX_LEN // T, BATCH, H, T).transpose(2, 1, 4, 0, 3)
    return jnp.concatenate([yl.reshape(N_LAT, D_SSM), yc.reshape(N_CTX, D_SSM)], axis=0)


def _top2_gates(logits):
    lane = lax.broadcasted_iota(jnp.int32, logits.shape, 1)
    valid = lane < N_EXPERTS
    lg = jnp.where(valid, logits, -jnp.inf)
    m1 = jnp.max(lg, axis=-1, keepdims=True)
    i1 = jnp.min(jnp.where(lg == m1, lane, LANES), axis=-1, keepdims=True)
    lg2 = jnp.where(lane == i1, -jnp.inf, lg)
    m2 = jnp.max(lg2, axis=-1, keepdims=True)
    i2 = jnp.min(jnp.where(lg2 == m2, lane, LANES), axis=-1, keepdims=True)
    e2 = jnp.exp(m2 - m1)
    w1 = 1.0 / (1.0 + e2)
    return jnp.where(lane == i1, w1, 0.0) + jnp.where(lane == i2, e2 * w1, 0.0)


def _mix_kernel(*refs, with_router):
    if with_router:
        (x_ref, ys_ref, ya_ref, mod_ref, wglu_ref, bglu_ref, wout_ref, gffn_ref, router_ref,
         x1_ref, h2_ref, gate_ref) = refs
    else:
        (x_ref, ys_ref, ya_ref, mod_ref, wglu_ref, bglu_ref, wout_ref, gffn_ref,
         x1_ref, h2_ref) = refs
    z = jax.nn.gelu(ys_ref[...], approximate=True)
    gl = z * jax.nn.sigmoid(
        jnp.dot(z.astype(BF16), wglu_ref[...], preferred_element_type=F32) + bglu_ref[...])
    mix = jnp.concatenate([gl.astype(BF16), ya_ref[...]], axis=-1)
    o = jnp.dot(mix, wout_ref[...], preferred_element_type=F32)
    x1 = x_ref[...] + mod_ref[0, 2:3, :] * o
    x1_ref[...] = x1
    h2 = _rms(x1, gffn_ref[...]) * (1.0 + mod_ref[0, 4:5, :]) + mod_ref[0, 3:4, :]
    h2_ref[...] = h2.astype(h2_ref.dtype)
    if with_router:
        logits = jnp.dot(h2, router_ref[...], precision=HI, preferred_element_type=F32)
        gate_ref[...] = _top2_gates(logits)


def _mix_call(x_all, y_ssm, y_att, mod, wglu, bglu, wout, gffn, router, *, n_tiles):
    row = lambda i: (i, 0)
    out_row = row
    with_router = router is not None
    in_specs = [
        pl.BlockSpec((TM, D_MODEL), row),
        pl.BlockSpec((TM, D_SSM), row),
        pl.BlockSpec((TM, D_ATT), row),
        pl.BlockSpec((1, 6, D_MODEL), lambda i: (_mod_row(i), 0, 0)),
        _const_spec((D_SSM, D_SSM)),
        _const_spec((1, D_SSM)),
        _const_spec((D_MIX, D_MODEL)),
        _const_spec((1, D_MODEL)),
    ]
    args = [x_all, y_ssm, y_att, mod, wglu, bglu, wout, gffn]
    out_specs = [pl.BlockSpec((TM, D_MODEL), out_row), pl.BlockSpec((TM, D_MODEL), out_row)]
    out_shape = [jax.ShapeDtypeStruct((n_tiles * TM, D_MODEL), F32),
                 jax.ShapeDtypeStruct((n_tiles * TM, D_MODEL), BF16)]
    if with_router:
        in_specs.append(_const_spec((D_MODEL, LANES)))
        args.append(router)
        out_specs.append(pl.BlockSpec((TM, LANES), out_row))
        out_shape.append(jax.ShapeDtypeStruct((n_tiles * TM, LANES), F32))
    return pl.pallas_call(
        functools.partial(_mix_kernel, with_router=with_router),
        grid=(n_tiles,),
        in_specs=in_specs,
        out_specs=out_specs,
        out_shape=out_shape,
        compiler_params=_cparams("parallel"),
        name="mix_out",
    )(*args)


def _ffn_kernel(h_ref, x1_ref, mod_ref, w1_ref, w3_ref, w2_ref, o_ref):
    h = h_ref[...]
    acc = jnp.zeros((TM, D_MODEL), F32)
    for j in range(N_FF_CHUNKS):
        a = jnp.dot(h, w1_ref[j], preferred_element_type=F32)
        b = jnp.dot(h, w3_ref[j], preferred_element_type=F32)
        g = (a * jax.nn.sigmoid(a) * b).astype(BF16)
        acc = acc + jnp.dot(g, w2_ref[j], preferred_element_type=F32)
    o_ref[...] = x1_ref[...] + mod_ref[0, 5:6, :] * acc


def _ffn_call(h2, x1, mod, w1, w3, w2):
    n_tiles = N_TOT // TM
    row = lambda i: (i, 0)
    return pl.pallas_call(
        _ffn_kernel,
        grid=(n_tiles,),
        in_specs=[
            pl.BlockSpec((TM, D_MODEL), row),
            pl.BlockSpec((TM, D_MODEL), row),
            pl.BlockSpec((1, 6, D_MODEL), lambda i: (_mod_row(i), 0, 0)),
            _const_spec((N_FF_CHUNKS, D_MODEL, FF_CHUNK)),
            _const_spec((N_FF_CHUNKS, D_MODEL, FF_CHUNK)),
            _const_spec((N_FF_CHUNKS, FF_CHUNK, D_MODEL)),
        ],
        out_specs=pl.BlockSpec((TM, D_MODEL), row),
        out_shape=jax.ShapeDtypeStruct((N_TOT, D_MODEL), F32),
        compiler_params=_cparams("parallel"),
        name="ffn_dense",
    )(h2, x1, mod, w1, w3, w2)


def _moe_kernel(h_ref, x1_ref, mod_ref, gate_ref, w1_ref, w3_ref, w2_ref, fg_ref, o_ref, acc_scr):
    e = pl.program_id(1)

    @pl.when(e == 0)
    def _():
        acc_scr[...] = jnp.zeros_like(acc_scr)

    h = h_ref[...]
    a = jnp.dot(h, w1_ref[0], preferred_element_type=F32)
    b = jnp.dot(h, w3_ref[0], preferred_element_type=F32)
    g = (a * jax.nn.sigmoid(a) * b).astype(BF16)
    y = jnp.dot(g, w2_ref[0], preferred_element_type=F32)
    lane = lax.broadcasted_iota(jnp.int32, (TM, LANES), 1)
    gate_e = jnp.sum(jnp.where(lane == e, gate_ref[...], 0.0), axis=-1, keepdims=True)
    acc_scr[...] += gate_e * y

    @pl.when(e == N_EXPERTS - 1)
    def _():
        x2 = x1_ref[...] + mod_ref[0, 5:6, :] * acc_scr[...]
        o_ref[...] = _rms(x2, fg_ref[...])


def _moe_call(h2, x1, mod, gate, w1, w3, w2, fg):
    n_tiles = N_LAT // TM
    row = lambda i, e: (i, 0)
    return pl.pallas_call(
        _moe_kernel,
        grid=(n_tiles, N_EXPERTS),
        in_specs=[
            pl.BlockSpec((TM, D_MODEL), row),
            pl.BlockSpec((TM, D_MODEL), row),
            pl.BlockSpec((1, 6, D_MODEL), lambda i, e: (i // SEQ_TILES, 0, 0)),
            pl.BlockSpec((TM, LANES), row),
            pl.BlockSpec((1, D_MODEL, D_FF_EXPERT), lambda i, e: (e, 0, 0)),
            pl.BlockSpec((1, D_MODEL, D_FF_EXPERT), lambda i, e: (e, 0, 0)),
            pl.BlockSpec((1, D_FF_EXPERT, D_MODEL), lambda i, e: (e, 0, 0)),
            pl.BlockSpec((1, D_MODEL), lambda i, e: (0, 0)),
        ],
        out_specs=pl.BlockSpec((TM, D_MODEL), row),
        out_shape=jax.ShapeDtypeStruct((N_LAT, D_MODEL), F32),
        scratch_shapes=[pltpu.VMEM((TM, D_MODEL), F32)],
        compiler_params=_cparams("parallel", "arbitrary"),
        name="moe_experts",
    )(h2, x1, mod, gate, w1, w3, w2, fg)


def _rope_partner_perm():
    perm, sign = [], []
    for j in range(QK_ROPE):
        first_half = (j % AXIS_ROPE) < ROPE_FREQS
        perm.append(j + ROPE_FREQS if first_half else j - ROPE_FREQS)
        sign.append(-1.0 if first_half else 1.0)
    return jnp.array(perm, jnp.int32), jnp.array(sign, F32)


def _rope_tables():
    t = jnp.arange(SEQ)
    row = (t // GRID_W).astype(F32)
    col = (t % GRID_W).astype(F32)
    inv_freq = ROPE_BASE ** (-2.0 * jnp.arange(ROPE_FREQS, dtype=F32) / AXIS_ROPE)
    ang = jnp.concatenate([row[:, None] * inv_freq, row[:, None] * inv_freq,
                           col[:, None] * inv_freq, col[:, None] * inv_freq], axis=1)
    cos = jnp.concatenate([jnp.cos(ang), jnp.ones((N_CTX, QK_ROPE), F32)], axis=0)
    sin = jnp.concatenate([jnp.sin(ang), jnp.zeros((N_CTX, QK_ROPE), F32)], axis=0)
    n = N_CTX + SEQ
    pad32 = jnp.zeros((n, HEAD_PAD - D_QK), F32)
    qs = ATT_SCALE * math.log2(math.e)
    cq = jnp.concatenate([jnp.full((n, QK_NOPE), qs, F32), qs * cos, pad32], axis=1)
    sq = jnp.concatenate([jnp.zeros((n, QK_NOPE), F32), qs * sin, pad32], axis=1)
    cs = jnp.concatenate([cos, sin, jnp.zeros((n, LANES - 2 * QK_ROPE), F32)], axis=1)
    return cq, sq, cs


def _layer_weights(w_in, w_uq, w_ukv):
    perm, sign = _rope_partner_perm()
    s0 = D_SSM + Q_LORA + KV_LORA
    kr_w = w_in[:, s0:s0 + QK_ROPE]
    win = jnp.concatenate([w_in[:, :s0], kr_w, kr_w[:, perm] * sign,
                           jnp.zeros((D_MODEL, LANES - 2 * QK_ROPE), F32)], axis=1).astype(BF16)
    uq = w_uq.reshape(Q_LORA, N_HEADS, D_QK)
    nope, rope = uq[..., :QK_NOPE], uq[..., QK_NOPE:]
    zpad = jnp.zeros((Q_LORA, N_HEADS, HEAD_PAD - D_QK), F32)
    wq1 = jnp.concatenate([nope, rope, zpad], axis=-1).reshape(Q_LORA, N_HEADS * HEAD_PAD).astype(BF16)
    wq2 = jnp.concatenate([jnp.zeros_like(nope), rope[..., perm] * sign, zpad], axis=-1)
    wq2 = wq2.reshape(Q_LORA, N_HEADS * HEAD_PAD).astype(BF16)
    ukv = w_ukv.reshape(KV_LORA, N_HEADS, QK_NOPE + V_HEAD)
    wk = jnp.concatenate([ukv[..., :QK_NOPE], jnp.zeros((KV_LORA, N_HEADS, HEAD_PAD - QK_NOPE), F32)],
                         axis=-1).reshape(KV_LORA, N_HEADS * HEAD_PAD)
    eye = jnp.eye(QK_ROPE, dtype=F32)
    place = jnp.concatenate([jnp.zeros((QK_ROPE, QK_NOPE), F32), eye,
                             jnp.zeros((QK_ROPE, HEAD_PAD - D_QK), F32)], axis=1)
    place = jnp.tile(place, (1, N_HEADS))
    spread = jnp.concatenate([place, place, jnp.zeros((LANES - 2 * QK_ROPE, N_HEADS * HEAD_PAD), F32)], 0)
    wkk = jnp.concatenate([wk, spread], axis=0).astype(BF16)
    wv = jnp.concatenate([ukv[..., QK_NOPE:], jnp.zeros((KV_LORA, N_HEADS, HEAD_PAD - V_HEAD), F32)],
                         axis=-1).reshape(KV_LORA, N_HEADS * HEAD_PAD).astype(BF16)
    return win, wq1, wq2, wkk, wv


def kernel(x, c, ctx, c_ctx, w_ada, b_ada, norm_mix, norm_ffn, w_in, q_norm, kv_norm, w_uq, w_ukv,
           ssm_a_re, ssm_a_im, ssm_log_dt, ssm_b_re, ssm_b_im, ssm_c_re, ssm_c_im, ssm_d, w_glu,
           b_glu, w_out, ffn_w1, ffn_w3, ffn_w2, moe_router, moe_w1, moe_w3, moe_w2, final_norm):
    assert x.shape == (BATCH, SEQ, D_MODEL) and ctx.shape == (BATCH, CTX_LEN, D_MODEL)
    cond = jnp.concatenate([c, c_ctx[None, :], jnp.zeros((MOD_ROWS - BATCH - 1, D_MODEL), F32)], axis=0)
    mod_all = _ada_call(cond, w_ada, b_ada).reshape(DEPTH, MOD_ROWS, 6, D_MODEL)
    cq_t, sq_t, cs_t = _rope_tables()
    x_all = jnp.concatenate([x.reshape(N_LAT, D_MODEL), ctx.reshape(N_CTX, D_MODEL)], axis=0)

    out = None
    for i in range(DEPTH):
        last = i == DEPTH - 1
        mod = mod_all[i]
        win, wq1, wq2, wkk, wv = _layer_weights(w_in[i], w_uq[i], w_ukv[i])
        u, q, k, v = _inproj_call(x_all, mod, norm_mix[i][None, :], win, q_norm[i][None, :],
                                  kv_norm[i][None, :], wq1, wq2, wkk, wv, cq_t, sq_t, cs_t)
        tabs = _ssm_tables(ssm_a_re[i], ssm_a_im[i], ssm_log_dt[i], ssm_b_re[i], ssm_b_im[i],
                           ssm_c_re[i], ssm_c_im[i], ssm_d[i])
        y_ssm = _from_group_major(_ssm_call(_to_group_major(u), *tabs))
        y_att = _attn_latent_call(q, k, v)
        if last:
            n_tiles = LAT_TILES
        else:
            y_att = jnp.concatenate([y_att, _attn_ctx_call(q, k, v)], axis=0)
            n_tiles = N_TOT // TM
        j = i // 2
        if i % 2 == 0:
            router = None
        else:
            router = jnp.concatenate([moe_router[j], jnp.zeros((D_MODEL, LANES - N_EXPERTS), F32)], axis=1)
        res = _mix_call(x_all, y_ssm, y_att, mod, w_glu[i].astype(BF16), b_glu[i][None, :],
                        w_out[i].astype(BF16), norm_ffn[i][None, :], router, n_tiles=n_tiles)
        if i % 2 == 0:
            assert not last
            x1, h2 = res
            w1 = ffn_w1[j].reshape(D_MODEL, N_FF_CHUNKS, FF_CHUNK).transpose(1, 0, 2).astype(BF16)
            w3 = ffn_w3[j].reshape(D_MODEL, N_FF_CHUNKS, FF_CHUNK).transpose(1, 0, 2).astype(BF16)
            w2 = ffn_w2[j].reshape(N_FF_CHUNKS, FF_CHUNK, D_MODEL).astype(BF16)
            x_all = _ffn_call(h2, x1, mod, w1, w3, w2)
        else:
            assert last
            x1, h2, gate = res
            out = _moe_call(h2, x1, mod, gate, moe_w1[j].astype(BF16), moe_w3[j].astype(BF16),
                            moe_w2[j].astype(BF16), final_norm[None, :])
    return out.reshape(BATCH, SEQ, D_MODEL)
```

```python
import functools
import math

import jax
import jax.numpy as jnp
from jax import lax
from jax.experimental import pallas as pl
from jax.experimental.pallas import tpu as pltpu

D_MODEL = 1024
BATCH = 4
SEQ = 8192
DEPTH = 2
GRID_W = 64
CTX_LEN = 256
D_SSM = 512
SSM_GROUP = 16
N_SSM_GROUPS = D_SSM // SSM_GROUP
SSM_STATE = 64
N_HEADS = 8
QK_NOPE = 64
QK_ROPE = 32
V_HEAD = 64
Q_LORA = 256
KV_LORA = 128
D_QK = QK_NOPE + QK_ROPE
D_ATT = N_HEADS * V_HEAD
D_MIX = D_SSM + D_ATT
D_IN = D_SSM + Q_LORA + KV_LORA + QK_ROPE
AXIS_ROPE = QK_ROPE // 2
ROPE_FREQS = AXIS_ROPE // 2
ROPE_BASE = 10000.0
ATT_SCALE = 1.0 / math.sqrt(D_QK)
D_FF = 2816
N_EXPERTS = 8
D_FF_EXPERT = 1408
EPS = 1e-6

N_CTX = BATCH * CTX_LEN
N_LAT = BATCH * SEQ
N_TOT = N_CTX + N_LAT

LANES = 128
HEAD_PAD = 128
TM = 512
LAT_TILES = N_LAT // TM
SEQ_TILES = SEQ // TM
TQ = 512
TK = 512
SSM_T = 128
N_CHUNK_LAT = N_LAT // SSM_T
N_CHUNK = N_TOT // SSM_T
SSM_K = SSM_GROUP * SSM_T
FF_CHUNK = 256
N_FF_CHUNKS = D_FF // FF_CHUNK
D_IN_REST = Q_LORA + KV_LORA + LANES
MOD_ROWS = 8
VMEM_LIMIT = 52 * 1024 * 1024

F32 = jnp.float32
BF16 = jnp.bfloat16
HI = lax.Precision.HIGHEST


def _cparams(*sem):
    return pltpu.CompilerParams(dimension_semantics=sem, vmem_limit_bytes=VMEM_LIMIT)


def _const_spec(shape):
    nd = len(shape)
    return pl.BlockSpec(shape, lambda *_: (0,) * nd, pipeline_mode=pl.Buffered(1))


def _mod_row(i):
    return jnp.where(i < LAT_TILES, i // SEQ_TILES, BATCH)


def _pos_tile(i):
    return jnp.where(i < LAT_TILES, i % SEQ_TILES, SEQ_TILES + i - LAT_TILES)


def _rms(x, g):
    ms = jnp.mean(x * x, axis=-1, keepdims=True)
    return x * lax.rsqrt(ms + EPS) * g


ADA_TN = 1536


def _ada_kernel(c_ref, w_ref, b_ref, o_ref):
    c = c_ref[...]
    s = c * jax.nn.sigmoid(c)
    o_ref[0] = jnp.dot(s, w_ref[0], precision=HI, preferred_element_type=F32) + b_ref[0]


def _ada_call(cond, w_ada, b_ada):
    n_col = 6 * D_MODEL // ADA_TN
    return pl.pallas_call(
        _ada_kernel,
        grid=(DEPTH, n_col),
        in_specs=[
            pl.BlockSpec((MOD_ROWS, D_MODEL), lambda l, j: (0, 0)),
            pl.BlockSpec((1, D_MODEL, ADA_TN), lambda l, j: (l, 0, j)),
            pl.BlockSpec((1, 1, ADA_TN), lambda l, j: (l, 0, j)),
        ],
        out_specs=pl.BlockSpec((1, MOD_ROWS, ADA_TN), lambda l, j: (l, 0, j)),
        out_shape=jax.ShapeDtypeStruct((DEPTH, MOD_ROWS, 6 * D_MODEL), F32),
        compiler_params=_cparams("arbitrary", "arbitrary"),
        name="ada_mod",
    )(cond, w_ada, b_ada.reshape(DEPTH, 1, 6 * D_MODEL))


def _tile_rows(refs, n_x):
    if n_x == 1:
        return refs[0][...]
    return jnp.where(pl.program_id(0) < LAT_TILES, refs[0][...], refs[1][...])


def _x_specs(n_x):
    if n_x == 1:
        return [pl.BlockSpec((TM, D_MODEL), lambda i: (i, 0))]
    return [pl.BlockSpec((TM, D_MODEL), lambda i: (jnp.minimum(i, LAT_TILES - 1), 0)),
            pl.BlockSpec((TM, D_MODEL), lambda i: (jnp.maximum(i - LAT_TILES, 0), 0))]


def _inproj_kernel(*refs, n_x):
    (mod_ref, g_ref, wut_ref, win_ref, qg_ref, kvg_ref, wq1_ref, wq2_ref, wkk_ref, wv_ref,
     cq_ref, sq_ref, cs_ref, ut_ref, q_ref, k_ref, v_ref) = refs[n_x:]
    x = _tile_rows(refs, n_x)
    sh = mod_ref[0, 0:1, :]
    sc = mod_ref[0, 1:2, :]
    xm = (_rms(x, g_ref[...]) * (1.0 + sc) + sh).astype(BF16)
    ut_ref[...] = lax.dot_general(wut_ref[...], xm, (((1,), (1,)), ((), ())),
                                  preferred_element_type=F32)
    z = jnp.dot(xm, win_ref[...], preferred_element_type=F32)
    qn = _rms(z[:, :Q_LORA], qg_ref[...]).astype(BF16)
    kvn = _rms(z[:, Q_LORA:Q_LORA + KV_LORA], kvg_ref[...]).astype(BF16)
    krr = (z[:, Q_LORA + KV_LORA:] * cs_ref[...]).astype(BF16)
    q1 = jnp.dot(qn, wq1_ref[...], preferred_element_type=F32)
    q2 = jnp.dot(qn, wq2_ref[...], preferred_element_type=F32)
    cq = cq_ref[...]
    sq = sq_ref[...]
    for h in range(N_HEADS):
        sl = slice(h * HEAD_PAD, (h + 1) * HEAD_PAD)
        q_ref[:, sl] = (q1[:, sl] * cq + q2[:, sl] * sq).astype(q_ref.dtype)
    kin = jnp.concatenate([kvn, krr], axis=-1)
    k_ref[...] = jnp.dot(kin, wkk_ref[...], preferred_element_type=F32).astype(k_ref.dtype)
    vv = jnp.dot(kvn, wv_ref[...], preferred_element_type=F32)
    lane = lax.broadcasted_iota(jnp.int32, vv.shape, 1)
    v_ref[...] = jnp.where(lane % HEAD_PAD == V_HEAD, 1.0, vv).astype(v_ref.dtype)


def _inproj_call(xs, mod, g_mix, wut, win, qg, kvg, wq1, wq2, wkk, wv, cq_t, sq_t, cs_t):
    n_tiles = N_TOT // TM
    row = lambda i: (i, 0)
    pos = lambda i: (_pos_tile(i), 0)
    return pl.pallas_call(
        functools.partial(_inproj_kernel, n_x=len(xs)),
        grid=(n_tiles,),
        in_specs=_x_specs(len(xs)) + [
            pl.BlockSpec((1, 6, D_MODEL), lambda i: (_mod_row(i), 0, 0)),
            _const_spec((1, D_MODEL)),
            _const_spec((D_SSM, D_MODEL)),
            _const_spec((D_MODEL, D_IN_REST)),
            _const_spec((1, Q_LORA)),
            _const_spec((1, KV_LORA)),
            _const_spec((Q_LORA, N_HEADS * HEAD_PAD)),
            _const_spec((Q_LORA, N_HEADS * HEAD_PAD)),
            _const_spec((2 * KV_LORA, N_HEADS * HEAD_PAD)),
            _const_spec((KV_LORA, N_HEADS * HEAD_PAD)),
            pl.BlockSpec((TM, LANES), pos),
            pl.BlockSpec((TM, LANES), pos),
            pl.BlockSpec((TM, LANES), pos),
        ],
        out_specs=[
            pl.BlockSpec((D_SSM, TM), lambda i: (0, i)),
            pl.BlockSpec((TM, N_HEADS * HEAD_PAD), row),
            pl.BlockSpec((TM, N_HEADS * HEAD_PAD), row),
            pl.BlockSpec((TM, N_HEADS * HEAD_PAD), row),
        ],
        out_shape=[
            jax.ShapeDtypeStruct((D_SSM, N_TOT), F32),
            jax.ShapeDtypeStruct((N_TOT, N_HEADS * HEAD_PAD), BF16),
            jax.ShapeDtypeStruct((N_TOT, N_HEADS * HEAD_PAD), BF16),
            jax.ShapeDtypeStruct((N_TOT, N_HEADS * HEAD_PAD), BF16),
        ],
        compiler_params=_cparams("parallel"),
        name="in_proj",
    )(*xs, mod, g_mix, wut, win, qg, kvg, wq1, wq2, wkk, wv, cq_t, sq_t, cs_t)


def _attn_kernel(*refs, n_chunks, tq):
    if n_chunks:
        q_ref, k_ref, v_ref, kc_ref, vc_ref, o_ref, s_scr = refs
    else:
        q_ref, kc_ref, vc_ref, _, o_ref, s_scr = refs
    heads =[slice(hh * HEAD_PAD, (hh + 1) * HEAD_PAD) for hh in range(2)]
    qs = [q_ref[:, sl] for sl in heads]

    def put_scores(slot, k_at, width):
        for hh in range(2):
            s_scr[slot, hh, :, :width] = lax.dot_general(
                qs[hh], k_at(heads[hh]), (((1,), (1,)), ((), ())), preferred_element_type=F32)

    def consume(carry, slot, v_at, width):
        new = []
        for hh in range(2):
            m, acc = carry[hh]
            s = s_scr[slot, hh, :, :width]
            m_new = jnp.maximum(m, jnp.max(s, axis=-1, keepdims=True))
            alpha = jnp.exp2(m - m_new)
            p = jnp.exp2(s - m_new).astype(BF16)
            acc = alpha * acc + jnp.dot(p, v_at(heads[hh]), preferred_element_type=F32)
            new.append((m_new, acc))
        return tuple(new)

    def chunk(c):
        rows = pl.ds(pl.multiple_of(c * TK, TK), TK)
        return (lambda sl: k_ref[rows, sl]), (lambda sl: v_ref[rows, sl])

    ctx_k, ctx_v = (lambda sl: kc_ref[:, sl]), (lambda sl: vc_ref[:, sl])
    carry = tuple((jnp.full((tq, 1), -jnp.inf, F32), jnp.zeros((tq, HEAD_PAD), F32))
                  for _ in range(2))
    if n_chunks:
        assert n_chunks % 2 == 0
        put_scores(0, chunk(0)[0], TK)

        def body(j, carry):
            c0 = 2 * j
            put_scores(1, chunk(c0 + 1)[0], TK)
            carry = consume(carry, 0, chunk(c0)[1], TK)
            put_scores(0, chunk(c0 + 2)[0], TK)
            return consume(carry, 1, chunk(c0 + 1)[1], TK)

        carry = lax.fori_loop(0, n_chunks // 2 - 1, body, carry)
        put_scores(1, chunk(n_chunks - 1)[0], TK)
        carry = consume(carry, 0, chunk(n_chunks - 2)[1], TK)
        put_scores(0, ctx_k, CTX_LEN)
        carry = consume(carry, 1, chunk(n_chunks - 1)[1], TK)
    else:
        put_scores(0, ctx_k, CTX_LEN)
    carry = consume(carry, 0, ctx_v, CTX_LEN)
    outs = [acc[:, :V_HEAD] / acc[:, V_HEAD:V_HEAD + 1] for _, acc in carry]
    o_ref[...] = jnp.concatenate(outs, axis=-1).astype(o_ref.dtype)


def _attn_latent_call(q, k, v):
    qt = SEQ // TQ
    ctx0 = N_LAT // CTX_LEN
    return pl.pallas_call(
        functools.partial(_attn_kernel, n_chunks=SEQ // TK, tq=TQ),
        grid=(BATCH, N_HEADS // 2, qt),
        in_specs=[
            pl.BlockSpec((TQ, 2 * HEAD_PAD), lambda b, h, i: (b * qt + i, h)),
            pl.BlockSpec((SEQ, 2 * HEAD_PAD), lambda b, h, i: (b, h)),
            pl.BlockSpec((SEQ, 2 * HEAD_PAD), lambda b, h, i: (b, h)),
            pl.BlockSpec((CTX_LEN, 2 * HEAD_PAD), lambda b, h, i: (ctx0 + b, h)),
            pl.BlockSpec((CTX_LEN, 2 * HEAD_PAD), lambda b, h, i: (ctx0 + b, h)),
        ],
        out_specs=pl.BlockSpec((TQ, 2 * V_HEAD), lambda b, h, i: (b * qt + i, h)),
        out_shape=jax.ShapeDtypeStruct((N_TOT, D_ATT), BF16),
        scratch_shapes=[pltpu.VMEM((2, 2, TQ, TK), F32)],
        compiler_params=_cparams("parallel", "parallel", "arbitrary"),
        name="attn_latent",
    )(q, k, v, k, v)


def _attn_ctx_call(q, k, v, y_att):
    ctx0 = N_LAT // CTX_LEN
    return pl.pallas_call(
        functools.partial(_attn_kernel, n_chunks=0, tq=CTX_LEN),
        grid=(BATCH, N_HEADS // 2),
        in_specs=[
            pl.BlockSpec((CTX_LEN, 2 * HEAD_PAD), lambda b, h: (ctx0 + b, h)),
            pl.BlockSpec((CTX_LEN, 2 * HEAD_PAD), lambda b, h: (ctx0 + b, h)),
            pl.BlockSpec((CTX_LEN, 2 * HEAD_PAD), lambda b, h: (ctx0 + b, h)),
            pl.BlockSpec(memory_space=pl.ANY),
        ],
        out_specs=pl.BlockSpec((CTX_LEN, 2 * V_HEAD), lambda b, h: (ctx0 + b, h)),
        out_shape=jax.ShapeDtypeStruct((N_TOT, D_ATT), BF16),
        input_output_aliases={3: 0},
        scratch_shapes=[pltpu.VMEM((1, 2, CTX_LEN, CTX_LEN), F32)],
        compiler_params=_cparams("parallel", "parallel"),
        name="attn_ctx",
    )(q, k, v, y_att)


PT_LAG, PT_INC, PT_OUT = 0, 2 * SSM_T, 3 * SSM_T
PT_ROWS = 4 * SSM_T
NT_DIMS = (((1,), (1,)), ((), ()))


def _ssm_kernel(u_ref, pta_ref, ptb_ref, rows_ref, cc_ref, at_ref, dv_ref, y_ref,
                m_scr, ws_scr, wct_scr, s_scr, h_scr):
    H, T = SSM_GROUP, SSM_T
    cc = cc_ref[0]

    def scaled(row0, n_rows, ra, rb):
        return pta_ref[0, row0:row0 + n_rows, :] * ra + ptb_ref[0, row0:row0 + n_rows, :] * rb

    def build(i, _):
        row0 = pl.multiple_of(i * T, T)
        b_re = rows_ref[0, pl.ds(i, 1), :]
        b_im = rows_ref[0, pl.ds(H + i, 1), :]
        wl = lax.dot_general(cc, scaled(PT_LAG, 2 * T, b_re, b_im), NT_DIMS,
                             precision=HI, preferred_element_type=F32)
        for co in range(H):
            wb = jnp.broadcast_to(wl[co:co + 1, :], (T, 2 * T))
            toe = pltpu.roll(wb, 0, 1, stride=1, stride_axis=0)
            m_scr[pl.ds(row0, T), co * T:(co + 1) * T] = toe[:, :T].astype(BF16)
        ws_scr[pl.ds(row0, T), :] = scaled(PT_INC, T, b_re, b_im).astype(BF16)
        c_re = rows_ref[0, pl.ds(2 * H + i, 1), :]
        c_im = rows_ref[0, pl.ds(3 * H + i, 1), :]
        wct_scr[pl.ds(row0, T), :] = scaled(PT_OUT, T, c_re, c_im).astype(BF16)
        return 0

    lax.fori_loop(0, H, build, 0)

    u32 = jnp.concatenate([u_ref[c] for c in range(H)], axis=1)
    u = u32.astype(BF16)
    y = jnp.dot(u, m_scr[...], preferred_element_type=F32)
    s_scr[...] = jnp.dot(u, ws_scr[...], preferred_element_type=F32)

    def advance(h, a_row, s_row):
        return h * at_ref[0, a_row:a_row + 1, :] + \
            pltpu.roll(h, SSM_STATE, 1) * at_ref[0, a_row + 1:a_row + 2, :] + s_row

    n_lat, n_ctx = SEQ // T, CTX_LEN // T
    for b in range(BATCH):
        ctx_rows = [N_CHUNK_LAT + b * n_ctx + c for c in range(n_ctx)]
        lat_rows = [b * n_lat + k for k in range(n_lat)]
        hf = jnp.zeros((1, 2 * SSM_STATE), F32)
        hr = jnp.zeros((1, 2 * SSM_STATE), F32)
        for rf, rr in zip(ctx_rows + lat_rows, ctx_rows[::-1] + lat_rows[::-1]):
            h_scr[rf:rf + 1, 0:LANES] = hf
            hf = advance(hf, 0, s_scr[rf:rf + 1, 0:LANES])
            h_scr[rr:rr + 1, LANES:2 * LANES] = hr
            hr = advance(hr, 2, s_scr[rr:rr + 1, LANES:2 * LANES])

    y = y + lax.dot_general(h_scr[...].astype(BF16), wct_scr[...], NT_DIMS,
                            preferred_element_type=F32)
    y = y + u32 * dv_ref[0]
    for c in range(H):
        y_ref[c] = y[:, c * T:(c + 1) * T]


def _ssm_call(u_t, pta, ptb, rows, cc, at, dv):
    g3 = lambda g: (g, 0, 0)
    return pl.pallas_call(
        _ssm_kernel,
        grid=(N_SSM_GROUPS,),
        in_specs=[
            pl.BlockSpec((SSM_GROUP, N_CHUNK, SSM_T), g3),
            pl.BlockSpec((1, PT_ROWS, 4 * SSM_STATE), g3),
            pl.BlockSpec((1, PT_ROWS, 4 * SSM_STATE), g3),
            pl.BlockSpec((1, 4 * SSM_GROUP, 4 * SSM_STATE), g3),
            pl.BlockSpec((1, SSM_GROUP, 4 * SSM_STATE), g3),
            pl.BlockSpec((1, 4, 2 * SSM_STATE), g3),
            pl.BlockSpec((1, 1, SSM_K), g3),
        ],
        out_specs=pl.BlockSpec((SSM_GROUP, N_CHUNK, SSM_T), g3),
        out_shape=jax.ShapeDtypeStruct((D_SSM, N_CHUNK, SSM_T), F32),
        scratch_shapes=[
            pltpu.VMEM((SSM_K, SSM_K), BF16),
            pltpu.VMEM((SSM_K, 4 * SSM_STATE), BF16),
            pltpu.VMEM((SSM_K, 4 * SSM_STATE), BF16),
            pltpu.VMEM((N_CHUNK, 4 * SSM_STATE), F32),
            pltpu.VMEM((N_CHUNK, 4 * SSM_STATE), F32),
        ],
        compiler_params=_cparams("parallel"),
        name="s5_mixer",
    )(u_t, pta, ptb, rows, cc, at, dv)


def _ssm_tables(a_re, a_im, log_dt, b_re, b_im, c_re, c_im, d_skip):
    G, P, H, T = N_SSM_GROUPS, SSM_STATE, SSM_GROUP, SSM_T
    a_re, a_im = a_re.astype(F32), a_im.astype(F32)
    dt = jnp.exp(log_dt.astype(F32))[..., None]
    den = a_re * a_re + a_im * a_im
    mag1 = jnp.exp(dt * a_re)
    ab_re, ab_im = mag1 * jnp.cos(dt * a_im), mag1 * jnp.sin(dt * a_im)
    num_re = ab_re - 1.0
    f_re = (num_re * a_re + ab_im * a_im) / den
    f_im = (ab_im * a_re - num_re * a_im) / den
    b_re, b_im = b_re.astype(F32), b_im.astype(F32)
    bb_re = f_re[..., None] * b_re - f_im[..., None] * b_im
    bb_im = f_re[..., None] * b_im + f_im[..., None] * b_re
    c_re, c_im = c_re.astype(F32), c_im.astype(F32)
    la, th = dt * a_re, dt * a_im

    def powers(d, n):
        mag = jnp.exp(la[d][:, None, :] * n[None, :, None])
        ph = th[d][:, None, :] * n[None, :, None]
        return mag * jnp.cos(ph), mag * jnp.sin(ph)

    j = jnp.arange(2 * T)
    s_idx = jnp.arange(T)
    f_exp = jnp.concatenate([jnp.where(j < T, j, 0), T - 1 - s_idx, s_idx + 1]).astype(F32)
    r_exp = jnp.concatenate([jnp.where(j > T, 2 * T - j, 0), s_idx, T - s_idx]).astype(F32)
    ones = jnp.ones((2 * T,), F32)
    f_on = jnp.concatenate([(j < T).astype(F32), ones])[None, :, None]
    r_on = jnp.concatenate([((j == 0) | (j > T)).astype(F32), ones])[None, :, None]
    f_re_n, f_im_n = powers(0, f_exp)
    r_re_n, r_im_n = powers(1, r_exp)
    f_re_n, f_im_n, r_re_n, r_im_n = f_re_n * f_on, f_im_n * f_on, r_re_n * r_on, r_im_n * r_on
    pta = jnp.concatenate([f_re_n, f_im_n, r_re_n, r_im_n], axis=-1)
    ptb = jnp.concatenate([-f_im_n, f_re_n, -r_im_n, r_re_n], axis=-1)

    def per_channel(v):
        f, r = v[0].transpose(0, 2, 1), v[1].transpose(0, 2, 1)
        return jnp.concatenate([f, f, r, r], axis=-1)
    cf_re, cr_re = c_re[0], c_re[1]
    cf_im, cr_im = c_im[0], c_im[1]
    rows = jnp.concatenate([
        per_channel(bb_re), per_channel(bb_im),
        jnp.concatenate([cf_re, -cf_re, cr_re, -cr_re], axis=-1),
        jnp.concatenate([cf_im, -cf_im, cr_im, -cr_im], axis=-1)], axis=1)
    cc = jnp.concatenate([cf_re, -cf_im, cr_re, -cr_im], axis=-1)

    t_exp = jnp.full((1,), float(T), F32)
    (f_re_t, f_im_t), (r_re_t, r_im_t) = powers(0, t_exp), powers(1, t_exp)
    f_re_t, f_im_t, r_re_t, r_im_t = (v[:, 0, :] for v in (f_re_t, f_im_t, r_re_t, r_im_t))
    at = jnp.stack([jnp.concatenate([f_re_t, f_re_t], -1), jnp.concatenate([-f_im_t, f_im_t], -1),
                    jnp.concatenate([r_re_t, r_re_t], -1), jnp.concatenate([-r_im_t, r_im_t], -1)],
                   axis=1)
    dv = jnp.repeat(d_skip.astype(F32).reshape(G, H), T, axis=-1).reshape(G, 1, H * T)
    return pta, ptb, rows, cc, at, dv


def _top2_gates(logits):
    lane = lax.broadcasted_iota(jnp.int32, logits.shape, 1)
    valid = lane < N_EXPERTS
    lg = jnp.where(valid, logits, -jnp.inf)
    m1 = jnp.max(lg, axis=-1, keepdims=True)
    i1 = jnp.min(jnp.where(lg == m1, lane, LANES), axis=-1, keepdims=True)
    lg2 = jnp.where(lane == i1, -jnp.inf, lg)
    m2 = jnp.max(lg2, axis=-1, keepdims=True)
    i2 = jnp.min(jnp.where(lg2 == m2, lane, LANES), axis=-1, keepdims=True)
    e2 = jnp.exp(m2 - m1)
    w1 = 1.0 / (1.0 + e2)
    return jnp.where(lane == i1, w1, 0.0) + jnp.where(lane == i2, e2 * w1, 0.0)


def _mix_kernel(*refs, with_router, n_x):
    rest = refs[n_x:]
    if with_router:
        (yst_ref, ya_ref, mod_ref, wglut_ref, bglu_ref, wout_ref, gffn_ref, router_ref,
         x1_ref, h2_ref, gate_ref) = rest
    else:
        (yst_ref, ya_ref, mod_ref, wglut_ref, bglu_ref, wout_ref, gffn_ref,
         x1_ref, h2_ref) = rest
    zt = jax.nn.gelu(yst_ref[...], approximate=True)
    glt = zt * jax.nn.sigmoid(
        jnp.dot(wglut_ref[...], zt.astype(BF16), preferred_element_type=F32) + bglu_ref[...])
    mix = jnp.concatenate([glt.T.astype(BF16), ya_ref[...]], axis=-1)
    o = jnp.dot(mix, wout_ref[...], preferred_element_type=F32)
    x1 = _tile_rows(refs, n_x) + mod_ref[0, 2:3, :] * o
    x1_ref[...] = x1
    h2 = _rms(x1, gffn_ref[...]) * (1.0 + mod_ref[0, 4:5, :]) + mod_ref[0, 3:4, :]
    h2_ref[...] = h2.astype(h2_ref.dtype)
    if with_router:
        logits = jnp.dot(h2, router_ref[...], precision=HI, preferred_element_type=F32)
        gate_ref[...] = _top2_gates(logits)


def _mix_call(xs, y_ssm_t, y_att, mod, wglut, bglu, wout, gffn, router, *, n_tiles):
    row = lambda i: (i, 0)
    out_row = row
    with_router = router is not None
    in_specs = _x_specs(len(xs)) + [
        pl.BlockSpec((D_SSM, TM), lambda i: (0, i)),
        pl.BlockSpec((TM, D_ATT), row),
        pl.BlockSpec((1, 6, D_MODEL), lambda i: (_mod_row(i), 0, 0)),
        _const_spec((D_SSM, D_SSM)),
        _const_spec((D_SSM, 1)),
        _const_spec((D_MIX, D_MODEL)),
        _const_spec((1, D_MODEL)),
    ]
    args = [*xs, y_ssm_t, y_att, mod, wglut, bglu, wout, gffn]
    out_specs = [pl.BlockSpec((TM, D_MODEL), out_row), pl.BlockSpec((TM, D_MODEL), out_row)]
    out_shape = [jax.ShapeDtypeStruct((n_tiles * TM, D_MODEL), F32),
                 jax.ShapeDtypeStruct((n_tiles * TM, D_MODEL), BF16)]
    if with_router:
        in_specs.append(_const_spec((D_MODEL, LANES)))
        args.append(router)
        out_specs.append(pl.BlockSpec((TM, LANES), out_row))
        out_shape.append(jax.ShapeDtypeStruct((n_tiles * TM, LANES), F32))
    return pl.pallas_call(
        functools.partial(_mix_kernel, with_router=with_router, n_x=len(xs)),
        grid=(n_tiles,),
        in_specs=in_specs,
        out_specs=out_specs,
        out_shape=out_shape,
        compiler_params=_cparams("parallel"),
        name="mix_out",
    )(*args)


def _ffn_kernel(h_ref, x1_ref, mod_ref, w1_ref, w3_ref, w2_ref, o_ref):
    h = h_ref[...]
    acc = jnp.zeros((TM, D_MODEL), F32)
    for j in range(N_FF_CHUNKS):
        a = jnp.dot(h, w1_ref[j], preferred_element_type=F32)
        b = jnp.dot(h, w3_ref[j], preferred_element_type=F32)
        g = (a * jax.nn.sigmoid(a) * b).astype(BF16)
        acc = acc + jnp.dot(g, w2_ref[j], preferred_element_type=F32)
    o_ref[...] = x1_ref[...] + mod_ref[0, 5:6, :] * acc


def _ffn_call(h2, x1, mod, w1, w3, w2):
    n_tiles = N_TOT // TM
    row = lambda i: (i, 0)
    return pl.pallas_call(
        _ffn_kernel,
        grid=(n_tiles,),
        in_specs=[
            pl.BlockSpec((TM, D_MODEL), row),
            pl.BlockSpec((TM, D_MODEL), row),
            pl.BlockSpec((1, 6, D_MODEL), lambda i: (_mod_row(i), 0, 0)),
            _const_spec((N_FF_CHUNKS, D_MODEL, FF_CHUNK)),
            _const_spec((N_FF_CHUNKS, D_MODEL, FF_CHUNK)),
            _const_spec((N_FF_CHUNKS, FF_CHUNK, D_MODEL)),
        ],
        out_specs=pl.BlockSpec((TM, D_MODEL), row),
        out_shape=jax.ShapeDtypeStruct((N_TOT, D_MODEL), F32),
        compiler_params=_cparams("parallel"),
        name="ffn_dense",
    )(h2, x1, mod, w1, w3, w2)


def _moe_kernel(h_ref, x1_ref, mod_ref, gate_ref, w1_ref, w3_ref, w2_ref, fg_ref, o_ref, acc_scr):
    e = pl.program_id(1)

    @pl.when(e == 0)
    def _():
        acc_scr[...] = jnp.zeros_like(acc_scr)

    h = h_ref[...]
    a = jnp.dot(h, w1_ref[0], preferred_element_type=F32)
    b = jnp.dot(h, w3_ref[0], preferred_element_type=F32)
    g = (a * jax.nn.sigmoid(a) * b).astype(BF16)
    y = jnp.dot(g, w2_ref[0], preferred_element_type=F32)
    lane = lax.broadcasted_iota(jnp.int32, (TM, LANES), 1)
    gate_e = jnp.sum(jnp.where(lane == e, gate_ref[...], 0.0), axis=-1, keepdims=True)
    acc_scr[...] += gate_e * y

    @pl.when(e == N_EXPERTS - 1)
    def _():
        x2 = x1_ref[...] + mod_ref[0, 5:6, :] * acc_scr[...]
        o_ref[...] = _rms(x2, fg_ref[...])


def _moe_call(h2, x1, mod, gate, w1, w3, w2, fg):
    n_tiles = N_LAT // TM
    row = lambda i, e: (i, 0)
    return pl.pallas_call(
        _moe_kernel,
        grid=(n_tiles, N_EXPERTS),
        in_specs=[
            pl.BlockSpec((TM, D_MODEL), row),
            pl.BlockSpec((TM, D_MODEL), row),
            pl.BlockSpec((1, 6, D_MODEL), lambda i, e: (i // SEQ_TILES, 0, 0)),
            pl.BlockSpec((TM, LANES), row),
            pl.BlockSpec((1, D_MODEL, D_FF_EXPERT), lambda i, e: (e, 0, 0)),
            pl.BlockSpec((1, D_MODEL, D_FF_EXPERT), lambda i, e: (e, 0, 0)),
            pl.BlockSpec((1, D_FF_EXPERT, D_MODEL), lambda i, e: (e, 0, 0)),
            pl.BlockSpec((1, D_MODEL), lambda i, e: (0, 0)),
        ],
        out_specs=pl.BlockSpec((TM, D_MODEL), row),
        out_shape=jax.ShapeDtypeStruct((N_LAT, D_MODEL), F32),
        scratch_shapes=[pltpu.VMEM((TM, D_MODEL), F32)],
        compiler_params=_cparams("parallel", "arbitrary"),
        name="moe_experts",
    )(h2, x1, mod, gate, w1, w3, w2, fg)


def _rope_partner_perm():
    perm, sign = [], []
    for j in range(QK_ROPE):
        first_half = (j % AXIS_ROPE) < ROPE_FREQS
        perm.append(j + ROPE_FREQS if first_half else j - ROPE_FREQS)
        sign.append(-1.0 if first_half else 1.0)
    return jnp.array(perm, jnp.int32), jnp.array(sign, F32)


def _rope_tables():
    t = jnp.arange(SEQ)
    row = (t // GRID_W).astype(F32)
    col = (t % GRID_W).astype(F32)
    inv_freq = ROPE_BASE ** (-2.0 * jnp.arange(ROPE_FREQS, dtype=F32) / AXIS_ROPE)
    ang = jnp.concatenate([row[:, None] * inv_freq, row[:, None] * inv_freq,
                           col[:, None] * inv_freq, col[:, None] * inv_freq], axis=1)
    cos = jnp.concatenate([jnp.cos(ang), jnp.ones((N_CTX, QK_ROPE), F32)], axis=0)
    sin = jnp.concatenate([jnp.sin(ang), jnp.zeros((N_CTX, QK_ROPE), F32)], axis=0)
    n = N_CTX + SEQ
    pad32 = jnp.zeros((n, HEAD_PAD - D_QK), F32)
    qs = ATT_SCALE * math.log2(math.e)
    cq = jnp.concatenate([jnp.full((n, QK_NOPE), qs, F32), qs * cos, pad32], axis=1)
    sq = jnp.concatenate([jnp.zeros((n, QK_NOPE), F32), qs * sin, pad32], axis=1)
    cs = jnp.concatenate([cos, sin, jnp.zeros((n, LANES - 2 * QK_ROPE), F32)], axis=1)
    return cq, sq, cs


def _layer_weights(w_in, w_uq, w_ukv):
    perm, sign = _rope_partner_perm()
    s0 = D_SSM + Q_LORA + KV_LORA
    kr_w = w_in[:, s0:s0 + QK_ROPE]
    wut = w_in[:, :D_SSM].T.astype(BF16)
    win = jnp.concatenate([w_in[:, D_SSM:s0], kr_w, kr_w[:, perm] * sign,
                           jnp.zeros((D_MODEL, LANES - 2 * QK_ROPE), F32)], axis=1).astype(BF16)
    uq = w_uq.reshape(Q_LORA, N_HEADS, D_QK)
    nope, rope = uq[..., :QK_NOPE], uq[..., QK_NOPE:]
    zpad = jnp.zeros((Q_LORA, N_HEADS, HEAD_PAD - D_QK), F32)
    wq1 = jnp.concatenate([nope, rope, zpad], axis=-1).reshape(Q_LORA, N_HEADS * HEAD_PAD).astype(BF16)
    wq2 = jnp.concatenate([jnp.zeros_like(nope), rope[..., perm] * sign, zpad], axis=-1)
    wq2 = wq2.reshape(Q_LORA, N_HEADS * HEAD_PAD).astype(BF16)
    ukv = w_ukv.reshape(KV_LORA, N_HEADS, QK_NOPE + V_HEAD)
    wk = jnp.concatenate([ukv[..., :QK_NOPE], jnp.zeros((KV_LORA, N_HEADS, HEAD_PAD - QK_NOPE), F32)],
                         axis=-1).reshape(KV_LORA, N_HEADS * HEAD_PAD)
    eye = jnp.eye(QK_ROPE, dtype=F32)
    place = jnp.concatenate([jnp.zeros((QK_ROPE, QK_NOPE), F32), eye,
                             jnp.zeros((QK_ROPE, HEAD_PAD - D_QK), F32)], axis=1)
    place = jnp.tile(place, (1, N_HEADS))
    spread = jnp.concatenate([place, place, jnp.zeros((LANES - 2 * QK_ROPE, N_HEADS * HEAD_PAD), F32)], 0)
    wkk = jnp.concatenate([wk, spread], axis=0).astype(BF16)
    wv = jnp.concatenate([ukv[..., QK_NOPE:], jnp.zeros((KV_LORA, N_HEADS, HEAD_PAD - V_HEAD), F32)],
                         axis=-1).reshape(KV_LORA, N_HEADS * HEAD_PAD).astype(BF16)
    return wut, win, wq1, wq2, wkk, wv


def kernel(x, c, ctx, c_ctx, w_ada, b_ada, norm_mix, norm_ffn, w_in, q_norm, kv_norm, w_uq, w_ukv,
           ssm_a_re, ssm_a_im, ssm_log_dt, ssm_b_re, ssm_b_im, ssm_c_re, ssm_c_im, ssm_d, w_glu,
           b_glu, w_out, ffn_w1, ffn_w3, ffn_w2, moe_router, moe_w1, moe_w3, moe_w2, final_norm):
    assert x.shape == (BATCH, SEQ, D_MODEL) and ctx.shape == (BATCH, CTX_LEN, D_MODEL)
    cond = jnp.concatenate([c, c_ctx[None, :], jnp.zeros((MOD_ROWS - BATCH - 1, D_MODEL), F32)], axis=0)
    mod_all = _ada_call(cond, w_ada, b_ada).reshape(DEPTH, MOD_ROWS, 6, D_MODEL)
    cq_t, sq_t, cs_t = _rope_tables()
    xs = (x.reshape(N_LAT, D_MODEL), ctx.reshape(N_CTX, D_MODEL))

    out = None
    for i in range(DEPTH):
        last = i == DEPTH - 1
        mod = mod_all[i]
        wut, win, wq1, wq2, wkk, wv = _layer_weights(w_in[i], w_uq[i], w_ukv[i])
        u_t, q, k, v = _inproj_call(xs, mod, norm_mix[i][None, :], wut, win, q_norm[i][None, :],
                                    kv_norm[i][None, :], wq1, wq2, wkk, wv, cq_t, sq_t, cs_t)
        tabs = _ssm_tables(ssm_a_re[i], ssm_a_im[i], ssm_log_dt[i], ssm_b_re[i], ssm_b_im[i],
                           ssm_c_re[i], ssm_c_im[i], ssm_d[i])
        y_ssm_t = _ssm_call(u_t.reshape(D_SSM, N_CHUNK, SSM_T), *tabs).reshape(D_SSM, N_TOT)
        y_att = _attn_latent_call(q, k, v)
        if last:
            n_tiles = LAT_TILES
        else:
            y_att = _attn_ctx_call(q, k, v, y_att)
            n_tiles = N_TOT // TM
        j = i // 2
        if i % 2 == 0:
            router = None
        else:
            router = jnp.concatenate([moe_router[j], jnp.zeros((D_MODEL, LANES - N_EXPERTS), F32)], axis=1)
        res = _mix_call(xs, y_ssm_t, y_att, mod, w_glu[i].T.astype(BF16), b_glu[i][:, None],
                        w_out[i].astype(BF16), norm_ffn[i][None, :], router, n_tiles=n_tiles)
        if i % 2 == 0:
            assert not last
            x1, h2 = res
            w1 = ffn_w1[j].reshape(D_MODEL, N_FF_CHUNKS, FF_CHUNK).transpose(1, 0, 2).astype(BF16)
            w3 = ffn_w3[j].reshape(D_MODEL, N_FF_CHUNKS, FF_CHUNK).transpose(1, 0, 2).astype(BF16)
            w2 = ffn_w2[j].reshape(N_FF_CHUNKS, FF_CHUNK, D_MODEL).astype(BF16)
            xs = (_ffn_call(h2, x1, mod, w1, w3, w2),)
        else:
            assert last
            x1, h2, gate = res
            out = _moe_call(h2, x1, mod, gate, moe_w1[j].astype(BF16), moe_w3[j].astype(BF16),
                            moe_w2[j].astype(BF16), final_norm[None, :])
    return out.reshape(BATCH, SEQ, D_MODEL)
```

```python
import functools
import math

import jax
import jax.numpy as jnp
from jax import lax
from jax.experimental import pallas as pl
from jax.experimental.pallas import tpu as pltpu

D_MODEL = 1024
BATCH = 4
SEQ = 8192
DEPTH = 2
GRID_W = 64
CTX_LEN = 256
D_SSM = 512
SSM_GROUP = 16
N_SSM_GROUPS = D_SSM // SSM_GROUP
SSM_STATE = 64
N_HEADS = 8
QK_NOPE = 64
QK_ROPE = 32
V_HEAD = 64
Q_LORA = 256
KV_LORA = 128
D_QK = QK_NOPE + QK_ROPE
D_ATT = N_HEADS * V_HEAD
D_MIX = D_SSM + D_ATT
D_IN = D_SSM + Q_LORA + KV_LORA + QK_ROPE
AXIS_ROPE = QK_ROPE // 2
ROPE_FREQS = AXIS_ROPE // 2
ROPE_BASE = 10000.0
ATT_SCALE = 1.0 / math.sqrt(D_QK)
D_FF = 2816
N_EXPERTS = 8
D_FF_EXPERT = 1408
EPS = 1e-6

N_CTX = BATCH * CTX_LEN
N_LAT = BATCH * SEQ
N_TOT = N_CTX + N_LAT

LANES = 128
HEAD_PAD = 128
TM = 512
LAT_TILES = N_LAT // TM
SEQ_TILES = SEQ // TM
TQ = 512
TK = 2048
SSM_T = 128
N_CHUNK_LAT = N_LAT // SSM_T
N_CHUNK = N_TOT // SSM_T
SSM_K = SSM_GROUP * SSM_T
FF_CHUNK = 256
N_FF_CHUNKS = D_FF // FF_CHUNK
D_IN_REST = Q_LORA + KV_LORA + LANES
MOD_ROWS = 8
VMEM_LIMIT = 52 * 1024 * 1024

F32 = jnp.float32
BF16 = jnp.bfloat16
HI = lax.Precision.HIGHEST


def _cparams(*sem):
    return pltpu.CompilerParams(dimension_semantics=sem, vmem_limit_bytes=VMEM_LIMIT)


def _const_spec(shape):
    nd = len(shape)
    return pl.BlockSpec(shape, lambda *_: (0,) * nd, pipeline_mode=pl.Buffered(1))


def _mod_row(i):
    return jnp.where(i < LAT_TILES, i // SEQ_TILES, BATCH)


def _pos_tile(i):
    return jnp.where(i < LAT_TILES, i % SEQ_TILES, SEQ_TILES + i - LAT_TILES)


def _rms(x, g):
    ms = jnp.mean(x * x, axis=-1, keepdims=True)
    return x * lax.rsqrt(ms + EPS) * g


ADA_TN = 1536


def _ada_kernel(c_ref, w_ref, b_ref, o_ref):
    c = c_ref[...]
    s = c * jax.nn.sigmoid(c)
    o_ref[0] = jnp.dot(s, w_ref[0], precision=HI, preferred_element_type=F32) + b_ref[0]


def _ada_call(cond, w_ada, b_ada):
    n_col = 6 * D_MODEL // ADA_TN
    return pl.pallas_call(
        _ada_kernel,
        grid=(DEPTH, n_col),
        in_specs=[
            pl.BlockSpec((MOD_ROWS, D_MODEL), lambda l, j: (0, 0)),
            pl.BlockSpec((1, D_MODEL, ADA_TN), lambda l, j: (l, 0, j)),
            pl.BlockSpec((1, 1, ADA_TN), lambda l, j: (l, 0, j)),
        ],
        out_specs=pl.BlockSpec((1, MOD_ROWS, ADA_TN), lambda l, j: (l, 0, j)),
        out_shape=jax.ShapeDtypeStruct((DEPTH, MOD_ROWS, 6 * D_MODEL), F32),
        compiler_params=_cparams("arbitrary", "arbitrary"),
        name="ada_mod",
    )(cond, w_ada, b_ada.reshape(DEPTH, 1, 6 * D_MODEL))


def _tile_rows(refs, n_x):
    if n_x == 1:
        return refs[0][...]
    return jnp.where(pl.program_id(0) < LAT_TILES, refs[0][...], refs[1][...])


def _x_specs(n_x):
    if n_x == 1:
        return [pl.BlockSpec((TM, D_MODEL), lambda i: (i, 0))]
    return [pl.BlockSpec((TM, D_MODEL), lambda i: (jnp.minimum(i, LAT_TILES - 1), 0)),
            pl.BlockSpec((TM, D_MODEL), lambda i: (jnp.maximum(i - LAT_TILES, 0), 0))]


def _inproj_kernel(*refs, n_x):
    (mod_ref, g_ref, wut_ref, win_ref, qg_ref, kvg_ref, wq1_ref, wq2_ref, wkk_ref, wv_ref,
     cq_ref, sq_ref, cs_ref, ut_ref, q_ref, k_ref, v_ref) = refs[n_x:]
    x = _tile_rows(refs, n_x)
    sh = mod_ref[0, 0:1, :]
    sc = mod_ref[0, 1:2, :]
    xm = (_rms(x, g_ref[...]) * (1.0 + sc) + sh).astype(BF16)
    ut_ref[...] = lax.dot_general(wut_ref[...], xm, (((1,), (1,)), ((), ())),
                                  preferred_element_type=F32)
    z = jnp.dot(xm, win_ref[...], preferred_element_type=F32)
    qn = _rms(z[:, :Q_LORA], qg_ref[...]).astype(BF16)
    kvn = _rms(z[:, Q_LORA:Q_LORA + KV_LORA], kvg_ref[...]).astype(BF16)
    krr = (z[:, Q_LORA + KV_LORA:] * cs_ref[...]).astype(BF16)
    q1 = jnp.dot(qn, wq1_ref[...], preferred_element_type=F32)
    q2 = jnp.dot(qn, wq2_ref[...], preferred_element_type=F32)
    cq = cq_ref[...]
    sq = sq_ref[...]
    for h in range(N_HEADS):
        sl = slice(h * HEAD_PAD, (h + 1) * HEAD_PAD)
        q_ref[:, sl] = (q1[:, sl] * cq + q2[:, sl] * sq).astype(q_ref.dtype)
    kin = jnp.concatenate([kvn, krr], axis=-1)
    k_ref[...] = jnp.dot(kin, wkk_ref[...], preferred_element_type=F32).astype(k_ref.dtype)
    vv = jnp.dot(kvn, wv_ref[...], preferred_element_type=F32)
    lane = lax.broadcasted_iota(jnp.int32, vv.shape, 1)
    v_ref[...] = jnp.where(lane % HEAD_PAD == V_HEAD, 1.0, vv).astype(v_ref.dtype)


def _inproj_call(xs, mod, g_mix, wut, win, qg, kvg, wq1, wq2, wkk, wv, cq_t, sq_t, cs_t):
    n_tiles = N_TOT // TM
    row = lambda i: (i, 0)
    pos = lambda i: (_pos_tile(i), 0)
    return pl.pallas_call(
        functools.partial(_inproj_kernel, n_x=len(xs)),
        grid=(n_tiles,),
        in_specs=_x_specs(len(xs)) + [
            pl.BlockSpec((1, 6, D_MODEL), lambda i: (_mod_row(i), 0, 0)),
            _const_spec((1, D_MODEL)),
            _const_spec((D_SSM, D_MODEL)),
            _const_spec((D_MODEL, D_IN_REST)),
            _const_spec((1, Q_LORA)),
            _const_spec((1, KV_LORA)),
            _const_spec((Q_LORA, N_HEADS * HEAD_PAD)),
            _const_spec((Q_LORA, N_HEADS * HEAD_PAD)),
            _const_spec((2 * KV_LORA, N_HEADS * HEAD_PAD)),
            _const_spec((KV_LORA, N_HEADS * HEAD_PAD)),
            pl.BlockSpec((TM, LANES), pos),
            pl.BlockSpec((TM, LANES), pos),
            pl.BlockSpec((TM, LANES), pos),
        ],
        out_specs=[
            pl.BlockSpec((D_SSM, TM), lambda i: (0, i)),
            pl.BlockSpec((TM, N_HEADS * HEAD_PAD), row),
            pl.BlockSpec((TM, N_HEADS * HEAD_PAD), row),
            pl.BlockSpec((TM, N_HEADS * HEAD_PAD), row),
        ],
        out_shape=[
            jax.ShapeDtypeStruct((D_SSM, N_TOT), F32),
            jax.ShapeDtypeStruct((N_TOT, N_HEADS * HEAD_PAD), BF16),
            jax.ShapeDtypeStruct((N_TOT, N_HEADS * HEAD_PAD), BF16),
            jax.ShapeDtypeStruct((N_TOT, N_HEADS * HEAD_PAD), BF16),
        ],
        compiler_params=_cparams("parallel"),
        name="in_proj",
    )(*xs, mod, g_mix, wut, win, qg, kvg, wq1, wq2, wkk, wv, cq_t, sq_t, cs_t)


def _attn_kernel(*refs, n_chunks, tq):
    if n_chunks:
        q_ref, k_ref, v_ref, kc_ref, vc_ref, o_ref, s_scr = refs
    else:
        q_ref, kc_ref, vc_ref, _, o_ref, s_scr = refs
    heads =[slice(hh * HEAD_PAD, (hh + 1) * HEAD_PAD) for hh in range(2)]
    qs = [q_ref[:, sl] for sl in heads]

    def put_scores(slot, k_at, width):
        for hh in range(2):
            s_scr[slot, hh, :, :width] = lax.dot_general(
                qs[hh], k_at(heads[hh]), (((1,), (1,)), ((), ())), preferred_element_type=F32)

    def consume(carry, slot, v_at, width):
        new = []
        for hh in range(2):
            m, acc = carry[hh]
            s = s_scr[slot, hh, :, :width]
            m_new = jnp.maximum(m, jnp.max(s, axis=-1, keepdims=True))
            alpha = jnp.exp2(m - m_new)
            p = jnp.exp2(s - m_new).astype(BF16)
            acc = alpha * acc + jnp.dot(p, v_at(heads[hh]), preferred_element_type=F32)
            new.append((m_new, acc))
        return tuple(new)

    def chunk(c):
        rows = pl.ds(pl.multiple_of(c * TK, TK), TK)
        return (lambda sl: k_ref[rows, sl]), (lambda sl: v_ref[rows, sl])

    ctx_k, ctx_v = (lambda sl: kc_ref[:, sl]), (lambda sl: vc_ref[:, sl])
    carry = tuple((jnp.full((tq, 1), -jnp.inf, F32), jnp.zeros((tq, HEAD_PAD), F32))
                  for _ in range(2))
    if n_chunks:
        assert n_chunks % 2 == 0
        put_scores(0, chunk(0)[0], TK)

        def body(j, carry):
            c0 = 2 * j
            put_scores(1, chunk(c0 + 1)[0], TK)
            carry = consume(carry, 0, chunk(c0)[1], TK)
            put_scores(0, chunk(c0 + 2)[0], TK)
            return consume(carry, 1, chunk(c0 + 1)[1], TK)

        carry = lax.fori_loop(0, n_chunks // 2 - 1, body, carry)
        put_scores(1, chunk(n_chunks - 1)[0], TK)
        carry = consume(carry, 0, chunk(n_chunks - 2)[1], TK)
        put_scores(0, ctx_k, CTX_LEN)
        carry = consume(carry, 1, chunk(n_chunks - 1)[1], TK)
    else:
        put_scores(0, ctx_k, CTX_LEN)
    carry = consume(carry, 0, ctx_v, CTX_LEN)
    outs = [acc[:, :V_HEAD] / acc[:, V_HEAD:V_HEAD + 1] for _, acc in carry]
    o_ref[...] = jnp.concatenate(outs, axis=-1).astype(o_ref.dtype)


def _attn_latent_call(q, k, v):
    qt = SEQ // TQ
    ctx0 = N_LAT // CTX_LEN
    return pl.pallas_call(
        functools.partial(_attn_kernel, n_chunks=SEQ // TK, tq=TQ),
        grid=(BATCH, N_HEADS // 2, qt),
        in_specs=[
            pl.BlockSpec((TQ, 2 * HEAD_PAD), lambda b, h, i: (b * qt + i, h)),
            pl.BlockSpec((SEQ, 2 * HEAD_PAD), lambda b, h, i: (b, h)),
            pl.BlockSpec((SEQ, 2 * HEAD_PAD), lambda b, h, i: (b, h)),
            pl.BlockSpec((CTX_LEN, 2 * HEAD_PAD), lambda b, h, i: (ctx0 + b, h)),
            pl.BlockSpec((CTX_LEN, 2 * HEAD_PAD), lambda b, h, i: (ctx0 + b, h)),
        ],
        out_specs=pl.BlockSpec((TQ, 2 * V_HEAD), lambda b, h, i: (b * qt + i, h)),
        out_shape=jax.ShapeDtypeStruct((N_TOT, D_ATT), BF16),
        scratch_shapes=[pltpu.VMEM((2, 2, TQ, TK), F32)],
        compiler_params=_cparams("parallel", "parallel", "arbitrary"),
        name="attn_latent",
    )(q, k, v, k, v)


def _attn_ctx_call(q, k, v, y_att):
    ctx0 = N_LAT // CTX_LEN
    return pl.pallas_call(
        functools.partial(_attn_kernel, n_chunks=0, tq=CTX_LEN),
        grid=(BATCH, N_HEADS // 2),
        in_specs=[
            pl.BlockSpec((CTX_LEN, 2 * HEAD_PAD), lambda b, h: (ctx0 + b, h)),
            pl.BlockSpec((CTX_LEN, 2 * HEAD_PAD), lambda b, h: (ctx0 + b, h)),
            pl.BlockSpec((CTX_LEN, 2 * HEAD_PAD), lambda b, h: (ctx0 + b, h)),
            pl.BlockSpec(memory_space=pl.ANY),
        ],
        out_specs=pl.BlockSpec((CTX_LEN, 2 * V_HEAD), lambda b, h: (ctx0 + b, h)),
        out_shape=jax.ShapeDtypeStruct((N_TOT, D_ATT), BF16),
        input_output_aliases={3: 0},
        scratch_shapes=[pltpu.VMEM((1, 2, CTX_LEN, CTX_LEN), F32)],
        compiler_params=_cparams("parallel", "parallel"),
        name="attn_ctx",
    )(q, k, v, y_att)


PT_LAG, PT_INC, PT_OUT = 0, 2 * SSM_T, 3 * SSM_T
PT_ROWS = 4 * SSM_T
NT_DIMS = (((1,), (1,)), ((), ()))


def _ssm_kernel(u_ref, pta_ref, ptb_ref, rows_ref, cc_ref, at_ref, dv_ref, y_ref,
                m_scr, ws_scr, wct_scr, s_scr, h_scr):
    H, T = SSM_GROUP, SSM_T
    cc = cc_ref[0]

    def scaled(row0, n_rows, ra, rb):
        return pta_ref[0, row0:row0 + n_rows, :] * ra + ptb_ref[0, row0:row0 + n_rows, :] * rb

    def build(i, _):
        row0 = pl.multiple_of(i * T, T)
        b_re = rows_ref[0, pl.ds(i, 1), :]
        b_im = rows_ref[0, pl.ds(H + i, 1), :]
        wl = lax.dot_general(cc, scaled(PT_LAG, 2 * T, b_re, b_im), NT_DIMS,
                             precision=HI, preferred_element_type=F32)
        for co in range(H):
            wb = jnp.broadcast_to(wl[co:co + 1, :], (T, 2 * T))
            toe = pltpu.roll(wb, 0, 1, stride=1, stride_axis=0)
            m_scr[pl.ds(row0, T), co * T:(co + 1) * T] = toe[:, :T].astype(BF16)
        ws_scr[pl.ds(row0, T), :] = scaled(PT_INC, T, b_re, b_im).astype(BF16)
        c_re = rows_ref[0, pl.ds(2 * H + i, 1), :]
        c_im = rows_ref[0, pl.ds(3 * H + i, 1), :]
        wct_scr[pl.ds(row0, T), :] = scaled(PT_OUT, T, c_re, c_im).astype(BF16)
        return 0

    lax.fori_loop(0, H, build, 0)

    u32 = jnp.concatenate([u_ref[c] for c in range(H)], axis=1)
    u = u32.astype(BF16)
    y = jnp.dot(u, m_scr[...], preferred_element_type=F32)
    s_scr[...] = jnp.dot(u, ws_scr[...], preferred_element_type=F32)

    def advance(h, a_row, s_row):
        return h * at_ref[0, a_row:a_row + 1, :] + \
            pltpu.roll(h, SSM_STATE, 1) * at_ref[0, a_row + 1:a_row + 2, :] + s_row

    n_lat, n_ctx = SEQ // T, CTX_LEN // T
    for b in range(BATCH):
        ctx_rows = [N_CHUNK_LAT + b * n_ctx + c for c in range(n_ctx)]
        lat_rows = [b * n_lat + k for k in range(n_lat)]
        hf = jnp.zeros((1, 2 * SSM_STATE), F32)
        hr = jnp.zeros((1, 2 * SSM_STATE), F32)
        for rf, rr in zip(ctx_rows + lat_rows, ctx_rows[::-1] + lat_rows[::-1]):
            h_scr[rf:rf + 1, 0:LANES] = hf
            hf = advance(hf, 0, s_scr[rf:rf + 1, 0:LANES])
            h_scr[rr:rr + 1, LANES:2 * LANES] = hr
            hr = advance(hr, 2, s_scr[rr:rr + 1, LANES:2 * LANES])

    y = y + lax.dot_general(h_scr[...].astype(BF16), wct_scr[...], NT_DIMS,
                            preferred_element_type=F32)
    y = y + u32 * dv_ref[0]
    for c in range(H):
        y_ref[c] = y[:, c * T:(c + 1) * T]


def _ssm_call(u_t, pta, ptb, rows, cc, at, dv):
    g3 = lambda g: (g, 0, 0)
    return pl.pallas_call(
        _ssm_kernel,
        grid=(N_SSM_GROUPS,),
        in_specs=[
            pl.BlockSpec((SSM_GROUP, N_CHUNK, SSM_T), g3),
            pl.BlockSpec((1, PT_ROWS, 4 * SSM_STATE), g3),
            pl.BlockSpec((1, PT_ROWS, 4 * SSM_STATE), g3),
            pl.BlockSpec((1, 4 * SSM_GROUP, 4 * SSM_STATE), g3),
            pl.BlockSpec((1, SSM_GROUP, 4 * SSM_STATE), g3),
            pl.BlockSpec((1, 4, 2 * SSM_STATE), g3),
            pl.BlockSpec((1, 1, SSM_K), g3),
        ],
        out_specs=pl.BlockSpec((SSM_GROUP, N_CHUNK, SSM_T), g3),
        out_shape=jax.ShapeDtypeStruct((D_SSM, N_CHUNK, SSM_T), F32),
        scratch_shapes=[
            pltpu.VMEM((SSM_K, SSM_K), BF16),
            pltpu.VMEM((SSM_K, 4 * SSM_STATE), BF16),
            pltpu.VMEM((SSM_K, 4 * SSM_STATE), BF16),
            pltpu.VMEM((N_CHUNK, 4 * SSM_STATE), F32),
            pltpu.VMEM((N_CHUNK, 4 * SSM_STATE), F32),
        ],
        compiler_params=_cparams("parallel"),
        name="s5_mixer",
    )(u_t, pta, ptb, rows, cc, at, dv)


def _ssm_tables(a_re, a_im, log_dt, b_re, b_im, c_re, c_im, d_skip):
    G, P, H, T = N_SSM_GROUPS, SSM_STATE, SSM_GROUP, SSM_T
    a_re, a_im = a_re.astype(F32), a_im.astype(F32)
    dt = jnp.exp(log_dt.astype(F32))[..., None]
    den = a_re * a_re + a_im * a_im
    mag1 = jnp.exp(dt * a_re)
    ab_re, ab_im = mag1 * jnp.cos(dt * a_im), mag1 * jnp.sin(dt * a_im)
    num_re = ab_re - 1.0
    f_re = (num_re * a_re + ab_im * a_im) / den
    f_im = (ab_im * a_re - num_re * a_im) / den
    b_re, b_im = b_re.astype(F32), b_im.astype(F32)
    bb_re = f_re[..., None] * b_re - f_im[..., None] * b_im
    bb_im = f_re[..., None] * b_im + f_im[..., None] * b_re
    c_re, c_im = c_re.astype(F32), c_im.astype(F32)
    la, th = dt * a_re, dt * a_im

    def powers(d, n):
        mag = jnp.exp(la[d][:, None, :] * n[None, :, None])
        ph = th[d][:, None, :] * n[None, :, None]
        return mag * jnp.cos(ph), mag * jnp.sin(ph)

    j = jnp.arange(2 * T)
    s_idx = jnp.arange(T)
    f_exp = jnp.concatenate([jnp.where(j < T, j, 0), T - 1 - s_idx, s_idx + 1]).astype(F32)
    r_exp = jnp.concatenate([jnp.where(j > T, 2 * T - j, 0), s_idx, T - s_idx]).astype(F32)
    ones = jnp.ones((2 * T,), F32)
    f_on = jnp.concatenate([(j < T).astype(F32), ones])[None, :, None]
    r_on = jnp.concatenate([((j == 0) | (j > T)).astype(F32), ones])[None, :, None]
    f_re_n, f_im_n = powers(0, f_exp)
    r_re_n, r_im_n = powers(1, r_exp)
    f_re_n, f_im_n, r_re_n, r_im_n = f_re_n * f_on, f_im_n * f_on, r_re_n * r_on, r_im_n * r_on
    pta = jnp.concatenate([f_re_n, f_im_n, r_re_n, r_im_n], axis=-1)
    ptb = jnp.concatenate([-f_im_n, f_re_n, -r_im_n, r_re_n], axis=-1)

    def per_channel(v):
        f, r = v[0].transpose(0, 2, 1), v[1].transpose(0, 2, 1)
        return jnp.concatenate([f, f, r, r], axis=-1)
    cf_re, cr_re = c_re[0], c_re[1]
    cf_im, cr_im = c_im[0], c_im[1]
    rows = jnp.concatenate([
        per_channel(bb_re), per_channel(bb_im),
        jnp.concatenate([cf_re, -cf_re, cr_re, -cr_re], axis=-1),
        jnp.concatenate([cf_im, -cf_im, cr_im, -cr_im], axis=-1)], axis=1)
    cc = jnp.concatenate([cf_re, -cf_im, cr_re, -cr_im], axis=-1)

    t_exp = jnp.full((1,), float(T), F32)
    (f_re_t, f_im_t), (r_re_t, r_im_t) = powers(0, t_exp), powers(1, t_exp)
    f_re_t, f_im_t, r_re_t, r_im_t = (v[:, 0, :] for v in (f_re_t, f_im_t, r_re_t, r_im_t))
    at = jnp.stack([jnp.concatenate([f_re_t, f_re_t], -1), jnp.concatenate([-f_im_t, f_im_t], -1),
                    jnp.concatenate([r_re_t, r_re_t], -1), jnp.concatenate([-r_im_t, r_im_t], -1)],
                   axis=1)
    dv = jnp.repeat(d_skip.astype(F32).reshape(G, H), T, axis=-1).reshape(G, 1, H * T)
    return pta, ptb, rows, cc, at, dv


def _top2_gates(logits):
    lane = lax.broadcasted_iota(jnp.int32, logits.shape, 1)
    valid = lane < N_EXPERTS
    lg = jnp.where(valid, logits, -jnp.inf)
    m1 = jnp.max(lg, axis=-1, keepdims=True)
    i1 = jnp.min(jnp.where(lg == m1, lane, LANES), axis=-1, keepdims=True)
    lg2 = jnp.where(lane == i1, -jnp.inf, lg)
    m2 = jnp.max(lg2, axis=-1, keepdims=True)
    i2 = jnp.min(jnp.where(lg2 == m2, lane, LANES), axis=-1, keepdims=True)
    e2 = jnp.exp(m2 - m1)
    w1 = 1.0 / (1.0 + e2)
    return jnp.where(lane == i1, w1, 0.0) + jnp.where(lane == i2, e2 * w1, 0.0)


def _mix_kernel(*refs, with_router, n_x):
    rest = refs[n_x:]
    if with_router:
        (yst_ref, ya_ref, mod_ref, wglut_ref, bglu_ref, wout_ref, gffn_ref, router_ref,
         x1_ref, h2_ref, gate_ref) = rest
    else:
        (yst_ref, ya_ref, mod_ref, wglut_ref, bglu_ref, wout_ref, gffn_ref,
         x1_ref, h2_ref) = rest
    zt = jax.nn.gelu(yst_ref[...], approximate=True)
    glt = zt * jax.nn.sigmoid(
        jnp.dot(wglut_ref[...], zt.astype(BF16), preferred_element_type=F32) + bglu_ref[...])
    mix = jnp.concatenate([glt.T.astype(BF16), ya_ref[...]], axis=-1)
    o = jnp.dot(mix, wout_ref[...], preferred_element_type=F32)
    x1 = _tile_rows(refs, n_x) + mod_ref[0, 2:3, :] * o
    x1_ref[...] = x1
    h2 = _rms(x1, gffn_ref[...]) * (1.0 + mod_ref[0, 4:5, :]) + mod_ref[0, 3:4, :]
    h2_ref[...] = h2.astype(h2_ref.dtype)
    if with_router:
        h_hi = h2.astype(BF16)
        h_lo = (h2 - h_hi.astype(F32)).astype(BF16)
        o1 = jnp.dot(h_hi, router_ref[0], preferred_element_type=F32)
        o2 = jnp.dot(h_lo, router_ref[1], preferred_element_type=F32)
        logits = o1 + pltpu.roll(o1, LANES - N_EXPERTS, 1) + o2
        gate_ref[...] = _top2_gates(logits)


def _mix_call(xs, y_ssm_t, y_att, mod, wglut, bglu, wout, gffn, router, *, n_tiles):
    row = lambda i: (i, 0)
    out_row = row
    with_router = router is not None
    in_specs = _x_specs(len(xs)) + [
        pl.BlockSpec((D_SSM, TM), lambda i: (0, i)),
        pl.BlockSpec((TM, D_ATT), row),
        pl.BlockSpec((1, 6, D_MODEL), lambda i: (_mod_row(i), 0, 0)),
        _const_spec((D_SSM, D_SSM)),
        _const_spec((D_SSM, 1)),
        _const_spec((D_MIX, D_MODEL)),
        _const_spec((1, D_MODEL)),
    ]
    args = [*xs, y_ssm_t, y_att, mod, wglut, bglu, wout, gffn]
    out_specs = [pl.BlockSpec((TM, D_MODEL), out_row), pl.BlockSpec((TM, D_MODEL), out_row)]
    out_shape = [jax.ShapeDtypeStruct((n_tiles * TM, D_MODEL), F32),
                 jax.ShapeDtypeStruct((n_tiles * TM, D_MODEL), BF16)]
    if with_router:
        in_specs.append(_const_spec((2, D_MODEL, LANES)))
        args.append(router)
        out_specs.append(pl.BlockSpec((TM, LANES), out_row))
        out_shape.append(jax.ShapeDtypeStruct((n_tiles * TM, LANES), F32))
    return pl.pallas_call(
        functools.partial(_mix_kernel, with_router=with_router, n_x=len(xs)),
        grid=(n_tiles,),
        in_specs=in_specs,
        out_specs=out_specs,
        out_shape=out_shape,
        compiler_params=_cparams("parallel"),
        name="mix_out",
    )(*args)


def _ffn_kernel(h_ref, x1_ref, mod_ref, w1_ref, w3_ref, w2_ref, o_ref):
    h = h_ref[...]
    acc = jnp.zeros((TM, D_MODEL), F32)
    for j in range(N_FF_CHUNKS):
        a = jnp.dot(h, w1_ref[j], preferred_element_type=F32)
        b = jnp.dot(h, w3_ref[j], preferred_element_type=F32)
        g = (a * jax.nn.sigmoid(a) * b).astype(BF16)
        acc = acc + jnp.dot(g, w2_ref[j], preferred_element_type=F32)
    o_ref[...] = x1_ref[...] + mod_ref[0, 5:6, :] * acc


def _ffn_call(h2, x1, mod, w1, w3, w2):
    n_tiles = N_TOT // TM
    row = lambda i: (i, 0)
    return pl.pallas_call(
        _ffn_kernel,
        grid=(n_tiles,),
        in_specs=[
            pl.BlockSpec((TM, D_MODEL), row),
            pl.BlockSpec((TM, D_MODEL), row),
            pl.BlockSpec((1, 6, D_MODEL), lambda i: (_mod_row(i), 0, 0)),
            _const_spec((N_FF_CHUNKS, D_MODEL, FF_CHUNK)),
            _const_spec((N_FF_CHUNKS, D_MODEL, FF_CHUNK)),
            _const_spec((N_FF_CHUNKS, FF_CHUNK, D_MODEL)),
        ],
        out_specs=pl.BlockSpec((TM, D_MODEL), row),
        out_shape=jax.ShapeDtypeStruct((N_TOT, D_MODEL), F32),
        compiler_params=_cparams("parallel"),
        name="ffn_dense",
    )(h2, x1, mod, w1, w3, w2)


def _moe_kernel(h_ref, x1_ref, mod_ref, gate_ref, w1_ref, w3_ref, w2_ref, fg_ref, o_ref, acc_scr):
    e = pl.program_id(1)

    @pl.when(e == 0)
    def _():
        acc_scr[...] = jnp.zeros_like(acc_scr)

    h = h_ref[...]
    a = jnp.dot(h, w1_ref[0], preferred_element_type=F32)
    b = jnp.dot(h, w3_ref[0], preferred_element_type=F32)
    g = (a * jax.nn.sigmoid(a) * b).astype(BF16)
    y = jnp.dot(g, w2_ref[0], preferred_element_type=F32)
    lane = lax.broadcasted_iota(jnp.int32, (TM, LANES), 1)
    gate_e = jnp.sum(jnp.where(lane == e, gate_ref[...], 0.0), axis=-1, keepdims=True)
    acc_scr[...] += gate_e * y

    @pl.when(e == N_EXPERTS - 1)
    def _():
        x2 = x1_ref[...] + mod_ref[0, 5:6, :] * acc_scr[...]
        o_ref[...] = _rms(x2, fg_ref[...])


def _moe_call(h2, x1, mod, gate, w1, w3, w2, fg):
    n_tiles = N_LAT // TM
    row = lambda i, e: (i, 0)
    return pl.pallas_call(
        _moe_kernel,
        grid=(n_tiles, N_EXPERTS),
        in_specs=[
            pl.BlockSpec((TM, D_MODEL), row),
            pl.BlockSpec((TM, D_MODEL), row),
            pl.BlockSpec((1, 6, D_MODEL), lambda i, e: (i // SEQ_TILES, 0, 0)),
            pl.BlockSpec((TM, LANES), row),
            pl.BlockSpec((1, D_MODEL, D_FF_EXPERT), lambda i, e: (e, 0, 0)),
            pl.BlockSpec((1, D_MODEL, D_FF_EXPERT), lambda i, e: (e, 0, 0)),
            pl.BlockSpec((1, D_FF_EXPERT, D_MODEL), lambda i, e: (e, 0, 0)),
            pl.BlockSpec((1, D_MODEL), lambda i, e: (0, 0)),
        ],
        out_specs=pl.BlockSpec((TM, D_MODEL), row),
        out_shape=jax.ShapeDtypeStruct((N_LAT, D_MODEL), F32),
        scratch_shapes=[pltpu.VMEM((TM, D_MODEL), F32)],
        compiler_params=_cparams("parallel", "arbitrary"),
        name="moe_experts",
    )(h2, x1, mod, gate, w1, w3, w2, fg)


def _rope_partner_perm():
    perm, sign = [], []
    for j in range(QK_ROPE):
        first_half = (j % AXIS_ROPE) < ROPE_FREQS
        perm.append(j + ROPE_FREQS if first_half else j - ROPE_FREQS)
        sign.append(-1.0 if first_half else 1.0)
    return jnp.array(perm, jnp.int32), jnp.array(sign, F32)


def _rope_tables():
    t = jnp.arange(SEQ)
    row = (t // GRID_W).astype(F32)
    col = (t % GRID_W).astype(F32)
    inv_freq = ROPE_BASE ** (-2.0 * jnp.arange(ROPE_FREQS, dtype=F32) / AXIS_ROPE)
    ang = jnp.concatenate([row[:, None] * inv_freq, row[:, None] * inv_freq,
                           col[:, None] * inv_freq, col[:, None] * inv_freq], axis=1)
    cos = jnp.concatenate([jnp.cos(ang), jnp.ones((N_CTX, QK_ROPE), F32)], axis=0)
    sin = jnp.concatenate([jnp.sin(ang), jnp.zeros((N_CTX, QK_ROPE), F32)], axis=0)
    n = N_CTX + SEQ
    pad32 = jnp.zeros((n, HEAD_PAD - D_QK), F32)
    qs = ATT_SCALE * math.log2(math.e)
    cq = jnp.concatenate([jnp.full((n, QK_NOPE), qs, F32), qs * cos, pad32], axis=1)
    sq = jnp.concatenate([jnp.zeros((n, QK_NOPE), F32), qs * sin, pad32], axis=1)
    cs = jnp.concatenate([cos, sin, jnp.zeros((n, LANES - 2 * QK_ROPE), F32)], axis=1)
    return cq, sq, cs


def _layer_weights(w_in, w_uq, w_ukv):
    perm, sign = _rope_partner_perm()
    s0 = D_SSM + Q_LORA + KV_LORA
    kr_w = w_in[:, s0:s0 + QK_ROPE]
    wut = w_in[:, :D_SSM].T.astype(BF16)
    win = jnp.concatenate([w_in[:, D_SSM:s0], kr_w, kr_w[:, perm] * sign,
                           jnp.zeros((D_MODEL, LANES - 2 * QK_ROPE), F32)], axis=1).astype(BF16)
    uq = w_uq.reshape(Q_LORA, N_HEADS, D_QK)
    nope, rope = uq[..., :QK_NOPE], uq[..., QK_NOPE:]
    zpad = jnp.zeros((Q_LORA, N_HEADS, HEAD_PAD - D_QK), F32)
    wq1 = jnp.concatenate([nope, rope, zpad], axis=-1).reshape(Q_LORA, N_HEADS * HEAD_PAD).astype(BF16)
    wq2 = jnp.concatenate([jnp.zeros_like(nope), rope[..., perm] * sign, zpad], axis=-1)
    wq2 = wq2.reshape(Q_LORA, N_HEADS * HEAD_PAD).astype(BF16)
    ukv = w_ukv.reshape(KV_LORA, N_HEADS, QK_NOPE + V_HEAD)
    wk = jnp.concatenate([ukv[..., :QK_NOPE], jnp.zeros((KV_LORA, N_HEADS, HEAD_PAD - QK_NOPE), F32)],
                         axis=-1).reshape(KV_LORA, N_HEADS * HEAD_PAD)
    eye = jnp.eye(QK_ROPE, dtype=F32)
    place = jnp.concatenate([jnp.zeros((QK_ROPE, QK_NOPE), F32), eye,
                             jnp.zeros((QK_ROPE, HEAD_PAD - D_QK), F32)], axis=1)
    place = jnp.tile(place, (1, N_HEADS))
    spread = jnp.concatenate([place, place, jnp.zeros((LANES - 2 * QK_ROPE, N_HEADS * HEAD_PAD), F32)], 0)
    wkk = jnp.concatenate([wk, spread], axis=0).astype(BF16)
    wv = jnp.concatenate([ukv[..., QK_NOPE:], jnp.zeros((KV_LORA, N_HEADS, HEAD_PAD - V_HEAD), F32)],
                         axis=-1).reshape(KV_LORA, N_HEADS * HEAD_PAD).astype(BF16)
    return wut, win, wq1, wq2, wkk, wv


def kernel(x, c, ctx, c_ctx, w_ada, b_ada, norm_mix, norm_ffn, w_in, q_norm, kv_norm, w_uq, w_ukv,
           ssm_a_re, ssm_a_im, ssm_log_dt, ssm_b_re, ssm_b_im, ssm_c_re, ssm_c_im, ssm_d, w_glu,
           b_glu, w_out, ffn_w1, ffn_w3, ffn_w2, moe_router, moe_w1, moe_w3, moe_w2, final_norm):
    assert x.shape == (BATCH, SEQ, D_MODEL) and ctx.shape == (BATCH, CTX_LEN, D_MODEL)
    cond = jnp.concatenate([c, c_ctx[None, :], jnp.zeros((MOD_ROWS - BATCH - 1, D_MODEL), F32)], axis=0)
    mod_all = _ada_call(cond, w_ada, b_ada).reshape(DEPTH, MOD_ROWS, 6, D_MODEL)
    cq_t, sq_t, cs_t = _rope_tables()
    xs = (x.reshape(N_LAT, D_MODEL), ctx.reshape(N_CTX, D_MODEL))

    out = None
    for i in range(DEPTH):
        last = i == DEPTH - 1
        mod = mod_all[i]
        wut, win, wq1, wq2, wkk, wv = _layer_weights(w_in[i], w_uq[i], w_ukv[i])
        u_t, q, k, v = _inproj_call(xs, mod, norm_mix[i][None, :], wut, win, q_norm[i][None, :],
                                    kv_norm[i][None, :], wq1, wq2, wkk, wv, cq_t, sq_t, cs_t)
        tabs = _ssm_tables(ssm_a_re[i], ssm_a_im[i], ssm_log_dt[i], ssm_b_re[i], ssm_b_im[i],
                           ssm_c_re[i], ssm_c_im[i], ssm_d[i])
        y_ssm_t = _ssm_call(u_t.reshape(D_SSM, N_CHUNK, SSM_T), *tabs).reshape(D_SSM, N_TOT)
        y_att = _attn_latent_call(q, k, v)
        if last:
            n_tiles = LAT_TILES
        else:
            y_att = _attn_ctx_call(q, k, v, y_att)
            n_tiles = N_TOT // TM
        j = i // 2
        if i % 2 == 0:
            router = None
        else:
            r_hi = moe_router[j].astype(BF16)
            r_lo = (moe_router[j] - r_hi.astype(F32)).astype(BF16)
            zr = jnp.zeros((D_MODEL, LANES - 2 * N_EXPERTS), BF16)
            router = jnp.stack([jnp.concatenate([r_hi, r_lo, zr], axis=1),
                                jnp.concatenate([r_hi, jnp.zeros_like(r_lo), zr], axis=1)])
        res = _mix_call(xs, y_ssm_t, y_att, mod, w_glu[i].T.astype(BF16), b_glu[i][:, None],
                        w_out[i].astype(BF16), norm_ffn[i][None, :], router, n_tiles=n_tiles)
        if i % 2 == 0:
            assert not last
            x1, h2 = res
            w1 = ffn_w1[j].reshape(D_MODEL, N_FF_CHUNKS, FF_CHUNK).transpose(1, 0, 2).astype(BF16)
            w3 = ffn_w3[j].reshape(D_MODEL, N_FF_CHUNKS, FF_CHUNK).transpose(1, 0, 2).astype(BF16)
            w2 = ffn_w2[j].reshape(N_FF_CHUNKS, FF_CHUNK, D_MODEL).astype(BF16)
            xs = (_ffn_call(h2, x1, mod, w1, w3, w2),)
        else:
            assert last
            x1, h2, gate = res
            out = _moe_call(h2, x1, mod, gate, moe_w1[j].astype(BF16), moe_w3[j].astype(BF16),
                            moe_w2[j].astype(BF16), final_norm[None, :])
    return out.reshape(BATCH, SEQ, D_MODEL)
```

```python
import functools
import math

import jax
import jax.numpy as jnp
from jax import lax
from jax.experimental import pallas as pl
from jax.experimental.pallas import tpu as pltpu

D_MODEL = 1024
BATCH = 4
SEQ = 8192
DEPTH = 2
GRID_W = 64
CTX_LEN = 256
D_SSM = 512
SSM_GROUP = 16
N_SSM_GROUPS = D_SSM // SSM_GROUP
SSM_STATE = 64
N_HEADS = 8
QK_NOPE = 64
QK_ROPE = 32
V_HEAD = 64
Q_LORA = 256
KV_LORA = 128
D_QK = QK_NOPE + QK_ROPE
D_ATT = N_HEADS * V_HEAD
D_MIX = D_SSM + D_ATT
D_IN = D_SSM + Q_LORA + KV_LORA + QK_ROPE
AXIS_ROPE = QK_ROPE // 2
ROPE_FREQS = AXIS_ROPE // 2
ROPE_BASE = 10000.0
ATT_SCALE = 1.0 / math.sqrt(D_QK)
D_FF = 2816
N_EXPERTS = 8
D_FF_EXPERT = 1408
EPS = 1e-6

N_CTX = BATCH * CTX_LEN
N_LAT = BATCH * SEQ
N_TOT = N_CTX + N_LAT

LANES = 128
HEAD_PAD = 128
TM = 512
LAT_TILES = N_LAT // TM
SEQ_TILES = SEQ // TM
TQ = 512
TK = 2048
SSM_T = 128
N_CHUNK_LAT = N_LAT // SSM_T
N_CHUNK = N_TOT // SSM_T
SSM_K = SSM_GROUP * SSM_T
FF_CHUNK = 256
N_FF_CHUNKS = D_FF // FF_CHUNK
D_IN_REST = Q_LORA + KV_LORA + LANES
MOD_ROWS = 8
VMEM_LIMIT = 52 * 1024 * 1024

F32 = jnp.float32
BF16 = jnp.bfloat16
HI = lax.Precision.HIGHEST


def _cparams(*sem):
    return pltpu.CompilerParams(dimension_semantics=sem, vmem_limit_bytes=VMEM_LIMIT)


def _const_spec(shape):
    nd = len(shape)
    return pl.BlockSpec(shape, lambda *_: (0,) * nd, pipeline_mode=pl.Buffered(1))


def _mod_row(i):
    return jnp.where(i < LAT_TILES, i // SEQ_TILES, BATCH)


def _pos_tile(i):
    return jnp.where(i < LAT_TILES, i % SEQ_TILES, SEQ_TILES + i - LAT_TILES)


def _rms(x, g):
    ms = jnp.mean(x * x, axis=-1, keepdims=True)
    return x * lax.rsqrt(ms + EPS) * g


ADA_TN = 1536


def _ada_kernel(c_ref, w_ref, b_ref, o_ref):
    c = c_ref[...]
    s = c * jax.nn.sigmoid(c)
    o_ref[0] = jnp.dot(s, w_ref[0], precision=HI, preferred_element_type=F32) + b_ref[0]


def _ada_call(cond, w_ada, b_ada):
    n_col = 6 * D_MODEL // ADA_TN
    return pl.pallas_call(
        _ada_kernel,
        grid=(DEPTH, n_col),
        in_specs=[
            pl.BlockSpec((MOD_ROWS, D_MODEL), lambda l, j: (0, 0)),
            pl.BlockSpec((1, D_MODEL, ADA_TN), lambda l, j: (l, 0, j)),
            pl.BlockSpec((1, 1, ADA_TN), lambda l, j: (l, 0, j)),
        ],
        out_specs=pl.BlockSpec((1, MOD_ROWS, ADA_TN), lambda l, j: (l, 0, j)),
        out_shape=jax.ShapeDtypeStruct((DEPTH, MOD_ROWS, 6 * D_MODEL), F32),
        compiler_params=_cparams("arbitrary", "arbitrary"),
        name="ada_mod",
    )(cond, w_ada, b_ada.reshape(DEPTH, 1, 6 * D_MODEL))


def _tile_rows(refs, n_x):
    if n_x == 1:
        return refs[0][...]
    return jnp.where(pl.program_id(0) < LAT_TILES, refs[0][...], refs[1][...])


def _x_specs(n_x):
    if n_x == 1:
        return [pl.BlockSpec((TM, D_MODEL), lambda i: (i, 0))]
    return [pl.BlockSpec((TM, D_MODEL), lambda i: (jnp.minimum(i, LAT_TILES - 1), 0)),
            pl.BlockSpec((TM, D_MODEL), lambda i: (jnp.maximum(i - LAT_TILES, 0), 0))]


def _inproj_kernel(*refs, n_x):
    (mod_ref, g_ref, wut_ref, win_ref, qg_ref, kvg_ref, wq1_ref, wq2_ref, wkk_ref, wv_ref,
     cq_ref, sq_ref, cs_ref, ut_ref, q_ref, k_ref, v_ref) = refs[n_x:]
    x = _tile_rows(refs, n_x)
    sh = mod_ref[0, 0:1, :]
    sc = mod_ref[0, 1:2, :]
    xm = (_rms(x, g_ref[...]) * (1.0 + sc) + sh).astype(BF16)
    ut_ref[...] = lax.dot_general(wut_ref[...], xm, (((1,), (1,)), ((), ())),
                                  preferred_element_type=F32)
    z = jnp.dot(xm, win_ref[...], preferred_element_type=F32)
    qn = _rms(z[:, :Q_LORA], qg_ref[...]).astype(BF16)
    kvn = _rms(z[:, Q_LORA:Q_LORA + KV_LORA], kvg_ref[...]).astype(BF16)
    krr = (z[:, Q_LORA + KV_LORA:] * cs_ref[...]).astype(BF16)
    q1 = jnp.dot(qn, wq1_ref[...], preferred_element_type=F32)
    q2 = jnp.dot(qn, wq2_ref[...], preferred_element_type=F32)
    cq = cq_ref[...]
    sq = sq_ref[...]
    for h in range(N_HEADS):
        sl = slice(h * HEAD_PAD, (h + 1) * HEAD_PAD)
        q_ref[:, sl] = (q1[:, sl] * cq + q2[:, sl] * sq).astype(q_ref.dtype)
    kin = jnp.concatenate([kvn, krr], axis=-1)
    k_ref[...] = jnp.dot(kin, wkk_ref[...], preferred_element_type=F32).astype(k_ref.dtype)
    vv = jnp.dot(kvn, wv_ref[...], preferred_element_type=F32)
    lane = lax.broadcasted_iota(jnp.int32, vv.shape, 1)
    v_ref[...] = jnp.where(lane % HEAD_PAD == V_HEAD, 1.0, vv).astype(v_ref.dtype)


def _inproj_call(xs, mod, g_mix, wut, win, qg, kvg, wq1, wq2, wkk, wv, cq_t, sq_t, cs_t):
    n_tiles = N_TOT // TM
    row = lambda i: (i, 0)
    pos = lambda i: (_pos_tile(i), 0)
    return pl.pallas_call(
        functools.partial(_inproj_kernel, n_x=len(xs)),
        grid=(n_tiles,),
        in_specs=_x_specs(len(xs)) + [
            pl.BlockSpec((1, 6, D_MODEL), lambda i: (_mod_row(i), 0, 0)),
            _const_spec((1, D_MODEL)),
            _const_spec((D_SSM, D_MODEL)),
            _const_spec((D_MODEL, D_IN_REST)),
            _const_spec((1, Q_LORA)),
            _const_spec((1, KV_LORA)),
            _const_spec((Q_LORA, N_HEADS * HEAD_PAD)),
            _const_spec((Q_LORA, N_HEADS * HEAD_PAD)),
            _const_spec((2 * KV_LORA, N_HEADS * HEAD_PAD)),
            _const_spec((KV_LORA, N_HEADS * HEAD_PAD)),
            pl.BlockSpec((TM, LANES), pos),
            pl.BlockSpec((TM, LANES), pos),
            pl.BlockSpec((TM, LANES), pos),
        ],
        out_specs=[
            pl.BlockSpec((D_SSM, TM), lambda i: (0, i)),
            pl.BlockSpec((TM, N_HEADS * HEAD_PAD), row),
            pl.BlockSpec((TM, N_HEADS * HEAD_PAD), row),
            pl.BlockSpec((TM, N_HEADS * HEAD_PAD), row),
        ],
        out_shape=[
            jax.ShapeDtypeStruct((D_SSM, N_TOT), F32),
            jax.ShapeDtypeStruct((N_TOT, N_HEADS * HEAD_PAD), BF16),
            jax.ShapeDtypeStruct((N_TOT, N_HEADS * HEAD_PAD), BF16),
            jax.ShapeDtypeStruct((N_TOT, N_HEADS * HEAD_PAD), BF16),
        ],
        compiler_params=_cparams("parallel"),
        name="in_proj",
    )(*xs, mod, g_mix, wut, win, qg, kvg, wq1, wq2, wkk, wv, cq_t, sq_t, cs_t)


def _attn_kernel(*refs, n_chunks, tq):
    if n_chunks:
        q_ref, k_ref, v_ref, kc_ref, vc_ref, o_ref, s_scr = refs
    else:
        q_ref, kc_ref, vc_ref, _, o_ref, s_scr = refs
    heads =[slice(hh * HEAD_PAD, (hh + 1) * HEAD_PAD) for hh in range(2)]
    qs = [q_ref[:, sl] for sl in heads]

    def put_scores(slot, k_at, width):
        for hh in range(2):
            s_scr[slot, hh, :, :width] = lax.dot_general(
                qs[hh], k_at(heads[hh]), (((1,), (1,)), ((), ())), preferred_element_type=F32)

    def consume(carry, slot, v_at, width):
        new = []
        for hh in range(2):
            m, acc = carry[hh]
            s = s_scr[slot, hh, :, :width]
            m_new = jnp.maximum(m, jnp.max(s, axis=-1, keepdims=True))
            alpha = jnp.exp2(m - m_new)
            p = jnp.exp2(s - m_new).astype(BF16)
            acc = alpha * acc + jnp.dot(p, v_at(heads[hh]), preferred_element_type=F32)
            new.append((m_new, acc))
        return tuple(new)

    def chunk(c):
        rows = pl.ds(pl.multiple_of(c * TK, TK), TK)
        return (lambda sl: k_ref[rows, sl]), (lambda sl: v_ref[rows, sl])

    ctx_k, ctx_v = (lambda sl: kc_ref[:, sl]), (lambda sl: vc_ref[:, sl])
    carry = tuple((jnp.full((tq, 1), -jnp.inf, F32), jnp.zeros((tq, HEAD_PAD), F32))
                  for _ in range(2))
    if n_chunks:
        assert n_chunks % 2 == 0
        put_scores(0, chunk(0)[0], TK)

        def body(j, carry):
            c0 = 2 * j
            put_scores(1, chunk(c0 + 1)[0], TK)
            carry = consume(carry, 0, chunk(c0)[1], TK)
            put_scores(0, chunk(c0 + 2)[0], TK)
            return consume(carry, 1, chunk(c0 + 1)[1], TK)

        carry = lax.fori_loop(0, n_chunks // 2 - 1, body, carry)
        put_scores(1, chunk(n_chunks - 1)[0], TK)
        carry = consume(carry, 0, chunk(n_chunks - 2)[1], TK)
        put_scores(0, ctx_k, CTX_LEN)
        carry = consume(carry, 1, chunk(n_chunks - 1)[1], TK)
    else:
        put_scores(0, ctx_k, CTX_LEN)
    carry = consume(carry, 0, ctx_v, CTX_LEN)
    outs = [acc[:, :V_HEAD] / acc[:, V_HEAD:V_HEAD + 1] for _, acc in carry]
    o_ref[...] = jnp.concatenate(outs, axis=-1).astype(o_ref.dtype)


def _attn_latent_call(q, k, v):
    qt = SEQ // TQ
    ctx0 = N_LAT // CTX_LEN
    return pl.pallas_call(
        functools.partial(_attn_kernel, n_chunks=SEQ // TK, tq=TQ),
        grid=(BATCH, N_HEADS // 2, qt),
        in_specs=[
            pl.BlockSpec((TQ, 2 * HEAD_PAD), lambda b, h, i: (b * qt + i, h)),
            pl.BlockSpec((SEQ, 2 * HEAD_PAD), lambda b, h, i: (b, h)),
            pl.BlockSpec((SEQ, 2 * HEAD_PAD), lambda b, h, i: (b, h)),
            pl.BlockSpec((CTX_LEN, 2 * HEAD_PAD), lambda b, h, i: (ctx0 + b, h)),
            pl.BlockSpec((CTX_LEN, 2 * HEAD_PAD), lambda b, h, i: (ctx0 + b, h)),
        ],
        out_specs=pl.BlockSpec((TQ, 2 * V_HEAD), lambda b, h, i: (b * qt + i, h)),
        out_shape=jax.ShapeDtypeStruct((N_TOT, D_ATT), BF16),
        scratch_shapes=[pltpu.VMEM((2, 2, TQ, TK), F32)],
        compiler_params=_cparams("parallel", "parallel", "arbitrary"),
        name="attn_latent",
    )(q, k, v, k, v)


def _attn_ctx_call(q, k, v, y_att):
    ctx0 = N_LAT // CTX_LEN
    return pl.pallas_call(
        functools.partial(_attn_kernel, n_chunks=0, tq=CTX_LEN),
        grid=(BATCH, N_HEADS // 2),
        in_specs=[
            pl.BlockSpec((CTX_LEN, 2 * HEAD_PAD), lambda b, h: (ctx0 + b, h)),
            pl.BlockSpec((CTX_LEN, 2 * HEAD_PAD), lambda b, h: (ctx0 + b, h)),
            pl.BlockSpec((CTX_LEN, 2 * HEAD_PAD), lambda b, h: (ctx0 + b, h)),
            pl.BlockSpec(memory_space=pl.ANY),
        ],
        out_specs=pl.BlockSpec((CTX_LEN, 2 * V_HEAD), lambda b, h: (ctx0 + b, h)),
        out_shape=jax.ShapeDtypeStruct((N_TOT, D_ATT), BF16),
        input_output_aliases={3: 0},
        scratch_shapes=[pltpu.VMEM((1, 2, CTX_LEN, CTX_LEN), F32)],
        compiler_params=_cparams("parallel", "parallel"),
        name="attn_ctx",
    )(q, k, v, y_att)


PT_LAG, PT_INC, PT_OUT = 0, 2 * SSM_T, 3 * SSM_T
PT_ROWS = 4 * SSM_T
NT_DIMS = (((1,), (1,)), ((), ()))


def _ssm_kernel(u_ref, pta_ref, ptb_ref, rows_ref, cc_ref, at_ref, dv_ref, y_ref,
                abt_hi_scr, abt_lo_scr, wl_scr, m_scr, ws_scr, wct_scr, s_scr, h_scr):
    H, T = SSM_GROUP, SSM_T

    def split(v):
        hi = v.astype(BF16)
        return hi, (v - hi.astype(F32)).astype(BF16)

    def scaled(row0, n_rows, ra, rb):
        return pta_ref[0, row0:row0 + n_rows, :] * ra + ptb_ref[0, row0:row0 + n_rows, :] * rb

    def build_tables(i, _):
        row0 = pl.multiple_of(i * T, T)
        b_re = rows_ref[0, pl.ds(i, 1), :]
        b_im = rows_ref[0, pl.ds(H + i, 1), :]
        lag_rows = pl.ds(pl.multiple_of(i * 2 * T, 2 * T), 2 * T)
        abt_hi_scr[lag_rows, :], abt_lo_scr[lag_rows, :] = split(scaled(PT_LAG, 2 * T, b_re, b_im))
        ws_scr[pl.ds(row0, T), :] = scaled(PT_INC, T, b_re, b_im).astype(BF16)
        c_re = rows_ref[0, pl.ds(2 * H + i, 1), :]
        c_im = rows_ref[0, pl.ds(3 * H + i, 1), :]
        wct_scr[pl.ds(row0, T), :] = scaled(PT_OUT, T, c_re, c_im).astype(BF16)
        return 0

    lax.fori_loop(0, H, build_tables, 0)
    cc_hi, cc_lo = split(cc_ref[0])
    nt = lambda a, b: lax.dot_general(a, b, NT_DIMS, preferred_element_type=F32)
    wl_scr[...] = (nt(cc_hi, abt_hi_scr[...]) + nt(cc_hi, abt_lo_scr[...])
                   + nt(cc_lo, abt_hi_scr[...]))

    def build_toeplitz(ci, _):
        row0 = pl.multiple_of(ci * T, T)
        lane0 = pl.multiple_of(ci * 2 * T, 2 * T)
        for co in range(H):
            wb = jnp.broadcast_to(wl_scr[co:co + 1, pl.ds(lane0, 2 * T)], (T, 2 * T))
            toe = pltpu.roll(wb, 0, 1, stride=1, stride_axis=0)
            m_scr[pl.ds(row0, T), co * T:(co + 1) * T] = toe[:, :T].astype(BF16)
        return 0

    lax.fori_loop(0, H, build_toeplitz, 0)

    u32 = jnp.concatenate([u_ref[c] for c in range(H)], axis=1)
    u = u32.astype(BF16)
    y = jnp.dot(u, m_scr[...], preferred_element_type=F32)
    s = jnp.dot(u, ws_scr[...], preferred_element_type=F32)
    s_scr[:, 0:2 * LANES] = s
    s_scr[:, 2 * LANES:3 * LANES] = pltpu.roll(s[:, 0:LANES], SSM_STATE, 1)
    s_scr[:, 3 * LANES:4 * LANES] = pltpu.roll(s[:, LANES:2 * LANES], SSM_STATE, 1)

    def advance(h, h_sw, d, row):
        a0 = at_ref[0, 2 * d:2 * d + 1, :]
        a1 = at_ref[0, 2 * d + 1:2 * d + 2, :]
        lanes = slice(d * LANES, (d + 1) * LANES)
        lanes_sw = slice((2 + d) * LANES, (3 + d) * LANES)
        return (h * a0 + h_sw * a1 + s_scr[row:row + 1, lanes],
                h_sw * a0 - h * a1 + s_scr[row:row + 1, lanes_sw])

    n_lat, n_ctx = SEQ // T, CTX_LEN // T
    zero = jnp.zeros((1, 2 * SSM_STATE), F32)
    for b in range(BATCH):
        ctx_rows = [N_CHUNK_LAT + b * n_ctx + c for c in range(n_ctx)]
        lat_rows = [b * n_lat + k for k in range(n_lat)]
        hf, hf_sw, hr, hr_sw = zero, zero, zero, zero
        for rf, rr in zip(ctx_rows + lat_rows, ctx_rows[::-1] + lat_rows[::-1]):
            h_scr[rf:rf + 1, 0:LANES] = hf
            hf, hf_sw = advance(hf, hf_sw, 0, rf)
            h_scr[rr:rr + 1, LANES:2 * LANES] = hr
            hr, hr_sw = advance(hr, hr_sw, 1, rr)

    y = y + lax.dot_general(h_scr[...].astype(BF16), wct_scr[...], NT_DIMS,
                            preferred_element_type=F32)
    y = y + u32 * dv_ref[0]
    for c in range(H):
        y_ref[c] = y[:, c * T:(c + 1) * T]


def _ssm_call(u_t, pta, ptb, rows, cc, at, dv):
    g3 = lambda g: (g, 0, 0)
    return pl.pallas_call(
        _ssm_kernel,
        grid=(N_SSM_GROUPS,),
        in_specs=[
            pl.BlockSpec((SSM_GROUP, N_CHUNK, SSM_T), g3),
            pl.BlockSpec((1, PT_ROWS, 4 * SSM_STATE), g3),
            pl.BlockSpec((1, PT_ROWS, 4 * SSM_STATE), g3),
            pl.BlockSpec((1, 4 * SSM_GROUP, 4 * SSM_STATE), g3),
            pl.BlockSpec((1, SSM_GROUP, 4 * SSM_STATE), g3),
            pl.BlockSpec((1, 4, 2 * SSM_STATE), g3),
            pl.BlockSpec((1, 1, SSM_K), g3),
        ],
        out_specs=pl.BlockSpec((SSM_GROUP, N_CHUNK, SSM_T), g3),
        out_shape=jax.ShapeDtypeStruct((D_SSM, N_CHUNK, SSM_T), F32),
        scratch_shapes=[
            pltpu.VMEM((SSM_GROUP * 2 * SSM_T, 4 * SSM_STATE), BF16),
            pltpu.VMEM((SSM_GROUP * 2 * SSM_T, 4 * SSM_STATE), BF16),
            pltpu.VMEM((SSM_GROUP, SSM_GROUP * 2 * SSM_T), F32),
            pltpu.VMEM((SSM_K, SSM_K), BF16),
            pltpu.VMEM((SSM_K, 4 * SSM_STATE), BF16),
            pltpu.VMEM((SSM_K, 4 * SSM_STATE), BF16),
            pltpu.VMEM((N_CHUNK, 8 * SSM_STATE), F32),
            pltpu.VMEM((N_CHUNK, 4 * SSM_STATE), F32),
        ],
        compiler_params=_cparams("parallel"),
        name="s5_mixer",
    )(u_t, pta, ptb, rows, cc, at, dv)


def _ssm_tables(a_re, a_im, log_dt, b_re, b_im, c_re, c_im, d_skip):
    G, P, H, T = N_SSM_GROUPS, SSM_STATE, SSM_GROUP, SSM_T
    a_re, a_im = a_re.astype(F32), a_im.astype(F32)
    dt = jnp.exp(log_dt.astype(F32))[..., None]
    den = a_re * a_re + a_im * a_im
    mag1 = jnp.exp(dt * a_re)
    ab_re, ab_im = mag1 * jnp.cos(dt * a_im), mag1 * jnp.sin(dt * a_im)
    num_re = ab_re - 1.0
    f_re = (num_re * a_re + ab_im * a_im) / den
    f_im = (ab_im * a_re - num_re * a_im) / den
    b_re, b_im = b_re.astype(F32), b_im.astype(F32)
    bb_re = f_re[..., None] * b_re - f_im[..., None] * b_im
    bb_im = f_re[..., None] * b_im + f_im[..., None] * b_re
    c_re, c_im = c_re.astype(F32), c_im.astype(F32)
    la, th = dt * a_re, dt * a_im

    def powers(d, n):
        mag = jnp.exp(la[d][:, None, :] * n[None, :, None])
        ph = th[d][:, None, :] * n[None, :, None]
        return mag * jnp.cos(ph), mag * jnp.sin(ph)

    j = jnp.arange(2 * T)
    s_idx = jnp.arange(T)
    f_exp = jnp.concatenate([jnp.where(j < T, j, 0), T - 1 - s_idx, s_idx + 1]).astype(F32)
    r_exp = jnp.concatenate([jnp.where(j > T, 2 * T - j, 0), s_idx, T - s_idx]).astype(F32)
    ones = jnp.ones((2 * T,), F32)
    f_on = jnp.concatenate([(j < T).astype(F32), ones])[None, :, None]
    r_on = jnp.concatenate([((j == 0) | (j > T)).astype(F32), ones])[None, :, None]
    f_re_n, f_im_n = powers(0, f_exp)
    r_re_n, r_im_n = powers(1, r_exp)
    f_re_n, f_im_n, r_re_n, r_im_n = f_re_n * f_on, f_im_n * f_on, r_re_n * r_on, r_im_n * r_on
    pta = jnp.concatenate([f_re_n, f_im_n, r_re_n, r_im_n], axis=-1)
    ptb = jnp.concatenate([-f_im_n, f_re_n, -r_im_n, r_re_n], axis=-1)

    def per_channel(v):
        f, r = v[0].transpose(0, 2, 1), v[1].transpose(0, 2, 1)
        return jnp.concatenate([f, f, r, r], axis=-1)
    cf_re, cr_re = c_re[0], c_re[1]
    cf_im, cr_im = c_im[0], c_im[1]
    rows = jnp.concatenate([
        per_channel(bb_re), per_channel(bb_im),
        jnp.concatenate([cf_re, -cf_re, cr_re, -cr_re], axis=-1),
        jnp.concatenate([cf_im, -cf_im, cr_im, -cr_im], axis=-1)], axis=1)
    cc = jnp.concatenate([cf_re, -cf_im, cr_re, -cr_im], axis=-1)

    t_exp = jnp.full((1,), float(T), F32)
    (f_re_t, f_im_t), (r_re_t, r_im_t) = powers(0, t_exp), powers(1, t_exp)
    f_re_t, f_im_t, r_re_t, r_im_t = (v[:, 0, :] for v in (f_re_t, f_im_t, r_re_t, r_im_t))
    at = jnp.stack([jnp.concatenate([f_re_t, f_re_t], -1), jnp.concatenate([-f_im_t, f_im_t], -1),
                    jnp.concatenate([r_re_t, r_re_t], -1), jnp.concatenate([-r_im_t, r_im_t], -1)],
                   axis=1)
    dv = jnp.repeat(d_skip.astype(F32).reshape(G, H), T, axis=-1).reshape(G, 1, H * T)
    return pta, ptb, rows, cc, at, dv


def _top2_gates(logits):
    lane = lax.broadcasted_iota(jnp.int32, logits.shape, 1)
    valid = lane < N_EXPERTS
    lg = jnp.where(valid, logits, -jnp.inf)
    m1 = jnp.max(lg, axis=-1, keepdims=True)
    i1 = jnp.min(jnp.where(lg == m1, lane, LANES), axis=-1, keepdims=True)
    lg2 = jnp.where(lane == i1, -jnp.inf, lg)
    m2 = jnp.max(lg2, axis=-1, keepdims=True)
    i2 = jnp.min(jnp.where(lg2 == m2, lane, LANES), axis=-1, keepdims=True)
    e2 = jnp.exp(m2 - m1)
    w1 = 1.0 / (1.0 + e2)
    return jnp.where(lane == i1, w1, 0.0) + jnp.where(lane == i2, e2 * w1, 0.0)


def _mix_kernel(*refs, with_router, n_x):
    rest = refs[n_x:]
    if with_router:
        (yst_ref, ya_ref, mod_ref, wglut_ref, bglu_ref, wout_ref, gffn_ref, router_ref,
         x1_ref, h2_ref, gate_ref) = rest
    else:
        (yst_ref, ya_ref, mod_ref, wglut_ref, bglu_ref, wout_ref, gffn_ref,
         x1_ref, h2_ref) = rest
    zt = jax.nn.gelu(yst_ref[...], approximate=True)
    glt = zt * jax.nn.sigmoid(
        jnp.dot(wglut_ref[...], zt.astype(BF16), preferred_element_type=F32) + bglu_ref[...])
    mix = jnp.concatenate([glt.T.astype(BF16), ya_ref[...]], axis=-1)
    o = jnp.dot(mix, wout_ref[...], preferred_element_type=F32)
    x1 = _tile_rows(refs, n_x) + mod_ref[0, 2:3, :] * o
    x1_ref[...] = x1
    h2 = _rms(x1, gffn_ref[...]) * (1.0 + mod_ref[0, 4:5, :]) + mod_ref[0, 3:4, :]
    h2_ref[...] = h2.astype(h2_ref.dtype)
    if with_router:
        h_hi = h2.astype(BF16)
        h_lo = (h2 - h_hi.astype(F32)).astype(BF16)
        o1 = jnp.dot(h_hi, router_ref[0], preferred_element_type=F32)
        o2 = jnp.dot(h_lo, router_ref[1], preferred_element_type=F32)
        logits = o1 + pltpu.roll(o1, LANES - N_EXPERTS, 1) + o2
        gate_ref[...] = _top2_gates(logits)


def _mix_call(xs, y_ssm_t, y_att, mod, wglut, bglu, wout, gffn, router, *, n_tiles):
    row = lambda i: (i, 0)
    out_row = row
    with_router = router is not None
    in_specs = _x_specs(len(xs)) + [
        pl.BlockSpec((D_SSM, TM), lambda i: (0, i)),
        pl.BlockSpec((TM, D_ATT), row),
        pl.BlockSpec((1, 6, D_MODEL), lambda i: (_mod_row(i), 0, 0)),
        _const_spec((D_SSM, D_SSM)),
        _const_spec((D_SSM, 1)),
        _const_spec((D_MIX, D_MODEL)),
        _const_spec((1, D_MODEL)),
    ]
    args = [*xs, y_ssm_t, y_att, mod, wglut, bglu, wout, gffn]
    out_specs = [pl.BlockSpec((TM, D_MODEL), out_row), pl.BlockSpec((TM, D_MODEL), out_row)]
    out_shape = [jax.ShapeDtypeStruct((n_tiles * TM, D_MODEL), F32),
                 jax.ShapeDtypeStruct((n_tiles * TM, D_MODEL), BF16)]
    if with_router:
        in_specs.append(_const_spec((2, D_MODEL, LANES)))
        args.append(router)
        out_specs.append(pl.BlockSpec((TM, LANES), out_row))
        out_shape.append(jax.ShapeDtypeStruct((n_tiles * TM, LANES), F32))
    return pl.pallas_call(
        functools.partial(_mix_kernel, with_router=with_router, n_x=len(xs)),
        grid=(n_tiles,),
        in_specs=in_specs,
        out_specs=out_specs,
        out_shape=out_shape,
        compiler_params=_cparams("parallel"),
        name="mix_out",
    )(*args)


def _swiglu(h, w1_at, w3_at, w2_at, d_ff):
    acc = jnp.zeros((h.shape[0], D_MODEL), F32)
    for lo in range(0, d_ff, FF_CHUNK):
        sl = slice(lo, min(lo + FF_CHUNK, d_ff))
        a = jnp.dot(h, w1_at(sl), preferred_element_type=F32)
        b = jnp.dot(h, w3_at(sl), preferred_element_type=F32)
        g = (a * jax.nn.sigmoid(a) * b).astype(BF16)
        acc = acc + jnp.dot(g, w2_at(sl), preferred_element_type=F32)
    return acc


def _ffn_kernel(h_ref, x1_ref, mod_ref, w1_ref, w3_ref, w2_ref, o_ref):
    acc = _swiglu(h_ref[...], lambda sl: w1_ref[:, sl], lambda sl: w3_ref[:, sl],
                  lambda sl: w2_ref[sl, :], D_FF)
    o_ref[...] = x1_ref[...] + mod_ref[0, 5:6, :] * acc


def _ffn_call(h2, x1, mod, w1, w3, w2):
    n_tiles = N_TOT // TM
    row = lambda i: (i, 0)
    return pl.pallas_call(
        _ffn_kernel,
        grid=(n_tiles,),
        in_specs=[
            pl.BlockSpec((TM, D_MODEL), row),
            pl.BlockSpec((TM, D_MODEL), row),
            pl.BlockSpec((1, 6, D_MODEL), lambda i: (_mod_row(i), 0, 0)),
            _const_spec((D_MODEL, D_FF)),
            _const_spec((D_MODEL, D_FF)),
            _const_spec((D_FF, D_MODEL)),
        ],
        out_specs=pl.BlockSpec((TM, D_MODEL), row),
        out_shape=jax.ShapeDtypeStruct((N_TOT, D_MODEL), F32),
        compiler_params=_cparams("parallel"),
        name="ffn_dense",
    )(h2, x1, mod, w1, w3, w2)


def _moe_kernel(h_ref, x1_ref, mod_ref, gate_ref, w1_ref, w3_ref, w2_ref, fg_ref, o_ref, acc_scr):
    e = pl.program_id(1)

    @pl.when(e == 0)
    def _():
        acc_scr[...] = jnp.zeros_like(acc_scr)

    y = _swiglu(h_ref[...], lambda sl: w1_ref[0, :, sl], lambda sl: w3_ref[0, :, sl],
                lambda sl: w2_ref[0, sl, :], D_FF_EXPERT)
    lane = lax.broadcasted_iota(jnp.int32, (TM, LANES), 1)
    gate_e = jnp.sum(jnp.where(lane == e, gate_ref[...], 0.0), axis=-1, keepdims=True)
    acc_scr[...] += gate_e * y

    @pl.when(e == N_EXPERTS - 1)
    def _():
        x2 = x1_ref[...] + mod_ref[0, 5:6, :] * acc_scr[...]
        o_ref[...] = _rms(x2, fg_ref[...])


def _moe_call(h2, x1, mod, gate, w1, w3, w2, fg):
    n_tiles = N_LAT // TM
    row = lambda i, e: (i, 0)
    return pl.pallas_call(
        _moe_kernel,
        grid=(n_tiles, N_EXPERTS),
        in_specs=[
            pl.BlockSpec((TM, D_MODEL), row),
            pl.BlockSpec((TM, D_MODEL), row),
            pl.BlockSpec((1, 6, D_MODEL), lambda i, e: (i // SEQ_TILES, 0, 0)),
            pl.BlockSpec((TM, LANES), row),
            pl.BlockSpec((1, D_MODEL, D_FF_EXPERT), lambda i, e: (e, 0, 0)),
            pl.BlockSpec((1, D_MODEL, D_FF_EXPERT), lambda i, e: (e, 0, 0)),
            pl.BlockSpec((1, D_FF_EXPERT, D_MODEL), lambda i, e: (e, 0, 0)),
            pl.BlockSpec((1, D_MODEL), lambda i, e: (0, 0)),
        ],
        out_specs=pl.BlockSpec((TM, D_MODEL), row),
        out_shape=jax.ShapeDtypeStruct((N_LAT, D_MODEL), F32),
        scratch_shapes=[pltpu.VMEM((TM, D_MODEL), F32)],
        compiler_params=_cparams("parallel", "arbitrary"),
        name="moe_experts",
    )(h2, x1, mod, gate, w1, w3, w2, fg)


def _rope_partner_perm():
    perm, sign = [], []
    for j in range(QK_ROPE):
        first_half = (j % AXIS_ROPE) < ROPE_FREQS
        perm.append(j + ROPE_FREQS if first_half else j - ROPE_FREQS)
        sign.append(-1.0 if first_half else 1.0)
    return jnp.array(perm, jnp.int32), jnp.array(sign, F32)


def _rope_tables():
    t = jnp.arange(SEQ)
    row = (t // GRID_W).astype(F32)
    col = (t % GRID_W).astype(F32)
    inv_freq = ROPE_BASE ** (-2.0 * jnp.arange(ROPE_FREQS, dtype=F32) / AXIS_ROPE)
    ang = jnp.concatenate([row[:, None] * inv_freq, row[:, None] * inv_freq,
                           col[:, None] * inv_freq, col[:, None] * inv_freq], axis=1)
    cos = jnp.concatenate([jnp.cos(ang), jnp.ones((N_CTX, QK_ROPE), F32)], axis=0)
    sin = jnp.concatenate([jnp.sin(ang), jnp.zeros((N_CTX, QK_ROPE), F32)], axis=0)
    n = N_CTX + SEQ
    pad32 = jnp.zeros((n, HEAD_PAD - D_QK), F32)
    qs = ATT_SCALE * math.log2(math.e)
    cq = jnp.concatenate([jnp.full((n, QK_NOPE), qs, F32), qs * cos, pad32], axis=1)
    sq = jnp.concatenate([jnp.zeros((n, QK_NOPE), F32), qs * sin, pad32], axis=1)
    cs = jnp.concatenate([cos, sin, jnp.zeros((n, LANES - 2 * QK_ROPE), F32)], axis=1)
    return cq, sq, cs


def _layer_weights(w_in, w_uq, w_ukv):
    perm, sign = _rope_partner_perm()
    s0 = D_SSM + Q_LORA + KV_LORA
    kr_w = w_in[:, s0:s0 + QK_ROPE]
    wut = w_in[:, :D_SSM].T.astype(BF16)
    win = jnp.concatenate([w_in[:, D_SSM:s0], kr_w, kr_w[:, perm] * sign,
                           jnp.zeros((D_MODEL, LANES - 2 * QK_ROPE), F32)], axis=1).astype(BF16)
    uq = w_uq.reshape(Q_LORA, N_HEADS, D_QK)
    nope, rope = uq[..., :QK_NOPE], uq[..., QK_NOPE:]
    zpad = jnp.zeros((Q_LORA, N_HEADS, HEAD_PAD - D_QK), F32)
    wq1 = jnp.concatenate([nope, rope, zpad], axis=-1).reshape(Q_LORA, N_HEADS * HEAD_PAD).astype(BF16)
    wq2 = jnp.concatenate([jnp.zeros_like(nope), rope[..., perm] * sign, zpad], axis=-1)
    wq2 = wq2.reshape(Q_LORA, N_HEADS * HEAD_PAD).astype(BF16)
    ukv = w_ukv.reshape(KV_LORA, N_HEADS, QK_NOPE + V_HEAD)
    wk = jnp.concatenate([ukv[..., :QK_NOPE], jnp.zeros((KV_LORA, N_HEADS, HEAD_PAD - QK_NOPE), F32)],
                         axis=-1).reshape(KV_LORA, N_HEADS * HEAD_PAD)
    eye = jnp.eye(QK_ROPE, dtype=F32)
    place = jnp.concatenate([jnp.zeros((QK_ROPE, QK_NOPE), F32), eye,
                             jnp.zeros((QK_ROPE, HEAD_PAD - D_QK), F32)], axis=1)
    place = jnp.tile(place, (1, N_HEADS))
    spread = jnp.concatenate([place, place, jnp.zeros((LANES - 2 * QK_ROPE, N_HEADS * HEAD_PAD), F32)], 0)
    wkk = jnp.concatenate([wk, spread], axis=0).astype(BF16)
    wv = jnp.concatenate([ukv[..., QK_NOPE:], jnp.zeros((KV_LORA, N_HEADS, HEAD_PAD - V_HEAD), F32)],
                         axis=-1).reshape(KV_LORA, N_HEADS * HEAD_PAD).astype(BF16)
    return wut, win, wq1, wq2, wkk, wv


def kernel(x, c, ctx, c_ctx, w_ada, b_ada, norm_mix, norm_ffn, w_in, q_norm, kv_norm, w_uq, w_ukv,
           ssm_a_re, ssm_a_im, ssm_log_dt, ssm_b_re, ssm_b_im, ssm_c_re, ssm_c_im, ssm_d, w_glu,
           b_glu, w_out, ffn_w1, ffn_w3, ffn_w2, moe_router, moe_w1, moe_w3, moe_w2, final_norm):
    assert x.shape == (BATCH, SEQ, D_MODEL) and ctx.shape == (BATCH, CTX_LEN, D_MODEL)
    cond = jnp.concatenate([c, c_ctx[None, :], jnp.zeros((MOD_ROWS - BATCH - 1, D_MODEL), F32)], axis=0)
    mod_all = _ada_call(cond, w_ada, b_ada).reshape(DEPTH, MOD_ROWS, 6, D_MODEL)
    cq_t, sq_t, cs_t = _rope_tables()
    xs = (x.reshape(N_LAT, D_MODEL), ctx.reshape(N_CTX, D_MODEL))

    out = None
    for i in range(DEPTH):
        last = i == DEPTH - 1
        mod = mod_all[i]
        wut, win, wq1, wq2, wkk, wv = _layer_weights(w_in[i], w_uq[i], w_ukv[i])
        u_t, q, k, v = _inproj_call(xs, mod, norm_mix[i][None, :], wut, win, q_norm[i][None, :],
                                    kv_norm[i][None, :], wq1, wq2, wkk, wv, cq_t, sq_t, cs_t)
        tabs = _ssm_tables(ssm_a_re[i], ssm_a_im[i], ssm_log_dt[i], ssm_b_re[i], ssm_b_im[i],
                           ssm_c_re[i], ssm_c_im[i], ssm_d[i])
        y_ssm_t = _ssm_call(u_t.reshape(D_SSM, N_CHUNK, SSM_T), *tabs).reshape(D_SSM, N_TOT)
        y_att = _attn_latent_call(q, k, v)
        if last:
            n_tiles = LAT_TILES
        else:
            y_att = _attn_ctx_call(q, k, v, y_att)
            n_tiles = N_TOT // TM
        j = i // 2
        if i % 2 == 0:
            router = None
        else:
            r = moe_router[j]
            r_top = lax.bitcast_convert_type(
                lax.bitcast_convert_type(r, jnp.uint32) & jnp.uint32(0xFFFF0000), F32)
            r_hi = r_top.astype(BF16)
            r_lo = (r - r_top).astype(BF16)
            zr = jnp.zeros((D_MODEL, LANES - 2 * N_EXPERTS), BF16)
            router = jnp.stack([jnp.concatenate([r_hi, r_lo, zr], axis=1),
                                jnp.concatenate([r_hi, jnp.zeros_like(r_lo), zr], axis=1)])
        res = _mix_call(xs, y_ssm_t, y_att, mod, w_glu[i].T.astype(BF16), b_glu[i][:, None],
                        w_out[i].astype(BF16), norm_ffn[i][None, :], router, n_tiles=n_tiles)
        if i % 2 == 0:
            assert not last
            x1, h2 = res
            xs = (_ffn_call(h2, x1, mod, ffn_w1[j].astype(BF16), ffn_w3[j].astype(BF16),
                            ffn_w2[j].astype(BF16)),)
        else:
            assert last
            x1, h2, gate = res
            out = _moe_call(h2, x1, mod, gate, moe_w1[j].astype(BF16), moe_w3[j].astype(BF16),
                            moe_w2[j].astype(BF16), final_norm[None, :])
    return out.reshape(BATCH, SEQ, D_MODEL)
```

```python
import functools
import math

import jax
import jax.numpy as jnp
from jax import lax
from jax.experimental import pallas as pl
from jax.experimental.pallas import tpu as pltpu
from jax.experimental.pallas import tpu_sc as plsc

D_MODEL = 1024
BATCH = 4
SEQ = 8192
DEPTH = 2
GRID_W = 64
CTX_LEN = 256
D_SSM = 512
SSM_GROUP = 16
N_SSM_GROUPS = D_SSM // SSM_GROUP
SSM_STATE = 64
N_HEADS = 8
QK_NOPE = 64
QK_ROPE = 32
V_HEAD = 64
Q_LORA = 256
KV_LORA = 128
D_QK = QK_NOPE + QK_ROPE
D_ATT = N_HEADS * V_HEAD
D_MIX = D_SSM + D_ATT
D_IN = D_SSM + Q_LORA + KV_LORA + QK_ROPE
AXIS_ROPE = QK_ROPE // 2
ROPE_FREQS = AXIS_ROPE // 2
ROPE_BASE = 10000.0
ATT_SCALE = 1.0 / math.sqrt(D_QK)
D_FF = 2816
N_EXPERTS = 8
D_FF_EXPERT = 1408
EPS = 1e-6

N_CTX = BATCH * CTX_LEN
N_LAT = BATCH * SEQ
N_TOT = N_CTX + N_LAT

LANES = 128
HEAD_PAD = 128
TM = 512
LAT_TILES = N_LAT // TM
SEQ_TILES = SEQ // TM
TQ = 512
TK = 2048
SSM_T = 128
N_CHUNK_LAT = N_LAT // SSM_T
N_CHUNK = N_TOT // SSM_T
SSM_K = SSM_GROUP * SSM_T
FF_CHUNK = 256
N_FF_CHUNKS = D_FF // FF_CHUNK
D_IN_REST = Q_LORA + KV_LORA + LANES
MOD_ROWS = 8
VMEM_LIMIT = 52 * 1024 * 1024

F32 = jnp.float32
BF16 = jnp.bfloat16
HI = lax.Precision.HIGHEST


def _cparams(*sem):
    return pltpu.CompilerParams(dimension_semantics=sem, vmem_limit_bytes=VMEM_LIMIT)


def _const_spec(shape):
    nd = len(shape)
    return pl.BlockSpec(shape, lambda *_: (0,) * nd, pipeline_mode=pl.Buffered(1))


def _mod_row(i):
    return jnp.where(i < LAT_TILES, i // SEQ_TILES, BATCH)


def _pos_tile(i):
    return jnp.where(i < LAT_TILES, i % SEQ_TILES, SEQ_TILES + i - LAT_TILES)


def _rms(x, g):
    ms = jnp.mean(x * x, axis=-1, keepdims=True)
    return x * lax.rsqrt(ms + EPS) * g


ADA_TN = 1536


def _ada_kernel(c_ref, w_ref, b_ref, o_ref):
    c = c_ref[...]
    s = c * jax.nn.sigmoid(c)
    o_ref[0] = jnp.dot(s, w_ref[0], precision=HI, preferred_element_type=F32) + b_ref[0]


def _ada_call(cond, w_ada, b_ada):
    n_col = 6 * D_MODEL // ADA_TN
    return pl.pallas_call(
        _ada_kernel,
        grid=(DEPTH, n_col),
        in_specs=[
            pl.BlockSpec((MOD_ROWS, D_MODEL), lambda l, j: (0, 0)),
            pl.BlockSpec((1, D_MODEL, ADA_TN), lambda l, j: (l, 0, j)),
            pl.BlockSpec((1, 1, ADA_TN), lambda l, j: (l, 0, j)),
        ],
        out_specs=pl.BlockSpec((1, MOD_ROWS, ADA_TN), lambda l, j: (l, 0, j)),
        out_shape=jax.ShapeDtypeStruct((DEPTH, MOD_ROWS, 6 * D_MODEL), F32),
        compiler_params=_cparams("arbitrary", "arbitrary"),
        name="ada_mod",
    )(cond, w_ada, b_ada.reshape(DEPTH, 1, 6 * D_MODEL))


def _tile_rows(refs, n_x):
    if n_x == 1:
        return refs[0][...]
    return jnp.where(pl.program_id(0) < LAT_TILES, refs[0][...], refs[1][...])


def _x_specs(n_x):
    if n_x == 1:
        return [pl.BlockSpec((TM, D_MODEL), lambda i: (i, 0))]
    return [pl.BlockSpec((TM, D_MODEL), lambda i: (jnp.minimum(i, LAT_TILES - 1), 0)),
            pl.BlockSpec((TM, D_MODEL), lambda i: (jnp.maximum(i - LAT_TILES, 0), 0))]


def _inproj_kernel(*refs, n_x):
    (mod_ref, g_ref, wut_ref, win_ref, qg_ref, kvg_ref, wq1_ref, wq2_ref, wkk_ref, wv_ref,
     cq_ref, sq_ref, cs_ref, ut_ref, q_ref, k_ref, v_ref) = refs[n_x:]
    x = _tile_rows(refs, n_x)
    sh = mod_ref[0, 0:1, :]
    sc = mod_ref[0, 1:2, :]
    xm = (_rms(x, g_ref[...]) * (1.0 + sc) + sh).astype(BF16)
    ut_ref[...] = lax.dot_general(wut_ref[...], xm, (((1,), (1,)), ((), ())),
                                  preferred_element_type=F32)
    z = jnp.dot(xm, win_ref[...], preferred_element_type=F32)
    qn = _rms(z[:, :Q_LORA], qg_ref[...]).astype(BF16)
    kvn = _rms(z[:, Q_LORA:Q_LORA + KV_LORA], kvg_ref[...]).astype(BF16)
    krr = (z[:, Q_LORA + KV_LORA:] * cs_ref[...]).astype(BF16)
    q1 = jnp.dot(qn, wq1_ref[...], preferred_element_type=F32)
    q2 = jnp.dot(qn, wq2_ref[...], preferred_element_type=F32)
    cq = cq_ref[...]
    sq = sq_ref[...]
    for h in range(N_HEADS):
        sl = slice(h * HEAD_PAD, (h + 1) * HEAD_PAD)
        q_ref[:, sl] = (q1[:, sl] * cq + q2[:, sl] * sq).astype(q_ref.dtype)
    kin = jnp.concatenate([kvn, krr], axis=-1)
    k_ref[...] = jnp.dot(kin, wkk_ref[...], preferred_element_type=F32).astype(k_ref.dtype)
    vv = jnp.dot(kvn, wv_ref[...], preferred_element_type=F32)
    lane = lax.broadcasted_iota(jnp.int32, vv.shape, 1)
    v_ref[...] = jnp.where(lane % HEAD_PAD == V_HEAD, 1.0, vv).astype(v_ref.dtype)


def _inproj_call(xs, mod, g_mix, wut, win, qg, kvg, wq1, wq2, wkk, wv, cq_t, sq_t, cs_t):
    n_tiles = N_TOT // TM
    row = lambda i: (i, 0)
    pos = lambda i: (_pos_tile(i), 0)
    return pl.pallas_call(
        functools.partial(_inproj_kernel, n_x=len(xs)),
        grid=(n_tiles,),
        in_specs=_x_specs(len(xs)) + [
            pl.BlockSpec((1, 6, D_MODEL), lambda i: (_mod_row(i), 0, 0)),
            _const_spec((1, D_MODEL)),
            _const_spec((D_SSM, D_MODEL)),
            _const_spec((D_MODEL, D_IN_REST)),
            _const_spec((1, Q_LORA)),
            _const_spec((1, KV_LORA)),
            _const_spec((Q_LORA, N_HEADS * HEAD_PAD)),
            _const_spec((Q_LORA, N_HEADS * HEAD_PAD)),
            _const_spec((2 * KV_LORA, N_HEADS * HEAD_PAD)),
            _const_spec((KV_LORA, N_HEADS * HEAD_PAD)),
            pl.BlockSpec((TM, LANES), pos),
            pl.BlockSpec((TM, LANES), pos),
            pl.BlockSpec((TM, LANES), pos),
        ],
        out_specs=[
            pl.BlockSpec((D_SSM, TM), lambda i: (0, i)),
            pl.BlockSpec((TM, N_HEADS * HEAD_PAD), row),
            pl.BlockSpec((TM, N_HEADS * HEAD_PAD), row),
            pl.BlockSpec((TM, N_HEADS * HEAD_PAD), row),
        ],
        out_shape=[
            jax.ShapeDtypeStruct((D_SSM, N_TOT), F32),
            jax.ShapeDtypeStruct((N_TOT, N_HEADS * HEAD_PAD), BF16),
            jax.ShapeDtypeStruct((N_TOT, N_HEADS * HEAD_PAD), BF16),
            jax.ShapeDtypeStruct((N_TOT, N_HEADS * HEAD_PAD), BF16),
        ],
        compiler_params=_cparams("parallel"),
        name="in_proj",
    )(*xs, mod, g_mix, wut, win, qg, kvg, wq1, wq2, wkk, wv, cq_t, sq_t, cs_t)


def _attn_kernel(*refs, n_chunks, tq):
    if n_chunks:
        q_ref, k_ref, v_ref, kc_ref, vc_ref, o_ref, s_scr = refs
    else:
        q_ref, kc_ref, vc_ref, _, o_ref, s_scr = refs
    heads =[slice(hh * HEAD_PAD, (hh + 1) * HEAD_PAD) for hh in range(2)]
    qs = [q_ref[:, sl] for sl in heads]

    def put_scores(slot, k_at, width):
        for hh in range(2):
            s_scr[slot, hh, :, :width] = lax.dot_general(
                qs[hh], k_at(heads[hh]), (((1,), (1,)), ((), ())), preferred_element_type=F32)

    def consume(carry, slot, v_at, width):
        new = []
        for hh in range(2):
            m, acc = carry[hh]
            s = s_scr[slot, hh, :, :width]
            m_new = jnp.maximum(m, jnp.max(s, axis=-1, keepdims=True))
            alpha = jnp.exp2(m - m_new)
            p = jnp.exp2(s - m_new).astype(BF16)
            acc = alpha * acc + jnp.dot(p, v_at(heads[hh]), preferred_element_type=F32)
            new.append((m_new, acc))
        return tuple(new)

    def chunk(c):
        rows = pl.ds(pl.multiple_of(c * TK, TK), TK)
        return (lambda sl: k_ref[rows, sl]), (lambda sl: v_ref[rows, sl])

    ctx_k, ctx_v = (lambda sl: kc_ref[:, sl]), (lambda sl: vc_ref[:, sl])
    carry = tuple((jnp.full((tq, 1), -jnp.inf, F32), jnp.zeros((tq, HEAD_PAD), F32))
                  for _ in range(2))
    if n_chunks:
        assert n_chunks % 2 == 0
        put_scores(0, chunk(0)[0], TK)

        def body(j, carry):
            c0 = 2 * j
            put_scores(1, chunk(c0 + 1)[0], TK)
            carry = consume(carry, 0, chunk(c0)[1], TK)
            put_scores(0, chunk(c0 + 2)[0], TK)
            return consume(carry, 1, chunk(c0 + 1)[1], TK)

        carry = lax.fori_loop(0, n_chunks // 2 - 1, body, carry)
        put_scores(1, chunk(n_chunks - 1)[0], TK)
        carry = consume(carry, 0, chunk(n_chunks - 2)[1], TK)
        put_scores(0, ctx_k, CTX_LEN)
        carry = consume(carry, 1, chunk(n_chunks - 1)[1], TK)
    else:
        put_scores(0, ctx_k, CTX_LEN)
    carry = consume(carry, 0, ctx_v, CTX_LEN)
    outs = [acc[:, :V_HEAD] / acc[:, V_HEAD:V_HEAD + 1] for _, acc in carry]
    o_ref[...] = jnp.concatenate(outs, axis=-1).astype(o_ref.dtype)


def _attn_latent_call(q, k, v):
    qt = SEQ // TQ
    ctx0 = N_LAT // CTX_LEN
    return pl.pallas_call(
        functools.partial(_attn_kernel, n_chunks=SEQ // TK, tq=TQ),
        grid=(BATCH, N_HEADS // 2, qt),
        in_specs=[
            pl.BlockSpec((TQ, 2 * HEAD_PAD), lambda b, h, i: (b * qt + i, h)),
            pl.BlockSpec((SEQ, 2 * HEAD_PAD), lambda b, h, i: (b, h)),
            pl.BlockSpec((SEQ, 2 * HEAD_PAD), lambda b, h, i: (b, h)),
            pl.BlockSpec((CTX_LEN, 2 * HEAD_PAD), lambda b, h, i: (ctx0 + b, h)),
            pl.BlockSpec((CTX_LEN, 2 * HEAD_PAD), lambda b, h, i: (ctx0 + b, h)),
        ],
        out_specs=pl.BlockSpec((TQ, 2 * V_HEAD), lambda b, h, i: (b * qt + i, h)),
        out_shape=jax.ShapeDtypeStruct((N_TOT, D_ATT), BF16),
        scratch_shapes=[pltpu.VMEM((2, 2, TQ, TK), F32)],
        compiler_params=_cparams("parallel", "parallel", "arbitrary"),
        name="attn_latent",
    )(q, k, v, k, v)


def _attn_ctx_call(q, k, v, y_att):
    ctx0 = N_LAT // CTX_LEN
    return pl.pallas_call(
        functools.partial(_attn_kernel, n_chunks=0, tq=CTX_LEN),
        grid=(BATCH, N_HEADS // 2),
        in_specs=[
            pl.BlockSpec((CTX_LEN, 2 * HEAD_PAD), lambda b, h: (ctx0 + b, h)),
            pl.BlockSpec((CTX_LEN, 2 * HEAD_PAD), lambda b, h: (ctx0 + b, h)),
            pl.BlockSpec((CTX_LEN, 2 * HEAD_PAD), lambda b, h: (ctx0 + b, h)),
            pl.BlockSpec(memory_space=pl.ANY),
        ],
        out_specs=pl.BlockSpec((CTX_LEN, 2 * V_HEAD), lambda b, h: (ctx0 + b, h)),
        out_shape=jax.ShapeDtypeStruct((N_TOT, D_ATT), BF16),
        input_output_aliases={3: 0},
        scratch_shapes=[pltpu.VMEM((1, 2, CTX_LEN, CTX_LEN), F32)],
        compiler_params=_cparams("parallel", "parallel"),
        name="attn_ctx",
    )(q, k, v, y_att)


PT_LAG, PT_INC, PT_OUT = 0, 2 * SSM_T, 3 * SSM_T
PT_ROWS = 4 * SSM_T
NT_DIMS = (((1,), (1,)), ((), ()))


def _ssm_kernel(u_ref, pta_ref, ptb_ref, rows_ref, cc_ref, at_ref, dv_ref, y_ref,
                abt_hi_scr, abt_lo_scr, wl_scr, m_scr, ws_scr, wct_scr, s_scr, h_scr):
    H, T = SSM_GROUP, SSM_T

    def split(v):
        hi = v.astype(BF16)
        return hi, (v - hi.astype(F32)).astype(BF16)

    def scaled(row0, n_rows, ra, rb):
        return pta_ref[0, row0:row0 + n_rows, :] * ra + ptb_ref[0, row0:row0 + n_rows, :] * rb

    def build_tables(i, _):
        row0 = pl.multiple_of(i * T, T)
        b_re = rows_ref[0, pl.ds(i, 1), :]
        b_im = rows_ref[0, pl.ds(H + i, 1), :]
        lag_rows = pl.ds(pl.multiple_of(i * 2 * T, 2 * T), 2 * T)
        abt_hi_scr[lag_rows, :], abt_lo_scr[lag_rows, :] = split(scaled(PT_LAG, 2 * T, b_re, b_im))
        ws_scr[pl.ds(row0, T), :] = scaled(PT_INC, T, b_re, b_im).astype(BF16)
        c_re = rows_ref[0, pl.ds(2 * H + i, 1), :]
        c_im = rows_ref[0, pl.ds(3 * H + i, 1), :]
        wct_scr[pl.ds(row0, T), :] = scaled(PT_OUT, T, c_re, c_im).astype(BF16)
        return 0

    lax.fori_loop(0, H, build_tables, 0)
    cc_hi, cc_lo = split(cc_ref[0])
    nt = lambda a, b: lax.dot_general(a, b, NT_DIMS, preferred_element_type=F32)
    wl_scr[...] = (nt(cc_hi, abt_hi_scr[...]) + nt(cc_hi, abt_lo_scr[...])
                   + nt(cc_lo, abt_hi_scr[...]))

    def build_toeplitz(ci, _):
        row0 = pl.multiple_of(ci * T, T)
        lane0 = pl.multiple_of(ci * 2 * T, 2 * T)
        for co in range(H):
            wb = jnp.broadcast_to(wl_scr[co:co + 1, pl.ds(lane0, 2 * T)], (T, 2 * T))
            toe = pltpu.roll(wb, 0, 1, stride=1, stride_axis=0)
            m_scr[pl.ds(row0, T), co * T:(co + 1) * T] = toe[:, :T].astype(BF16)
        return 0

    lax.fori_loop(0, H, build_toeplitz, 0)

    u32 = jnp.concatenate([u_ref[c] for c in range(H)], axis=1)
    u = u32.astype(BF16)
    y = jnp.dot(u, m_scr[...], preferred_element_type=F32)
    s = jnp.dot(u, ws_scr[...], preferred_element_type=F32)
    s_scr[:, 0:2 * LANES] = s
    s_scr[:, 2 * LANES:3 * LANES] = pltpu.roll(s[:, 0:LANES], SSM_STATE, 1)
    s_scr[:, 3 * LANES:4 * LANES] = pltpu.roll(s[:, LANES:2 * LANES], SSM_STATE, 1)

    def advance(h, h_sw, d, row):
        a0 = at_ref[0, 2 * d:2 * d + 1, :]
        a1 = at_ref[0, 2 * d + 1:2 * d + 2, :]
        lanes = slice(d * LANES, (d + 1) * LANES)
        lanes_sw = slice((2 + d) * LANES, (3 + d) * LANES)
        return (h * a0 + h_sw * a1 + s_scr[row:row + 1, lanes],
                h_sw * a0 - h * a1 + s_scr[row:row + 1, lanes_sw])

    n_lat, n_ctx = SEQ // T, CTX_LEN // T
    zero = jnp.zeros((1, 2 * SSM_STATE), F32)
    for b in range(BATCH):
        ctx_rows = [N_CHUNK_LAT + b * n_ctx + c for c in range(n_ctx)]
        lat_rows = [b * n_lat + k for k in range(n_lat)]
        hf, hf_sw, hr, hr_sw = zero, zero, zero, zero
        for rf, rr in zip(ctx_rows + lat_rows, ctx_rows[::-1] + lat_rows[::-1]):
            h_scr[rf:rf + 1, 0:LANES] = hf
            hf, hf_sw = advance(hf, hf_sw, 0, rf)
            h_scr[rr:rr + 1, LANES:2 * LANES] = hr
            hr, hr_sw = advance(hr, hr_sw, 1, rr)

    y = y + lax.dot_general(h_scr[...].astype(BF16), wct_scr[...], NT_DIMS,
                            preferred_element_type=F32)
    y = y + u32 * dv_ref[0]
    for c in range(H):
        y_ref[c] = y[:, c * T:(c + 1) * T]


def _ssm_call(u_t, pta, ptb, rows, cc, at, dv):
    g3 = lambda g: (g, 0, 0)
    return pl.pallas_call(
        _ssm_kernel,
        grid=(N_SSM_GROUPS,),
        in_specs=[
            pl.BlockSpec((SSM_GROUP, N_CHUNK, SSM_T), g3),
            pl.BlockSpec((1, PT_ROWS, 4 * SSM_STATE), g3),
            pl.BlockSpec((1, PT_ROWS, 4 * SSM_STATE), g3),
            pl.BlockSpec((1, 4 * SSM_GROUP, 4 * SSM_STATE), g3),
            pl.BlockSpec((1, SSM_GROUP, 4 * SSM_STATE), g3),
            pl.BlockSpec((1, 4, 2 * SSM_STATE), g3),
            pl.BlockSpec((1, 1, SSM_K), g3),
        ],
        out_specs=pl.BlockSpec((SSM_GROUP, N_CHUNK, SSM_T), g3),
        out_shape=jax.ShapeDtypeStruct((D_SSM, N_CHUNK, SSM_T), F32),
        scratch_shapes=[
            pltpu.VMEM((SSM_GROUP * 2 * SSM_T, 4 * SSM_STATE), BF16),
            pltpu.VMEM((SSM_GROUP * 2 * SSM_T, 4 * SSM_STATE), BF16),
            pltpu.VMEM((SSM_GROUP, SSM_GROUP * 2 * SSM_T), F32),
            pltpu.VMEM((SSM_K, SSM_K), BF16),
            pltpu.VMEM((SSM_K, 4 * SSM_STATE), BF16),
            pltpu.VMEM((SSM_K, 4 * SSM_STATE), BF16),
            pltpu.VMEM((N_CHUNK, 8 * SSM_STATE), F32),
            pltpu.VMEM((N_CHUNK, 4 * SSM_STATE), F32),
        ],
        compiler_params=_cparams("parallel"),
        name="s5_mixer",
    )(u_t, pta, ptb, rows, cc, at, dv)


def _ssm_tables(a_re, a_im, log_dt, b_re, b_im, c_re, c_im, d_skip):
    G, P, H, T = N_SSM_GROUPS, SSM_STATE, SSM_GROUP, SSM_T
    a_re, a_im = a_re.astype(F32), a_im.astype(F32)
    dt = jnp.exp(log_dt.astype(F32))[..., None]
    den = a_re * a_re + a_im * a_im
    mag1 = jnp.exp(dt * a_re)
    ab_re, ab_im = mag1 * jnp.cos(dt * a_im), mag1 * jnp.sin(dt * a_im)
    num_re = ab_re - 1.0
    f_re = (num_re * a_re + ab_im * a_im) / den
    f_im = (ab_im * a_re - num_re * a_im) / den
    b_re, b_im = b_re.astype(F32), b_im.astype(F32)
    bb_re = f_re[..., None] * b_re - f_im[..., None] * b_im
    bb_im = f_re[..., None] * b_im + f_im[..., None] * b_re
    c_re, c_im = c_re.astype(F32), c_im.astype(F32)
    la, th = dt * a_re, dt * a_im

    def powers(d, n):
        mag = jnp.exp(la[d][:, None, :] * n[None, :, None])
        ph = th[d][:, None, :] * n[None, :, None]
        return mag * jnp.cos(ph), mag * jnp.sin(ph)

    j = jnp.arange(2 * T)
    s_idx = jnp.arange(T)
    f_exp = jnp.concatenate([jnp.where(j < T, j, 0), T - 1 - s_idx, s_idx + 1]).astype(F32)
    r_exp = jnp.concatenate([jnp.where(j > T, 2 * T - j, 0), s_idx, T - s_idx]).astype(F32)
    ones = jnp.ones((2 * T,), F32)
    f_on = jnp.concatenate([(j < T).astype(F32), ones])[None, :, None]
    r_on = jnp.concatenate([((j == 0) | (j > T)).astype(F32), ones])[None, :, None]
    f_re_n, f_im_n = powers(0, f_exp)
    r_re_n, r_im_n = powers(1, r_exp)
    f_re_n, f_im_n, r_re_n, r_im_n = f_re_n * f_on, f_im_n * f_on, r_re_n * r_on, r_im_n * r_on
    pta = jnp.concatenate([f_re_n, f_im_n, r_re_n, r_im_n], axis=-1)
    ptb = jnp.concatenate([-f_im_n, f_re_n, -r_im_n, r_re_n], axis=-1)

    def per_channel(v):
        f, r = v[0].transpose(0, 2, 1), v[1].transpose(0, 2, 1)
        return jnp.concatenate([f, f, r, r], axis=-1)
    cf_re, cr_re = c_re[0], c_re[1]
    cf_im, cr_im = c_im[0], c_im[1]
    rows = jnp.concatenate([
        per_channel(bb_re), per_channel(bb_im),
        jnp.concatenate([cf_re, -cf_re, cr_re, -cr_re], axis=-1),
        jnp.concatenate([cf_im, -cf_im, cr_im, -cr_im], axis=-1)], axis=1)
    cc = jnp.concatenate([cf_re, -cf_im, cr_re, -cr_im], axis=-1)

    t_exp = jnp.full((1,), float(T), F32)
    (f_re_t, f_im_t), (r_re_t, r_im_t) = powers(0, t_exp), powers(1, t_exp)
    f_re_t, f_im_t, r_re_t, r_im_t = (v[:, 0, :] for v in (f_re_t, f_im_t, r_re_t, r_im_t))
    at = jnp.stack([jnp.concatenate([f_re_t, f_re_t], -1), jnp.concatenate([-f_im_t, f_im_t], -1),
                    jnp.concatenate([r_re_t, r_re_t], -1), jnp.concatenate([-r_im_t, r_im_t], -1)],
                   axis=1)
    dv = jnp.repeat(d_skip.astype(F32).reshape(G, H), T, axis=-1).reshape(G, 1, H * T)
    return pta, ptb, rows, cc, at, dv


ROUTE_E1, ROUTE_E2, ROUTE_W1, ROUTE_W2 = 0, 1, 2, 3


def _top2_route(logits):
    lane = lax.broadcasted_iota(jnp.int32, logits.shape, 1)
    lg = jnp.where(lane < N_EXPERTS, logits, -jnp.inf)
    m1 = jnp.max(lg, axis=-1, keepdims=True)
    i1 = jnp.min(jnp.where(lg == m1, lane, LANES), axis=-1, keepdims=True)
    lg2 = jnp.where(lane == i1, -jnp.inf, lg)
    m2 = jnp.max(lg2, axis=-1, keepdims=True)
    i2 = jnp.min(jnp.where(lg2 == m2, lane, LANES), axis=-1, keepdims=True)
    e2 = jnp.exp(m2 - m1)
    w1 = 1.0 / (1.0 + e2)
    rec = jnp.where(lane == ROUTE_E1, i1.astype(F32), 0.0)
    rec = jnp.where(lane == ROUTE_E2, i2.astype(F32), rec)
    rec = jnp.where(lane == ROUTE_W1, w1, rec)
    return jnp.where(lane == ROUTE_W2, e2 * w1, rec)


def _pack_bf16_pairs(v):
    k = v.shape[1] // 2
    bits = pltpu.bitcast(v.astype(BF16).astype(F32), jnp.uint32)
    return (bits[:, :k] & jnp.uint32(0xFFFF0000)) | (bits[:, k:] >> 16)


def _unpack_bf16_pairs(w):
    hi = pltpu.bitcast(w & jnp.uint32(0xFFFF0000), F32)
    lo = pltpu.bitcast(w << 16, F32)
    return hi, lo


def _mix_kernel(*refs, with_router, n_x):
    rest = refs[n_x:]
    if with_router:
        (yst_ref, ya_ref, mod_ref, wglut_ref, bglu_ref, wout_ref, gffn_ref, router_ref,
         x1_ref, h2_ref, gate_ref) = rest
    else:
        (yst_ref, ya_ref, mod_ref, wglut_ref, bglu_ref, wout_ref, gffn_ref,
         x1_ref, h2_ref) = rest
    zt = jax.nn.gelu(yst_ref[...], approximate=True)
    glt = zt * jax.nn.sigmoid(
        jnp.dot(wglut_ref[...], zt.astype(BF16), preferred_element_type=F32) + bglu_ref[...])
    mix = jnp.concatenate([glt.T.astype(BF16), ya_ref[...]], axis=-1)
    o = jnp.dot(mix, wout_ref[...], preferred_element_type=F32)
    x1 = _tile_rows(refs, n_x) + mod_ref[0, 2:3, :] * o
    x1_ref[...] = x1
    h2 = _rms(x1, gffn_ref[...]) * (1.0 + mod_ref[0, 4:5, :]) + mod_ref[0, 3:4, :]
    if not with_router:
        h2_ref[...] = h2.astype(h2_ref.dtype)
    else:
        h2_ref[...] = _pack_bf16_pairs(h2)
        h_hi = h2.astype(BF16)
        h_lo = (h2 - h_hi.astype(F32)).astype(BF16)
        o1 = jnp.dot(h_hi, router_ref[0], preferred_element_type=F32)
        o2 = jnp.dot(h_lo, router_ref[1], preferred_element_type=F32)
        logits = o1 + pltpu.roll(o1, LANES - N_EXPERTS, 1) + o2
        gate_ref[...] = _top2_route(logits)


def _mix_call(xs, y_ssm_t, y_att, mod, wglut, bglu, wout, gffn, router, *, n_tiles):
    row = lambda i: (i, 0)
    out_row = row
    with_router = router is not None
    in_specs = _x_specs(len(xs)) + [
        pl.BlockSpec((D_SSM, TM), lambda i: (0, i)),
        pl.BlockSpec((TM, D_ATT), row),
        pl.BlockSpec((1, 6, D_MODEL), lambda i: (_mod_row(i), 0, 0)),
        _const_spec((D_SSM, D_SSM)),
        _const_spec((D_SSM, 1)),
        _const_spec((D_MIX, D_MODEL)),
        _const_spec((1, D_MODEL)),
    ]
    args = [*xs, y_ssm_t, y_att, mod, wglut, bglu, wout, gffn]
    h2_cols, h2_dtype = (D_MODEL // 2, jnp.uint32) if with_router else (D_MODEL, BF16)
    out_specs = [pl.BlockSpec((TM, D_MODEL), out_row), pl.BlockSpec((TM, h2_cols), out_row)]
    out_shape = [jax.ShapeDtypeStruct((n_tiles * TM, D_MODEL), F32),
                 jax.ShapeDtypeStruct((n_tiles * TM, h2_cols), h2_dtype)]
    if with_router:
        in_specs.append(_const_spec((2, D_MODEL, LANES)))
        args.append(router)
        out_specs.append(pl.BlockSpec((TM, LANES), out_row))
        out_shape.append(jax.ShapeDtypeStruct((n_tiles * TM, LANES), F32))
    return pl.pallas_call(
        functools.partial(_mix_kernel, with_router=with_router, n_x=len(xs)),
        grid=(n_tiles,),
        in_specs=in_specs,
        out_specs=out_specs,
        out_shape=out_shape,
        compiler_params=_cparams("parallel"),
        name="mix_out",
    )(*args)


def _swiglu(h, w1_at, w3_at, w2_at, d_ff):
    acc = jnp.zeros((h.shape[0], D_MODEL), F32)
    for lo in range(0, d_ff, FF_CHUNK):
        sl = slice(lo, min(lo + FF_CHUNK, d_ff))
        a = jnp.dot(h, w1_at(sl), preferred_element_type=F32)
        b = jnp.dot(h, w3_at(sl), preferred_element_type=F32)
        g = (a * jax.nn.sigmoid(a) * b).astype(BF16)
        acc = acc + jnp.dot(g, w2_at(sl), preferred_element_type=F32)
    return acc


def _ffn_kernel(h_ref, x1_ref, mod_ref, w1_ref, w3_ref, w2_ref, o_ref):
    acc = _swiglu(h_ref[...], lambda sl: w1_ref[:, sl], lambda sl: w3_ref[:, sl],
                  lambda sl: w2_ref[sl, :], D_FF)
    o_ref[...] = x1_ref[...] + mod_ref[0, 5:6, :] * acc


def _ffn_call(h2, x1, mod, w1, w3, w2):
    n_tiles = N_TOT // TM
    row = lambda i: (i, 0)
    return pl.pallas_call(
        _ffn_kernel,
        grid=(n_tiles,),
        in_specs=[
            pl.BlockSpec((TM, D_MODEL), row),
            pl.BlockSpec((TM, D_MODEL), row),
            pl.BlockSpec((1, 6, D_MODEL), lambda i: (_mod_row(i), 0, 0)),
            _const_spec((D_MODEL, D_FF)),
            _const_spec((D_MODEL, D_FF)),
            _const_spec((D_FF, D_MODEL)),
        ],
        out_specs=pl.BlockSpec((TM, D_MODEL), row),
        out_shape=jax.ShapeDtypeStruct((N_TOT, D_MODEL), F32),
        compiler_params=_cparams("parallel"),
        name="ffn_dense",
    )(h2, x1, mod, w1, w3, w2)


EXPERT_BLK = 512
N_SORTED = 2 * N_LAT + N_EXPERTS * EXPERT_BLK
N_EXPERT_BLKS = N_SORTED // EXPERT_BLK
PACKED = D_MODEL // 2
SC_ROW = 256
SC_SPLIT = PACKED // SC_ROW
SC_WIN = 128


def _route_kernel(r_ref, dest_ref, cnt_ref, carry_scr, off_scr):
    phase, i = pl.program_id(0), pl.program_id(1)
    lane = lax.broadcasted_iota(jnp.int32, (TM, LANES), 1)
    e1 = r_ref[:, ROUTE_E1:ROUTE_E1 + 1].astype(jnp.int32)
    e2 = r_ref[:, ROUTE_E2:ROUTE_E2 + 1].astype(jnp.int32)
    picked = jnp.where((lane == e1) | (lane == e2), 1.0, 0.0)
    tile_cnt = jnp.sum(picked, axis=0, keepdims=True)

    @pl.when((phase == 0) & (i == 0))
    def _():
        carry_scr[...] = jnp.zeros_like(carry_scr)

    @pl.when(phase == 0)
    def _():
        carry_scr[...] += tile_cnt

    @pl.when((phase == 1) & (i == 0))
    def _():
        cnt = carry_scr[...]
        cnt_ref[...] = jnp.broadcast_to(cnt, cnt_ref.shape)
        padded = jnp.floor((cnt + (EXPERT_BLK - 1)) * (1.0 / EXPERT_BLK)) * EXPERT_BLK
        before = (lax.broadcasted_iota(jnp.int32, (LANES, LANES), 0)
                  < lax.broadcasted_iota(jnp.int32, (LANES, LANES), 1)).astype(F32)
        off = jnp.dot(jnp.broadcast_to(padded, (8, LANES)), before, precision=HI,
                      preferred_element_type=F32)
        off_scr[...] = off[0:1, :]
        carry_scr[...] = jnp.zeros_like(carry_scr)

    @pl.when(phase == 1)
    def _():
        earlier = (lax.broadcasted_iota(jnp.int32, (TM, TM), 1)
                   < lax.broadcasted_iota(jnp.int32, (TM, TM), 0)).astype(BF16)
        rank = jnp.dot(earlier, picked.astype(BF16), preferred_element_type=F32)
        slot = rank + carry_scr[...] + off_scr[...]
        d1 = jnp.sum(jnp.where(lane == e1, slot, 0.0), axis=-1, keepdims=True)
        d2 = jnp.sum(jnp.where(lane == e2, slot, 0.0), axis=-1, keepdims=True)
        dest_ref[...] = jnp.where(lane == 0, d1, jnp.where(lane == 1, d2, 0.0)).astype(jnp.int32)
        carry_scr[...] += tile_cnt


def _route_call(route):
    n_tiles = N_LAT // TM
    return pl.pallas_call(
        _route_kernel,
        grid=(2, n_tiles),
        in_specs=[pl.BlockSpec((TM, LANES), lambda p, i: (i, 0))],
        out_specs=[pl.BlockSpec((TM, LANES), lambda p, i: (p * i, 0)),
                   pl.BlockSpec((8, LANES), lambda p, i: (0, 0))],
        out_shape=[jax.ShapeDtypeStruct((N_LAT, LANES), jnp.int32),
                   jax.ShapeDtypeStruct((8, LANES), F32)],
        scratch_shapes=[pltpu.VMEM((1, LANES), F32), pltpu.VMEM((1, LANES), F32)],
        compiler_params=_cparams("arbitrary", "arbitrary"),
        name="moe_route",
    )(route)


def _sc_mesh():
    return plsc.VectorSubcoreMesh(core_axis_name="core", subcore_axis_name="subcore")


def _sc_scatter(x, idx_a, idx_b, n_out):
    n = x.shape[0]

    @pl.kernel(out_type=jax.ShapeDtypeStruct((n_out, SC_ROW), x.dtype), mesh=_sc_mesh(),
               scratch_types=[])
    def scatter(x_hbm, a_hbm, b_hbm, o_hbm):
        def body(x_vmem, a_vmem, b_vmem):
            pltpu.sync_copy(x_vmem, o_hbm.at[a_vmem.at[0]])
            pltpu.sync_copy(x_vmem, o_hbm.at[b_vmem.at[0]])

        pltpu.emit_pipeline(
            body, grid=(n // SC_WIN,),
            in_specs=[pl.BlockSpec((SC_WIN, SC_ROW), lambda i: (i, 0)),
                      pl.BlockSpec((1, SC_WIN), lambda i: (0, i)),
                      pl.BlockSpec((1, SC_WIN), lambda i: (0, i))],
            out_specs=[],
            core_axis_name=("core", "subcore"),
            dimension_semantics=(pltpu.PARALLEL,),
        )(x_hbm, a_hbm, b_hbm)

    return scatter(x, idx_a.reshape(1, n), idx_b.reshape(1, n))


def _sc_gather(y, idx):
    n = idx.shape[0]

    @pl.kernel(out_type=jax.ShapeDtypeStruct((n, SC_ROW), y.dtype), mesh=_sc_mesh(),
               scratch_types=[])
    def gather(y_hbm, i_hbm, o_hbm):
        def body(i_vmem, o_vmem):
            pltpu.sync_copy(y_hbm.at[i_vmem.at[0]], o_vmem)

        pltpu.emit_pipeline(
            body, grid=(n // SC_WIN,),
            in_specs=[pl.BlockSpec((1, SC_WIN), lambda i: (0, i))],
            out_specs=[pl.BlockSpec((SC_WIN, SC_ROW), lambda i: (i, 0))],
            core_axis_name=("core", "subcore"),
            dimension_semantics=(pltpu.PARALLEL,),
        )(i_hbm, o_hbm)

    return gather(y, idx.reshape(1, n))


def _experts_kernel(blk_expert_ref, n_used_ref, x_ref, w1_ref, w3_ref, w2_ref, o_ref):
    del blk_expert_ref

    @pl.when(pl.program_id(0) < n_used_ref[0])
    def _():
        hi, lo = _unpack_bf16_pairs(x_ref[...])
        h = jnp.concatenate([hi.astype(BF16), lo.astype(BF16)], axis=-1)
        y = _swiglu(h, lambda sl: w1_ref[0, :, sl], lambda sl: w3_ref[0, :, sl],
                    lambda sl: w2_ref[0, sl, :], D_FF_EXPERT)
        o_ref[...] = _pack_bf16_pairs(y)


def _experts_call(blk_expert, n_used, xs, w1, w3, w2):
    row = lambda b, be, nu: (b, 0)
    wsel = lambda b, be, nu: (be[b], 0, 0)
    return pl.pallas_call(
        _experts_kernel,
        grid_spec=pltpu.PrefetchScalarGridSpec(
            num_scalar_prefetch=2,
            grid=(N_EXPERT_BLKS,),
            in_specs=[
                pl.BlockSpec((EXPERT_BLK, PACKED), row),
                pl.BlockSpec((1, D_MODEL, D_FF_EXPERT), wsel),
                pl.BlockSpec((1, D_MODEL, D_FF_EXPERT), wsel),
                pl.BlockSpec((1, D_FF_EXPERT, D_MODEL), wsel),
            ],
            out_specs=pl.BlockSpec((EXPERT_BLK, PACKED), row),
        ),
        out_shape=jax.ShapeDtypeStruct((N_SORTED, PACKED), jnp.uint32),
        compiler_params=_cparams("arbitrary"),
        name="moe_experts",
    )(blk_expert, n_used, xs, w1, w3, w2)


def _combine_kernel(x1_ref, r_ref, y_ref, mod_ref, fg_ref, o_ref):
    def expert_out(slot):
        hi, lo = _unpack_bf16_pairs(y_ref[slot])
        return jnp.concatenate([hi, lo], axis=-1)
    w1 = r_ref[:, ROUTE_W1:ROUTE_W1 + 1]
    w2 = r_ref[:, ROUTE_W2:ROUTE_W2 + 1]
    y = w1 * expert_out(0) + w2 * expert_out(1)
    x2 = x1_ref[...] + mod_ref[0, 5:6, :] * y
    o_ref[...] = _rms(x2, fg_ref[...])


def _combine_call(x1, route, y_tok, mod, fg):
    n_tiles = N_LAT // TM
    row = lambda i: (i, 0)
    return pl.pallas_call(
        _combine_kernel,
        grid=(n_tiles,),
        in_specs=[
            pl.BlockSpec((TM, D_MODEL), row),
            pl.BlockSpec((TM, LANES), row),
            pl.BlockSpec((2, TM, PACKED), lambda i: (0, i, 0)),
            pl.BlockSpec((1, 6, D_MODEL), lambda i: (i // SEQ_TILES, 0, 0)),
            pl.BlockSpec((1, D_MODEL), lambda i: (0, 0)),
        ],
        out_specs=pl.BlockSpec((TM, D_MODEL), row),
        out_shape=jax.ShapeDtypeStruct((N_LAT, D_MODEL), F32),
        compiler_params=_cparams("parallel"),
        name="moe_combine",
    )(x1, route, y_tok, mod, fg)


def _moe_routed(h2p, x1, route, mod, w1, w3, w2, fg):
    dest, cnt = _route_call(route)
    blks = (cnt[0, :N_EXPERTS].astype(jnp.int32) + (EXPERT_BLK - 1)) // EXPERT_BLK
    blk_end = jnp.cumsum(blks)
    n_used = blk_end[-1:]
    b = jnp.minimum(jnp.arange(N_EXPERT_BLKS, dtype=jnp.int32), n_used[0] - 1)
    blk_expert = jnp.sum((b[:, None] >= blk_end[None, :]).astype(jnp.int32), axis=1)
    part = jnp.arange(SC_SPLIT, dtype=jnp.int32)
    idx = [(dest[:, s:s + 1] * SC_SPLIT + part).reshape(N_LAT * SC_SPLIT) for s in range(2)]
    xs = _sc_scatter(h2p.reshape(N_LAT * SC_SPLIT, SC_ROW), idx[0], idx[1], N_SORTED * SC_SPLIT)
    ys = _experts_call(blk_expert, n_used, xs.reshape(N_SORTED, PACKED), w1, w3, w2)
    y_tok = _sc_gather(ys.reshape(N_SORTED * SC_SPLIT, SC_ROW), jnp.concatenate(idx))
    return _combine_call(x1, route, y_tok.reshape(2, N_LAT, PACKED), mod, fg)


def _rope_partner_perm():
    perm, sign = [], []
    for j in range(QK_ROPE):
        first_half = (j % AXIS_ROPE) < ROPE_FREQS
        perm.append(j + ROPE_FREQS if first_half else j - ROPE_FREQS)
        sign.append(-1.0 if first_half else 1.0)
    return jnp.array(perm, jnp.int32), jnp.array(sign, F32)


def _rope_tables():
    t = jnp.arange(SEQ)
    row = (t // GRID_W).astype(F32)
    col = (t % GRID_W).astype(F32)
    inv_freq = ROPE_BASE ** (-2.0 * jnp.arange(ROPE_FREQS, dtype=F32) / AXIS_ROPE)
    ang = jnp.concatenate([row[:, None] * inv_freq, row[:, None] * inv_freq,
                           col[:, None] * inv_freq, col[:, None] * inv_freq], axis=1)
    cos = jnp.concatenate([jnp.cos(ang), jnp.ones((N_CTX, QK_ROPE), F32)], axis=0)
    sin = jnp.concatenate([jnp.sin(ang), jnp.zeros((N_CTX, QK_ROPE), F32)], axis=0)
    n = N_CTX + SEQ
    pad32 = jnp.zeros((n, HEAD_PAD - D_QK), F32)
    qs = ATT_SCALE * math.log2(math.e)
    cq = jnp.concatenate([jnp.full((n, QK_NOPE), qs, F32), qs * cos, pad32], axis=1)
    sq = jnp.concatenate([jnp.zeros((n, QK_NOPE), F32), qs * sin, pad32], axis=1)
    cs = jnp.concatenate([cos, sin, jnp.zeros((n, LANES - 2 * QK_ROPE), F32)], axis=1)
    return cq, sq, cs


def _layer_weights(w_in, w_uq, w_ukv):
    perm, sign = _rope_partner_perm()
    s0 = D_SSM + Q_LORA + KV_LORA
    kr_w = w_in[:, s0:s0 + QK_ROPE]
    wut = w_in[:, :D_SSM].T.astype(BF16)
    win = jnp.concatenate([w_in[:, D_SSM:s0], kr_w, kr_w[:, perm] * sign,
                           jnp.zeros((D_MODEL, LANES - 2 * QK_ROPE), F32)], axis=1).astype(BF16)
    uq = w_uq.reshape(Q_LORA, N_HEADS, D_QK)
    nope, rope = uq[..., :QK_NOPE], uq[..., QK_NOPE:]
    zpad = jnp.zeros((Q_LORA, N_HEADS, HEAD_PAD - D_QK), F32)
    wq1 = jnp.concatenate([nope, rope, zpad], axis=-1).reshape(Q_LORA, N_HEADS * HEAD_PAD).astype(BF16)
    wq2 = jnp.concatenate([jnp.zeros_like(nope), rope[..., perm] * sign, zpad], axis=-1)
    wq2 = wq2.reshape(Q_LORA, N_HEADS * HEAD_PAD).astype(BF16)
    ukv = w_ukv.reshape(KV_LORA, N_HEADS, QK_NOPE + V_HEAD)
    wk = jnp.concatenate([ukv[..., :QK_NOPE], jnp.zeros((KV_LORA, N_HEADS, HEAD_PAD - QK_NOPE), F32)],
                         axis=-1).reshape(KV_LORA, N_HEADS * HEAD_PAD)
    eye = jnp.eye(QK_ROPE, dtype=F32)
    place = jnp.concatenate([jnp.zeros((QK_ROPE, QK_NOPE), F32), eye,
                             jnp.zeros((QK_ROPE, HEAD_PAD - D_QK), F32)], axis=1)
    place = jnp.tile(place, (1, N_HEADS))
    spread = jnp.concatenate([place, place, jnp.zeros((LANES - 2 * QK_ROPE, N_HEADS * HEAD_PAD), F32)], 0)
    wkk = jnp.concatenate([wk, spread], axis=0).astype(BF16)
    wv = jnp.concatenate([ukv[..., QK_NOPE:], jnp.zeros((KV_LORA, N_HEADS, HEAD_PAD - V_HEAD), F32)],
                         axis=-1).reshape(KV_LORA, N_HEADS * HEAD_PAD).astype(BF16)
    return wut, win, wq1, wq2, wkk, wv


def kernel(x, c, ctx, c_ctx, w_ada, b_ada, norm_mix, norm_ffn, w_in, q_norm, kv_norm, w_uq, w_ukv,
           ssm_a_re, ssm_a_im, ssm_log_dt, ssm_b_re, ssm_b_im, ssm_c_re, ssm_c_im, ssm_d, w_glu,
           b_glu, w_out, ffn_w1, ffn_w3, ffn_w2, moe_router, moe_w1, moe_w3, moe_w2, final_norm):
    assert x.shape == (BATCH, SEQ, D_MODEL) and ctx.shape == (BATCH, CTX_LEN, D_MODEL)
    cond = jnp.concatenate([c, c_ctx[None, :], jnp.zeros((MOD_ROWS - BATCH - 1, D_MODEL), F32)], axis=0)
    mod_all = _ada_call(cond, w_ada, b_ada).reshape(DEPTH, MOD_ROWS, 6, D_MODEL)
    cq_t, sq_t, cs_t = _rope_tables()
    xs = (x.reshape(N_LAT, D_MODEL), ctx.reshape(N_CTX, D_MODEL))

    out = None
    for i in range(DEPTH):
        last = i == DEPTH - 1
        mod = mod_all[i]
        wut, win, wq1, wq2, wkk, wv = _layer_weights(w_in[i], w_uq[i], w_ukv[i])
        u_t, q, k, v = _inproj_call(xs, mod, norm_mix[i][None, :], wut, win, q_norm[i][None, :],
                                    kv_norm[i][None, :], wq1, wq2, wkk, wv, cq_t, sq_t, cs_t)
        tabs = _ssm_tables(ssm_a_re[i], ssm_a_im[i], ssm_log_dt[i], ssm_b_re[i], ssm_b_im[i],
                           ssm_c_re[i], ssm_c_im[i], ssm_d[i])
        y_ssm_t = _ssm_call(u_t.reshape(D_SSM, N_CHUNK, SSM_T), *tabs).reshape(D_SSM, N_TOT)
        y_att = _attn_latent_call(q, k, v)
        if last:
            n_tiles = LAT_TILES
        else:
            y_att = _attn_ctx_call(q, k, v, y_att)
            n_tiles = N_TOT // TM
        j = i // 2
        if i % 2 == 0:
            router = None
        else:
            r = moe_router[j]
            r_top = lax.bitcast_convert_type(
                lax.bitcast_convert_type(r, jnp.uint32) & jnp.uint32(0xFFFF0000), F32)
            r_hi = r_top.astype(BF16)
            r_lo = (r - r_top).astype(BF16)
            zr = jnp.zeros((D_MODEL, LANES - 2 * N_EXPERTS), BF16)
            router = jnp.stack([jnp.concatenate([r_hi, r_lo, zr], axis=1),
                                jnp.concatenate([r_hi, jnp.zeros_like(r_lo), zr], axis=1)])
        res = _mix_call(xs, y_ssm_t, y_att, mod, w_glu[i].T.astype(BF16), b_glu[i][:, None],
                        w_out[i].astype(BF16), norm_ffn[i][None, :], router, n_tiles=n_tiles)
        if i % 2 == 0:
            assert not last
            x1, h2 = res
            xs = (_ffn_call(h2, x1, mod, ffn_w1[j].astype(BF16), ffn_w3[j].astype(BF16),
                            ffn_w2[j].astype(BF16)),)
        else:
            assert last
            x1, h2p, route = res
            out = _moe_routed(h2p, x1, route, mod, moe_w1[j].astype(BF16), moe_w3[j].astype(BF16),
                              moe_w2[j].astype(BF16), final_norm[None, :])
    return out.reshape(BATCH, SEQ, D_MODEL)
```

```python
import functools
import math

import jax
import jax.numpy as jnp
from jax import lax
from jax.experimental import pallas as pl
from jax.experimental.pallas import tpu as pltpu
from jax.experimental.pallas import tpu_sc as plsc

D_MODEL = 1024
BATCH = 4
SEQ = 8192
DEPTH = 2
GRID_W = 64
CTX_LEN = 256
D_SSM = 512
SSM_GROUP = 16
N_SSM_GROUPS = D_SSM // SSM_GROUP
SSM_STATE = 64
N_HEADS = 8
QK_NOPE = 64
QK_ROPE = 32
V_HEAD = 64
Q_LORA = 256
KV_LORA = 128
D_QK = QK_NOPE + QK_ROPE
D_ATT = N_HEADS * V_HEAD
D_MIX = D_SSM + D_ATT
D_IN = D_SSM + Q_LORA + KV_LORA + QK_ROPE
AXIS_ROPE = QK_ROPE // 2
ROPE_FREQS = AXIS_ROPE // 2
ROPE_BASE = 10000.0
ATT_SCALE = 1.0 / math.sqrt(D_QK)
D_FF = 2816
N_EXPERTS = 8
D_FF_EXPERT = 1408
EPS = 1e-6

N_CTX = BATCH * CTX_LEN
N_LAT = BATCH * SEQ
N_TOT = N_CTX + N_LAT

LANES = 128
HEAD_PAD = 128
TM = 512
LAT_TILES = N_LAT // TM
SEQ_TILES = SEQ // TM
TQ = 512
TK = 2048
SSM_T = 128
N_CHUNK_LAT = N_LAT // SSM_T
N_CHUNK = N_TOT // SSM_T
SSM_K = SSM_GROUP * SSM_T
FF_CHUNK = 256
N_FF_CHUNKS = D_FF // FF_CHUNK
D_IN_REST = Q_LORA + KV_LORA + LANES
MOD_ROWS = 8
VMEM_LIMIT = 52 * 1024 * 1024

F32 = jnp.float32
BF16 = jnp.bfloat16
HI = lax.Precision.HIGHEST


def _cparams(*sem):
    return pltpu.CompilerParams(dimension_semantics=sem, vmem_limit_bytes=VMEM_LIMIT)


def _const_spec(shape):
    nd = len(shape)
    return pl.BlockSpec(shape, lambda *_: (0,) * nd, pipeline_mode=pl.Buffered(1))


def _mod_row(i):
    return jnp.where(i < LAT_TILES, i // SEQ_TILES, BATCH)


def _pos_tile(i):
    return jnp.where(i < LAT_TILES, i % SEQ_TILES, SEQ_TILES + i - LAT_TILES)


def _rms(x, g):
    ms = jnp.mean(x * x, axis=-1, keepdims=True)
    return x * lax.rsqrt(ms + EPS) * g


ADA_TN = 1536


def _ada_kernel(c_ref, w_ref, b_ref, o_ref):
    c = c_ref[...]
    s = c * jax.nn.sigmoid(c)
    o_ref[0] = jnp.dot(s, w_ref[0], precision=HI, preferred_element_type=F32) + b_ref[0]


def _ada_call(cond, w_ada, b_ada):
    n_col = 6 * D_MODEL // ADA_TN
    return pl.pallas_call(
        _ada_kernel,
        grid=(DEPTH, n_col),
        in_specs=[
            pl.BlockSpec((MOD_ROWS, D_MODEL), lambda l, j: (0, 0)),
            pl.BlockSpec((1, D_MODEL, ADA_TN), lambda l, j: (l, 0, j)),
            pl.BlockSpec((1, 1, ADA_TN), lambda l, j: (l, 0, j)),
        ],
        out_specs=pl.BlockSpec((1, MOD_ROWS, ADA_TN), lambda l, j: (l, 0, j)),
        out_shape=jax.ShapeDtypeStruct((DEPTH, MOD_ROWS, 6 * D_MODEL), F32),
        compiler_params=_cparams("arbitrary", "arbitrary"),
        name="ada_mod",
    )(cond, w_ada, b_ada.reshape(DEPTH, 1, 6 * D_MODEL))


def _tile_rows(refs, n_x):
    if n_x == 1:
        return refs[0][...]
    return jnp.where(pl.program_id(0) < LAT_TILES, refs[0][...], refs[1][...])


def _x_specs(n_x):
    if n_x == 1:
        return [pl.BlockSpec((TM, D_MODEL), lambda i: (i, 0))]
    return [pl.BlockSpec((TM, D_MODEL), lambda i: (jnp.minimum(i, LAT_TILES - 1), 0)),
            pl.BlockSpec((TM, D_MODEL), lambda i: (jnp.maximum(i - LAT_TILES, 0), 0))]


def _inproj_kernel(*refs, n_x):
    (mod_ref, g_ref, wut_ref, win_ref, qg_ref, kvg_ref, wq1_ref, wq2_ref, wkk_ref, wv_ref,
     cq_ref, sq_ref, cs_ref, ut_ref, q_ref, k_ref, v_ref) = refs[n_x:]
    x = _tile_rows(refs, n_x)
    sh = mod_ref[0, 0:1, :]
    sc = mod_ref[0, 1:2, :]
    xm = (_rms(x, g_ref[...]) * (1.0 + sc) + sh).astype(BF16)
    ut_ref[...] = lax.dot_general(wut_ref[...], xm, (((1,), (1,)), ((), ())),
                                  preferred_element_type=F32)
    z = jnp.dot(xm, win_ref[...], preferred_element_type=F32)
    qn = _rms(z[:, :Q_LORA], qg_ref[...]).astype(BF16)
    kvn = _rms(z[:, Q_LORA:Q_LORA + KV_LORA], kvg_ref[...]).astype(BF16)
    krr = (z[:, Q_LORA + KV_LORA:] * cs_ref[...]).astype(BF16)
    q1 = jnp.dot(qn, wq1_ref[...], preferred_element_type=F32)
    q2 = jnp.dot(qn, wq2_ref[...], preferred_element_type=F32)
    cq = cq_ref[...]
    sq = sq_ref[...]
    for h in range(N_HEADS):
        sl = slice(h * HEAD_PAD, (h + 1) * HEAD_PAD)
        q_ref[:, sl] = (q1[:, sl] * cq + q2[:, sl] * sq).astype(q_ref.dtype)
    kin = jnp.concatenate([kvn, krr], axis=-1)
    k_ref[...] = jnp.dot(kin, wkk_ref[...], preferred_element_type=F32).astype(k_ref.dtype)
    vv = jnp.dot(kvn, wv_ref[...], preferred_element_type=F32)
    lane = lax.broadcasted_iota(jnp.int32, vv.shape, 1)
    v_ref[...] = jnp.where(lane % HEAD_PAD == V_HEAD, 1.0, vv).astype(v_ref.dtype)


def _inproj_call(xs, mod, g_mix, wut, win, qg, kvg, wq1, wq2, wkk, wv, cq_t, sq_t, cs_t):
    n_tiles = N_TOT // TM
    row = lambda i: (i, 0)
    pos = lambda i: (_pos_tile(i), 0)
    return pl.pallas_call(
        functools.partial(_inproj_kernel, n_x=len(xs)),
        grid=(n_tiles,),
        in_specs=_x_specs(len(xs)) + [
            pl.BlockSpec((1, 6, D_MODEL), lambda i: (_mod_row(i), 0, 0)),
            _const_spec((1, D_MODEL)),
            _const_spec((D_SSM, D_MODEL)),
            _const_spec((D_MODEL, D_IN_REST)),
            _const_spec((1, Q_LORA)),
            _const_spec((1, KV_LORA)),
            _const_spec((Q_LORA, N_HEADS * HEAD_PAD)),
            _const_spec((Q_LORA, N_HEADS * HEAD_PAD)),
            _const_spec((2 * KV_LORA, N_HEADS * HEAD_PAD)),
            _const_spec((KV_LORA, N_HEADS * HEAD_PAD)),
            pl.BlockSpec((TM, LANES), pos),
            pl.BlockSpec((TM, LANES), pos),
            pl.BlockSpec((TM, LANES), pos),
        ],
        out_specs=[
            pl.BlockSpec((D_SSM, TM), lambda i: (0, i)),
            pl.BlockSpec((TM, N_HEADS * HEAD_PAD), row),
            pl.BlockSpec((TM, N_HEADS * HEAD_PAD), row),
            pl.BlockSpec((TM, N_HEADS * HEAD_PAD), row),
        ],
        out_shape=[
            jax.ShapeDtypeStruct((D_SSM, N_TOT), F32),
            jax.ShapeDtypeStruct((N_TOT, N_HEADS * HEAD_PAD), BF16),
            jax.ShapeDtypeStruct((N_TOT, N_HEADS * HEAD_PAD), BF16),
            jax.ShapeDtypeStruct((N_TOT, N_HEADS * HEAD_PAD), BF16),
        ],
        compiler_params=_cparams("parallel"),
        name="in_proj",
    )(*xs, mod, g_mix, wut, win, qg, kvg, wq1, wq2, wkk, wv, cq_t, sq_t, cs_t)


def _attn_kernel(*refs, n_chunks, tq):
    if n_chunks:
        q_ref, k_ref, v_ref, kc_ref, vc_ref, o_ref, s_scr = refs
    else:
        q_ref, kc_ref, vc_ref, _, o_ref, s_scr = refs
    heads =[slice(hh * HEAD_PAD, (hh + 1) * HEAD_PAD) for hh in range(2)]
    qs = [q_ref[:, sl] for sl in heads]

    def put_scores(slot, k_at, width):
        for hh in range(2):
            s_scr[slot, hh, :, :width] = lax.dot_general(
                qs[hh], k_at(heads[hh]), (((1,), (1,)), ((), ())), preferred_element_type=F32)

    def consume(carry, slot, v_at, width):
        new = []
        for hh in range(2):
            m, acc = carry[hh]
            s = s_scr[slot, hh, :, :width]
            m_new = jnp.maximum(m, jnp.max(s, axis=-1, keepdims=True))
            alpha = jnp.exp2(m - m_new)
            p = jnp.exp2(s - m_new).astype(BF16)
            acc = alpha * acc + jnp.dot(p, v_at(heads[hh]), preferred_element_type=F32)
            new.append((m_new, acc))
        return tuple(new)

    def chunk(c):
        rows = pl.ds(pl.multiple_of(c * TK, TK), TK)
        return (lambda sl: k_ref[rows, sl]), (lambda sl: v_ref[rows, sl])

    ctx_k, ctx_v = (lambda sl: kc_ref[:, sl]), (lambda sl: vc_ref[:, sl])
    carry = tuple((jnp.full((tq, 1), -jnp.inf, F32), jnp.zeros((tq, HEAD_PAD), F32))
                  for _ in range(2))
    if n_chunks:
        assert n_chunks % 2 == 0
        put_scores(0, chunk(0)[0], TK)

        def body(j, carry):
            c0 = 2 * j
            put_scores(1, chunk(c0 + 1)[0], TK)
            carry = consume(carry, 0, chunk(c0)[1], TK)
            put_scores(0, chunk(c0 + 2)[0], TK)
            return consume(carry, 1, chunk(c0 + 1)[1], TK)

        carry = lax.fori_loop(0, n_chunks // 2 - 1, body, carry)
        put_scores(1, chunk(n_chunks - 1)[0], TK)
        carry = consume(carry, 0, chunk(n_chunks - 2)[1], TK)
        put_scores(0, ctx_k, CTX_LEN)
        carry = consume(carry, 1, chunk(n_chunks - 1)[1], TK)
    else:
        put_scores(0, ctx_k, CTX_LEN)
    carry = consume(carry, 0, ctx_v, CTX_LEN)
    outs = [acc[:, :V_HEAD] / acc[:, V_HEAD:V_HEAD + 1] for _, acc in carry]
    o_ref[...] = jnp.concatenate(outs, axis=-1).astype(o_ref.dtype)


def _attn_latent_call(q, k, v):
    qt = SEQ // TQ
    ctx0 = N_LAT // CTX_LEN
    return pl.pallas_call(
        functools.partial(_attn_kernel, n_chunks=SEQ // TK, tq=TQ),
        grid=(BATCH, N_HEADS // 2, qt),
        in_specs=[
            pl.BlockSpec((TQ, 2 * HEAD_PAD), lambda b, h, i: (b * qt + i, h)),
            pl.BlockSpec((SEQ, 2 * HEAD_PAD), lambda b, h, i: (b, h)),
            pl.BlockSpec((SEQ, 2 * HEAD_PAD), lambda b, h, i: (b, h)),
            pl.BlockSpec((CTX_LEN, 2 * HEAD_PAD), lambda b, h, i: (ctx0 + b, h)),
            pl.BlockSpec((CTX_LEN, 2 * HEAD_PAD), lambda b, h, i: (ctx0 + b, h)),
        ],
        out_specs=pl.BlockSpec((TQ, 2 * V_HEAD), lambda b, h, i: (b * qt + i, h)),
        out_shape=jax.ShapeDtypeStruct((N_TOT, D_ATT), BF16),
        scratch_shapes=[pltpu.VMEM((2, 2, TQ, TK), F32)],
        compiler_params=_cparams("parallel", "parallel", "arbitrary"),
        name="attn_latent",
    )(q, k, v, k, v)


def _attn_ctx_call(q, k, v, y_att):
    ctx0 = N_LAT // CTX_LEN
    return pl.pallas_call(
        functools.partial(_attn_kernel, n_chunks=0, tq=CTX_LEN),
        grid=(BATCH, N_HEADS // 2),
        in_specs=[
            pl.BlockSpec((CTX_LEN, 2 * HEAD_PAD), lambda b, h: (ctx0 + b, h)),
            pl.BlockSpec((CTX_LEN, 2 * HEAD_PAD), lambda b, h: (ctx0 + b, h)),
            pl.BlockSpec((CTX_LEN, 2 * HEAD_PAD), lambda b, h: (ctx0 + b, h)),
            pl.BlockSpec(memory_space=pl.ANY),
        ],
        out_specs=pl.BlockSpec((CTX_LEN, 2 * V_HEAD), lambda b, h: (ctx0 + b, h)),
        out_shape=jax.ShapeDtypeStruct((N_TOT, D_ATT), BF16),
        input_output_aliases={3: 0},
        scratch_shapes=[pltpu.VMEM((1, 2, CTX_LEN, CTX_LEN), F32)],
        compiler_params=_cparams("parallel", "parallel"),
        name="attn_ctx",
    )(q, k, v, y_att)


PT_LAG, PT_INC, PT_OUT = 0, 2 * SSM_T, 3 * SSM_T
PT_ROWS = 4 * SSM_T
NT_DIMS = (((1,), (1,)), ((), ()))


def _ssm_kernel(u_ref, pta_ref, ptb_ref, rows_ref, cc_ref, at_ref, dv_ref, y_ref,
                abt_hi_scr, abt_lo_scr, wl_scr, m_scr, ws_scr, wct_scr, s_scr, h_scr):
    H, T = SSM_GROUP, SSM_T

    def split(v):
        hi = v.astype(BF16)
        return hi, (v - hi.astype(F32)).astype(BF16)

    def scaled(row0, n_rows, ra, rb):
        return pta_ref[0, row0:row0 + n_rows, :] * ra + ptb_ref[0, row0:row0 + n_rows, :] * rb

    def build_tables(i, _):
        row0 = pl.multiple_of(i * T, T)
        b_re = rows_ref[0, pl.ds(i, 1), :]
        b_im = rows_ref[0, pl.ds(H + i, 1), :]
        lag_rows = pl.ds(pl.multiple_of(i * 2 * T, 2 * T), 2 * T)
        abt_hi_scr[lag_rows, :], abt_lo_scr[lag_rows, :] = split(scaled(PT_LAG, 2 * T, b_re, b_im))
        ws_scr[pl.ds(row0, T), :] = scaled(PT_INC, T, b_re, b_im).astype(BF16)
        c_re = rows_ref[0, pl.ds(2 * H + i, 1), :]
        c_im = rows_ref[0, pl.ds(3 * H + i, 1), :]
        wct_scr[pl.ds(row0, T), :] = scaled(PT_OUT, T, c_re, c_im).astype(BF16)
        return 0

    lax.fori_loop(0, H, build_tables, 0)
    cc_hi, cc_lo = split(cc_ref[0])
    nt = lambda a, b: lax.dot_general(a, b, NT_DIMS, preferred_element_type=F32)
    wl_scr[...] = (nt(cc_hi, abt_hi_scr[...]) + nt(cc_hi, abt_lo_scr[...])
                   + nt(cc_lo, abt_hi_scr[...]))

    def build_toeplitz(ci, _):
        row0 = pl.multiple_of(ci * T, T)
        lane0 = pl.multiple_of(ci * 2 * T, 2 * T)
        for co in range(H):
            wb = jnp.broadcast_to(wl_scr[co:co + 1, pl.ds(lane0, 2 * T)], (T, 2 * T))
            toe = pltpu.roll(wb, 0, 1, stride=1, stride_axis=0)
            m_scr[pl.ds(row0, T), co * T:(co + 1) * T] = toe[:, :T].astype(BF16)
        return 0

    lax.fori_loop(0, H, build_toeplitz, 0)

    u32 = jnp.concatenate([u_ref[c] for c in range(H)], axis=1)
    u = u32.astype(BF16)
    y = jnp.dot(u, m_scr[...], preferred_element_type=F32)
    s = jnp.dot(u, ws_scr[...], preferred_element_type=F32)
    s_scr[:, 0:2 * LANES] = s
    s_scr[:, 2 * LANES:3 * LANES] = pltpu.roll(s[:, 0:LANES], SSM_STATE, 1)
    s_scr[:, 3 * LANES:4 * LANES] = pltpu.roll(s[:, LANES:2 * LANES], SSM_STATE, 1)

    def advance(h, h_sw, d, row):
        a0 = at_ref[0, 2 * d:2 * d + 1, :]
        a1 = at_ref[0, 2 * d + 1:2 * d + 2, :]
        lanes = slice(d * LANES, (d + 1) * LANES)
        lanes_sw = slice((2 + d) * LANES, (3 + d) * LANES)
        return (h * a0 + h_sw * a1 + s_scr[row:row + 1, lanes],
                h_sw * a0 - h * a1 + s_scr[row:row + 1, lanes_sw])

    n_lat, n_ctx = SEQ // T, CTX_LEN // T
    zero = jnp.zeros((1, 2 * SSM_STATE), F32)
    for b in range(BATCH):
        ctx_rows = [N_CHUNK_LAT + b * n_ctx + c for c in range(n_ctx)]
        lat_rows = [b * n_lat + k for k in range(n_lat)]
        hf, hf_sw, hr, hr_sw = zero, zero, zero, zero
        for rf, rr in zip(ctx_rows + lat_rows, ctx_rows[::-1] + lat_rows[::-1]):
            h_scr[rf:rf + 1, 0:LANES] = hf
            hf, hf_sw = advance(hf, hf_sw, 0, rf)
            h_scr[rr:rr + 1, LANES:2 * LANES] = hr
            hr, hr_sw = advance(hr, hr_sw, 1, rr)

    y = y + lax.dot_general(h_scr[...].astype(BF16), wct_scr[...], NT_DIMS,
                            preferred_element_type=F32)
    y = y + u32 * dv_ref[0]
    for c in range(H):
        y_ref[c] = y[:, c * T:(c + 1) * T]


def _ssm_call(u_t, pta, ptb, rows, cc, at, dv):
    g3 = lambda g: (g, 0, 0)
    return pl.pallas_call(
        _ssm_kernel,
        grid=(N_SSM_GROUPS,),
        in_specs=[
            pl.BlockSpec((SSM_GROUP, N_CHUNK, SSM_T), g3),
            pl.BlockSpec((1, PT_ROWS, 4 * SSM_STATE), g3),
            pl.BlockSpec((1, PT_ROWS, 4 * SSM_STATE), g3),
            pl.BlockSpec((1, 4 * SSM_GROUP, 4 * SSM_STATE), g3),
            pl.BlockSpec((1, SSM_GROUP, 4 * SSM_STATE), g3),
            pl.BlockSpec((1, 4, 2 * SSM_STATE), g3),
            pl.BlockSpec((1, 1, SSM_K), g3),
        ],
        out_specs=pl.BlockSpec((SSM_GROUP, N_CHUNK, SSM_T), g3),
        out_shape=jax.ShapeDtypeStruct((D_SSM, N_CHUNK, SSM_T), F32),
        scratch_shapes=[
            pltpu.VMEM((SSM_GROUP * 2 * SSM_T, 4 * SSM_STATE), BF16),
            pltpu.VMEM((SSM_GROUP * 2 * SSM_T, 4 * SSM_STATE), BF16),
            pltpu.VMEM((SSM_GROUP, SSM_GROUP * 2 * SSM_T), F32),
            pltpu.VMEM((SSM_K, SSM_K), BF16),
            pltpu.VMEM((SSM_K, 4 * SSM_STATE), BF16),
            pltpu.VMEM((SSM_K, 4 * SSM_STATE), BF16),
            pltpu.VMEM((N_CHUNK, 8 * SSM_STATE), F32),
            pltpu.VMEM((N_CHUNK, 4 * SSM_STATE), F32),
        ],
        compiler_params=_cparams("parallel"),
        name="s5_mixer",
    )(u_t, pta, ptb, rows, cc, at, dv)


def _ssm_tables(a_re, a_im, log_dt, b_re, b_im, c_re, c_im, d_skip):
    G, P, H, T = N_SSM_GROUPS, SSM_STATE, SSM_GROUP, SSM_T
    a_re, a_im = a_re.astype(F32), a_im.astype(F32)
    dt = jnp.exp(log_dt.astype(F32))[..., None]
    den = a_re * a_re + a_im * a_im
    mag1 = jnp.exp(dt * a_re)
    ab_re, ab_im = mag1 * jnp.cos(dt * a_im), mag1 * jnp.sin(dt * a_im)
    num_re = ab_re - 1.0
    f_re = (num_re * a_re + ab_im * a_im) / den
    f_im = (ab_im * a_re - num_re * a_im) / den
    b_re, b_im = b_re.astype(F32), b_im.astype(F32)
    bb_re = f_re[..., None] * b_re - f_im[..., None] * b_im
    bb_im = f_re[..., None] * b_im + f_im[..., None] * b_re
    c_re, c_im = c_re.astype(F32), c_im.astype(F32)
    la, th = dt * a_re, dt * a_im

    def powers(d, n):
        mag = jnp.exp(la[d][:, None, :] * n[None, :, None])
        ph = th[d][:, None, :] * n[None, :, None]
        return mag * jnp.cos(ph), mag * jnp.sin(ph)

    j = jnp.arange(2 * T)
    s_idx = jnp.arange(T)
    f_exp = jnp.concatenate([jnp.where(j < T, j, 0), T - 1 - s_idx, s_idx + 1]).astype(F32)
    r_exp = jnp.concatenate([jnp.where(j > T, 2 * T - j, 0), s_idx, T - s_idx]).astype(F32)
    ones = jnp.ones((2 * T,), F32)
    f_on = jnp.concatenate([(j < T).astype(F32), ones])[None, :, None]
    r_on = jnp.concatenate([((j == 0) | (j > T)).astype(F32), ones])[None, :, None]
    f_re_n, f_im_n = powers(0, f_exp)
    r_re_n, r_im_n = powers(1, r_exp)
    f_re_n, f_im_n, r_re_n, r_im_n = f_re_n * f_on, f_im_n * f_on, r_re_n * r_on, r_im_n * r_on
    pta = jnp.concatenate([f_re_n, f_im_n, r_re_n, r_im_n], axis=-1)
    ptb = jnp.concatenate([-f_im_n, f_re_n, -r_im_n, r_re_n], axis=-1)

    def per_channel(v):
        f, r = v[0].transpose(0, 2, 1), v[1].transpose(0, 2, 1)
        return jnp.concatenate([f, f, r, r], axis=-1)
    cf_re, cr_re = c_re[0], c_re[1]
    cf_im, cr_im = c_im[0], c_im[1]
    rows = jnp.concatenate([
        per_channel(bb_re), per_channel(bb_im),
        jnp.concatenate([cf_re, -cf_re, cr_re, -cr_re], axis=-1),
        jnp.concatenate([cf_im, -cf_im, cr_im, -cr_im], axis=-1)], axis=1)
    cc = jnp.concatenate([cf_re, -cf_im, cr_re, -cr_im], axis=-1)

    t_exp = jnp.full((1,), float(T), F32)
    (f_re_t, f_im_t), (r_re_t, r_im_t) = powers(0, t_exp), powers(1, t_exp)
    f_re_t, f_im_t, r_re_t, r_im_t = (v[:, 0, :] for v in (f_re_t, f_im_t, r_re_t, r_im_t))
    at = jnp.stack([jnp.concatenate([f_re_t, f_re_t], -1), jnp.concatenate([-f_im_t, f_im_t], -1),
                    jnp.concatenate([r_re_t, r_re_t], -1), jnp.concatenate([-r_im_t, r_im_t], -1)],
                   axis=1)
    dv = jnp.repeat(d_skip.astype(F32).reshape(G, H), T, axis=-1).reshape(G, 1, H * T)
    return pta, ptb, rows, cc, at, dv


ROUTE_E1, ROUTE_E2, ROUTE_W1, ROUTE_W2 = 0, 1, 2, 3


def _top2_route(logits):
    lane = lax.broadcasted_iota(jnp.int32, logits.shape, 1)
    lg = jnp.where(lane < N_EXPERTS, logits, -jnp.inf)
    m1 = jnp.max(lg, axis=-1, keepdims=True)
    i1 = jnp.min(jnp.where(lg == m1, lane, LANES), axis=-1, keepdims=True)
    lg2 = jnp.where(lane == i1, -jnp.inf, lg)
    m2 = jnp.max(lg2, axis=-1, keepdims=True)
    i2 = jnp.min(jnp.where(lg2 == m2, lane, LANES), axis=-1, keepdims=True)
    e2 = jnp.exp(m2 - m1)
    w1 = 1.0 / (1.0 + e2)
    rec = jnp.where(lane == ROUTE_E1, i1.astype(F32), 0.0)
    rec = jnp.where(lane == ROUTE_E2, i2.astype(F32), rec)
    rec = jnp.where(lane == ROUTE_W1, w1, rec)
    return jnp.where(lane == ROUTE_W2, e2 * w1, rec)


def _pack_bf16_pairs(v):
    k = v.shape[1] // 2
    bits = pltpu.bitcast(v.astype(BF16).astype(F32), jnp.uint32)
    return (bits[:, :k] & jnp.uint32(0xFFFF0000)) | (bits[:, k:] >> 16)


def _unpack_bf16_pairs(w):
    hi = pltpu.bitcast(w & jnp.uint32(0xFFFF0000), F32)
    lo = pltpu.bitcast(w << 16, F32)
    return hi, lo


def _store_packed(ref, v):
    words = _pack_bf16_pairs(v)
    for s in range(SC_SPLIT):
        ref[s] = words[:, s * SC_ROW:(s + 1) * SC_ROW]


def _load_packed(ref):
    return _unpack_bf16_pairs(jnp.concatenate([ref[s] for s in range(SC_SPLIT)], axis=-1))


def _mix_kernel(*refs, with_router, n_x):
    rest = refs[n_x:]
    if with_router:
        (yst_ref, ya_ref, mod_ref, wglut_ref, bglu_ref, wout_ref, gffn_ref, router_ref,
         x1_ref, h2_ref, gate_ref) = rest
    else:
        (yst_ref, ya_ref, mod_ref, wglut_ref, bglu_ref, wout_ref, gffn_ref,
         x1_ref, h2_ref) = rest
    zt = jax.nn.gelu(yst_ref[...], approximate=True)
    glt = zt * jax.nn.sigmoid(
        jnp.dot(wglut_ref[...], zt.astype(BF16), preferred_element_type=F32) + bglu_ref[...])
    mix = jnp.concatenate([glt.T.astype(BF16), ya_ref[...]], axis=-1)
    o = jnp.dot(mix, wout_ref[...], preferred_element_type=F32)
    x1 = _tile_rows(refs, n_x) + mod_ref[0, 2:3, :] * o
    x1_ref[...] = x1
    h2 = _rms(x1, gffn_ref[...]) * (1.0 + mod_ref[0, 4:5, :]) + mod_ref[0, 3:4, :]
    if not with_router:
        h2_ref[...] = h2.astype(h2_ref.dtype)
    else:
        _store_packed(h2_ref, h2)
        h_hi = h2.astype(BF16)
        h_lo = (h2 - h_hi.astype(F32)).astype(BF16)
        o1 = jnp.dot(h_hi, router_ref[0], preferred_element_type=F32)
        o2 = jnp.dot(h_lo, router_ref[1], preferred_element_type=F32)
        logits = o1 + pltpu.roll(o1, LANES - N_EXPERTS, 1) + o2
        gate_ref[...] = _top2_route(logits)


def _mix_call(xs, y_ssm_t, y_att, mod, wglut, bglu, wout, gffn, router, *, n_tiles):
    row = lambda i: (i, 0)
    out_row = row
    with_router = router is not None
    in_specs = _x_specs(len(xs)) + [
        pl.BlockSpec((D_SSM, TM), lambda i: (0, i)),
        pl.BlockSpec((TM, D_ATT), row),
        pl.BlockSpec((1, 6, D_MODEL), lambda i: (_mod_row(i), 0, 0)),
        _const_spec((D_SSM, D_SSM)),
        _const_spec((D_SSM, 1)),
        _const_spec((D_MIX, D_MODEL)),
        _const_spec((1, D_MODEL)),
    ]
    args = [*xs, y_ssm_t, y_att, mod, wglut, bglu, wout, gffn]
    out_specs = [pl.BlockSpec((TM, D_MODEL), out_row)]
    out_shape = [jax.ShapeDtypeStruct((n_tiles * TM, D_MODEL), F32)]
    if with_router:
        out_specs.append(pl.BlockSpec((SC_SPLIT, TM, SC_ROW), lambda i: (0, i, 0)))
        out_shape.append(jax.ShapeDtypeStruct((SC_SPLIT, n_tiles * TM, SC_ROW), jnp.uint32))
    else:
        out_specs.append(pl.BlockSpec((TM, D_MODEL), out_row))
        out_shape.append(jax.ShapeDtypeStruct((n_tiles * TM, D_MODEL), BF16))
    if with_router:
        in_specs.append(_const_spec((2, D_MODEL, LANES)))
        args.append(router)
        out_specs.append(pl.BlockSpec((TM, LANES), out_row))
        out_shape.append(jax.ShapeDtypeStruct((n_tiles * TM, LANES), F32))
    return pl.pallas_call(
        functools.partial(_mix_kernel, with_router=with_router, n_x=len(xs)),
        grid=(n_tiles,),
        in_specs=in_specs,
        out_specs=out_specs,
        out_shape=out_shape,
        compiler_params=_cparams("parallel"),
        name="mix_out",
    )(*args)


def _swiglu(h, w1_at, w3_at, w2_at, d_ff):
    acc = jnp.zeros((h.shape[0], D_MODEL), F32)
    for lo in range(0, d_ff, FF_CHUNK):
        sl = slice(lo, min(lo + FF_CHUNK, d_ff))
        a = jnp.dot(h, w1_at(sl), preferred_element_type=F32)
        b = jnp.dot(h, w3_at(sl), preferred_element_type=F32)
        g = (a * jax.nn.sigmoid(a) * b).astype(BF16)
        acc = acc + jnp.dot(g, w2_at(sl), preferred_element_type=F32)
    return acc


def _ffn_kernel(h_ref, x1_ref, mod_ref, w1_ref, w3_ref, w2_ref, o_ref):
    acc = _swiglu(h_ref[...], lambda sl: w1_ref[:, sl], lambda sl: w3_ref[:, sl],
                  lambda sl: w2_ref[sl, :], D_FF)
    o_ref[...] = x1_ref[...] + mod_ref[0, 5:6, :] * acc


def _ffn_call(h2, x1, mod, w1, w3, w2):
    n_tiles = N_TOT // TM
    row = lambda i: (i, 0)
    return pl.pallas_call(
        _ffn_kernel,
        grid=(n_tiles,),
        in_specs=[
            pl.BlockSpec((TM, D_MODEL), row),
            pl.BlockSpec((TM, D_MODEL), row),
            pl.BlockSpec((1, 6, D_MODEL), lambda i: (_mod_row(i), 0, 0)),
            _const_spec((D_MODEL, D_FF)),
            _const_spec((D_MODEL, D_FF)),
            _const_spec((D_FF, D_MODEL)),
        ],
        out_specs=pl.BlockSpec((TM, D_MODEL), row),
        out_shape=jax.ShapeDtypeStruct((N_TOT, D_MODEL), F32),
        compiler_params=_cparams("parallel"),
        name="ffn_dense",
    )(h2, x1, mod, w1, w3, w2)


EXPERT_BLK = 512
N_SORTED = 2 * N_LAT + N_EXPERTS * EXPERT_BLK
N_EXPERT_BLKS = N_SORTED // EXPERT_BLK
PACKED = D_MODEL // 2
SC_ROW = 256
SC_SPLIT = PACKED // SC_ROW
SC_WIN = 128


def _route_kernel(r_ref, dest_ref, cnt_ref, carry_scr, off_scr):
    phase, i = pl.program_id(0), pl.program_id(1)
    lane = lax.broadcasted_iota(jnp.int32, (TM, LANES), 1)
    e1 = r_ref[:, ROUTE_E1:ROUTE_E1 + 1].astype(jnp.int32)
    e2 = r_ref[:, ROUTE_E2:ROUTE_E2 + 1].astype(jnp.int32)
    picked = jnp.where((lane == e1) | (lane == e2), 1.0, 0.0)
    tile_cnt = jnp.sum(picked, axis=0, keepdims=True)

    @pl.when((phase == 0) & (i == 0))
    def _():
        carry_scr[...] = jnp.zeros_like(carry_scr)

    @pl.when(phase == 0)
    def _():
        carry_scr[...] += tile_cnt

    @pl.when((phase == 1) & (i == 0))
    def _():
        cnt = carry_scr[...]
        cnt_ref[...] = jnp.broadcast_to(cnt, cnt_ref.shape)
        padded = jnp.floor((cnt + (EXPERT_BLK - 1)) * (1.0 / EXPERT_BLK)) * EXPERT_BLK
        before = (lax.broadcasted_iota(jnp.int32, (LANES, LANES), 0)
                  < lax.broadcasted_iota(jnp.int32, (LANES, LANES), 1)).astype(F32)
        off = jnp.dot(jnp.broadcast_to(padded, (8, LANES)), before, precision=HI,
                      preferred_element_type=F32)
        off_scr[...] = off[0:1, :]
        carry_scr[...] = jnp.zeros_like(carry_scr)

    @pl.when(phase == 1)
    def _():
        earlier = (lax.broadcasted_iota(jnp.int32, (TM, TM), 1)
                   < lax.broadcasted_iota(jnp.int32, (TM, TM), 0)).astype(BF16)
        rank = jnp.dot(earlier, picked.astype(BF16), preferred_element_type=F32)
        slot = rank + carry_scr[...] + off_scr[...]
        d1 = jnp.sum(jnp.where(lane == e1, slot, 0.0), axis=-1, keepdims=True)
        d2 = jnp.sum(jnp.where(lane == e2, slot, 0.0), axis=-1, keepdims=True)
        dest_ref[...] = jnp.where(lane == 0, d1, jnp.where(lane == 1, d2, 0.0)).astype(jnp.int32)
        carry_scr[...] += tile_cnt


def _route_call(route):
    n_tiles = N_LAT // TM
    return pl.pallas_call(
        _route_kernel,
        grid=(2, n_tiles),
        in_specs=[pl.BlockSpec((TM, LANES), lambda p, i: (i, 0))],
        out_specs=[pl.BlockSpec((TM, LANES), lambda p, i: (p * i, 0)),
                   pl.BlockSpec((8, LANES), lambda p, i: (0, 0))],
        out_shape=[jax.ShapeDtypeStruct((N_LAT, LANES), jnp.int32),
                   jax.ShapeDtypeStruct((8, LANES), F32)],
        scratch_shapes=[pltpu.VMEM((1, LANES), F32), pltpu.VMEM((1, LANES), F32)],
        compiler_params=_cparams("arbitrary", "arbitrary"),
        name="moe_route",
    )(route)


def _sc_mesh():
    return plsc.VectorSubcoreMesh(core_axis_name="core", subcore_axis_name="subcore")


def _sc_scatter(x, idx_a, idx_b, n_out):
    n = x.shape[0]

    @pl.kernel(out_type=jax.ShapeDtypeStruct((n_out, SC_ROW), x.dtype), mesh=_sc_mesh(),
               scratch_types=[])
    def scatter(x_hbm, a_hbm, b_hbm, o_hbm):
        def body(x_vmem, a_vmem, b_vmem):
            pltpu.sync_copy(x_vmem, o_hbm.at[a_vmem.at[0]])
            pltpu.sync_copy(x_vmem, o_hbm.at[b_vmem.at[0]])

        pltpu.emit_pipeline(
            body, grid=(n // SC_WIN,),
            in_specs=[pl.BlockSpec((SC_WIN, SC_ROW), lambda i: (i, 0)),
                      pl.BlockSpec((1, SC_WIN), lambda i: (0, i)),
                      pl.BlockSpec((1, SC_WIN), lambda i: (0, i))],
            out_specs=[],
            core_axis_name=("core", "subcore"),
            dimension_semantics=(pltpu.PARALLEL,),
        )(x_hbm, a_hbm, b_hbm)

    return scatter(x, idx_a.reshape(1, n), idx_b.reshape(1, n))


def _sc_gather(y, idx):
    n = idx.shape[0]

    @pl.kernel(out_type=jax.ShapeDtypeStruct((n, SC_ROW), y.dtype), mesh=_sc_mesh(),
               scratch_types=[])
    def gather(y_hbm, i_hbm, o_hbm):
        def body(i_vmem, o_vmem):
            pltpu.sync_copy(y_hbm.at[i_vmem.at[0]], o_vmem)

        pltpu.emit_pipeline(
            body, grid=(n // SC_WIN,),
            in_specs=[pl.BlockSpec((1, SC_WIN), lambda i: (0, i))],
            out_specs=[pl.BlockSpec((SC_WIN, SC_ROW), lambda i: (i, 0))],
            core_axis_name=("core", "subcore"),
            dimension_semantics=(pltpu.PARALLEL,),
        )(i_hbm, o_hbm)

    return gather(y, idx.reshape(1, n))


W_CAST_ROWS = 128


def _experts_kernel(blk_expert_ref, n_used_ref, x_ref, w1_ref, w3_ref, w2_ref, o_ref,
                    w1_scr, w3_scr, w2_scr):
    b = pl.program_id(0)
    live = b < n_used_ref[0]
    new_expert = (b == 0) | (blk_expert_ref[b] != blk_expert_ref[jnp.maximum(b - 1, 0)])

    @pl.when(live & new_expert)
    def _():
        for src, dst in ((w1_ref, w1_scr), (w3_ref, w3_scr), (w2_ref, w2_scr)):
            for r in range(0, dst.shape[0], W_CAST_ROWS):
                dst[r:r + W_CAST_ROWS, :] = src[0, r:r + W_CAST_ROWS, :].astype(BF16)

    @pl.when(live)
    def _():
        hi, lo = _load_packed(x_ref)
        h = jnp.concatenate([hi.astype(BF16), lo.astype(BF16)], axis=-1)
        y = _swiglu(h, lambda sl: w1_scr[:, sl], lambda sl: w3_scr[:, sl],
                    lambda sl: w2_scr[sl, :], D_FF_EXPERT)
        _store_packed(o_ref, y)


def _experts_call(blk_expert, n_used, xs, w1, w3, w2):
    row = lambda b, be, nu: (0, b, 0)
    wsel = lambda b, be, nu: (be[b], 0, 0)
    single = pl.Buffered(1)
    return pl.pallas_call(
        _experts_kernel,
        grid_spec=pltpu.PrefetchScalarGridSpec(
            num_scalar_prefetch=2,
            grid=(N_EXPERT_BLKS,),
            in_specs=[
                pl.BlockSpec((SC_SPLIT, EXPERT_BLK, SC_ROW), row),
                pl.BlockSpec((1, D_MODEL, D_FF_EXPERT), wsel, pipeline_mode=single),
                pl.BlockSpec((1, D_MODEL, D_FF_EXPERT), wsel, pipeline_mode=single),
                pl.BlockSpec((1, D_FF_EXPERT, D_MODEL), wsel, pipeline_mode=single),
            ],
            out_specs=pl.BlockSpec((SC_SPLIT, EXPERT_BLK, SC_ROW), row),
            scratch_shapes=[pltpu.VMEM((D_MODEL, D_FF_EXPERT), BF16),
                            pltpu.VMEM((D_MODEL, D_FF_EXPERT), BF16),
                            pltpu.VMEM((D_FF_EXPERT, D_MODEL), BF16)],
        ),
        out_shape=jax.ShapeDtypeStruct((SC_SPLIT, N_SORTED, SC_ROW), jnp.uint32),
        compiler_params=_cparams("arbitrary"),
        name="moe_experts",
    )(blk_expert, n_used, xs, w1, w3, w2)


def _combine_kernel(x1_ref, r_ref, y_ref, mod_ref, fg_ref, o_ref):
    def expert_out(slot):
        hi, lo = _load_packed(y_ref.at[slot])
        return jnp.concatenate([hi, lo], axis=-1)
    w1 = r_ref[:, ROUTE_W1:ROUTE_W1 + 1]
    w2 = r_ref[:, ROUTE_W2:ROUTE_W2 + 1]
    y = w1 * expert_out(0) + w2 * expert_out(1)
    x2 = x1_ref[...] + mod_ref[0, 5:6, :] * y
    o_ref[...] = _rms(x2, fg_ref[...])


def _combine_call(x1, route, y_tok, mod, fg):
    n_tiles = N_LAT // TM
    row = lambda i: (i, 0)
    return pl.pallas_call(
        _combine_kernel,
        grid=(n_tiles,),
        in_specs=[
            pl.BlockSpec((TM, D_MODEL), row),
            pl.BlockSpec((TM, LANES), row),
            pl.BlockSpec((2, SC_SPLIT, TM, SC_ROW), lambda i: (0, 0, i, 0)),
            pl.BlockSpec((1, 6, D_MODEL), lambda i: (i // SEQ_TILES, 0, 0)),
            pl.BlockSpec((1, D_MODEL), lambda i: (0, 0)),
        ],
        out_specs=pl.BlockSpec((TM, D_MODEL), row),
        out_shape=jax.ShapeDtypeStruct((N_LAT, D_MODEL), F32),
        compiler_params=_cparams("parallel"),
        name="moe_combine",
    )(x1, route, y_tok, mod, fg)


def _moe_routed(h2p, x1, route, mod, w1, w3, w2, fg):
    dest, cnt = _route_call(route)
    blks = (cnt[0, :N_EXPERTS].astype(jnp.int32) + (EXPERT_BLK - 1)) // EXPERT_BLK
    blk_end = jnp.cumsum(blks)
    n_used = blk_end[-1:]
    b = jnp.minimum(jnp.arange(N_EXPERT_BLKS, dtype=jnp.int32), n_used[0] - 1)
    blk_expert = jnp.sum((b[:, None] >= blk_end[None, :]).astype(jnp.int32), axis=1)
    piece = jnp.arange(SC_SPLIT, dtype=jnp.int32)[:, None] * N_SORTED
    idx = [(piece + dest[:, slot][None, :]).reshape(SC_SPLIT * N_LAT) for slot in range(2)]
    xs = _sc_scatter(h2p.reshape(SC_SPLIT * N_LAT, SC_ROW), idx[0], idx[1], SC_SPLIT * N_SORTED)
    ys = _experts_call(blk_expert, n_used, xs.reshape(SC_SPLIT, N_SORTED, SC_ROW), w1, w3, w2)
    y_tok = _sc_gather(ys.reshape(SC_SPLIT * N_SORTED, SC_ROW), jnp.concatenate(idx))
    return _combine_call(x1, route, y_tok.reshape(2, SC_SPLIT, N_LAT, SC_ROW), mod, fg)


def _rope_partner_perm():
    perm, sign = [], []
    for j in range(QK_ROPE):
        first_half = (j % AXIS_ROPE) < ROPE_FREQS
        perm.append(j + ROPE_FREQS if first_half else j - ROPE_FREQS)
        sign.append(-1.0 if first_half else 1.0)
    return jnp.array(perm, jnp.int32), jnp.array(sign, F32)


def _rope_tables():
    t = jnp.arange(SEQ)
    row = (t // GRID_W).astype(F32)
    col = (t % GRID_W).astype(F32)
    inv_freq = ROPE_BASE ** (-2.0 * jnp.arange(ROPE_FREQS, dtype=F32) / AXIS_ROPE)
    ang = jnp.concatenate([row[:, None] * inv_freq, row[:, None] * inv_freq,
                           col[:, None] * inv_freq, col[:, None] * inv_freq], axis=1)
    cos = jnp.concatenate([jnp.cos(ang), jnp.ones((N_CTX, QK_ROPE), F32)], axis=0)
    sin = jnp.concatenate([jnp.sin(ang), jnp.zeros((N_CTX, QK_ROPE), F32)], axis=0)
    n = N_CTX + SEQ
    pad32 = jnp.zeros((n, HEAD_PAD - D_QK), F32)
    qs = ATT_SCALE * math.log2(math.e)
    cq = jnp.concatenate([jnp.full((n, QK_NOPE), qs, F32), qs * cos, pad32], axis=1)
    sq = jnp.concatenate([jnp.zeros((n, QK_NOPE), F32), qs * sin, pad32], axis=1)
    cs = jnp.concatenate([cos, sin, jnp.zeros((n, LANES - 2 * QK_ROPE), F32)], axis=1)
    return cq, sq, cs


def _layer_weights(w_in, w_uq, w_ukv):
    perm, sign = _rope_partner_perm()
    s0 = D_SSM + Q_LORA + KV_LORA
    kr_w = w_in[:, s0:s0 + QK_ROPE]
    wut = w_in[:, :D_SSM].T.astype(BF16)
    win = jnp.concatenate([w_in[:, D_SSM:s0], kr_w, kr_w[:, perm] * sign,
                           jnp.zeros((D_MODEL, LANES - 2 * QK_ROPE), F32)], axis=1).astype(BF16)
    uq = w_uq.reshape(Q_LORA, N_HEADS, D_QK)
    nope, rope = uq[..., :QK_NOPE], uq[..., QK_NOPE:]
    zpad = jnp.zeros((Q_LORA, N_HEADS, HEAD_PAD - D_QK), F32)
    wq1 = jnp.concatenate([nope, rope, zpad], axis=-1).reshape(Q_LORA, N_HEADS * HEAD_PAD).astype(BF16)
    wq2 = jnp.concatenate([jnp.zeros_like(nope), rope[..., perm] * sign, zpad], axis=-1)
    wq2 = wq2.reshape(Q_LORA, N_HEADS * HEAD_PAD).astype(BF16)
    ukv = w_ukv.reshape(KV_LORA, N_HEADS, QK_NOPE + V_HEAD)
    wk = jnp.concatenate([ukv[..., :QK_NOPE], jnp.zeros((KV_LORA, N_HEADS, HEAD_PAD - QK_NOPE), F32)],
                         axis=-1).reshape(KV_LORA, N_HEADS * HEAD_PAD)
    eye = jnp.eye(QK_ROPE, dtype=F32)
    place = jnp.concatenate([jnp.zeros((QK_ROPE, QK_NOPE), F32), eye,
                             jnp.zeros((QK_ROPE, HEAD_PAD - D_QK), F32)], axis=1)
    place = jnp.tile(place, (1, N_HEADS))
    spread = jnp.concatenate([place, place, jnp.zeros((LANES - 2 * QK_ROPE, N_HEADS * HEAD_PAD), F32)], 0)
    wkk = jnp.concatenate([wk, spread], axis=0).astype(BF16)
    wv = jnp.concatenate([ukv[..., QK_NOPE:], jnp.zeros((KV_LORA, N_HEADS, HEAD_PAD - V_HEAD), F32)],
                         axis=-1).reshape(KV_LORA, N_HEADS * HEAD_PAD).astype(BF16)
    return wut, win, wq1, wq2, wkk, wv


def kernel(x, c, ctx, c_ctx, w_ada, b_ada, norm_mix, norm_ffn, w_in, q_norm, kv_norm, w_uq, w_ukv,
           ssm_a_re, ssm_a_im, ssm_log_dt, ssm_b_re, ssm_b_im, ssm_c_re, ssm_c_im, ssm_d, w_glu,
           b_glu, w_out, ffn_w1, ffn_w3, ffn_w2, moe_router, moe_w1, moe_w3, moe_w2, final_norm):
    assert x.shape == (BATCH, SEQ, D_MODEL) and ctx.shape == (BATCH, CTX_LEN, D_MODEL)
    cond = jnp.concatenate([c, c_ctx[None, :], jnp.zeros((MOD_ROWS - BATCH - 1, D_MODEL), F32)], axis=0)
    mod_all = _ada_call(cond, w_ada, b_ada).reshape(DEPTH, MOD_ROWS, 6, D_MODEL)
    cq_t, sq_t, cs_t = _rope_tables()
    xs = (x.reshape(N_LAT, D_MODEL), ctx.reshape(N_CTX, D_MODEL))

    out = None
    for i in range(DEPTH):
        last = i == DEPTH - 1
        mod = mod_all[i]
        wut, win, wq1, wq2, wkk, wv = _layer_weights(w_in[i], w_uq[i], w_ukv[i])
        u_t, q, k, v = _inproj_call(xs, mod, norm_mix[i][None, :], wut, win, q_norm[i][None, :],
                                    kv_norm[i][None, :], wq1, wq2, wkk, wv, cq_t, sq_t, cs_t)
        tabs = _ssm_tables(ssm_a_re[i], ssm_a_im[i], ssm_log_dt[i], ssm_b_re[i], ssm_b_im[i],
                           ssm_c_re[i], ssm_c_im[i], ssm_d[i])
        y_ssm_t = _ssm_call(u_t.reshape(D_SSM, N_CHUNK, SSM_T), *tabs).reshape(D_SSM, N_TOT)
        y_att = _attn_latent_call(q, k, v)
        if last:
            n_tiles = LAT_TILES
        else:
            y_att = _attn_ctx_call(q, k, v, y_att)
            n_tiles = N_TOT // TM
        j = i // 2
        if i % 2 == 0:
            router = None
        else:
            r = moe_router[j]
            r_top = lax.bitcast_convert_type(
                lax.bitcast_convert_type(r, jnp.uint32) & jnp.uint32(0xFFFF0000), F32)
            r_hi = r_top.astype(BF16)
            r_lo = (r - r_top).astype(BF16)
            zr = jnp.zeros((D_MODEL, LANES - 2 * N_EXPERTS), BF16)
            router = jnp.stack([jnp.concatenate([r_hi, r_lo, zr], axis=1),
                                jnp.concatenate([r_hi, jnp.zeros_like(r_lo), zr], axis=1)])
        res = _mix_call(xs, y_ssm_t, y_att, mod, w_glu[i].T.astype(BF16), b_glu[i][:, None],
                        w_out[i].astype(BF16), norm_ffn[i][None, :], router, n_tiles=n_tiles)
        if i % 2 == 0:
            assert not last
            x1, h2 = res
            xs = (_ffn_call(h2, x1, mod, ffn_w1[j].astype(BF16), ffn_w3[j].astype(BF16),
                            ffn_w2[j].astype(BF16)),)
        else:
            assert last
            x1, h2p, route = res
            out = _moe_routed(h2p, x1, route, mod, moe_w1[j], moe_w3[j], moe_w2[j],
                              final_norm[None, :])
    return out.reshape(BATCH, SEQ, D_MODEL)
```

```python
import functools
import math

import jax
import jax.numpy as jnp
from jax import lax
from jax.experimental import pallas as pl
from jax.experimental.pallas import tpu as pltpu
from jax.experimental.pallas import tpu_sc as plsc

D_MODEL = 1024
BATCH = 4
SEQ = 8192
DEPTH = 2
GRID_W = 64
CTX_LEN = 256
D_SSM = 512
SSM_GROUP = 16
N_SSM_GROUPS = D_SSM // SSM_GROUP
SSM_STATE = 64
N_HEADS = 8
QK_NOPE = 64
QK_ROPE = 32
V_HEAD = 64
Q_LORA = 256
KV_LORA = 128
D_QK = QK_NOPE + QK_ROPE
D_ATT = N_HEADS * V_HEAD
D_MIX = D_SSM + D_ATT
D_IN = D_SSM + Q_LORA + KV_LORA + QK_ROPE
AXIS_ROPE = QK_ROPE // 2
ROPE_FREQS = AXIS_ROPE // 2
ROPE_BASE = 10000.0
ATT_SCALE = 1.0 / math.sqrt(D_QK)
D_FF = 2816
N_EXPERTS = 8
D_FF_EXPERT = 1408
EPS = 1e-6

N_CTX = BATCH * CTX_LEN
N_LAT = BATCH * SEQ
N_TOT = N_CTX + N_LAT

LANES = 128
HEAD_PAD = 128
TM = 512
LAT_TILES = N_LAT // TM
SEQ_TILES = SEQ // TM
TQ = 512
TK = 2048
SSM_T = 128
N_CHUNK_LAT = N_LAT // SSM_T
N_CHUNK = N_TOT // SSM_T
SSM_K = SSM_GROUP * SSM_T
FF_CHUNK = 256
N_FF_CHUNKS = D_FF // FF_CHUNK
D_IN_REST = Q_LORA + KV_LORA + LANES
MOD_ROWS = 8
VMEM_LIMIT = 52 * 1024 * 1024

F32 = jnp.float32
BF16 = jnp.bfloat16
HI = lax.Precision.HIGHEST


def _cparams(*sem):
    return pltpu.CompilerParams(dimension_semantics=sem, vmem_limit_bytes=VMEM_LIMIT)


def _const_spec(shape):
    nd = len(shape)
    return pl.BlockSpec(shape, lambda *_: (0,) * nd, pipeline_mode=pl.Buffered(1))


def _mod_row(i):
    return jnp.where(i < LAT_TILES, i // SEQ_TILES, BATCH)


def _pos_tile(i):
    return jnp.where(i < LAT_TILES, i % SEQ_TILES, SEQ_TILES + i - LAT_TILES)


def _rms(x, g):
    ms = jnp.mean(x * x, axis=-1, keepdims=True)
    return x * lax.rsqrt(ms + EPS) * g


ADA_TN = 1536


def _ada_kernel(c_ref, w_ref, b_ref, o_ref):
    c = c_ref[...]
    s = c * jax.nn.sigmoid(c)
    o_ref[0] = jnp.dot(s, w_ref[0], precision=HI, preferred_element_type=F32) + b_ref[0]


def _ada_call(cond, w_ada, b_ada):
    n_col = 6 * D_MODEL // ADA_TN
    return pl.pallas_call(
        _ada_kernel,
        grid=(DEPTH, n_col),
        in_specs=[
            pl.BlockSpec((MOD_ROWS, D_MODEL), lambda l, j: (0, 0)),
            pl.BlockSpec((1, D_MODEL, ADA_TN), lambda l, j: (l, 0, j)),
            pl.BlockSpec((1, 1, ADA_TN), lambda l, j: (l, 0, j)),
        ],
        out_specs=pl.BlockSpec((1, MOD_ROWS, ADA_TN), lambda l, j: (l, 0, j)),
        out_shape=jax.ShapeDtypeStruct((DEPTH, MOD_ROWS, 6 * D_MODEL), F32),
        compiler_params=_cparams("arbitrary", "arbitrary"),
        name="ada_mod",
    )(cond, w_ada, b_ada.reshape(DEPTH, 1, 6 * D_MODEL))


def _tile_rows(refs, n_x):
    if n_x == 1:
        return refs[0][...]
    return jnp.where(pl.program_id(0) < LAT_TILES, refs[0][...], refs[1][...])


def _x_specs(n_x):
    if n_x == 1:
        return [pl.BlockSpec((TM, D_MODEL), lambda i: (i, 0))]
    return [pl.BlockSpec((TM, D_MODEL), lambda i: (jnp.minimum(i, LAT_TILES - 1), 0)),
            pl.BlockSpec((TM, D_MODEL), lambda i: (jnp.maximum(i - LAT_TILES, 0), 0))]


def _inproj_kernel(*refs, n_x):
    (mod_ref, g_ref, wut_ref, win_ref, qg_ref, kvg_ref, wq1_ref, wq2_ref, wkk_ref, wv_ref,
     cq_ref, sq_ref, cs_ref, ut_ref, q_ref, k_ref, v_ref) = refs[n_x:]
    x = _tile_rows(refs, n_x)
    sh = mod_ref[0, 0:1, :]
    sc = mod_ref[0, 1:2, :]
    xm = (_rms(x, g_ref[...]) * (1.0 + sc) + sh).astype(BF16)
    ut_ref[...] = lax.dot_general(wut_ref[...], xm, (((1,), (1,)), ((), ())),
                                  preferred_element_type=F32)
    z = jnp.dot(xm, win_ref[...], preferred_element_type=F32)
    qn = _rms(z[:, :Q_LORA], qg_ref[...]).astype(BF16)
    kvn = _rms(z[:, Q_LORA:Q_LORA + KV_LORA], kvg_ref[...]).astype(BF16)
    krr = (z[:, Q_LORA + KV_LORA:] * cs_ref[...]).astype(BF16)
    q1 = jnp.dot(qn, wq1_ref[...], preferred_element_type=F32)
    q2 = jnp.dot(qn, wq2_ref[...], preferred_element_type=F32)
    cq = cq_ref[...]
    sq = sq_ref[...]
    for h in range(N_HEADS):
        sl = slice(h * HEAD_PAD, (h + 1) * HEAD_PAD)
        q_ref[:, sl] = (q1[:, sl] * cq + q2[:, sl] * sq).astype(q_ref.dtype)
    kin = jnp.concatenate([kvn, krr], axis=-1)
    k_ref[...] = jnp.dot(kin, wkk_ref[...], preferred_element_type=F32).astype(k_ref.dtype)
    vv = jnp.dot(kvn, wv_ref[...], preferred_element_type=F32)
    lane = lax.broadcasted_iota(jnp.int32, vv.shape, 1)
    v_ref[...] = jnp.where(lane % HEAD_PAD == V_HEAD, 1.0, vv).astype(v_ref.dtype)


def _inproj_call(xs, mod, g_mix, wut, win, qg, kvg, wq1, wq2, wkk, wv, cq_t, sq_t, cs_t):
    n_tiles = N_TOT // TM
    row = lambda i: (i, 0)
    pos = lambda i: (_pos_tile(i), 0)
    return pl.pallas_call(
        functools.partial(_inproj_kernel, n_x=len(xs)),
        grid=(n_tiles,),
        in_specs=_x_specs(len(xs)) + [
            pl.BlockSpec((1, 6, D_MODEL), lambda i: (_mod_row(i), 0, 0)),
            _const_spec((1, D_MODEL)),
            _const_spec((D_SSM, D_MODEL)),
            _const_spec((D_MODEL, D_IN_REST)),
            _const_spec((1, Q_LORA)),
            _const_spec((1, KV_LORA)),
            _const_spec((Q_LORA, N_HEADS * HEAD_PAD)),
            _const_spec((Q_LORA, N_HEADS * HEAD_PAD)),
            _const_spec((2 * KV_LORA, N_HEADS * HEAD_PAD)),
            _const_spec((KV_LORA, N_HEADS * HEAD_PAD)),
            pl.BlockSpec((TM, LANES), pos),
            pl.BlockSpec((TM, LANES), pos),
            pl.BlockSpec((TM, LANES), pos),
        ],
        out_specs=[
            pl.BlockSpec((D_SSM, TM), lambda i: (0, i)),
            pl.BlockSpec((TM, N_HEADS * HEAD_PAD), row),
            pl.BlockSpec((TM, N_HEADS * HEAD_PAD), row),
            pl.BlockSpec((TM, N_HEADS * HEAD_PAD), row),
        ],
        out_shape=[
            jax.ShapeDtypeStruct((D_SSM, N_TOT), F32),
            jax.ShapeDtypeStruct((N_TOT, N_HEADS * HEAD_PAD), BF16),
            jax.ShapeDtypeStruct((N_TOT, N_HEADS * HEAD_PAD), BF16),
            jax.ShapeDtypeStruct((N_TOT, N_HEADS * HEAD_PAD), BF16),
        ],
        compiler_params=_cparams("parallel"),
        name="in_proj",
    )(*xs, mod, g_mix, wut, win, qg, kvg, wq1, wq2, wkk, wv, cq_t, sq_t, cs_t)


def _attn_kernel(*refs, n_chunks, tq):
    if n_chunks:
        q_ref, k_ref, v_ref, kc_ref, vc_ref, o_ref, s_scr = refs
    else:
        q_ref, kc_ref, vc_ref, _, o_ref, s_scr = refs
    heads =[slice(hh * HEAD_PAD, (hh + 1) * HEAD_PAD) for hh in range(2)]
    qs = [q_ref[:, sl] for sl in heads]

    def put_scores(slot, k_at, width):
        for hh in range(2):
            s_scr[slot, hh, :, :width] = lax.dot_general(
                qs[hh], k_at(heads[hh]), (((1,), (1,)), ((), ())), preferred_element_type=F32)

    def consume(carry, slot, v_at, width):
        new = []
        for hh in range(2):
            m, acc = carry[hh]
            s = s_scr[slot, hh, :, :width]
            m_new = jnp.maximum(m, jnp.max(s, axis=-1, keepdims=True))
            alpha = jnp.exp2(m - m_new)
            p = jnp.exp2(s - m_new).astype(BF16)
            acc = alpha * acc + jnp.dot(p, v_at(heads[hh]), preferred_element_type=F32)
            new.append((m_new, acc))
        return tuple(new)

    def chunk(c):
        rows = pl.ds(pl.multiple_of(c * TK, TK), TK)
        return (lambda sl: k_ref[rows, sl]), (lambda sl: v_ref[rows, sl])

    ctx_k, ctx_v = (lambda sl: kc_ref[:, sl]), (lambda sl: vc_ref[:, sl])
    carry = tuple((jnp.full((tq, 1), -jnp.inf, F32), jnp.zeros((tq, HEAD_PAD), F32))
                  for _ in range(2))
    if n_chunks:
        assert n_chunks % 2 == 0
        put_scores(0, chunk(0)[0], TK)

        def body(j, carry):
            c0 = 2 * j
            put_scores(1, chunk(c0 + 1)[0], TK)
            carry = consume(carry, 0, chunk(c0)[1], TK)
            put_scores(0, chunk(c0 + 2)[0], TK)
            return consume(carry, 1, chunk(c0 + 1)[1], TK)

        carry = lax.fori_loop(0, n_chunks // 2 - 1, body, carry)
        put_scores(1, chunk(n_chunks - 1)[0], TK)
        carry = consume(carry, 0, chunk(n_chunks - 2)[1], TK)
        put_scores(0, ctx_k, CTX_LEN)
        carry = consume(carry, 1, chunk(n_chunks - 1)[1], TK)
    else:
        put_scores(0, ctx_k, CTX_LEN)
    carry = consume(carry, 0, ctx_v, CTX_LEN)
    outs = [acc[:, :V_HEAD] / acc[:, V_HEAD:V_HEAD + 1] for _, acc in carry]
    o_ref[...] = jnp.concatenate(outs, axis=-1).astype(o_ref.dtype)


def _attn_latent_call(q, k, v):
    qt = SEQ // TQ
    ctx0 = N_LAT // CTX_LEN
    return pl.pallas_call(
        functools.partial(_attn_kernel, n_chunks=SEQ // TK, tq=TQ),
        grid=(BATCH, N_HEADS // 2, qt),
        in_specs=[
            pl.BlockSpec((TQ, 2 * HEAD_PAD), lambda b, h, i: (b * qt + i, h)),
            pl.BlockSpec((SEQ, 2 * HEAD_PAD), lambda b, h, i: (b, h), pipeline_mode=pl.Buffered(1)),
            pl.BlockSpec((SEQ, 2 * HEAD_PAD), lambda b, h, i: (b, h), pipeline_mode=pl.Buffered(1)),
            pl.BlockSpec((CTX_LEN, 2 * HEAD_PAD), lambda b, h, i: (ctx0 + b, h)),
            pl.BlockSpec((CTX_LEN, 2 * HEAD_PAD), lambda b, h, i: (ctx0 + b, h)),
        ],
        out_specs=pl.BlockSpec((TQ, 2 * V_HEAD), lambda b, h, i: (b * qt + i, h)),
        out_shape=jax.ShapeDtypeStruct((N_TOT, D_ATT), BF16),
        scratch_shapes=[pltpu.VMEM((2, 2, TQ, TK), F32)],
        compiler_params=_cparams("parallel", "parallel", "arbitrary"),
        name="attn_latent",
    )(q, k, v, k, v)


def _attn_ctx_call(q, k, v, y_att):
    ctx0 = N_LAT // CTX_LEN
    return pl.pallas_call(
        functools.partial(_attn_kernel, n_chunks=0, tq=CTX_LEN),
        grid=(BATCH, N_HEADS // 2),
        in_specs=[
            pl.BlockSpec((CTX_LEN, 2 * HEAD_PAD), lambda b, h: (ctx0 + b, h)),
            pl.BlockSpec((CTX_LEN, 2 * HEAD_PAD), lambda b, h: (ctx0 + b, h)),
            pl.BlockSpec((CTX_LEN, 2 * HEAD_PAD), lambda b, h: (ctx0 + b, h)),
            pl.BlockSpec(memory_space=pl.ANY),
        ],
        out_specs=pl.BlockSpec((CTX_LEN, 2 * V_HEAD), lambda b, h: (ctx0 + b, h)),
        out_shape=jax.ShapeDtypeStruct((N_TOT, D_ATT), BF16),
        input_output_aliases={3: 0},
        scratch_shapes=[pltpu.VMEM((1, 2, CTX_LEN, CTX_LEN), F32)],
        compiler_params=_cparams("parallel", "parallel"),
        name="attn_ctx",
    )(q, k, v, y_att)


PT_LAG, PT_INC, PT_OUT = 0, 2 * SSM_T, 3 * SSM_T
PT_ROWS = 4 * SSM_T
NT_DIMS = (((1,), (1,)), ((), ()))


def _ssm_kernel(u_ref, pta_ref, ptb_ref, rows_ref, cc_ref, at_ref, dv_ref, y_ref,
                abt_hi_scr, abt_lo_scr, wl_scr, m_scr, ws_scr, wct_scr, s_scr, h_scr):
    H, T = SSM_GROUP, SSM_T

    def split(v):
        hi = v.astype(BF16)
        return hi, (v - hi.astype(F32)).astype(BF16)

    def scaled(row0, n_rows, ra, rb):
        return pta_ref[0, row0:row0 + n_rows, :] * ra + ptb_ref[0, row0:row0 + n_rows, :] * rb

    def build_tables(i, _):
        row0 = pl.multiple_of(i * T, T)
        b_re = rows_ref[0, pl.ds(i, 1), :]
        b_im = rows_ref[0, pl.ds(H + i, 1), :]
        lag_rows = pl.ds(pl.multiple_of(i * 2 * T, 2 * T), 2 * T)
        abt_hi_scr[lag_rows, :], abt_lo_scr[lag_rows, :] = split(scaled(PT_LAG, 2 * T, b_re, b_im))
        ws_scr[pl.ds(row0, T), :] = scaled(PT_INC, T, b_re, b_im).astype(BF16)
        c_re = rows_ref[0, pl.ds(2 * H + i, 1), :]
        c_im = rows_ref[0, pl.ds(3 * H + i, 1), :]
        wct_scr[pl.ds(row0, T), :] = scaled(PT_OUT, T, c_re, c_im).astype(BF16)
        return 0

    lax.fori_loop(0, H, build_tables, 0)
    cc_hi, cc_lo = split(cc_ref[0])
    nt = lambda a, b: lax.dot_general(a, b, NT_DIMS, preferred_element_type=F32)
    wl_scr[...] = (nt(cc_hi, abt_hi_scr[...]) + nt(cc_hi, abt_lo_scr[...])
                   + nt(cc_lo, abt_hi_scr[...]))

    def build_toeplitz(ci, _):
        row0 = pl.multiple_of(ci * T, T)
        lane0 = pl.multiple_of(ci * 2 * T, 2 * T)
        for co in range(H):
            wb = jnp.broadcast_to(wl_scr[co:co + 1, pl.ds(lane0, 2 * T)], (T, 2 * T))
            toe = pltpu.roll(wb, 0, 1, stride=1, stride_axis=0)
            m_scr[pl.ds(row0, T), co * T:(co + 1) * T] = toe[:, :T].astype(BF16)
        return 0

    lax.fori_loop(0, H, build_toeplitz, 0)

    u32 = jnp.concatenate([u_ref[c] for c in range(H)], axis=1)
    u = u32.astype(BF16)
    y = jnp.dot(u, m_scr[...], preferred_element_type=F32)
    s = jnp.dot(u, ws_scr[...], preferred_element_type=F32)
    s_scr[:, 0:2 * LANES] = s
    s_scr[:, 2 * LANES:3 * LANES] = pltpu.roll(s[:, 0:LANES], SSM_STATE, 1)
    s_scr[:, 3 * LANES:4 * LANES] = pltpu.roll(s[:, LANES:2 * LANES], SSM_STATE, 1)

    def advance(h, h_sw, d, row):
        a0 = at_ref[0, 2 * d:2 * d + 1, :]
        a1 = at_ref[0, 2 * d + 1:2 * d + 2, :]
        lanes = slice(d * LANES, (d + 1) * LANES)
        lanes_sw = slice((2 + d) * LANES, (3 + d) * LANES)
        return (h * a0 + h_sw * a1 + s_scr[row:row + 1, lanes],
                h_sw * a0 - h * a1 + s_scr[row:row + 1, lanes_sw])

    n_lat, n_ctx = SEQ // T, CTX_LEN // T
    zero = jnp.zeros((1, 2 * SSM_STATE), F32)
    for b in range(BATCH):
        ctx_rows = [N_CHUNK_LAT + b * n_ctx + c for c in range(n_ctx)]
        lat_rows = [b * n_lat + k for k in range(n_lat)]
        hf, hf_sw, hr, hr_sw = zero, zero, zero, zero
        for rf, rr in zip(ctx_rows + lat_rows, ctx_rows[::-1] + lat_rows[::-1]):
            h_scr[rf:rf + 1, 0:LANES] = hf
            hf, hf_sw = advance(hf, hf_sw, 0, rf)
            h_scr[rr:rr + 1, LANES:2 * LANES] = hr
            hr, hr_sw = advance(hr, hr_sw, 1, rr)

    y = y + lax.dot_general(h_scr[...].astype(BF16), wct_scr[...], NT_DIMS,
                            preferred_element_type=F32)
    y = y + u32 * dv_ref[0]
    for c in range(H):
        y_ref[c] = y[:, c * T:(c + 1) * T]


def _ssm_call(u_t, pta, ptb, rows, cc, at, dv):
    g3 = lambda g: (g, 0, 0)
    return pl.pallas_call(
        _ssm_kernel,
        grid=(N_SSM_GROUPS,),
        in_specs=[
            pl.BlockSpec((SSM_GROUP, N_CHUNK, SSM_T), g3),
            pl.BlockSpec((1, PT_ROWS, 4 * SSM_STATE), g3),
            pl.BlockSpec((1, PT_ROWS, 4 * SSM_STATE), g3),
            pl.BlockSpec((1, 4 * SSM_GROUP, 4 * SSM_STATE), g3),
            pl.BlockSpec((1, SSM_GROUP, 4 * SSM_STATE), g3),
            pl.BlockSpec((1, 4, 2 * SSM_STATE), g3),
            pl.BlockSpec((1, 1, SSM_K), g3),
        ],
        out_specs=pl.BlockSpec((SSM_GROUP, N_CHUNK, SSM_T), g3),
        out_shape=jax.ShapeDtypeStruct((D_SSM, N_CHUNK, SSM_T), F32),
        scratch_shapes=[
            pltpu.VMEM((SSM_GROUP * 2 * SSM_T, 4 * SSM_STATE), BF16),
            pltpu.VMEM((SSM_GROUP * 2 * SSM_T, 4 * SSM_STATE), BF16),
            pltpu.VMEM((SSM_GROUP, SSM_GROUP * 2 * SSM_T), F32),
            pltpu.VMEM((SSM_K, SSM_K), BF16),
            pltpu.VMEM((SSM_K, 4 * SSM_STATE), BF16),
            pltpu.VMEM((SSM_K, 4 * SSM_STATE), BF16),
            pltpu.VMEM((N_CHUNK, 8 * SSM_STATE), F32),
            pltpu.VMEM((N_CHUNK, 4 * SSM_STATE), F32),
        ],
        compiler_params=_cparams("parallel"),
        name="s5_mixer",
    )(u_t, pta, ptb, rows, cc, at, dv)


def _ssm_tables(a_re, a_im, log_dt, b_re, b_im, c_re, c_im, d_skip):
    G, P, H, T = N_SSM_GROUPS, SSM_STATE, SSM_GROUP, SSM_T
    a_re, a_im = a_re.astype(F32), a_im.astype(F32)
    dt = jnp.exp(log_dt.astype(F32))[..., None]
    den = a_re * a_re + a_im * a_im
    mag1 = jnp.exp(dt * a_re)
    ab_re, ab_im = mag1 * jnp.cos(dt * a_im), mag1 * jnp.sin(dt * a_im)
    num_re = ab_re - 1.0
    f_re = (num_re * a_re + ab_im * a_im) / den
    f_im = (ab_im * a_re - num_re * a_im) / den
    b_re, b_im = b_re.astype(F32), b_im.astype(F32)
    bb_re = f_re[..., None] * b_re - f_im[..., None] * b_im
    bb_im = f_re[..., None] * b_im + f_im[..., None] * b_re
    c_re, c_im = c_re.astype(F32), c_im.astype(F32)
    la, th = dt * a_re, dt * a_im

    def powers(d, n):
        mag = jnp.exp(la[d][:, None, :] * n[None, :, None])
        ph = th[d][:, None, :] * n[None, :, None]
        return mag * jnp.cos(ph), mag * jnp.sin(ph)

    j = jnp.arange(2 * T)
    s_idx = jnp.arange(T)
    f_exp = jnp.concatenate([jnp.where(j < T, j, 0), T - 1 - s_idx, s_idx + 1]).astype(F32)
    r_exp = jnp.concatenate([jnp.where(j > T, 2 * T - j, 0), s_idx, T - s_idx]).astype(F32)
    ones = jnp.ones((2 * T,), F32)
    f_on = jnp.concatenate([(j < T).astype(F32), ones])
    r_on = jnp.concatenate([((j == 0) | (j > T)).astype(F32), ones])
    lane = jnp.arange(4 * P)
    is_fwd = (lane < 2 * P)[None, :]
    is_re = ((lane // P) % 2 == 0)[None, None, :]
    expo = jnp.where(is_fwd, f_exp[:, None], r_exp[:, None])[None]
    on = jnp.where(is_fwd, f_on[:, None], r_on[:, None])[None]
    la4 = jnp.concatenate([la[0], la[0], la[1], la[1]], axis=-1)[:, None, :]
    th4 = jnp.concatenate([th[0], th[0], th[1], th[1]], axis=-1)[:, None, :]
    mag = jnp.exp(la4 * expo) * on
    p_re, p_im = mag * jnp.cos(th4 * expo), mag * jnp.sin(th4 * expo)
    pta = jnp.where(is_re, p_re, p_im)
    ptb = jnp.where(is_re, -p_im, p_re)

    def per_channel(v):
        f, r = v[0].transpose(0, 2, 1), v[1].transpose(0, 2, 1)
        return jnp.concatenate([f, f, r, r], axis=-1)
    cf_re, cr_re = c_re[0], c_re[1]
    cf_im, cr_im = c_im[0], c_im[1]
    rows = jnp.concatenate([
        per_channel(bb_re), per_channel(bb_im),
        jnp.concatenate([cf_re, -cf_re, cr_re, -cr_re], axis=-1),
        jnp.concatenate([cf_im, -cf_im, cr_im, -cr_im], axis=-1)], axis=1)
    cc = jnp.concatenate([cf_re, -cf_im, cr_re, -cr_im], axis=-1)

    t_exp = jnp.full((1,), float(T), F32)
    (f_re_t, f_im_t), (r_re_t, r_im_t) = powers(0, t_exp), powers(1, t_exp)
    f_re_t, f_im_t, r_re_t, r_im_t = (v[:, 0, :] for v in (f_re_t, f_im_t, r_re_t, r_im_t))
    at = jnp.stack([jnp.concatenate([f_re_t, f_re_t], -1), jnp.concatenate([-f_im_t, f_im_t], -1),
                    jnp.concatenate([r_re_t, r_re_t], -1), jnp.concatenate([-r_im_t, r_im_t], -1)],
                   axis=1)
    dv = jnp.repeat(d_skip.astype(F32).reshape(G, H), T, axis=-1).reshape(G, 1, H * T)
    return pta, ptb, rows, cc, at, dv


ROUTE_E1, ROUTE_E2, ROUTE_W1, ROUTE_W2 = 0, 1, 2, 3


def _top2_route(logits):
    lane = lax.broadcasted_iota(jnp.int32, logits.shape, 1)
    lg = jnp.where(lane < N_EXPERTS, logits, -jnp.inf)
    m1 = jnp.max(lg, axis=-1, keepdims=True)
    i1 = jnp.min(jnp.where(lg == m1, lane, LANES), axis=-1, keepdims=True)
    lg2 = jnp.where(lane == i1, -jnp.inf, lg)
    m2 = jnp.max(lg2, axis=-1, keepdims=True)
    i2 = jnp.min(jnp.where(lg2 == m2, lane, LANES), axis=-1, keepdims=True)
    e2 = jnp.exp(m2 - m1)
    w1 = 1.0 / (1.0 + e2)
    rec = jnp.where(lane == ROUTE_E1, i1.astype(F32), 0.0)
    rec = jnp.where(lane == ROUTE_E2, i2.astype(F32), rec)
    rec = jnp.where(lane == ROUTE_W1, w1, rec)
    return jnp.where(lane == ROUTE_W2, e2 * w1, rec)


def _pack_bf16_pairs(v):
    k = v.shape[1] // 2
    bits = pltpu.bitcast(v.astype(BF16).astype(F32), jnp.uint32)
    return (bits[:, :k] & jnp.uint32(0xFFFF0000)) | (bits[:, k:] >> 16)


def _unpack_bf16_pairs(w):
    hi = pltpu.bitcast(w & jnp.uint32(0xFFFF0000), F32)
    lo = pltpu.bitcast(w << 16, F32)
    return hi, lo


def _store_packed(ref, v):
    words = _pack_bf16_pairs(v)
    for s in range(SC_SPLIT):
        ref[s] = words[:, s * SC_ROW:(s + 1) * SC_ROW]


def _load_packed(ref):
    return _unpack_bf16_pairs(jnp.concatenate([ref[s] for s in range(SC_SPLIT)], axis=-1))


def _mix_kernel(*refs, with_router, n_x):
    rest = refs[n_x:]
    if with_router:
        (yst_ref, ya_ref, mod_ref, wglut_ref, bglu_ref, wout_ref, gffn_ref, router_ref,
         x1_ref, h2_ref, gate_ref) = rest
    else:
        (yst_ref, ya_ref, mod_ref, wglut_ref, bglu_ref, wout_ref, gffn_ref,
         w1_ref, w3_ref, w2_ref, x2_ref) = rest
    zt = jax.nn.gelu(yst_ref[...], approximate=True)
    glt = zt * jax.nn.sigmoid(
        jnp.dot(wglut_ref[...], zt.astype(BF16), preferred_element_type=F32) + bglu_ref[...])
    mix = jnp.concatenate([glt.T.astype(BF16), ya_ref[...]], axis=-1)
    o = jnp.dot(mix, wout_ref[...], preferred_element_type=F32)
    x1 = _tile_rows(refs, n_x) + mod_ref[0, 2:3, :] * o
    h2 = _rms(x1, gffn_ref[...]) * (1.0 + mod_ref[0, 4:5, :]) + mod_ref[0, 3:4, :]
    if not with_router:
        acc = _swiglu(h2.astype(BF16), lambda sl: w1_ref[:, sl], lambda sl: w3_ref[:, sl],
                      lambda sl: w2_ref[sl, :], D_FF)
        x2_ref[...] = x1 + mod_ref[0, 5:6, :] * acc
    else:
        x1_ref[...] = x1
        _store_packed(h2_ref, h2)
        h_hi = h2.astype(BF16)
        h_lo = (h2 - h_hi.astype(F32)).astype(BF16)
        o1 = jnp.dot(h_hi, router_ref[0], preferred_element_type=F32)
        o2 = jnp.dot(h_lo, router_ref[1], preferred_element_type=F32)
        logits = o1 + pltpu.roll(o1, LANES - N_EXPERTS, 1) + o2
        gate_ref[...] = _top2_route(logits)


def _mix_call(xs, y_ssm_t, y_att, mod, wglut, bglu, wout, gffn, *, router=None, ffn=None, n_tiles):
    row = lambda i: (i, 0)
    out_row = row
    with_router = router is not None
    assert with_router != (ffn is not None)
    in_specs = _x_specs(len(xs)) + [
        pl.BlockSpec((D_SSM, TM), lambda i: (0, i)),
        pl.BlockSpec((TM, D_ATT), row),
        pl.BlockSpec((1, 6, D_MODEL), lambda i: (_mod_row(i), 0, 0)),
        _const_spec((D_SSM, D_SSM)),
        _const_spec((D_SSM, 1)),
        _const_spec((D_MIX, D_MODEL)),
        _const_spec((1, D_MODEL)),
    ]
    args = [*xs, y_ssm_t, y_att, mod, wglut, bglu, wout, gffn]
    out_specs = [pl.BlockSpec((TM, D_MODEL), out_row)]
    out_shape = [jax.ShapeDtypeStruct((n_tiles * TM, D_MODEL), F32)]
    if with_router:
        in_specs.append(_const_spec((2, D_MODEL, LANES)))
        args.append(router)
        out_specs += [pl.BlockSpec((SC_SPLIT, TM, SC_ROW), lambda i: (0, i, 0)),
                      pl.BlockSpec((TM, LANES), out_row)]
        out_shape += [jax.ShapeDtypeStruct((SC_SPLIT, n_tiles * TM, SC_ROW), jnp.uint32),
                      jax.ShapeDtypeStruct((n_tiles * TM, LANES), F32)]
    else:
        in_specs += [_const_spec((D_MODEL, D_FF)), _const_spec((D_MODEL, D_FF)),
                     _const_spec((D_FF, D_MODEL))]
        args += list(ffn)
    return pl.pallas_call(
        functools.partial(_mix_kernel, with_router=with_router, n_x=len(xs)),
        grid=(n_tiles,),
        in_specs=in_specs,
        out_specs=out_specs,
        out_shape=out_shape,
        compiler_params=_cparams("parallel"),
        name="mix_out",
    )(*args)


def _swiglu(h, w1_at, w3_at, w2_at, d_ff):
    acc = jnp.zeros((h.shape[0], D_MODEL), F32)
    for lo in range(0, d_ff, FF_CHUNK):
        sl = slice(lo, min(lo + FF_CHUNK, d_ff))
        a = jnp.dot(h, w1_at(sl), preferred_element_type=F32)
        b = jnp.dot(h, w3_at(sl), preferred_element_type=F32)
        g = (a * jax.nn.sigmoid(a) * b).astype(BF16)
        acc = acc + jnp.dot(g, w2_at(sl), preferred_element_type=F32)
    return acc


EXPERT_BLK = 512
N_SORTED = 2 * N_LAT + N_EXPERTS * EXPERT_BLK
N_EXPERT_BLKS = N_SORTED // EXPERT_BLK
PACKED = D_MODEL // 2
SC_ROW = 256
SC_SPLIT = PACKED // SC_ROW
SC_WIN = 128


def _route_kernel(r_ref, dest_ref, cnt_ref, carry_scr, off_scr):
    phase, i = pl.program_id(0), pl.program_id(1)
    lane = lax.broadcasted_iota(jnp.int32, (TM, LANES), 1)
    e1 = r_ref[:, ROUTE_E1:ROUTE_E1 + 1].astype(jnp.int32)
    e2 = r_ref[:, ROUTE_E2:ROUTE_E2 + 1].astype(jnp.int32)
    picked = jnp.where((lane == e1) | (lane == e2), 1.0, 0.0)
    tile_cnt = jnp.sum(picked, axis=0, keepdims=True)

    @pl.when((phase == 0) & (i == 0))
    def _():
        carry_scr[...] = jnp.zeros_like(carry_scr)

    @pl.when(phase == 0)
    def _():
        carry_scr[...] += tile_cnt

    @pl.when((phase == 1) & (i == 0))
    def _():
        cnt = carry_scr[...]
        cnt_ref[...] = jnp.broadcast_to(cnt, cnt_ref.shape)
        padded = jnp.floor((cnt + (EXPERT_BLK - 1)) * (1.0 / EXPERT_BLK)) * EXPERT_BLK
        before = (lax.broadcasted_iota(jnp.int32, (LANES, LANES), 0)
                  < lax.broadcasted_iota(jnp.int32, (LANES, LANES), 1)).astype(F32)
        off = jnp.dot(jnp.broadcast_to(padded, (8, LANES)), before, precision=HI,
                      preferred_element_type=F32)
        off_scr[...] = off[0:1, :]
        carry_scr[...] = jnp.zeros_like(carry_scr)

    @pl.when(phase == 1)
    def _():
        earlier = (lax.broadcasted_iota(jnp.int32, (TM, TM), 1)
                   < lax.broadcasted_iota(jnp.int32, (TM, TM), 0)).astype(BF16)
        rank = jnp.dot(earlier, picked.astype(BF16), preferred_element_type=F32)
        slot = rank + carry_scr[...] + off_scr[...]
        d1 = jnp.sum(jnp.where(lane == e1, slot, 0.0), axis=-1, keepdims=True)
        d2 = jnp.sum(jnp.where(lane == e2, slot, 0.0), axis=-1, keepdims=True)
        dest_ref[...] = jnp.where(lane == 0, d1, jnp.where(lane == 1, d2, 0.0)).astype(jnp.int32)
        carry_scr[...] += tile_cnt


def _route_call(route):
    n_tiles = N_LAT // TM
    return pl.pallas_call(
        _route_kernel,
        grid=(2, n_tiles),
        in_specs=[pl.BlockSpec((TM, LANES), lambda p, i: (i, 0))],
        out_specs=[pl.BlockSpec((TM, LANES), lambda p, i: (p * i, 0)),
                   pl.BlockSpec((8, LANES), lambda p, i: (0, 0))],
        out_shape=[jax.ShapeDtypeStruct((N_LAT, LANES), jnp.int32),
                   jax.ShapeDtypeStruct((8, LANES), F32)],
        scratch_shapes=[pltpu.VMEM((1, LANES), F32), pltpu.VMEM((1, LANES), F32)],
        compiler_params=_cparams("arbitrary", "arbitrary"),
        name="moe_route",
    )(route)


def _sc_mesh():
    return plsc.VectorSubcoreMesh(core_axis_name="core", subcore_axis_name="subcore")


def _sc_scatter(x, idx_a, idx_b, n_out):
    n = x.shape[0]

    @pl.kernel(out_type=jax.ShapeDtypeStruct((n_out, SC_ROW), x.dtype), mesh=_sc_mesh(),
               scratch_types=[])
    def scatter(x_hbm, a_hbm, b_hbm, o_hbm):
        def body(x_vmem, a_vmem, b_vmem):
            pltpu.sync_copy(x_vmem, o_hbm.at[a_vmem.at[0]])
            pltpu.sync_copy(x_vmem, o_hbm.at[b_vmem.at[0]])

        pltpu.emit_pipeline(
            body, grid=(n // SC_WIN,),
            in_specs=[pl.BlockSpec((SC_WIN, SC_ROW), lambda i: (i, 0)),
                      pl.BlockSpec((1, SC_WIN), lambda i: (0, i)),
                      pl.BlockSpec((1, SC_WIN), lambda i: (0, i))],
            out_specs=[],
            core_axis_name=("core", "subcore"),
            dimension_semantics=(pltpu.PARALLEL,),
        )(x_hbm, a_hbm, b_hbm)

    return scatter(x, idx_a.reshape(1, n), idx_b.reshape(1, n))


def _sc_gather(y, idx):
    n = idx.shape[0]

    @pl.kernel(out_type=jax.ShapeDtypeStruct((n, SC_ROW), y.dtype), mesh=_sc_mesh(),
               scratch_types=[])
    def gather(y_hbm, i_hbm, o_hbm):
        def body(i_vmem, o_vmem):
            pltpu.sync_copy(y_hbm.at[i_vmem.at[0]], o_vmem)

        pltpu.emit_pipeline(
            body, grid=(n // SC_WIN,),
            in_specs=[pl.BlockSpec((1, SC_WIN), lambda i: (0, i))],
            out_specs=[pl.BlockSpec((SC_WIN, SC_ROW), lambda i: (i, 0))],
            core_axis_name=("core", "subcore"),
            dimension_semantics=(pltpu.PARALLEL,),
        )(i_hbm, o_hbm)

    return gather(y, idx.reshape(1, n))


W_CAST_ROWS = 128


def _experts_kernel(blk_expert_ref, n_used_ref, x_ref, w1_ref, w3_ref, w2_ref, o_ref,
                    w1_scr, w3_scr, w2_scr):
    b = pl.program_id(0)
    live = b < n_used_ref[0]
    new_expert = (b == 0) | (blk_expert_ref[b] != blk_expert_ref[jnp.maximum(b - 1, 0)])

    @pl.when(live & new_expert)
    def _():
        for src, dst in ((w1_ref, w1_scr), (w3_ref, w3_scr), (w2_ref, w2_scr)):
            for r in range(0, dst.shape[0], W_CAST_ROWS):
                dst[r:r + W_CAST_ROWS, :] = src[0, r:r + W_CAST_ROWS, :].astype(BF16)

    @pl.when(live)
    def _():
        hi, lo = _load_packed(x_ref)
        h = jnp.concatenate([hi.astype(BF16), lo.astype(BF16)], axis=-1)
        y = _swiglu(h, lambda sl: w1_scr[:, sl], lambda sl: w3_scr[:, sl],
                    lambda sl: w2_scr[sl, :], D_FF_EXPERT)
        _store_packed(o_ref, y)


def _experts_call(blk_expert, n_used, xs, w1, w3, w2):
    row = lambda b, be, nu: (0, b, 0)
    wsel = lambda b, be, nu: (be[b], 0, 0)
    single = pl.Buffered(1)
    return pl.pallas_call(
        _experts_kernel,
        grid_spec=pltpu.PrefetchScalarGridSpec(
            num_scalar_prefetch=2,
            grid=(N_EXPERT_BLKS,),
            in_specs=[
                pl.BlockSpec((SC_SPLIT, EXPERT_BLK, SC_ROW), row),
                pl.BlockSpec((1, D_MODEL, D_FF_EXPERT), wsel, pipeline_mode=single),
                pl.BlockSpec((1, D_MODEL, D_FF_EXPERT), wsel, pipeline_mode=single),
                pl.BlockSpec((1, D_FF_EXPERT, D_MODEL), wsel, pipeline_mode=single),
            ],
            out_specs=pl.BlockSpec((SC_SPLIT, EXPERT_BLK, SC_ROW), row),
            scratch_shapes=[pltpu.VMEM((D_MODEL, D_FF_EXPERT), BF16),
                            pltpu.VMEM((D_MODEL, D_FF_EXPERT), BF16),
                            pltpu.VMEM((D_FF_EXPERT, D_MODEL), BF16)],
        ),
        out_shape=jax.ShapeDtypeStruct((SC_SPLIT, N_SORTED, SC_ROW), jnp.uint32),
        compiler_params=_cparams("arbitrary"),
        name="moe_experts",
    )(blk_expert, n_used, xs, w1, w3, w2)


def _combine_kernel(x1_ref, r_ref, y_ref, mod_ref, fg_ref, o_ref):
    def expert_out(slot):
        hi, lo = _load_packed(y_ref.at[slot])
        return jnp.concatenate([hi, lo], axis=-1)
    w1 = r_ref[:, ROUTE_W1:ROUTE_W1 + 1]
    w2 = r_ref[:, ROUTE_W2:ROUTE_W2 + 1]
    y = w1 * expert_out(0) + w2 * expert_out(1)
    x2 = x1_ref[...] + mod_ref[0, 5:6, :] * y
    o_ref[...] = _rms(x2, fg_ref[...])


def _combine_call(x1, route, y_tok, mod, fg):
    n_tiles = N_LAT // TM
    row = lambda i: (i, 0)
    return pl.pallas_call(
        _combine_kernel,
        grid=(n_tiles,),
        in_specs=[
            pl.BlockSpec((TM, D_MODEL), row),
            pl.BlockSpec((TM, LANES), row),
            pl.BlockSpec((2, SC_SPLIT, TM, SC_ROW), lambda i: (0, 0, i, 0)),
            pl.BlockSpec((1, 6, D_MODEL), lambda i: (i // SEQ_TILES, 0, 0)),
            pl.BlockSpec((1, D_MODEL), lambda i: (0, 0)),
        ],
        out_specs=pl.BlockSpec((TM, D_MODEL), row),
        out_shape=jax.ShapeDtypeStruct((N_LAT, D_MODEL), F32),
        compiler_params=_cparams("parallel"),
        name="moe_combine",
    )(x1, route, y_tok, mod, fg)


def _moe_routed(h2p, x1, route, mod, w1, w3, w2, fg):
    dest, cnt = _route_call(route)
    blks = (cnt[0, :N_EXPERTS].astype(jnp.int32) + (EXPERT_BLK - 1)) // EXPERT_BLK
    blk_end = jnp.cumsum(blks)
    n_used = blk_end[-1:]
    b = jnp.minimum(jnp.arange(N_EXPERT_BLKS, dtype=jnp.int32), n_used[0] - 1)
    blk_expert = jnp.sum((b[:, None] >= blk_end[None, :]).astype(jnp.int32), axis=1)
    piece = jnp.arange(SC_SPLIT, dtype=jnp.int32)[:, None] * N_SORTED
    idx = [(piece + dest[:, slot][None, :]).reshape(SC_SPLIT * N_LAT) for slot in range(2)]
    xs = _sc_scatter(h2p.reshape(SC_SPLIT * N_LAT, SC_ROW), idx[0], idx[1], SC_SPLIT * N_SORTED)
    ys = _experts_call(blk_expert, n_used, xs.reshape(SC_SPLIT, N_SORTED, SC_ROW), w1, w3, w2)
    y_tok = _sc_gather(ys.reshape(SC_SPLIT * N_SORTED, SC_ROW), jnp.concatenate(idx))
    return _combine_call(x1, route, y_tok.reshape(2, SC_SPLIT, N_LAT, SC_ROW), mod, fg)


def _rope_partner_perm():
    perm, sign = [], []
    for j in range(QK_ROPE):
        first_half = (j % AXIS_ROPE) < ROPE_FREQS
        perm.append(j + ROPE_FREQS if first_half else j - ROPE_FREQS)
        sign.append(-1.0 if first_half else 1.0)
    return jnp.array(perm, jnp.int32), jnp.array(sign, F32)


def _rope_tables():
    t = jnp.arange(SEQ)
    row = (t // GRID_W).astype(F32)
    col = (t % GRID_W).astype(F32)
    inv_freq = ROPE_BASE ** (-2.0 * jnp.arange(ROPE_FREQS, dtype=F32) / AXIS_ROPE)
    ang = jnp.concatenate([row[:, None] * inv_freq, row[:, None] * inv_freq,
                           col[:, None] * inv_freq, col[:, None] * inv_freq], axis=1)
    cos = jnp.concatenate([jnp.cos(ang), jnp.ones((N_CTX, QK_ROPE), F32)], axis=0)
    sin = jnp.concatenate([jnp.sin(ang), jnp.zeros((N_CTX, QK_ROPE), F32)], axis=0)
    n = N_CTX + SEQ
    pad32 = jnp.zeros((n, HEAD_PAD - D_QK), F32)
    qs = ATT_SCALE * math.log2(math.e)
    cq = jnp.concatenate([jnp.full((n, QK_NOPE), qs, F32), qs * cos, pad32], axis=1)
    sq = jnp.concatenate([jnp.zeros((n, QK_NOPE), F32), qs * sin, pad32], axis=1)
    cs = jnp.concatenate([cos, sin, jnp.zeros((n, LANES - 2 * QK_ROPE), F32)], axis=1)
    return cq, sq, cs


def _layer_weights(w_in, w_uq, w_ukv):
    perm, sign = _rope_partner_perm()
    s0 = D_SSM + Q_LORA + KV_LORA
    kr_w = w_in[:, s0:s0 + QK_ROPE]
    wut = w_in[:, :D_SSM].T.astype(BF16)
    win = jnp.concatenate([w_in[:, D_SSM:s0], kr_w, kr_w[:, perm] * sign,
                           jnp.zeros((D_MODEL, LANES - 2 * QK_ROPE), F32)], axis=1).astype(BF16)
    uq = w_uq.reshape(Q_LORA, N_HEADS, D_QK)
    nope, rope = uq[..., :QK_NOPE], uq[..., QK_NOPE:]
    zpad = jnp.zeros((Q_LORA, N_HEADS, HEAD_PAD - D_QK), F32)
    wq1 = jnp.concatenate([nope, rope, zpad], axis=-1).reshape(Q_LORA, N_HEADS * HEAD_PAD).astype(BF16)
    wq2 = jnp.concatenate([jnp.zeros_like(nope), rope[..., perm] * sign, zpad], axis=-1)
    wq2 = wq2.reshape(Q_LORA, N_HEADS * HEAD_PAD).astype(BF16)
    ukv = w_ukv.reshape(KV_LORA, N_HEADS, QK_NOPE + V_HEAD)
    wk = jnp.concatenate([ukv[..., :QK_NOPE], jnp.zeros((KV_LORA, N_HEADS, HEAD_PAD - QK_NOPE), F32)],
                         axis=-1).reshape(KV_LORA, N_HEADS * HEAD_PAD)
    eye = jnp.eye(QK_ROPE, dtype=F32)
    place = jnp.concatenate([jnp.zeros((QK_ROPE, QK_NOPE), F32), eye,
                             jnp.zeros((QK_ROPE, HEAD_PAD - D_QK), F32)], axis=1)
    place = jnp.tile(place, (1, N_HEADS))
    spread = jnp.concatenate([place, place, jnp.zeros((LANES - 2 * QK_ROPE, N_HEADS * HEAD_PAD), F32)], 0)
    wkk = jnp.concatenate([wk, spread], axis=0).astype(BF16)
    wv = jnp.concatenate([ukv[..., QK_NOPE:], jnp.zeros((KV_LORA, N_HEADS, HEAD_PAD - V_HEAD), F32)],
                         axis=-1).reshape(KV_LORA, N_HEADS * HEAD_PAD).astype(BF16)
    return wut, win, wq1, wq2, wkk, wv


def kernel(x, c, ctx, c_ctx, w_ada, b_ada, norm_mix, norm_ffn, w_in, q_norm, kv_norm, w_uq, w_ukv,
           ssm_a_re, ssm_a_im, ssm_log_dt, ssm_b_re, ssm_b_im, ssm_c_re, ssm_c_im, ssm_d, w_glu,
           b_glu, w_out, ffn_w1, ffn_w3, ffn_w2, moe_router, moe_w1, moe_w3, moe_w2, final_norm):
    assert x.shape == (BATCH, SEQ, D_MODEL) and ctx.shape == (BATCH, CTX_LEN, D_MODEL)
    cond = jnp.concatenate([c, c_ctx[None, :], jnp.zeros((MOD_ROWS - BATCH - 1, D_MODEL), F32)], axis=0)
    mod_all = _ada_call(cond, w_ada, b_ada).reshape(DEPTH, MOD_ROWS, 6, D_MODEL)
    cq_t, sq_t, cs_t = _rope_tables()
    xs = (x.reshape(N_LAT, D_MODEL), ctx.reshape(N_CTX, D_MODEL))

    out = None
    for i in range(DEPTH):
        last = i == DEPTH - 1
        mod = mod_all[i]
        wut, win, wq1, wq2, wkk, wv = _layer_weights(w_in[i], w_uq[i], w_ukv[i])
        u_t, q, k, v = _inproj_call(xs, mod, norm_mix[i][None, :], wut, win, q_norm[i][None, :],
                                    kv_norm[i][None, :], wq1, wq2, wkk, wv, cq_t, sq_t, cs_t)
        tabs = _ssm_tables(ssm_a_re[i], ssm_a_im[i], ssm_log_dt[i], ssm_b_re[i], ssm_b_im[i],
                           ssm_c_re[i], ssm_c_im[i], ssm_d[i])
        y_ssm_t = _ssm_call(u_t.reshape(D_SSM, N_CHUNK, SSM_T), *tabs).reshape(D_SSM, N_TOT)
        y_att = _attn_latent_call(q, k, v)
        if last:
            n_tiles = LAT_TILES
        else:
            y_att = _attn_ctx_call(q, k, v, y_att)
            n_tiles = N_TOT // TM
        j = i // 2
        mix_args = (xs, y_ssm_t, y_att, mod, w_glu[i].T.astype(BF16), b_glu[i][:, None],
                    w_out[i].astype(BF16), norm_ffn[i][None, :])
        if i % 2 == 0:
            assert not last
            ffn = (ffn_w1[j].astype(BF16), ffn_w3[j].astype(BF16), ffn_w2[j].astype(BF16))
            xs = tuple(_mix_call(*mix_args, ffn=ffn, n_tiles=n_tiles))
        else:
            assert last
            r = moe_router[j]
            r_top = lax.bitcast_convert_type(
                lax.bitcast_convert_type(r, jnp.uint32) & jnp.uint32(0xFFFF0000), F32)
            r_hi = r_top.astype(BF16)
            r_lo = (r - r_top).astype(BF16)
            zr = jnp.zeros((D_MODEL, LANES - 2 * N_EXPERTS), BF16)
            router = jnp.stack([jnp.concatenate([r_hi, r_lo, zr], axis=1),
                                jnp.concatenate([r_hi, jnp.zeros_like(r_lo), zr], axis=1)])
            x1, h2p, route = _mix_call(*mix_args, router=router, n_tiles=n_tiles)
            out = _moe_routed(h2p, x1, route, mod, moe_w1[j], moe_w3[j], moe_w2[j],
                              final_norm[None, :])
    return out.reshape(BATCH, SEQ, D_MODEL)
```

```python
import functools
import math

import jax
import jax.numpy as jnp
import numpy as np
from jax import lax
from jax.experimental import pallas as pl
from jax.experimental.pallas import tpu as pltpu
from jax.experimental.pallas import tpu_sc as plsc

D_MODEL = 1024
BATCH = 4
SEQ = 8192
DEPTH = 2
GRID_W = 64
CTX_LEN = 256
D_SSM = 512
SSM_GROUP = 16
N_SSM_GROUPS = D_SSM // SSM_GROUP
SSM_STATE = 64
N_HEADS = 8
QK_NOPE = 64
QK_ROPE = 32
V_HEAD = 64
Q_LORA = 256
KV_LORA = 128
D_QK = QK_NOPE + QK_ROPE
D_ATT = N_HEADS * V_HEAD
D_MIX = D_SSM + D_ATT
D_IN = D_SSM + Q_LORA + KV_LORA + QK_ROPE
AXIS_ROPE = QK_ROPE // 2
ROPE_FREQS = AXIS_ROPE // 2
ROPE_BASE = 10000.0
ATT_SCALE = 1.0 / math.sqrt(D_QK)
D_FF = 2816
N_EXPERTS = 8
D_FF_EXPERT = 1408
EPS = 1e-6

N_CTX = BATCH * CTX_LEN
N_LAT = BATCH * SEQ
N_TOT = N_CTX + N_LAT

LANES = 128
HEAD_PAD = 128
TM = 512
LAT_TILES = N_LAT // TM
SEQ_TILES = SEQ // TM
TQ = 512
TK = 2048
SSM_T = 128
N_CHUNK_LAT = N_LAT // SSM_T
N_CHUNK = N_TOT // SSM_T
SSM_K = SSM_GROUP * SSM_T
FF_CHUNK = 256
N_FF_CHUNKS = D_FF // FF_CHUNK
D_IN_REST = Q_LORA + KV_LORA + LANES
MOD_ROWS = 8
VMEM_LIMIT = 52 * 1024 * 1024

F32 = jnp.float32
BF16 = jnp.bfloat16
HI = lax.Precision.HIGHEST


def _cparams(*sem):
    return pltpu.CompilerParams(dimension_semantics=sem, vmem_limit_bytes=VMEM_LIMIT)


def _const_spec(shape):
    nd = len(shape)
    return pl.BlockSpec(shape, lambda *_: (0,) * nd, pipeline_mode=pl.Buffered(1))


def _mod_row(i):
    return jnp.where(i < LAT_TILES, i // SEQ_TILES, BATCH)


def _pos_tile(i):
    return jnp.where(i < LAT_TILES, i % SEQ_TILES, SEQ_TILES + i - LAT_TILES)


def _rms(x, g):
    ms = jnp.mean(x * x, axis=-1, keepdims=True)
    return x * lax.rsqrt(ms + EPS) * g


ADA_TN = 1536


def _ada_kernel(c_ref, w_ref, b_ref, o_ref):
    c = c_ref[...]
    s = c * jax.nn.sigmoid(c)
    o_ref[0] = jnp.dot(s, w_ref[0], precision=HI, preferred_element_type=F32) + b_ref[0]


def _ada_call(cond, w_ada, b_ada):
    n_col = 6 * D_MODEL // ADA_TN
    return pl.pallas_call(
        _ada_kernel,
        grid=(DEPTH, n_col),
        in_specs=[
            pl.BlockSpec((MOD_ROWS, D_MODEL), lambda l, j: (0, 0)),
            pl.BlockSpec((1, D_MODEL, ADA_TN), lambda l, j: (l, 0, j)),
            pl.BlockSpec((1, 1, ADA_TN), lambda l, j: (l, 0, j)),
        ],
        out_specs=pl.BlockSpec((1, MOD_ROWS, ADA_TN), lambda l, j: (l, 0, j)),
        out_shape=jax.ShapeDtypeStruct((DEPTH, MOD_ROWS, 6 * D_MODEL), F32),
        compiler_params=_cparams("arbitrary", "arbitrary"),
        name="ada_mod",
    )(cond, w_ada, b_ada.reshape(DEPTH, 1, 6 * D_MODEL))


def _tile_rows(refs, n_x):
    if n_x == 1:
        return refs[0][...]
    return jnp.where(pl.program_id(0) < LAT_TILES, refs[0][...], refs[1][...])


def _x_specs(n_x):
    if n_x == 1:
        return [pl.BlockSpec((TM, D_MODEL), lambda i: (i, 0))]
    return [pl.BlockSpec((TM, D_MODEL), lambda i: (jnp.minimum(i, LAT_TILES - 1), 0)),
            pl.BlockSpec((TM, D_MODEL), lambda i: (jnp.maximum(i - LAT_TILES, 0), 0))]


def _inproj_kernel(*refs, n_x):
    (mod_ref, g_ref, wut_ref, win_ref, qg_ref, kvg_ref, wq1_ref, wq2_ref, wkk_ref, wv_ref,
     cq_ref, sq_ref, cs_ref, ut_ref, q_ref, k_ref, v_ref) = refs[n_x:]
    x = _tile_rows(refs, n_x)
    sh = mod_ref[0, 0:1, :]
    sc = mod_ref[0, 1:2, :]
    xm = (_rms(x, g_ref[...]) * (1.0 + sc) + sh).astype(BF16)
    ut_ref[...] = lax.dot_general(wut_ref[...], xm, (((1,), (1,)), ((), ())),
                                  preferred_element_type=F32)
    z = jnp.dot(xm, win_ref[...], preferred_element_type=F32)
    qn = _rms(z[:, :Q_LORA], qg_ref[...]).astype(BF16)
    kvn = _rms(z[:, Q_LORA:Q_LORA + KV_LORA], kvg_ref[...]).astype(BF16)
    krr = (z[:, Q_LORA + KV_LORA:] * cs_ref[...]).astype(BF16)
    q1 = jnp.dot(qn, wq1_ref[...], preferred_element_type=F32)
    q2 = jnp.dot(qn, wq2_ref[...], preferred_element_type=F32)
    cq = cq_ref[...]
    sq = sq_ref[...]
    for h in range(N_HEADS):
        sl = slice(h * HEAD_PAD, (h + 1) * HEAD_PAD)
        q_ref[:, sl] = (q1[:, sl] * cq + q2[:, sl] * sq).astype(q_ref.dtype)
    kin = jnp.concatenate([kvn, krr], axis=-1)
    k_ref[...] = jnp.dot(kin, wkk_ref[...], preferred_element_type=F32).astype(k_ref.dtype)
    vv = jnp.dot(kvn, wv_ref[...], preferred_element_type=F32)
    lane = lax.broadcasted_iota(jnp.int32, vv.shape, 1)
    v_ref[...] = jnp.where(lane % HEAD_PAD == V_HEAD, 1.0, vv).astype(v_ref.dtype)


def _inproj_call(xs, mod, g_mix, wut, win, qg, kvg, wq1, wq2, wkk, wv, cq_t, sq_t, cs_t):
    n_tiles = N_TOT // TM
    row = lambda i: (i, 0)
    pos = lambda i: (_pos_tile(i), 0)
    return pl.pallas_call(
        functools.partial(_inproj_kernel, n_x=len(xs)),
        grid=(n_tiles,),
        in_specs=_x_specs(len(xs)) + [
            pl.BlockSpec((1, 6, D_MODEL), lambda i: (_mod_row(i), 0, 0)),
            _const_spec((1, D_MODEL)),
            _const_spec((D_SSM, D_MODEL)),
            _const_spec((D_MODEL, D_IN_REST)),
            _const_spec((1, Q_LORA)),
            _const_spec((1, KV_LORA)),
            _const_spec((Q_LORA, N_HEADS * HEAD_PAD)),
            _const_spec((Q_LORA, N_HEADS * HEAD_PAD)),
            _const_spec((2 * KV_LORA, N_HEADS * HEAD_PAD)),
            _const_spec((KV_LORA, N_HEADS * HEAD_PAD)),
            pl.BlockSpec((TM, LANES), pos),
            pl.BlockSpec((TM, LANES), pos),
            pl.BlockSpec((TM, LANES), pos),
        ],
        out_specs=[
            pl.BlockSpec((D_SSM, TM), lambda i: (0, i)),
            pl.BlockSpec((TM, N_HEADS * HEAD_PAD), row),
            pl.BlockSpec((TM, N_HEADS * HEAD_PAD), row),
            pl.BlockSpec((TM, N_HEADS * HEAD_PAD), row),
        ],
        out_shape=[
            jax.ShapeDtypeStruct((D_SSM, N_TOT), F32),
            jax.ShapeDtypeStruct((N_TOT, N_HEADS * HEAD_PAD), BF16),
            jax.ShapeDtypeStruct((N_TOT, N_HEADS * HEAD_PAD), BF16),
            jax.ShapeDtypeStruct((N_TOT, N_HEADS * HEAD_PAD), BF16),
        ],
        compiler_params=_cparams("parallel"),
        name="in_proj",
    )(*xs, mod, g_mix, wut, win, qg, kvg, wq1, wq2, wkk, wv, cq_t, sq_t, cs_t)


def _attn_kernel(*refs, n_chunks, tq):
    if n_chunks:
        q_ref, k_ref, v_ref, kc_ref, vc_ref, o_ref, s_scr = refs
    else:
        q_ref, kc_ref, vc_ref, _, o_ref, s_scr = refs
    heads =[slice(hh * HEAD_PAD, (hh + 1) * HEAD_PAD) for hh in range(2)]
    qs = [q_ref[:, sl] for sl in heads]

    def put_scores(slot, k_at, width):
        for hh in range(2):
            s_scr[slot, hh, :, :width] = lax.dot_general(
                qs[hh], k_at(heads[hh]), (((1,), (1,)), ((), ())), preferred_element_type=F32)

    def consume(carry, slot, v_at, width):
        new = []
        for hh in range(2):
            m, acc = carry[hh]
            s = s_scr[slot, hh, :, :width]
            m_new = jnp.maximum(m, jnp.max(s, axis=-1, keepdims=True))
            alpha = jnp.exp2(m - m_new)
            p = jnp.exp2(s - m_new).astype(BF16)
            acc = alpha * acc + jnp.dot(p, v_at(heads[hh]), preferred_element_type=F32)
            new.append((m_new, acc))
        return tuple(new)

    def chunk(c):
        rows = pl.ds(pl.multiple_of(c * TK, TK), TK)
        return (lambda sl: k_ref[rows, sl]), (lambda sl: v_ref[rows, sl])

    ctx_k, ctx_v = (lambda sl: kc_ref[:, sl]), (lambda sl: vc_ref[:, sl])
    carry = tuple((jnp.full((tq, 1), -jnp.inf, F32), jnp.zeros((tq, HEAD_PAD), F32))
                  for _ in range(2))
    if n_chunks:
        assert n_chunks % 2 == 0
        put_scores(0, chunk(0)[0], TK)

        def body(j, carry):
            c0 = 2 * j
            put_scores(1, chunk(c0 + 1)[0], TK)
            carry = consume(carry, 0, chunk(c0)[1], TK)
            put_scores(0, chunk(c0 + 2)[0], TK)
            return consume(carry, 1, chunk(c0 + 1)[1], TK)

        carry = lax.fori_loop(0, n_chunks // 2 - 1, body, carry)
        put_scores(1, chunk(n_chunks - 1)[0], TK)
        carry = consume(carry, 0, chunk(n_chunks - 2)[1], TK)
        put_scores(0, ctx_k, CTX_LEN)
        carry = consume(carry, 1, chunk(n_chunks - 1)[1], TK)
    else:
        put_scores(0, ctx_k, CTX_LEN)
    carry = consume(carry, 0, ctx_v, CTX_LEN)
    outs = [acc[:, :V_HEAD] / acc[:, V_HEAD:V_HEAD + 1] for _, acc in carry]
    o_ref[...] = jnp.concatenate(outs, axis=-1).astype(o_ref.dtype)


def _attn_latent_call(q, k, v):
    qt = SEQ // TQ
    ctx0 = N_LAT // CTX_LEN
    return pl.pallas_call(
        functools.partial(_attn_kernel, n_chunks=SEQ // TK, tq=TQ),
        grid=(BATCH, N_HEADS // 2, qt),
        in_specs=[
            pl.BlockSpec((TQ, 2 * HEAD_PAD), lambda b, h, i: (b * qt + i, h)),
            pl.BlockSpec((SEQ, 2 * HEAD_PAD), lambda b, h, i: (b, h)),
            pl.BlockSpec((SEQ, 2 * HEAD_PAD), lambda b, h, i: (b, h)),
            pl.BlockSpec((CTX_LEN, 2 * HEAD_PAD), lambda b, h, i: (ctx0 + b, h)),
            pl.BlockSpec((CTX_LEN, 2 * HEAD_PAD), lambda b, h, i: (ctx0 + b, h)),
        ],
        out_specs=pl.BlockSpec((TQ, 2 * V_HEAD), lambda b, h, i: (b * qt + i, h)),
        out_shape=jax.ShapeDtypeStruct((N_TOT, D_ATT), BF16),
        scratch_shapes=[pltpu.VMEM((2, 2, TQ, TK), F32)],
        compiler_params=_cparams("parallel", "parallel", "arbitrary"),
        name="attn_latent",
    )(q, k, v, k, v)


def _attn_ctx_call(q, k, v, y_att):
    ctx0 = N_LAT // CTX_LEN
    return pl.pallas_call(
        functools.partial(_attn_kernel, n_chunks=0, tq=CTX_LEN),
        grid=(BATCH, N_HEADS // 2),
        in_specs=[
            pl.BlockSpec((CTX_LEN, 2 * HEAD_PAD), lambda b, h: (ctx0 + b, h)),
            pl.BlockSpec((CTX_LEN, 2 * HEAD_PAD), lambda b, h: (ctx0 + b, h)),
            pl.BlockSpec((CTX_LEN, 2 * HEAD_PAD), lambda b, h: (ctx0 + b, h)),
            pl.BlockSpec(memory_space=pl.ANY),
        ],
        out_specs=pl.BlockSpec((CTX_LEN, 2 * V_HEAD), lambda b, h: (ctx0 + b, h)),
        out_shape=jax.ShapeDtypeStruct((N_TOT, D_ATT), BF16),
        input_output_aliases={3: 0},
        scratch_shapes=[pltpu.VMEM((1, 2, CTX_LEN, CTX_LEN), F32)],
        compiler_params=_cparams("parallel", "parallel"),
        name="attn_ctx",
    )(q, k, v, y_att)


SSM_C = 64
SSM_HALVES = SSM_T // SSM_C
SSM_KC = SSM_GROUP * SSM_C
LAG_ROWS = 2 * SSM_C
PT_LAG, PT_INC, PT_OUT = 0, LAG_ROWS, LAG_ROWS + SSM_C
PT_ROWS = LAG_ROWS + 2 * SSM_C
NT_DIMS = (((1,), (1,)), ((), ()))


def _ssm_kernel(u_ref, pta_ref, ptb_ref, rows_ref, cc_ref, at_ref, dv_ref, y_ref,
                abt_hi_scr, abt_lo_scr, wl_scr, m_scr, ws_scr, wct_scr, s_scr, h_scr):
    H, C = SSM_GROUP, SSM_C
    half_lane = lax.broadcasted_iota(jnp.int32, (1, LANES), 1) < C

    def split(v):
        hi = v.astype(BF16)
        return hi, (v - hi.astype(F32)).astype(BF16)

    def scaled(row0, n_rows, ra, rb):
        return pta_ref[0, row0:row0 + n_rows, :] * ra + ptb_ref[0, row0:row0 + n_rows, :] * rb

    def build_tables(i, _):
        row0 = pl.multiple_of(i * C, C)
        b_re = rows_ref[0, pl.ds(i, 1), :]
        b_im = rows_ref[0, pl.ds(H + i, 1), :]
        lag_rows = pl.ds(pl.multiple_of(i * LAG_ROWS, LAG_ROWS), LAG_ROWS)
        abt_hi_scr[lag_rows, :], abt_lo_scr[lag_rows, :] = split(
            scaled(PT_LAG, LAG_ROWS, b_re, b_im))
        ws_scr[pl.ds(row0, C), :] = scaled(PT_INC, C, b_re, b_im).astype(BF16)
        c_re = rows_ref[0, pl.ds(2 * H + i, 1), :]
        c_im = rows_ref[0, pl.ds(3 * H + i, 1), :]
        wct_scr[pl.ds(row0, C), :] = scaled(PT_OUT, C, c_re, c_im).astype(BF16)
        return 0

    lax.fori_loop(0, H, build_tables, 0)
    cc_hi, cc_lo = split(cc_ref[0])
    nt = lambda a, b: lax.dot_general(a, b, NT_DIMS, preferred_element_type=F32)
    wl_scr[...] = (nt(cc_hi, abt_hi_scr[...]) + nt(cc_hi, abt_lo_scr[...])
                   + nt(cc_lo, abt_hi_scr[...]))

    slot2 = lax.broadcasted_iota(jnp.int32, (1, 2 * LANES), 1)
    first_low = (slot2 < C) | (slot2 > 2 * LANES - C)
    first_high = slot2 < LANES

    def toeplitz_pair(x):
        return pltpu.roll(jnp.broadcast_to(x, (C, 2 * LANES)), 0, 1, stride=1, stride_axis=0)

    def build_toeplitz(ci, _):
        row0 = pl.multiple_of(ci * C, C)
        slots = pl.ds(pl.multiple_of(ci * LAG_ROWS, LAG_ROWS), LAG_ROWS)

        def lags(co, shift):
            v = wl_scr[co:co + 1, slots]
            if shift:
                v = pltpu.roll(v, C, 1)
            return jnp.concatenate([v, v], axis=1)

        for ka in range(0, H // 2, 2):
            kb = ka + 1
            low = toeplitz_pair(jnp.where(first_low, lags(2 * ka, False), lags(2 * kb, False)))
            high = toeplitz_pair(jnp.where(first_high, lags(2 * ka + 1, True),
                                           lags(2 * kb + 1, True)))
            for k, lanes in ((ka, slice(0, LANES)), (kb, slice(LANES, 2 * LANES))):
                m_scr[pl.ds(row0, C), k * LANES:(k + 1) * LANES] = jnp.where(
                    half_lane, low[:, lanes], high[:, lanes]).astype(BF16)
        return 0

    lax.fori_loop(0, H, build_toeplitz, 0)

    def chunk_operand(hf):
        cols = []
        for k in range(H // 2):
            a, b = u_ref[2 * k], u_ref[2 * k + 1]
            if hf == 0:
                cols.append(jnp.where(half_lane, a, pltpu.roll(b, C, 1)))
            else:
                cols.append(jnp.where(half_lane, pltpu.roll(a, C, 1), b))
        return jnp.concatenate(cols, axis=1).astype(BF16)

    ys = []
    for hf in range(SSM_HALVES):
        u = chunk_operand(hf)
        ys.append(jnp.dot(u, m_scr[...], preferred_element_type=F32))
        s = jnp.dot(u, ws_scr[...], preferred_element_type=F32)
        for d in range(2):
            s_d = s[:, d * LANES:(d + 1) * LANES]
            s_scr[hf, d] = s_d
            s_scr[hf, 2 + d] = pltpu.roll(s_d, SSM_STATE, 1)

    n_lat, n_ctx = SEQ // SSM_T, CTX_LEN // SSM_T
    ctx = [((N_CHUNK_LAT + c, n_ctx), hf) for c in range(n_ctx) for hf in range(SSM_HALVES)]
    lat = [((k, n_lat), hf) for k in range(n_lat) for hf in range(SSM_HALVES)]
    coef = [jnp.broadcast_to(at_ref[0, r:r + 1, :], (BATCH, 2 * SSM_STATE)) for r in range(4)]

    def advance(h, h_sw, d, rows, hf):
        a0, a1 = coef[2 * d], coef[2 * d + 1]
        return (h * a0 + h_sw * a1 + s_scr[hf, d, rows, :],
                h_sw * a0 - h * a1 + s_scr[hf, 2 + d, rows, :])

    zero = jnp.zeros((BATCH, 2 * SSM_STATE), F32)
    h_f, h_f_sw, h_r, h_r_sw = zero, zero, zero, zero
    for ((sf, stf), cf), ((sr, strd), cr) in zip(ctx + lat, ctx[::-1] + lat[::-1]):
        rows_f = pl.ds(sf, BATCH, stride=stf)
        rows_r = pl.ds(sr, BATCH, stride=strd)
        h_scr[cf, 0, rows_f, :] = h_f
        h_f, h_f_sw = advance(h_f, h_f_sw, 0, rows_f, cf)
        h_scr[cr, 1, rows_r, :] = h_r
        h_r, h_r_sw = advance(h_r, h_r_sw, 1, rows_r, cr)

    for hf in range(SSM_HALVES):
        h_in = jnp.concatenate([h_scr[hf, 0], h_scr[hf, 1]], axis=-1).astype(BF16)
        ys[hf] = ys[hf] + lax.dot_general(h_in, wct_scr[...], NT_DIMS,
                                          preferred_element_type=F32)
    for k in range(H // 2):
        y0, y1 = ys[0][:, k * LANES:(k + 1) * LANES], ys[1][:, k * LANES:(k + 1) * LANES]
        for c, yc in ((2 * k, jnp.where(half_lane, y0, pltpu.roll(y1, C, 1))),
                      (2 * k + 1, jnp.where(half_lane, pltpu.roll(y0, C, 1), y1))):
            y_ref[c] = yc + u_ref[c] * dv_ref[0, :, c * SSM_T:(c + 1) * SSM_T]


def _ssm_call(u_t, pta, ptb, rows, cc, at, dv):
    g3 = lambda g: (g, 0, 0)
    return pl.pallas_call(
        _ssm_kernel,
        grid=(N_SSM_GROUPS,),
        in_specs=[
            pl.BlockSpec((SSM_GROUP, N_CHUNK, SSM_T), g3),
            pl.BlockSpec((1, PT_ROWS, 4 * SSM_STATE), g3),
            pl.BlockSpec((1, PT_ROWS, 4 * SSM_STATE), g3),
            pl.BlockSpec((1, 4 * SSM_GROUP, 4 * SSM_STATE), g3),
            pl.BlockSpec((1, SSM_GROUP, 4 * SSM_STATE), g3),
            pl.BlockSpec((1, 4, 2 * SSM_STATE), g3),
            pl.BlockSpec((1, 1, SSM_K), g3),
        ],
        out_specs=pl.BlockSpec((SSM_GROUP, N_CHUNK, SSM_T), g3),
        out_shape=jax.ShapeDtypeStruct((D_SSM, N_CHUNK, SSM_T), F32),
        scratch_shapes=[
            pltpu.VMEM((SSM_GROUP * LAG_ROWS, 4 * SSM_STATE), BF16),
            pltpu.VMEM((SSM_GROUP * LAG_ROWS, 4 * SSM_STATE), BF16),
            pltpu.VMEM((SSM_GROUP, SSM_GROUP * LAG_ROWS), F32),
            pltpu.VMEM((SSM_KC, SSM_KC), BF16),
            pltpu.VMEM((SSM_KC, 4 * SSM_STATE), BF16),
            pltpu.VMEM((SSM_KC, 4 * SSM_STATE), BF16),
            pltpu.VMEM((SSM_HALVES, 4, N_CHUNK, 2 * SSM_STATE), F32),
            pltpu.VMEM((SSM_HALVES, 2, N_CHUNK, 2 * SSM_STATE), F32),
        ],
        compiler_params=_cparams("parallel"),
        name="s5_mixer",
    )(u_t, pta, ptb, rows, cc, at, dv)


def _ssm_tables(a_re, a_im, log_dt, b_re, b_im, c_re, c_im, d_skip):
    G, P, H, T = N_SSM_GROUPS, SSM_STATE, SSM_GROUP, SSM_T
    a_re, a_im = a_re.astype(F32), a_im.astype(F32)
    dt = jnp.exp(log_dt.astype(F32))[..., None]
    den = a_re * a_re + a_im * a_im
    mag1 = jnp.exp(dt * a_re)
    ab_re, ab_im = mag1 * jnp.cos(dt * a_im), mag1 * jnp.sin(dt * a_im)
    num_re = ab_re - 1.0
    f_re = (num_re * a_re + ab_im * a_im) / den
    f_im = (ab_im * a_re - num_re * a_im) / den
    b_re, b_im = b_re.astype(F32), b_im.astype(F32)
    bb_re = f_re[..., None] * b_re - f_im[..., None] * b_im
    bb_im = f_re[..., None] * b_im + f_im[..., None] * b_re
    c_re, c_im = c_re.astype(F32), c_im.astype(F32)
    la, th = dt * a_re, dt * a_im

    def powers(d, n):
        mag = jnp.exp(la[d][:, None, :] * n[None, :, None])
        ph = th[d][:, None, :] * n[None, :, None]
        return mag * jnp.cos(ph), mag * jnp.sin(ph)

    C = SSM_C
    slot = np.arange(LAG_ROWS)
    lag_f = np.where(slot < C, slot, -1)
    lag_r = np.where((-slot) % LAG_ROWS < C, (-slot) % LAG_ROWS, -1)
    s_idx = np.arange(C)
    f_all = np.concatenate([lag_f, C - 1 - s_idx, s_idx + 1])
    r_all = np.concatenate([lag_r, s_idx, C - s_idx])
    f_exp, r_exp = jnp.asarray(np.maximum(f_all, 0), F32), jnp.asarray(np.maximum(r_all, 0), F32)
    f_on, r_on = jnp.asarray(f_all >= 0, F32), jnp.asarray(r_all >= 0, F32)
    lane = jnp.arange(4 * P)
    is_fwd = (lane < 2 * P)[None, :]
    is_re = ((lane // P) % 2 == 0)[None, None, :]
    expo = jnp.where(is_fwd, f_exp[:, None], r_exp[:, None])[None]
    on = jnp.where(is_fwd, f_on[:, None], r_on[:, None])[None]
    la4 = jnp.concatenate([la[0], la[0], la[1], la[1]], axis=-1)[:, None, :]
    th4 = jnp.concatenate([th[0], th[0], th[1], th[1]], axis=-1)[:, None, :]
    mag = jnp.exp(la4 * expo) * on
    p_re, p_im = mag * jnp.cos(th4 * expo), mag * jnp.sin(th4 * expo)
    pta = jnp.where(is_re, p_re, p_im)
    ptb = jnp.where(is_re, -p_im, p_re)

    def per_channel(v):
        f, r = v[0].transpose(0, 2, 1), v[1].transpose(0, 2, 1)
        return jnp.concatenate([f, f, r, r], axis=-1)
    cf_re, cr_re = c_re[0], c_re[1]
    cf_im, cr_im = c_im[0], c_im[1]
    rows = jnp.concatenate([
        per_channel(bb_re), per_channel(bb_im),
        jnp.concatenate([cf_re, -cf_re, cr_re, -cr_re], axis=-1),
        jnp.concatenate([cf_im, -cf_im, cr_im, -cr_im], axis=-1)], axis=1)
    cc = jnp.concatenate([cf_re, -cf_im, cr_re, -cr_im], axis=-1)

    t_exp = jnp.full((1,), float(SSM_C), F32)
    (f_re_t, f_im_t), (r_re_t, r_im_t) = powers(0, t_exp), powers(1, t_exp)
    f_re_t, f_im_t, r_re_t, r_im_t = (v[:, 0, :] for v in (f_re_t, f_im_t, r_re_t, r_im_t))
    at = jnp.stack([jnp.concatenate([f_re_t, f_re_t], -1), jnp.concatenate([-f_im_t, f_im_t], -1),
                    jnp.concatenate([r_re_t, r_re_t], -1), jnp.concatenate([-r_im_t, r_im_t], -1)],
                   axis=1)
    dv = jnp.repeat(d_skip.astype(F32).reshape(G, H), T, axis=-1).reshape(G, 1, H * T)
    return pta, ptb, rows, cc, at, dv


ROUTE_E1, ROUTE_E2, ROUTE_W1, ROUTE_W2 = 0, 1, 2, 3


def _top2_route(logits):
    lane = lax.broadcasted_iota(jnp.int32, logits.shape, 1)
    lg = jnp.where(lane < N_EXPERTS, logits, -jnp.inf)
    m1 = jnp.max(lg, axis=-1, keepdims=True)
    i1 = jnp.min(jnp.where(lg == m1, lane, LANES), axis=-1, keepdims=True)
    lg2 = jnp.where(lane == i1, -jnp.inf, lg)
    m2 = jnp.max(lg2, axis=-1, keepdims=True)
    i2 = jnp.min(jnp.where(lg2 == m2, lane, LANES), axis=-1, keepdims=True)
    e2 = jnp.exp(m2 - m1)
    w1 = 1.0 / (1.0 + e2)
    rec = jnp.where(lane == ROUTE_E1, i1.astype(F32), 0.0)
    rec = jnp.where(lane == ROUTE_E2, i2.astype(F32), rec)
    rec = jnp.where(lane == ROUTE_W1, w1, rec)
    return jnp.where(lane == ROUTE_W2, e2 * w1, rec)


def _pack_bf16_pairs(v):
    k = v.shape[1] // 2
    bits = pltpu.bitcast(v.astype(BF16).astype(F32), jnp.uint32)
    return (bits[:, :k] & jnp.uint32(0xFFFF0000)) | (bits[:, k:] >> 16)


def _unpack_bf16_pairs(w):
    hi = pltpu.bitcast(w & jnp.uint32(0xFFFF0000), F32)
    lo = pltpu.bitcast(w << 16, F32)
    return hi, lo


def _store_packed(ref, v):
    words = _pack_bf16_pairs(v)
    for s in range(SC_SPLIT):
        ref[s] = words[:, s * SC_ROW:(s + 1) * SC_ROW]


def _load_packed(ref):
    return _unpack_bf16_pairs(jnp.concatenate([ref[s] for s in range(SC_SPLIT)], axis=-1))


def _mix_kernel(*refs, with_router, n_x):
    rest = refs[n_x:]
    if with_router:
        (yst_ref, ya_ref, mod_ref, wglut_ref, bglu_ref, wout_ref, gffn_ref, router_ref,
         x1_ref, h2_ref, gate_ref) = rest
    else:
        (yst_ref, ya_ref, mod_ref, wglut_ref, bglu_ref, wout_ref, gffn_ref,
         w1_ref, w3_ref, w2_ref, x2_ref) = rest
    zt = jax.nn.gelu(yst_ref[...], approximate=True)
    glt = zt * jax.nn.sigmoid(
        jnp.dot(wglut_ref[...], zt.astype(BF16), preferred_element_type=F32) + bglu_ref[...])
    mix = jnp.concatenate([glt.T.astype(BF16), ya_ref[...]], axis=-1)
    o = jnp.dot(mix, wout_ref[...], preferred_element_type=F32)
    x1 = _tile_rows(refs, n_x) + mod_ref[0, 2:3, :] * o
    h2 = _rms(x1, gffn_ref[...]) * (1.0 + mod_ref[0, 4:5, :]) + mod_ref[0, 3:4, :]
    if not with_router:
        acc = _swiglu(h2.astype(BF16), lambda sl: w1_ref[:, sl], lambda sl: w3_ref[:, sl],
                      lambda sl: w2_ref[sl, :], D_FF)
        x2_ref[...] = x1 + mod_ref[0, 5:6, :] * acc
    else:
        x1_ref[...] = x1
        _store_packed(h2_ref, h2)
        h_hi = h2.astype(BF16)
        h_lo = (h2 - h_hi.astype(F32)).astype(BF16)
        o1 = jnp.dot(h_hi, router_ref[0], preferred_element_type=F32)
        o2 = jnp.dot(h_lo, router_ref[1], preferred_element_type=F32)
        logits = o1 + pltpu.roll(o1, LANES - N_EXPERTS, 1) + o2
        gate_ref[...] = _top2_route(logits)


def _mix_call(xs, y_ssm_t, y_att, mod, wglut, bglu, wout, gffn, *, router=None, ffn=None, n_tiles):
    row = lambda i: (i, 0)
    out_row = row
    with_router = router is not None
    assert with_router != (ffn is not None)
    in_specs = _x_specs(len(xs)) + [
        pl.BlockSpec((D_SSM, TM), lambda i: (0, i)),
        pl.BlockSpec((TM, D_ATT), row),
        pl.BlockSpec((1, 6, D_MODEL), lambda i: (_mod_row(i), 0, 0)),
        _const_spec((D_SSM, D_SSM)),
        _const_spec((D_SSM, 1)),
        _const_spec((D_MIX, D_MODEL)),
        _const_spec((1, D_MODEL)),
    ]
    args = [*xs, y_ssm_t, y_att, mod, wglut, bglu, wout, gffn]
    out_specs = [pl.BlockSpec((TM, D_MODEL), out_row)]
    out_shape = [jax.ShapeDtypeStruct((n_tiles * TM, D_MODEL), F32)]
    if with_router:
        in_specs.append(_const_spec((2, D_MODEL, LANES)))
        args.append(router)
        out_specs += [pl.BlockSpec((SC_SPLIT, TM, SC_ROW), lambda i: (0, i, 0)),
                      pl.BlockSpec((TM, LANES), out_row)]
        out_shape += [jax.ShapeDtypeStruct((SC_SPLIT, n_tiles * TM, SC_ROW), jnp.uint32),
                      jax.ShapeDtypeStruct((n_tiles * TM, LANES), F32)]
    else:
        in_specs += [_const_spec((D_MODEL, D_FF)), _const_spec((D_MODEL, D_FF)),
                     _const_spec((D_FF, D_MODEL))]
        args += list(ffn)
    return pl.pallas_call(
        functools.partial(_mix_kernel, with_router=with_router, n_x=len(xs)),
        grid=(n_tiles,),
        in_specs=in_specs,
        out_specs=out_specs,
        out_shape=out_shape,
        compiler_params=_cparams("parallel"),
        name="mix_out",
    )(*args)


def _swiglu(h, w1_at, w3_at, w2_at, d_ff):
    acc = jnp.zeros((h.shape[0], D_MODEL), F32)
    for lo in range(0, d_ff, FF_CHUNK):
        sl = slice(lo, min(lo + FF_CHUNK, d_ff))
        a = jnp.dot(h, w1_at(sl), preferred_element_type=F32)
        b = jnp.dot(h, w3_at(sl), preferred_element_type=F32)
        g = (a * jax.nn.sigmoid(a) * b).astype(BF16)
        acc = acc + jnp.dot(g, w2_at(sl), preferred_element_type=F32)
    return acc


EXPERT_BLK = 512
N_SORTED = 2 * N_LAT + N_EXPERTS * EXPERT_BLK
N_EXPERT_BLKS = N_SORTED // EXPERT_BLK
PACKED = D_MODEL // 2
SC_ROW = 256
SC_SPLIT = PACKED // SC_ROW
SC_WIN = 128


def _route_kernel(r_ref, dest_ref, cnt_ref, carry_scr, off_scr):
    phase, i = pl.program_id(0), pl.program_id(1)
    lane = lax.broadcasted_iota(jnp.int32, (TM, LANES), 1)
    e1 = r_ref[:, ROUTE_E1:ROUTE_E1 + 1].astype(jnp.int32)
    e2 = r_ref[:, ROUTE_E2:ROUTE_E2 + 1].astype(jnp.int32)
    picked = jnp.where((lane == e1) | (lane == e2), 1.0, 0.0)
    tile_cnt = jnp.sum(picked, axis=0, keepdims=True)

    @pl.when((phase == 0) & (i == 0))
    def _():
        carry_scr[...] = jnp.zeros_like(carry_scr)

    @pl.when(phase == 0)
    def _():
        carry_scr[...] += tile_cnt

    @pl.when((phase == 1) & (i == 0))
    def _():
        cnt = carry_scr[...]
        cnt_ref[...] = jnp.broadcast_to(cnt, cnt_ref.shape)
        padded = jnp.floor((cnt + (EXPERT_BLK - 1)) * (1.0 / EXPERT_BLK)) * EXPERT_BLK
        before = (lax.broadcasted_iota(jnp.int32, (LANES, LANES), 0)
                  < lax.broadcasted_iota(jnp.int32, (LANES, LANES), 1)).astype(F32)
        off = jnp.dot(jnp.broadcast_to(padded, (8, LANES)), before, precision=HI,
                      preferred_element_type=F32)
        off_scr[...] = off[0:1, :]
        carry_scr[...] = jnp.zeros_like(carry_scr)

    @pl.when(phase == 1)
    def _():
        earlier = (lax.broadcasted_iota(jnp.int32, (TM, TM), 1)
                   < lax.broadcasted_iota(jnp.int32, (TM, TM), 0)).astype(BF16)
        rank = jnp.dot(earlier, picked.astype(BF16), preferred_element_type=F32)
        slot = rank + carry_scr[...] + off_scr[...]
        d1 = jnp.sum(jnp.where(lane == e1, slot, 0.0), axis=-1, keepdims=True)
        d2 = jnp.sum(jnp.where(lane == e2, slot, 0.0), axis=-1, keepdims=True)
        dest_ref[...] = jnp.where(lane == 0, d1, jnp.where(lane == 1, d2, 0.0)).astype(jnp.int32)
        carry_scr[...] += tile_cnt


def _route_call(route):
    n_tiles = N_LAT // TM
    return pl.pallas_call(
        _route_kernel,
        grid=(2, n_tiles),
        in_specs=[pl.BlockSpec((TM, LANES), lambda p, i: (i, 0))],
        out_specs=[pl.BlockSpec((TM, LANES), lambda p, i: (p * i, 0)),
                   pl.BlockSpec((8, LANES), lambda p, i: (0, 0))],
        out_shape=[jax.ShapeDtypeStruct((N_LAT, LANES), jnp.int32),
                   jax.ShapeDtypeStruct((8, LANES), F32)],
        scratch_shapes=[pltpu.VMEM((1, LANES), F32), pltpu.VMEM((1, LANES), F32)],
        compiler_params=_cparams("arbitrary", "arbitrary"),
        name="moe_route",
    )(route)


def _sc_mesh():
    return plsc.VectorSubcoreMesh(core_axis_name="core", subcore_axis_name="subcore")


def _sc_scatter(x, idx_a, idx_b, n_out):
    n = x.shape[0]

    @pl.kernel(out_type=jax.ShapeDtypeStruct((n_out, SC_ROW), x.dtype), mesh=_sc_mesh(),
               scratch_types=[])
    def scatter(x_hbm, a_hbm, b_hbm, o_hbm):
        def body(x_vmem, a_vmem, b_vmem):
            pltpu.sync_copy(x_vmem, o_hbm.at[a_vmem.at[0]])
            pltpu.sync_copy(x_vmem, o_hbm.at[b_vmem.at[0]])

        pltpu.emit_pipeline(
            body, grid=(n // SC_WIN,),
            in_specs=[pl.BlockSpec((SC_WIN, SC_ROW), lambda i: (i, 0)),
                      pl.BlockSpec((1, SC_WIN), lambda i: (0, i)),
                      pl.BlockSpec((1, SC_WIN), lambda i: (0, i))],
            out_specs=[],
            core_axis_name=("core", "subcore"),
            dimension_semantics=(pltpu.PARALLEL,),
        )(x_hbm, a_hbm, b_hbm)

    return scatter(x, idx_a.reshape(1, n), idx_b.reshape(1, n))


def _sc_gather(y, idx):
    n = idx.shape[0]

    @pl.kernel(out_type=jax.ShapeDtypeStruct((n, SC_ROW), y.dtype), mesh=_sc_mesh(),
               scratch_types=[])
    def gather(y_hbm, i_hbm, o_hbm):
        def body(i_vmem, o_vmem):
            pltpu.sync_copy(y_hbm.at[i_vmem.at[0]], o_vmem)

        pltpu.emit_pipeline(
            body, grid=(n // SC_WIN,),
            in_specs=[pl.BlockSpec((1, SC_WIN), lambda i: (0, i))],
            out_specs=[pl.BlockSpec((SC_WIN, SC_ROW), lambda i: (i, 0))],
            core_axis_name=("core", "subcore"),
            dimension_semantics=(pltpu.PARALLEL,),
        )(i_hbm, o_hbm)

    return gather(y, idx.reshape(1, n))


W_CAST_ROWS = 128


def _experts_kernel(blk_expert_ref, n_used_ref, x_ref, w1_ref, w3_ref, w2_ref, o_ref,
                    w1_scr, w3_scr, w2_scr):
    b = pl.program_id(0)
    live = b < n_used_ref[0]
    new_expert = (b == 0) | (blk_expert_ref[b] != blk_expert_ref[jnp.maximum(b - 1, 0)])

    @pl.when(live & new_expert)
    def _():
        for src, dst in ((w1_ref, w1_scr), (w3_ref, w3_scr), (w2_ref, w2_scr)):
            for r in range(0, dst.shape[0], W_CAST_ROWS):
                dst[r:r + W_CAST_ROWS, :] = src[0, r:r + W_CAST_ROWS, :].astype(BF16)

    @pl.when(live)
    def _():
        hi, lo = _load_packed(x_ref)
        h = jnp.concatenate([hi.astype(BF16), lo.astype(BF16)], axis=-1)
        y = _swiglu(h, lambda sl: w1_scr[:, sl], lambda sl: w3_scr[:, sl],
                    lambda sl: w2_scr[sl, :], D_FF_EXPERT)
        _store_packed(o_ref, y)


def _experts_call(blk_expert, n_used, xs, w1, w3, w2):
    row = lambda b, be, nu: (0, b, 0)
    wsel = lambda b, be, nu: (be[b], 0, 0)
    single = pl.Buffered(1)
    return pl.pallas_call(
        _experts_kernel,
        grid_spec=pltpu.PrefetchScalarGridSpec(
            num_scalar_prefetch=2,
            grid=(N_EXPERT_BLKS,),
            in_specs=[
                pl.BlockSpec((SC_SPLIT, EXPERT_BLK, SC_ROW), row),
                pl.BlockSpec((1, D_MODEL, D_FF_EXPERT), wsel, pipeline_mode=single),
                pl.BlockSpec((1, D_MODEL, D_FF_EXPERT), wsel, pipeline_mode=single),
                pl.BlockSpec((1, D_FF_EXPERT, D_MODEL), wsel, pipeline_mode=single),
            ],
            out_specs=pl.BlockSpec((SC_SPLIT, EXPERT_BLK, SC_ROW), row),
            scratch_shapes=[pltpu.VMEM((D_MODEL, D_FF_EXPERT), BF16),
                            pltpu.VMEM((D_MODEL, D_FF_EXPERT), BF16),
                            pltpu.VMEM((D_FF_EXPERT, D_MODEL), BF16)],
        ),
        out_shape=jax.ShapeDtypeStruct((SC_SPLIT, N_SORTED, SC_ROW), jnp.uint32),
        compiler_params=_cparams("arbitrary"),
        name="moe_experts",
    )(blk_expert, n_used, xs, w1, w3, w2)


def _combine_kernel(x1_ref, r_ref, y_ref, mod_ref, fg_ref, o_ref):
    def expert_out(slot):
        hi, lo = _load_packed(y_ref.at[slot])
        return jnp.concatenate([hi, lo], axis=-1)
    w1 = r_ref[:, ROUTE_W1:ROUTE_W1 + 1]
    w2 = r_ref[:, ROUTE_W2:ROUTE_W2 + 1]
    y = w1 * expert_out(0) + w2 * expert_out(1)
    x2 = x1_ref[...] + mod_ref[0, 5:6, :] * y
    o_ref[...] = _rms(x2, fg_ref[...])


def _combine_call(x1, route, y_tok, mod, fg):
    n_tiles = N_LAT // TM
    row = lambda i: (i, 0)
    return pl.pallas_call(
        _combine_kernel,
        grid=(n_tiles,),
        in_specs=[
            pl.BlockSpec((TM, D_MODEL), row),
            pl.BlockSpec((TM, LANES), row),
            pl.BlockSpec((2, SC_SPLIT, TM, SC_ROW), lambda i: (0, 0, i, 0)),
            pl.BlockSpec((1, 6, D_MODEL), lambda i: (i // SEQ_TILES, 0, 0)),
            pl.BlockSpec((1, D_MODEL), lambda i: (0, 0)),
        ],
        out_specs=pl.BlockSpec((TM, D_MODEL), row),
        out_shape=jax.ShapeDtypeStruct((N_LAT, D_MODEL), F32),
        compiler_params=_cparams("parallel"),
        name="moe_combine",
    )(x1, route, y_tok, mod, fg)


def _moe_routed(h2p, x1, route, mod, w1, w3, w2, fg):
    dest, cnt = _route_call(route)
    blks = (cnt[0, :N_EXPERTS].astype(jnp.int32) + (EXPERT_BLK - 1)) // EXPERT_BLK
    blk_end = jnp.cumsum(blks)
    n_used = blk_end[-1:]
    b = jnp.minimum(jnp.arange(N_EXPERT_BLKS, dtype=jnp.int32), n_used[0] - 1)
    blk_expert = jnp.sum((b[:, None] >= blk_end[None, :]).astype(jnp.int32), axis=1)
    piece = jnp.arange(SC_SPLIT, dtype=jnp.int32)[:, None] * N_SORTED
    idx = [(piece + dest[:, slot][None, :]).reshape(SC_SPLIT * N_LAT) for slot in range(2)]
    xs = _sc_scatter(h2p.reshape(SC_SPLIT * N_LAT, SC_ROW), idx[0], idx[1], SC_SPLIT * N_SORTED)
    ys = _experts_call(blk_expert, n_used, xs.reshape(SC_SPLIT, N_SORTED, SC_ROW), w1, w3, w2)
    y_tok = _sc_gather(ys.reshape(SC_SPLIT * N_SORTED, SC_ROW), jnp.concatenate(idx))
    return _combine_call(x1, route, y_tok.reshape(2, SC_SPLIT, N_LAT, SC_ROW), mod, fg)


def _rope_partner_perm():
    perm, sign = [], []
    for j in range(QK_ROPE):
        first_half = (j % AXIS_ROPE) < ROPE_FREQS
        perm.append(j + ROPE_FREQS if first_half else j - ROPE_FREQS)
        sign.append(-1.0 if first_half else 1.0)
    return jnp.array(perm, jnp.int32), jnp.array(sign, F32)


def _rope_tables():
    t = jnp.arange(SEQ)
    row = (t // GRID_W).astype(F32)
    col = (t % GRID_W).astype(F32)
    inv_freq = ROPE_BASE ** (-2.0 * jnp.arange(ROPE_FREQS, dtype=F32) / AXIS_ROPE)
    ang = jnp.concatenate([row[:, None] * inv_freq, row[:, None] * inv_freq,
                           col[:, None] * inv_freq, col[:, None] * inv_freq], axis=1)
    cos = jnp.concatenate([jnp.cos(ang), jnp.ones((N_CTX, QK_ROPE), F32)], axis=0)
    sin = jnp.concatenate([jnp.sin(ang), jnp.zeros((N_CTX, QK_ROPE), F32)], axis=0)
    n = N_CTX + SEQ
    pad32 = jnp.zeros((n, HEAD_PAD - D_QK), F32)
    qs = ATT_SCALE * math.log2(math.e)
    cq = jnp.concatenate([jnp.full((n, QK_NOPE), qs, F32), qs * cos, pad32], axis=1)
    sq = jnp.concatenate([jnp.zeros((n, QK_NOPE), F32), qs * sin, pad32], axis=1)
    cs = jnp.concatenate([cos, sin, jnp.zeros((n, LANES - 2 * QK_ROPE), F32)], axis=1)
    return cq, sq, cs


def _layer_weights(w_in, w_uq, w_ukv):
    perm, sign = _rope_partner_perm()
    s0 = D_SSM + Q_LORA + KV_LORA
    kr_w = w_in[:, s0:s0 + QK_ROPE]
    wut = w_in[:, :D_SSM].T.astype(BF16)
    win = jnp.concatenate([w_in[:, D_SSM:s0], kr_w, kr_w[:, perm] * sign,
                           jnp.zeros((D_MODEL, LANES - 2 * QK_ROPE), F32)], axis=1).astype(BF16)
    uq = w_uq.reshape(Q_LORA, N_HEADS, D_QK)
    nope, rope = uq[..., :QK_NOPE], uq[..., QK_NOPE:]
    zpad = jnp.zeros((Q_LORA, N_HEADS, HEAD_PAD - D_QK), F32)
    wq1 = jnp.concatenate([nope, rope, zpad], axis=-1).reshape(Q_LORA, N_HEADS * HEAD_PAD).astype(BF16)
    wq2 = jnp.concatenate([jnp.zeros_like(nope), rope[..., perm] * sign, zpad], axis=-1)
    wq2 = wq2.reshape(Q_LORA, N_HEADS * HEAD_PAD).astype(BF16)
    ukv = w_ukv.reshape(KV_LORA, N_HEADS, QK_NOPE + V_HEAD)
    wk = jnp.concatenate([ukv[..., :QK_NOPE], jnp.zeros((KV_LORA, N_HEADS, HEAD_PAD - QK_NOPE), F32)],
                         axis=-1).reshape(KV_LORA, N_HEADS * HEAD_PAD)
    eye = jnp.eye(QK_ROPE, dtype=F32)
    place = jnp.concatenate([jnp.zeros((QK_ROPE, QK_NOPE), F32), eye,
                             jnp.zeros((QK_ROPE, HEAD_PAD - D_QK), F32)], axis=1)
    place = jnp.tile(place, (1, N_HEADS))
    spread = jnp.concatenate([place, place, jnp.zeros((LANES - 2 * QK_ROPE, N_HEADS * HEAD_PAD), F32)], 0)
    wkk = jnp.concatenate([wk, spread], axis=0).astype(BF16)
    wv = jnp.concatenate([ukv[..., QK_NOPE:], jnp.zeros((KV_LORA, N_HEADS, HEAD_PAD - V_HEAD), F32)],
                         axis=-1).reshape(KV_LORA, N_HEADS * HEAD_PAD).astype(BF16)
    return wut, win, wq1, wq2, wkk, wv


def kernel(x, c, ctx, c_ctx, w_ada, b_ada, norm_mix, norm_ffn, w_in, q_norm, kv_norm, w_uq, w_ukv,
           ssm_a_re, ssm_a_im, ssm_log_dt, ssm_b_re, ssm_b_im, ssm_c_re, ssm_c_im, ssm_d, w_glu,
           b_glu, w_out, ffn_w1, ffn_w3, ffn_w2, moe_router, moe_w1, moe_w3, moe_w2, final_norm):
    assert x.shape == (BATCH, SEQ, D_MODEL) and ctx.shape == (BATCH, CTX_LEN, D_MODEL)
    cond = jnp.concatenate([c, c_ctx[None, :], jnp.zeros((MOD_ROWS - BATCH - 1, D_MODEL), F32)], axis=0)
    mod_all = _ada_call(cond, w_ada, b_ada).reshape(DEPTH, MOD_ROWS, 6, D_MODEL)
    cq_t, sq_t, cs_t = _rope_tables()
    xs = (x.reshape(N_LAT, D_MODEL), ctx.reshape(N_CTX, D_MODEL))

    out = None
    for i in range(DEPTH):
        last = i == DEPTH - 1
        mod = mod_all[i]
        wut, win, wq1, wq2, wkk, wv = _layer_weights(w_in[i], w_uq[i], w_ukv[i])
        u_t, q, k, v = _inproj_call(xs, mod, norm_mix[i][None, :], wut, win, q_norm[i][None, :],
                                    kv_norm[i][None, :], wq1, wq2, wkk, wv, cq_t, sq_t, cs_t)
        tabs = _ssm_tables(ssm_a_re[i], ssm_a_im[i], ssm_log_dt[i], ssm_b_re[i], ssm_b_im[i],
                           ssm_c_re[i], ssm_c_im[i], ssm_d[i])
        y_ssm_t = _ssm_call(u_t.reshape(D_SSM, N_CHUNK, SSM_T), *tabs).reshape(D_SSM, N_TOT)
        y_att = _attn_latent_call(q, k, v)
        if last:
            n_tiles = LAT_TILES
        else:
            y_att = _attn_ctx_call(q, k, v, y_att)
            n_tiles = N_TOT // TM
        j = i // 2
        mix_args = (xs, y_ssm_t, y_att, mod, w_glu[i].T.astype(BF16), b_glu[i][:, None],
                    w_out[i].astype(BF16), norm_ffn[i][None, :])
        if i % 2 == 0:
            assert not last
            ffn = (ffn_w1[j].astype(BF16), ffn_w3[j].astype(BF16), ffn_w2[j].astype(BF16))
            xs = tuple(_mix_call(*mix_args, ffn=ffn, n_tiles=n_tiles))
        else:
            assert last
            r = moe_router[j]
            r_top = lax.bitcast_convert_type(
                lax.bitcast_convert_type(r, jnp.uint32) & jnp.uint32(0xFFFF0000), F32)
            r_hi = r_top.astype(BF16)
            r_lo = (r - r_top).astype(BF16)
            zr = jnp.zeros((D_MODEL, LANES - 2 * N_EXPERTS), BF16)
            router = jnp.stack([jnp.concatenate([r_hi, r_lo, zr], axis=1),
                                jnp.concatenate([r_hi, jnp.zeros_like(r_lo), zr], axis=1)])
            x1, h2p, route = _mix_call(*mix_args, router=router, n_tiles=n_tiles)
            out = _moe_routed(h2p, x1, route, mod, moe_w1[j], moe_w3[j], moe_w2[j],
                              final_norm[None, :])
    return out.reshape(BATCH, SEQ, D_MODEL)
```

```python
import functools
import math

import jax
import jax.numpy as jnp
import numpy as np
from jax import lax
from jax.experimental import pallas as pl
from jax.experimental.pallas import tpu as pltpu
from jax.experimental.pallas import tpu_sc as plsc

D_MODEL = 1024
BATCH = 4
SEQ = 8192
DEPTH = 2
GRID_W = 64
CTX_LEN = 256
D_SSM = 512
SSM_GROUP = 16
N_SSM_GROUPS = D_SSM // SSM_GROUP
SSM_STATE = 64
N_HEADS = 8
QK_NOPE = 64
QK_ROPE = 32
V_HEAD = 64
Q_LORA = 256
KV_LORA = 128
D_QK = QK_NOPE + QK_ROPE
D_ATT = N_HEADS * V_HEAD
D_MIX = D_SSM + D_ATT
D_IN = D_SSM + Q_LORA + KV_LORA + QK_ROPE
AXIS_ROPE = QK_ROPE // 2
ROPE_FREQS = AXIS_ROPE // 2
ROPE_BASE = 10000.0
ATT_SCALE = 1.0 / math.sqrt(D_QK)
D_FF = 2816
N_EXPERTS = 8
D_FF_EXPERT = 1408
EPS = 1e-6

N_CTX = BATCH * CTX_LEN
N_LAT = BATCH * SEQ
N_TOT = N_CTX + N_LAT

LANES = 128
HEAD_PAD = 128
TM = 512
LAT_TILES = N_LAT // TM
SEQ_TILES = SEQ // TM
TQ = 512
TK = 2048
SSM_T = 128
N_CHUNK_LAT = N_LAT // SSM_T
N_CHUNK = N_TOT // SSM_T
SSM_K = SSM_GROUP * SSM_T
FF_CHUNK = 256
N_FF_CHUNKS = D_FF // FF_CHUNK
D_IN_REST = Q_LORA + KV_LORA + LANES
MOD_ROWS = 8
VMEM_LIMIT = 52 * 1024 * 1024

F32 = jnp.float32
BF16 = jnp.bfloat16
HI = lax.Precision.HIGHEST


def _cparams(*sem):
    return pltpu.CompilerParams(dimension_semantics=sem, vmem_limit_bytes=VMEM_LIMIT)


def _const_spec(shape):
    nd = len(shape)
    return pl.BlockSpec(shape, lambda *_: (0,) * nd, pipeline_mode=pl.Buffered(1))


def _mod_row(i):
    return jnp.where(i < LAT_TILES, i // SEQ_TILES, BATCH)


def _pos_tile(i):
    return jnp.where(i < LAT_TILES, i % SEQ_TILES, SEQ_TILES + i - LAT_TILES)


def _rms(x, g):
    ms = jnp.mean(x * x, axis=-1, keepdims=True)
    return x * lax.rsqrt(ms + EPS) * g


ADA_TN = 1536


def _ada_kernel(c_ref, w_ref, b_ref, o_ref):
    c = c_ref[...]
    s = c * jax.nn.sigmoid(c)
    o_ref[0] = jnp.dot(s, w_ref[0], precision=HI, preferred_element_type=F32) + b_ref[0]


def _ada_call(cond, w_ada, b_ada):
    n_col = 6 * D_MODEL // ADA_TN
    return pl.pallas_call(
        _ada_kernel,
        grid=(DEPTH, n_col),
        in_specs=[
            pl.BlockSpec((MOD_ROWS, D_MODEL), lambda l, j: (0, 0)),
            pl.BlockSpec((1, D_MODEL, ADA_TN), lambda l, j: (l, 0, j)),
            pl.BlockSpec((1, 1, ADA_TN), lambda l, j: (l, 0, j)),
        ],
        out_specs=pl.BlockSpec((1, MOD_ROWS, ADA_TN), lambda l, j: (l, 0, j)),
        out_shape=jax.ShapeDtypeStruct((DEPTH, MOD_ROWS, 6 * D_MODEL), F32),
        compiler_params=_cparams("arbitrary", "arbitrary"),
        name="ada_mod",
    )(cond, w_ada, b_ada.reshape(DEPTH, 1, 6 * D_MODEL))


def _tile_rows(refs, n_x):
    if n_x == 1:
        return refs[0][...]
    return jnp.where(pl.program_id(0) < LAT_TILES, refs[0][...], refs[1][...])


def _x_specs(n_x):
    if n_x == 1:
        return [pl.BlockSpec((TM, D_MODEL), lambda i: (i, 0))]
    return [pl.BlockSpec((TM, D_MODEL), lambda i: (jnp.minimum(i, LAT_TILES - 1), 0)),
            pl.BlockSpec((TM, D_MODEL), lambda i: (jnp.maximum(i - LAT_TILES, 0), 0))]


def _inproj_kernel(*refs, n_x):
    (mod_ref, g_ref, wut_ref, win_ref, qg_ref, kvg_ref, wq1_ref, wq2_ref, wkk_ref, wv_ref,
     cq_ref, sq_ref, cs_ref, ut_ref, q_ref, k_ref, v_ref) = refs[n_x:]
    x = _tile_rows(refs, n_x)
    sh = mod_ref[0, 0:1, :]
    sc = mod_ref[0, 1:2, :]
    xm = (_rms(x, g_ref[...]) * (1.0 + sc) + sh).astype(BF16)
    ut_ref[...] = lax.dot_general(wut_ref[...], xm, (((1,), (1,)), ((), ())),
                                  preferred_element_type=F32)
    z = jnp.dot(xm, win_ref[...], preferred_element_type=F32)
    qn = _rms(z[:, :Q_LORA], qg_ref[...]).astype(BF16)
    kvn = _rms(z[:, Q_LORA:Q_LORA + KV_LORA], kvg_ref[...]).astype(BF16)
    krr = (z[:, Q_LORA + KV_LORA:] * cs_ref[...]).astype(BF16)
    q1 = jnp.dot(qn, wq1_ref[...], preferred_element_type=F32)
    q2 = jnp.dot(qn, wq2_ref[...], preferred_element_type=F32)
    cq = cq_ref[...]
    sq = sq_ref[...]
    for h in range(N_HEADS):
        sl = slice(h * HEAD_PAD, (h + 1) * HEAD_PAD)
        q_ref[:, sl] = (q1[:, sl] * cq + q2[:, sl] * sq).astype(q_ref.dtype)
    kin = jnp.concatenate([kvn, krr], axis=-1)
    k_ref[...] = jnp.dot(kin, wkk_ref[...], preferred_element_type=F32).astype(k_ref.dtype)
    vv = jnp.dot(kvn, wv_ref[...], preferred_element_type=F32)
    lane = lax.broadcasted_iota(jnp.int32, vv.shape, 1)
    v_ref[...] = jnp.where(lane % HEAD_PAD == V_HEAD, 1.0, vv).astype(v_ref.dtype)


def _inproj_call(xs, mod, g_mix, wut, win, qg, kvg, wq1, wq2, wkk, wv, cq_t, sq_t, cs_t):
    n_tiles = N_TOT // TM
    row = lambda i: (i, 0)
    pos = lambda i: (_pos_tile(i), 0)
    return pl.pallas_call(
        functools.partial(_inproj_kernel, n_x=len(xs)),
        grid=(n_tiles,),
        in_specs=_x_specs(len(xs)) + [
            pl.BlockSpec((1, 6, D_MODEL), lambda i: (_mod_row(i), 0, 0)),
            _const_spec((1, D_MODEL)),
            _const_spec((D_SSM, D_MODEL)),
            _const_spec((D_MODEL, D_IN_REST)),
            _const_spec((1, Q_LORA)),
            _const_spec((1, KV_LORA)),
            _const_spec((Q_LORA, N_HEADS * HEAD_PAD)),
            _const_spec((Q_LORA, N_HEADS * HEAD_PAD)),
            _const_spec((2 * KV_LORA, N_HEADS * HEAD_PAD)),
            _const_spec((KV_LORA, N_HEADS * HEAD_PAD)),
            pl.BlockSpec((TM, LANES), pos),
            pl.BlockSpec((TM, LANES), pos),
            pl.BlockSpec((TM, LANES), pos),
        ],
        out_specs=[
            pl.BlockSpec((D_SSM, TM), lambda i: (0, i)),
            pl.BlockSpec((TM, N_HEADS * HEAD_PAD), row),
            pl.BlockSpec((TM, N_HEADS * HEAD_PAD), row),
            pl.BlockSpec((TM, N_HEADS * HEAD_PAD), row),
        ],
        out_shape=[
            jax.ShapeDtypeStruct((D_SSM, N_TOT), F32),
            jax.ShapeDtypeStruct((N_TOT, N_HEADS * HEAD_PAD), BF16),
            jax.ShapeDtypeStruct((N_TOT, N_HEADS * HEAD_PAD), BF16),
            jax.ShapeDtypeStruct((N_TOT, N_HEADS * HEAD_PAD), BF16),
        ],
        compiler_params=_cparams("parallel"),
        name="in_proj",
    )(*xs, mod, g_mix, wut, win, qg, kvg, wq1, wq2, wkk, wv, cq_t, sq_t, cs_t)


def _attn_kernel(*refs, n_chunks, tq):
    if n_chunks:
        q_ref, k_ref, v_ref, kc_ref, vc_ref, o_ref, s_scr = refs
    else:
        q_ref, kc_ref, vc_ref, _, o_ref, s_scr = refs
    heads =[slice(hh * HEAD_PAD, (hh + 1) * HEAD_PAD) for hh in range(2)]
    qs = [q_ref[:, sl] for sl in heads]

    def put_scores(slot, k_at, width):
        for hh in range(2):
            s_scr[slot, hh, :, :width] = lax.dot_general(
                qs[hh], k_at(heads[hh]), (((1,), (1,)), ((), ())), preferred_element_type=F32)

    def consume(carry, slot, v_at, width):
        new = []
        for hh in range(2):
            m, acc = carry[hh]
            s = s_scr[slot, hh, :, :width]
            m_new = jnp.maximum(m, jnp.max(s, axis=-1, keepdims=True))
            alpha = jnp.exp2(m - m_new)
            p = jnp.exp2(s - m_new).astype(BF16)
            acc = alpha * acc + jnp.dot(p, v_at(heads[hh]), preferred_element_type=F32)
            new.append((m_new, acc))
        return tuple(new)

    def chunk(c):
        rows = pl.ds(pl.multiple_of(c * TK, TK), TK)
        return (lambda sl: k_ref[rows, sl]), (lambda sl: v_ref[rows, sl])

    ctx_k, ctx_v = (lambda sl: kc_ref[:, sl]), (lambda sl: vc_ref[:, sl])
    carry = tuple((jnp.full((tq, 1), -jnp.inf, F32), jnp.zeros((tq, HEAD_PAD), F32))
                  for _ in range(2))
    if n_chunks:
        assert n_chunks % 2 == 0
        put_scores(0, chunk(0)[0], TK)

        def body(j, carry):
            c0 = 2 * j
            put_scores(1, chunk(c0 + 1)[0], TK)
            carry = consume(carry, 0, chunk(c0)[1], TK)
            put_scores(0, chunk(c0 + 2)[0], TK)
            return consume(carry, 1, chunk(c0 + 1)[1], TK)

        carry = lax.fori_loop(0, n_chunks // 2 - 1, body, carry)
        put_scores(1, chunk(n_chunks - 1)[0], TK)
        carry = consume(carry, 0, chunk(n_chunks - 2)[1], TK)
        put_scores(0, ctx_k, CTX_LEN)
        carry = consume(carry, 1, chunk(n_chunks - 1)[1], TK)
    else:
        put_scores(0, ctx_k, CTX_LEN)
    carry = consume(carry, 0, ctx_v, CTX_LEN)
    outs = [acc[:, :V_HEAD] / acc[:, V_HEAD:V_HEAD + 1] for _, acc in carry]
    o_ref[...] = jnp.concatenate(outs, axis=-1).astype(o_ref.dtype)


def _attn_latent_call(q, k, v):
    qt = SEQ // TQ
    ctx0 = N_LAT // CTX_LEN
    return pl.pallas_call(
        functools.partial(_attn_kernel, n_chunks=SEQ // TK, tq=TQ),
        grid=(BATCH, N_HEADS // 2, qt),
        in_specs=[
            pl.BlockSpec((TQ, 2 * HEAD_PAD), lambda b, h, i: (b * qt + i, h)),
            pl.BlockSpec((SEQ, 2 * HEAD_PAD), lambda b, h, i: (b, h)),
            pl.BlockSpec((SEQ, 2 * HEAD_PAD), lambda b, h, i: (b, h)),
            pl.BlockSpec((CTX_LEN, 2 * HEAD_PAD), lambda b, h, i: (ctx0 + b, h)),
            pl.BlockSpec((CTX_LEN, 2 * HEAD_PAD), lambda b, h, i: (ctx0 + b, h)),
        ],
        out_specs=pl.BlockSpec((TQ, 2 * V_HEAD), lambda b, h, i: (b * qt + i, h)),
        out_shape=jax.ShapeDtypeStruct((N_TOT, D_ATT), BF16),
        scratch_shapes=[pltpu.VMEM((2, 2, TQ, TK), F32)],
        compiler_params=_cparams("parallel", "parallel", "arbitrary"),
        name="attn_latent",
    )(q, k, v, k, v)


def _attn_ctx_call(q, k, v, y_att):
    ctx0 = N_LAT // CTX_LEN
    return pl.pallas_call(
        functools.partial(_attn_kernel, n_chunks=0, tq=CTX_LEN),
        grid=(BATCH, N_HEADS // 2),
        in_specs=[
            pl.BlockSpec((CTX_LEN, 2 * HEAD_PAD), lambda b, h: (ctx0 + b, h)),
            pl.BlockSpec((CTX_LEN, 2 * HEAD_PAD), lambda b, h: (ctx0 + b, h)),
            pl.BlockSpec((CTX_LEN, 2 * HEAD_PAD), lambda b, h: (ctx0 + b, h)),
            pl.BlockSpec(memory_space=pl.ANY),
        ],
        out_specs=pl.BlockSpec((CTX_LEN, 2 * V_HEAD), lambda b, h: (ctx0 + b, h)),
        out_shape=jax.ShapeDtypeStruct((N_TOT, D_ATT), BF16),
        input_output_aliases={3: 0},
        scratch_shapes=[pltpu.VMEM((1, 2, CTX_LEN, CTX_LEN), F32)],
        compiler_params=_cparams("parallel", "parallel"),
        name="attn_ctx",
    )(q, k, v, y_att)


SSM_C = 64
SSM_HALVES = SSM_T // SSM_C
SSM_KC = SSM_GROUP * SSM_C
LAG_ROWS = 2 * SSM_C
PT_LAG, PT_INC, PT_OUT = 0, LAG_ROWS, LAG_ROWS + SSM_C
PT_ROWS = LAG_ROWS + 2 * SSM_C
NT_DIMS = (((1,), (1,)), ((), ()))


def _ssm_kernel(u_ref, pta_ref, ptb_ref, rows_ref, cc_ref, at_ref, dv_ref, y_ref,
                abt_hi_scr, abt_lo_scr, wl_scr, m_scr, ws_scr, wct_scr, s_scr, h_scr):
    H, C = SSM_GROUP, SSM_C
    half_lane = lax.broadcasted_iota(jnp.int32, (1, LANES), 1) < C

    def split(v):
        hi = v.astype(BF16)
        return hi, (v - hi.astype(F32)).astype(BF16)

    def scaled(row0, n_rows, ra, rb):
        return pta_ref[0, row0:row0 + n_rows, :] * ra + ptb_ref[0, row0:row0 + n_rows, :] * rb

    def build_tables(i, _):
        row0 = pl.multiple_of(i * C, C)
        b_re = rows_ref[0, pl.ds(i, 1), :]
        b_im = rows_ref[0, pl.ds(H + i, 1), :]
        lag_rows = pl.ds(pl.multiple_of(i * LAG_ROWS, LAG_ROWS), LAG_ROWS)
        abt_hi_scr[lag_rows, :], abt_lo_scr[lag_rows, :] = split(
            scaled(PT_LAG, LAG_ROWS, b_re, b_im))
        ws_scr[pl.ds(row0, C), :] = scaled(PT_INC, C, b_re, b_im).astype(BF16)
        c_re = rows_ref[0, pl.ds(2 * H + i, 1), :]
        c_im = rows_ref[0, pl.ds(3 * H + i, 1), :]
        wct_scr[pl.ds(row0, C), :] = scaled(PT_OUT, C, c_re, c_im).astype(BF16)
        return 0

    lax.fori_loop(0, H, build_tables, 0)
    cc_hi, cc_lo = split(cc_ref[0])
    nt = lambda a, b: lax.dot_general(a, b, NT_DIMS, preferred_element_type=F32)
    wl_scr[...] = (nt(cc_hi, abt_hi_scr[...]) + nt(cc_hi, abt_lo_scr[...])
                   + nt(cc_lo, abt_hi_scr[...]))

    slot2 = lax.broadcasted_iota(jnp.int32, (1, 2 * LANES), 1)
    first_low = (slot2 < C) | (slot2 > 2 * LANES - C)
    first_high = slot2 < LANES

    def toeplitz_pair(x):
        return pltpu.roll(jnp.broadcast_to(x, (C, 2 * LANES)), 0, 1, stride=1, stride_axis=0)

    def build_toeplitz(ci, _):
        row0 = pl.multiple_of(ci * C, C)
        slots = pl.ds(pl.multiple_of(ci * LAG_ROWS, LAG_ROWS), LAG_ROWS)

        def lags(co, shift):
            v = wl_scr[co:co + 1, slots]
            if shift:
                v = pltpu.roll(v, C, 1)
            return jnp.concatenate([v, v], axis=1)

        for ka in range(0, H // 2, 2):
            kb = ka + 1
            low = toeplitz_pair(jnp.where(first_low, lags(2 * ka, False), lags(2 * kb, False)))
            high = toeplitz_pair(jnp.where(first_high, lags(2 * ka + 1, True),
                                           lags(2 * kb + 1, True)))
            for k, lanes in ((ka, slice(0, LANES)), (kb, slice(LANES, 2 * LANES))):
                m_scr[pl.ds(row0, C), k * LANES:(k + 1) * LANES] = jnp.where(
                    half_lane, low[:, lanes], high[:, lanes]).astype(BF16)
        return 0

    lax.fori_loop(0, H, build_toeplitz, 0)

    def chunk_operand(hf):
        cols = []
        for k in range(H // 2):
            a, b = u_ref[2 * k], u_ref[2 * k + 1]
            if hf == 0:
                cols.append(jnp.where(half_lane, a, pltpu.roll(b, C, 1)))
            else:
                cols.append(jnp.where(half_lane, pltpu.roll(a, C, 1), b))
        return jnp.concatenate(cols, axis=1).astype(BF16)

    ys = []
    for hf in range(SSM_HALVES):
        u = chunk_operand(hf)
        ys.append(jnp.dot(u, m_scr[...], preferred_element_type=F32))
        s = jnp.dot(u, ws_scr[...], preferred_element_type=F32)
        for d in range(2):
            s_d = s[:, d * LANES:(d + 1) * LANES]
            s_scr[hf, d] = s_d
            s_scr[hf, 2 + d] = pltpu.roll(s_d, SSM_STATE, 1)

    n_lat, n_ctx = SEQ // SSM_T, CTX_LEN // SSM_T
    ctx = [((N_CHUNK_LAT + c, n_ctx), hf) for c in range(n_ctx) for hf in range(SSM_HALVES)]
    lat = [((k, n_lat), hf) for k in range(n_lat) for hf in range(SSM_HALVES)]
    coef = [jnp.broadcast_to(at_ref[0, r:r + 1, :], (BATCH, 2 * SSM_STATE)) for r in range(4)]

    def advance(h, h_sw, d, rows, hf):
        a0, a1 = coef[2 * d], coef[2 * d + 1]
        return (h * a0 + h_sw * a1 + s_scr[hf, d, rows, :],
                h_sw * a0 - h * a1 + s_scr[hf, 2 + d, rows, :])

    zero = jnp.zeros((BATCH, 2 * SSM_STATE), F32)
    h_f, h_f_sw, h_r, h_r_sw = zero, zero, zero, zero
    for ((sf, stf), cf), ((sr, strd), cr) in zip(ctx + lat, ctx[::-1] + lat[::-1]):
        rows_f = pl.ds(sf, BATCH, stride=stf)
        rows_r = pl.ds(sr, BATCH, stride=strd)
        h_scr[cf, 0, rows_f, :] = h_f
        h_f, h_f_sw = advance(h_f, h_f_sw, 0, rows_f, cf)
        h_scr[cr, 1, rows_r, :] = h_r
        h_r, h_r_sw = advance(h_r, h_r_sw, 1, rows_r, cr)

    for hf in range(SSM_HALVES):
        h_in = jnp.concatenate([h_scr[hf, 0], h_scr[hf, 1]], axis=-1).astype(BF16)
        ys[hf] = ys[hf] + lax.dot_general(h_in, wct_scr[...], NT_DIMS,
                                          preferred_element_type=F32)
    for k in range(H // 2):
        y0, y1 = ys[0][:, k * LANES:(k + 1) * LANES], ys[1][:, k * LANES:(k + 1) * LANES]
        for c, yc in ((2 * k, jnp.where(half_lane, y0, pltpu.roll(y1, C, 1))),
                      (2 * k + 1, jnp.where(half_lane, pltpu.roll(y0, C, 1), y1))):
            y_ref[c] = yc + u_ref[c] * dv_ref[0, :, c * SSM_T:(c + 1) * SSM_T]


def _ssm_call(u_t, pta, ptb, rows, cc, at, dv):
    g3 = lambda g: (g, 0, 0)
    return pl.pallas_call(
        _ssm_kernel,
        grid=(N_SSM_GROUPS,),
        in_specs=[
            pl.BlockSpec((SSM_GROUP, N_CHUNK, SSM_T), g3),
            pl.BlockSpec((1, PT_ROWS, 4 * SSM_STATE), g3),
            pl.BlockSpec((1, PT_ROWS, 4 * SSM_STATE), g3),
            pl.BlockSpec((1, 4 * SSM_GROUP, 4 * SSM_STATE), g3),
            pl.BlockSpec((1, SSM_GROUP, 4 * SSM_STATE), g3),
            pl.BlockSpec((1, 4, 2 * SSM_STATE), g3),
            pl.BlockSpec((1, 1, SSM_K), g3),
        ],
        out_specs=pl.BlockSpec((SSM_GROUP, N_CHUNK, SSM_T), g3),
        out_shape=jax.ShapeDtypeStruct((D_SSM, N_CHUNK, SSM_T), F32),
        scratch_shapes=[
            pltpu.VMEM((SSM_GROUP * LAG_ROWS, 4 * SSM_STATE), BF16),
            pltpu.VMEM((SSM_GROUP * LAG_ROWS, 4 * SSM_STATE), BF16),
            pltpu.VMEM((SSM_GROUP, SSM_GROUP * LAG_ROWS), F32),
            pltpu.VMEM((SSM_KC, SSM_KC), BF16),
            pltpu.VMEM((SSM_KC, 4 * SSM_STATE), BF16),
            pltpu.VMEM((SSM_KC, 4 * SSM_STATE), BF16),
            pltpu.VMEM((SSM_HALVES, 4, N_CHUNK, 2 * SSM_STATE), F32),
            pltpu.VMEM((SSM_HALVES, 2, N_CHUNK, 2 * SSM_STATE), F32),
        ],
        compiler_params=_cparams("parallel"),
        name="s5_mixer",
    )(u_t, pta, ptb, rows, cc, at, dv)


def _ssm_tables(a_re, a_im, log_dt, b_re, b_im, c_re, c_im, d_skip):
    G, P, H, T = N_SSM_GROUPS, SSM_STATE, SSM_GROUP, SSM_T
    a_re, a_im = a_re.astype(F32), a_im.astype(F32)
    dt = jnp.exp(log_dt.astype(F32))[..., None]
    den = a_re * a_re + a_im * a_im
    mag1 = jnp.exp(dt * a_re)
    ab_re, ab_im = mag1 * jnp.cos(dt * a_im), mag1 * jnp.sin(dt * a_im)
    num_re = ab_re - 1.0
    f_re = (num_re * a_re + ab_im * a_im) / den
    f_im = (ab_im * a_re - num_re * a_im) / den
    b_re, b_im = b_re.astype(F32), b_im.astype(F32)
    bb_re = f_re[..., None] * b_re - f_im[..., None] * b_im
    bb_im = f_re[..., None] * b_im + f_im[..., None] * b_re
    c_re, c_im = c_re.astype(F32), c_im.astype(F32)
    la, th = dt * a_re, dt * a_im

    def powers(d, n):
        mag = jnp.exp(la[d][:, None, :] * n[None, :, None])
        ph = th[d][:, None, :] * n[None, :, None]
        return mag * jnp.cos(ph), mag * jnp.sin(ph)

    C = SSM_C
    slot = np.arange(LAG_ROWS)
    lag_f = np.where(slot < C, slot, -1)
    lag_r = np.where((-slot) % LAG_ROWS < C, (-slot) % LAG_ROWS, -1)
    s_idx = np.arange(C)
    f_all = np.concatenate([lag_f, C - 1 - s_idx, s_idx + 1])
    r_all = np.concatenate([lag_r, s_idx, C - s_idx])
    f_exp, r_exp = jnp.asarray(np.maximum(f_all, 0), F32), jnp.asarray(np.maximum(r_all, 0), F32)
    f_on, r_on = jnp.asarray(f_all >= 0, F32), jnp.asarray(r_all >= 0, F32)
    lane = jnp.arange(4 * P)
    is_fwd = (lane < 2 * P)[None, :]
    is_re = ((lane // P) % 2 == 0)[None, None, :]
    expo = jnp.where(is_fwd, f_exp[:, None], r_exp[:, None])[None]
    on = jnp.where(is_fwd, f_on[:, None], r_on[:, None])[None]
    la4 = jnp.concatenate([la[0], la[0], la[1], la[1]], axis=-1)[:, None, :]
    th4 = jnp.concatenate([th[0], th[0], th[1], th[1]], axis=-1)[:, None, :]
    mag = jnp.exp(la4 * expo) * on
    p_re, p_im = mag * jnp.cos(th4 * expo), mag * jnp.sin(th4 * expo)
    pta = jnp.where(is_re, p_re, p_im)
    ptb = jnp.where(is_re, -p_im, p_re)

    def per_channel(v):
        f, r = v[0].transpose(0, 2, 1), v[1].transpose(0, 2, 1)
        return jnp.concatenate([f, f, r, r], axis=-1)
    cf_re, cr_re = c_re[0], c_re[1]
    cf_im, cr_im = c_im[0], c_im[1]
    rows = jnp.concatenate([
        per_channel(bb_re), per_channel(bb_im),
        jnp.concatenate([cf_re, -cf_re, cr_re, -cr_re], axis=-1),
        jnp.concatenate([cf_im, -cf_im, cr_im, -cr_im], axis=-1)], axis=1)
    cc = jnp.concatenate([cf_re, -cf_im, cr_re, -cr_im], axis=-1)

    t_exp = jnp.full((1,), float(SSM_C), F32)
    (f_re_t, f_im_t), (r_re_t, r_im_t) = powers(0, t_exp), powers(1, t_exp)
    f_re_t, f_im_t, r_re_t, r_im_t = (v[:, 0, :] for v in (f_re_t, f_im_t, r_re_t, r_im_t))
    at = jnp.stack([jnp.concatenate([f_re_t, f_re_t], -1), jnp.concatenate([-f_im_t, f_im_t], -1),
                    jnp.concatenate([r_re_t, r_re_t], -1), jnp.concatenate([-r_im_t, r_im_t], -1)],
                   axis=1)
    dv = jnp.repeat(d_skip.astype(F32).reshape(G, H), T, axis=-1).reshape(G, 1, H * T)
    return pta, ptb, rows, cc, at, dv


ROUTE_E1, ROUTE_E2, ROUTE_W1, ROUTE_W2 = 0, 1, 2, 3


def _top2_route(logits):
    lane = lax.broadcasted_iota(jnp.int32, logits.shape, 1)
    lg = jnp.where(lane < N_EXPERTS, logits, -jnp.inf)
    m1 = jnp.max(lg, axis=-1, keepdims=True)
    i1 = jnp.min(jnp.where(lg == m1, lane, LANES), axis=-1, keepdims=True)
    lg2 = jnp.where(lane == i1, -jnp.inf, lg)
    m2 = jnp.max(lg2, axis=-1, keepdims=True)
    i2 = jnp.min(jnp.where(lg2 == m2, lane, LANES), axis=-1, keepdims=True)
    e2 = jnp.exp(m2 - m1)
    w1 = 1.0 / (1.0 + e2)
    rec = jnp.where(lane == ROUTE_E1, i1.astype(F32), 0.0)
    rec = jnp.where(lane == ROUTE_E2, i2.astype(F32), rec)
    rec = jnp.where(lane == ROUTE_W1, w1, rec)
    return jnp.where(lane == ROUTE_W2, e2 * w1, rec), i1, i2


def _expert_slots(i1, i2, taken):
    n = i1.shape[0]
    lane = lax.broadcasted_iota(jnp.int32, (n, LANES), 1)
    picked = jnp.where((lane == i1) | (lane == i2), 1.0, 0.0)
    earlier = (lax.broadcasted_iota(jnp.int32, (n, n), 1)
               < lax.broadcasted_iota(jnp.int32, (n, n), 0)).astype(BF16)
    rank = jnp.dot(earlier, picked.astype(BF16), preferred_element_type=F32)
    slot = rank + taken + (lane * EXPERT_CAP).astype(F32)
    d1 = jnp.sum(jnp.where(lane == i1, slot, 0.0), axis=-1, keepdims=True)
    d2 = jnp.sum(jnp.where(lane == i2, slot, 0.0), axis=-1, keepdims=True)
    dest = jnp.where(lane == 0, d1, jnp.where(lane == 1, d2, 0.0)).astype(jnp.int32)
    return dest, jnp.sum(picked, axis=0, keepdims=True)


def _pack_bf16_pairs(v):
    k = v.shape[1] // 2
    bits = pltpu.bitcast(v.astype(BF16).astype(F32), jnp.uint32)
    return (bits[:, :k] & jnp.uint32(0xFFFF0000)) | (bits[:, k:] >> 16)


def _unpack_bf16_pairs(w):
    hi = pltpu.bitcast(w & jnp.uint32(0xFFFF0000), F32)
    lo = pltpu.bitcast(w << 16, F32)
    return hi, lo


def _store_packed(ref, v):
    words = _pack_bf16_pairs(v)
    for s in range(SC_SPLIT):
        ref[s] = words[:, s * SC_ROW:(s + 1) * SC_ROW]


def _load_packed(ref):
    return _unpack_bf16_pairs(jnp.concatenate([ref[s] for s in range(SC_SPLIT)], axis=-1))


def _mix_kernel(*refs, with_router, n_x):
    rest = refs[n_x:]
    if with_router:
        (yst_ref, ya_ref, mod_ref, wglut_ref, bglu_ref, wout_ref, gffn_ref, router_ref,
         x1_ref, h2_ref, gate_ref, dest_ref, cnt_ref, taken_scr) = rest
    else:
        (yst_ref, ya_ref, mod_ref, wglut_ref, bglu_ref, wout_ref, gffn_ref,
         w1_ref, w3_ref, w2_ref, x2_ref) = rest
    zt = jax.nn.gelu(yst_ref[...], approximate=True)
    glt = zt * jax.nn.sigmoid(
        jnp.dot(wglut_ref[...], zt.astype(BF16), preferred_element_type=F32) + bglu_ref[...])
    mix = jnp.concatenate([glt.T.astype(BF16), ya_ref[...]], axis=-1)
    o = jnp.dot(mix, wout_ref[...], preferred_element_type=F32)
    x1 = _tile_rows(refs, n_x) + mod_ref[0, 2:3, :] * o
    h2 = _rms(x1, gffn_ref[...]) * (1.0 + mod_ref[0, 4:5, :]) + mod_ref[0, 3:4, :]
    if not with_router:
        acc = _swiglu(h2.astype(BF16), lambda sl: w1_ref[:, sl], lambda sl: w3_ref[:, sl],
                      lambda sl: w2_ref[sl, :], D_FF)
        x2_ref[...] = x1 + mod_ref[0, 5:6, :] * acc
    else:
        x1_ref[...] = x1
        _store_packed(h2_ref, h2)
        h_hi = h2.astype(BF16)
        h_lo = (h2 - h_hi.astype(F32)).astype(BF16)
        o1 = jnp.dot(h_hi, router_ref[0], preferred_element_type=F32)
        o2 = jnp.dot(h_lo, router_ref[1], preferred_element_type=F32)
        logits = o1 + pltpu.roll(o1, LANES - N_EXPERTS, 1) + o2
        gate_ref[...], i1, i2 = _top2_route(logits)

        @pl.when(pl.program_id(0) == 0)
        def _():
            taken_scr[...] = jnp.zeros_like(taken_scr)

        dest_ref[...], tile_cnt = _expert_slots(i1, i2, taken_scr[...])
        taken_scr[...] += tile_cnt
        cnt_ref[...] = jnp.broadcast_to(taken_scr[...], cnt_ref.shape)


def _mix_call(xs, y_ssm_t, y_att, mod, wglut, bglu, wout, gffn, *, router=None, ffn=None, n_tiles):
    row = lambda i: (i, 0)
    out_row = row
    with_router = router is not None
    assert with_router != (ffn is not None)
    in_specs = _x_specs(len(xs)) + [
        pl.BlockSpec((D_SSM, TM), lambda i: (0, i)),
        pl.BlockSpec((TM, D_ATT), row),
        pl.BlockSpec((1, 6, D_MODEL), lambda i: (_mod_row(i), 0, 0)),
        _const_spec((D_SSM, D_SSM)),
        _const_spec((D_SSM, 1)),
        _const_spec((D_MIX, D_MODEL)),
        _const_spec((1, D_MODEL)),
    ]
    args = [*xs, y_ssm_t, y_att, mod, wglut, bglu, wout, gffn]
    out_specs = [pl.BlockSpec((TM, D_MODEL), out_row)]
    out_shape = [jax.ShapeDtypeStruct((n_tiles * TM, D_MODEL), F32)]
    if with_router:
        in_specs.append(_const_spec((2, D_MODEL, LANES)))
        args.append(router)
        out_specs += [pl.BlockSpec((SC_SPLIT, TM, SC_ROW), lambda i: (0, i, 0)),
                      pl.BlockSpec((TM, LANES), out_row),
                      pl.BlockSpec((TM, LANES), out_row),
                      pl.BlockSpec((8, LANES), lambda i: (0, 0))]
        out_shape += [jax.ShapeDtypeStruct((SC_SPLIT, n_tiles * TM, SC_ROW), jnp.uint32),
                      jax.ShapeDtypeStruct((n_tiles * TM, LANES), F32),
                      jax.ShapeDtypeStruct((n_tiles * TM, LANES), jnp.int32),
                      jax.ShapeDtypeStruct((8, LANES), F32)]
        scratch = [pltpu.VMEM((1, LANES), F32)]
    else:
        in_specs += [_const_spec((D_MODEL, D_FF)), _const_spec((D_MODEL, D_FF)),
                     _const_spec((D_FF, D_MODEL))]
        args += list(ffn)
        scratch = []
    return pl.pallas_call(
        functools.partial(_mix_kernel, with_router=with_router, n_x=len(xs)),
        grid=(n_tiles,),
        in_specs=in_specs,
        out_specs=out_specs,
        out_shape=out_shape,
        scratch_shapes=scratch,
        compiler_params=_cparams("arbitrary" if with_router else "parallel"),
        name="mix_out",
    )(*args)


def _swiglu(h, w1_at, w3_at, w2_at, d_ff):
    acc = jnp.zeros((h.shape[0], D_MODEL), F32)
    for lo in range(0, d_ff, FF_CHUNK):
        sl = slice(lo, min(lo + FF_CHUNK, d_ff))
        a = jnp.dot(h, w1_at(sl), preferred_element_type=F32)
        b = jnp.dot(h, w3_at(sl), preferred_element_type=F32)
        g = (a * jax.nn.sigmoid(a) * b).astype(BF16)
        acc = acc + jnp.dot(g, w2_at(sl), preferred_element_type=F32)
    return acc


EXPERT_BLK = 512
EXPERT_CAP = N_LAT
N_SORTED = N_EXPERTS * EXPERT_CAP
N_EXPERT_BLKS = 2 * N_LAT // EXPERT_BLK + N_EXPERTS
PACKED = D_MODEL // 2
SC_ROW = 256
SC_SPLIT = PACKED // SC_ROW
SC_WIN = 128


def _sc_mesh():
    return plsc.VectorSubcoreMesh(core_axis_name="core", subcore_axis_name="subcore")


def _sc_scatter(x, idx_a, idx_b, n_out):
    n = x.shape[0]

    @pl.kernel(out_type=jax.ShapeDtypeStruct((n_out, SC_ROW), x.dtype), mesh=_sc_mesh(),
               scratch_types=[])
    def scatter(x_hbm, a_hbm, b_hbm, o_hbm):
        def body(x_vmem, a_vmem, b_vmem):
            pltpu.sync_copy(x_vmem, o_hbm.at[a_vmem.at[0]])
            pltpu.sync_copy(x_vmem, o_hbm.at[b_vmem.at[0]])

        pltpu.emit_pipeline(
            body, grid=(n // SC_WIN,),
            in_specs=[pl.BlockSpec((SC_WIN, SC_ROW), lambda i: (i, 0)),
                      pl.BlockSpec((1, SC_WIN), lambda i: (0, i)),
                      pl.BlockSpec((1, SC_WIN), lambda i: (0, i))],
            out_specs=[],
            core_axis_name=("core", "subcore"),
            dimension_semantics=(pltpu.PARALLEL,),
        )(x_hbm, a_hbm, b_hbm)

    return scatter(x, idx_a.reshape(1, n), idx_b.reshape(1, n))


def _sc_gather(y, idx):
    n = idx.shape[0]

    @pl.kernel(out_type=jax.ShapeDtypeStruct((n, SC_ROW), y.dtype), mesh=_sc_mesh(),
               scratch_types=[])
    def gather(y_hbm, i_hbm, o_hbm):
        def body(i_vmem, o_vmem):
            pltpu.sync_copy(y_hbm.at[i_vmem.at[0]], o_vmem)

        pltpu.emit_pipeline(
            body, grid=(n // SC_WIN,),
            in_specs=[pl.BlockSpec((1, SC_WIN), lambda i: (0, i))],
            out_specs=[pl.BlockSpec((SC_WIN, SC_ROW), lambda i: (i, 0))],
            core_axis_name=("core", "subcore"),
            dimension_semantics=(pltpu.PARALLEL,),
        )(i_hbm, o_hbm)

    return gather(y, idx.reshape(1, n))


W_CAST_ROWS = 128


def _experts_kernel(blk_expert_ref, blk_row_ref, n_used_ref, x_ref, w1_ref, w3_ref, w2_ref, o_ref,
                    w1_scr, w3_scr, w2_scr):
    del blk_row_ref
    b = pl.program_id(0)
    live = b < n_used_ref[0]
    new_expert = (b == 0) | (blk_expert_ref[b] != blk_expert_ref[jnp.maximum(b - 1, 0)])

    @pl.when(live & new_expert)
    def _():
        for src, dst in ((w1_ref, w1_scr), (w3_ref, w3_scr), (w2_ref, w2_scr)):
            for r in range(0, dst.shape[0], W_CAST_ROWS):
                dst[r:r + W_CAST_ROWS, :] = src[0, r:r + W_CAST_ROWS, :].astype(BF16)

    @pl.when(live)
    def _():
        hi, lo = _load_packed(x_ref)
        h = jnp.concatenate([hi.astype(BF16), lo.astype(BF16)], axis=-1)
        y = _swiglu(h, lambda sl: w1_scr[:, sl], lambda sl: w3_scr[:, sl],
                    lambda sl: w2_scr[sl, :], D_FF_EXPERT)
        _store_packed(o_ref, y)


def _experts_call(blk_expert, blk_row, n_used, xs, w1, w3, w2):
    row = lambda b, be, br, nu: (0, br[b], 0)
    wsel = lambda b, be, br, nu: (be[b], 0, 0)
    return pl.pallas_call(
        _experts_kernel,
        grid_spec=pltpu.PrefetchScalarGridSpec(
            num_scalar_prefetch=3,
            grid=(N_EXPERT_BLKS,),
            in_specs=[
                pl.BlockSpec((SC_SPLIT, EXPERT_BLK, SC_ROW), row),
                pl.BlockSpec((1, D_MODEL, D_FF_EXPERT), wsel),
                pl.BlockSpec((1, D_MODEL, D_FF_EXPERT), wsel),
                pl.BlockSpec((1, D_FF_EXPERT, D_MODEL), wsel),
            ],
            out_specs=pl.BlockSpec((SC_SPLIT, EXPERT_BLK, SC_ROW), row),
            scratch_shapes=[pltpu.VMEM((D_MODEL, D_FF_EXPERT), BF16),
                            pltpu.VMEM((D_MODEL, D_FF_EXPERT), BF16),
                            pltpu.VMEM((D_FF_EXPERT, D_MODEL), BF16)],
        ),
        out_shape=jax.ShapeDtypeStruct((SC_SPLIT, N_SORTED, SC_ROW), jnp.uint32),
        compiler_params=_cparams("arbitrary"),
        name="moe_experts",
    )(blk_expert, blk_row, n_used, xs, w1, w3, w2)


def _combine_kernel(x1_ref, r_ref, y_ref, mod_ref, fg_ref, o_ref):
    def expert_out(slot):
        hi, lo = _load_packed(y_ref.at[slot])
        return jnp.concatenate([hi, lo], axis=-1)
    w1 = r_ref[:, ROUTE_W1:ROUTE_W1 + 1]
    w2 = r_ref[:, ROUTE_W2:ROUTE_W2 + 1]
    y = w1 * expert_out(0) + w2 * expert_out(1)
    x2 = x1_ref[...] + mod_ref[0, 5:6, :] * y
    o_ref[...] = _rms(x2, fg_ref[...])


def _combine_call(x1, route, y_tok, mod, fg):
    n_tiles = N_LAT // TM
    row = lambda i: (i, 0)
    return pl.pallas_call(
        _combine_kernel,
        grid=(n_tiles,),
        in_specs=[
            pl.BlockSpec((TM, D_MODEL), row),
            pl.BlockSpec((TM, LANES), row),
            pl.BlockSpec((2, SC_SPLIT, TM, SC_ROW), lambda i: (0, 0, i, 0)),
            pl.BlockSpec((1, 6, D_MODEL), lambda i: (i // SEQ_TILES, 0, 0)),
            pl.BlockSpec((1, D_MODEL), lambda i: (0, 0)),
        ],
        out_specs=pl.BlockSpec((TM, D_MODEL), row),
        out_shape=jax.ShapeDtypeStruct((N_LAT, D_MODEL), F32),
        compiler_params=_cparams("parallel"),
        name="moe_combine",
    )(x1, route, y_tok, mod, fg)


def _moe_routed(h2p, x1, route, dest, cnt, mod, w1, w3, w2, fg):
    blks = (cnt[0, :N_EXPERTS].astype(jnp.int32) + (EXPERT_BLK - 1)) // EXPERT_BLK
    blk_end = jnp.cumsum(blks)
    n_used = blk_end[-1:]
    b = jnp.minimum(jnp.arange(N_EXPERT_BLKS, dtype=jnp.int32), n_used[0] - 1)
    blk_expert = jnp.sum((b[:, None] >= blk_end[None, :]).astype(jnp.int32), axis=1)
    blk_row = blk_expert * (EXPERT_CAP // EXPERT_BLK) + b - (blk_end - blks)[blk_expert]
    piece = jnp.arange(SC_SPLIT, dtype=jnp.int32)[:, None] * N_SORTED
    idx = [(piece + dest[:, slot][None, :]).reshape(SC_SPLIT * N_LAT) for slot in range(2)]
    xs = _sc_scatter(h2p.reshape(SC_SPLIT * N_LAT, SC_ROW), idx[0], idx[1], SC_SPLIT * N_SORTED)
    ys = _experts_call(blk_expert, blk_row, n_used, xs.reshape(SC_SPLIT, N_SORTED, SC_ROW),
                       w1, w3, w2)
    y_tok = _sc_gather(ys.reshape(SC_SPLIT * N_SORTED, SC_ROW), jnp.concatenate(idx))
    return _combine_call(x1, route, y_tok.reshape(2, SC_SPLIT, N_LAT, SC_ROW), mod, fg)


def _rope_partner_perm():
    perm, sign = [], []
    for j in range(QK_ROPE):
        first_half = (j % AXIS_ROPE) < ROPE_FREQS
        perm.append(j + ROPE_FREQS if first_half else j - ROPE_FREQS)
        sign.append(-1.0 if first_half else 1.0)
    return jnp.array(perm, jnp.int32), jnp.array(sign, F32)


def _rope_tables():
    t = jnp.arange(SEQ)
    row = (t // GRID_W).astype(F32)
    col = (t % GRID_W).astype(F32)
    inv_freq = ROPE_BASE ** (-2.0 * jnp.arange(ROPE_FREQS, dtype=F32) / AXIS_ROPE)
    ang = jnp.concatenate([row[:, None] * inv_freq, row[:, None] * inv_freq,
                           col[:, None] * inv_freq, col[:, None] * inv_freq], axis=1)
    cos = jnp.concatenate([jnp.cos(ang), jnp.ones((N_CTX, QK_ROPE), F32)], axis=0)
    sin = jnp.concatenate([jnp.sin(ang), jnp.zeros((N_CTX, QK_ROPE), F32)], axis=0)
    n = N_CTX + SEQ
    pad32 = jnp.zeros((n, HEAD_PAD - D_QK), F32)
    qs = ATT_SCALE * math.log2(math.e)
    cq = jnp.concatenate([jnp.full((n, QK_NOPE), qs, F32), qs * cos, pad32], axis=1)
    sq = jnp.concatenate([jnp.zeros((n, QK_NOPE), F32), qs * sin, pad32], axis=1)
    cs = jnp.concatenate([cos, sin, jnp.zeros((n, LANES - 2 * QK_ROPE), F32)], axis=1)
    return cq, sq, cs


def _layer_weights(w_in, w_uq, w_ukv):
    perm, sign = _rope_partner_perm()
    s0 = D_SSM + Q_LORA + KV_LORA
    kr_w = w_in[:, s0:s0 + QK_ROPE]
    wut = w_in[:, :D_SSM].T.astype(BF16)
    win = jnp.concatenate([w_in[:, D_SSM:s0], kr_w, kr_w[:, perm] * sign,
                           jnp.zeros((D_MODEL, LANES - 2 * QK_ROPE), F32)], axis=1).astype(BF16)
    uq = w_uq.reshape(Q_LORA, N_HEADS, D_QK)
    nope, rope = uq[..., :QK_NOPE], uq[..., QK_NOPE:]
    zpad = jnp.zeros((Q_LORA, N_HEADS, HEAD_PAD - D_QK), F32)
    wq1 = jnp.concatenate([nope, rope, zpad], axis=-1).reshape(Q_LORA, N_HEADS * HEAD_PAD).astype(BF16)
    wq2 = jnp.concatenate([jnp.zeros_like(nope), rope[..., perm] * sign, zpad], axis=-1)
    wq2 = wq2.reshape(Q_LORA, N_HEADS * HEAD_PAD).astype(BF16)
    ukv = w_ukv.reshape(KV_LORA, N_HEADS, QK_NOPE + V_HEAD)
    wk = jnp.concatenate([ukv[..., :QK_NOPE], jnp.zeros((KV_LORA, N_HEADS, HEAD_PAD - QK_NOPE), F32)],
                         axis=-1).reshape(KV_LORA, N_HEADS * HEAD_PAD)
    eye = jnp.eye(QK_ROPE, dtype=F32)
    place = jnp.concatenate([jnp.zeros((QK_ROPE, QK_NOPE), F32), eye,
                             jnp.zeros((QK_ROPE, HEAD_PAD - D_QK), F32)], axis=1)
    place = jnp.tile(place, (1, N_HEADS))
    spread = jnp.concatenate([place, place, jnp.zeros((LANES - 2 * QK_ROPE, N_HEADS * HEAD_PAD), F32)], 0)
    wkk = jnp.concatenate([wk, spread], axis=0).astype(BF16)
    wv = jnp.concatenate([ukv[..., QK_NOPE:], jnp.zeros((KV_LORA, N_HEADS, HEAD_PAD - V_HEAD), F32)],
                         axis=-1).reshape(KV_LORA, N_HEADS * HEAD_PAD).astype(BF16)
    return wut, win, wq1, wq2, wkk, wv


def kernel(x, c, ctx, c_ctx, w_ada, b_ada, norm_mix, norm_ffn, w_in, q_norm, kv_norm, w_uq, w_ukv,
           ssm_a_re, ssm_a_im, ssm_log_dt, ssm_b_re, ssm_b_im, ssm_c_re, ssm_c_im, ssm_d, w_glu,
           b_glu, w_out, ffn_w1, ffn_w3, ffn_w2, moe_router, moe_w1, moe_w3, moe_w2, final_norm):
    assert x.shape == (BATCH, SEQ, D_MODEL) and ctx.shape == (BATCH, CTX_LEN, D_MODEL)
    cond = jnp.concatenate([c, c_ctx[None, :], jnp.zeros((MOD_ROWS - BATCH - 1, D_MODEL), F32)], axis=0)
    mod_all = _ada_call(cond, w_ada, b_ada).reshape(DEPTH, MOD_ROWS, 6, D_MODEL)
    cq_t, sq_t, cs_t = _rope_tables()
    xs = (x.reshape(N_LAT, D_MODEL), ctx.reshape(N_CTX, D_MODEL))

    out = None
    for i in range(DEPTH):
        last = i == DEPTH - 1
        mod = mod_all[i]
        wut, win, wq1, wq2, wkk, wv = _layer_weights(w_in[i], w_uq[i], w_ukv[i])
        u_t, q, k, v = _inproj_call(xs, mod, norm_mix[i][None, :], wut, win, q_norm[i][None, :],
                                    kv_norm[i][None, :], wq1, wq2, wkk, wv, cq_t, sq_t, cs_t)
        tabs = _ssm_tables(ssm_a_re[i], ssm_a_im[i], ssm_log_dt[i], ssm_b_re[i], ssm_b_im[i],
                           ssm_c_re[i], ssm_c_im[i], ssm_d[i])
        y_ssm_t = _ssm_call(u_t.reshape(D_SSM, N_CHUNK, SSM_T), *tabs).reshape(D_SSM, N_TOT)
        y_att = _attn_latent_call(q, k, v)
        if last:
            n_tiles = LAT_TILES
        else:
            y_att = _attn_ctx_call(q, k, v, y_att)
            n_tiles = N_TOT // TM
        j = i // 2
        mix_args = (xs, y_ssm_t, y_att, mod, w_glu[i].T.astype(BF16), b_glu[i][:, None],
                    w_out[i].astype(BF16), norm_ffn[i][None, :])
        if i % 2 == 0:
            assert not last
            ffn = (ffn_w1[j].astype(BF16), ffn_w3[j].astype(BF16), ffn_w2[j].astype(BF16))
            xs = tuple(_mix_call(*mix_args, ffn=ffn, n_tiles=n_tiles))
        else:
            assert last
            r = moe_router[j]
            r_top = lax.bitcast_convert_type(
                lax.bitcast_convert_type(r, jnp.uint32) & jnp.uint32(0xFFFF0000), F32)
            r_hi = r_top.astype(BF16)
            r_lo = (r - r_top).astype(BF16)
            zr = jnp.zeros((D_MODEL, LANES - 2 * N_EXPERTS), BF16)
            router = jnp.stack([jnp.concatenate([r_hi, r_lo, zr], axis=1),
                                jnp.concatenate([r_hi, jnp.zeros_like(r_lo), zr], axis=1)])
            x1, h2p, route, dest, cnt = _mix_call(*mix_args, router=router, n_tiles=n_tiles)
            out = _moe_routed(h2p, x1, route, dest, cnt, mod, moe_w1[j], moe_w3[j], moe_w2[j],
                              final_norm[None, :])
    return out.reshape(BATCH, SEQ, D_MODEL)
```

```python
import functools
import math

import jax
import jax.numpy as jnp
import numpy as np
from jax import lax
from jax.experimental import pallas as pl
from jax.experimental.pallas import tpu as pltpu
from jax.experimental.pallas import tpu_sc as plsc

D_MODEL = 1024
BATCH = 4
SEQ = 8192
DEPTH = 2
GRID_W = 64
CTX_LEN = 256
D_SSM = 512
SSM_GROUP = 16
N_SSM_GROUPS = D_SSM // SSM_GROUP
SSM_STATE = 64
N_HEADS = 8
QK_NOPE = 64
QK_ROPE = 32
V_HEAD = 64
Q_LORA = 256
KV_LORA = 128
D_QK = QK_NOPE + QK_ROPE
D_ATT = N_HEADS * V_HEAD
D_MIX = D_SSM + D_ATT
D_IN = D_SSM + Q_LORA + KV_LORA + QK_ROPE
AXIS_ROPE = QK_ROPE // 2
ROPE_FREQS = AXIS_ROPE // 2
ROPE_BASE = 10000.0
ATT_SCALE = 1.0 / math.sqrt(D_QK)
D_FF = 2816
N_EXPERTS = 8
D_FF_EXPERT = 1408
EPS = 1e-6

N_CTX = BATCH * CTX_LEN
N_LAT = BATCH * SEQ
N_TOT = N_CTX + N_LAT

LANES = 128
HEAD_PAD = 128
TM = 512
LAT_TILES = N_LAT // TM
SEQ_TILES = SEQ // TM
TQ = 512
KEY_PIECES = (2048, 2048, 2048, 2048)
assert sum(KEY_PIECES) == SEQ
SSM_T = 128
N_CHUNK_LAT = N_LAT // SSM_T
N_CHUNK = N_TOT // SSM_T
SSM_K = SSM_GROUP * SSM_T
FF_CHUNK = 256
N_FF_CHUNKS = D_FF // FF_CHUNK
D_IN_REST = Q_LORA + KV_LORA + LANES
MOD_ROWS = 8
VMEM_LIMIT = 52 * 1024 * 1024

F32 = jnp.float32
BF16 = jnp.bfloat16
HI = lax.Precision.HIGHEST


def _cparams(*sem):
    return pltpu.CompilerParams(dimension_semantics=sem, vmem_limit_bytes=VMEM_LIMIT)


def _const_spec(shape):
    nd = len(shape)
    return pl.BlockSpec(shape, lambda *_: (0,) * nd, pipeline_mode=pl.Buffered(1))


def _mod_row(i):
    return jnp.where(i < LAT_TILES, i // SEQ_TILES, BATCH)


def _pos_tile(i):
    return jnp.where(i < LAT_TILES, i % SEQ_TILES, SEQ_TILES + i - LAT_TILES)


def _rms(x, g):
    ms = jnp.mean(x * x, axis=-1, keepdims=True)
    return x * lax.rsqrt(ms + EPS) * g


ADA_TN = 1536


def _ada_kernel(c_ref, w_ref, b_ref, o_ref):
    c = c_ref[...]
    s = c * jax.nn.sigmoid(c)
    o_ref[0] = jnp.dot(s, w_ref[0], precision=HI, preferred_element_type=F32) + b_ref[0]


def _ada_call(cond, w_ada, b_ada):
    n_col = 6 * D_MODEL // ADA_TN
    return pl.pallas_call(
        _ada_kernel,
        grid=(DEPTH, n_col),
        in_specs=[
            pl.BlockSpec((MOD_ROWS, D_MODEL), lambda l, j: (0, 0)),
            pl.BlockSpec((1, D_MODEL, ADA_TN), lambda l, j: (l, 0, j)),
            pl.BlockSpec((1, 1, ADA_TN), lambda l, j: (l, 0, j)),
        ],
        out_specs=pl.BlockSpec((1, MOD_ROWS, ADA_TN), lambda l, j: (l, 0, j)),
        out_shape=jax.ShapeDtypeStruct((DEPTH, MOD_ROWS, 6 * D_MODEL), F32),
        compiler_params=_cparams("arbitrary", "arbitrary"),
        name="ada_mod",
    )(cond, w_ada, b_ada.reshape(DEPTH, 1, 6 * D_MODEL))


def _tile_rows(refs, n_x):
    if n_x == 1:
        return refs[0][...]
    return jnp.where(pl.program_id(0) < LAT_TILES, refs[0][...], refs[1][...])


def _x_specs(n_x):
    if n_x == 1:
        return [pl.BlockSpec((TM, D_MODEL), lambda i: (i, 0))]
    return [pl.BlockSpec((TM, D_MODEL), lambda i: (jnp.minimum(i, LAT_TILES - 1), 0)),
            pl.BlockSpec((TM, D_MODEL), lambda i: (jnp.maximum(i - LAT_TILES, 0), 0))]


def _inproj_kernel(*refs, n_x):
    (mod_ref, g_ref, wut_ref, win_ref, qg_ref, kvg_ref, wq1_ref, wq2_ref, wkk_ref, wv_ref,
     cq_ref, sq_ref, cs_ref, ut_ref, q_ref, k_ref, v_ref) = refs[n_x:]
    x = _tile_rows(refs, n_x)
    sh = mod_ref[0, 0:1, :]
    sc = mod_ref[0, 1:2, :]
    xm = (_rms(x, g_ref[...]) * (1.0 + sc) + sh).astype(BF16)
    ut_ref[...] = lax.dot_general(wut_ref[...], xm, (((1,), (1,)), ((), ())),
                                  preferred_element_type=F32)
    z = jnp.dot(xm, win_ref[...], preferred_element_type=F32)
    qn = _rms(z[:, :Q_LORA], qg_ref[...]).astype(BF16)
    kvn = _rms(z[:, Q_LORA:Q_LORA + KV_LORA], kvg_ref[...]).astype(BF16)
    krr = (z[:, Q_LORA + KV_LORA:] * cs_ref[...]).astype(BF16)
    q1 = jnp.dot(qn, wq1_ref[...], preferred_element_type=F32)
    q2 = jnp.dot(qn, wq2_ref[...], preferred_element_type=F32)
    cq = cq_ref[...]
    sq = sq_ref[...]
    for h in range(N_HEADS):
        sl = slice(h * HEAD_PAD, (h + 1) * HEAD_PAD)
        q_ref[:, sl] = (q1[:, sl] * cq + q2[:, sl] * sq).astype(q_ref.dtype)
    kin = jnp.concatenate([kvn, krr], axis=-1)
    k_ref[...] = jnp.dot(kin, wkk_ref[...], preferred_element_type=F32).astype(k_ref.dtype)
    vv = jnp.dot(kvn, wv_ref[...], preferred_element_type=F32)
    lane = lax.broadcasted_iota(jnp.int32, vv.shape, 1)
    v_ref[...] = jnp.where(lane % HEAD_PAD == V_HEAD, 1.0, vv).astype(v_ref.dtype)


def _inproj_call(xs, mod, g_mix, wut, win, qg, kvg, wq1, wq2, wkk, wv, cq_t, sq_t, cs_t):
    n_tiles = N_TOT // TM
    row = lambda i: (i, 0)
    pos = lambda i: (_pos_tile(i), 0)
    return pl.pallas_call(
        functools.partial(_inproj_kernel, n_x=len(xs)),
        grid=(n_tiles,),
        in_specs=_x_specs(len(xs)) + [
            pl.BlockSpec((1, 6, D_MODEL), lambda i: (_mod_row(i), 0, 0)),
            _const_spec((1, D_MODEL)),
            _const_spec((D_SSM, D_MODEL)),
            _const_spec((D_MODEL, D_IN_REST)),
            _const_spec((1, Q_LORA)),
            _const_spec((1, KV_LORA)),
            _const_spec((Q_LORA, N_HEADS * HEAD_PAD)),
            _const_spec((Q_LORA, N_HEADS * HEAD_PAD)),
            _const_spec((2 * KV_LORA, N_HEADS * HEAD_PAD)),
            _const_spec((KV_LORA, N_HEADS * HEAD_PAD)),
            pl.BlockSpec((TM, LANES), pos),
            pl.BlockSpec((TM, LANES), pos),
            pl.BlockSpec((TM, LANES), pos),
        ],
        out_specs=[
            pl.BlockSpec((D_SSM, TM), lambda i: (0, i)),
            pl.BlockSpec((TM, N_HEADS * HEAD_PAD), row),
            pl.BlockSpec((TM, N_HEADS * HEAD_PAD), row),
            pl.BlockSpec((TM, N_HEADS * HEAD_PAD), row),
        ],
        out_shape=[
            jax.ShapeDtypeStruct((D_SSM, N_TOT), F32),
            jax.ShapeDtypeStruct((N_TOT, N_HEADS * HEAD_PAD), BF16),
            jax.ShapeDtypeStruct((N_TOT, N_HEADS * HEAD_PAD), BF16),
            jax.ShapeDtypeStruct((N_TOT, N_HEADS * HEAD_PAD), BF16),
        ],
        compiler_params=_cparams("parallel"),
        name="in_proj",
    )(*xs, mod, g_mix, wut, win, qg, kvg, wq1, wq2, wkk, wv, cq_t, sq_t, cs_t)


def _attn_kernel(*refs, latent_pieces, tq):
    if latent_pieces:
        q_ref, k_ref, v_ref, kc_ref, vc_ref, o_ref, s_scr = refs
    else:
        q_ref, kc_ref, vc_ref, _, o_ref, s_scr = refs
    heads =[slice(hh * HEAD_PAD, (hh + 1) * HEAD_PAD) for hh in range(2)]
    qs = [q_ref[:, sl] for sl in heads]

    def put_scores(slot, k_at, width):
        for hh in range(2):
            s_scr[slot, hh, :, :width] = lax.dot_general(
                qs[hh], k_at(heads[hh]), (((1,), (1,)), ((), ())), preferred_element_type=F32)

    def consume(carry, slot, v_at, width):
        new = []
        for hh in range(2):
            m, acc = carry[hh]
            s = s_scr[slot, hh, :, :width]
            m_new = jnp.maximum(m, jnp.max(s, axis=-1, keepdims=True))
            alpha = jnp.exp2(m - m_new)
            p = jnp.exp2(s - m_new).astype(BF16)
            acc = alpha * acc + jnp.dot(p, v_at(heads[hh]), preferred_element_type=F32)
            new.append((m_new, acc))
        return tuple(new)

    def piece(kr, vr, start, size):
        return (lambda sl: kr[start:start + size, sl]), (lambda sl: vr[start:start + size, sl]), size

    pieces = [piece(kc_ref, vc_ref, 0, CTX_LEN)]
    start = 0
    for size in latent_pieces:
        pieces.append(piece(k_ref, v_ref, start, size))
        start += size
    carry = tuple((jnp.full((tq, 1), -jnp.inf, F32), jnp.zeros((tq, HEAD_PAD), F32))
                  for _ in range(2))
    put_scores(0, pieces[0][0], pieces[0][2])
    for i, (_, v_at, size) in enumerate(pieces):
        if i + 1 < len(pieces):
            put_scores((i + 1) % 2, pieces[i + 1][0], pieces[i + 1][2])
        carry = consume(carry, i % 2, v_at, size)
    outs = [acc[:, :V_HEAD] / acc[:, V_HEAD:V_HEAD + 1] for _, acc in carry]
    o_ref[...] = jnp.concatenate(outs, axis=-1).astype(o_ref.dtype)


def _attn_latent_call(q, k, v):
    qt = SEQ // TQ
    ctx0 = N_LAT // CTX_LEN
    return pl.pallas_call(
        functools.partial(_attn_kernel, latent_pieces=KEY_PIECES, tq=TQ),
        grid=(BATCH, N_HEADS // 2, qt),
        in_specs=[
            pl.BlockSpec((TQ, 2 * HEAD_PAD), lambda b, h, i: (b * qt + i, h)),
            pl.BlockSpec((SEQ, 2 * HEAD_PAD), lambda b, h, i: (b, h)),
            pl.BlockSpec((SEQ, 2 * HEAD_PAD), lambda b, h, i: (b, h)),
            pl.BlockSpec((CTX_LEN, 2 * HEAD_PAD), lambda b, h, i: (ctx0 + b, h)),
            pl.BlockSpec((CTX_LEN, 2 * HEAD_PAD), lambda b, h, i: (ctx0 + b, h)),
        ],
        out_specs=pl.BlockSpec((TQ, 2 * V_HEAD), lambda b, h, i: (b * qt + i, h)),
        out_shape=jax.ShapeDtypeStruct((N_TOT, D_ATT), BF16),
        scratch_shapes=[pltpu.VMEM((2, 2, TQ, max(KEY_PIECES)), F32)],
        compiler_params=_cparams("parallel", "parallel", "arbitrary"),
        name="attn_latent",
    )(q, k, v, k, v)


def _attn_ctx_call(q, k, v, y_att):
    ctx0 = N_LAT // CTX_LEN
    return pl.pallas_call(
        functools.partial(_attn_kernel, latent_pieces=(), tq=CTX_LEN),
        grid=(BATCH, N_HEADS // 2),
        in_specs=[
            pl.BlockSpec((CTX_LEN, 2 * HEAD_PAD), lambda b, h: (ctx0 + b, h)),
            pl.BlockSpec((CTX_LEN, 2 * HEAD_PAD), lambda b, h: (ctx0 + b, h)),
            pl.BlockSpec((CTX_LEN, 2 * HEAD_PAD), lambda b, h: (ctx0 + b, h)),
            pl.BlockSpec(memory_space=pl.ANY),
        ],
        out_specs=pl.BlockSpec((CTX_LEN, 2 * V_HEAD), lambda b, h: (ctx0 + b, h)),
        out_shape=jax.ShapeDtypeStruct((N_TOT, D_ATT), BF16),
        input_output_aliases={3: 0},
        scratch_shapes=[pltpu.VMEM((1, 2, CTX_LEN, CTX_LEN), F32)],
        compiler_params=_cparams("parallel", "parallel"),
        name="attn_ctx",
    )(q, k, v, y_att)


SSM_C = 64
SSM_HALVES = SSM_T // SSM_C
SSM_KC = SSM_GROUP * SSM_C
LAG_ROWS = 2 * SSM_C
PT_LAG, PT_INC, PT_OUT = 0, LAG_ROWS, LAG_ROWS + SSM_C
PT_ROWS = LAG_ROWS + 2 * SSM_C
NT_DIMS = (((1,), (1,)), ((), ()))


def _ssm_kernel(u_ref, pta_ref, ptb_ref, rows_ref, cc_ref, at_ref, dv_ref, y_ref,
                abt_hi_scr, abt_lo_scr, wl_scr, m_scr, ws_scr, wct_scr, s_scr, h_scr):
    H, C = SSM_GROUP, SSM_C
    half_lane = lax.broadcasted_iota(jnp.int32, (1, LANES), 1) < C

    def split(v):
        hi = v.astype(BF16)
        return hi, (v - hi.astype(F32)).astype(BF16)

    def scaled(row0, n_rows, ra, rb):
        return pta_ref[0, row0:row0 + n_rows, :] * ra + ptb_ref[0, row0:row0 + n_rows, :] * rb

    def build_tables(i, _):
        row0 = pl.multiple_of(i * C, C)
        b_re = rows_ref[0, pl.ds(i, 1), :]
        b_im = rows_ref[0, pl.ds(H + i, 1), :]
        lag_rows = pl.ds(pl.multiple_of(i * LAG_ROWS, LAG_ROWS), LAG_ROWS)
        abt_hi_scr[lag_rows, :], abt_lo_scr[lag_rows, :] = split(
            scaled(PT_LAG, LAG_ROWS, b_re, b_im))
        ws_scr[pl.ds(row0, C), :] = scaled(PT_INC, C, b_re, b_im).astype(BF16)
        c_re = rows_ref[0, pl.ds(2 * H + i, 1), :]
        c_im = rows_ref[0, pl.ds(3 * H + i, 1), :]
        wct_scr[pl.ds(row0, C), :] = scaled(PT_OUT, C, c_re, c_im).astype(BF16)
        return 0

    lax.fori_loop(0, H, build_tables, 0)
    cc_hi, cc_lo = split(cc_ref[0])
    nt = lambda a, b: lax.dot_general(a, b, NT_DIMS, preferred_element_type=F32)
    wl_scr[...] = (nt(cc_hi, abt_hi_scr[...]) + nt(cc_hi, abt_lo_scr[...])
                   + nt(cc_lo, abt_hi_scr[...]))

    slot2 = lax.broadcasted_iota(jnp.int32, (1, 2 * LANES), 1)
    first_low = (slot2 < C) | (slot2 > 2 * LANES - C)
    first_high = slot2 < LANES

    def toeplitz_pair(x):
        return pltpu.roll(jnp.broadcast_to(x, (C, 2 * LANES)), 0, 1, stride=1, stride_axis=0)

    def build_toeplitz(ci, _):
        row0 = pl.multiple_of(ci * C, C)
        slots = pl.ds(pl.multiple_of(ci * LAG_ROWS, LAG_ROWS), LAG_ROWS)

        def lags(co, shift):
            v = wl_scr[co:co + 1, slots]
            if shift:
                v = pltpu.roll(v, C, 1)
            return jnp.concatenate([v, v], axis=1)

        for ka in range(0, H // 2, 2):
            kb = ka + 1
            low = toeplitz_pair(jnp.where(first_low, lags(2 * ka, False), lags(2 * kb, False)))
            high = toeplitz_pair(jnp.where(first_high, lags(2 * ka + 1, True),
                                           lags(2 * kb + 1, True)))
            for k, lanes in ((ka, slice(0, LANES)), (kb, slice(LANES, 2 * LANES))):
                m_scr[pl.ds(row0, C), k * LANES:(k + 1) * LANES] = jnp.where(
                    half_lane, low[:, lanes], high[:, lanes]).astype(BF16)
        return 0

    lax.fori_loop(0, H, build_toeplitz, 0)

    def chunk_operand(hf):
        cols = []
        for k in range(H // 2):
            a, b = u_ref[2 * k], u_ref[2 * k + 1]
            if hf == 0:
                cols.append(jnp.where(half_lane, a, pltpu.roll(b, C, 1)))
            else:
                cols.append(jnp.where(half_lane, pltpu.roll(a, C, 1), b))
        return jnp.concatenate(cols, axis=1).astype(BF16)

    ys = []
    for hf in range(SSM_HALVES):
        u = chunk_operand(hf)
        ys.append(jnp.dot(u, m_scr[...], preferred_element_type=F32))
        s = jnp.dot(u, ws_scr[...], preferred_element_type=F32)
        for d in range(2):
            s_d = s[:, d * LANES:(d + 1) * LANES]
            s_scr[hf, d] = s_d
            s_scr[hf, 2 + d] = pltpu.roll(s_d, SSM_STATE, 1)

    n_lat, n_ctx = SEQ // SSM_T, CTX_LEN // SSM_T
    ctx = [((N_CHUNK_LAT + c, n_ctx), hf) for c in range(n_ctx) for hf in range(SSM_HALVES)]
    lat = [((k, n_lat), hf) for k in range(n_lat) for hf in range(SSM_HALVES)]
    coef = [jnp.broadcast_to(at_ref[0, r:r + 1, :], (BATCH, 2 * SSM_STATE)) for r in range(4)]

    def advance(h, h_sw, d, rows, hf):
        a0, a1 = coef[2 * d], coef[2 * d + 1]
        return (h * a0 + h_sw * a1 + s_scr[hf, d, rows, :],
                h_sw * a0 - h * a1 + s_scr[hf, 2 + d, rows, :])

    zero = jnp.zeros((BATCH, 2 * SSM_STATE), F32)
    h_f, h_f_sw, h_r, h_r_sw = zero, zero, zero, zero
    for ((sf, stf), cf), ((sr, strd), cr) in zip(ctx + lat, ctx[::-1] + lat[::-1]):
        rows_f = pl.ds(sf, BATCH, stride=stf)
        rows_r = pl.ds(sr, BATCH, stride=strd)
        h_scr[cf, 0, rows_f, :] = h_f
        h_f, h_f_sw = advance(h_f, h_f_sw, 0, rows_f, cf)
        h_scr[cr, 1, rows_r, :] = h_r
        h_r, h_r_sw = advance(h_r, h_r_sw, 1, rows_r, cr)

    for hf in range(SSM_HALVES):
        h_in = jnp.concatenate([h_scr[hf, 0], h_scr[hf, 1]], axis=-1).astype(BF16)
        ys[hf] = ys[hf] + lax.dot_general(h_in, wct_scr[...], NT_DIMS,
                                          preferred_element_type=F32)
    for k in range(H // 2):
        y0, y1 = ys[0][:, k * LANES:(k + 1) * LANES], ys[1][:, k * LANES:(k + 1) * LANES]
        for c, yc in ((2 * k, jnp.where(half_lane, y0, pltpu.roll(y1, C, 1))),
                      (2 * k + 1, jnp.where(half_lane, pltpu.roll(y0, C, 1), y1))):
            y_ref[c] = yc + u_ref[c] * dv_ref[0, :, c * SSM_T:(c + 1) * SSM_T]


def _ssm_call(u_t, pta, ptb, rows, cc, at, dv):
    g3 = lambda g: (g, 0, 0)
    return pl.pallas_call(
        _ssm_kernel,
        grid=(N_SSM_GROUPS,),
        in_specs=[
            pl.BlockSpec((SSM_GROUP, N_CHUNK, SSM_T), g3),
            pl.BlockSpec((1, PT_ROWS, 4 * SSM_STATE), g3),
            pl.BlockSpec((1, PT_ROWS, 4 * SSM_STATE), g3),
            pl.BlockSpec((1, 4 * SSM_GROUP, 4 * SSM_STATE), g3),
            pl.BlockSpec((1, SSM_GROUP, 4 * SSM_STATE), g3),
            pl.BlockSpec((1, 4, 2 * SSM_STATE), g3),
            pl.BlockSpec((1, 1, SSM_K), g3),
        ],
        out_specs=pl.BlockSpec((SSM_GROUP, N_CHUNK, SSM_T), g3),
        out_shape=jax.ShapeDtypeStruct((D_SSM, N_CHUNK, SSM_T), F32),
        scratch_shapes=[
            pltpu.VMEM((SSM_GROUP * LAG_ROWS, 4 * SSM_STATE), BF16),
            pltpu.VMEM((SSM_GROUP * LAG_ROWS, 4 * SSM_STATE), BF16),
            pltpu.VMEM((SSM_GROUP, SSM_GROUP * LAG_ROWS), F32),
            pltpu.VMEM((SSM_KC, SSM_KC), BF16),
            pltpu.VMEM((SSM_KC, 4 * SSM_STATE), BF16),
            pltpu.VMEM((SSM_KC, 4 * SSM_STATE), BF16),
            pltpu.VMEM((SSM_HALVES, 4, N_CHUNK, 2 * SSM_STATE), F32),
            pltpu.VMEM((SSM_HALVES, 2, N_CHUNK, 2 * SSM_STATE), F32),
        ],
        compiler_params=_cparams("parallel"),
        name="s5_mixer",
    )(u_t, pta, ptb, rows, cc, at, dv)


def _ssm_tables(a_re, a_im, log_dt, b_re, b_im, c_re, c_im, d_skip):
    G, P, H, T = N_SSM_GROUPS, SSM_STATE, SSM_GROUP, SSM_T
    a_re, a_im = a_re.astype(F32), a_im.astype(F32)
    dt = jnp.exp(log_dt.astype(F32))[..., None]
    den = a_re * a_re + a_im * a_im
    mag1 = jnp.exp(dt * a_re)
    ab_re, ab_im = mag1 * jnp.cos(dt * a_im), mag1 * jnp.sin(dt * a_im)
    num_re = ab_re - 1.0
    f_re = (num_re * a_re + ab_im * a_im) / den
    f_im = (ab_im * a_re - num_re * a_im) / den
    b_re, b_im = b_re.astype(F32), b_im.astype(F32)
    bb_re = f_re[..., None] * b_re - f_im[..., None] * b_im
    bb_im = f_re[..., None] * b_im + f_im[..., None] * b_re
    c_re, c_im = c_re.astype(F32), c_im.astype(F32)
    la, th = dt * a_re, dt * a_im

    def powers(d, n):
        mag = jnp.exp(la[d][:, None, :] * n[None, :, None])
        ph = th[d][:, None, :] * n[None, :, None]
        return mag * jnp.cos(ph), mag * jnp.sin(ph)

    C = SSM_C
    slot = np.arange(LAG_ROWS)
    lag_f = np.where(slot < C, slot, -1)
    lag_r = np.where((-slot) % LAG_ROWS < C, (-slot) % LAG_ROWS, -1)
    s_idx = np.arange(C)
    f_all = np.concatenate([lag_f, C - 1 - s_idx, s_idx + 1])
    r_all = np.concatenate([lag_r, s_idx, C - s_idx])
    f_exp, r_exp = jnp.asarray(np.maximum(f_all, 0), F32), jnp.asarray(np.maximum(r_all, 0), F32)
    f_on, r_on = jnp.asarray(f_all >= 0, F32), jnp.asarray(r_all >= 0, F32)
    lane = jnp.arange(4 * P)
    is_fwd = (lane < 2 * P)[None, :]
    is_re = ((lane // P) % 2 == 0)[None, None, :]
    expo = jnp.where(is_fwd, f_exp[:, None], r_exp[:, None])[None]
    on = jnp.where(is_fwd, f_on[:, None], r_on[:, None])[None]
    la4 = jnp.concatenate([la[0], la[0], la[1], la[1]], axis=-1)[:, None, :]
    th4 = jnp.concatenate([th[0], th[0], th[1], th[1]], axis=-1)[:, None, :]
    mag = jnp.exp(la4 * expo) * on
    p_re, p_im = mag * jnp.cos(th4 * expo), mag * jnp.sin(th4 * expo)
    pta = jnp.where(is_re, p_re, p_im)
    ptb = jnp.where(is_re, -p_im, p_re)

    def per_channel(v):
        f, r = v[0].transpose(0, 2, 1), v[1].transpose(0, 2, 1)
        return jnp.concatenate([f, f, r, r], axis=-1)
    cf_re, cr_re = c_re[0], c_re[1]
    cf_im, cr_im = c_im[0], c_im[1]
    rows = jnp.concatenate([
        per_channel(bb_re), per_channel(bb_im),
        jnp.concatenate([cf_re, -cf_re, cr_re, -cr_re], axis=-1),
        jnp.concatenate([cf_im, -cf_im, cr_im, -cr_im], axis=-1)], axis=1)
    cc = jnp.concatenate([cf_re, -cf_im, cr_re, -cr_im], axis=-1)

    t_exp = jnp.full((1,), float(SSM_C), F32)
    (f_re_t, f_im_t), (r_re_t, r_im_t) = powers(0, t_exp), powers(1, t_exp)
    f_re_t, f_im_t, r_re_t, r_im_t = (v[:, 0, :] for v in (f_re_t, f_im_t, r_re_t, r_im_t))
    at = jnp.stack([jnp.concatenate([f_re_t, f_re_t], -1), jnp.concatenate([-f_im_t, f_im_t], -1),
                    jnp.concatenate([r_re_t, r_re_t], -1), jnp.concatenate([-r_im_t, r_im_t], -1)],
                   axis=1)
    dv = jnp.repeat(d_skip.astype(F32).reshape(G, H), T, axis=-1).reshape(G, 1, H * T)
    return pta, ptb, rows, cc, at, dv


ROUTE_E1, ROUTE_E2, ROUTE_W1, ROUTE_W2 = 0, 1, 2, 3


def _top2_route(logits):
    lane = lax.broadcasted_iota(jnp.int32, logits.shape, 1)
    lg = jnp.where(lane < N_EXPERTS, logits, -jnp.inf)
    m1 = jnp.max(lg, axis=-1, keepdims=True)
    i1 = jnp.min(jnp.where(lg == m1, lane, LANES), axis=-1, keepdims=True)
    lg2 = jnp.where(lane == i1, -jnp.inf, lg)
    m2 = jnp.max(lg2, axis=-1, keepdims=True)
    i2 = jnp.min(jnp.where(lg2 == m2, lane, LANES), axis=-1, keepdims=True)
    e2 = jnp.exp(m2 - m1)
    w1 = 1.0 / (1.0 + e2)
    rec = jnp.where(lane == ROUTE_E1, i1.astype(F32), 0.0)
    rec = jnp.where(lane == ROUTE_E2, i2.astype(F32), rec)
    rec = jnp.where(lane == ROUTE_W1, w1, rec)
    return jnp.where(lane == ROUTE_W2, e2 * w1, rec), i1, i2


def _expert_slots(i1, i2, taken):
    n = i1.shape[0]
    lane = lax.broadcasted_iota(jnp.int32, (n, LANES), 1)
    picked = jnp.where((lane == i1) | (lane == i2), 1.0, 0.0)
    earlier = (lax.broadcasted_iota(jnp.int32, (n, n), 1)
               < lax.broadcasted_iota(jnp.int32, (n, n), 0)).astype(BF16)
    rank = jnp.dot(earlier, picked.astype(BF16), preferred_element_type=F32)
    slot = rank + taken + (lane * EXPERT_CAP).astype(F32)
    d1 = jnp.sum(jnp.where(lane == i1, slot, 0.0), axis=-1, keepdims=True)
    d2 = jnp.sum(jnp.where(lane == i2, slot, 0.0), axis=-1, keepdims=True)
    dest = jnp.where(lane == 0, d1, jnp.where(lane == 1, d2, 0.0)).astype(jnp.int32)
    return dest, jnp.sum(picked, axis=0, keepdims=True)


def _pack_bf16_pairs(v):
    k = v.shape[1] // 2
    bits = pltpu.bitcast(v.astype(BF16).astype(F32), jnp.uint32)
    return (bits[:, :k] & jnp.uint32(0xFFFF0000)) | (bits[:, k:] >> 16)


def _unpack_bf16_pairs(w):
    hi = pltpu.bitcast(w & jnp.uint32(0xFFFF0000), F32)
    lo = pltpu.bitcast(w << 16, F32)
    return hi, lo


def _store_packed(ref, v):
    words = _pack_bf16_pairs(v)
    for s in range(SC_SPLIT):
        ref[s] = words[:, s * SC_ROW:(s + 1) * SC_ROW]


def _load_packed(ref):
    return _unpack_bf16_pairs(jnp.concatenate([ref[s] for s in range(SC_SPLIT)], axis=-1))


def _mix_kernel(*refs, with_router, n_x):
    rest = refs[n_x:]
    if with_router:
        (yst_ref, ya_ref, mod_ref, wglut_ref, bglu_ref, wout_ref, gffn_ref, router_ref,
         x1_ref, h2_ref, gate_ref, dest_ref, cnt_ref, taken_scr) = rest
    else:
        (yst_ref, ya_ref, mod_ref, wglut_ref, bglu_ref, wout_ref, gffn_ref,
         w1_ref, w3_ref, w2_ref, x2_ref) = rest
    zt = jax.nn.gelu(yst_ref[...], approximate=True)
    glt = zt * jax.nn.sigmoid(
        jnp.dot(wglut_ref[...], zt.astype(BF16), preferred_element_type=F32) + bglu_ref[...])
    mix = jnp.concatenate([glt.T.astype(BF16), ya_ref[...]], axis=-1)
    o = jnp.dot(mix, wout_ref[...], preferred_element_type=F32)
    x1 = _tile_rows(refs, n_x) + mod_ref[0, 2:3, :] * o
    h2 = _rms(x1, gffn_ref[...]) * (1.0 + mod_ref[0, 4:5, :]) + mod_ref[0, 3:4, :]
    if not with_router:
        acc = _swiglu(h2.astype(BF16), lambda sl: w1_ref[:, sl], lambda sl: w3_ref[:, sl],
                      lambda sl: w2_ref[sl, :], D_FF)
        x2_ref[...] = x1 + mod_ref[0, 5:6, :] * acc
    else:
        x1_ref[...] = x1
        _store_packed(h2_ref, h2)
        h_hi = h2.astype(BF16)
        h_lo = (h2 - h_hi.astype(F32)).astype(BF16)
        o1 = jnp.dot(h_hi, router_ref[0], preferred_element_type=F32)
        o2 = jnp.dot(h_lo, router_ref[1], preferred_element_type=F32)
        logits = o1 + pltpu.roll(o1, LANES - N_EXPERTS, 1) + o2
        gate_ref[...], i1, i2 = _top2_route(logits)

        @pl.when(pl.program_id(0) == 0)
        def _():
            taken_scr[...] = jnp.zeros_like(taken_scr)

        dest_ref[...], tile_cnt = _expert_slots(i1, i2, taken_scr[...])
        taken_scr[...] += tile_cnt
        cnt_ref[...] = jnp.broadcast_to(taken_scr[...], cnt_ref.shape)


def _mix_call(xs, y_ssm_t, y_att, mod, wglut, bglu, wout, gffn, *, router=None, ffn=None, n_tiles):
    row = lambda i: (i, 0)
    out_row = row
    with_router = router is not None
    assert with_router != (ffn is not None)
    in_specs = _x_specs(len(xs)) + [
        pl.BlockSpec((D_SSM, TM), lambda i: (0, i)),
        pl.BlockSpec((TM, D_ATT), row),
        pl.BlockSpec((1, 6, D_MODEL), lambda i: (_mod_row(i), 0, 0)),
        _const_spec((D_SSM, D_SSM)),
        _const_spec((D_SSM, 1)),
        _const_spec((D_MIX, D_MODEL)),
        _const_spec((1, D_MODEL)),
    ]
    args = [*xs, y_ssm_t, y_att, mod, wglut, bglu, wout, gffn]
    out_specs = [pl.BlockSpec((TM, D_MODEL), out_row)]
    out_shape = [jax.ShapeDtypeStruct((n_tiles * TM, D_MODEL), F32)]
    if with_router:
        in_specs.append(_const_spec((2, D_MODEL, LANES)))
        args.append(router)
        out_specs += [pl.BlockSpec((SC_SPLIT, TM, SC_ROW), lambda i: (0, i, 0)),
                      pl.BlockSpec((TM, LANES), out_row),
                      pl.BlockSpec((TM, LANES), out_row),
                      pl.BlockSpec((8, LANES), lambda i: (0, 0))]
        out_shape += [jax.ShapeDtypeStruct((SC_SPLIT, n_tiles * TM, SC_ROW), jnp.uint32),
                      jax.ShapeDtypeStruct((n_tiles * TM, LANES), F32),
                      jax.ShapeDtypeStruct((n_tiles * TM, LANES), jnp.int32),
                      jax.ShapeDtypeStruct((8, LANES), F32)]
        scratch = [pltpu.VMEM((1, LANES), F32)]
    else:
        in_specs += [_const_spec((D_MODEL, D_FF)), _const_spec((D_MODEL, D_FF)),
                     _const_spec((D_FF, D_MODEL))]
        args += list(ffn)
        scratch = []
    return pl.pallas_call(
        functools.partial(_mix_kernel, with_router=with_router, n_x=len(xs)),
        grid=(n_tiles,),
        in_specs=in_specs,
        out_specs=out_specs,
        out_shape=out_shape,
        scratch_shapes=scratch,
        compiler_params=_cparams("arbitrary" if with_router else "parallel"),
        name="mix_out",
    )(*args)


def _swiglu(h, w1_at, w3_at, w2_at, d_ff):
    acc = jnp.zeros((h.shape[0], D_MODEL), F32)
    for lo in range(0, d_ff, FF_CHUNK):
        sl = slice(lo, min(lo + FF_CHUNK, d_ff))
        a = jnp.dot(h, w1_at(sl), preferred_element_type=F32)
        b = jnp.dot(h, w3_at(sl), preferred_element_type=F32)
        g = (a * jax.nn.sigmoid(a) * b).astype(BF16)
        acc = acc + jnp.dot(g, w2_at(sl), preferred_element_type=F32)
    return acc


EXPERT_BLK = 512
EXPERT_CAP = N_LAT
N_SORTED = N_EXPERTS * EXPERT_CAP
N_EXPERT_BLKS = 2 * N_LAT // EXPERT_BLK + N_EXPERTS
PACKED = D_MODEL // 2
SC_ROW = 256
SC_SPLIT = PACKED // SC_ROW
SC_WIN = 128


def _sc_mesh():
    return plsc.VectorSubcoreMesh(core_axis_name="core", subcore_axis_name="subcore")


def _sc_scatter(x, idx_a, idx_b, n_out):
    n = x.shape[0]

    @pl.kernel(out_type=jax.ShapeDtypeStruct((n_out, SC_ROW), x.dtype), mesh=_sc_mesh(),
               scratch_types=[])
    def scatter(x_hbm, a_hbm, b_hbm, o_hbm):
        def body(x_vmem, a_vmem, b_vmem):
            pltpu.sync_copy(x_vmem, o_hbm.at[a_vmem.at[0]])
            pltpu.sync_copy(x_vmem, o_hbm.at[b_vmem.at[0]])

        pltpu.emit_pipeline(
            body, grid=(n // SC_WIN,),
            in_specs=[pl.BlockSpec((SC_WIN, SC_ROW), lambda i: (i, 0)),
                      pl.BlockSpec((1, SC_WIN), lambda i: (0, i)),
                      pl.BlockSpec((1, SC_WIN), lambda i: (0, i))],
            out_specs=[],
            core_axis_name=("core", "subcore"),
            dimension_semantics=(pltpu.PARALLEL,),
        )(x_hbm, a_hbm, b_hbm)

    return scatter(x, idx_a.reshape(1, n), idx_b.reshape(1, n))


def _sc_gather(y, idx):
    n = idx.shape[0]

    @pl.kernel(out_type=jax.ShapeDtypeStruct((n, SC_ROW), y.dtype), mesh=_sc_mesh(),
               scratch_types=[])
    def gather(y_hbm, i_hbm, o_hbm):
        def body(i_vmem, o_vmem):
            pltpu.sync_copy(y_hbm.at[i_vmem.at[0]], o_vmem)

        pltpu.emit_pipeline(
            body, grid=(n // SC_WIN,),
            in_specs=[pl.BlockSpec((1, SC_WIN), lambda i: (0, i))],
            out_specs=[pl.BlockSpec((SC_WIN, SC_ROW), lambda i: (i, 0))],
            core_axis_name=("core", "subcore"),
            dimension_semantics=(pltpu.PARALLEL,),
        )(i_hbm, o_hbm)

    return gather(y, idx.reshape(1, n))


W_CAST_ROWS = 128


def _experts_kernel(blk_expert_ref, blk_row_ref, n_used_ref, x_ref, w1_ref, w3_ref, w2_ref, o_ref,
                    w1_scr, w3_scr, w2_scr):
    del blk_row_ref
    b = pl.program_id(0)
    live = b < n_used_ref[0]
    new_expert = (b == 0) | (blk_expert_ref[b] != blk_expert_ref[jnp.maximum(b - 1, 0)])

    @pl.when(live & new_expert)
    def _():
        for src, dst in ((w1_ref, w1_scr), (w3_ref, w3_scr), (w2_ref, w2_scr)):
            for r in range(0, dst.shape[0], W_CAST_ROWS):
                dst[r:r + W_CAST_ROWS, :] = src[0, r:r + W_CAST_ROWS, :].astype(BF16)

    @pl.when(live)
    def _():
        hi, lo = _load_packed(x_ref)
        h = jnp.concatenate([hi.astype(BF16), lo.astype(BF16)], axis=-1)
        y = _swiglu(h, lambda sl: w1_scr[:, sl], lambda sl: w3_scr[:, sl],
                    lambda sl: w2_scr[sl, :], D_FF_EXPERT)
        _store_packed(o_ref, y)


def _experts_call(blk_expert, blk_row, n_used, xs, w1, w3, w2):
    row = lambda b, be, br, nu: (0, br[b], 0)
    wsel = lambda b, be, br, nu: (be[b], 0, 0)
    return pl.pallas_call(
        _experts_kernel,
        grid_spec=pltpu.PrefetchScalarGridSpec(
            num_scalar_prefetch=3,
            grid=(N_EXPERT_BLKS,),
            in_specs=[
                pl.BlockSpec((SC_SPLIT, EXPERT_BLK, SC_ROW), row),
                pl.BlockSpec((1, D_MODEL, D_FF_EXPERT), wsel),
                pl.BlockSpec((1, D_MODEL, D_FF_EXPERT), wsel),
                pl.BlockSpec((1, D_FF_EXPERT, D_MODEL), wsel),
            ],
            out_specs=pl.BlockSpec((SC_SPLIT, EXPERT_BLK, SC_ROW), row),
            scratch_shapes=[pltpu.VMEM((D_MODEL, D_FF_EXPERT), BF16),
                            pltpu.VMEM((D_MODEL, D_FF_EXPERT), BF16),
                            pltpu.VMEM((D_FF_EXPERT, D_MODEL), BF16)],
        ),
        out_shape=jax.ShapeDtypeStruct((SC_SPLIT, N_SORTED, SC_ROW), jnp.uint32),
        compiler_params=_cparams("arbitrary"),
        name="moe_experts",
    )(blk_expert, blk_row, n_used, xs, w1, w3, w2)


def _combine_kernel(x1_ref, r_ref, y_ref, mod_ref, fg_ref, o_ref):
    def expert_out(slot):
        hi, lo = _load_packed(y_ref.at[slot])
        return jnp.concatenate([hi, lo], axis=-1)
    w1 = r_ref[:, ROUTE_W1:ROUTE_W1 + 1]
    w2 = r_ref[:, ROUTE_W2:ROUTE_W2 + 1]
    y = w1 * expert_out(0) + w2 * expert_out(1)
    x2 = x1_ref[...] + mod_ref[0, 5:6, :] * y
    o_ref[...] = _rms(x2, fg_ref[...])


def _combine_call(x1, route, y_tok, mod, fg):
    n_tiles = N_LAT // TM
    row = lambda i: (i, 0)
    return pl.pallas_call(
        _combine_kernel,
        grid=(n_tiles,),
        in_specs=[
            pl.BlockSpec((TM, D_MODEL), row),
            pl.BlockSpec((TM, LANES), row),
            pl.BlockSpec((2, SC_SPLIT, TM, SC_ROW), lambda i: (0, 0, i, 0)),
            pl.BlockSpec((1, 6, D_MODEL), lambda i: (i // SEQ_TILES, 0, 0)),
            pl.BlockSpec((1, D_MODEL), lambda i: (0, 0)),
        ],
        out_specs=pl.BlockSpec((TM, D_MODEL), row),
        out_shape=jax.ShapeDtypeStruct((N_LAT, D_MODEL), F32),
        compiler_params=_cparams("parallel"),
        name="moe_combine",
    )(x1, route, y_tok, mod, fg)


def _moe_routed(h2p, x1, route, dest, cnt, mod, w1, w3, w2, fg):
    blks = (cnt[0, :N_EXPERTS].astype(jnp.int32) + (EXPERT_BLK - 1)) // EXPERT_BLK
    blk_end = jnp.cumsum(blks)
    n_used = blk_end[-1:]
    b = jnp.minimum(jnp.arange(N_EXPERT_BLKS, dtype=jnp.int32), n_used[0] - 1)
    blk_expert = jnp.sum((b[:, None] >= blk_end[None, :]).astype(jnp.int32), axis=1)
    blk_row = blk_expert * (EXPERT_CAP // EXPERT_BLK) + b - (blk_end - blks)[blk_expert]
    piece = jnp.arange(SC_SPLIT, dtype=jnp.int32)[:, None] * N_SORTED
    idx = [(piece + dest[:, slot][None, :]).reshape(SC_SPLIT * N_LAT) for slot in range(2)]
    xs = _sc_scatter(h2p.reshape(SC_SPLIT * N_LAT, SC_ROW), idx[0], idx[1], SC_SPLIT * N_SORTED)
    ys = _experts_call(blk_expert, blk_row, n_used, xs.reshape(SC_SPLIT, N_SORTED, SC_ROW),
                       w1, w3, w2)
    y_tok = _sc_gather(ys.reshape(SC_SPLIT * N_SORTED, SC_ROW), jnp.concatenate(idx))
    return _combine_call(x1, route, y_tok.reshape(2, SC_SPLIT, N_LAT, SC_ROW), mod, fg)


def _rope_partner_perm():
    perm, sign = [], []
    for j in range(QK_ROPE):
        first_half = (j % AXIS_ROPE) < ROPE_FREQS
        perm.append(j + ROPE_FREQS if first_half else j - ROPE_FREQS)
        sign.append(-1.0 if first_half else 1.0)
    return jnp.array(perm, jnp.int32), jnp.array(sign, F32)


def _rope_tables():
    t = jnp.arange(SEQ)
    row = (t // GRID_W).astype(F32)
    col = (t % GRID_W).astype(F32)
    inv_freq = ROPE_BASE ** (-2.0 * jnp.arange(ROPE_FREQS, dtype=F32) / AXIS_ROPE)
    ang = jnp.concatenate([row[:, None] * inv_freq, row[:, None] * inv_freq,
                           col[:, None] * inv_freq, col[:, None] * inv_freq], axis=1)
    cos = jnp.concatenate([jnp.cos(ang), jnp.ones((N_CTX, QK_ROPE), F32)], axis=0)
    sin = jnp.concatenate([jnp.sin(ang), jnp.zeros((N_CTX, QK_ROPE), F32)], axis=0)
    n = N_CTX + SEQ
    pad32 = jnp.zeros((n, HEAD_PAD - D_QK), F32)
    qs = ATT_SCALE * math.log2(math.e)
    cq = jnp.concatenate([jnp.full((n, QK_NOPE), qs, F32), qs * cos, pad32], axis=1)
    sq = jnp.concatenate([jnp.zeros((n, QK_NOPE), F32), qs * sin, pad32], axis=1)
    cs = jnp.concatenate([cos, sin, jnp.zeros((n, LANES - 2 * QK_ROPE), F32)], axis=1)
    return cq, sq, cs


def _layer_weights(w_in, w_uq, w_ukv):
    perm, sign = _rope_partner_perm()
    s0 = D_SSM + Q_LORA + KV_LORA
    kr_w = w_in[:, s0:s0 + QK_ROPE]
    wut = w_in[:, :D_SSM].T.astype(BF16)
    win = jnp.concatenate([w_in[:, D_SSM:s0], kr_w, kr_w[:, perm] * sign,
                           jnp.zeros((D_MODEL, LANES - 2 * QK_ROPE), F32)], axis=1).astype(BF16)
    uq = w_uq.reshape(Q_LORA, N_HEADS, D_QK)
    nope, rope = uq[..., :QK_NOPE], uq[..., QK_NOPE:]
    zpad = jnp.zeros((Q_LORA, N_HEADS, HEAD_PAD - D_QK), F32)
    wq1 = jnp.concatenate([nope, rope, zpad], axis=-1).reshape(Q_LORA, N_HEADS * HEAD_PAD).astype(BF16)
    wq2 = jnp.concatenate([jnp.zeros_like(nope), rope[..., perm] * sign, zpad], axis=-1)
    wq2 = wq2.reshape(Q_LORA, N_HEADS * HEAD_PAD).astype(BF16)
    ukv = w_ukv.reshape(KV_LORA, N_HEADS, QK_NOPE + V_HEAD)
    wk = jnp.concatenate([ukv[..., :QK_NOPE], jnp.zeros((KV_LORA, N_HEADS, HEAD_PAD - QK_NOPE), F32)],
                         axis=-1).reshape(KV_LORA, N_HEADS * HEAD_PAD)
    eye = jnp.eye(QK_ROPE, dtype=F32)
    place = jnp.concatenate([jnp.zeros((QK_ROPE, QK_NOPE), F32), eye,
                             jnp.zeros((QK_ROPE, HEAD_PAD - D_QK), F32)], axis=1)
    place = jnp.tile(place, (1, N_HEADS))
    spread = jnp.concatenate([place, place, jnp.zeros((LANES - 2 * QK_ROPE, N_HEADS * HEAD_PAD), F32)], 0)
    wkk = jnp.concatenate([wk, spread], axis=0).astype(BF16)
    wv = jnp.concatenate([ukv[..., QK_NOPE:], jnp.zeros((KV_LORA, N_HEADS, HEAD_PAD - V_HEAD), F32)],
                         axis=-1).reshape(KV_LORA, N_HEADS * HEAD_PAD).astype(BF16)
    return wut, win, wq1, wq2, wkk, wv


def kernel(x, c, ctx, c_ctx, w_ada, b_ada, norm_mix, norm_ffn, w_in, q_norm, kv_norm, w_uq, w_ukv,
           ssm_a_re, ssm_a_im, ssm_log_dt, ssm_b_re, ssm_b_im, ssm_c_re, ssm_c_im, ssm_d, w_glu,
           b_glu, w_out, ffn_w1, ffn_w3, ffn_w2, moe_router, moe_w1, moe_w3, moe_w2, final_norm):
    assert x.shape == (BATCH, SEQ, D_MODEL) and ctx.shape == (BATCH, CTX_LEN, D_MODEL)
    cond = jnp.concatenate([c, c_ctx[None, :], jnp.zeros((MOD_ROWS - BATCH - 1, D_MODEL), F32)], axis=0)
    mod_all = _ada_call(cond, w_ada, b_ada).reshape(DEPTH, MOD_ROWS, 6, D_MODEL)
    cq_t, sq_t, cs_t = _rope_tables()
    xs = (x.reshape(N_LAT, D_MODEL), ctx.reshape(N_CTX, D_MODEL))

    out = None
    for i in range(DEPTH):
        last = i == DEPTH - 1
        mod = mod_all[i]
        wut, win, wq1, wq2, wkk, wv = _layer_weights(w_in[i], w_uq[i], w_ukv[i])
        u_t, q, k, v = _inproj_call(xs, mod, norm_mix[i][None, :], wut, win, q_norm[i][None, :],
                                    kv_norm[i][None, :], wq1, wq2, wkk, wv, cq_t, sq_t, cs_t)
        tabs = _ssm_tables(ssm_a_re[i], ssm_a_im[i], ssm_log_dt[i], ssm_b_re[i], ssm_b_im[i],
                           ssm_c_re[i], ssm_c_im[i], ssm_d[i])
        y_ssm_t = _ssm_call(u_t.reshape(D_SSM, N_CHUNK, SSM_T), *tabs).reshape(D_SSM, N_TOT)
        y_att = _attn_latent_call(q, k, v)
        if last:
            n_tiles = LAT_TILES
        else:
            y_att = _attn_ctx_call(q, k, v, y_att)
            n_tiles = N_TOT // TM
        j = i // 2
        mix_args = (xs, y_ssm_t, y_att, mod, w_glu[i].T.astype(BF16), b_glu[i][:, None],
                    w_out[i].astype(BF16), norm_ffn[i][None, :])
        if i % 2 == 0:
            assert not last
            ffn = (ffn_w1[j].astype(BF16), ffn_w3[j].astype(BF16), ffn_w2[j].astype(BF16))
            xs = tuple(_mix_call(*mix_args, ffn=ffn, n_tiles=n_tiles))
        else:
            assert last
            r = moe_router[j]
            r_top = lax.bitcast_convert_type(
                lax.bitcast_convert_type(r, jnp.uint32) & jnp.uint32(0xFFFF0000), F32)
            r_hi = r_top.astype(BF16)
            r_lo = (r - r_top).astype(BF16)
            zr = jnp.zeros((D_MODEL, LANES - 2 * N_EXPERTS), BF16)
            router = jnp.stack([jnp.concatenate([r_hi, r_lo, zr], axis=1),
                                jnp.concatenate([r_hi, jnp.zeros_like(r_lo), zr], axis=1)])
            x1, h2p, route, dest, cnt = _mix_call(*mix_args, router=router, n_tiles=n_tiles)
            out = _moe_routed(h2p, x1, route, dest, cnt, mod, moe_w1[j], moe_w3[j], moe_w2[j],
                              final_norm[None, :])
    return out.reshape(BATCH, SEQ, D_MODEL)
```

```python
import functools
import math

import jax
import jax.numpy as jnp
import numpy as np
from jax import lax
from jax.experimental import pallas as pl
from jax.experimental.pallas import tpu as pltpu
from jax.experimental.pallas import tpu_sc as plsc

D_MODEL = 1024
BATCH = 4
SEQ = 8192
DEPTH = 2
GRID_W = 64
CTX_LEN = 256
D_SSM = 512
SSM_GROUP = 16
N_SSM_GROUPS = D_SSM // SSM_GROUP
SSM_STATE = 64
N_HEADS = 8
QK_NOPE = 64
QK_ROPE = 32
V_HEAD = 64
Q_LORA = 256
KV_LORA = 128
D_QK = QK_NOPE + QK_ROPE
D_ATT = N_HEADS * V_HEAD
D_MIX = D_SSM + D_ATT
AXIS_ROPE = QK_ROPE // 2
ROPE_FREQS = AXIS_ROPE // 2
ROPE_BASE = 10000.0
ATT_SCALE = 1.0 / math.sqrt(D_QK)
D_FF = 2816
N_EXPERTS = 8
D_FF_EXPERT = 1408
EPS = 1e-6

N_CTX = BATCH * CTX_LEN
N_LAT = BATCH * SEQ
N_TOT = N_CTX + N_LAT

LANES = 128
HEAD_PAD = 128
TM = 512
LAT_TILES = N_LAT // TM
SEQ_TILES = SEQ // TM
TQ = 512
KEY_PIECES = (2048, 2048, 2048, 2048)
assert sum(KEY_PIECES) == SEQ
SSM_T = 128
N_CHUNK_LAT = N_LAT // SSM_T
N_CHUNK = N_TOT // SSM_T
SSM_K = SSM_GROUP * SSM_T
FF_CHUNK = 256
D_IN_REST = Q_LORA + KV_LORA + LANES
MOD_ROWS = 8
VMEM_LIMIT = 52 * 1024 * 1024

F32 = jnp.float32
BF16 = jnp.bfloat16
HI = lax.Precision.HIGHEST


def _cparams(*sem):
    return pltpu.CompilerParams(dimension_semantics=sem, vmem_limit_bytes=VMEM_LIMIT)


def _const_spec(shape):
    nd = len(shape)
    return pl.BlockSpec(shape, lambda *_: (0,) * nd, pipeline_mode=pl.Buffered(1))


def _mod_row(i):
    return jnp.where(i < LAT_TILES, i // SEQ_TILES, BATCH)


def _pos_tile(i):
    return jnp.where(i < LAT_TILES, i % SEQ_TILES, SEQ_TILES + i - LAT_TILES)


def _rms(x, g):
    ms = jnp.mean(x * x, axis=-1, keepdims=True)
    return x * lax.rsqrt(ms + EPS) * g


ADA_TN = 1536


def _ada_kernel(c_ref, w_ref, b_ref, o_ref):
    c = c_ref[...]
    s = c * jax.nn.sigmoid(c)
    o_ref[0] = jnp.dot(s, w_ref[0], precision=HI, preferred_element_type=F32) + b_ref[0]


def _ada_call(cond, w_ada, b_ada):
    n_col = 6 * D_MODEL // ADA_TN
    return pl.pallas_call(
        _ada_kernel,
        grid=(DEPTH, n_col),
        in_specs=[
            pl.BlockSpec((MOD_ROWS, D_MODEL), lambda l, j: (0, 0)),
            pl.BlockSpec((1, D_MODEL, ADA_TN), lambda l, j: (l, 0, j)),
            pl.BlockSpec((1, 1, ADA_TN), lambda l, j: (l, 0, j)),
        ],
        out_specs=pl.BlockSpec((1, MOD_ROWS, ADA_TN), lambda l, j: (l, 0, j)),
        out_shape=jax.ShapeDtypeStruct((DEPTH, MOD_ROWS, 6 * D_MODEL), F32),
        compiler_params=_cparams("arbitrary", "arbitrary"),
        name="ada_mod",
    )(cond, w_ada, b_ada.reshape(DEPTH, 1, 6 * D_MODEL))


def _tile_rows(refs, n_x):
    if n_x == 1:
        return refs[0][...]
    return jnp.where(pl.program_id(0) < LAT_TILES, refs[0][...], refs[1][...])


def _x_specs(n_x):
    if n_x == 1:
        return [pl.BlockSpec((TM, D_MODEL), lambda i: (i, 0))]
    return [pl.BlockSpec((TM, D_MODEL), lambda i: (jnp.minimum(i, LAT_TILES - 1), 0)),
            pl.BlockSpec((TM, D_MODEL), lambda i: (jnp.maximum(i - LAT_TILES, 0), 0))]


def _inproj_kernel(*refs, n_x):
    (mod_ref, g_ref, wut_ref, win_ref, qg_ref, kvg_ref, wq1_ref, wq2_ref, wkk_ref, wv_ref,
     cq_ref, sq_ref, cs_ref, ut_ref, q_ref, k_ref, v_ref) = refs[n_x:]
    x = _tile_rows(refs, n_x)
    sh = mod_ref[0, 0:1, :]
    sc = mod_ref[0, 1:2, :]
    xm = (_rms(x, g_ref[...]) * (1.0 + sc) + sh).astype(BF16)
    ut_ref[...] = lax.dot_general(wut_ref[...], xm, (((1,), (1,)), ((), ())),
                                  preferred_element_type=F32)
    z = jnp.dot(xm, win_ref[...], preferred_element_type=F32)
    qn = _rms(z[:, :Q_LORA], qg_ref[...]).astype(BF16)
    kvn = _rms(z[:, Q_LORA:Q_LORA + KV_LORA], kvg_ref[...]).astype(BF16)
    krr = (z[:, Q_LORA + KV_LORA:] * cs_ref[...]).astype(BF16)
    q1 = jnp.dot(qn, wq1_ref[...], preferred_element_type=F32)
    q2 = jnp.dot(qn, wq2_ref[...], preferred_element_type=F32)
    cq = cq_ref[...]
    sq = sq_ref[...]
    for h in range(N_HEADS):
        sl = slice(h * HEAD_PAD, (h + 1) * HEAD_PAD)
        q_ref[:, sl] = (q1[:, sl] * cq + q2[:, sl] * sq).astype(q_ref.dtype)
    kin = jnp.concatenate([kvn, krr], axis=-1)
    k_ref[...] = jnp.dot(kin, wkk_ref[...], preferred_element_type=F32).astype(k_ref.dtype)
    vv = jnp.dot(kvn, wv_ref[...], preferred_element_type=F32)
    lane = lax.broadcasted_iota(jnp.int32, vv.shape, 1)
    v_ref[...] = jnp.where(lane % HEAD_PAD == V_HEAD, 1.0, vv).astype(v_ref.dtype)


def _inproj_call(xs, mod, g_mix, wut, win, qg, kvg, wq1, wq2, wkk, wv, cq_t, sq_t, cs_t):
    n_tiles = N_TOT // TM
    row = lambda i: (i, 0)
    pos = lambda i: (_pos_tile(i), 0)
    return pl.pallas_call(
        functools.partial(_inproj_kernel, n_x=len(xs)),
        grid=(n_tiles,),
        in_specs=_x_specs(len(xs)) + [
            pl.BlockSpec((1, 6, D_MODEL), lambda i: (_mod_row(i), 0, 0)),
            _const_spec((1, D_MODEL)),
            _const_spec((D_SSM, D_MODEL)),
            _const_spec((D_MODEL, D_IN_REST)),
            _const_spec((1, Q_LORA)),
            _const_spec((1, KV_LORA)),
            _const_spec((Q_LORA, N_HEADS * HEAD_PAD)),
            _const_spec((Q_LORA, N_HEADS * HEAD_PAD)),
            _const_spec((2 * KV_LORA, N_HEADS * HEAD_PAD)),
            _const_spec((KV_LORA, N_HEADS * HEAD_PAD)),
            pl.BlockSpec((TM, LANES), pos),
            pl.BlockSpec((TM, LANES), pos),
            pl.BlockSpec((TM, LANES), pos),
        ],
        out_specs=[
            pl.BlockSpec((D_SSM, TM), lambda i: (0, i)),
            pl.BlockSpec((TM, N_HEADS * HEAD_PAD), row),
            pl.BlockSpec((TM, N_HEADS * HEAD_PAD), row),
            pl.BlockSpec((TM, N_HEADS * HEAD_PAD), row),
        ],
        out_shape=[
            jax.ShapeDtypeStruct((D_SSM, N_TOT), F32),
            jax.ShapeDtypeStruct((N_TOT, N_HEADS * HEAD_PAD), BF16),
            jax.ShapeDtypeStruct((N_TOT, N_HEADS * HEAD_PAD), BF16),
            jax.ShapeDtypeStruct((N_TOT, N_HEADS * HEAD_PAD), BF16),
        ],
        compiler_params=_cparams("parallel"),
        name="in_proj",
    )(*xs, mod, g_mix, wut, win, qg, kvg, wq1, wq2, wkk, wv, cq_t, sq_t, cs_t)


def _attn_kernel(*refs, latent_pieces, tq):
    if latent_pieces:
        q_ref, k_ref, v_ref, kc_ref, vc_ref, o_ref, s_scr = refs
    else:
        q_ref, kc_ref, vc_ref, _, o_ref, s_scr = refs
    heads =[slice(hh * HEAD_PAD, (hh + 1) * HEAD_PAD) for hh in range(2)]
    qs = [q_ref[:, sl] for sl in heads]

    def put_scores(slot, k_at, width):
        for hh in range(2):
            s_scr[slot, hh, :, :width] = lax.dot_general(
                qs[hh], k_at(heads[hh]), (((1,), (1,)), ((), ())), preferred_element_type=F32)

    def consume(carry, slot, v_at, width):
        new = []
        for hh in range(2):
            m, acc = carry[hh]
            s = s_scr[slot, hh, :, :width]
            m_new = jnp.maximum(m, jnp.max(s, axis=-1, keepdims=True))
            alpha = jnp.exp2(m - m_new)
            p = jnp.exp2(s - m_new).astype(BF16)
            acc = alpha * acc + jnp.dot(p, v_at(heads[hh]), preferred_element_type=F32)
            new.append((m_new, acc))
        return tuple(new)

    def piece(kr, vr, start, size):
        return (lambda sl: kr[start:start + size, sl]), (lambda sl: vr[start:start + size, sl]), size

    pieces = [piece(kc_ref, vc_ref, 0, CTX_LEN)]
    start = 0
    for size in latent_pieces:
        pieces.append(piece(k_ref, v_ref, start, size))
        start += size
    carry = tuple((jnp.full((tq, 1), -jnp.inf, F32), jnp.zeros((tq, HEAD_PAD), F32))
                  for _ in range(2))
    put_scores(0, pieces[0][0], pieces[0][2])
    for i, (_, v_at, size) in enumerate(pieces):
        if i + 1 < len(pieces):
            put_scores((i + 1) % 2, pieces[i + 1][0], pieces[i + 1][2])
        carry = consume(carry, i % 2, v_at, size)
    outs = [acc[:, :V_HEAD] / acc[:, V_HEAD:V_HEAD + 1] for _, acc in carry]
    o_ref[...] = jnp.concatenate(outs, axis=-1).astype(o_ref.dtype)


def _attn_latent_call(q, k, v):
    qt = SEQ // TQ
    ctx0 = N_LAT // CTX_LEN
    return pl.pallas_call(
        functools.partial(_attn_kernel, latent_pieces=KEY_PIECES, tq=TQ),
        grid=(BATCH, N_HEADS // 2, qt),
        in_specs=[
            pl.BlockSpec((TQ, 2 * HEAD_PAD), lambda b, h, i: (b * qt + i, h)),
            pl.BlockSpec((SEQ, 2 * HEAD_PAD), lambda b, h, i: (b, h)),
            pl.BlockSpec((SEQ, 2 * HEAD_PAD), lambda b, h, i: (b, h)),
            pl.BlockSpec((CTX_LEN, 2 * HEAD_PAD), lambda b, h, i: (ctx0 + b, h)),
            pl.BlockSpec((CTX_LEN, 2 * HEAD_PAD), lambda b, h, i: (ctx0 + b, h)),
        ],
        out_specs=pl.BlockSpec((TQ, 2 * V_HEAD), lambda b, h, i: (b * qt + i, h)),
        out_shape=jax.ShapeDtypeStruct((N_TOT, D_ATT), BF16),
        scratch_shapes=[pltpu.VMEM((2, 2, TQ, max(KEY_PIECES)), F32)],
        compiler_params=_cparams("parallel", "parallel", "arbitrary"),
        name="attn_latent",
    )(q, k, v, k, v)


def _attn_ctx_call(q, k, v, y_att):
    ctx0 = N_LAT // CTX_LEN
    return pl.pallas_call(
        functools.partial(_attn_kernel, latent_pieces=(), tq=CTX_LEN),
        grid=(BATCH, N_HEADS // 2),
        in_specs=[
            pl.BlockSpec((CTX_LEN, 2 * HEAD_PAD), lambda b, h: (ctx0 + b, h)),
            pl.BlockSpec((CTX_LEN, 2 * HEAD_PAD), lambda b, h: (ctx0 + b, h)),
            pl.BlockSpec((CTX_LEN, 2 * HEAD_PAD), lambda b, h: (ctx0 + b, h)),
            pl.BlockSpec(memory_space=pl.ANY),
        ],
        out_specs=pl.BlockSpec((CTX_LEN, 2 * V_HEAD), lambda b, h: (ctx0 + b, h)),
        out_shape=jax.ShapeDtypeStruct((N_TOT, D_ATT), BF16),
        input_output_aliases={3: 0},
        scratch_shapes=[pltpu.VMEM((1, 2, CTX_LEN, CTX_LEN), F32)],
        compiler_params=_cparams("parallel", "parallel"),
        name="attn_ctx",
    )(q, k, v, y_att)


SSM_C = 64
SSM_HALVES = SSM_T // SSM_C
SSM_KC = SSM_GROUP * SSM_C
LAG_ROWS = 2 * SSM_C
PT_LAG, PT_INC, PT_OUT = 0, LAG_ROWS, LAG_ROWS + SSM_C
PT_ROWS = LAG_ROWS + 2 * SSM_C
NT_DIMS = (((1,), (1,)), ((), ()))


def _ssm_kernel(u_ref, pta_ref, ptb_ref, rows_ref, cc_ref, at_ref, dv_ref, y_ref,
                abt_hi_scr, abt_lo_scr, wl_scr, m_scr, ws_scr, wct_scr, s_scr, h_scr):
    H, C = SSM_GROUP, SSM_C
    half_lane = lax.broadcasted_iota(jnp.int32, (1, LANES), 1) < C

    def split(v):
        hi = v.astype(BF16)
        return hi, (v - hi.astype(F32)).astype(BF16)

    def scaled(row0, n_rows, ra, rb):
        return pta_ref[0, row0:row0 + n_rows, :] * ra + ptb_ref[0, row0:row0 + n_rows, :] * rb

    def build_tables(i, _):
        row0 = pl.multiple_of(i * C, C)
        b_re = rows_ref[0, pl.ds(i, 1), :]
        b_im = rows_ref[0, pl.ds(H + i, 1), :]
        lag_rows = pl.ds(pl.multiple_of(i * LAG_ROWS, LAG_ROWS), LAG_ROWS)
        abt_hi_scr[lag_rows, :], abt_lo_scr[lag_rows, :] = split(
            scaled(PT_LAG, LAG_ROWS, b_re, b_im))
        ws_scr[pl.ds(row0, C), :] = scaled(PT_INC, C, b_re, b_im).astype(BF16)
        c_re = rows_ref[0, pl.ds(2 * H + i, 1), :]
        c_im = rows_ref[0, pl.ds(3 * H + i, 1), :]
        wct_scr[pl.ds(row0, C), :] = scaled(PT_OUT, C, c_re, c_im).astype(BF16)
        return 0

    lax.fori_loop(0, H, build_tables, 0)
    cc_hi, cc_lo = split(cc_ref[0])
    nt = lambda a, b: lax.dot_general(a, b, NT_DIMS, preferred_element_type=F32)
    wl = nt(cc_hi, abt_hi_scr[...]) + nt(cc_hi, abt_lo_scr[...]) + nt(cc_lo, abt_hi_scr[...])
    wl_scr[0] = wl
    wl_scr[1] = jnp.concatenate(
        [pltpu.roll(wl[:, ci * LAG_ROWS:(ci + 1) * LAG_ROWS], C, 1) for ci in range(H)], axis=1)

    slot2 = lax.broadcasted_iota(jnp.int32, (1, 2 * LANES), 1)
    first_low = (slot2 < C) | (slot2 > 2 * LANES - C)
    first_high = slot2 < LANES

    def toeplitz_pair(x):
        return pltpu.roll(jnp.broadcast_to(x, (C, 2 * LANES)), 0, 1, stride=1, stride_axis=0)

    def build_toeplitz(ci, _):
        row0 = pl.multiple_of(ci * C, C)
        slots = pl.ds(pl.multiple_of(ci * LAG_ROWS, LAG_ROWS), LAG_ROWS)

        def lags(co, shifted):
            v = wl_scr[int(shifted), co:co + 1, slots]
            return jnp.concatenate([v, v], axis=1)

        for ka in range(0, H // 2, 2):
            kb = ka + 1
            low = toeplitz_pair(jnp.where(first_low, lags(2 * ka, False), lags(2 * kb, False)))
            high = toeplitz_pair(jnp.where(first_high, lags(2 * ka + 1, True),
                                           lags(2 * kb + 1, True)))
            for k, lanes in ((ka, slice(0, LANES)), (kb, slice(LANES, 2 * LANES))):
                m_scr[pl.ds(row0, C), k * LANES:(k + 1) * LANES] = jnp.where(
                    half_lane, low[:, lanes], high[:, lanes]).astype(BF16)
        return 0

    lax.fori_loop(0, H, build_toeplitz, 0)

    def chunk_operand(hf):
        cols = []
        for k in range(H // 2):
            a, b = u_ref[2 * k], u_ref[2 * k + 1]
            if hf == 0:
                cols.append(jnp.where(half_lane, a, pltpu.roll(b, C, 1)))
            else:
                cols.append(jnp.where(half_lane, pltpu.roll(a, C, 1), b))
        return jnp.concatenate(cols, axis=1).astype(BF16)

    ys = []
    for hf in range(SSM_HALVES):
        u = chunk_operand(hf)
        ys.append(jnp.dot(u, m_scr[...], preferred_element_type=F32))
        s = jnp.dot(u, ws_scr[...], preferred_element_type=F32)
        for d in range(2):
            s_d = s[:, d * LANES:(d + 1) * LANES]
            s_scr[hf, d] = s_d
            s_scr[hf, 2 + d] = pltpu.roll(s_d, SSM_STATE, 1)

    n_lat, n_ctx = SEQ // SSM_T, CTX_LEN // SSM_T
    ctx = [((N_CHUNK_LAT + c, n_ctx), hf) for c in range(n_ctx) for hf in range(SSM_HALVES)]
    lat = [((k, n_lat), hf) for k in range(n_lat) for hf in range(SSM_HALVES)]
    coef = [jnp.broadcast_to(at_ref[0, r:r + 1, :], (BATCH, 2 * SSM_STATE)) for r in range(4)]

    def advance(h, h_sw, d, rows, hf):
        a0, a1 = coef[2 * d], coef[2 * d + 1]
        return (h * a0 + h_sw * a1 + s_scr[hf, d, rows, :],
                h_sw * a0 - h * a1 + s_scr[hf, 2 + d, rows, :])

    zero = jnp.zeros((BATCH, 2 * SSM_STATE), F32)
    h_f, h_f_sw, h_r, h_r_sw = zero, zero, zero, zero
    for ((sf, stf), cf), ((sr, strd), cr) in zip(ctx + lat, ctx[::-1] + lat[::-1]):
        rows_f = pl.ds(sf, BATCH, stride=stf)
        rows_r = pl.ds(sr, BATCH, stride=strd)
        h_scr[cf, 0, rows_f, :] = h_f
        h_f, h_f_sw = advance(h_f, h_f_sw, 0, rows_f, cf)
        h_scr[cr, 1, rows_r, :] = h_r
        h_r, h_r_sw = advance(h_r, h_r_sw, 1, rows_r, cr)

    for hf in range(SSM_HALVES):
        h_in = jnp.concatenate([h_scr[hf, 0], h_scr[hf, 1]], axis=-1).astype(BF16)
        ys[hf] = ys[hf] + lax.dot_general(h_in, wct_scr[...], NT_DIMS,
                                          preferred_element_type=F32)
    for k in range(H // 2):
        y0, y1 = ys[0][:, k * LANES:(k + 1) * LANES], ys[1][:, k * LANES:(k + 1) * LANES]
        for c, yc in ((2 * k, jnp.where(half_lane, y0, pltpu.roll(y1, C, 1))),
                      (2 * k + 1, jnp.where(half_lane, pltpu.roll(y0, C, 1), y1))):
            y_ref[c] = yc + u_ref[c] * dv_ref[0, :, c * SSM_T:(c + 1) * SSM_T]


def _ssm_call(u_t, pta, ptb, rows, cc, at, dv):
    g3 = lambda g: (g, 0, 0)
    return pl.pallas_call(
        _ssm_kernel,
        grid=(N_SSM_GROUPS,),
        in_specs=[
            pl.BlockSpec((SSM_GROUP, N_CHUNK, SSM_T), g3),
            pl.BlockSpec((1, PT_ROWS, 4 * SSM_STATE), g3),
            pl.BlockSpec((1, PT_ROWS, 4 * SSM_STATE), g3),
            pl.BlockSpec((1, 4 * SSM_GROUP, 4 * SSM_STATE), g3),
            pl.BlockSpec((1, SSM_GROUP, 4 * SSM_STATE), g3),
            pl.BlockSpec((1, 4, 2 * SSM_STATE), g3),
            pl.BlockSpec((1, 1, SSM_K), g3),
        ],
        out_specs=pl.BlockSpec((SSM_GROUP, N_CHUNK, SSM_T), g3),
        out_shape=jax.ShapeDtypeStruct((D_SSM, N_CHUNK, SSM_T), F32),
        scratch_shapes=[
            pltpu.VMEM((SSM_GROUP * LAG_ROWS, 4 * SSM_STATE), BF16),
            pltpu.VMEM((SSM_GROUP * LAG_ROWS, 4 * SSM_STATE), BF16),
            pltpu.VMEM((2, SSM_GROUP, SSM_GROUP * LAG_ROWS), F32),
            pltpu.VMEM((SSM_KC, SSM_KC), BF16),
            pltpu.VMEM((SSM_KC, 4 * SSM_STATE), BF16),
            pltpu.VMEM((SSM_KC, 4 * SSM_STATE), BF16),
            pltpu.VMEM((SSM_HALVES, 4, N_CHUNK, 2 * SSM_STATE), F32),
            pltpu.VMEM((SSM_HALVES, 2, N_CHUNK, 2 * SSM_STATE), F32),
        ],
        compiler_params=_cparams("parallel"),
        name="s5_mixer",
    )(u_t, pta, ptb, rows, cc, at, dv)


def _ssm_tables(a_re, a_im, log_dt, b_re, b_im, c_re, c_im, d_skip):
    G, P, H, T = N_SSM_GROUPS, SSM_STATE, SSM_GROUP, SSM_T
    a_re, a_im = a_re.astype(F32), a_im.astype(F32)
    dt = jnp.exp(log_dt.astype(F32))[..., None]
    den = a_re * a_re + a_im * a_im
    mag1 = jnp.exp(dt * a_re)
    ab_re, ab_im = mag1 * jnp.cos(dt * a_im), mag1 * jnp.sin(dt * a_im)
    num_re = ab_re - 1.0
    f_re = (num_re * a_re + ab_im * a_im) / den
    f_im = (ab_im * a_re - num_re * a_im) / den
    b_re, b_im = b_re.astype(F32), b_im.astype(F32)
    bb_re = f_re[..., None] * b_re - f_im[..., None] * b_im
    bb_im = f_re[..., None] * b_im + f_im[..., None] * b_re
    c_re, c_im = c_re.astype(F32), c_im.astype(F32)
    la, th = dt * a_re, dt * a_im

    def powers(d, n):
        mag = jnp.exp(la[d][:, None, :] * n[None, :, None])
        ph = th[d][:, None, :] * n[None, :, None]
        return mag * jnp.cos(ph), mag * jnp.sin(ph)

    C = SSM_C
    slot = np.arange(LAG_ROWS)
    lag_f = np.where(slot < C, slot, -1)
    lag_r = np.where((-slot) % LAG_ROWS < C, (-slot) % LAG_ROWS, -1)
    s_idx = np.arange(C)
    f_all = np.concatenate([lag_f, C - 1 - s_idx, s_idx + 1])
    r_all = np.concatenate([lag_r, s_idx, C - s_idx])
    f_exp, r_exp = jnp.asarray(np.maximum(f_all, 0), F32), jnp.asarray(np.maximum(r_all, 0), F32)
    f_on, r_on = jnp.asarray(f_all >= 0, F32), jnp.asarray(r_all >= 0, F32)
    lane = jnp.arange(4 * P)
    is_fwd = (lane < 2 * P)[None, :]
    is_re = ((lane // P) % 2 == 0)[None, None, :]
    expo = jnp.where(is_fwd, f_exp[:, None], r_exp[:, None])[None]
    on = jnp.where(is_fwd, f_on[:, None], r_on[:, None])[None]
    la4 = jnp.concatenate([la[0], la[0], la[1], la[1]], axis=-1)[:, None, :]
    th4 = jnp.concatenate([th[0], th[0], th[1], th[1]], axis=-1)[:, None, :]
    mag = jnp.exp(la4 * expo) * on
    p_re, p_im = mag * jnp.cos(th4 * expo), mag * jnp.sin(th4 * expo)
    pta = jnp.where(is_re, p_re, p_im)
    ptb = jnp.where(is_re, -p_im, p_re)

    def per_channel(v):
        f, r = v[0].transpose(0, 2, 1), v[1].transpose(0, 2, 1)
        return jnp.concatenate([f, f, r, r], axis=-1)
    cf_re, cr_re = c_re[0], c_re[1]
    cf_im, cr_im = c_im[0], c_im[1]
    rows = jnp.concatenate([
        per_channel(bb_re), per_channel(bb_im),
        jnp.concatenate([cf_re, -cf_re, cr_re, -cr_re], axis=-1),
        jnp.concatenate([cf_im, -cf_im, cr_im, -cr_im], axis=-1)], axis=1)
    cc = jnp.concatenate([cf_re, -cf_im, cr_re, -cr_im], axis=-1)

    t_exp = jnp.full((1,), float(SSM_C), F32)
    (f_re_t, f_im_t), (r_re_t, r_im_t) = powers(0, t_exp), powers(1, t_exp)
    f_re_t, f_im_t, r_re_t, r_im_t = (v[:, 0, :] for v in (f_re_t, f_im_t, r_re_t, r_im_t))
    at = jnp.stack([jnp.concatenate([f_re_t, f_re_t], -1), jnp.concatenate([-f_im_t, f_im_t], -1),
                    jnp.concatenate([r_re_t, r_re_t], -1), jnp.concatenate([-r_im_t, r_im_t], -1)],
                   axis=1)
    dv = jnp.repeat(d_skip.astype(F32).reshape(G, H), T, axis=-1).reshape(G, 1, H * T)
    return pta, ptb, rows, cc, at, dv


ROUTE_E1, ROUTE_E2, ROUTE_W1, ROUTE_W2 = 0, 1, 2, 3


def _top2_route(logits):
    lane = lax.broadcasted_iota(jnp.int32, logits.shape, 1)
    lg = jnp.where(lane < N_EXPERTS, logits, -jnp.inf)
    m1 = jnp.max(lg, axis=-1, keepdims=True)
    i1 = jnp.min(jnp.where(lg == m1, lane, LANES), axis=-1, keepdims=True)
    lg2 = jnp.where(lane == i1, -jnp.inf, lg)
    m2 = jnp.max(lg2, axis=-1, keepdims=True)
    i2 = jnp.min(jnp.where(lg2 == m2, lane, LANES), axis=-1, keepdims=True)
    e2 = jnp.exp(m2 - m1)
    w1 = 1.0 / (1.0 + e2)
    rec = jnp.where(lane == ROUTE_E1, i1.astype(F32), 0.0)
    rec = jnp.where(lane == ROUTE_E2, i2.astype(F32), rec)
    rec = jnp.where(lane == ROUTE_W1, w1, rec)
    return jnp.where(lane == ROUTE_W2, e2 * w1, rec), i1, i2


def _expert_slots(i1, i2, taken):
    n = i1.shape[0]
    lane = lax.broadcasted_iota(jnp.int32, (n, LANES), 1)
    picked = jnp.where((lane == i1) | (lane == i2), 1.0, 0.0)
    earlier = (lax.broadcasted_iota(jnp.int32, (n, n), 1)
               < lax.broadcasted_iota(jnp.int32, (n, n), 0)).astype(BF16)
    rank = jnp.dot(earlier, picked.astype(BF16), preferred_element_type=F32)
    slot = rank + taken + (lane * EXPERT_CAP).astype(F32)
    d1 = jnp.sum(jnp.where(lane == i1, slot, 0.0), axis=-1, keepdims=True)
    d2 = jnp.sum(jnp.where(lane == i2, slot, 0.0), axis=-1, keepdims=True)
    dest = jnp.where(lane == 0, d1, jnp.where(lane == 1, d2, 0.0)).astype(jnp.int32)
    return dest, jnp.sum(picked, axis=0, keepdims=True)


def _pack_bf16_pairs(v):
    k = v.shape[1] // 2
    bits = pltpu.bitcast(v.astype(BF16).astype(F32), jnp.uint32)
    return (bits[:, :k] & jnp.uint32(0xFFFF0000)) | (bits[:, k:] >> 16)


def _unpack_bf16_pairs(w):
    hi = pltpu.bitcast(w & jnp.uint32(0xFFFF0000), F32)
    lo = pltpu.bitcast(w << 16, F32)
    return hi, lo


def _store_packed(ref, v):
    words = _pack_bf16_pairs(v)
    for s in range(SC_SPLIT):
        ref[s] = words[:, s * SC_ROW:(s + 1) * SC_ROW]


def _load_packed(ref):
    return _unpack_bf16_pairs(jnp.concatenate([ref[s] for s in range(SC_SPLIT)], axis=-1))


def _mix_kernel(*refs, with_router, n_x):
    rest = refs[n_x:]
    if with_router:
        (yst_ref, ya_ref, mod_ref, wglut_ref, bglu_ref, wout_ref, gffn_ref, router_ref,
         x1_ref, h2_ref, gate_ref, dest_ref, cnt_ref, taken_scr) = rest
    else:
        (yst_ref, ya_ref, mod_ref, wglut_ref, bglu_ref, wout_ref, gffn_ref,
         w1_ref, w3_ref, w2_ref, x2_ref) = rest
    zt = jax.nn.gelu(yst_ref[...], approximate=True)
    glt = zt * jax.nn.sigmoid(
        jnp.dot(wglut_ref[...], zt.astype(BF16), preferred_element_type=F32) + bglu_ref[...])
    mix = jnp.concatenate([glt.T.astype(BF16), ya_ref[...]], axis=-1)
    o = jnp.dot(mix, wout_ref[...], preferred_element_type=F32)
    x1 = _tile_rows(refs, n_x) + mod_ref[0, 2:3, :] * o
    h2 = _rms(x1, gffn_ref[...]) * (1.0 + mod_ref[0, 4:5, :]) + mod_ref[0, 3:4, :]
    if not with_router:
        acc = _swiglu(h2.astype(BF16), lambda sl: w1_ref[:, sl], lambda sl: w3_ref[:, sl],
                      lambda sl: w2_ref[sl, :], D_FF)
        x2_ref[...] = x1 + mod_ref[0, 5:6, :] * acc
    else:
        x1_ref[...] = x1
        _store_packed(h2_ref, h2)
        h_hi = h2.astype(BF16)
        h_lo = (h2 - h_hi.astype(F32)).astype(BF16)
        o1 = jnp.dot(h_hi, router_ref[0], preferred_element_type=F32)
        o2 = jnp.dot(h_lo, router_ref[1], preferred_element_type=F32)
        logits = o1 + pltpu.roll(o1, LANES - N_EXPERTS, 1) + o2
        gate_ref[...], i1, i2 = _top2_route(logits)

        @pl.when(pl.program_id(0) == 0)
        def _():
            taken_scr[...] = jnp.zeros_like(taken_scr)

        dest_ref[...], tile_cnt = _expert_slots(i1, i2, taken_scr[...])
        taken_scr[...] += tile_cnt
        cnt_ref[...] = jnp.broadcast_to(taken_scr[...], cnt_ref.shape)


def _mix_call(xs, y_ssm_t, y_att, mod, wglut, bglu, wout, gffn, *, router=None, ffn=None, n_tiles):
    row = lambda i: (i, 0)
    out_row = row
    with_router = router is not None
    assert with_router != (ffn is not None)
    in_specs = _x_specs(len(xs)) + [
        pl.BlockSpec((D_SSM, TM), lambda i: (0, i)),
        pl.BlockSpec((TM, D_ATT), row),
        pl.BlockSpec((1, 6, D_MODEL), lambda i: (_mod_row(i), 0, 0)),
        _const_spec((D_SSM, D_SSM)),
        _const_spec((D_SSM, 1)),
        _const_spec((D_MIX, D_MODEL)),
        _const_spec((1, D_MODEL)),
    ]
    args = [*xs, y_ssm_t, y_att, mod, wglut, bglu, wout, gffn]
    out_specs = [pl.BlockSpec((TM, D_MODEL), out_row)]
    out_shape = [jax.ShapeDtypeStruct((n_tiles * TM, D_MODEL), F32)]
    if with_router:
        in_specs.append(_const_spec((2, D_MODEL, LANES)))
        args.append(router)
        out_specs += [pl.BlockSpec((SC_SPLIT, TM, SC_ROW), lambda i: (0, i, 0)),
                      pl.BlockSpec((TM, LANES), out_row),
                      pl.BlockSpec((TM, LANES), out_row),
                      pl.BlockSpec((8, LANES), lambda i: (0, 0))]
        out_shape += [jax.ShapeDtypeStruct((SC_SPLIT, n_tiles * TM, SC_ROW), jnp.uint32),
                      jax.ShapeDtypeStruct((n_tiles * TM, LANES), F32),
                      jax.ShapeDtypeStruct((n_tiles * TM, LANES), jnp.int32),
                      jax.ShapeDtypeStruct((8, LANES), F32)]
        scratch = [pltpu.VMEM((1, LANES), F32)]
    else:
        in_specs += [_const_spec((D_MODEL, D_FF)), _const_spec((D_MODEL, D_FF)),
                     _const_spec((D_FF, D_MODEL))]
        args += list(ffn)
        scratch = []
    return pl.pallas_call(
        functools.partial(_mix_kernel, with_router=with_router, n_x=len(xs)),
        grid=(n_tiles,),
        in_specs=in_specs,
        out_specs=out_specs,
        out_shape=out_shape,
        scratch_shapes=scratch,
        compiler_params=_cparams("arbitrary" if with_router else "parallel"),
        name="mix_out",
    )(*args)


def _swiglu(h, w1_at, w3_at, w2_at, d_ff):
    acc = jnp.zeros((h.shape[0], D_MODEL), F32)
    for lo in range(0, d_ff, FF_CHUNK):
        sl = slice(lo, min(lo + FF_CHUNK, d_ff))
        a = jnp.dot(h, w1_at(sl), preferred_element_type=F32)
        b = jnp.dot(h, w3_at(sl), preferred_element_type=F32)
        g = (a * jax.nn.sigmoid(a) * b).astype(BF16)
        acc = acc + jnp.dot(g, w2_at(sl), preferred_element_type=F32)
    return acc


EXPERT_BLK = 512
EXPERT_CAP = N_LAT
N_SORTED = N_EXPERTS * EXPERT_CAP
N_EXPERT_BLKS = 2 * N_LAT // EXPERT_BLK + N_EXPERTS
PACKED = D_MODEL // 2
SC_ROW = 256
SC_SPLIT = PACKED // SC_ROW
SC_WIN = 128


def _sc_mesh():
    return plsc.VectorSubcoreMesh(core_axis_name="core", subcore_axis_name="subcore")


def _sc_scatter(x, idx_a, idx_b, n_out):
    n = x.shape[0]

    @pl.kernel(out_type=jax.ShapeDtypeStruct((n_out, SC_ROW), x.dtype), mesh=_sc_mesh(),
               scratch_types=[])
    def scatter(x_hbm, a_hbm, b_hbm, o_hbm):
        def body(x_vmem, a_vmem, b_vmem):
            pltpu.sync_copy(x_vmem, o_hbm.at[a_vmem.at[0]])
            pltpu.sync_copy(x_vmem, o_hbm.at[b_vmem.at[0]])

        pltpu.emit_pipeline(
            body, grid=(n // SC_WIN,),
            in_specs=[pl.BlockSpec((SC_WIN, SC_ROW), lambda i: (i, 0)),
                      pl.BlockSpec((1, SC_WIN), lambda i: (0, i)),
                      pl.BlockSpec((1, SC_WIN), lambda i: (0, i))],
            out_specs=[],
            core_axis_name=("core", "subcore"),
            dimension_semantics=(pltpu.PARALLEL,),
        )(x_hbm, a_hbm, b_hbm)

    return scatter(x, idx_a.reshape(1, n), idx_b.reshape(1, n))


def _sc_gather(y, idx):
    n = idx.shape[0]

    @pl.kernel(out_type=jax.ShapeDtypeStruct((n, SC_ROW), y.dtype), mesh=_sc_mesh(),
               scratch_types=[])
    def gather(y_hbm, i_hbm, o_hbm):
        def body(i_vmem, o_vmem):
            pltpu.sync_copy(y_hbm.at[i_vmem.at[0]], o_vmem)

        pltpu.emit_pipeline(
            body, grid=(n // SC_WIN,),
            in_specs=[pl.BlockSpec((1, SC_WIN), lambda i: (0, i))],
            out_specs=[pl.BlockSpec((SC_WIN, SC_ROW), lambda i: (i, 0))],
            core_axis_name=("core", "subcore"),
            dimension_semantics=(pltpu.PARALLEL,),
        )(i_hbm, o_hbm)

    return gather(y, idx.reshape(1, n))


W_CAST_ROWS = 128


def _experts_kernel(blk_expert_ref, blk_row_ref, n_used_ref, x_ref, w1_ref, w3_ref, w2_ref, o_ref,
                    w1_scr, w3_scr, w2_scr):
    del blk_row_ref
    b = pl.program_id(0)
    live = b < n_used_ref[0]
    new_expert = (b == 0) | (blk_expert_ref[b] != blk_expert_ref[jnp.maximum(b - 1, 0)])

    @pl.when(live & new_expert)
    def _():
        for src, dst in ((w1_ref, w1_scr), (w3_ref, w3_scr), (w2_ref, w2_scr)):
            for r in range(0, dst.shape[0], W_CAST_ROWS):
                dst[r:r + W_CAST_ROWS, :] = src[0, r:r + W_CAST_ROWS, :].astype(BF16)

    @pl.when(live)
    def _():
        hi, lo = _load_packed(x_ref)
        h = jnp.concatenate([hi.astype(BF16), lo.astype(BF16)], axis=-1)
        y = _swiglu(h, lambda sl: w1_scr[:, sl], lambda sl: w3_scr[:, sl],
                    lambda sl: w2_scr[sl, :], D_FF_EXPERT)
        _store_packed(o_ref, y)


def _experts_call(blk_expert, blk_row, n_used, xs, w1, w3, w2):
    row = lambda b, be, br, nu: (0, br[b], 0)
    wsel = lambda b, be, br, nu: (be[b], 0, 0)
    return pl.pallas_call(
        _experts_kernel,
        grid_spec=pltpu.PrefetchScalarGridSpec(
            num_scalar_prefetch=3,
            grid=(N_EXPERT_BLKS,),
            in_specs=[
                pl.BlockSpec((SC_SPLIT, EXPERT_BLK, SC_ROW), row),
                pl.BlockSpec((1, D_MODEL, D_FF_EXPERT), wsel),
                pl.BlockSpec((1, D_MODEL, D_FF_EXPERT), wsel),
                pl.BlockSpec((1, D_FF_EXPERT, D_MODEL), wsel),
            ],
            out_specs=pl.BlockSpec((SC_SPLIT, EXPERT_BLK, SC_ROW), row),
            scratch_shapes=[pltpu.VMEM((D_MODEL, D_FF_EXPERT), BF16),
                            pltpu.VMEM((D_MODEL, D_FF_EXPERT), BF16),
                            pltpu.VMEM((D_FF_EXPERT, D_MODEL), BF16)],
        ),
        out_shape=jax.ShapeDtypeStruct((SC_SPLIT, N_SORTED, SC_ROW), jnp.uint32),
        compiler_params=_cparams("arbitrary"),
        name="moe_experts",
    )(blk_expert, blk_row, n_used, xs, w1, w3, w2)


def _combine_kernel(x1_ref, r_ref, y_ref, mod_ref, fg_ref, *rest):
    o_ref = rest[-1]

    def expert_out(slot):
        hi, lo = _load_packed(y_ref.at[slot])
        return jnp.concatenate([hi, lo], axis=-1)
    w1 = r_ref[:, ROUTE_W1:ROUTE_W1 + 1]
    w2 = r_ref[:, ROUTE_W2:ROUTE_W2 + 1]
    y = w1 * expert_out(0) + w2 * expert_out(1)
    x2 = x1_ref[...] + mod_ref[0, 5:6, :] * y
    o_ref[...] = _rms(x2, fg_ref[...])


COMBINE_PARTS = 2


def _combine_call(x1, route, y_tok, mod, fg, part, out):
    n_tiles = N_LAT // TM // COMBINE_PARTS
    tile0 = part * n_tiles
    row = lambda i: (tile0 + i, 0)
    in_specs = [
        pl.BlockSpec((TM, D_MODEL), row),
        pl.BlockSpec((TM, LANES), row),
        pl.BlockSpec((2, SC_SPLIT, TM, SC_ROW), lambda i: (0, 0, i, 0)),
        pl.BlockSpec((1, 6, D_MODEL), lambda i: ((tile0 + i) // SEQ_TILES, 0, 0)),
        pl.BlockSpec((1, D_MODEL), lambda i: (0, 0)),
    ]
    args = [x1, route, y_tok, mod, fg]
    aliases = {}
    if out is not None:
        in_specs.append(pl.BlockSpec(memory_space=pl.ANY))
        args.append(out)
        aliases = {len(args) - 1: 0}
    return pl.pallas_call(
        _combine_kernel,
        grid=(n_tiles,),
        in_specs=in_specs,
        out_specs=pl.BlockSpec((TM, D_MODEL), row),
        out_shape=jax.ShapeDtypeStruct((N_LAT, D_MODEL), F32),
        input_output_aliases=aliases,
        compiler_params=_cparams("parallel"),
        name="moe_combine",
    )(*args)


def _moe_routed(h2p, x1, route, dest, cnt, mod, w1, w3, w2, fg):
    blks = (cnt[0, :N_EXPERTS].astype(jnp.int32) + (EXPERT_BLK - 1)) // EXPERT_BLK
    blk_end = jnp.cumsum(blks)
    n_used = blk_end[-1:]
    b = jnp.minimum(jnp.arange(N_EXPERT_BLKS, dtype=jnp.int32), n_used[0] - 1)
    blk_expert = jnp.sum((b[:, None] >= blk_end[None, :]).astype(jnp.int32), axis=1)
    blk_row = blk_expert * (EXPERT_CAP // EXPERT_BLK) + b - (blk_end - blks)[blk_expert]
    piece = jnp.arange(SC_SPLIT, dtype=jnp.int32)[:, None] * N_SORTED
    idx = [(piece + dest[:, slot][None, :]).reshape(SC_SPLIT * N_LAT) for slot in range(2)]
    xs = _sc_scatter(h2p.reshape(SC_SPLIT * N_LAT, SC_ROW), idx[0], idx[1], SC_SPLIT * N_SORTED)
    ys = _experts_call(blk_expert, blk_row, n_used, xs.reshape(SC_SPLIT, N_SORTED, SC_ROW),
                       w1, w3, w2)
    ys_flat = ys.reshape(SC_SPLIT * N_SORTED, SC_ROW)
    n_part = N_LAT // COMBINE_PARTS
    out = None
    for part in range(COMBINE_PARTS):
        tokens = slice(part * n_part, (part + 1) * n_part)
        idx_part = jnp.concatenate(
            [(piece + dest[tokens, slot][None, :]).reshape(SC_SPLIT * n_part) for slot in range(2)])
        y_tok = _sc_gather(ys_flat, idx_part).reshape(2, SC_SPLIT, n_part, SC_ROW)
        out = _combine_call(x1, route, y_tok, mod, fg, part, out)
    return out


def _rope_partner_perm():
    perm, sign = [], []
    for j in range(QK_ROPE):
        first_half = (j % AXIS_ROPE) < ROPE_FREQS
        perm.append(j + ROPE_FREQS if first_half else j - ROPE_FREQS)
        sign.append(-1.0 if first_half else 1.0)
    return jnp.array(perm, jnp.int32), jnp.array(sign, F32)


def _rope_tables():
    t = jnp.arange(SEQ)
    row = (t // GRID_W).astype(F32)
    col = (t % GRID_W).astype(F32)
    inv_freq = ROPE_BASE ** (-2.0 * jnp.arange(ROPE_FREQS, dtype=F32) / AXIS_ROPE)
    ang = jnp.concatenate([row[:, None] * inv_freq, row[:, None] * inv_freq,
                           col[:, None] * inv_freq, col[:, None] * inv_freq], axis=1)
    cos = jnp.concatenate([jnp.cos(ang), jnp.ones((N_CTX, QK_ROPE), F32)], axis=0)
    sin = jnp.concatenate([jnp.sin(ang), jnp.zeros((N_CTX, QK_ROPE), F32)], axis=0)
    n = N_CTX + SEQ
    pad32 = jnp.zeros((n, HEAD_PAD - D_QK), F32)
    qs = ATT_SCALE * math.log2(math.e)
    cq = jnp.concatenate([jnp.full((n, QK_NOPE), qs, F32), qs * cos, pad32], axis=1)
    sq = jnp.concatenate([jnp.zeros((n, QK_NOPE), F32), qs * sin, pad32], axis=1)
    cs = jnp.concatenate([cos, sin, jnp.zeros((n, LANES - 2 * QK_ROPE), F32)], axis=1)
    return cq, sq, cs


def _layer_weights(w_in, w_uq, w_ukv):
    perm, sign = _rope_partner_perm()
    s0 = D_SSM + Q_LORA + KV_LORA
    kr_w = w_in[:, s0:s0 + QK_ROPE]
    wut = w_in[:, :D_SSM].T.astype(BF16)
    win = jnp.concatenate([w_in[:, D_SSM:s0], kr_w, kr_w[:, perm] * sign,
                           jnp.zeros((D_MODEL, LANES - 2 * QK_ROPE), F32)], axis=1).astype(BF16)
    uq = w_uq.reshape(Q_LORA, N_HEADS, D_QK)
    nope, rope = uq[..., :QK_NOPE], uq[..., QK_NOPE:]
    zpad = jnp.zeros((Q_LORA, N_HEADS, HEAD_PAD - D_QK), F32)
    wq1 = jnp.concatenate([nope, rope, zpad], axis=-1).reshape(Q_LORA, N_HEADS * HEAD_PAD).astype(BF16)
    wq2 = jnp.concatenate([jnp.zeros_like(nope), rope[..., perm] * sign, zpad], axis=-1)
    wq2 = wq2.reshape(Q_LORA, N_HEADS * HEAD_PAD).astype(BF16)
    ukv = w_ukv.reshape(KV_LORA, N_HEADS, QK_NOPE + V_HEAD)
    wk = jnp.concatenate([ukv[..., :QK_NOPE], jnp.zeros((KV_LORA, N_HEADS, HEAD_PAD - QK_NOPE), F32)],
                         axis=-1).reshape(KV_LORA, N_HEADS * HEAD_PAD)
    eye = jnp.eye(QK_ROPE, dtype=F32)
    place = jnp.concatenate([jnp.zeros((QK_ROPE, QK_NOPE), F32), eye,
                             jnp.zeros((QK_ROPE, HEAD_PAD - D_QK), F32)], axis=1)
    place = jnp.tile(place, (1, N_HEADS))
    spread = jnp.concatenate([place, place, jnp.zeros((LANES - 2 * QK_ROPE, N_HEADS * HEAD_PAD), F32)], 0)
    wkk = jnp.concatenate([wk, spread], axis=0).astype(BF16)
    wv = jnp.concatenate([ukv[..., QK_NOPE:], jnp.zeros((KV_LORA, N_HEADS, HEAD_PAD - V_HEAD), F32)],
                         axis=-1).reshape(KV_LORA, N_HEADS * HEAD_PAD).astype(BF16)
    return wut, win, wq1, wq2, wkk, wv


def kernel(x, c, ctx, c_ctx, w_ada, b_ada, norm_mix, norm_ffn, w_in, q_norm, kv_norm, w_uq, w_ukv,
           ssm_a_re, ssm_a_im, ssm_log_dt, ssm_b_re, ssm_b_im, ssm_c_re, ssm_c_im, ssm_d, w_glu,
           b_glu, w_out, ffn_w1, ffn_w3, ffn_w2, moe_router, moe_w1, moe_w3, moe_w2, final_norm):
    assert x.shape == (BATCH, SEQ, D_MODEL) and ctx.shape == (BATCH, CTX_LEN, D_MODEL)
    cond = jnp.concatenate([c, c_ctx[None, :], jnp.zeros((MOD_ROWS - BATCH - 1, D_MODEL), F32)], axis=0)
    mod_all = _ada_call(cond, w_ada, b_ada).reshape(DEPTH, MOD_ROWS, 6, D_MODEL)
    cq_t, sq_t, cs_t = _rope_tables()
    xs = (x.reshape(N_LAT, D_MODEL), ctx.reshape(N_CTX, D_MODEL))

    out = None
    for i in range(DEPTH):
        last = i == DEPTH - 1
        mod = mod_all[i]
        wut, win, wq1, wq2, wkk, wv = _layer_weights(w_in[i], w_uq[i], w_ukv[i])
        u_t, q, k, v = _inproj_call(xs, mod, norm_mix[i][None, :], wut, win, q_norm[i][None, :],
                                    kv_norm[i][None, :], wq1, wq2, wkk, wv, cq_t, sq_t, cs_t)
        tabs = _ssm_tables(ssm_a_re[i], ssm_a_im[i], ssm_log_dt[i], ssm_b_re[i], ssm_b_im[i],
                           ssm_c_re[i], ssm_c_im[i], ssm_d[i])
        y_ssm_t = _ssm_call(u_t.reshape(D_SSM, N_CHUNK, SSM_T), *tabs).reshape(D_SSM, N_TOT)
        y_att = _attn_latent_call(q, k, v)
        if last:
            n_tiles = LAT_TILES
        else:
            y_att = _attn_ctx_call(q, k, v, y_att)
            n_tiles = N_TOT // TM
        j = i // 2
        mix_args = (xs, y_ssm_t, y_att, mod, w_glu[i].T.astype(BF16), b_glu[i][:, None],
                    w_out[i].astype(BF16), norm_ffn[i][None, :])
        if i % 2 == 0:
            assert not last
            ffn = (ffn_w1[j].astype(BF16), ffn_w3[j].astype(BF16), ffn_w2[j].astype(BF16))
            xs = tuple(_mix_call(*mix_args, ffn=ffn, n_tiles=n_tiles))
        else:
            assert last
            r = moe_router[j]
            r_top = lax.bitcast_convert_type(
                lax.bitcast_convert_type(r, jnp.uint32) & jnp.uint32(0xFFFF0000), F32)
            r_hi = r_top.astype(BF16)
            r_lo = (r - r_top).astype(BF16)
            zr = jnp.zeros((D_MODEL, LANES - 2 * N_EXPERTS), BF16)
            router = jnp.stack([jnp.concatenate([r_hi, r_lo, zr], axis=1),
                                jnp.concatenate([r_hi, jnp.zeros_like(r_lo), zr], axis=1)])
            x1, h2p, route, dest, cnt = _mix_call(*mix_args, router=router, n_tiles=n_tiles)
            out = _moe_routed(h2p, x1, route, dest, cnt, mod, moe_w1[j], moe_w3[j], moe_w2[j],
                              final_norm[None, :])
    return out.reshape(BATCH, SEQ, D_MODEL)
```

```python
import functools
import math

import jax
import jax.numpy as jnp
import numpy as np
from jax import lax
from jax.experimental import pallas as pl
from jax.experimental.pallas import tpu as pltpu
from jax.experimental.pallas import tpu_sc as plsc

D_MODEL = 1024
BATCH = 4
SEQ = 8192
DEPTH = 2
GRID_W = 64
CTX_LEN = 256
D_SSM = 512
SSM_GROUP = 16
N_SSM_GROUPS = D_SSM // SSM_GROUP
SSM_STATE = 64
N_HEADS = 8
QK_NOPE = 64
QK_ROPE = 32
V_HEAD = 64
Q_LORA = 256
KV_LORA = 128
D_QK = QK_NOPE + QK_ROPE
D_ATT = N_HEADS * V_HEAD
D_MIX = D_SSM + D_ATT
AXIS_ROPE = QK_ROPE // 2
ROPE_FREQS = AXIS_ROPE // 2
ROPE_BASE = 10000.0
ATT_SCALE = 1.0 / math.sqrt(D_QK)
D_FF = 2816
N_EXPERTS = 8
D_FF_EXPERT = 1408
EPS = 1e-6

N_CTX = BATCH * CTX_LEN
N_LAT = BATCH * SEQ
N_TOT = N_CTX + N_LAT

LANES = 128
HEAD_PAD = 128
VT_ROWS = 80
NT_DIMS = (((1,), (1,)), ((), ()))
TM = 512
LAT_TILES = N_LAT // TM
SEQ_TILES = SEQ // TM
TQ = 512
KEY_PIECES = (2048, 2048, 2048, 2048)
assert sum(KEY_PIECES) == SEQ
SSM_T = 128
N_CHUNK_LAT = N_LAT // SSM_T
N_CHUNK = N_TOT // SSM_T
SSM_K = SSM_GROUP * SSM_T
FF_CHUNK = 256
D_IN_REST = Q_LORA + KV_LORA + LANES
MOD_ROWS = 8
VMEM_LIMIT = 52 * 1024 * 1024

F32 = jnp.float32
BF16 = jnp.bfloat16
HI = lax.Precision.HIGHEST


def _cparams(*sem):
    return pltpu.CompilerParams(dimension_semantics=sem, vmem_limit_bytes=VMEM_LIMIT)


def _const_spec(shape):
    nd = len(shape)
    return pl.BlockSpec(shape, lambda *_: (0,) * nd, pipeline_mode=pl.Buffered(1))


def _mod_row(i):
    return jnp.where(i < LAT_TILES, i // SEQ_TILES, BATCH)


def _pos_tile(i):
    return jnp.where(i < LAT_TILES, i % SEQ_TILES, SEQ_TILES + i - LAT_TILES)


def _rms(x, g):
    ms = jnp.mean(x * x, axis=-1, keepdims=True)
    return x * lax.rsqrt(ms + EPS) * g


ADA_TN = 1536


def _ada_kernel(c_ref, w_ref, b_ref, o_ref):
    c = c_ref[...]
    s = c * jax.nn.sigmoid(c)
    o_ref[0] = jnp.dot(s, w_ref[0], precision=HI, preferred_element_type=F32) + b_ref[0]


def _ada_call(cond, w_ada, b_ada):
    n_col = 6 * D_MODEL // ADA_TN
    return pl.pallas_call(
        _ada_kernel,
        grid=(DEPTH, n_col),
        in_specs=[
            pl.BlockSpec((MOD_ROWS, D_MODEL), lambda l, j: (0, 0)),
            pl.BlockSpec((1, D_MODEL, ADA_TN), lambda l, j: (l, 0, j)),
            pl.BlockSpec((1, 1, ADA_TN), lambda l, j: (l, 0, j)),
        ],
        out_specs=pl.BlockSpec((1, MOD_ROWS, ADA_TN), lambda l, j: (l, 0, j)),
        out_shape=jax.ShapeDtypeStruct((DEPTH, MOD_ROWS, 6 * D_MODEL), F32),
        compiler_params=_cparams("arbitrary", "arbitrary"),
        name="ada_mod",
    )(cond, w_ada, b_ada.reshape(DEPTH, 1, 6 * D_MODEL))


def _tile_rows(refs, n_x):
    if n_x == 1:
        return refs[0][...]
    return jnp.where(pl.program_id(0) < LAT_TILES, refs[0][...], refs[1][...])


def _x_specs(n_x):
    if n_x == 1:
        return [pl.BlockSpec((TM, D_MODEL), lambda i: (i, 0))]
    return [pl.BlockSpec((TM, D_MODEL), lambda i: (jnp.minimum(i, LAT_TILES - 1), 0)),
            pl.BlockSpec((TM, D_MODEL), lambda i: (jnp.maximum(i - LAT_TILES, 0), 0))]


def _inproj_kernel(*refs, n_x):
    (mod_ref, g_ref, wut_ref, win_ref, qg_ref, kvg_ref, wq1_ref, wq2_ref, wkk_ref, wv_ref,
     cq_ref, sq_ref, cs_ref, ut_ref, q_ref, k_ref, v_ref) = refs[n_x:]
    x = _tile_rows(refs, n_x)
    sh = mod_ref[0, 0:1, :]
    sc = mod_ref[0, 1:2, :]
    xm = (_rms(x, g_ref[...]) * (1.0 + sc) + sh).astype(BF16)
    ut_ref[...] = lax.dot_general(wut_ref[...], xm, (((1,), (1,)), ((), ())),
                                  preferred_element_type=F32)
    z = jnp.dot(xm, win_ref[...], preferred_element_type=F32)
    qn = _rms(z[:, :Q_LORA], qg_ref[...]).astype(BF16)
    kvn = _rms(z[:, Q_LORA:Q_LORA + KV_LORA], kvg_ref[...]).astype(BF16)
    krr = (z[:, Q_LORA + KV_LORA:] * cs_ref[...]).astype(BF16)
    q1 = jnp.dot(qn, wq1_ref[...], preferred_element_type=F32)
    q2 = jnp.dot(qn, wq2_ref[...], preferred_element_type=F32)
    cq = cq_ref[...]
    sq = sq_ref[...]
    for h in range(N_HEADS):
        sl = slice(h * HEAD_PAD, (h + 1) * HEAD_PAD)
        q_ref[:, sl] = (q1[:, sl] * cq + q2[:, sl] * sq).astype(q_ref.dtype)
    kin = jnp.concatenate([kvn, krr], axis=-1)
    k_ref[...] = jnp.dot(kin, wkk_ref[...], preferred_element_type=F32).astype(k_ref.dtype)
    vt = lax.dot_general(wv_ref[...], kvn, NT_DIMS, preferred_element_type=F32)
    row = lax.broadcasted_iota(jnp.int32, vt.shape, 0)
    v_ref[...] = jnp.where(row % VT_ROWS == V_HEAD, 1.0, vt).astype(v_ref.dtype)


def _inproj_call(xs, mod, g_mix, wut, win, qg, kvg, wq1, wq2, wkk, wv, cq_t, sq_t, cs_t):
    n_tiles = N_TOT // TM
    row = lambda i: (i, 0)
    pos = lambda i: (_pos_tile(i), 0)
    return pl.pallas_call(
        functools.partial(_inproj_kernel, n_x=len(xs)),
        grid=(n_tiles,),
        in_specs=_x_specs(len(xs)) + [
            pl.BlockSpec((1, 6, D_MODEL), lambda i: (_mod_row(i), 0, 0)),
            _const_spec((1, D_MODEL)),
            _const_spec((D_SSM, D_MODEL)),
            _const_spec((D_MODEL, D_IN_REST)),
            _const_spec((1, Q_LORA)),
            _const_spec((1, KV_LORA)),
            _const_spec((Q_LORA, N_HEADS * HEAD_PAD)),
            _const_spec((Q_LORA, N_HEADS * HEAD_PAD)),
            _const_spec((2 * KV_LORA, N_HEADS * HEAD_PAD)),
            _const_spec((N_HEADS * VT_ROWS, KV_LORA)),
            pl.BlockSpec((TM, LANES), pos),
            pl.BlockSpec((TM, LANES), pos),
            pl.BlockSpec((TM, LANES), pos),
        ],
        out_specs=[
            pl.BlockSpec((D_SSM, TM), lambda i: (0, i)),
            pl.BlockSpec((TM, N_HEADS * HEAD_PAD), row),
            pl.BlockSpec((TM, N_HEADS * HEAD_PAD), row),
            pl.BlockSpec((N_HEADS * VT_ROWS, TM), lambda i: (0, i)),
        ],
        out_shape=[
            jax.ShapeDtypeStruct((D_SSM, N_TOT), F32),
            jax.ShapeDtypeStruct((N_TOT, N_HEADS * HEAD_PAD), BF16),
            jax.ShapeDtypeStruct((N_TOT, N_HEADS * HEAD_PAD), BF16),
            jax.ShapeDtypeStruct((N_HEADS * VT_ROWS, N_TOT), BF16),
        ],
        compiler_params=_cparams("parallel"),
        name="in_proj",
    )(*xs, mod, g_mix, wut, win, qg, kvg, wq1, wq2, wkk, wv, cq_t, sq_t, cs_t)


def _attn_kernel(*refs, latent_pieces, tq):
    if latent_pieces:
        q_ref, k_ref, vt_ref, kc_ref, vtc_ref, o_ref, s_scr = refs
    else:
        q_ref, kc_ref, vtc_ref, _, o_ref, s_scr = refs
    heads = [slice(hh * HEAD_PAD, (hh + 1) * HEAD_PAD) for hh in range(2)]
    vrows = [slice(hh * VT_ROWS, (hh + 1) * VT_ROWS) for hh in range(2)]
    qs = [q_ref[:, sl] for sl in heads]

    def put_scores(slot, k_at, width):
        for hh in range(2):
            s_scr[slot, hh, :width, :] = lax.dot_general(
                k_at(heads[hh]), qs[hh], NT_DIMS, preferred_element_type=F32)

    def consume(carry, slot, vt_at, width):
        new = []
        for hh in range(2):
            m, acc = carry[hh]
            s = s_scr[slot, hh, :width, :]
            m_new = jnp.maximum(m, jnp.max(s, axis=0, keepdims=True))
            alpha = jnp.exp2(m - m_new)
            p = jnp.exp2(s - m_new).astype(BF16)
            acc = alpha * acc + jnp.dot(vt_at(vrows[hh]), p, preferred_element_type=F32)
            new.append((m_new, acc))
        return tuple(new)

    def piece(kr, vr, start, size):
        return (lambda sl: kr[start:start + size, sl]), (lambda rw: vr[rw, start:start + size]), size

    pieces = [piece(kc_ref, vtc_ref, 0, CTX_LEN)]
    start = 0
    for size in latent_pieces:
        pieces.append(piece(k_ref, vt_ref, start, size))
        start += size
    carry = tuple((jnp.full((1, tq), -jnp.inf, F32), jnp.zeros((VT_ROWS, tq), F32))
                  for _ in range(2))
    put_scores(0, pieces[0][0], pieces[0][2])
    for i, (_, vt_at, size) in enumerate(pieces):
        if i + 1 < len(pieces):
            put_scores((i + 1) % 2, pieces[i + 1][0], pieces[i + 1][2])
        carry = consume(carry, i % 2, vt_at, size)
    for hh, (_, acc) in enumerate(carry):
        o_ref[hh * V_HEAD:(hh + 1) * V_HEAD, :] = (
            acc[:V_HEAD] / acc[V_HEAD:V_HEAD + 1]).astype(o_ref.dtype)


def _attn_latent_call(q, k, vt):
    qt = SEQ // TQ
    ctx0 = N_LAT // CTX_LEN
    return pl.pallas_call(
        functools.partial(_attn_kernel, latent_pieces=KEY_PIECES, tq=TQ),
        grid=(BATCH, N_HEADS // 2, qt),
        in_specs=[
            pl.BlockSpec((TQ, 2 * HEAD_PAD), lambda b, h, i: (b * qt + i, h)),
            pl.BlockSpec((SEQ, 2 * HEAD_PAD), lambda b, h, i: (b, h)),
            pl.BlockSpec((2 * VT_ROWS, SEQ), lambda b, h, i: (h, b)),
            pl.BlockSpec((CTX_LEN, 2 * HEAD_PAD), lambda b, h, i: (ctx0 + b, h)),
            pl.BlockSpec((2 * VT_ROWS, CTX_LEN), lambda b, h, i: (h, ctx0 + b)),
        ],
        out_specs=pl.BlockSpec((2 * V_HEAD, TQ), lambda b, h, i: (h, b * qt + i)),
        out_shape=jax.ShapeDtypeStruct((D_ATT, N_TOT), BF16),
        scratch_shapes=[pltpu.VMEM((2, 2, max(KEY_PIECES), TQ), F32)],
        compiler_params=_cparams("parallel", "parallel", "arbitrary"),
        name="attn_latent",
    )(q, k, vt, k, vt)


def _attn_ctx_call(q, k, vt, y_att):
    ctx0 = N_LAT // CTX_LEN
    return pl.pallas_call(
        functools.partial(_attn_kernel, latent_pieces=(), tq=CTX_LEN),
        grid=(BATCH, N_HEADS // 2),
        in_specs=[
            pl.BlockSpec((CTX_LEN, 2 * HEAD_PAD), lambda b, h: (ctx0 + b, h)),
            pl.BlockSpec((CTX_LEN, 2 * HEAD_PAD), lambda b, h: (ctx0 + b, h)),
            pl.BlockSpec((2 * VT_ROWS, CTX_LEN), lambda b, h: (h, ctx0 + b)),
            pl.BlockSpec(memory_space=pl.ANY),
        ],
        out_specs=pl.BlockSpec((2 * V_HEAD, CTX_LEN), lambda b, h: (h, ctx0 + b)),
        out_shape=jax.ShapeDtypeStruct((D_ATT, N_TOT), BF16),
        input_output_aliases={3: 0},
        scratch_shapes=[pltpu.VMEM((1, 2, CTX_LEN, CTX_LEN), F32)],
        compiler_params=_cparams("parallel", "parallel"),
        name="attn_ctx",
    )(q, k, vt, y_att)


SSM_C = 64
SSM_HALVES = SSM_T // SSM_C
SSM_KC = SSM_GROUP * SSM_C
LAG_ROWS = 2 * SSM_C
PT_LAG, PT_INC, PT_OUT = 0, LAG_ROWS, LAG_ROWS + SSM_C
PT_ROWS = LAG_ROWS + 2 * SSM_C


def _ssm_kernel(u_ref, pta_ref, ptb_ref, rows_ref, cc_ref, at_ref, dv_ref, y_ref,
                abt_hi_scr, abt_lo_scr, wl_scr, m_scr, ws_scr, wct_scr, s_scr, h_scr):
    H, C = SSM_GROUP, SSM_C
    half_lane = lax.broadcasted_iota(jnp.int32, (1, LANES), 1) < C

    def split(v):
        hi = v.astype(BF16)
        return hi, (v - hi.astype(F32)).astype(BF16)

    def scaled(row0, n_rows, ra, rb):
        return pta_ref[0, row0:row0 + n_rows, :] * ra + ptb_ref[0, row0:row0 + n_rows, :] * rb

    def build_tables(i, _):
        row0 = pl.multiple_of(i * C, C)
        b_re = rows_ref[0, pl.ds(i, 1), :]
        b_im = rows_ref[0, pl.ds(H + i, 1), :]
        lag_rows = pl.ds(pl.multiple_of(i * LAG_ROWS, LAG_ROWS), LAG_ROWS)
        abt_hi_scr[lag_rows, :], abt_lo_scr[lag_rows, :] = split(
            scaled(PT_LAG, LAG_ROWS, b_re, b_im))
        ws_scr[pl.ds(row0, C), :] = scaled(PT_INC, C, b_re, b_im).astype(BF16)
        c_re = rows_ref[0, pl.ds(2 * H + i, 1), :]
        c_im = rows_ref[0, pl.ds(3 * H + i, 1), :]
        wct_scr[pl.ds(row0, C), :] = scaled(PT_OUT, C, c_re, c_im).astype(BF16)
        return 0

    lax.fori_loop(0, H, build_tables, 0)
    cc_hi, cc_lo = split(cc_ref[0])
    nt = lambda a, b: lax.dot_general(a, b, NT_DIMS, preferred_element_type=F32)
    wl = nt(cc_hi, abt_hi_scr[...]) + nt(cc_hi, abt_lo_scr[...]) + nt(cc_lo, abt_hi_scr[...])
    wl_scr[0] = wl
    wl_scr[1] = jnp.concatenate(
        [pltpu.roll(wl[:, ci * LAG_ROWS:(ci + 1) * LAG_ROWS], C, 1) for ci in range(H)], axis=1)

    slot2 = lax.broadcasted_iota(jnp.int32, (1, 2 * LANES), 1)
    first_low = (slot2 < C) | (slot2 > 2 * LANES - C)
    first_high = slot2 < LANES

    def toeplitz_pair(x):
        return pltpu.roll(jnp.broadcast_to(x, (C, 2 * LANES)), 0, 1, stride=1, stride_axis=0)

    def build_toeplitz(ci, _):
        row0 = pl.multiple_of(ci * C, C)
        slots = pl.ds(pl.multiple_of(ci * LAG_ROWS, LAG_ROWS), LAG_ROWS)

        def lags(co, shifted):
            v = wl_scr[int(shifted), co:co + 1, slots]
            return jnp.concatenate([v, v], axis=1)

        for ka in range(0, H // 2, 2):
            kb = ka + 1
            low = toeplitz_pair(jnp.where(first_low, lags(2 * ka, False), lags(2 * kb, False)))
            high = toeplitz_pair(jnp.where(first_high, lags(2 * ka + 1, True),
                                           lags(2 * kb + 1, True)))
            for k, lanes in ((ka, slice(0, LANES)), (kb, slice(LANES, 2 * LANES))):
                m_scr[pl.ds(row0, C), k * LANES:(k + 1) * LANES] = jnp.where(
                    half_lane, low[:, lanes], high[:, lanes]).astype(BF16)
        return 0

    lax.fori_loop(0, H, build_toeplitz, 0)

    def chunk_operand(hf):
        cols = []
        for k in range(H // 2):
            a, b = u_ref[2 * k], u_ref[2 * k + 1]
            if hf == 0:
                cols.append(jnp.where(half_lane, a, pltpu.roll(b, C, 1)))
            else:
                cols.append(jnp.where(half_lane, pltpu.roll(a, C, 1), b))
        return jnp.concatenate(cols, axis=1).astype(BF16)

    ys = []
    for hf in range(SSM_HALVES):
        u = chunk_operand(hf)
        ys.append(jnp.dot(u, m_scr[...], preferred_element_type=F32))
        s = jnp.dot(u, ws_scr[...], preferred_element_type=F32)
        for d in range(2):
            s_d = s[:, d * LANES:(d + 1) * LANES]
            s_scr[hf, d] = s_d
            s_scr[hf, 2 + d] = pltpu.roll(s_d, SSM_STATE, 1)

    n_lat, n_ctx = SEQ // SSM_T, CTX_LEN // SSM_T
    ctx = [((N_CHUNK_LAT + c, n_ctx), hf) for c in range(n_ctx) for hf in range(SSM_HALVES)]
    lat = [((k, n_lat), hf) for k in range(n_lat) for hf in range(SSM_HALVES)]
    coef = [jnp.broadcast_to(at_ref[0, r:r + 1, :], (BATCH, 2 * SSM_STATE)) for r in range(4)]

    def advance(h, h_sw, d, rows, hf):
        a0, a1 = coef[2 * d], coef[2 * d + 1]
        return (h * a0 + h_sw * a1 + s_scr[hf, d, rows, :],
                h_sw * a0 - h * a1 + s_scr[hf, 2 + d, rows, :])

    zero = jnp.zeros((BATCH, 2 * SSM_STATE), F32)
    h_f, h_f_sw, h_r, h_r_sw = zero, zero, zero, zero
    for ((sf, stf), cf), ((sr, strd), cr) in zip(ctx + lat, ctx[::-1] + lat[::-1]):
        rows_f = pl.ds(sf, BATCH, stride=stf)
        rows_r = pl.ds(sr, BATCH, stride=strd)
        h_scr[cf, 0, rows_f, :] = h_f
        h_f, h_f_sw = advance(h_f, h_f_sw, 0, rows_f, cf)
        h_scr[cr, 1, rows_r, :] = h_r
        h_r, h_r_sw = advance(h_r, h_r_sw, 1, rows_r, cr)

    for hf in range(SSM_HALVES):
        h_in = jnp.concatenate([h_scr[hf, 0], h_scr[hf, 1]], axis=-1).astype(BF16)
        ys[hf] = ys[hf] + lax.dot_general(h_in, wct_scr[...], NT_DIMS,
                                          preferred_element_type=F32)
    for k in range(H // 2):
        y0, y1 = ys[0][:, k * LANES:(k + 1) * LANES], ys[1][:, k * LANES:(k + 1) * LANES]
        for c, yc in ((2 * k, jnp.where(half_lane, y0, pltpu.roll(y1, C, 1))),
                      (2 * k + 1, jnp.where(half_lane, pltpu.roll(y0, C, 1), y1))):
            y_ref[c] = yc + u_ref[c] * dv_ref[0, :, c * SSM_T:(c + 1) * SSM_T]


def _ssm_call(u_t, pta, ptb, rows, cc, at, dv):
    g3 = lambda g: (g, 0, 0)
    return pl.pallas_call(
        _ssm_kernel,
        grid=(N_SSM_GROUPS,),
        in_specs=[
            pl.BlockSpec((SSM_GROUP, N_CHUNK, SSM_T), g3),
            pl.BlockSpec((1, PT_ROWS, 4 * SSM_STATE), g3),
            pl.BlockSpec((1, PT_ROWS, 4 * SSM_STATE), g3),
            pl.BlockSpec((1, 4 * SSM_GROUP, 4 * SSM_STATE), g3),
            pl.BlockSpec((1, SSM_GROUP, 4 * SSM_STATE), g3),
            pl.BlockSpec((1, 4, 2 * SSM_STATE), g3),
            pl.BlockSpec((1, 1, SSM_K), g3),
        ],
        out_specs=pl.BlockSpec((SSM_GROUP, N_CHUNK, SSM_T), g3),
        out_shape=jax.ShapeDtypeStruct((D_SSM, N_CHUNK, SSM_T), F32),
        scratch_shapes=[
            pltpu.VMEM((SSM_GROUP * LAG_ROWS, 4 * SSM_STATE), BF16),
            pltpu.VMEM((SSM_GROUP * LAG_ROWS, 4 * SSM_STATE), BF16),
            pltpu.VMEM((2, SSM_GROUP, SSM_GROUP * LAG_ROWS), F32),
            pltpu.VMEM((SSM_KC, SSM_KC), BF16),
            pltpu.VMEM((SSM_KC, 4 * SSM_STATE), BF16),
            pltpu.VMEM((SSM_KC, 4 * SSM_STATE), BF16),
            pltpu.VMEM((SSM_HALVES, 4, N_CHUNK, 2 * SSM_STATE), F32),
            pltpu.VMEM((SSM_HALVES, 2, N_CHUNK, 2 * SSM_STATE), F32),
        ],
        compiler_params=_cparams("parallel"),
        name="s5_mixer",
    )(u_t, pta, ptb, rows, cc, at, dv)


def _ssm_tables(a_re, a_im, log_dt, b_re, b_im, c_re, c_im, d_skip):
    G, P, H, T = N_SSM_GROUPS, SSM_STATE, SSM_GROUP, SSM_T
    a_re, a_im = a_re.astype(F32), a_im.astype(F32)
    dt = jnp.exp(log_dt.astype(F32))[..., None]
    den = a_re * a_re + a_im * a_im
    mag1 = jnp.exp(dt * a_re)
    ab_re, ab_im = mag1 * jnp.cos(dt * a_im), mag1 * jnp.sin(dt * a_im)
    num_re = ab_re - 1.0
    f_re = (num_re * a_re + ab_im * a_im) / den
    f_im = (ab_im * a_re - num_re * a_im) / den
    b_re, b_im = b_re.astype(F32), b_im.astype(F32)
    bb_re = f_re[..., None] * b_re - f_im[..., None] * b_im
    bb_im = f_re[..., None] * b_im + f_im[..., None] * b_re
    c_re, c_im = c_re.astype(F32), c_im.astype(F32)
    la, th = dt * a_re, dt * a_im

    def powers(d, n):
        mag = jnp.exp(la[d][:, None, :] * n[None, :, None])
        ph = th[d][:, None, :] * n[None, :, None]
        return mag * jnp.cos(ph), mag * jnp.sin(ph)

    C = SSM_C
    slot = np.arange(LAG_ROWS)
    lag_f = np.where(slot < C, slot, -1)
    lag_r = np.where((-slot) % LAG_ROWS < C, (-slot) % LAG_ROWS, -1)
    s_idx = np.arange(C)
    f_all = np.concatenate([lag_f, C - 1 - s_idx, s_idx + 1])
    r_all = np.concatenate([lag_r, s_idx, C - s_idx])
    f_exp, r_exp = jnp.asarray(np.maximum(f_all, 0), F32), jnp.asarray(np.maximum(r_all, 0), F32)
    f_on, r_on = jnp.asarray(f_all >= 0, F32), jnp.asarray(r_all >= 0, F32)
    lane = jnp.arange(4 * P)
    is_fwd = (lane < 2 * P)[None, :]
    is_re = ((lane // P) % 2 == 0)[None, None, :]
    expo = jnp.where(is_fwd, f_exp[:, None], r_exp[:, None])[None]
    on = jnp.where(is_fwd, f_on[:, None], r_on[:, None])[None]
    la4 = jnp.concatenate([la[0], la[0], la[1], la[1]], axis=-1)[:, None, :]
    th4 = jnp.concatenate([th[0], th[0], th[1], th[1]], axis=-1)[:, None, :]
    mag = jnp.exp(la4 * expo) * on
    p_re, p_im = mag * jnp.cos(th4 * expo), mag * jnp.sin(th4 * expo)
    pta = jnp.where(is_re, p_re, p_im)
    ptb = jnp.where(is_re, -p_im, p_re)

    def per_channel(v):
        f, r = v[0].transpose(0, 2, 1), v[1].transpose(0, 2, 1)
        return jnp.concatenate([f, f, r, r], axis=-1)
    cf_re, cr_re = c_re[0], c_re[1]
    cf_im, cr_im = c_im[0], c_im[1]
    rows = jnp.concatenate([
        per_channel(bb_re), per_channel(bb_im),
        jnp.concatenate([cf_re, -cf_re, cr_re, -cr_re], axis=-1),
        jnp.concatenate([cf_im, -cf_im, cr_im, -cr_im], axis=-1)], axis=1)
    cc = jnp.concatenate([cf_re, -cf_im, cr_re, -cr_im], axis=-1)

    t_exp = jnp.full((1,), float(SSM_C), F32)
    (f_re_t, f_im_t), (r_re_t, r_im_t) = powers(0, t_exp), powers(1, t_exp)
    f_re_t, f_im_t, r_re_t, r_im_t = (v[:, 0, :] for v in (f_re_t, f_im_t, r_re_t, r_im_t))
    at = jnp.stack([jnp.concatenate([f_re_t, f_re_t], -1), jnp.concatenate([-f_im_t, f_im_t], -1),
                    jnp.concatenate([r_re_t, r_re_t], -1), jnp.concatenate([-r_im_t, r_im_t], -1)],
                   axis=1)
    dv = jnp.repeat(d_skip.astype(F32).reshape(G, H), T, axis=-1).reshape(G, 1, H * T)
    return pta, ptb, rows, cc, at, dv


ROUTE_E1, ROUTE_E2, ROUTE_W1, ROUTE_W2 = 0, 1, 2, 3


def _top2_route(logits):
    lane = lax.broadcasted_iota(jnp.int32, logits.shape, 1)
    lg = jnp.where(lane < N_EXPERTS, logits, -jnp.inf)
    m1 = jnp.max(lg, axis=-1, keepdims=True)
    i1 = jnp.min(jnp.where(lg == m1, lane, LANES), axis=-1, keepdims=True)
    lg2 = jnp.where(lane == i1, -jnp.inf, lg)
    m2 = jnp.max(lg2, axis=-1, keepdims=True)
    i2 = jnp.min(jnp.where(lg2 == m2, lane, LANES), axis=-1, keepdims=True)
    e2 = jnp.exp(m2 - m1)
    w1 = 1.0 / (1.0 + e2)
    rec = jnp.where(lane == ROUTE_E1, i1.astype(F32), 0.0)
    rec = jnp.where(lane == ROUTE_E2, i2.astype(F32), rec)
    rec = jnp.where(lane == ROUTE_W1, w1, rec)
    return jnp.where(lane == ROUTE_W2, e2 * w1, rec), i1, i2


def _expert_slots(i1, i2, taken):
    n = i1.shape[0]
    lane = lax.broadcasted_iota(jnp.int32, (n, LANES), 1)
    picked = jnp.where((lane == i1) | (lane == i2), 1.0, 0.0)
    earlier = (lax.broadcasted_iota(jnp.int32, (n, n), 1)
               < lax.broadcasted_iota(jnp.int32, (n, n), 0)).astype(BF16)
    rank = jnp.dot(earlier, picked.astype(BF16), preferred_element_type=F32)
    slot = rank + taken + (lane * EXPERT_CAP).astype(F32)
    d1 = jnp.sum(jnp.where(lane == i1, slot, 0.0), axis=-1, keepdims=True)
    d2 = jnp.sum(jnp.where(lane == i2, slot, 0.0), axis=-1, keepdims=True)
    dest = jnp.where(lane == 0, d1, jnp.where(lane == 1, d2, 0.0)).astype(jnp.int32)
    return dest, jnp.sum(picked, axis=0, keepdims=True)


def _pack_bf16_pairs(v):
    k = v.shape[1] // 2
    bits = pltpu.bitcast(v.astype(BF16).astype(F32), jnp.uint32)
    return (bits[:, :k] & jnp.uint32(0xFFFF0000)) | (bits[:, k:] >> 16)


def _unpack_bf16_pairs(w):
    hi = pltpu.bitcast(w & jnp.uint32(0xFFFF0000), F32)
    lo = pltpu.bitcast(w << 16, F32)
    return hi, lo


def _store_packed(ref, v):
    words = _pack_bf16_pairs(v)
    for s in range(SC_SPLIT):
        ref[s] = words[:, s * SC_ROW:(s + 1) * SC_ROW]


def _load_packed(ref):
    return _unpack_bf16_pairs(jnp.concatenate([ref[s] for s in range(SC_SPLIT)], axis=-1))


def _mix_kernel(*refs, with_router, n_x):
    rest = refs[n_x:]
    if with_router:
        (yst_ref, ya_ref, mod_ref, wglut_ref, bglu_ref, wout_ref, gffn_ref, router_ref,
         x1_ref, h2_ref, gate_ref, dest_ref, cnt_ref, taken_scr) = rest
    else:
        (yst_ref, ya_ref, mod_ref, wglut_ref, bglu_ref, wout_ref, gffn_ref,
         w1_ref, w3_ref, w2_ref, x2_ref) = rest
    zt = jax.nn.gelu(yst_ref[...], approximate=True)
    glt = zt * jax.nn.sigmoid(
        jnp.dot(wglut_ref[...], zt.astype(BF16), preferred_element_type=F32) + bglu_ref[...])
    mix = jnp.concatenate([glt, ya_ref[...].astype(F32)], axis=0).T.astype(BF16)
    o = jnp.dot(mix, wout_ref[...], preferred_element_type=F32)
    x1 = _tile_rows(refs, n_x) + mod_ref[0, 2:3, :] * o
    h2 = _rms(x1, gffn_ref[...]) * (1.0 + mod_ref[0, 4:5, :]) + mod_ref[0, 3:4, :]
    if not with_router:
        acc = _swiglu(h2.astype(BF16), lambda sl: w1_ref[:, sl], lambda sl: w3_ref[:, sl],
                      lambda sl: w2_ref[sl, :], D_FF)
        x2_ref[...] = x1 + mod_ref[0, 5:6, :] * acc
    else:
        x1_ref[...] = x1
        _store_packed(h2_ref, h2)
        h_hi = h2.astype(BF16)
        h_lo = (h2 - h_hi.astype(F32)).astype(BF16)
        o1 = jnp.dot(h_hi, router_ref[0], preferred_element_type=F32)
        o2 = jnp.dot(h_lo, router_ref[1], preferred_element_type=F32)
        logits = o1 + pltpu.roll(o1, LANES - N_EXPERTS, 1) + o2
        gate_ref[...], i1, i2 = _top2_route(logits)

        @pl.when(pl.program_id(0) == 0)
        def _():
            taken_scr[...] = jnp.zeros_like(taken_scr)

        dest_ref[...], tile_cnt = _expert_slots(i1, i2, taken_scr[...])
        taken_scr[...] += tile_cnt
        cnt_ref[...] = jnp.broadcast_to(taken_scr[...], cnt_ref.shape)


def _mix_call(xs, y_ssm_t, y_att, mod, wglut, bglu, wout, gffn, *, router=None, ffn=None, n_tiles):
    row = lambda i: (i, 0)
    out_row = row
    with_router = router is not None
    assert with_router != (ffn is not None)
    in_specs = _x_specs(len(xs)) + [
        pl.BlockSpec((D_SSM, TM), lambda i: (0, i)),
        pl.BlockSpec((D_ATT, TM), lambda i: (0, i)),
        pl.BlockSpec((1, 6, D_MODEL), lambda i: (_mod_row(i), 0, 0)),
        _const_spec((D_SSM, D_SSM)),
        _const_spec((D_SSM, 1)),
        _const_spec((D_MIX, D_MODEL)),
        _const_spec((1, D_MODEL)),
    ]
    args = [*xs, y_ssm_t, y_att, mod, wglut, bglu, wout, gffn]
    out_specs = [pl.BlockSpec((TM, D_MODEL), out_row)]
    out_shape = [jax.ShapeDtypeStruct((n_tiles * TM, D_MODEL), F32)]
    if with_router:
        in_specs.append(_const_spec((2, D_MODEL, LANES)))
        args.append(router)
        out_specs += [pl.BlockSpec((SC_SPLIT, TM, SC_ROW), lambda i: (0, i, 0)),
                      pl.BlockSpec((TM, LANES), out_row),
                      pl.BlockSpec((TM, LANES), out_row),
                      pl.BlockSpec((8, LANES), lambda i: (0, 0))]
        out_shape += [jax.ShapeDtypeStruct((SC_SPLIT, n_tiles * TM, SC_ROW), jnp.uint32),
                      jax.ShapeDtypeStruct((n_tiles * TM, LANES), F32),
                      jax.ShapeDtypeStruct((n_tiles * TM, LANES), jnp.int32),
                      jax.ShapeDtypeStruct((8, LANES), F32)]
        scratch = [pltpu.VMEM((1, LANES), F32)]
    else:
        in_specs += [_const_spec((D_MODEL, D_FF)), _const_spec((D_MODEL, D_FF)),
                     _const_spec((D_FF, D_MODEL))]
        args += list(ffn)
        scratch = []
    return pl.pallas_call(
        functools.partial(_mix_kernel, with_router=with_router, n_x=len(xs)),
        grid=(n_tiles,),
        in_specs=in_specs,
        out_specs=out_specs,
        out_shape=out_shape,
        scratch_shapes=scratch,
        compiler_params=_cparams("arbitrary" if with_router else "parallel"),
        name="mix_out",
    )(*args)


def _swiglu(h, w1_at, w3_at, w2_at, d_ff):
    acc = jnp.zeros((h.shape[0], D_MODEL), F32)
    for lo in range(0, d_ff, FF_CHUNK):
        sl = slice(lo, min(lo + FF_CHUNK, d_ff))
        a = jnp.dot(h, w1_at(sl), preferred_element_type=F32)
        b = jnp.dot(h, w3_at(sl), preferred_element_type=F32)
        g = (a * jax.nn.sigmoid(a) * b).astype(BF16)
        acc = acc + jnp.dot(g, w2_at(sl), preferred_element_type=F32)
    return acc


EXPERT_BLK = 512
EXPERT_CAP = N_LAT
N_SORTED = N_EXPERTS * EXPERT_CAP
N_EXPERT_BLKS = 2 * N_LAT // EXPERT_BLK + N_EXPERTS
PACKED = D_MODEL // 2
SC_ROW = 256
SC_SPLIT = PACKED // SC_ROW
SC_WIN = 128


def _sc_mesh():
    return plsc.VectorSubcoreMesh(core_axis_name="core", subcore_axis_name="subcore")


def _sc_scatter(x, idx_a, idx_b, n_out):
    n = x.shape[0]

    @pl.kernel(out_type=jax.ShapeDtypeStruct((n_out, SC_ROW), x.dtype), mesh=_sc_mesh(),
               scratch_types=[])
    def scatter(x_hbm, a_hbm, b_hbm, o_hbm):
        def body(x_vmem, a_vmem, b_vmem):
            pltpu.sync_copy(x_vmem, o_hbm.at[a_vmem.at[0]])
            pltpu.sync_copy(x_vmem, o_hbm.at[b_vmem.at[0]])

        pltpu.emit_pipeline(
            body, grid=(n // SC_WIN,),
            in_specs=[pl.BlockSpec((SC_WIN, SC_ROW), lambda i: (i, 0)),
                      pl.BlockSpec((1, SC_WIN), lambda i: (0, i)),
                      pl.BlockSpec((1, SC_WIN), lambda i: (0, i))],
            out_specs=[],
            core_axis_name=("core", "subcore"),
            dimension_semantics=(pltpu.PARALLEL,),
        )(x_hbm, a_hbm, b_hbm)

    return scatter(x, idx_a.reshape(1, n), idx_b.reshape(1, n))


def _sc_gather(y, idx):
    n = idx.shape[0]

    @pl.kernel(out_type=jax.ShapeDtypeStruct((n, SC_ROW), y.dtype), mesh=_sc_mesh(),
               scratch_types=[])
    def gather(y_hbm, i_hbm, o_hbm):
        def body(i_vmem, o_vmem):
            pltpu.sync_copy(y_hbm.at[i_vmem.at[0]], o_vmem)

        pltpu.emit_pipeline(
            body, grid=(n // SC_WIN,),
            in_specs=[pl.BlockSpec((1, SC_WIN), lambda i: (0, i))],
            out_specs=[pl.BlockSpec((SC_WIN, SC_ROW), lambda i: (i, 0))],
            core_axis_name=("core", "subcore"),
            dimension_semantics=(pltpu.PARALLEL,),
        )(i_hbm, o_hbm)

    return gather(y, idx.reshape(1, n))


W_CAST_ROWS = 128


def _experts_kernel(blk_expert_ref, blk_row_ref, n_used_ref, x_ref, w1_ref, w3_ref, w2_ref, o_ref,
                    w1_scr, w3_scr, w2_scr):
    del blk_row_ref
    b = pl.program_id(0)
    live = b < n_used_ref[0]
    new_expert = (b == 0) | (blk_expert_ref[b] != blk_expert_ref[jnp.maximum(b - 1, 0)])

    @pl.when(live & new_expert)
    def _():
        for src, dst in ((w1_ref, w1_scr), (w3_ref, w3_scr), (w2_ref, w2_scr)):
            for r in range(0, dst.shape[0], W_CAST_ROWS):
                dst[r:r + W_CAST_ROWS, :] = src[0, r:r + W_CAST_ROWS, :].astype(BF16)

    @pl.when(live)
    def _():
        hi, lo = _load_packed(x_ref)
        h = jnp.concatenate([hi.astype(BF16), lo.astype(BF16)], axis=-1)
        y = _swiglu(h, lambda sl: w1_scr[:, sl], lambda sl: w3_scr[:, sl],
                    lambda sl: w2_scr[sl, :], D_FF_EXPERT)
        _store_packed(o_ref, y)


def _experts_call(blk_expert, blk_row, n_used, xs, w1, w3, w2):
    row = lambda b, be, br, nu: (0, br[b], 0)
    wsel = lambda b, be, br, nu: (be[b], 0, 0)
    return pl.pallas_call(
        _experts_kernel,
        grid_spec=pltpu.PrefetchScalarGridSpec(
            num_scalar_prefetch=3,
            grid=(N_EXPERT_BLKS,),
            in_specs=[
                pl.BlockSpec((SC_SPLIT, EXPERT_BLK, SC_ROW), row),
                pl.BlockSpec((1, D_MODEL, D_FF_EXPERT), wsel),
                pl.BlockSpec((1, D_MODEL, D_FF_EXPERT), wsel),
                pl.BlockSpec((1, D_FF_EXPERT, D_MODEL), wsel),
            ],
            out_specs=pl.BlockSpec((SC_SPLIT, EXPERT_BLK, SC_ROW), row),
            scratch_shapes=[pltpu.VMEM((D_MODEL, D_FF_EXPERT), BF16),
                            pltpu.VMEM((D_MODEL, D_FF_EXPERT), BF16),
                            pltpu.VMEM((D_FF_EXPERT, D_MODEL), BF16)],
        ),
        out_shape=jax.ShapeDtypeStruct((SC_SPLIT, N_SORTED, SC_ROW), jnp.uint32),
        compiler_params=_cparams("arbitrary"),
        name="moe_experts",
    )(blk_expert, blk_row, n_used, xs, w1, w3, w2)


def _combine_kernel(x1_ref, r_ref, y_ref, mod_ref, fg_ref, *rest):
    o_ref = rest[-1]

    def expert_out(slot):
        hi, lo = _load_packed(y_ref.at[slot])
        return jnp.concatenate([hi, lo], axis=-1)
    w1 = r_ref[:, ROUTE_W1:ROUTE_W1 + 1]
    w2 = r_ref[:, ROUTE_W2:ROUTE_W2 + 1]
    y = w1 * expert_out(0) + w2 * expert_out(1)
    x2 = x1_ref[...] + mod_ref[0, 5:6, :] * y
    o_ref[...] = _rms(x2, fg_ref[...])


COMBINE_PARTS = 2


def _combine_call(x1, route, y_tok, mod, fg, part, out):
    n_tiles = N_LAT // TM // COMBINE_PARTS
    tile0 = part * n_tiles
    row = lambda i: (tile0 + i, 0)
    in_specs = [
        pl.BlockSpec((TM, D_MODEL), row),
        pl.BlockSpec((TM, LANES), row),
        pl.BlockSpec((2, SC_SPLIT, TM, SC_ROW), lambda i: (0, 0, i, 0)),
        pl.BlockSpec((1, 6, D_MODEL), lambda i: ((tile0 + i) // SEQ_TILES, 0, 0)),
        pl.BlockSpec((1, D_MODEL), lambda i: (0, 0)),
    ]
    args = [x1, route, y_tok, mod, fg]
    aliases = {}
    if out is not None:
        in_specs.append(pl.BlockSpec(memory_space=pl.ANY))
        args.append(out)
        aliases = {len(args) - 1: 0}
    return pl.pallas_call(
        _combine_kernel,
        grid=(n_tiles,),
        in_specs=in_specs,
        out_specs=pl.BlockSpec((TM, D_MODEL), row),
        out_shape=jax.ShapeDtypeStruct((N_LAT, D_MODEL), F32),
        input_output_aliases=aliases,
        compiler_params=_cparams("parallel"),
        name="moe_combine",
    )(*args)


def _moe_routed(h2p, x1, route, dest, cnt, mod, w1, w3, w2, fg):
    blks = (cnt[0, :N_EXPERTS].astype(jnp.int32) + (EXPERT_BLK - 1)) // EXPERT_BLK
    blk_end = jnp.cumsum(blks)
    n_used = blk_end[-1:]
    b = jnp.minimum(jnp.arange(N_EXPERT_BLKS, dtype=jnp.int32), n_used[0] - 1)
    blk_expert = jnp.sum((b[:, None] >= blk_end[None, :]).astype(jnp.int32), axis=1)
    blk_row = blk_expert * (EXPERT_CAP // EXPERT_BLK) + b - (blk_end - blks)[blk_expert]
    piece = jnp.arange(SC_SPLIT, dtype=jnp.int32)[:, None] * N_SORTED
    idx = [(piece + dest[:, slot][None, :]).reshape(SC_SPLIT * N_LAT) for slot in range(2)]
    xs = _sc_scatter(h2p.reshape(SC_SPLIT * N_LAT, SC_ROW), idx[0], idx[1], SC_SPLIT * N_SORTED)
    ys = _experts_call(blk_expert, blk_row, n_used, xs.reshape(SC_SPLIT, N_SORTED, SC_ROW),
                       w1, w3, w2)
    ys_flat = ys.reshape(SC_SPLIT * N_SORTED, SC_ROW)
    n_part = N_LAT // COMBINE_PARTS
    out = None
    for part in range(COMBINE_PARTS):
        tokens = slice(part * n_part, (part + 1) * n_part)
        idx_part = jnp.concatenate(
            [(piece + dest[tokens, slot][None, :]).reshape(SC_SPLIT * n_part) for slot in range(2)])
        y_tok = _sc_gather(ys_flat, idx_part).reshape(2, SC_SPLIT, n_part, SC_ROW)
        out = _combine_call(x1, route, y_tok, mod, fg, part, out)
    return out


def _rope_partner_perm():
    perm, sign = [], []
    for j in range(QK_ROPE):
        first_half = (j % AXIS_ROPE) < ROPE_FREQS
        perm.append(j + ROPE_FREQS if first_half else j - ROPE_FREQS)
        sign.append(-1.0 if first_half else 1.0)
    return jnp.array(perm, jnp.int32), jnp.array(sign, F32)


def _rope_tables():
    t = jnp.arange(SEQ)
    row = (t // GRID_W).astype(F32)
    col = (t % GRID_W).astype(F32)
    inv_freq = ROPE_BASE ** (-2.0 * jnp.arange(ROPE_FREQS, dtype=F32) / AXIS_ROPE)
    ang = jnp.concatenate([row[:, None] * inv_freq, row[:, None] * inv_freq,
                           col[:, None] * inv_freq, col[:, None] * inv_freq], axis=1)
    cos = jnp.concatenate([jnp.cos(ang), jnp.ones((N_CTX, QK_ROPE), F32)], axis=0)
    sin = jnp.concatenate([jnp.sin(ang), jnp.zeros((N_CTX, QK_ROPE), F32)], axis=0)
    n = N_CTX + SEQ
    pad32 = jnp.zeros((n, HEAD_PAD - D_QK), F32)
    qs = ATT_SCALE * math.log2(math.e)
    cq = jnp.concatenate([jnp.full((n, QK_NOPE), qs, F32), qs * cos, pad32], axis=1)
    sq = jnp.concatenate([jnp.zeros((n, QK_NOPE), F32), qs * sin, pad32], axis=1)
    cs = jnp.concatenate([cos, sin, jnp.zeros((n, LANES - 2 * QK_ROPE), F32)], axis=1)
    return cq, sq, cs


def _layer_weights(w_in, w_uq, w_ukv):
    perm, sign = _rope_partner_perm()
    s0 = D_SSM + Q_LORA + KV_LORA
    kr_w = w_in[:, s0:s0 + QK_ROPE]
    wut = w_in[:, :D_SSM].T.astype(BF16)
    win = jnp.concatenate([w_in[:, D_SSM:s0], kr_w, kr_w[:, perm] * sign,
                           jnp.zeros((D_MODEL, LANES - 2 * QK_ROPE), F32)], axis=1).astype(BF16)
    uq = w_uq.reshape(Q_LORA, N_HEADS, D_QK)
    nope, rope = uq[..., :QK_NOPE], uq[..., QK_NOPE:]
    zpad = jnp.zeros((Q_LORA, N_HEADS, HEAD_PAD - D_QK), F32)
    wq1 = jnp.concatenate([nope, rope, zpad], axis=-1).reshape(Q_LORA, N_HEADS * HEAD_PAD).astype(BF16)
    wq2 = jnp.concatenate([jnp.zeros_like(nope), rope[..., perm] * sign, zpad], axis=-1)
    wq2 = wq2.reshape(Q_LORA, N_HEADS * HEAD_PAD).astype(BF16)
    ukv = w_ukv.reshape(KV_LORA, N_HEADS, QK_NOPE + V_HEAD)
    wk = jnp.concatenate([ukv[..., :QK_NOPE], jnp.zeros((KV_LORA, N_HEADS, HEAD_PAD - QK_NOPE), F32)],
                         axis=-1).reshape(KV_LORA, N_HEADS * HEAD_PAD)
    eye = jnp.eye(QK_ROPE, dtype=F32)
    place = jnp.concatenate([jnp.zeros((QK_ROPE, QK_NOPE), F32), eye,
                             jnp.zeros((QK_ROPE, HEAD_PAD - D_QK), F32)], axis=1)
    place = jnp.tile(place, (1, N_HEADS))
    spread = jnp.concatenate([place, place, jnp.zeros((LANES - 2 * QK_ROPE, N_HEADS * HEAD_PAD), F32)], 0)
    wkk = jnp.concatenate([wk, spread], axis=0).astype(BF16)
    wv = jnp.concatenate([ukv[..., QK_NOPE:], jnp.zeros((KV_LORA, N_HEADS, VT_ROWS - V_HEAD), F32)],
                         axis=-1).reshape(KV_LORA, N_HEADS * VT_ROWS).T.astype(BF16)
    return wut, win, wq1, wq2, wkk, wv


def kernel(x, c, ctx, c_ctx, w_ada, b_ada, norm_mix, norm_ffn, w_in, q_norm, kv_norm, w_uq, w_ukv,
           ssm_a_re, ssm_a_im, ssm_log_dt, ssm_b_re, ssm_b_im, ssm_c_re, ssm_c_im, ssm_d, w_glu,
           b_glu, w_out, ffn_w1, ffn_w3, ffn_w2, moe_router, moe_w1, moe_w3, moe_w2, final_norm):
    assert x.shape == (BATCH, SEQ, D_MODEL) and ctx.shape == (BATCH, CTX_LEN, D_MODEL)
    cond = jnp.concatenate([c, c_ctx[None, :], jnp.zeros((MOD_ROWS - BATCH - 1, D_MODEL), F32)], axis=0)
    mod_all = _ada_call(cond, w_ada, b_ada).reshape(DEPTH, MOD_ROWS, 6, D_MODEL)
    cq_t, sq_t, cs_t = _rope_tables()
    xs = (x.reshape(N_LAT, D_MODEL), ctx.reshape(N_CTX, D_MODEL))

    out = None
    for i in range(DEPTH):
        last = i == DEPTH - 1
        mod = mod_all[i]
        wut, win, wq1, wq2, wkk, wv = _layer_weights(w_in[i], w_uq[i], w_ukv[i])
        u_t, q, k, v = _inproj_call(xs, mod, norm_mix[i][None, :], wut, win, q_norm[i][None, :],
                                    kv_norm[i][None, :], wq1, wq2, wkk, wv, cq_t, sq_t, cs_t)
        tabs = _ssm_tables(ssm_a_re[i], ssm_a_im[i], ssm_log_dt[i], ssm_b_re[i], ssm_b_im[i],
                           ssm_c_re[i], ssm_c_im[i], ssm_d[i])
        y_ssm_t = _ssm_call(u_t.reshape(D_SSM, N_CHUNK, SSM_T), *tabs).reshape(D_SSM, N_TOT)
        y_att = _attn_latent_call(q, k, v)
        if last:
            n_tiles = LAT_TILES
        else:
            y_att = _attn_ctx_call(q, k, v, y_att)
            n_tiles = N_TOT // TM
        j = i // 2
        mix_args = (xs, y_ssm_t, y_att, mod, w_glu[i].T.astype(BF16), b_glu[i][:, None],
                    w_out[i].astype(BF16), norm_ffn[i][None, :])
        if i % 2 == 0:
            assert not last
            ffn = (ffn_w1[j].astype(BF16), ffn_w3[j].astype(BF16), ffn_w2[j].astype(BF16))
            xs = tuple(_mix_call(*mix_args, ffn=ffn, n_tiles=n_tiles))
        else:
            assert last
            r = moe_router[j]
            r_top = lax.bitcast_convert_type(
                lax.bitcast_convert_type(r, jnp.uint32) & jnp.uint32(0xFFFF0000), F32)
            r_hi = r_top.astype(BF16)
            r_lo = (r - r_top).astype(BF16)
            zr = jnp.zeros((D_MODEL, LANES - 2 * N_EXPERTS), BF16)
            router = jnp.stack([jnp.concatenate([r_hi, r_lo, zr], axis=1),
                                jnp.concatenate([r_hi, jnp.zeros_like(r_lo), zr], axis=1)])
            x1, h2p, route, dest, cnt = _mix_call(*mix_args, router=router, n_tiles=n_tiles)
            out = _moe_routed(h2p, x1, route, dest, cnt, mod, moe_w1[j], moe_w3[j], moe_w2[j],
                              final_norm[None, :])
    return out.reshape(BATCH, SEQ, D_MODEL)
```

```python
import functools
import math

import jax
import jax.numpy as jnp
import numpy as np
from jax import lax
from jax.experimental import pallas as pl
from jax.experimental.pallas import tpu as pltpu
from jax.experimental.pallas import tpu_sc as plsc

D_MODEL = 1024
BATCH = 4
SEQ = 8192
DEPTH = 2
GRID_W = 64
CTX_LEN = 256
D_SSM = 512
SSM_GROUP = 16
N_SSM_GROUPS = D_SSM // SSM_GROUP
SSM_STATE = 64
N_HEADS = 8
QK_NOPE = 64
QK_ROPE = 32
V_HEAD = 64
Q_LORA = 256
KV_LORA = 128
D_QK = QK_NOPE + QK_ROPE
D_ATT = N_HEADS * V_HEAD
D_MIX = D_SSM + D_ATT
AXIS_ROPE = QK_ROPE // 2
ROPE_FREQS = AXIS_ROPE // 2
ROPE_BASE = 10000.0
ATT_SCALE = 1.0 / math.sqrt(D_QK)
D_FF = 2816
N_EXPERTS = 8
D_FF_EXPERT = 1408
EPS = 1e-6

N_CTX = BATCH * CTX_LEN
N_LAT = BATCH * SEQ
N_TOT = N_CTX + N_LAT

LANES = 128
HEAD_PAD = 128
TM = 512
LAT_TILES = N_LAT // TM
SEQ_TILES = SEQ // TM
TQ = 512
KEY_PIECES = (2048, 2048, 2048, 2048)
assert sum(KEY_PIECES) == SEQ
SSM_T = 128
N_CHUNK_LAT = N_LAT // SSM_T
N_CHUNK = N_TOT // SSM_T
SSM_K = SSM_GROUP * SSM_T
FF_CHUNK = 256
D_IN_REST = Q_LORA + KV_LORA + LANES
MOD_ROWS = 8
VMEM_LIMIT = 52 * 1024 * 1024

F32 = jnp.float32
BF16 = jnp.bfloat16
HI = lax.Precision.HIGHEST


def _cparams(*sem):
    return pltpu.CompilerParams(dimension_semantics=sem, vmem_limit_bytes=VMEM_LIMIT)


def _const_spec(shape):
    nd = len(shape)
    return pl.BlockSpec(shape, lambda *_: (0,) * nd, pipeline_mode=pl.Buffered(1))


def _mod_row(i):
    return jnp.where(i < LAT_TILES, i // SEQ_TILES, BATCH)


def _pos_tile(i):
    return jnp.where(i < LAT_TILES, i % SEQ_TILES, SEQ_TILES + i - LAT_TILES)


def _rms(x, g):
    ms = jnp.mean(x * x, axis=-1, keepdims=True)
    return x * lax.rsqrt(ms + EPS) * g


ADA_TN = 1536


def _ada_kernel(c_ref, w_ref, b_ref, o_ref):
    c = c_ref[...]
    s = c * jax.nn.sigmoid(c)
    o_ref[0] = jnp.dot(s, w_ref[0], precision=HI, preferred_element_type=F32) + b_ref[0]


def _ada_call(cond, w_ada, b_ada):
    n_col = 6 * D_MODEL // ADA_TN
    return pl.pallas_call(
        _ada_kernel,
        grid=(DEPTH, n_col),
        in_specs=[
            pl.BlockSpec((MOD_ROWS, D_MODEL), lambda l, j: (0, 0)),
            pl.BlockSpec((1, D_MODEL, ADA_TN), lambda l, j: (l, 0, j)),
            pl.BlockSpec((1, 1, ADA_TN), lambda l, j: (l, 0, j)),
        ],
        out_specs=pl.BlockSpec((1, MOD_ROWS, ADA_TN), lambda l, j: (l, 0, j)),
        out_shape=jax.ShapeDtypeStruct((DEPTH, MOD_ROWS, 6 * D_MODEL), F32),
        compiler_params=_cparams("arbitrary", "arbitrary"),
        name="ada_mod",
    )(cond, w_ada, b_ada.reshape(DEPTH, 1, 6 * D_MODEL))


def _tile_rows(refs, n_x):
    if n_x == 1:
        return refs[0][...]
    return jnp.where(pl.program_id(0) < LAT_TILES, refs[0][...], refs[1][...])


def _x_specs(n_x):
    if n_x == 1:
        return [pl.BlockSpec((TM, D_MODEL), lambda i: (i, 0))]
    return [pl.BlockSpec((TM, D_MODEL), lambda i: (jnp.minimum(i, LAT_TILES - 1), 0)),
            pl.BlockSpec((TM, D_MODEL), lambda i: (jnp.maximum(i - LAT_TILES, 0), 0))]


def _inproj_kernel(*refs, n_x):
    (mod_ref, g_ref, wut_ref, win_ref, qg_ref, kvg_ref, wq1_ref, wq2_ref, wkk_ref, wv_ref,
     cq_ref, sq_ref, cs_ref, ut_ref, q_ref, k_ref, v_ref) = refs[n_x:]
    x = _tile_rows(refs, n_x)
    sh = mod_ref[0, 0:1, :]
    sc = mod_ref[0, 1:2, :]
    xm = (_rms(x, g_ref[...]) * (1.0 + sc) + sh).astype(BF16)
    ut_ref[...] = lax.dot_general(wut_ref[...], xm, (((1,), (1,)), ((), ())),
                                  preferred_element_type=F32)
    z = jnp.dot(xm, win_ref[...], preferred_element_type=F32)
    qn = _rms(z[:, :Q_LORA], qg_ref[...]).astype(BF16)
    kvn = _rms(z[:, Q_LORA:Q_LORA + KV_LORA], kvg_ref[...]).astype(BF16)
    krr = (z[:, Q_LORA + KV_LORA:] * cs_ref[...]).astype(BF16)
    q1 = jnp.dot(qn, wq1_ref[...], preferred_element_type=F32)
    q2 = jnp.dot(qn, wq2_ref[...], preferred_element_type=F32)
    cq = cq_ref[...]
    sq = sq_ref[...]
    for h in range(N_HEADS):
        sl = slice(h * HEAD_PAD, (h + 1) * HEAD_PAD)
        q_ref[:, sl] = (q1[:, sl] * cq + q2[:, sl] * sq).astype(q_ref.dtype)
    kin = jnp.concatenate([kvn, krr], axis=-1)
    k_ref[...] = jnp.dot(kin, wkk_ref[...], preferred_element_type=F32).astype(k_ref.dtype)
    vv = jnp.dot(kvn, wv_ref[...], preferred_element_type=F32)
    lane = lax.broadcasted_iota(jnp.int32, vv.shape, 1)
    v_ref[...] = jnp.where(lane % HEAD_PAD == V_HEAD, 1.0, vv).astype(v_ref.dtype)


def _inproj_call(xs, mod, g_mix, wut, win, qg, kvg, wq1, wq2, wkk, wv, cq_t, sq_t, cs_t):
    n_tiles = N_TOT // TM
    row = lambda i: (i, 0)
    pos = lambda i: (_pos_tile(i), 0)
    return pl.pallas_call(
        functools.partial(_inproj_kernel, n_x=len(xs)),
        grid=(n_tiles,),
        in_specs=_x_specs(len(xs)) + [
            pl.BlockSpec((1, 6, D_MODEL), lambda i: (_mod_row(i), 0, 0)),
            _const_spec((1, D_MODEL)),
            _const_spec((D_SSM, D_MODEL)),
            _const_spec((D_MODEL, D_IN_REST)),
            _const_spec((1, Q_LORA)),
            _const_spec((1, KV_LORA)),
            _const_spec((Q_LORA, N_HEADS * HEAD_PAD)),
            _const_spec((Q_LORA, N_HEADS * HEAD_PAD)),
            _const_spec((2 * KV_LORA, N_HEADS * HEAD_PAD)),
            _const_spec((KV_LORA, N_HEADS * HEAD_PAD)),
            pl.BlockSpec((TM, LANES), pos),
            pl.BlockSpec((TM, LANES), pos),
            pl.BlockSpec((TM, LANES), pos),
        ],
        out_specs=[
            pl.BlockSpec((D_SSM, TM), lambda i: (0, i)),
            pl.BlockSpec((TM, N_HEADS * HEAD_PAD), row),
            pl.BlockSpec((TM, N_HEADS * HEAD_PAD), row),
            pl.BlockSpec((TM, N_HEADS * HEAD_PAD), row),
        ],
        out_shape=[
            jax.ShapeDtypeStruct((D_SSM, N_TOT), F32),
            jax.ShapeDtypeStruct((N_TOT, N_HEADS * HEAD_PAD), BF16),
            jax.ShapeDtypeStruct((N_TOT, N_HEADS * HEAD_PAD), BF16),
            jax.ShapeDtypeStruct((N_TOT, N_HEADS * HEAD_PAD), BF16),
        ],
        compiler_params=_cparams("parallel"),
        name="in_proj",
    )(*xs, mod, g_mix, wut, win, qg, kvg, wq1, wq2, wkk, wv, cq_t, sq_t, cs_t)


def _attn_kernel(*refs, latent_pieces, tq):
    if latent_pieces:
        q_ref, k_ref, v_ref, kc_ref, vc_ref, o_ref, s_scr = refs
    else:
        q_ref, kc_ref, vc_ref, _, o_ref, s_scr = refs
    heads = [slice(hh * HEAD_PAD, (hh + 1) * HEAD_PAD) for hh in range(2)]
    qs = [q_ref[:, sl] for sl in heads]

    def put_scores(slot, k_at, width):
        for hh in range(2):
            s_scr[slot, hh, :, :width] = lax.dot_general(
                qs[hh], k_at(heads[hh]), (((1,), (1,)), ((), ())), preferred_element_type=F32)

    def consume(carry, slot, v_at, width):
        new = []
        for hh in range(2):
            m, acc = carry[hh]
            s = s_scr[slot, hh, :, :width]
            m_new = jnp.maximum(m, jnp.max(s, axis=-1, keepdims=True))
            alpha = jnp.exp2(m - m_new)
            p = jnp.exp2(s - m_new).astype(BF16)
            acc = alpha * acc + jnp.dot(p, v_at(heads[hh]), preferred_element_type=F32)
            new.append((m_new, acc))
        return tuple(new)

    def piece(kr, vr, start, size):
        return (lambda sl: kr[start:start + size, sl]), (lambda sl: vr[start:start + size, sl]), size

    pieces = [piece(kc_ref, vc_ref, 0, CTX_LEN)]
    start = 0
    for size in latent_pieces:
        pieces.append(piece(k_ref, v_ref, start, size))
        start += size
    carry = tuple((jnp.full((tq, 1), -jnp.inf, F32), jnp.zeros((tq, HEAD_PAD), F32))
                  for _ in range(2))
    put_scores(0, pieces[0][0], pieces[0][2])
    for i, (_, v_at, size) in enumerate(pieces):
        if i + 1 < len(pieces):
            put_scores((i + 1) % 2, pieces[i + 1][0], pieces[i + 1][2])
        carry = consume(carry, i % 2, v_at, size)
    outs = [acc[:, :V_HEAD] / acc[:, V_HEAD:V_HEAD + 1] for _, acc in carry]
    o_ref[...] = jnp.concatenate(outs, axis=-1).astype(o_ref.dtype)


def _attn_latent_call(q, k, v):
    qt = SEQ // TQ
    ctx0 = N_LAT // CTX_LEN
    return pl.pallas_call(
        functools.partial(_attn_kernel, latent_pieces=KEY_PIECES, tq=TQ),
        grid=(BATCH, N_HEADS // 2, qt),
        in_specs=[
            pl.BlockSpec((TQ, 2 * HEAD_PAD), lambda b, h, i: (b * qt + i, h)),
            pl.BlockSpec((SEQ, 2 * HEAD_PAD), lambda b, h, i: (b, h)),
            pl.BlockSpec((SEQ, 2 * HEAD_PAD), lambda b, h, i: (b, h)),
            pl.BlockSpec((CTX_LEN, 2 * HEAD_PAD), lambda b, h, i: (ctx0 + b, h)),
            pl.BlockSpec((CTX_LEN, 2 * HEAD_PAD), lambda b, h, i: (ctx0 + b, h)),
        ],
        out_specs=pl.BlockSpec((TQ, 2 * V_HEAD), lambda b, h, i: (b * qt + i, h)),
        out_shape=jax.ShapeDtypeStruct((N_TOT, D_ATT), BF16),
        scratch_shapes=[pltpu.VMEM((2, 2, TQ, max(KEY_PIECES)), F32)],
        compiler_params=_cparams("parallel", "parallel", "arbitrary"),
        name="attn_latent",
    )(q, k, v, k, v)


def _attn_ctx_call(q, k, v, y_att):
    ctx0 = N_LAT // CTX_LEN
    return pl.pallas_call(
        functools.partial(_attn_kernel, latent_pieces=(), tq=CTX_LEN),
        grid=(BATCH, N_HEADS // 2),
        in_specs=[
            pl.BlockSpec((CTX_LEN, 2 * HEAD_PAD), lambda b, h: (ctx0 + b, h)),
            pl.BlockSpec((CTX_LEN, 2 * HEAD_PAD), lambda b, h: (ctx0 + b, h)),
            pl.BlockSpec((CTX_LEN, 2 * HEAD_PAD), lambda b, h: (ctx0 + b, h)),
            pl.BlockSpec(memory_space=pl.ANY),
        ],
        out_specs=pl.BlockSpec((CTX_LEN, 2 * V_HEAD), lambda b, h: (ctx0 + b, h)),
        out_shape=jax.ShapeDtypeStruct((N_TOT, D_ATT), BF16),
        input_output_aliases={3: 0},
        scratch_shapes=[pltpu.VMEM((1, 2, CTX_LEN, CTX_LEN), F32)],
        compiler_params=_cparams("parallel", "parallel"),
        name="attn_ctx",
    )(q, k, v, y_att)


SSM_C = 64
SSM_HALVES = SSM_T // SSM_C
SSM_KC = SSM_GROUP * SSM_C
LAG_ROWS = 2 * SSM_C
PT_LAG, PT_INC, PT_OUT = 0, LAG_ROWS, LAG_ROWS + SSM_C
PT_ROWS = LAG_ROWS + 2 * SSM_C
NT_DIMS = (((1,), (1,)), ((), ()))


def _ssm_kernel(u_ref, pta_ref, ptb_ref, rows_ref, cc_ref, at_ref, dv_ref, y_ref,
                abt_hi_scr, abt_lo_scr, wl_scr, m_scr, ws_scr, wct_scr, s_scr, h_scr):
    H, C = SSM_GROUP, SSM_C
    half_lane = lax.broadcasted_iota(jnp.int32, (1, LANES), 1) < C

    def split(v):
        hi = v.astype(BF16)
        return hi, (v - hi.astype(F32)).astype(BF16)

    def scaled(row0, n_rows, ra, rb):
        return pta_ref[0, row0:row0 + n_rows, :] * ra + ptb_ref[0, row0:row0 + n_rows, :] * rb

    def build_tables(i, _):
        row0 = pl.multiple_of(i * C, C)
        b_re = rows_ref[0, pl.ds(i, 1), :]
        b_im = rows_ref[0, pl.ds(H + i, 1), :]
        lag_rows = pl.ds(pl.multiple_of(i * LAG_ROWS, LAG_ROWS), LAG_ROWS)
        abt_hi_scr[lag_rows, :], abt_lo_scr[lag_rows, :] = split(
            scaled(PT_LAG, LAG_ROWS, b_re, b_im))
        ws_scr[pl.ds(row0, C), :] = scaled(PT_INC, C, b_re, b_im).astype(BF16)
        c_re = rows_ref[0, pl.ds(2 * H + i, 1), :]
        c_im = rows_ref[0, pl.ds(3 * H + i, 1), :]
        wct_scr[pl.ds(row0, C), :] = scaled(PT_OUT, C, c_re, c_im).astype(BF16)
        return 0

    lax.fori_loop(0, H, build_tables, 0)
    cc_hi, cc_lo = split(cc_ref[0])
    nt = lambda a, b: lax.dot_general(a, b, NT_DIMS, preferred_element_type=F32)
    wl = nt(cc_hi, abt_hi_scr[...]) + nt(cc_hi, abt_lo_scr[...]) + nt(cc_lo, abt_hi_scr[...])
    wl_scr[0] = wl
    wl_scr[1] = jnp.concatenate(
        [pltpu.roll(wl[:, ci * LAG_ROWS:(ci + 1) * LAG_ROWS], C, 1) for ci in range(H)], axis=1)

    slot2 = lax.broadcasted_iota(jnp.int32, (1, 2 * LANES), 1)
    first_low = (slot2 < C) | (slot2 > 2 * LANES - C)
    first_high = slot2 < LANES

    def toeplitz_pair(x):
        return pltpu.roll(jnp.broadcast_to(x, (C, 2 * LANES)), 0, 1, stride=1, stride_axis=0)

    def build_toeplitz(ci, _):
        row0 = pl.multiple_of(ci * C, C)
        slots = pl.ds(pl.multiple_of(ci * LAG_ROWS, LAG_ROWS), LAG_ROWS)

        def lags(co, shifted):
            v = wl_scr[int(shifted), co:co + 1, slots]
            return jnp.concatenate([v, v], axis=1)

        for ka in range(0, H // 2, 2):
            kb = ka + 1
            low = toeplitz_pair(jnp.where(first_low, lags(2 * ka, False), lags(2 * kb, False)))
            high = toeplitz_pair(jnp.where(first_high, lags(2 * ka + 1, True),
                                           lags(2 * kb + 1, True)))
            for k, lanes in ((ka, slice(0, LANES)), (kb, slice(LANES, 2 * LANES))):
                m_scr[pl.ds(row0, C), k * LANES:(k + 1) * LANES] = jnp.where(
                    half_lane, low[:, lanes], high[:, lanes]).astype(BF16)
        return 0

    lax.fori_loop(0, H, build_toeplitz, 0)

    def chunk_operand(hf):
        cols = []
        for k in range(H // 2):
            a, b = u_ref[2 * k], u_ref[2 * k + 1]
            if hf == 0:
                cols.append(jnp.where(half_lane, a, pltpu.roll(b, C, 1)))
            else:
                cols.append(jnp.where(half_lane, pltpu.roll(a, C, 1), b))
        return jnp.concatenate(cols, axis=1).astype(BF16)

    ys = []
    for hf in range(SSM_HALVES):
        u = chunk_operand(hf)
        ys.append(jnp.dot(u, m_scr[...], preferred_element_type=F32))
        s = jnp.dot(u, ws_scr[...], preferred_element_type=F32)
        for d in range(2):
            s_d = s[:, d * LANES:(d + 1) * LANES]
            s_scr[hf, d] = s_d
            s_scr[hf, 2 + d] = pltpu.roll(s_d, SSM_STATE, 1)

    n_lat, n_ctx = SEQ // SSM_T, CTX_LEN // SSM_T
    ctx = [((N_CHUNK_LAT + c, n_ctx), hf) for c in range(n_ctx) for hf in range(SSM_HALVES)]
    lat = [((k, n_lat), hf) for k in range(n_lat) for hf in range(SSM_HALVES)]
    coef = [jnp.broadcast_to(at_ref[0, r:r + 1, :], (BATCH, 2 * SSM_STATE)) for r in range(4)]

    def advance(h, h_sw, d, rows, hf):
        a0, a1 = coef[2 * d], coef[2 * d + 1]
        return (h * a0 + h_sw * a1 + s_scr[hf, d, rows, :],
                h_sw * a0 - h * a1 + s_scr[hf, 2 + d, rows, :])

    zero = jnp.zeros((BATCH, 2 * SSM_STATE), F32)
    h_f, h_f_sw, h_r, h_r_sw = zero, zero, zero, zero
    for ((sf, stf), cf), ((sr, strd), cr) in zip(ctx + lat, ctx[::-1] + lat[::-1]):
        rows_f = pl.ds(sf, BATCH, stride=stf)
        rows_r = pl.ds(sr, BATCH, stride=strd)
        h_scr[cf, 0, rows_f, :] = h_f
        h_f, h_f_sw = advance(h_f, h_f_sw, 0, rows_f, cf)
        h_scr[cr, 1, rows_r, :] = h_r
        h_r, h_r_sw = advance(h_r, h_r_sw, 1, rows_r, cr)

    for hf in range(SSM_HALVES):
        h_in = jnp.concatenate([h_scr[hf, 0], h_scr[hf, 1]], axis=-1).astype(BF16)
        ys[hf] = ys[hf] + lax.dot_general(h_in, wct_scr[...], NT_DIMS,
                                          preferred_element_type=F32)
    for k in range(H // 2):
        y0, y1 = ys[0][:, k * LANES:(k + 1) * LANES], ys[1][:, k * LANES:(k + 1) * LANES]
        for c, yc in ((2 * k, jnp.where(half_lane, y0, pltpu.roll(y1, C, 1))),
                      (2 * k + 1, jnp.where(half_lane, pltpu.roll(y0, C, 1), y1))):
            y_ref[c] = yc + u_ref[c] * dv_ref[0, :, c * SSM_T:(c + 1) * SSM_T]


def _ssm_call(u_t, pta, ptb, rows, cc, at, dv):
    g3 = lambda g: (g, 0, 0)
    return pl.pallas_call(
        _ssm_kernel,
        grid=(N_SSM_GROUPS,),
        in_specs=[
            pl.BlockSpec((SSM_GROUP, N_CHUNK, SSM_T), g3),
            pl.BlockSpec((1, PT_ROWS, 4 * SSM_STATE), g3),
            pl.BlockSpec((1, PT_ROWS, 4 * SSM_STATE), g3),
            pl.BlockSpec((1, 4 * SSM_GROUP, 4 * SSM_STATE), g3),
            pl.BlockSpec((1, SSM_GROUP, 4 * SSM_STATE), g3),
            pl.BlockSpec((1, 4, 2 * SSM_STATE), g3),
            pl.BlockSpec((1, 1, SSM_K), g3),
        ],
        out_specs=pl.BlockSpec((SSM_GROUP, N_CHUNK, SSM_T), g3),
        out_shape=jax.ShapeDtypeStruct((D_SSM, N_CHUNK, SSM_T), F32),
        scratch_shapes=[
            pltpu.VMEM((SSM_GROUP * LAG_ROWS, 4 * SSM_STATE), BF16),
            pltpu.VMEM((SSM_GROUP * LAG_ROWS, 4 * SSM_STATE), BF16),
            pltpu.VMEM((2, SSM_GROUP, SSM_GROUP * LAG_ROWS), F32),
            pltpu.VMEM((SSM_KC, SSM_KC), BF16),
            pltpu.VMEM((SSM_KC, 4 * SSM_STATE), BF16),
            pltpu.VMEM((SSM_KC, 4 * SSM_STATE), BF16),
            pltpu.VMEM((SSM_HALVES, 4, N_CHUNK, 2 * SSM_STATE), F32),
            pltpu.VMEM((SSM_HALVES, 2, N_CHUNK, 2 * SSM_STATE), F32),
        ],
        compiler_params=_cparams("parallel"),
        name="s5_mixer",
    )(u_t, pta, ptb, rows, cc, at, dv)


def _ssm_tables(a_re, a_im, log_dt, b_re, b_im, c_re, c_im, d_skip):
    G, P, H, T = N_SSM_GROUPS, SSM_STATE, SSM_GROUP, SSM_T
    a_re, a_im = a_re.astype(F32), a_im.astype(F32)
    dt = jnp.exp(log_dt.astype(F32))[..., None]
    den = a_re * a_re + a_im * a_im
    mag1 = jnp.exp(dt * a_re)
    ab_re, ab_im = mag1 * jnp.cos(dt * a_im), mag1 * jnp.sin(dt * a_im)
    num_re = ab_re - 1.0
    f_re = (num_re * a_re + ab_im * a_im) / den
    f_im = (ab_im * a_re - num_re * a_im) / den
    b_re, b_im = b_re.astype(F32), b_im.astype(F32)
    bb_re = f_re[..., None] * b_re - f_im[..., None] * b_im
    bb_im = f_re[..., None] * b_im + f_im[..., None] * b_re
    c_re, c_im = c_re.astype(F32), c_im.astype(F32)
    la, th = dt * a_re, dt * a_im

    def powers(d, n):
        mag = jnp.exp(la[d][:, None, :] * n[None, :, None])
        ph = th[d][:, None, :] * n[None, :, None]
        return mag * jnp.cos(ph), mag * jnp.sin(ph)

    C = SSM_C
    slot = np.arange(LAG_ROWS)
    lag_f = np.where(slot < C, slot, -1)
    lag_r = np.where((-slot) % LAG_ROWS < C, (-slot) % LAG_ROWS, -1)
    s_idx = np.arange(C)
    f_all = np.concatenate([lag_f, C - 1 - s_idx, s_idx + 1])
    r_all = np.concatenate([lag_r, s_idx, C - s_idx])
    f_exp, r_exp = jnp.asarray(np.maximum(f_all, 0), F32), jnp.asarray(np.maximum(r_all, 0), F32)
    f_on, r_on = jnp.asarray(f_all >= 0, F32), jnp.asarray(r_all >= 0, F32)
    lane = jnp.arange(4 * P)
    is_fwd = (lane < 2 * P)[None, :]
    is_re = ((lane // P) % 2 == 0)[None, None, :]
    expo = jnp.where(is_fwd, f_exp[:, None], r_exp[:, None])[None]
    on = jnp.where(is_fwd, f_on[:, None], r_on[:, None])[None]
    la4 = jnp.concatenate([la[0], la[0], la[1], la[1]], axis=-1)[:, None, :]
    th4 = jnp.concatenate([th[0], th[0], th[1], th[1]], axis=-1)[:, None, :]
    mag = jnp.exp(la4 * expo) * on
    p_re, p_im = mag * jnp.cos(th4 * expo), mag * jnp.sin(th4 * expo)
    pta = jnp.where(is_re, p_re, p_im)
    ptb = jnp.where(is_re, -p_im, p_re)

    def per_channel(v):
        f, r = v[0].transpose(0, 2, 1), v[1].transpose(0, 2, 1)
        return jnp.concatenate([f, f, r, r], axis=-1)
    cf_re, cr_re = c_re[0], c_re[1]
    cf_im, cr_im = c_im[0], c_im[1]
    rows = jnp.concatenate([
        per_channel(bb_re), per_channel(bb_im),
        jnp.concatenate([cf_re, -cf_re, cr_re, -cr_re], axis=-1),
        jnp.concatenate([cf_im, -cf_im, cr_im, -cr_im], axis=-1)], axis=1)
    cc = jnp.concatenate([cf_re, -cf_im, cr_re, -cr_im], axis=-1)

    t_exp = jnp.full((1,), float(SSM_C), F32)
    (f_re_t, f_im_t), (r_re_t, r_im_t) = powers(0, t_exp), powers(1, t_exp)
    f_re_t, f_im_t, r_re_t, r_im_t = (v[:, 0, :] for v in (f_re_t, f_im_t, r_re_t, r_im_t))
    at = jnp.stack([jnp.concatenate([f_re_t, f_re_t], -1), jnp.concatenate([-f_im_t, f_im_t], -1),
                    jnp.concatenate([r_re_t, r_re_t], -1), jnp.concatenate([-r_im_t, r_im_t], -1)],
                   axis=1)
    dv = jnp.repeat(d_skip.astype(F32).reshape(G, H), T, axis=-1).reshape(G, 1, H * T)
    return pta, ptb, rows, cc, at, dv


ROUTE_E1, ROUTE_E2, ROUTE_W1, ROUTE_W2 = 0, 1, 2, 3


def _top2_route(logits):
    lane = lax.broadcasted_iota(jnp.int32, logits.shape, 1)
    lg = jnp.where(lane < N_EXPERTS, logits, -jnp.inf)
    m1 = jnp.max(lg, axis=-1, keepdims=True)
    i1 = jnp.min(jnp.where(lg == m1, lane, LANES), axis=-1, keepdims=True)
    lg2 = jnp.where(lane == i1, -jnp.inf, lg)
    m2 = jnp.max(lg2, axis=-1, keepdims=True)
    i2 = jnp.min(jnp.where(lg2 == m2, lane, LANES), axis=-1, keepdims=True)
    e2 = jnp.exp(m2 - m1)
    w1 = 1.0 / (1.0 + e2)
    rec = jnp.where(lane == ROUTE_E1, i1.astype(F32), 0.0)
    rec = jnp.where(lane == ROUTE_E2, i2.astype(F32), rec)
    rec = jnp.where(lane == ROUTE_W1, w1, rec)
    return jnp.where(lane == ROUTE_W2, e2 * w1, rec), i1, i2


def _expert_slots(i1, i2, taken):
    n = i1.shape[0]
    lane = lax.broadcasted_iota(jnp.int32, (n, LANES), 1)
    picked = jnp.where((lane == i1) | (lane == i2), 1.0, 0.0)
    earlier = (lax.broadcasted_iota(jnp.int32, (n, n), 1)
               < lax.broadcasted_iota(jnp.int32, (n, n), 0)).astype(BF16)
    rank = jnp.dot(earlier, picked.astype(BF16), preferred_element_type=F32)
    slot = rank + taken + (lane * EXPERT_CAP).astype(F32)
    d1 = jnp.sum(jnp.where(lane == i1, slot, 0.0), axis=-1, keepdims=True)
    d2 = jnp.sum(jnp.where(lane == i2, slot, 0.0), axis=-1, keepdims=True)
    dest = jnp.where(lane == 0, d1, jnp.where(lane == 1, d2, 0.0)).astype(jnp.int32)
    return dest, jnp.sum(picked, axis=0, keepdims=True)


def _pack_bf16_pairs(v):
    k = v.shape[1] // 2
    bits = pltpu.bitcast(v.astype(BF16).astype(F32), jnp.uint32)
    return (bits[:, :k] & jnp.uint32(0xFFFF0000)) | (bits[:, k:] >> 16)


def _unpack_bf16_pairs(w):
    hi = pltpu.bitcast(w & jnp.uint32(0xFFFF0000), F32)
    lo = pltpu.bitcast(w << 16, F32)
    return hi, lo


def _store_packed(ref, v):
    words = _pack_bf16_pairs(v)
    for s in range(SC_SPLIT):
        ref[s] = words[:, s * SC_ROW:(s + 1) * SC_ROW]


def _load_packed(ref):
    return _unpack_bf16_pairs(jnp.concatenate([ref[s] for s in range(SC_SPLIT)], axis=-1))


def _mix_kernel(*refs, with_router, n_x):
    rest = refs[n_x:]
    if with_router:
        (yst_ref, ya_ref, mod_ref, wglut_ref, bglu_ref, wout_ref, gffn_ref, router_ref,
         x1_ref, h2_ref, gate_ref, dest_ref, cnt_ref, taken_scr) = rest
    else:
        (yst_ref, ya_ref, mod_ref, wglut_ref, bglu_ref, wout_ref, gffn_ref,
         w1_ref, w3_ref, w2_ref, x2_ref) = rest
    zt = jax.nn.gelu(yst_ref[...], approximate=True)
    glt = zt * jax.nn.sigmoid(
        jnp.dot(wglut_ref[...], zt.astype(BF16), preferred_element_type=F32) + bglu_ref[...])
    mix = jnp.concatenate([glt.T.astype(BF16), ya_ref[...]], axis=-1)
    o = jnp.dot(mix, wout_ref[...], preferred_element_type=F32)
    x1 = _tile_rows(refs, n_x) + mod_ref[0, 2:3, :] * o
    h2 = _rms(x1, gffn_ref[...]) * (1.0 + mod_ref[0, 4:5, :]) + mod_ref[0, 3:4, :]
    if not with_router:
        acc = _swiglu(h2.astype(BF16), lambda sl: w1_ref[:, sl], lambda sl: w3_ref[:, sl],
                      lambda sl: w2_ref[sl, :], D_FF)
        x2_ref[...] = x1 + mod_ref[0, 5:6, :] * acc
    else:
        x1_ref[...] = x1
        _store_packed(h2_ref, h2)
        h_hi = h2.astype(BF16)
        h_lo = (h2 - h_hi.astype(F32)).astype(BF16)
        o1 = jnp.dot(h_hi, router_ref[0], preferred_element_type=F32)
        o2 = jnp.dot(h_lo, router_ref[1], preferred_element_type=F32)
        logits = o1 + pltpu.roll(o1, LANES - N_EXPERTS, 1) + o2
        gate_ref[...], i1, i2 = _top2_route(logits)

        @pl.when(pl.program_id(0) == 0)
        def _():
            taken_scr[...] = jnp.zeros_like(taken_scr)

        dest_ref[...], tile_cnt = _expert_slots(i1, i2, taken_scr[...])
        taken_scr[...] += tile_cnt
        cnt_ref[...] = jnp.broadcast_to(taken_scr[...], cnt_ref.shape)


def _mix_call(xs, y_ssm_t, y_att, mod, wglut, bglu, wout, gffn, *, router=None, ffn=None, n_tiles):
    row = lambda i: (i, 0)
    out_row = row
    with_router = router is not None
    assert with_router != (ffn is not None)
    in_specs = _x_specs(len(xs)) + [
        pl.BlockSpec((D_SSM, TM), lambda i: (0, i)),
        pl.BlockSpec((TM, D_ATT), row),
        pl.BlockSpec((1, 6, D_MODEL), lambda i: (_mod_row(i), 0, 0)),
        _const_spec((D_SSM, D_SSM)),
        _const_spec((D_SSM, 1)),
        _const_spec((D_MIX, D_MODEL)),
        _const_spec((1, D_MODEL)),
    ]
    args = [*xs, y_ssm_t, y_att, mod, wglut, bglu, wout, gffn]
    out_specs = [pl.BlockSpec((TM, D_MODEL), out_row)]
    out_shape = [jax.ShapeDtypeStruct((n_tiles * TM, D_MODEL), F32)]
    if with_router:
        in_specs.append(_const_spec((2, D_MODEL, LANES)))
        args.append(router)
        out_specs += [pl.BlockSpec((SC_SPLIT, TM, SC_ROW), lambda i: (0, i, 0)),
                      pl.BlockSpec((TM, LANES), out_row),
                      pl.BlockSpec((TM, LANES), out_row),
                      pl.BlockSpec((8, LANES), lambda i: (0, 0))]
        out_shape += [jax.ShapeDtypeStruct((SC_SPLIT, n_tiles * TM, SC_ROW), jnp.uint32),
                      jax.ShapeDtypeStruct((n_tiles * TM, LANES), F32),
                      jax.ShapeDtypeStruct((n_tiles * TM, LANES), jnp.int32),
                      jax.ShapeDtypeStruct((8, LANES), F32)]
        scratch = [pltpu.VMEM((1, LANES), F32)]
    else:
        in_specs += [_const_spec((D_MODEL, D_FF)), _const_spec((D_MODEL, D_FF)),
                     _const_spec((D_FF, D_MODEL))]
        args += list(ffn)
        scratch = []
    return pl.pallas_call(
        functools.partial(_mix_kernel, with_router=with_router, n_x=len(xs)),
        grid=(n_tiles,),
        in_specs=in_specs,
        out_specs=out_specs,
        out_shape=out_shape,
        scratch_shapes=scratch,
        compiler_params=_cparams("arbitrary" if with_router else "parallel"),
        name="mix_out",
    )(*args)


def _swiglu(h, w1_at, w3_at, w2_at, d_ff):
    acc = jnp.zeros((h.shape[0], D_MODEL), F32)
    for lo in range(0, d_ff, FF_CHUNK):
        sl = slice(lo, min(lo + FF_CHUNK, d_ff))
        a = jnp.dot(h, w1_at(sl), preferred_element_type=F32)
        b = jnp.dot(h, w3_at(sl), preferred_element_type=F32)
        g = (a * jax.nn.sigmoid(a) * b).astype(BF16)
        acc = acc + jnp.dot(g, w2_at(sl), preferred_element_type=F32)
    return acc


EXPERT_BLK = 512
EXPERT_CAP = N_LAT
N_SORTED = N_EXPERTS * EXPERT_CAP
N_EXPERT_BLKS = 2 * N_LAT // EXPERT_BLK + N_EXPERTS
PACKED = D_MODEL // 2
SC_ROW = 256
SC_SPLIT = PACKED // SC_ROW
SC_WIN = 128


def _sc_mesh():
    return plsc.VectorSubcoreMesh(core_axis_name="core", subcore_axis_name="subcore")


def _sc_scatter(x, idx_a, idx_b, n_out):
    n = x.shape[0]

    @pl.kernel(out_type=jax.ShapeDtypeStruct((n_out, SC_ROW), x.dtype), mesh=_sc_mesh(),
               scratch_types=[])
    def scatter(x_hbm, a_hbm, b_hbm, o_hbm):
        def body(x_vmem, a_vmem, b_vmem):
            pltpu.sync_copy(x_vmem, o_hbm.at[a_vmem.at[0]])
            pltpu.sync_copy(x_vmem, o_hbm.at[b_vmem.at[0]])

        pltpu.emit_pipeline(
            body, grid=(n // SC_WIN,),
            in_specs=[pl.BlockSpec((SC_WIN, SC_ROW), lambda i: (i, 0)),
                      pl.BlockSpec((1, SC_WIN), lambda i: (0, i)),
                      pl.BlockSpec((1, SC_WIN), lambda i: (0, i))],
            out_specs=[],
            core_axis_name=("core", "subcore"),
            dimension_semantics=(pltpu.PARALLEL,),
        )(x_hbm, a_hbm, b_hbm)

    return scatter(x, idx_a.reshape(1, n), idx_b.reshape(1, n))


def _sc_gather(y, idx):
    n = idx.shape[0]

    @pl.kernel(out_type=jax.ShapeDtypeStruct((n, SC_ROW), y.dtype), mesh=_sc_mesh(),
               scratch_types=[])
    def gather(y_hbm, i_hbm, o_hbm):
        def body(i_vmem, o_vmem):
            pltpu.sync_copy(y_hbm.at[i_vmem.at[0]], o_vmem)

        pltpu.emit_pipeline(
            body, grid=(n // SC_WIN,),
            in_specs=[pl.BlockSpec((1, SC_WIN), lambda i: (0, i))],
            out_specs=[pl.BlockSpec((SC_WIN, SC_ROW), lambda i: (i, 0))],
            core_axis_name=("core", "subcore"),
            dimension_semantics=(pltpu.PARALLEL,),
        )(i_hbm, o_hbm)

    return gather(y, idx.reshape(1, n))


W_CAST_ROWS = 128


def _experts_kernel(blk_expert_ref, blk_row_ref, n_used_ref, x_ref, w1_ref, w3_ref, w2_ref, o_ref,
                    w1_scr, w3_scr, w2_scr):
    del blk_row_ref
    b = pl.program_id(0)
    live = b < n_used_ref[0]
    new_expert = (b == 0) | (blk_expert_ref[b] != blk_expert_ref[jnp.maximum(b - 1, 0)])

    @pl.when(live & new_expert)
    def _():
        for src, dst in ((w1_ref, w1_scr), (w3_ref, w3_scr), (w2_ref, w2_scr)):
            for r in range(0, dst.shape[0], W_CAST_ROWS):
                dst[r:r + W_CAST_ROWS, :] = src[0, r:r + W_CAST_ROWS, :].astype(BF16)

    @pl.when(live)
    def _():
        hi, lo = _load_packed(x_ref)
        h = jnp.concatenate([hi.astype(BF16), lo.astype(BF16)], axis=-1)
        y = _swiglu(h, lambda sl: w1_scr[:, sl], lambda sl: w3_scr[:, sl],
                    lambda sl: w2_scr[sl, :], D_FF_EXPERT)
        _store_packed(o_ref, y)


def _experts_call(blk_expert, blk_row, n_used, xs, w1, w3, w2):
    row = lambda b, be, br, nu: (0, br[b], 0)
    wsel = lambda b, be, br, nu: (be[b], 0, 0)
    return pl.pallas_call(
        _experts_kernel,
        grid_spec=pltpu.PrefetchScalarGridSpec(
            num_scalar_prefetch=3,
            grid=(N_EXPERT_BLKS,),
            in_specs=[
                pl.BlockSpec((SC_SPLIT, EXPERT_BLK, SC_ROW), row),
                pl.BlockSpec((1, D_MODEL, D_FF_EXPERT), wsel),
                pl.BlockSpec((1, D_MODEL, D_FF_EXPERT), wsel),
                pl.BlockSpec((1, D_FF_EXPERT, D_MODEL), wsel),
            ],
            out_specs=pl.BlockSpec((SC_SPLIT, EXPERT_BLK, SC_ROW), row),
            scratch_shapes=[pltpu.VMEM((D_MODEL, D_FF_EXPERT), BF16),
                            pltpu.VMEM((D_MODEL, D_FF_EXPERT), BF16),
                            pltpu.VMEM((D_FF_EXPERT, D_MODEL), BF16)],
        ),
        out_shape=jax.ShapeDtypeStruct((SC_SPLIT, N_SORTED, SC_ROW), jnp.uint32),
        compiler_params=_cparams("arbitrary"),
        name="moe_experts",
    )(blk_expert, blk_row, n_used, xs, w1, w3, w2)


def _combine_kernel(x1_ref, r_ref, y_ref, mod_ref, fg_ref, *rest):
    o_ref = rest[-1]

    def expert_out(slot):
        hi, lo = _load_packed(y_ref.at[slot])
        return jnp.concatenate([hi, lo], axis=-1)
    w1 = r_ref[:, ROUTE_W1:ROUTE_W1 + 1]
    w2 = r_ref[:, ROUTE_W2:ROUTE_W2 + 1]
    y = w1 * expert_out(0) + w2 * expert_out(1)
    x2 = x1_ref[...] + mod_ref[0, 5:6, :] * y
    o_ref[...] = _rms(x2, fg_ref[...])


COMBINE_PARTS = 2


def _combine_call(x1, route, y_tok, mod, fg, part, out):
    n_tiles = N_LAT // TM // COMBINE_PARTS
    tile0 = part * n_tiles
    row = lambda i: (tile0 + i, 0)
    in_specs = [
        pl.BlockSpec((TM, D_MODEL), row),
        pl.BlockSpec((TM, LANES), row),
        pl.BlockSpec((2, SC_SPLIT, TM, SC_ROW), lambda i: (0, 0, i, 0)),
        pl.BlockSpec((1, 6, D_MODEL), lambda i: ((tile0 + i) // SEQ_TILES, 0, 0)),
        pl.BlockSpec((1, D_MODEL), lambda i: (0, 0)),
    ]
    args = [x1, route, y_tok, mod, fg]
    aliases = {}
    if out is not None:
        in_specs.append(pl.BlockSpec(memory_space=pl.ANY))
        args.append(out)
        aliases = {len(args) - 1: 0}
    return pl.pallas_call(
        _combine_kernel,
        grid=(n_tiles,),
        in_specs=in_specs,
        out_specs=pl.BlockSpec((TM, D_MODEL), row),
        out_shape=jax.ShapeDtypeStruct((N_LAT, D_MODEL), F32),
        input_output_aliases=aliases,
        compiler_params=_cparams("parallel"),
        name="moe_combine",
    )(*args)


def _moe_routed(h2p, x1, route, dest, cnt, mod, w1, w3, w2, fg):
    blks = (cnt[0, :N_EXPERTS].astype(jnp.int32) + (EXPERT_BLK - 1)) // EXPERT_BLK
    blk_end = jnp.cumsum(blks)
    n_used = blk_end[-1:]
    b = jnp.minimum(jnp.arange(N_EXPERT_BLKS, dtype=jnp.int32), n_used[0] - 1)
    blk_expert = jnp.sum((b[:, None] >= blk_end[None, :]).astype(jnp.int32), axis=1)
    blk_row = blk_expert * (EXPERT_CAP // EXPERT_BLK) + b - (blk_end - blks)[blk_expert]
    piece = jnp.arange(SC_SPLIT, dtype=jnp.int32)[:, None] * N_SORTED
    idx = [(piece + dest[:, slot][None, :]).reshape(SC_SPLIT * N_LAT) for slot in range(2)]
    xs = _sc_scatter(h2p.reshape(SC_SPLIT * N_LAT, SC_ROW), idx[0], idx[1], SC_SPLIT * N_SORTED)
    ys = _experts_call(blk_expert, blk_row, n_used, xs.reshape(SC_SPLIT, N_SORTED, SC_ROW),
                       w1, w3, w2)
    ys_flat = ys.reshape(SC_SPLIT * N_SORTED, SC_ROW)
    n_part = N_LAT // COMBINE_PARTS
    out = None
    for part in range(COMBINE_PARTS):
        tokens = slice(part * n_part, (part + 1) * n_part)
        idx_part = jnp.concatenate(
            [(piece + dest[tokens, slot][None, :]).reshape(SC_SPLIT * n_part) for slot in range(2)])
        y_tok = _sc_gather(ys_flat, idx_part).reshape(2, SC_SPLIT, n_part, SC_ROW)
        out = _combine_call(x1, route, y_tok, mod, fg, part, out)
    return out


def _rope_partner_perm():
    perm, sign = [], []
    for j in range(QK_ROPE):
        first_half = (j % AXIS_ROPE) < ROPE_FREQS
        perm.append(j + ROPE_FREQS if first_half else j - ROPE_FREQS)
        sign.append(-1.0 if first_half else 1.0)
    return jnp.array(perm, jnp.int32), jnp.array(sign, F32)


def _rope_tables():
    t = jnp.arange(SEQ)
    row = (t // GRID_W).astype(F32)
    col = (t % GRID_W).astype(F32)
    inv_freq = ROPE_BASE ** (-2.0 * jnp.arange(ROPE_FREQS, dtype=F32) / AXIS_ROPE)
    ang = jnp.concatenate([row[:, None] * inv_freq, row[:, None] * inv_freq,
                           col[:, None] * inv_freq, col[:, None] * inv_freq], axis=1)
    cos = jnp.concatenate([jnp.cos(ang), jnp.ones((N_CTX, QK_ROPE), F32)], axis=0)
    sin = jnp.concatenate([jnp.sin(ang), jnp.zeros((N_CTX, QK_ROPE), F32)], axis=0)
    n = N_CTX + SEQ
    pad32 = jnp.zeros((n, HEAD_PAD - D_QK), F32)
    qs = ATT_SCALE * math.log2(math.e)
    cq = jnp.concatenate([jnp.full((n, QK_NOPE), qs, F32), qs * cos, pad32], axis=1)
    sq = jnp.concatenate([jnp.zeros((n, QK_NOPE), F32), qs * sin, pad32], axis=1)
    cs = jnp.concatenate([cos, sin, jnp.zeros((n, LANES - 2 * QK_ROPE), F32)], axis=1)
    return cq, sq, cs


def _layer_weights(w_in, w_uq, w_ukv):
    perm, sign = _rope_partner_perm()
    s0 = D_SSM + Q_LORA + KV_LORA
    kr_w = w_in[:, s0:s0 + QK_ROPE]
    wut = w_in[:, :D_SSM].T.astype(BF16)
    win = jnp.concatenate([w_in[:, D_SSM:s0], kr_w, kr_w[:, perm] * sign,
                           jnp.zeros((D_MODEL, LANES - 2 * QK_ROPE), F32)], axis=1).astype(BF16)
    uq = w_uq.reshape(Q_LORA, N_HEADS, D_QK)
    nope, rope = uq[..., :QK_NOPE], uq[..., QK_NOPE:]
    zpad = jnp.zeros((Q_LORA, N_HEADS, HEAD_PAD - D_QK), F32)
    wq1 = jnp.concatenate([nope, rope, zpad], axis=-1).reshape(Q_LORA, N_HEADS * HEAD_PAD).astype(BF16)
    wq2 = jnp.concatenate([jnp.zeros_like(nope), rope[..., perm] * sign, zpad], axis=-1)
    wq2 = wq2.reshape(Q_LORA, N_HEADS * HEAD_PAD).astype(BF16)
    ukv = w_ukv.reshape(KV_LORA, N_HEADS, QK_NOPE + V_HEAD)
    wk = jnp.concatenate([ukv[..., :QK_NOPE], jnp.zeros((KV_LORA, N_HEADS, HEAD_PAD - QK_NOPE), F32)],
                         axis=-1).reshape(KV_LORA, N_HEADS * HEAD_PAD)
    eye = jnp.eye(QK_ROPE, dtype=F32)
    place = jnp.concatenate([jnp.zeros((QK_ROPE, QK_NOPE), F32), eye,
                             jnp.zeros((QK_ROPE, HEAD_PAD - D_QK), F32)], axis=1)
    place = jnp.tile(place, (1, N_HEADS))
    spread = jnp.concatenate([place, place, jnp.zeros((LANES - 2 * QK_ROPE, N_HEADS * HEAD_PAD), F32)], 0)
    wkk = jnp.concatenate([wk, spread], axis=0).astype(BF16)
    wv = jnp.concatenate([ukv[..., QK_NOPE:], jnp.zeros((KV_LORA, N_HEADS, HEAD_PAD - V_HEAD), F32)],
                         axis=-1).reshape(KV_LORA, N_HEADS * HEAD_PAD).astype(BF16)
    return wut, win, wq1, wq2, wkk, wv


def kernel(x, c, ctx, c_ctx, w_ada, b_ada, norm_mix, norm_ffn, w_in, q_norm, kv_norm, w_uq, w_ukv,
           ssm_a_re, ssm_a_im, ssm_log_dt, ssm_b_re, ssm_b_im, ssm_c_re, ssm_c_im, ssm_d, w_glu,
           b_glu, w_out, ffn_w1, ffn_w3, ffn_w2, moe_router, moe_w1, moe_w3, moe_w2, final_norm):
    assert x.shape == (BATCH, SEQ, D_MODEL) and ctx.shape == (BATCH, CTX_LEN, D_MODEL)
    cond = jnp.concatenate([c, c_ctx[None, :], jnp.zeros((MOD_ROWS - BATCH - 1, D_MODEL), F32)], axis=0)
    mod_all = _ada_call(cond, w_ada, b_ada).reshape(DEPTH, MOD_ROWS, 6, D_MODEL)
    cq_t, sq_t, cs_t = _rope_tables()
    xs = (x.reshape(N_LAT, D_MODEL), ctx.reshape(N_CTX, D_MODEL))

    out = None
    for i in range(DEPTH):
        last = i == DEPTH - 1
        mod = mod_all[i]
        wut, win, wq1, wq2, wkk, wv = _layer_weights(w_in[i], w_uq[i], w_ukv[i])
        u_t, q, k, v = _inproj_call(xs, mod, norm_mix[i][None, :], wut, win, q_norm[i][None, :],
                                    kv_norm[i][None, :], wq1, wq2, wkk, wv, cq_t, sq_t, cs_t)
        tabs = _ssm_tables(ssm_a_re[i], ssm_a_im[i], ssm_log_dt[i], ssm_b_re[i], ssm_b_im[i],
                           ssm_c_re[i], ssm_c_im[i], ssm_d[i])
        y_ssm_t = _ssm_call(u_t.reshape(D_SSM, N_CHUNK, SSM_T), *tabs).reshape(D_SSM, N_TOT)
        y_att = _attn_latent_call(q, k, v)
        if last:
            n_tiles = LAT_TILES
        else:
            y_att = _attn_ctx_call(q, k, v, y_att)
            n_tiles = N_TOT // TM
        j = i // 2
        mix_args = (xs, y_ssm_t, y_att, mod, w_glu[i].T.astype(BF16), b_glu[i][:, None],
                    w_out[i].astype(BF16), norm_ffn[i][None, :])
        if i % 2 == 0:
            assert not last
            ffn = (ffn_w1[j].astype(BF16), ffn_w3[j].astype(BF16), ffn_w2[j].astype(BF16))
            xs = tuple(_mix_call(*mix_args, ffn=ffn, n_tiles=n_tiles))
        else:
            assert last
            r = moe_router[j]
            r_top = lax.bitcast_convert_type(
                lax.bitcast_convert_type(r, jnp.uint32) & jnp.uint32(0xFFFF0000), F32)
            r_hi = r_top.astype(BF16)
            r_lo = (r - r_top).astype(BF16)
            zr = jnp.zeros((D_MODEL, LANES - 2 * N_EXPERTS), BF16)
            router = jnp.stack([jnp.concatenate([r_hi, r_lo, zr], axis=1),
                                jnp.concatenate([r_hi, jnp.zeros_like(r_lo), zr], axis=1)])
            x1, h2p, route, dest, cnt = _mix_call(*mix_args, router=router, n_tiles=n_tiles)
            out = _moe_routed(h2p, x1, route, dest, cnt, mod, moe_w1[j], moe_w3[j], moe_w2[j],
                              final_norm[None, :])
    return out.reshape(BATCH, SEQ, D_MODEL)
```

```python
import functools
import math

import jax
import jax.numpy as jnp
import numpy as np
from jax import lax
from jax.experimental import pallas as pl
from jax.experimental.pallas import tpu as pltpu
from jax.experimental.pallas import tpu_sc as plsc

D_MODEL = 1024
BATCH = 4
SEQ = 8192
DEPTH = 2
GRID_W = 64
CTX_LEN = 256
D_SSM = 512
SSM_GROUP = 16
N_SSM_GROUPS = D_SSM // SSM_GROUP
SSM_STATE = 64
N_HEADS = 8
QK_NOPE = 64
QK_ROPE = 32
V_HEAD = 64
Q_LORA = 256
KV_LORA = 128
D_QK = QK_NOPE + QK_ROPE
D_ATT = N_HEADS * V_HEAD
D_MIX = D_SSM + D_ATT
AXIS_ROPE = QK_ROPE // 2
ROPE_FREQS = AXIS_ROPE // 2
ROPE_BASE = 10000.0
ATT_SCALE = 1.0 / math.sqrt(D_QK)
D_FF = 2816
N_EXPERTS = 8
D_FF_EXPERT = 1408
EPS = 1e-6

N_CTX = BATCH * CTX_LEN
N_LAT = BATCH * SEQ
N_TOT = N_CTX + N_LAT

LANES = 128
HEAD_PAD = 128
TM = 512
LAT_TILES = N_LAT // TM
SEQ_TILES = SEQ // TM
TQ = 1024
KEY_PIECES = (2048, 2048, 2048, 2048)
assert sum(KEY_PIECES) == SEQ
SSM_T = 128
N_CHUNK_LAT = N_LAT // SSM_T
N_CHUNK = N_TOT // SSM_T
SSM_K = SSM_GROUP * SSM_T
FF_CHUNK = 256
D_IN_REST = Q_LORA + KV_LORA + LANES
MOD_ROWS = 8
VMEM_LIMIT = 56 * 1024 * 1024

F32 = jnp.float32
BF16 = jnp.bfloat16
HI = lax.Precision.HIGHEST


def _cparams(*sem):
    return pltpu.CompilerParams(dimension_semantics=sem, vmem_limit_bytes=VMEM_LIMIT)


def _const_spec(shape):
    nd = len(shape)
    return pl.BlockSpec(shape, lambda *_: (0,) * nd, pipeline_mode=pl.Buffered(1))


def _mod_row(i):
    return jnp.where(i < LAT_TILES, i // SEQ_TILES, BATCH)


def _pos_tile(i):
    return jnp.where(i < LAT_TILES, i % SEQ_TILES, SEQ_TILES + i - LAT_TILES)


def _rms(x, g):
    ms = jnp.mean(x * x, axis=-1, keepdims=True)
    return x * lax.rsqrt(ms + EPS) * g


ADA_TN = 1536


def _ada_kernel(c_ref, w_ref, b_ref, o_ref):
    c = c_ref[...]
    s = c * jax.nn.sigmoid(c)
    o_ref[0] = jnp.dot(s, w_ref[0], precision=HI, preferred_element_type=F32) + b_ref[0]


def _ada_call(cond, w_ada, b_ada):
    n_col = 6 * D_MODEL // ADA_TN
    return pl.pallas_call(
        _ada_kernel,
        grid=(DEPTH, n_col),
        in_specs=[
            pl.BlockSpec((MOD_ROWS, D_MODEL), lambda l, j: (0, 0)),
            pl.BlockSpec((1, D_MODEL, ADA_TN), lambda l, j: (l, 0, j)),
            pl.BlockSpec((1, 1, ADA_TN), lambda l, j: (l, 0, j)),
        ],
        out_specs=pl.BlockSpec((1, MOD_ROWS, ADA_TN), lambda l, j: (l, 0, j)),
        out_shape=jax.ShapeDtypeStruct((DEPTH, MOD_ROWS, 6 * D_MODEL), F32),
        compiler_params=_cparams("arbitrary", "arbitrary"),
        name="ada_mod",
    )(cond, w_ada, b_ada.reshape(DEPTH, 1, 6 * D_MODEL))


def _tile_rows(refs, n_x):
    if n_x == 1:
        return refs[0][...]
    return jnp.where(pl.program_id(0) < LAT_TILES, refs[0][...], refs[1][...])


def _x_specs(n_x):
    if n_x == 1:
        return [pl.BlockSpec((TM, D_MODEL), lambda i: (i, 0))]
    return [pl.BlockSpec((TM, D_MODEL), lambda i: (jnp.minimum(i, LAT_TILES - 1), 0)),
            pl.BlockSpec((TM, D_MODEL), lambda i: (jnp.maximum(i - LAT_TILES, 0), 0))]


def _inproj_kernel(*refs, n_x):
    (mod_ref, g_ref, wut_ref, win_ref, qg_ref, kvg_ref, wq1_ref, wq2_ref, wkk_ref, wv_ref,
     cq_ref, sq_ref, cs_ref, ut_ref, q_ref, k_ref, v_ref) = refs[n_x:]
    x = _tile_rows(refs, n_x)
    sh = mod_ref[0, 0:1, :]
    sc = mod_ref[0, 1:2, :]
    xm = (_rms(x, g_ref[...]) * (1.0 + sc) + sh).astype(BF16)
    ut_ref[...] = lax.dot_general(wut_ref[...], xm, (((1,), (1,)), ((), ())),
                                  preferred_element_type=F32)
    z = jnp.dot(xm, win_ref[...], preferred_element_type=F32)
    qn = _rms(z[:, :Q_LORA], qg_ref[...]).astype(BF16)
    kvn = _rms(z[:, Q_LORA:Q_LORA + KV_LORA], kvg_ref[...]).astype(BF16)
    krr = (z[:, Q_LORA + KV_LORA:] * cs_ref[...]).astype(BF16)
    q1 = jnp.dot(qn, wq1_ref[...], preferred_element_type=F32)
    q2 = jnp.dot(qn, wq2_ref[...], preferred_element_type=F32)
    cq = cq_ref[...]
    sq = sq_ref[...]
    for h in range(N_HEADS):
        sl = slice(h * HEAD_PAD, (h + 1) * HEAD_PAD)
        q_ref[:, sl] = (q1[:, sl] * cq + q2[:, sl] * sq).astype(q_ref.dtype)
    kin = jnp.concatenate([kvn, krr], axis=-1)
    k_ref[...] = jnp.dot(kin, wkk_ref[...], preferred_element_type=F32).astype(k_ref.dtype)
    vv = jnp.dot(kvn, wv_ref[...], preferred_element_type=F32)
    lane = lax.broadcasted_iota(jnp.int32, vv.shape, 1)
    v_ref[...] = jnp.where(lane % HEAD_PAD == V_HEAD, 1.0, vv).astype(v_ref.dtype)


def _inproj_call(xs, mod, g_mix, wut, win, qg, kvg, wq1, wq2, wkk, wv, cq_t, sq_t, cs_t):
    n_tiles = N_TOT // TM
    row = lambda i: (i, 0)
    pos = lambda i: (_pos_tile(i), 0)
    return pl.pallas_call(
        functools.partial(_inproj_kernel, n_x=len(xs)),
        grid=(n_tiles,),
        in_specs=_x_specs(len(xs)) + [
            pl.BlockSpec((1, 6, D_MODEL), lambda i: (_mod_row(i), 0, 0)),
            _const_spec((1, D_MODEL)),
            _const_spec((D_SSM, D_MODEL)),
            _const_spec((D_MODEL, D_IN_REST)),
            _const_spec((1, Q_LORA)),
            _const_spec((1, KV_LORA)),
            _const_spec((Q_LORA, N_HEADS * HEAD_PAD)),
            _const_spec((Q_LORA, N_HEADS * HEAD_PAD)),
            _const_spec((2 * KV_LORA, N_HEADS * HEAD_PAD)),
            _const_spec((KV_LORA, N_HEADS * HEAD_PAD)),
            pl.BlockSpec((TM, LANES), pos),
            pl.BlockSpec((TM, LANES), pos),
            pl.BlockSpec((TM, LANES), pos),
        ],
        out_specs=[
            pl.BlockSpec((D_SSM, TM), lambda i: (0, i)),
            pl.BlockSpec((TM, N_HEADS * HEAD_PAD), row),
            pl.BlockSpec((TM, N_HEADS * HEAD_PAD), row),
            pl.BlockSpec((TM, N_HEADS * HEAD_PAD), row),
        ],
        out_shape=[
            jax.ShapeDtypeStruct((D_SSM, N_TOT), F32),
            jax.ShapeDtypeStruct((N_TOT, N_HEADS * HEAD_PAD), BF16),
            jax.ShapeDtypeStruct((N_TOT, N_HEADS * HEAD_PAD), BF16),
            jax.ShapeDtypeStruct((N_TOT, N_HEADS * HEAD_PAD), BF16),
        ],
        compiler_params=_cparams("parallel"),
        name="in_proj",
    )(*xs, mod, g_mix, wut, win, qg, kvg, wq1, wq2, wkk, wv, cq_t, sq_t, cs_t)


def _attn_kernel(*refs, latent_pieces, tq):
    if latent_pieces:
        q_ref, k_ref, v_ref, kc_ref, vc_ref, o_ref, s_scr = refs
    else:
        q_ref, kc_ref, vc_ref, _, o_ref, s_scr = refs
    heads = [slice(hh * HEAD_PAD, (hh + 1) * HEAD_PAD) for hh in range(2)]
    qs = [q_ref[:, sl] for sl in heads]

    def put_scores(slot, k_at, width):
        for hh in range(2):
            s_scr[slot, hh, :, :width] = lax.dot_general(
                qs[hh], k_at(heads[hh]), (((1,), (1,)), ((), ())), preferred_element_type=F32)

    def consume(carry, slot, v_at, width):
        new = []
        for hh in range(2):
            m, acc = carry[hh]
            s = s_scr[slot, hh, :, :width]
            m_new = jnp.maximum(m, jnp.max(s, axis=-1, keepdims=True))
            alpha = jnp.exp2(m - m_new)
            p = jnp.exp2(s - m_new).astype(BF16)
            acc = alpha * acc + jnp.dot(p, v_at(heads[hh]), preferred_element_type=F32)
            new.append((m_new, acc))
        return tuple(new)

    def piece(kr, vr, start, size):
        return (lambda sl: kr[start:start + size, sl]), (lambda sl: vr[start:start + size, sl]), size

    pieces = [piece(kc_ref, vc_ref, 0, CTX_LEN)]
    start = 0
    for size in latent_pieces:
        pieces.append(piece(k_ref, v_ref, start, size))
        start += size
    carry = tuple((jnp.full((tq, 1), -jnp.inf, F32), jnp.zeros((tq, HEAD_PAD), F32))
                  for _ in range(2))
    put_scores(0, pieces[0][0], pieces[0][2])
    for i, (_, v_at, size) in enumerate(pieces):
        if i + 1 < len(pieces):
            put_scores((i + 1) % 2, pieces[i + 1][0], pieces[i + 1][2])
        carry = consume(carry, i % 2, v_at, size)
    outs = [acc[:, :V_HEAD] / acc[:, V_HEAD:V_HEAD + 1] for _, acc in carry]
    o_ref[...] = jnp.concatenate(outs, axis=-1).astype(o_ref.dtype)


def _attn_latent_call(q, k, v):
    qt = SEQ // TQ
    ctx0 = N_LAT // CTX_LEN
    return pl.pallas_call(
        functools.partial(_attn_kernel, latent_pieces=KEY_PIECES, tq=TQ),
        grid=(BATCH, N_HEADS // 2, qt),
        in_specs=[
            pl.BlockSpec((TQ, 2 * HEAD_PAD), lambda b, h, i: (b * qt + i, h)),
            pl.BlockSpec((SEQ, 2 * HEAD_PAD), lambda b, h, i: (b, h)),
            pl.BlockSpec((SEQ, 2 * HEAD_PAD), lambda b, h, i: (b, h)),
            pl.BlockSpec((CTX_LEN, 2 * HEAD_PAD), lambda b, h, i: (ctx0 + b, h)),
            pl.BlockSpec((CTX_LEN, 2 * HEAD_PAD), lambda b, h, i: (ctx0 + b, h)),
        ],
        out_specs=pl.BlockSpec((TQ, 2 * V_HEAD), lambda b, h, i: (b * qt + i, h)),
        out_shape=jax.ShapeDtypeStruct((N_TOT, D_ATT), BF16),
        scratch_shapes=[pltpu.VMEM((2, 2, TQ, max(KEY_PIECES)), F32)],
        compiler_params=_cparams("parallel", "parallel", "arbitrary"),
        name="attn_latent",
    )(q, k, v, k, v)


def _attn_ctx_call(q, k, v, y_att):
    ctx0 = N_LAT // CTX_LEN
    return pl.pallas_call(
        functools.partial(_attn_kernel, latent_pieces=(), tq=CTX_LEN),
        grid=(BATCH, N_HEADS // 2),
        in_specs=[
            pl.BlockSpec((CTX_LEN, 2 * HEAD_PAD), lambda b, h: (ctx0 + b, h)),
            pl.BlockSpec((CTX_LEN, 2 * HEAD_PAD), lambda b, h: (ctx0 + b, h)),
            pl.BlockSpec((CTX_LEN, 2 * HEAD_PAD), lambda b, h: (ctx0 + b, h)),
            pl.BlockSpec(memory_space=pl.ANY),
        ],
        out_specs=pl.BlockSpec((CTX_LEN, 2 * V_HEAD), lambda b, h: (ctx0 + b, h)),
        out_shape=jax.ShapeDtypeStruct((N_TOT, D_ATT), BF16),
        input_output_aliases={3: 0},
        scratch_shapes=[pltpu.VMEM((1, 2, CTX_LEN, CTX_LEN), F32)],
        compiler_params=_cparams("parallel", "parallel"),
        name="attn_ctx",
    )(q, k, v, y_att)


SSM_C = 64
SSM_HALVES = SSM_T // SSM_C
SSM_KC = SSM_GROUP * SSM_C
LAG_ROWS = 2 * SSM_C
PT_LAG, PT_INC, PT_OUT = 0, LAG_ROWS, LAG_ROWS + SSM_C
PT_ROWS = LAG_ROWS + 2 * SSM_C
NT_DIMS = (((1,), (1,)), ((), ()))


def _ssm_kernel(u_ref, pta_ref, ptb_ref, rows_ref, cc_ref, at_ref, dv_ref, y_ref,
                abt_hi_scr, abt_lo_scr, wl_scr, m_scr, ws_scr, wct_scr, s_scr, h_scr):
    H, C = SSM_GROUP, SSM_C
    half_lane = lax.broadcasted_iota(jnp.int32, (1, LANES), 1) < C

    def split(v):
        hi = v.astype(BF16)
        return hi, (v - hi.astype(F32)).astype(BF16)

    def scaled(row0, n_rows, ra, rb):
        return pta_ref[0, row0:row0 + n_rows, :] * ra + ptb_ref[0, row0:row0 + n_rows, :] * rb

    def build_tables(i, _):
        row0 = pl.multiple_of(i * C, C)
        b_re = rows_ref[0, pl.ds(i, 1), :]
        b_im = rows_ref[0, pl.ds(H + i, 1), :]
        lag_rows = pl.ds(pl.multiple_of(i * LAG_ROWS, LAG_ROWS), LAG_ROWS)
        abt_hi_scr[lag_rows, :], abt_lo_scr[lag_rows, :] = split(
            scaled(PT_LAG, LAG_ROWS, b_re, b_im))
        ws_scr[pl.ds(row0, C), :] = scaled(PT_INC, C, b_re, b_im).astype(BF16)
        c_re = rows_ref[0, pl.ds(2 * H + i, 1), :]
        c_im = rows_ref[0, pl.ds(3 * H + i, 1), :]
        wct_scr[pl.ds(row0, C), :] = scaled(PT_OUT, C, c_re, c_im).astype(BF16)
        return 0

    lax.fori_loop(0, H, build_tables, 0)
    cc_hi, cc_lo = split(cc_ref[0])
    nt = lambda a, b: lax.dot_general(a, b, NT_DIMS, preferred_element_type=F32)
    wl = nt(cc_hi, abt_hi_scr[...]) + nt(cc_hi, abt_lo_scr[...]) + nt(cc_lo, abt_hi_scr[...])
    wl_scr[0] = wl
    wl_scr[1] = jnp.concatenate(
        [pltpu.roll(wl[:, ci * LAG_ROWS:(ci + 1) * LAG_ROWS], C, 1) for ci in range(H)], axis=1)

    slot2 = lax.broadcasted_iota(jnp.int32, (1, 2 * LANES), 1)
    first_low = (slot2 < C) | (slot2 > 2 * LANES - C)
    first_high = slot2 < LANES

    def toeplitz_pair(x):
        return pltpu.roll(jnp.broadcast_to(x, (C, 2 * LANES)), 0, 1, stride=1, stride_axis=0)

    def build_toeplitz(ci, _):
        row0 = pl.multiple_of(ci * C, C)
        slots = pl.ds(pl.multiple_of(ci * LAG_ROWS, LAG_ROWS), LAG_ROWS)

        def lags(co, shifted):
            v = wl_scr[int(shifted), co:co + 1, slots]
            return jnp.concatenate([v, v], axis=1)

        for ka in range(0, H // 2, 2):
            kb = ka + 1
            low = toeplitz_pair(jnp.where(first_low, lags(2 * ka, False), lags(2 * kb, False)))
            high = toeplitz_pair(jnp.where(first_high, lags(2 * ka + 1, True),
                                           lags(2 * kb + 1, True)))
            for k, lanes in ((ka, slice(0, LANES)), (kb, slice(LANES, 2 * LANES))):
                m_scr[pl.ds(row0, C), k * LANES:(k + 1) * LANES] = jnp.where(
                    half_lane, low[:, lanes], high[:, lanes]).astype(BF16)
        return 0

    lax.fori_loop(0, H, build_toeplitz, 0)

    def chunk_operand(hf):
        cols = []
        for k in range(H // 2):
            a, b = u_ref[2 * k], u_ref[2 * k + 1]
            if hf == 0:
                cols.append(jnp.where(half_lane, a, pltpu.roll(b, C, 1)))
            else:
                cols.append(jnp.where(half_lane, pltpu.roll(a, C, 1), b))
        return jnp.concatenate(cols, axis=1).astype(BF16)

    ys = []
    for hf in range(SSM_HALVES):
        u = chunk_operand(hf)
        ys.append(jnp.dot(u, m_scr[...], preferred_element_type=F32))
        s = jnp.dot(u, ws_scr[...], preferred_element_type=F32)
        for d in range(2):
            s_d = s[:, d * LANES:(d + 1) * LANES]
            s_scr[hf, d] = s_d
            s_scr[hf, 2 + d] = pltpu.roll(s_d, SSM_STATE, 1)

    n_lat, n_ctx = SEQ // SSM_T, CTX_LEN // SSM_T
    ctx = [((N_CHUNK_LAT + c, n_ctx), hf) for c in range(n_ctx) for hf in range(SSM_HALVES)]
    lat = [((k, n_lat), hf) for k in range(n_lat) for hf in range(SSM_HALVES)]
    coef = [jnp.broadcast_to(at_ref[0, r:r + 1, :], (BATCH, 2 * SSM_STATE)) for r in range(4)]

    def advance(h, h_sw, d, rows, hf):
        a0, a1 = coef[2 * d], coef[2 * d + 1]
        return (h * a0 + h_sw * a1 + s_scr[hf, d, rows, :],
                h_sw * a0 - h * a1 + s_scr[hf, 2 + d, rows, :])

    zero = jnp.zeros((BATCH, 2 * SSM_STATE), F32)
    h_f, h_f_sw, h_r, h_r_sw = zero, zero, zero, zero
    for ((sf, stf), cf), ((sr, strd), cr) in zip(ctx + lat, ctx[::-1] + lat[::-1]):
        rows_f = pl.ds(sf, BATCH, stride=stf)
        rows_r = pl.ds(sr, BATCH, stride=strd)
        h_scr[cf, 0, rows_f, :] = h_f
        h_f, h_f_sw = advance(h_f, h_f_sw, 0, rows_f, cf)
        h_scr[cr, 1, rows_r, :] = h_r
        h_r, h_r_sw = advance(h_r, h_r_sw, 1, rows_r, cr)

    for hf in range(SSM_HALVES):
        h_in = jnp.concatenate([h_scr[hf, 0], h_scr[hf, 1]], axis=-1).astype(BF16)
        ys[hf] = ys[hf] + lax.dot_general(h_in, wct_scr[...], NT_DIMS,
                                          preferred_element_type=F32)
    for k in range(H // 2):
        y0, y1 = ys[0][:, k * LANES:(k + 1) * LANES], ys[1][:, k * LANES:(k + 1) * LANES]
        for c, yc in ((2 * k, jnp.where(half_lane, y0, pltpu.roll(y1, C, 1))),
                      (2 * k + 1, jnp.where(half_lane, pltpu.roll(y0, C, 1), y1))):
            y_ref[c] = yc + u_ref[c] * dv_ref[0, :, c * SSM_T:(c + 1) * SSM_T]


def _ssm_call(u_t, pta, ptb, rows, cc, at, dv):
    g3 = lambda g: (g, 0, 0)
    return pl.pallas_call(
        _ssm_kernel,
        grid=(N_SSM_GROUPS,),
        in_specs=[
            pl.BlockSpec((SSM_GROUP, N_CHUNK, SSM_T), g3),
            pl.BlockSpec((1, PT_ROWS, 4 * SSM_STATE), g3),
            pl.BlockSpec((1, PT_ROWS, 4 * SSM_STATE), g3),
            pl.BlockSpec((1, 4 * SSM_GROUP, 4 * SSM_STATE), g3),
            pl.BlockSpec((1, SSM_GROUP, 4 * SSM_STATE), g3),
            pl.BlockSpec((1, 4, 2 * SSM_STATE), g3),
            pl.BlockSpec((1, 1, SSM_K), g3),
        ],
        out_specs=pl.BlockSpec((SSM_GROUP, N_CHUNK, SSM_T), g3),
        out_shape=jax.ShapeDtypeStruct((D_SSM, N_CHUNK, SSM_T), F32),
        scratch_shapes=[
            pltpu.VMEM((SSM_GROUP * LAG_ROWS, 4 * SSM_STATE), BF16),
            pltpu.VMEM((SSM_GROUP * LAG_ROWS, 4 * SSM_STATE), BF16),
            pltpu.VMEM((2, SSM_GROUP, SSM_GROUP * LAG_ROWS), F32),
            pltpu.VMEM((SSM_KC, SSM_KC), BF16),
            pltpu.VMEM((SSM_KC, 4 * SSM_STATE), BF16),
            pltpu.VMEM((SSM_KC, 4 * SSM_STATE), BF16),
            pltpu.VMEM((SSM_HALVES, 4, N_CHUNK, 2 * SSM_STATE), F32),
            pltpu.VMEM((SSM_HALVES, 2, N_CHUNK, 2 * SSM_STATE), F32),
        ],
        compiler_params=_cparams("parallel"),
        name="s5_mixer",
    )(u_t, pta, ptb, rows, cc, at, dv)


def _ssm_tables(a_re, a_im, log_dt, b_re, b_im, c_re, c_im, d_skip):
    G, P, H, T = N_SSM_GROUPS, SSM_STATE, SSM_GROUP, SSM_T
    a_re, a_im = a_re.astype(F32), a_im.astype(F32)
    dt = jnp.exp(log_dt.astype(F32))[..., None]
    den = a_re * a_re + a_im * a_im
    mag1 = jnp.exp(dt * a_re)
    ab_re, ab_im = mag1 * jnp.cos(dt * a_im), mag1 * jnp.sin(dt * a_im)
    num_re = ab_re - 1.0
    f_re = (num_re * a_re + ab_im * a_im) / den
    f_im = (ab_im * a_re - num_re * a_im) / den
    b_re, b_im = b_re.astype(F32), b_im.astype(F32)
    bb_re = f_re[..., None] * b_re - f_im[..., None] * b_im
    bb_im = f_re[..., None] * b_im + f_im[..., None] * b_re
    c_re, c_im = c_re.astype(F32), c_im.astype(F32)
    la, th = dt * a_re, dt * a_im

    def powers(d, n):
        mag = jnp.exp(la[d][:, None, :] * n[None, :, None])
        ph = th[d][:, None, :] * n[None, :, None]
        return mag * jnp.cos(ph), mag * jnp.sin(ph)

    C = SSM_C
    slot = np.arange(LAG_ROWS)
    lag_f = np.where(slot < C, slot, -1)
    lag_r = np.where((-slot) % LAG_ROWS < C, (-slot) % LAG_ROWS, -1)
    s_idx = np.arange(C)
    f_all = np.concatenate([lag_f, C - 1 - s_idx, s_idx + 1])
    r_all = np.concatenate([lag_r, s_idx, C - s_idx])
    f_exp, r_exp = jnp.asarray(np.maximum(f_all, 0), F32), jnp.asarray(np.maximum(r_all, 0), F32)
    f_on, r_on = jnp.asarray(f_all >= 0, F32), jnp.asarray(r_all >= 0, F32)
    lane = jnp.arange(4 * P)
    is_fwd = (lane < 2 * P)[None, :]
    is_re = ((lane // P) % 2 == 0)[None, None, :]
    expo = jnp.where(is_fwd, f_exp[:, None], r_exp[:, None])[None]
    on = jnp.where(is_fwd, f_on[:, None], r_on[:, None])[None]
    la4 = jnp.concatenate([la[0], la[0], la[1], la[1]], axis=-1)[:, None, :]
    th4 = jnp.concatenate([th[0], th[0], th[1], th[1]], axis=-1)[:, None, :]
    mag = jnp.exp(la4 * expo) * on
    p_re, p_im = mag * jnp.cos(th4 * expo), mag * jnp.sin(th4 * expo)
    pta = jnp.where(is_re, p_re, p_im)
    ptb = jnp.where(is_re, -p_im, p_re)

    def per_channel(v):
        f, r = v[0].transpose(0, 2, 1), v[1].transpose(0, 2, 1)
        return jnp.concatenate([f, f, r, r], axis=-1)
    cf_re, cr_re = c_re[0], c_re[1]
    cf_im, cr_im = c_im[0], c_im[1]
    rows = jnp.concatenate([
        per_channel(bb_re), per_channel(bb_im),
        jnp.concatenate([cf_re, -cf_re, cr_re, -cr_re], axis=-1),
        jnp.concatenate([cf_im, -cf_im, cr_im, -cr_im], axis=-1)], axis=1)
    cc = jnp.concatenate([cf_re, -cf_im, cr_re, -cr_im], axis=-1)

    t_exp = jnp.full((1,), float(SSM_C), F32)
    (f_re_t, f_im_t), (r_re_t, r_im_t) = powers(0, t_exp), powers(1, t_exp)
    f_re_t, f_im_t, r_re_t, r_im_t = (v[:, 0, :] for v in (f_re_t, f_im_t, r_re_t, r_im_t))
    at = jnp.stack([jnp.concatenate([f_re_t, f_re_t], -1), jnp.concatenate([-f_im_t, f_im_t], -1),
                    jnp.concatenate([r_re_t, r_re_t], -1), jnp.concatenate([-r_im_t, r_im_t], -1)],
                   axis=1)
    dv = jnp.repeat(d_skip.astype(F32).reshape(G, H), T, axis=-1).reshape(G, 1, H * T)
    return pta, ptb, rows, cc, at, dv


ROUTE_E1, ROUTE_E2, ROUTE_W1, ROUTE_W2 = 0, 1, 2, 3


def _top2_route(logits):
    lane = lax.broadcasted_iota(jnp.int32, logits.shape, 1)
    lg = jnp.where(lane < N_EXPERTS, logits, -jnp.inf)
    m1 = jnp.max(lg, axis=-1, keepdims=True)
    i1 = jnp.min(jnp.where(lg == m1, lane, LANES), axis=-1, keepdims=True)
    lg2 = jnp.where(lane == i1, -jnp.inf, lg)
    m2 = jnp.max(lg2, axis=-1, keepdims=True)
    i2 = jnp.min(jnp.where(lg2 == m2, lane, LANES), axis=-1, keepdims=True)
    e2 = jnp.exp(m2 - m1)
    w1 = 1.0 / (1.0 + e2)
    rec = jnp.where(lane == ROUTE_E1, i1.astype(F32), 0.0)
    rec = jnp.where(lane == ROUTE_E2, i2.astype(F32), rec)
    rec = jnp.where(lane == ROUTE_W1, w1, rec)
    return jnp.where(lane == ROUTE_W2, e2 * w1, rec), i1, i2


def _expert_slots(i1, i2, taken):
    n = i1.shape[0]
    lane = lax.broadcasted_iota(jnp.int32, (n, LANES), 1)
    picked = jnp.where((lane == i1) | (lane == i2), 1.0, 0.0)
    earlier = (lax.broadcasted_iota(jnp.int32, (n, n), 1)
               < lax.broadcasted_iota(jnp.int32, (n, n), 0)).astype(BF16)
    rank = jnp.dot(earlier, picked.astype(BF16), preferred_element_type=F32)
    slot = rank + taken + (lane * EXPERT_CAP).astype(F32)
    d1 = jnp.sum(jnp.where(lane == i1, slot, 0.0), axis=-1, keepdims=True)
    d2 = jnp.sum(jnp.where(lane == i2, slot, 0.0), axis=-1, keepdims=True)
    dest = jnp.where(lane == 0, d1, jnp.where(lane == 1, d2, 0.0)).astype(jnp.int32)
    return dest, jnp.sum(picked, axis=0, keepdims=True)


def _pack_bf16_pairs(v):
    k = v.shape[1] // 2
    bits = pltpu.bitcast(v.astype(BF16).astype(F32), jnp.uint32)
    return (bits[:, :k] & jnp.uint32(0xFFFF0000)) | (bits[:, k:] >> 16)


def _unpack_bf16_pairs(w):
    hi = pltpu.bitcast(w & jnp.uint32(0xFFFF0000), F32)
    lo = pltpu.bitcast(w << 16, F32)
    return hi, lo


def _store_packed(ref, v):
    words = _pack_bf16_pairs(v)
    for s in range(SC_SPLIT):
        ref[s] = words[:, s * SC_ROW:(s + 1) * SC_ROW]


def _load_packed(ref):
    return _unpack_bf16_pairs(jnp.concatenate([ref[s] for s in range(SC_SPLIT)], axis=-1))


def _mix_kernel(*refs, with_router, n_x):
    rest = refs[n_x:]
    if with_router:
        (yst_ref, ya_ref, mod_ref, wglut_ref, bglu_ref, wout_ref, gffn_ref, router_ref,
         x1_ref, h2_ref, gate_ref, dest_ref, cnt_ref, taken_scr) = rest
    else:
        (yst_ref, ya_ref, mod_ref, wglut_ref, bglu_ref, wout_ref, gffn_ref,
         w1_ref, w3_ref, w2_ref, x2_ref) = rest
    zt = jax.nn.gelu(yst_ref[...], approximate=True)
    glt = zt * jax.nn.sigmoid(
        jnp.dot(wglut_ref[...], zt.astype(BF16), preferred_element_type=F32) + bglu_ref[...])
    mix = jnp.concatenate([glt.T.astype(BF16), ya_ref[...]], axis=-1)
    o = jnp.dot(mix, wout_ref[...], preferred_element_type=F32)
    x1 = _tile_rows(refs, n_x) + mod_ref[0, 2:3, :] * o
    h2 = _rms(x1, gffn_ref[...]) * (1.0 + mod_ref[0, 4:5, :]) + mod_ref[0, 3:4, :]
    if not with_router:
        acc = _swiglu(h2.astype(BF16), lambda sl: w1_ref[:, sl], lambda sl: w3_ref[:, sl],
                      lambda sl: w2_ref[sl, :], D_FF)
        x2_ref[...] = x1 + mod_ref[0, 5:6, :] * acc
    else:
        x1_ref[...] = x1
        _store_packed(h2_ref, h2)
        h_hi = h2.astype(BF16)
        h_lo = (h2 - h_hi.astype(F32)).astype(BF16)
        o1 = jnp.dot(h_hi, router_ref[0], preferred_element_type=F32)
        o2 = jnp.dot(h_lo, router_ref[1], preferred_element_type=F32)
        logits = o1 + pltpu.roll(o1, LANES - N_EXPERTS, 1) + o2
        gate_ref[...], i1, i2 = _top2_route(logits)

        @pl.when(pl.program_id(0) == 0)
        def _():
            taken_scr[...] = jnp.zeros_like(taken_scr)

        dest_ref[...], tile_cnt = _expert_slots(i1, i2, taken_scr[...])
        taken_scr[...] += tile_cnt
        cnt_ref[...] = jnp.broadcast_to(taken_scr[...], cnt_ref.shape)


def _mix_call(xs, y_ssm_t, y_att, mod, wglut, bglu, wout, gffn, *, router=None, ffn=None, n_tiles):
    row = lambda i: (i, 0)
    out_row = row
    with_router = router is not None
    assert with_router != (ffn is not None)
    in_specs = _x_specs(len(xs)) + [
        pl.BlockSpec((D_SSM, TM), lambda i: (0, i)),
        pl.BlockSpec((TM, D_ATT), row),
        pl.BlockSpec((1, 6, D_MODEL), lambda i: (_mod_row(i), 0, 0)),
        _const_spec((D_SSM, D_SSM)),
        _const_spec((D_SSM, 1)),
        _const_spec((D_MIX, D_MODEL)),
        _const_spec((1, D_MODEL)),
    ]
    args = [*xs, y_ssm_t, y_att, mod, wglut, bglu, wout, gffn]
    out_specs = [pl.BlockSpec((TM, D_MODEL), out_row)]
    out_shape = [jax.ShapeDtypeStruct((n_tiles * TM, D_MODEL), F32)]
    if with_router:
        in_specs.append(_const_spec((2, D_MODEL, LANES)))
        args.append(router)
        out_specs += [pl.BlockSpec((SC_SPLIT, TM, SC_ROW), lambda i: (0, i, 0)),
                      pl.BlockSpec((TM, LANES), out_row),
                      pl.BlockSpec((TM, LANES), out_row),
                      pl.BlockSpec((8, LANES), lambda i: (0, 0))]
        out_shape += [jax.ShapeDtypeStruct((SC_SPLIT, n_tiles * TM, SC_ROW), jnp.uint32),
                      jax.ShapeDtypeStruct((n_tiles * TM, LANES), F32),
                      jax.ShapeDtypeStruct((n_tiles * TM, LANES), jnp.int32),
                      jax.ShapeDtypeStruct((8, LANES), F32)]
        scratch = [pltpu.VMEM((1, LANES), F32)]
    else:
        in_specs += [_const_spec((D_MODEL, D_FF)), _const_spec((D_MODEL, D_FF)),
                     _const_spec((D_FF, D_MODEL))]
        args += list(ffn)
        scratch = []
    return pl.pallas_call(
        functools.partial(_mix_kernel, with_router=with_router, n_x=len(xs)),
        grid=(n_tiles,),
        in_specs=in_specs,
        out_specs=out_specs,
        out_shape=out_shape,
        scratch_shapes=scratch,
        compiler_params=_cparams("arbitrary" if with_router else "parallel"),
        name="mix_out",
    )(*args)


def _swiglu(h, w1_at, w3_at, w2_at, d_ff):
    acc = jnp.zeros((h.shape[0], D_MODEL), F32)
    for lo in range(0, d_ff, FF_CHUNK):
        sl = slice(lo, min(lo + FF_CHUNK, d_ff))
        a = jnp.dot(h, w1_at(sl), preferred_element_type=F32)
        b = jnp.dot(h, w3_at(sl), preferred_element_type=F32)
        g = (a * jax.nn.sigmoid(a) * b).astype(BF16)
        acc = acc + jnp.dot(g, w2_at(sl), preferred_element_type=F32)
    return acc


EXPERT_BLK = 512
EXPERT_CAP = N_LAT
N_SORTED = N_EXPERTS * EXPERT_CAP
N_EXPERT_BLKS = 2 * N_LAT // EXPERT_BLK + N_EXPERTS
PACKED = D_MODEL // 2
SC_ROW = 256
SC_SPLIT = PACKED // SC_ROW
SC_WIN = 128


def _sc_mesh():
    return plsc.VectorSubcoreMesh(core_axis_name="core", subcore_axis_name="subcore")


def _sc_scatter(x, idx_a, idx_b, n_out):
    n = x.shape[0]

    @pl.kernel(out_type=jax.ShapeDtypeStruct((n_out, SC_ROW), x.dtype), mesh=_sc_mesh(),
               scratch_types=[])
    def scatter(x_hbm, a_hbm, b_hbm, o_hbm):
        def body(x_vmem, a_vmem, b_vmem):
            pltpu.sync_copy(x_vmem, o_hbm.at[a_vmem.at[0]])
            pltpu.sync_copy(x_vmem, o_hbm.at[b_vmem.at[0]])

        pltpu.emit_pipeline(
            body, grid=(n // SC_WIN,),
            in_specs=[pl.BlockSpec((SC_WIN, SC_ROW), lambda i: (i, 0)),
                      pl.BlockSpec((1, SC_WIN), lambda i: (0, i)),
                      pl.BlockSpec((1, SC_WIN), lambda i: (0, i))],
            out_specs=[],
            core_axis_name=("core", "subcore"),
            dimension_semantics=(pltpu.PARALLEL,),
        )(x_hbm, a_hbm, b_hbm)

    return scatter(x, idx_a.reshape(1, n), idx_b.reshape(1, n))


def _sc_gather(y, idx):
    n = idx.shape[0]

    @pl.kernel(out_type=jax.ShapeDtypeStruct((n, SC_ROW), y.dtype), mesh=_sc_mesh(),
               scratch_types=[])
    def gather(y_hbm, i_hbm, o_hbm):
        def body(i_vmem, o_vmem):
            pltpu.sync_copy(y_hbm.at[i_vmem.at[0]], o_vmem)

        pltpu.emit_pipeline(
            body, grid=(n // SC_WIN,),
            in_specs=[pl.BlockSpec((1, SC_WIN), lambda i: (0, i))],
            out_specs=[pl.BlockSpec((SC_WIN, SC_ROW), lambda i: (i, 0))],
            core_axis_name=("core", "subcore"),
            dimension_semantics=(pltpu.PARALLEL,),
        )(i_hbm, o_hbm)

    return gather(y, idx.reshape(1, n))


W_CAST_ROWS = 128


def _experts_kernel(blk_expert_ref, blk_row_ref, n_used_ref, x_ref, w1_ref, w3_ref, w2_ref, o_ref,
                    w1_scr, w3_scr, w2_scr):
    del blk_row_ref
    b = pl.program_id(0)
    live = b < n_used_ref[0]
    new_expert = (b == 0) | (blk_expert_ref[b] != blk_expert_ref[jnp.maximum(b - 1, 0)])

    @pl.when(live & new_expert)
    def _():
        for src, dst in ((w1_ref, w1_scr), (w3_ref, w3_scr), (w2_ref, w2_scr)):
            for r in range(0, dst.shape[0], W_CAST_ROWS):
                dst[r:r + W_CAST_ROWS, :] = src[0, r:r + W_CAST_ROWS, :].astype(BF16)

    @pl.when(live)
    def _():
        hi, lo = _load_packed(x_ref)
        h = jnp.concatenate([hi.astype(BF16), lo.astype(BF16)], axis=-1)
        y = _swiglu(h, lambda sl: w1_scr[:, sl], lambda sl: w3_scr[:, sl],
                    lambda sl: w2_scr[sl, :], D_FF_EXPERT)
        _store_packed(o_ref, y)


def _experts_call(blk_expert, blk_row, n_used, xs, w1, w3, w2):
    row = lambda b, be, br, nu: (0, br[b], 0)
    wsel = lambda b, be, br, nu: (be[b], 0, 0)
    return pl.pallas_call(
        _experts_kernel,
        grid_spec=pltpu.PrefetchScalarGridSpec(
            num_scalar_prefetch=3,
            grid=(N_EXPERT_BLKS,),
            in_specs=[
                pl.BlockSpec((SC_SPLIT, EXPERT_BLK, SC_ROW), row),
                pl.BlockSpec((1, D_MODEL, D_FF_EXPERT), wsel),
                pl.BlockSpec((1, D_MODEL, D_FF_EXPERT), wsel),
                pl.BlockSpec((1, D_FF_EXPERT, D_MODEL), wsel),
            ],
            out_specs=pl.BlockSpec((SC_SPLIT, EXPERT_BLK, SC_ROW), row),
            scratch_shapes=[pltpu.VMEM((D_MODEL, D_FF_EXPERT), BF16),
                            pltpu.VMEM((D_MODEL, D_FF_EXPERT), BF16),
                            pltpu.VMEM((D_FF_EXPERT, D_MODEL), BF16)],
        ),
        out_shape=jax.ShapeDtypeStruct((SC_SPLIT, N_SORTED, SC_ROW), jnp.uint32),
        compiler_params=_cparams("arbitrary"),
        name="moe_experts",
    )(blk_expert, blk_row, n_used, xs, w1, w3, w2)


def _combine_kernel(x1_ref, r_ref, y_ref, mod_ref, fg_ref, *rest):
    o_ref = rest[-1]

    def expert_out(slot):
        hi, lo = _load_packed(y_ref.at[slot])
        return jnp.concatenate([hi, lo], axis=-1)
    w1 = r_ref[:, ROUTE_W1:ROUTE_W1 + 1]
    w2 = r_ref[:, ROUTE_W2:ROUTE_W2 + 1]
    y = w1 * expert_out(0) + w2 * expert_out(1)
    x2 = x1_ref[...] + mod_ref[0, 5:6, :] * y
    o_ref[...] = _rms(x2, fg_ref[...])


COMBINE_PARTS = 2


def _combine_call(x1, route, y_tok, mod, fg, part, out):
    n_tiles = N_LAT // TM // COMBINE_PARTS
    tile0 = part * n_tiles
    row = lambda i: (tile0 + i, 0)
    in_specs = [
        pl.BlockSpec((TM, D_MODEL), row),
        pl.BlockSpec((TM, LANES), row),
        pl.BlockSpec((2, SC_SPLIT, TM, SC_ROW), lambda i: (0, 0, i, 0)),
        pl.BlockSpec((1, 6, D_MODEL), lambda i: ((tile0 + i) // SEQ_TILES, 0, 0)),
        pl.BlockSpec((1, D_MODEL), lambda i: (0, 0)),
    ]
    args = [x1, route, y_tok, mod, fg]
    aliases = {}
    if out is not None:
        in_specs.append(pl.BlockSpec(memory_space=pl.ANY))
        args.append(out)
        aliases = {len(args) - 1: 0}
    return pl.pallas_call(
        _combine_kernel,
        grid=(n_tiles,),
        in_specs=in_specs,
        out_specs=pl.BlockSpec((TM, D_MODEL), row),
        out_shape=jax.ShapeDtypeStruct((N_LAT, D_MODEL), F32),
        input_output_aliases=aliases,
        compiler_params=_cparams("parallel"),
        name="moe_combine",
    )(*args)


def _moe_routed(h2p, x1, route, dest, cnt, mod, w1, w3, w2, fg):
    blks = (cnt[0, :N_EXPERTS].astype(jnp.int32) + (EXPERT_BLK - 1)) // EXPERT_BLK
    blk_end = jnp.cumsum(blks)
    n_used = blk_end[-1:]
    b = jnp.minimum(jnp.arange(N_EXPERT_BLKS, dtype=jnp.int32), n_used[0] - 1)
    blk_expert = jnp.sum((b[:, None] >= blk_end[None, :]).astype(jnp.int32), axis=1)
    blk_row = blk_expert * (EXPERT_CAP // EXPERT_BLK) + b - (blk_end - blks)[blk_expert]
    piece = jnp.arange(SC_SPLIT, dtype=jnp.int32)[:, None] * N_SORTED
    idx = [(piece + dest[:, slot][None, :]).reshape(SC_SPLIT * N_LAT) for slot in range(2)]
    xs = _sc_scatter(h2p.reshape(SC_SPLIT * N_LAT, SC_ROW), idx[0], idx[1], SC_SPLIT * N_SORTED)
    ys = _experts_call(blk_expert, blk_row, n_used, xs.reshape(SC_SPLIT, N_SORTED, SC_ROW),
                       w1, w3, w2)
    ys_flat = ys.reshape(SC_SPLIT * N_SORTED, SC_ROW)
    n_part = N_LAT // COMBINE_PARTS
    out = None
    for part in range(COMBINE_PARTS):
        tokens = slice(part * n_part, (part + 1) * n_part)
        idx_part = jnp.concatenate(
            [(piece + dest[tokens, slot][None, :]).reshape(SC_SPLIT * n_part) for slot in range(2)])
        y_tok = _sc_gather(ys_flat, idx_part).reshape(2, SC_SPLIT, n_part, SC_ROW)
        out = _combine_call(x1, route, y_tok, mod, fg, part, out)
    return out


def _rope_partner_perm():
    perm, sign = [], []
    for j in range(QK_ROPE):
        first_half = (j % AXIS_ROPE) < ROPE_FREQS
        perm.append(j + ROPE_FREQS if first_half else j - ROPE_FREQS)
        sign.append(-1.0 if first_half else 1.0)
    return jnp.array(perm, jnp.int32), jnp.array(sign, F32)


def _rope_tables():
    t = jnp.arange(SEQ)
    row = (t // GRID_W).astype(F32)
    col = (t % GRID_W).astype(F32)
    inv_freq = ROPE_BASE ** (-2.0 * jnp.arange(ROPE_FREQS, dtype=F32) / AXIS_ROPE)
    ang = jnp.concatenate([row[:, None] * inv_freq, row[:, None] * inv_freq,
                           col[:, None] * inv_freq, col[:, None] * inv_freq], axis=1)
    cos = jnp.concatenate([jnp.cos(ang), jnp.ones((N_CTX, QK_ROPE), F32)], axis=0)
    sin = jnp.concatenate([jnp.sin(ang), jnp.zeros((N_CTX, QK_ROPE), F32)], axis=0)
    n = N_CTX + SEQ
    pad32 = jnp.zeros((n, HEAD_PAD - D_QK), F32)
    qs = ATT_SCALE * math.log2(math.e)
    cq = jnp.concatenate([jnp.full((n, QK_NOPE), qs, F32), qs * cos, pad32], axis=1)
    sq = jnp.concatenate([jnp.zeros((n, QK_NOPE), F32), qs * sin, pad32], axis=1)
    cs = jnp.concatenate([cos, sin, jnp.zeros((n, LANES - 2 * QK_ROPE), F32)], axis=1)
    return cq, sq, cs


def _layer_weights(w_in, w_uq, w_ukv):
    perm, sign = _rope_partner_perm()
    s0 = D_SSM + Q_LORA + KV_LORA
    kr_w = w_in[:, s0:s0 + QK_ROPE]
    wut = w_in[:, :D_SSM].T.astype(BF16)
    win = jnp.concatenate([w_in[:, D_SSM:s0], kr_w, kr_w[:, perm] * sign,
                           jnp.zeros((D_MODEL, LANES - 2 * QK_ROPE), F32)], axis=1).astype(BF16)
    uq = w_uq.reshape(Q_LORA, N_HEADS, D_QK)
    nope, rope = uq[..., :QK_NOPE], uq[..., QK_NOPE:]
    zpad = jnp.zeros((Q_LORA, N_HEADS, HEAD_PAD - D_QK), F32)
    wq1 = jnp.concatenate([nope, rope, zpad], axis=-1).reshape(Q_LORA, N_HEADS * HEAD_PAD).astype(BF16)
    wq2 = jnp.concatenate([jnp.zeros_like(nope), rope[..., perm] * sign, zpad], axis=-1)
    wq2 = wq2.reshape(Q_LORA, N_HEADS * HEAD_PAD).astype(BF16)
    ukv = w_ukv.reshape(KV_LORA, N_HEADS, QK_NOPE + V_HEAD)
    wk = jnp.concatenate([ukv[..., :QK_NOPE], jnp.zeros((KV_LORA, N_HEADS, HEAD_PAD - QK_NOPE), F32)],
                         axis=-1).reshape(KV_LORA, N_HEADS * HEAD_PAD)
    eye = jnp.eye(QK_ROPE, dtype=F32)
    place = jnp.concatenate([jnp.zeros((QK_ROPE, QK_NOPE), F32), eye,
                             jnp.zeros((QK_ROPE, HEAD_PAD - D_QK), F32)], axis=1)
    place = jnp.tile(place, (1, N_HEADS))
    spread = jnp.concatenate([place, place, jnp.zeros((LANES - 2 * QK_ROPE, N_HEADS * HEAD_PAD), F32)], 0)
    wkk = jnp.concatenate([wk, spread], axis=0).astype(BF16)
    wv = jnp.concatenate([ukv[..., QK_NOPE:], jnp.zeros((KV_LORA, N_HEADS, HEAD_PAD - V_HEAD), F32)],
                         axis=-1).reshape(KV_LORA, N_HEADS * HEAD_PAD).astype(BF16)
    return wut, win, wq1, wq2, wkk, wv


def kernel(x, c, ctx, c_ctx, w_ada, b_ada, norm_mix, norm_ffn, w_in, q_norm, kv_norm, w_uq, w_ukv,
           ssm_a_re, ssm_a_im, ssm_log_dt, ssm_b_re, ssm_b_im, ssm_c_re, ssm_c_im, ssm_d, w_glu,
           b_glu, w_out, ffn_w1, ffn_w3, ffn_w2, moe_router, moe_w1, moe_w3, moe_w2, final_norm):
    assert x.shape == (BATCH, SEQ, D_MODEL) and ctx.shape == (BATCH, CTX_LEN, D_MODEL)
    cond = jnp.concatenate([c, c_ctx[None, :], jnp.zeros((MOD_ROWS - BATCH - 1, D_MODEL), F32)], axis=0)
    mod_all = _ada_call(cond, w_ada, b_ada).reshape(DEPTH, MOD_ROWS, 6, D_MODEL)
    cq_t, sq_t, cs_t = _rope_tables()
    xs = (x.reshape(N_LAT, D_MODEL), ctx.reshape(N_CTX, D_MODEL))

    out = None
    for i in range(DEPTH):
        last = i == DEPTH - 1
        mod = mod_all[i]
        wut, win, wq1, wq2, wkk, wv = _layer_weights(w_in[i], w_uq[i], w_ukv[i])
        u_t, q, k, v = _inproj_call(xs, mod, norm_mix[i][None, :], wut, win, q_norm[i][None, :],
                                    kv_norm[i][None, :], wq1, wq2, wkk, wv, cq_t, sq_t, cs_t)
        tabs = _ssm_tables(ssm_a_re[i], ssm_a_im[i], ssm_log_dt[i], ssm_b_re[i], ssm_b_im[i],
                           ssm_c_re[i], ssm_c_im[i], ssm_d[i])
        y_ssm_t = _ssm_call(u_t.reshape(D_SSM, N_CHUNK, SSM_T), *tabs).reshape(D_SSM, N_TOT)
        y_att = _attn_latent_call(q, k, v)
        if last:
            n_tiles = LAT_TILES
        else:
            y_att = _attn_ctx_call(q, k, v, y_att)
            n_tiles = N_TOT // TM
        j = i // 2
        mix_args = (xs, y_ssm_t, y_att, mod, w_glu[i].T.astype(BF16), b_glu[i][:, None],
                    w_out[i].astype(BF16), norm_ffn[i][None, :])
        if i % 2 == 0:
            assert not last
            ffn = (ffn_w1[j].astype(BF16), ffn_w3[j].astype(BF16), ffn_w2[j].astype(BF16))
            xs = tuple(_mix_call(*mix_args, ffn=ffn, n_tiles=n_tiles))
        else:
            assert last
            r = moe_router[j]
            r_top = lax.bitcast_convert_type(
                lax.bitcast_convert_type(r, jnp.uint32) & jnp.uint32(0xFFFF0000), F32)
            r_hi = r_top.astype(BF16)
            r_lo = (r - r_top).astype(BF16)
            zr = jnp.zeros((D_MODEL, LANES - 2 * N_EXPERTS), BF16)
            router = jnp.stack([jnp.concatenate([r_hi, r_lo, zr], axis=1),
                                jnp.concatenate([r_hi, jnp.zeros_like(r_lo), zr], axis=1)])
            x1, h2p, route, dest, cnt = _mix_call(*mix_args, router=router, n_tiles=n_tiles)
            out = _moe_routed(h2p, x1, route, dest, cnt, mod, moe_w1[j], moe_w3[j], moe_w2[j],
                              final_norm[None, :])
    return out.reshape(BATCH, SEQ, D_MODEL)
```

```python
import functools
import math

import jax
import jax.numpy as jnp
import numpy as np
from jax import lax
from jax.experimental import pallas as pl
from jax.experimental.pallas import tpu as pltpu
from jax.experimental.pallas import tpu_sc as plsc

D_MODEL = 1024
BATCH = 4
SEQ = 8192
DEPTH = 2
GRID_W = 64
CTX_LEN = 256
D_SSM = 512
SSM_GROUP = 16
N_SSM_GROUPS = D_SSM // SSM_GROUP
SSM_STATE = 64
N_HEADS = 8
QK_NOPE = 64
QK_ROPE = 32
V_HEAD = 64
Q_LORA = 256
KV_LORA = 128
D_QK = QK_NOPE + QK_ROPE
D_ATT = N_HEADS * V_HEAD
D_MIX = D_SSM + D_ATT
AXIS_ROPE = QK_ROPE // 2
ROPE_FREQS = AXIS_ROPE // 2
ROPE_BASE = 10000.0
ATT_SCALE = 1.0 / math.sqrt(D_QK)
D_FF = 2816
N_EXPERTS = 8
D_FF_EXPERT = 1408
EPS = 1e-6

N_CTX = BATCH * CTX_LEN
N_LAT = BATCH * SEQ
N_TOT = N_CTX + N_LAT

LANES = 128
HEAD_PAD = 128
TM = 512
LAT_TILES = N_LAT // TM
SEQ_TILES = SEQ // TM
TQ = 1024
KEY_PIECES = (2048, 2048, 2048, 2048)
assert sum(KEY_PIECES) == SEQ
SSM_T = 128
N_CHUNK_LAT = N_LAT // SSM_T
N_CHUNK = N_TOT // SSM_T
SSM_K = SSM_GROUP * SSM_T
FF_CHUNK = 256
D_IN_REST = Q_LORA + KV_LORA + LANES
MOD_ROWS = 8
VMEM_LIMIT = 56 * 1024 * 1024

F32 = jnp.float32
BF16 = jnp.bfloat16
HI = lax.Precision.HIGHEST


def _cparams(*sem):
    return pltpu.CompilerParams(dimension_semantics=sem, vmem_limit_bytes=VMEM_LIMIT)


def _const_spec(shape):
    nd = len(shape)
    return pl.BlockSpec(shape, lambda *_: (0,) * nd, pipeline_mode=pl.Buffered(1))


def _mod_row(i):
    return jnp.where(i < LAT_TILES, i // SEQ_TILES, BATCH)


def _pos_tile(i):
    return jnp.where(i < LAT_TILES, i % SEQ_TILES, SEQ_TILES + i - LAT_TILES)


def _rms(x, g):
    ms = jnp.mean(x * x, axis=-1, keepdims=True)
    return x * lax.rsqrt(ms + EPS) * g


ADA_TN = 1536


def _ada_kernel(c_ref, w_ref, b_ref, o_ref):
    c = c_ref[...]
    s = c * jax.nn.sigmoid(c)
    o_ref[0] = jnp.dot(s, w_ref[0], precision=HI, preferred_element_type=F32) + b_ref[0]


def _ada_call(cond, w_ada, b_ada):
    n_col = 6 * D_MODEL // ADA_TN
    return pl.pallas_call(
        _ada_kernel,
        grid=(DEPTH, n_col),
        in_specs=[
            pl.BlockSpec((MOD_ROWS, D_MODEL), lambda l, j: (0, 0)),
            pl.BlockSpec((1, D_MODEL, ADA_TN), lambda l, j: (l, 0, j)),
            pl.BlockSpec((1, 1, ADA_TN), lambda l, j: (l, 0, j)),
        ],
        out_specs=pl.BlockSpec((1, MOD_ROWS, ADA_TN), lambda l, j: (l, 0, j)),
        out_shape=jax.ShapeDtypeStruct((DEPTH, MOD_ROWS, 6 * D_MODEL), F32),
        compiler_params=_cparams("arbitrary", "arbitrary"),
        name="ada_mod",
    )(cond, w_ada, b_ada.reshape(DEPTH, 1, 6 * D_MODEL))


def _tile_rows(refs, n_x):
    if n_x == 1:
        return refs[0][...]
    return jnp.where(pl.program_id(0) < LAT_TILES, refs[0][...], refs[1][...])


def _x_specs(n_x):
    if n_x == 1:
        return [pl.BlockSpec((TM, D_MODEL), lambda i: (i, 0))]
    return [pl.BlockSpec((TM, D_MODEL), lambda i: (jnp.minimum(i, LAT_TILES - 1), 0)),
            pl.BlockSpec((TM, D_MODEL), lambda i: (jnp.maximum(i - LAT_TILES, 0), 0))]


def _inproj_kernel(*refs, n_x):
    (mod_ref, g_ref, wut_ref, win_ref, qg_ref, kvg_ref, wq1_ref, wq2_ref, wkk_ref, wv_ref,
     cq_ref, sq_ref, cs_ref, ut_ref, q_ref, k_ref, v_ref) = refs[n_x:]
    x = _tile_rows(refs, n_x)
    sh = mod_ref[0, 0:1, :]
    sc = mod_ref[0, 1:2, :]
    xm = (_rms(x, g_ref[...]) * (1.0 + sc) + sh).astype(BF16)
    ut_ref[...] = lax.dot_general(wut_ref[...], xm, (((1,), (1,)), ((), ())),
                                  preferred_element_type=F32)
    z = jnp.dot(xm, win_ref[...], preferred_element_type=F32)
    qn = _rms(z[:, :Q_LORA], qg_ref[...]).astype(BF16)
    kvn = _rms(z[:, Q_LORA:Q_LORA + KV_LORA], kvg_ref[...]).astype(BF16)
    krr = (z[:, Q_LORA + KV_LORA:] * cs_ref[...]).astype(BF16)
    q1 = jnp.dot(qn, wq1_ref[...], preferred_element_type=F32)
    q2 = jnp.dot(qn, wq2_ref[...], preferred_element_type=F32)
    cq = cq_ref[...]
    sq = sq_ref[...]
    for h in range(N_HEADS):
        sl = slice(h * HEAD_PAD, (h + 1) * HEAD_PAD)
        q_ref[:, sl] = (q1[:, sl] * cq + q2[:, sl] * sq).astype(q_ref.dtype)
    kin = jnp.concatenate([kvn, krr], axis=-1)
    k_ref[...] = jnp.dot(kin, wkk_ref[...], preferred_element_type=F32).astype(k_ref.dtype)
    vv = jnp.dot(kvn, wv_ref[...], preferred_element_type=F32)
    lane = lax.broadcasted_iota(jnp.int32, vv.shape, 1)
    v_ref[...] = jnp.where(lane % HEAD_PAD == V_HEAD, 1.0, vv).astype(v_ref.dtype)


def _inproj_call(xs, mod, g_mix, wut, win, qg, kvg, wq1, wq2, wkk, wv, cq_t, sq_t, cs_t):
    n_tiles = N_TOT // TM
    row = lambda i: (i, 0)
    pos = lambda i: (_pos_tile(i), 0)
    return pl.pallas_call(
        functools.partial(_inproj_kernel, n_x=len(xs)),
        grid=(n_tiles,),
        in_specs=_x_specs(len(xs)) + [
            pl.BlockSpec((1, 6, D_MODEL), lambda i: (_mod_row(i), 0, 0)),
            _const_spec((1, D_MODEL)),
            _const_spec((D_SSM, D_MODEL)),
            _const_spec((D_MODEL, D_IN_REST)),
            _const_spec((1, Q_LORA)),
            _const_spec((1, KV_LORA)),
            _const_spec((Q_LORA, N_HEADS * HEAD_PAD)),
            _const_spec((Q_LORA, N_HEADS * HEAD_PAD)),
            _const_spec((2 * KV_LORA, N_HEADS * HEAD_PAD)),
            _const_spec((KV_LORA, N_HEADS * HEAD_PAD)),
            pl.BlockSpec((TM, LANES), pos),
            pl.BlockSpec((TM, LANES), pos),
            pl.BlockSpec((TM, LANES), pos),
        ],
        out_specs=[
            pl.BlockSpec((D_SSM, TM), lambda i: (0, i)),
            pl.BlockSpec((TM, N_HEADS * HEAD_PAD), row),
            pl.BlockSpec((TM, N_HEADS * HEAD_PAD), row),
            pl.BlockSpec((TM, N_HEADS * HEAD_PAD), row),
        ],
        out_shape=[
            jax.ShapeDtypeStruct((D_SSM, N_TOT), F32),
            jax.ShapeDtypeStruct((N_TOT, N_HEADS * HEAD_PAD), BF16),
            jax.ShapeDtypeStruct((N_TOT, N_HEADS * HEAD_PAD), BF16),
            jax.ShapeDtypeStruct((N_TOT, N_HEADS * HEAD_PAD), BF16),
        ],
        compiler_params=_cparams("parallel"),
        name="in_proj",
    )(*xs, mod, g_mix, wut, win, qg, kvg, wq1, wq2, wkk, wv, cq_t, sq_t, cs_t)


def _attn_kernel(*refs, latent_pieces, tq):
    if latent_pieces:
        q_ref, k_ref, v_ref, kc_ref, vc_ref, o_ref, s_scr = refs
    else:
        q_ref, kc_ref, vc_ref, _, o_ref, s_scr = refs
    heads = [slice(hh * HEAD_PAD, (hh + 1) * HEAD_PAD) for hh in range(2)]
    qs = [q_ref[:, sl] for sl in heads]

    def put_scores(slot, k_at, width):
        for hh in range(2):
            s_scr[slot, hh, :, :width] = lax.dot_general(
                qs[hh], k_at(heads[hh]), (((1,), (1,)), ((), ())), preferred_element_type=F32)

    def consume(carry, slot, v_at, width):
        new = []
        for hh in range(2):
            m, acc = carry[hh]
            s = s_scr[slot, hh, :, :width]
            m_new = jnp.maximum(m, jnp.max(s, axis=-1, keepdims=True))
            alpha = jnp.exp2(m - m_new)
            p = jnp.exp2(s - m_new).astype(BF16)
            acc = alpha * acc + jnp.dot(p, v_at(heads[hh]), preferred_element_type=F32)
            new.append((m_new, acc))
        return tuple(new)

    def piece(kr, vr, start, size):
        return (lambda sl: kr[start:start + size, sl]), (lambda sl: vr[start:start + size, sl]), size

    pieces = [piece(kc_ref, vc_ref, 0, CTX_LEN)]
    start = 0
    for size in latent_pieces:
        pieces.append(piece(k_ref, v_ref, start, size))
        start += size
    carry = tuple((jnp.full((tq, 1), -jnp.inf, F32), jnp.zeros((tq, HEAD_PAD), F32))
                  for _ in range(2))
    put_scores(0, pieces[0][0], pieces[0][2])
    for i, (_, v_at, size) in enumerate(pieces):
        if i + 1 < len(pieces):
            put_scores((i + 1) % 2, pieces[i + 1][0], pieces[i + 1][2])
        carry = consume(carry, i % 2, v_at, size)
    outs = [acc[:, :V_HEAD] / acc[:, V_HEAD:V_HEAD + 1] for _, acc in carry]
    o_ref[...] = jnp.concatenate(outs, axis=-1).astype(o_ref.dtype)


def _attn_latent_call(q, k, v):
    qt = SEQ // TQ
    ctx0 = N_LAT // CTX_LEN
    return pl.pallas_call(
        functools.partial(_attn_kernel, latent_pieces=KEY_PIECES, tq=TQ),
        grid=(BATCH, N_HEADS // 2, qt),
        in_specs=[
            pl.BlockSpec((TQ, 2 * HEAD_PAD), lambda b, h, i: (b * qt + i, h)),
            pl.BlockSpec((SEQ, 2 * HEAD_PAD), lambda b, h, i: (b, h)),
            pl.BlockSpec((SEQ, 2 * HEAD_PAD), lambda b, h, i: (b, h)),
            pl.BlockSpec((CTX_LEN, 2 * HEAD_PAD), lambda b, h, i: (ctx0 + b, h)),
            pl.BlockSpec((CTX_LEN, 2 * HEAD_PAD), lambda b, h, i: (ctx0 + b, h)),
        ],
        out_specs=pl.BlockSpec((TQ, 2 * V_HEAD), lambda b, h, i: (b * qt + i, h)),
        out_shape=jax.ShapeDtypeStruct((N_TOT, D_ATT), BF16),
        scratch_shapes=[pltpu.VMEM((2, 2, TQ, max(KEY_PIECES)), F32)],
        compiler_params=_cparams("parallel", "parallel", "arbitrary"),
        name="attn_latent",
    )(q, k, v, k, v)


def _attn_ctx_call(q, k, v, y_att):
    ctx0 = N_LAT // CTX_LEN
    return pl.pallas_call(
        functools.partial(_attn_kernel, latent_pieces=(), tq=CTX_LEN),
        grid=(BATCH, N_HEADS // 2),
        in_specs=[
            pl.BlockSpec((CTX_LEN, 2 * HEAD_PAD), lambda b, h: (ctx0 + b, h)),
            pl.BlockSpec((CTX_LEN, 2 * HEAD_PAD), lambda b, h: (ctx0 + b, h)),
            pl.BlockSpec((CTX_LEN, 2 * HEAD_PAD), lambda b, h: (ctx0 + b, h)),
            pl.BlockSpec(memory_space=pl.ANY),
        ],
        out_specs=pl.BlockSpec((CTX_LEN, 2 * V_HEAD), lambda b, h: (ctx0 + b, h)),
        out_shape=jax.ShapeDtypeStruct((N_TOT, D_ATT), BF16),
        input_output_aliases={3: 0},
        scratch_shapes=[pltpu.VMEM((1, 2, CTX_LEN, CTX_LEN), F32)],
        compiler_params=_cparams("parallel", "parallel"),
        name="attn_ctx",
    )(q, k, v, y_att)


SSM_C = 64
SSM_HALVES = SSM_T // SSM_C
SSM_KC = SSM_GROUP * SSM_C
LAG_ROWS = 2 * SSM_C
PT_LAG, PT_INC, PT_OUT = 0, LAG_ROWS, LAG_ROWS + SSM_C
PT_ROWS = LAG_ROWS + 2 * SSM_C
NT_DIMS = (((1,), (1,)), ((), ()))


def _ssm_kernel(u_ref, pta_ref, ptb_ref, rows_ref, cc_ref, at_ref, dv_ref, y_ref,
                abt_hi_scr, abt_lo_scr, wl_scr, m_scr, ws_scr, wct_scr, s_scr, h_scr):
    H, C = SSM_GROUP, SSM_C
    half_lane = lax.broadcasted_iota(jnp.int32, (1, LANES), 1) < C

    def split(v):
        hi = v.astype(BF16)
        return hi, (v - hi.astype(F32)).astype(BF16)

    def scaled(row0, n_rows, ra, rb):
        return pta_ref[0, row0:row0 + n_rows, :] * ra + ptb_ref[0, row0:row0 + n_rows, :] * rb

    def build_tables(i, _):
        row0 = pl.multiple_of(i * C, C)
        b_re = rows_ref[0, pl.ds(i, 1), :]
        b_im = rows_ref[0, pl.ds(H + i, 1), :]
        lag_rows = pl.ds(pl.multiple_of(i * LAG_ROWS, LAG_ROWS), LAG_ROWS)
        abt_hi_scr[lag_rows, :], abt_lo_scr[lag_rows, :] = split(
            scaled(PT_LAG, LAG_ROWS, b_re, b_im))
        ws_scr[pl.ds(row0, C), :] = scaled(PT_INC, C, b_re, b_im).astype(BF16)
        c_re = rows_ref[0, pl.ds(2 * H + i, 1), :]
        c_im = rows_ref[0, pl.ds(3 * H + i, 1), :]
        wct_scr[pl.ds(row0, C), :] = scaled(PT_OUT, C, c_re, c_im).astype(BF16)
        return 0

    lax.fori_loop(0, H, build_tables, 0)
    cc_hi, cc_lo = split(cc_ref[0])
    nt = lambda a, b: lax.dot_general(a, b, NT_DIMS, preferred_element_type=F32)
    wl = nt(cc_hi, abt_hi_scr[...]) + nt(cc_hi, abt_lo_scr[...]) + nt(cc_lo, abt_hi_scr[...])
    wl_scr[0] = wl
    wl_scr[1] = jnp.concatenate(
        [pltpu.roll(wl[:, ci * LAG_ROWS:(ci + 1) * LAG_ROWS], C, 1) for ci in range(H)], axis=1)

    slot2 = lax.broadcasted_iota(jnp.int32, (1, 2 * LANES), 1)
    first_low = (slot2 < C) | (slot2 > 2 * LANES - C)
    first_high = slot2 < LANES

    def toeplitz_pair(x):
        return pltpu.roll(jnp.broadcast_to(x, (C, 2 * LANES)), 0, 1, stride=1, stride_axis=0)

    def build_toeplitz(ci, _):
        row0 = pl.multiple_of(ci * C, C)
        slots = pl.ds(pl.multiple_of(ci * LAG_ROWS, LAG_ROWS), LAG_ROWS)

        def lags(co, shifted):
            v = wl_scr[int(shifted), co:co + 1, slots]
            return jnp.concatenate([v, v], axis=1)

        for ka in range(0, H // 2, 2):
            kb = ka + 1
            low = toeplitz_pair(jnp.where(first_low, lags(2 * ka, False), lags(2 * kb, False)))
            high = toeplitz_pair(jnp.where(first_high, lags(2 * ka + 1, True),
                                           lags(2 * kb + 1, True)))
            for k, lanes in ((ka, slice(0, LANES)), (kb, slice(LANES, 2 * LANES))):
                m_scr[pl.ds(row0, C), k * LANES:(k + 1) * LANES] = jnp.where(
                    half_lane, low[:, lanes], high[:, lanes]).astype(BF16)
        return 0

    lax.fori_loop(0, H, build_toeplitz, 0)

    def chunk_operand(hf):
        cols = []
        for k in range(H // 2):
            a, b = u_ref[2 * k], u_ref[2 * k + 1]
            if hf == 0:
                cols.append(jnp.where(half_lane, a, pltpu.roll(b, C, 1)))
            else:
                cols.append(jnp.where(half_lane, pltpu.roll(a, C, 1), b))
        return jnp.concatenate(cols, axis=1).astype(BF16)

    ys = []
    for hf in range(SSM_HALVES):
        u = chunk_operand(hf)
        ys.append(jnp.dot(u, m_scr[...], preferred_element_type=F32))
        s = jnp.dot(u, ws_scr[...], preferred_element_type=F32)
        for d in range(2):
            s_d = s[:, d * LANES:(d + 1) * LANES]
            s_scr[hf, d] = s_d
            s_scr[hf, 2 + d] = pltpu.roll(s_d, SSM_STATE, 1)

    n_lat, n_ctx = SEQ // SSM_T, CTX_LEN // SSM_T
    ctx = [((N_CHUNK_LAT + c, n_ctx), hf) for c in range(n_ctx) for hf in range(SSM_HALVES)]
    lat = [((k, n_lat), hf) for k in range(n_lat) for hf in range(SSM_HALVES)]
    coef = [jnp.broadcast_to(at_ref[0, r:r + 1, :], (BATCH, 2 * SSM_STATE)) for r in range(4)]

    def advance(h, h_sw, d, rows, hf):
        a0, a1 = coef[2 * d], coef[2 * d + 1]
        return (h * a0 + h_sw * a1 + s_scr[hf, d, rows, :],
                h_sw * a0 - h * a1 + s_scr[hf, 2 + d, rows, :])

    zero = jnp.zeros((BATCH, 2 * SSM_STATE), F32)
    h_f, h_f_sw, h_r, h_r_sw = zero, zero, zero, zero
    for ((sf, stf), cf), ((sr, strd), cr) in zip(ctx + lat, ctx[::-1] + lat[::-1]):
        rows_f = pl.ds(sf, BATCH, stride=stf)
        rows_r = pl.ds(sr, BATCH, stride=strd)
        h_scr[cf, 0, rows_f, :] = h_f
        h_f, h_f_sw = advance(h_f, h_f_sw, 0, rows_f, cf)
        h_scr[cr, 1, rows_r, :] = h_r
        h_r, h_r_sw = advance(h_r, h_r_sw, 1, rows_r, cr)

    for hf in range(SSM_HALVES):
        h_in = jnp.concatenate([h_scr[hf, 0], h_scr[hf, 1]], axis=-1).astype(BF16)
        ys[hf] = ys[hf] + lax.dot_general(h_in, wct_scr[...], NT_DIMS,
                                          preferred_element_type=F32)
    for k in range(H // 2):
        y0, y1 = ys[0][:, k * LANES:(k + 1) * LANES], ys[1][:, k * LANES:(k + 1) * LANES]
        for c, yc in ((2 * k, jnp.where(half_lane, y0, pltpu.roll(y1, C, 1))),
                      (2 * k + 1, jnp.where(half_lane, pltpu.roll(y0, C, 1), y1))):
            y_ref[c] = yc + u_ref[c] * dv_ref[0, :, c * SSM_T:(c + 1) * SSM_T]


def _ssm_call(u_t, pta, ptb, rows, cc, at, dv):
    g3 = lambda g: (g, 0, 0)
    return pl.pallas_call(
        _ssm_kernel,
        grid=(N_SSM_GROUPS,),
        in_specs=[
            pl.BlockSpec((SSM_GROUP, N_CHUNK, SSM_T), g3),
            pl.BlockSpec((1, PT_ROWS, 4 * SSM_STATE), g3),
            pl.BlockSpec((1, PT_ROWS, 4 * SSM_STATE), g3),
            pl.BlockSpec((1, 4 * SSM_GROUP, 4 * SSM_STATE), g3),
            pl.BlockSpec((1, SSM_GROUP, 4 * SSM_STATE), g3),
            pl.BlockSpec((1, 4, 2 * SSM_STATE), g3),
            pl.BlockSpec((1, 1, SSM_K), g3),
        ],
        out_specs=pl.BlockSpec((SSM_GROUP, N_CHUNK, SSM_T), g3),
        out_shape=jax.ShapeDtypeStruct((D_SSM, N_CHUNK, SSM_T), F32),
        scratch_shapes=[
            pltpu.VMEM((SSM_GROUP * LAG_ROWS, 4 * SSM_STATE), BF16),
            pltpu.VMEM((SSM_GROUP * LAG_ROWS, 4 * SSM_STATE), BF16),
            pltpu.VMEM((2, SSM_GROUP, SSM_GROUP * LAG_ROWS), F32),
            pltpu.VMEM((SSM_KC, SSM_KC), BF16),
            pltpu.VMEM((SSM_KC, 4 * SSM_STATE), BF16),
            pltpu.VMEM((SSM_KC, 4 * SSM_STATE), BF16),
            pltpu.VMEM((SSM_HALVES, 4, N_CHUNK, 2 * SSM_STATE), F32),
            pltpu.VMEM((SSM_HALVES, 2, N_CHUNK, 2 * SSM_STATE), F32),
        ],
        compiler_params=_cparams("parallel"),
        name="s5_mixer",
    )(u_t, pta, ptb, rows, cc, at, dv)


def _ssm_tables(a_re, a_im, log_dt, b_re, b_im, c_re, c_im, d_skip):
    G, P, H, T = N_SSM_GROUPS, SSM_STATE, SSM_GROUP, SSM_T
    a_re, a_im = a_re.astype(F32), a_im.astype(F32)
    dt = jnp.exp(log_dt.astype(F32))[..., None]
    den = a_re * a_re + a_im * a_im
    mag1 = jnp.exp(dt * a_re)
    ab_re, ab_im = mag1 * jnp.cos(dt * a_im), mag1 * jnp.sin(dt * a_im)
    num_re = ab_re - 1.0
    f_re = (num_re * a_re + ab_im * a_im) / den
    f_im = (ab_im * a_re - num_re * a_im) / den
    b_re, b_im = b_re.astype(F32), b_im.astype(F32)
    bb_re = f_re[..., None] * b_re - f_im[..., None] * b_im
    bb_im = f_re[..., None] * b_im + f_im[..., None] * b_re
    c_re, c_im = c_re.astype(F32), c_im.astype(F32)
    la, th = dt * a_re, dt * a_im

    def powers(d, n):
        mag = jnp.exp(la[d][:, None, :] * n[None, :, None])
        ph = th[d][:, None, :] * n[None, :, None]
        return mag * jnp.cos(ph), mag * jnp.sin(ph)

    C = SSM_C
    slot = np.arange(LAG_ROWS)
    lag_f = np.where(slot < C, slot, -1)
    lag_r = np.where((-slot) % LAG_ROWS < C, (-slot) % LAG_ROWS, -1)
    s_idx = np.arange(C)
    f_all = np.concatenate([lag_f, C - 1 - s_idx, s_idx + 1])
    r_all = np.concatenate([lag_r, s_idx, C - s_idx])
    f_exp, r_exp = jnp.asarray(np.maximum(f_all, 0), F32), jnp.asarray(np.maximum(r_all, 0), F32)
    f_on, r_on = jnp.asarray(f_all >= 0, F32), jnp.asarray(r_all >= 0, F32)
    lane = jnp.arange(4 * P)
    is_fwd = (lane < 2 * P)[None, :]
    is_re = ((lane // P) % 2 == 0)[None, None, :]
    expo = jnp.where(is_fwd, f_exp[:, None], r_exp[:, None])[None]
    on = jnp.where(is_fwd, f_on[:, None], r_on[:, None])[None]
    la4 = jnp.concatenate([la[0], la[0], la[1], la[1]], axis=-1)[:, None, :]
    th4 = jnp.concatenate([th[0], th[0], th[1], th[1]], axis=-1)[:, None, :]
    mag = jnp.exp(la4 * expo) * on
    p_re, p_im = mag * jnp.cos(th4 * expo), mag * jnp.sin(th4 * expo)
    pta = jnp.where(is_re, p_re, p_im)
    ptb = jnp.where(is_re, -p_im, p_re)

    def per_channel(v):
        f, r = v[0].transpose(0, 2, 1), v[1].transpose(0, 2, 1)
        return jnp.concatenate([f, f, r, r], axis=-1)
    cf_re, cr_re = c_re[0], c_re[1]
    cf_im, cr_im = c_im[0], c_im[1]
    rows = jnp.concatenate([
        per_channel(bb_re), per_channel(bb_im),
        jnp.concatenate([cf_re, -cf_re, cr_re, -cr_re], axis=-1),
        jnp.concatenate([cf_im, -cf_im, cr_im, -cr_im], axis=-1)], axis=1)
    cc = jnp.concatenate([cf_re, -cf_im, cr_re, -cr_im], axis=-1)

    t_exp = jnp.full((1,), float(SSM_C), F32)
    (f_re_t, f_im_t), (r_re_t, r_im_t) = powers(0, t_exp), powers(1, t_exp)
    f_re_t, f_im_t, r_re_t, r_im_t = (v[:, 0, :] for v in (f_re_t, f_im_t, r_re_t, r_im_t))
    at = jnp.stack([jnp.concatenate([f_re_t, f_re_t], -1), jnp.concatenate([-f_im_t, f_im_t], -1),
                    jnp.concatenate([r_re_t, r_re_t], -1), jnp.concatenate([-r_im_t, r_im_t], -1)],
                   axis=1)
    dv = jnp.repeat(d_skip.astype(F32).reshape(G, H), T, axis=-1).reshape(G, 1, H * T)
    return pta, ptb, rows, cc, at, dv


ROUTE_E1, ROUTE_E2, ROUTE_W1, ROUTE_W2 = 0, 1, 2, 3


def _top2_route(logits):
    lane = lax.broadcasted_iota(jnp.int32, logits.shape, 1)
    lg = jnp.where(lane < N_EXPERTS, logits, -jnp.inf)
    m1 = jnp.max(lg, axis=-1, keepdims=True)
    i1 = jnp.min(jnp.where(lg == m1, lane, LANES), axis=-1, keepdims=True)
    lg2 = jnp.where(lane == i1, -jnp.inf, lg)
    m2 = jnp.max(lg2, axis=-1, keepdims=True)
    i2 = jnp.min(jnp.where(lg2 == m2, lane, LANES), axis=-1, keepdims=True)
    e2 = jnp.exp(m2 - m1)
    w1 = 1.0 / (1.0 + e2)
    rec = jnp.where(lane == ROUTE_E1, i1.astype(F32), 0.0)
    rec = jnp.where(lane == ROUTE_E2, i2.astype(F32), rec)
    rec = jnp.where(lane == ROUTE_W1, w1, rec)
    return jnp.where(lane == ROUTE_W2, e2 * w1, rec), i1, i2


def _expert_slots(i1, i2, taken):
    n = i1.shape[0]
    lane = lax.broadcasted_iota(jnp.int32, (n, LANES), 1)
    picked = jnp.where((lane == i1) | (lane == i2), 1.0, 0.0)
    earlier = (lax.broadcasted_iota(jnp.int32, (n, n), 1)
               < lax.broadcasted_iota(jnp.int32, (n, n), 0)).astype(BF16)
    rank = jnp.dot(earlier, picked.astype(BF16), preferred_element_type=F32)
    slot = rank + taken + (lane * EXPERT_CAP).astype(F32)
    d1 = jnp.sum(jnp.where(lane == i1, slot, 0.0), axis=-1, keepdims=True)
    d2 = jnp.sum(jnp.where(lane == i2, slot, 0.0), axis=-1, keepdims=True)
    dest = jnp.where(lane == 0, d1, jnp.where(lane == 1, d2, 0.0)).astype(jnp.int32)
    return dest, jnp.sum(picked, axis=0, keepdims=True)


def _pack_bf16_pairs(v):
    k = v.shape[1] // 2
    bits = pltpu.bitcast(v.astype(BF16).astype(F32), jnp.uint32)
    return (bits[:, :k] & jnp.uint32(0xFFFF0000)) | (bits[:, k:] >> 16)


def _unpack_bf16_pairs(w):
    hi = pltpu.bitcast(w & jnp.uint32(0xFFFF0000), F32)
    lo = pltpu.bitcast(w << 16, F32)
    return hi, lo


def _store_packed(ref, v):
    words = _pack_bf16_pairs(v)
    for s in range(SC_SPLIT):
        ref[s] = words[:, s * SC_ROW:(s + 1) * SC_ROW]


def _load_packed(ref):
    return _unpack_bf16_pairs(jnp.concatenate([ref[s] for s in range(SC_SPLIT)], axis=-1))


def _mix_kernel(*refs, with_router, n_x):
    rest = refs[n_x:]
    if with_router:
        (yst_ref, ya_ref, mod_ref, wglut_ref, bglu_ref, wout_ref, gffn_ref, router_ref,
         x1_ref, h2_ref, gate_ref, dest_ref, cnt_ref, taken_scr) = rest
    else:
        (yst_ref, ya_ref, mod_ref, wglut_ref, bglu_ref, wout_ref, gffn_ref,
         w1_ref, w3_ref, w2_ref, x2_ref) = rest
    zt = jax.nn.gelu(yst_ref[...], approximate=True)
    glt = zt * jax.nn.sigmoid(
        jnp.dot(wglut_ref[...], zt.astype(BF16), preferred_element_type=F32) + bglu_ref[...])
    mix = jnp.concatenate([glt.T.astype(BF16), ya_ref[...]], axis=-1)
    o = jnp.dot(mix, wout_ref[...], preferred_element_type=F32)
    x1 = _tile_rows(refs, n_x) + mod_ref[0, 2:3, :] * o
    h2 = _rms(x1, gffn_ref[...]) * (1.0 + mod_ref[0, 4:5, :]) + mod_ref[0, 3:4, :]
    if not with_router:
        acc = _swiglu(h2.astype(BF16), lambda sl: w1_ref[:, sl], lambda sl: w3_ref[:, sl],
                      lambda sl: w2_ref[sl, :], D_FF)
        x2_ref[...] = x1 + mod_ref[0, 5:6, :] * acc
    else:
        x1_ref[...] = x1
        _store_packed(h2_ref, h2)
        h_hi = h2.astype(BF16)
        h_lo = (h2 - h_hi.astype(F32)).astype(BF16)
        o1 = jnp.dot(h_hi, router_ref[0], preferred_element_type=F32)
        o2 = jnp.dot(h_lo, router_ref[1], preferred_element_type=F32)
        logits = o1 + pltpu.roll(o1, LANES - N_EXPERTS, 1) + o2
        gate_ref[...], i1, i2 = _top2_route(logits)

        @pl.when(pl.program_id(0) == 0)
        def _():
            taken_scr[...] = jnp.zeros_like(taken_scr)

        dest_ref[...], tile_cnt = _expert_slots(i1, i2, taken_scr[...])
        taken_scr[...] += tile_cnt
        cnt_ref[...] = jnp.broadcast_to(taken_scr[...], cnt_ref.shape)


def _mix_call(xs, y_ssm_t, y_att, mod, wglut, bglu, wout, gffn, *, router=None, ffn=None, n_tiles):
    row = lambda i: (i, 0)
    out_row = row
    with_router = router is not None
    assert with_router != (ffn is not None)
    in_specs = _x_specs(len(xs)) + [
        pl.BlockSpec((D_SSM, TM), lambda i: (0, i)),
        pl.BlockSpec((TM, D_ATT), row),
        pl.BlockSpec((1, 6, D_MODEL), lambda i: (_mod_row(i), 0, 0)),
        _const_spec((D_SSM, D_SSM)),
        _const_spec((D_SSM, 1)),
        _const_spec((D_MIX, D_MODEL)),
        _const_spec((1, D_MODEL)),
    ]
    args = [*xs, y_ssm_t, y_att, mod, wglut, bglu, wout, gffn]
    out_specs = [pl.BlockSpec((TM, D_MODEL), out_row)]
    out_shape = [jax.ShapeDtypeStruct((n_tiles * TM, D_MODEL), F32)]
    if with_router:
        in_specs.append(_const_spec((2, D_MODEL, LANES)))
        args.append(router)
        out_specs += [pl.BlockSpec((SC_SPLIT, TM, SC_ROW), lambda i: (0, i, 0)),
                      pl.BlockSpec((TM, LANES), out_row),
                      pl.BlockSpec((TM, LANES), out_row),
                      pl.BlockSpec((8, LANES), lambda i: (0, 0))]
        out_shape += [jax.ShapeDtypeStruct((SC_SPLIT, n_tiles * TM, SC_ROW), jnp.uint32),
                      jax.ShapeDtypeStruct((n_tiles * TM, LANES), F32),
                      jax.ShapeDtypeStruct((n_tiles * TM, LANES), jnp.int32),
                      jax.ShapeDtypeStruct((8, LANES), F32)]
        scratch = [pltpu.VMEM((1, LANES), F32)]
    else:
        in_specs += [_const_spec((D_MODEL, D_FF)), _const_spec((D_MODEL, D_FF)),
                     _const_spec((D_FF, D_MODEL))]
        args += list(ffn)
        scratch = []
    return pl.pallas_call(
        functools.partial(_mix_kernel, with_router=with_router, n_x=len(xs)),
        grid=(n_tiles,),
        in_specs=in_specs,
        out_specs=out_specs,
        out_shape=out_shape,
        scratch_shapes=scratch,
        compiler_params=_cparams("arbitrary" if with_router else "parallel"),
        name="mix_out",
    )(*args)


def _swiglu(h, w1_at, w3_at, w2_at, d_ff):
    acc = jnp.zeros((h.shape[0], D_MODEL), F32)
    for lo in range(0, d_ff, FF_CHUNK):
        sl = slice(lo, min(lo + FF_CHUNK, d_ff))
        a = jnp.dot(h, w1_at(sl), preferred_element_type=F32)
        b = jnp.dot(h, w3_at(sl), preferred_element_type=F32)
        g = (a * jax.nn.sigmoid(a) * b).astype(BF16)
        acc = acc + jnp.dot(g, w2_at(sl), preferred_element_type=F32)
    return acc


EXPERT_BLK = 512
EXPERT_CAP = N_LAT
N_SORTED = N_EXPERTS * EXPERT_CAP
N_EXPERT_BLKS = 2 * N_LAT // EXPERT_BLK + N_EXPERTS
PACKED = D_MODEL // 2
SC_ROW = 256
SC_SPLIT = PACKED // SC_ROW
SC_WIN = 128


def _sc_mesh():
    return plsc.VectorSubcoreMesh(core_axis_name="core", subcore_axis_name="subcore")


def _sc_scatter(x, idx_a, idx_b, n_out):
    n = x.shape[0]

    @pl.kernel(out_type=jax.ShapeDtypeStruct((n_out, SC_ROW), x.dtype), mesh=_sc_mesh(),
               scratch_types=[])
    def scatter(x_hbm, a_hbm, b_hbm, o_hbm):
        def body(x_vmem, a_vmem, b_vmem):
            pltpu.sync_copy(x_vmem, o_hbm.at[a_vmem.at[0]])
            pltpu.sync_copy(x_vmem, o_hbm.at[b_vmem.at[0]])

        pltpu.emit_pipeline(
            body, grid=(n // SC_WIN,),
            in_specs=[pl.BlockSpec((SC_WIN, SC_ROW), lambda i: (i, 0)),
                      pl.BlockSpec((1, SC_WIN), lambda i: (0, i)),
                      pl.BlockSpec((1, SC_WIN), lambda i: (0, i))],
            out_specs=[],
            core_axis_name=("core", "subcore"),
            dimension_semantics=(pltpu.PARALLEL,),
        )(x_hbm, a_hbm, b_hbm)

    return scatter(x, idx_a.reshape(1, n), idx_b.reshape(1, n))


def _sc_gather(y, idx):
    n = idx.shape[0]

    @pl.kernel(out_type=jax.ShapeDtypeStruct((n, SC_ROW), y.dtype), mesh=_sc_mesh(),
               scratch_types=[])
    def gather(y_hbm, i_hbm, o_hbm):
        def body(i_vmem, o_vmem):
            pltpu.sync_copy(y_hbm.at[i_vmem.at[0]], o_vmem)

        pltpu.emit_pipeline(
            body, grid=(n // SC_WIN,),
            in_specs=[pl.BlockSpec((1, SC_WIN), lambda i: (0, i))],
            out_specs=[pl.BlockSpec((SC_WIN, SC_ROW), lambda i: (i, 0))],
            core_axis_name=("core", "subcore"),
            dimension_semantics=(pltpu.PARALLEL,),
        )(i_hbm, o_hbm)

    return gather(y, idx.reshape(1, n))


W_CAST_ROWS = 128


def _experts_kernel(blk_expert_ref, blk_row_ref, n_used_ref, x_ref, w1_ref, w3_ref, w2_ref, o_ref,
                    w1_scr, w3_scr, w2_scr):
    del blk_row_ref
    b = pl.program_id(0)
    live = b < n_used_ref[0]
    new_expert = (b == 0) | (blk_expert_ref[b] != blk_expert_ref[jnp.maximum(b - 1, 0)])

    @pl.when(live & new_expert)
    def _():
        for src, dst in ((w1_ref, w1_scr), (w3_ref, w3_scr), (w2_ref, w2_scr)):
            for r in range(0, dst.shape[0], W_CAST_ROWS):
                dst[r:r + W_CAST_ROWS, :] = src[0, r:r + W_CAST_ROWS, :].astype(BF16)

    @pl.when(live)
    def _():
        hi, lo = _load_packed(x_ref)
        h = jnp.concatenate([hi.astype(BF16), lo.astype(BF16)], axis=-1)
        y = _swiglu(h, lambda sl: w1_scr[:, sl], lambda sl: w3_scr[:, sl],
                    lambda sl: w2_scr[sl, :], D_FF_EXPERT)
        _store_packed(o_ref, y)


def _experts_call(blk_expert, blk_row, n_used, xs, w1, w3, w2):
    row = lambda b, be, br, nu: (0, br[b], 0)
    wsel = lambda b, be, br, nu: (be[b], 0, 0)
    return pl.pallas_call(
        _experts_kernel,
        grid_spec=pltpu.PrefetchScalarGridSpec(
            num_scalar_prefetch=3,
            grid=(N_EXPERT_BLKS,),
            in_specs=[
                pl.BlockSpec((SC_SPLIT, EXPERT_BLK, SC_ROW), row),
                pl.BlockSpec((1, D_MODEL, D_FF_EXPERT), wsel),
                pl.BlockSpec((1, D_MODEL, D_FF_EXPERT), wsel),
                pl.BlockSpec((1, D_FF_EXPERT, D_MODEL), wsel),
            ],
            out_specs=pl.BlockSpec((SC_SPLIT, EXPERT_BLK, SC_ROW), row),
            scratch_shapes=[pltpu.VMEM((D_MODEL, D_FF_EXPERT), BF16),
                            pltpu.VMEM((D_MODEL, D_FF_EXPERT), BF16),
                            pltpu.VMEM((D_FF_EXPERT, D_MODEL), BF16)],
        ),
        out_shape=jax.ShapeDtypeStruct((SC_SPLIT, N_SORTED, SC_ROW), jnp.uint32),
        compiler_params=_cparams("arbitrary"),
        name="moe_experts",
    )(blk_expert, blk_row, n_used, xs, w1, w3, w2)


def _combine_kernel(x1_ref, r_ref, y_ref, mod_ref, fg_ref, o_ref):
    def expert_out(slot):
        hi, lo = _load_packed(y_ref.at[slot])
        return jnp.concatenate([hi, lo], axis=-1)
    w1 = r_ref[:, ROUTE_W1:ROUTE_W1 + 1]
    w2 = r_ref[:, ROUTE_W2:ROUTE_W2 + 1]
    y = w1 * expert_out(0) + w2 * expert_out(1)
    x2 = x1_ref[...] + mod_ref[0, 5:6, :] * y
    o_ref[...] = _rms(x2, fg_ref[...])


def _combine_call(x1, route, y_tok, mod, fg):
    n_tiles = N_LAT // TM
    row = lambda i: (i, 0)
    return pl.pallas_call(
        _combine_kernel,
        grid=(n_tiles,),
        in_specs=[
            pl.BlockSpec((TM, D_MODEL), row),
            pl.BlockSpec((TM, LANES), row),
            pl.BlockSpec((2, SC_SPLIT, TM, SC_ROW), lambda i: (0, 0, i, 0)),
            pl.BlockSpec((1, 6, D_MODEL), lambda i: (i // SEQ_TILES, 0, 0)),
            pl.BlockSpec((1, D_MODEL), lambda i: (0, 0)),
        ],
        out_specs=pl.BlockSpec((TM, D_MODEL), row),
        out_shape=jax.ShapeDtypeStruct((N_LAT, D_MODEL), F32),
        compiler_params=_cparams("parallel"),
        name="moe_combine",
    )(x1, route, y_tok, mod, fg)


def _moe_routed(h2p, x1, route, dest, cnt, mod, w1, w3, w2, fg):
    blks = (cnt[0, :N_EXPERTS].astype(jnp.int32) + (EXPERT_BLK - 1)) // EXPERT_BLK
    blk_end = jnp.cumsum(blks)
    n_used = blk_end[-1:]
    b = jnp.minimum(jnp.arange(N_EXPERT_BLKS, dtype=jnp.int32), n_used[0] - 1)
    blk_expert = jnp.sum((b[:, None] >= blk_end[None, :]).astype(jnp.int32), axis=1)
    blk_row = blk_expert * (EXPERT_CAP // EXPERT_BLK) + b - (blk_end - blks)[blk_expert]
    piece = jnp.arange(SC_SPLIT, dtype=jnp.int32)[:, None] * N_SORTED
    idx = [(piece + dest[:, slot][None, :]).reshape(SC_SPLIT * N_LAT) for slot in range(2)]
    xs = _sc_scatter(h2p.reshape(SC_SPLIT * N_LAT, SC_ROW), idx[0], idx[1], SC_SPLIT * N_SORTED)
    ys = _experts_call(blk_expert, blk_row, n_used, xs.reshape(SC_SPLIT, N_SORTED, SC_ROW),
                       w1, w3, w2)
    y_tok = _sc_gather(ys.reshape(SC_SPLIT * N_SORTED, SC_ROW), jnp.concatenate(idx))
    return _combine_call(x1, route, y_tok.reshape(2, SC_SPLIT, N_LAT, SC_ROW), mod, fg)


def _rope_partner_perm():
    perm, sign = [], []
    for j in range(QK_ROPE):
        first_half = (j % AXIS_ROPE) < ROPE_FREQS
        perm.append(j + ROPE_FREQS if first_half else j - ROPE_FREQS)
        sign.append(-1.0 if first_half else 1.0)
    return jnp.array(perm, jnp.int32), jnp.array(sign, F32)


def _rope_tables():
    t = jnp.arange(SEQ)
    row = (t // GRID_W).astype(F32)
    col = (t % GRID_W).astype(F32)
    inv_freq = ROPE_BASE ** (-2.0 * jnp.arange(ROPE_FREQS, dtype=F32) / AXIS_ROPE)
    ang = jnp.concatenate([row[:, None] * inv_freq, row[:, None] * inv_freq,
                           col[:, None] * inv_freq, col[:, None] * inv_freq], axis=1)
    cos = jnp.concatenate([jnp.cos(ang), jnp.ones((N_CTX, QK_ROPE), F32)], axis=0)
    sin = jnp.concatenate([jnp.sin(ang), jnp.zeros((N_CTX, QK_ROPE), F32)], axis=0)
    n = N_CTX + SEQ
    pad32 = jnp.zeros((n, HEAD_PAD - D_QK), F32)
    qs = ATT_SCALE * math.log2(math.e)
    cq = jnp.concatenate([jnp.full((n, QK_NOPE), qs, F32), qs * cos, pad32], axis=1)
    sq = jnp.concatenate([jnp.zeros((n, QK_NOPE), F32), qs * sin, pad32], axis=1)
    cs = jnp.concatenate([cos, sin, jnp.zeros((n, LANES - 2 * QK_ROPE), F32)], axis=1)
    return cq, sq, cs


def _layer_weights(w_in, w_uq, w_ukv):
    perm, sign = _rope_partner_perm()
    s0 = D_SSM + Q_LORA + KV_LORA
    kr_w = w_in[:, s0:s0 + QK_ROPE]
    wut = w_in[:, :D_SSM].T.astype(BF16)
    win = jnp.concatenate([w_in[:, D_SSM:s0], kr_w, kr_w[:, perm] * sign,
                           jnp.zeros((D_MODEL, LANES - 2 * QK_ROPE), F32)], axis=1).astype(BF16)
    uq = w_uq.reshape(Q_LORA, N_HEADS, D_QK)
    nope, rope = uq[..., :QK_NOPE], uq[..., QK_NOPE:]
    zpad = jnp.zeros((Q_LORA, N_HEADS, HEAD_PAD - D_QK), F32)
    wq1 = jnp.concatenate([nope, rope, zpad], axis=-1).reshape(Q_LORA, N_HEADS * HEAD_PAD).astype(BF16)
    wq2 = jnp.concatenate([jnp.zeros_like(nope), rope[..., perm] * sign, zpad], axis=-1)
    wq2 = wq2.reshape(Q_LORA, N_HEADS * HEAD_PAD).astype(BF16)
    ukv = w_ukv.reshape(KV_LORA, N_HEADS, QK_NOPE + V_HEAD)
    wk = jnp.concatenate([ukv[..., :QK_NOPE], jnp.zeros((KV_LORA, N_HEADS, HEAD_PAD - QK_NOPE), F32)],
                         axis=-1).reshape(KV_LORA, N_HEADS * HEAD_PAD)
    eye = jnp.eye(QK_ROPE, dtype=F32)
    place = jnp.concatenate([jnp.zeros((QK_ROPE, QK_NOPE), F32), eye,
                             jnp.zeros((QK_ROPE, HEAD_PAD - D_QK), F32)], axis=1)
    place = jnp.tile(place, (1, N_HEADS))
    spread = jnp.concatenate([place, place, jnp.zeros((LANES - 2 * QK_ROPE, N_HEADS * HEAD_PAD), F32)], 0)
    wkk = jnp.concatenate([wk, spread], axis=0).astype(BF16)
    wv = jnp.concatenate([ukv[..., QK_NOPE:], jnp.zeros((KV_LORA, N_HEADS, HEAD_PAD - V_HEAD), F32)],
                         axis=-1).reshape(KV_LORA, N_HEADS * HEAD_PAD).astype(BF16)
    return wut, win, wq1, wq2, wkk, wv


def kernel(x, c, ctx, c_ctx, w_ada, b_ada, norm_mix, norm_ffn, w_in, q_norm, kv_norm, w_uq, w_ukv,
           ssm_a_re, ssm_a_im, ssm_log_dt, ssm_b_re, ssm_b_im, ssm_c_re, ssm_c_im, ssm_d, w_glu,
           b_glu, w_out, ffn_w1, ffn_w3, ffn_w2, moe_router, moe_w1, moe_w3, moe_w2, final_norm):
    assert x.shape == (BATCH, SEQ, D_MODEL) and ctx.shape == (BATCH, CTX_LEN, D_MODEL)
    cond = jnp.concatenate([c, c_ctx[None, :], jnp.zeros((MOD_ROWS - BATCH - 1, D_MODEL), F32)], axis=0)
    mod_all = _ada_call(cond, w_ada, b_ada).reshape(DEPTH, MOD_ROWS, 6, D_MODEL)
    cq_t, sq_t, cs_t = _rope_tables()
    xs = (x.reshape(N_LAT, D_MODEL), ctx.reshape(N_CTX, D_MODEL))

    out = None
    for i in range(DEPTH):
        last = i == DEPTH - 1
        mod = mod_all[i]
        wut, win, wq1, wq2, wkk, wv = _layer_weights(w_in[i], w_uq[i], w_ukv[i])
        u_t, q, k, v = _inproj_call(xs, mod, norm_mix[i][None, :], wut, win, q_norm[i][None, :],
                                    kv_norm[i][None, :], wq1, wq2, wkk, wv, cq_t, sq_t, cs_t)
        tabs = _ssm_tables(ssm_a_re[i], ssm_a_im[i], ssm_log_dt[i], ssm_b_re[i], ssm_b_im[i],
                           ssm_c_re[i], ssm_c_im[i], ssm_d[i])
        y_ssm_t = _ssm_call(u_t.reshape(D_SSM, N_CHUNK, SSM_T), *tabs).reshape(D_SSM, N_TOT)
        y_att = _attn_latent_call(q, k, v)
        if last:
            n_tiles = LAT_TILES
        else:
            y_att = _attn_ctx_call(q, k, v, y_att)
            n_tiles = N_TOT // TM
        j = i // 2
        mix_args = (xs, y_ssm_t, y_att, mod, w_glu[i].T.astype(BF16), b_glu[i][:, None],
                    w_out[i].astype(BF16), norm_ffn[i][None, :])
        if i % 2 == 0:
            assert not last
            ffn = (ffn_w1[j].astype(BF16), ffn_w3[j].astype(BF16), ffn_w2[j].astype(BF16))
            xs = tuple(_mix_call(*mix_args, ffn=ffn, n_tiles=n_tiles))
        else:
            assert last
            r = moe_router[j]
            r_top = lax.bitcast_convert_type(
                lax.bitcast_convert_type(r, jnp.uint32) & jnp.uint32(0xFFFF0000), F32)
            r_hi = r_top.astype(BF16)
            r_lo = (r - r_top).astype(BF16)
            zr = jnp.zeros((D_MODEL, LANES - 2 * N_EXPERTS), BF16)
            router = jnp.stack([jnp.concatenate([r_hi, r_lo, zr], axis=1),
                                jnp.concatenate([r_hi, jnp.zeros_like(r_lo), zr], axis=1)])
            x1, h2p, route, dest, cnt = _mix_call(*mix_args, router=router, n_tiles=n_tiles)
            out = _moe_routed(h2p, x1, route, dest, cnt, mod, moe_w1[j], moe_w3[j], moe_w2[j],
                              final_norm[None, :])
    return out.reshape(BATCH, SEQ, D_MODEL)
```

```python
import functools
import math

import jax
import jax.numpy as jnp
import numpy as np
from jax import lax
from jax.experimental import pallas as pl
from jax.experimental.pallas import tpu as pltpu
from jax.experimental.pallas import tpu_sc as plsc

D_MODEL = 1024
BATCH = 4
SEQ = 8192
DEPTH = 2
GRID_W = 64
CTX_LEN = 256
D_SSM = 512
SSM_GROUP = 16
N_SSM_GROUPS = D_SSM // SSM_GROUP
SSM_STATE = 64
N_HEADS = 8
QK_NOPE = 64
QK_ROPE = 32
V_HEAD = 64
Q_LORA = 256
KV_LORA = 128
D_QK = QK_NOPE + QK_ROPE
D_ATT = N_HEADS * V_HEAD
D_MIX = D_SSM + D_ATT
AXIS_ROPE = QK_ROPE // 2
ROPE_FREQS = AXIS_ROPE // 2
ROPE_BASE = 10000.0
ATT_SCALE = 1.0 / math.sqrt(D_QK)
D_FF = 2816
N_EXPERTS = 8
D_FF_EXPERT = 1408
EPS = 1e-6

N_CTX = BATCH * CTX_LEN
N_LAT = BATCH * SEQ
N_TOT = N_CTX + N_LAT

LANES = 128
HEAD_PAD = 128
TM = 512
LAT_TILES = N_LAT // TM
SEQ_TILES = SEQ // TM
TQ = 1024
KEY_PIECES = (2048, 2048, 2048, 2048)
assert sum(KEY_PIECES) == SEQ
SSM_T = 128
N_CHUNK_LAT = N_LAT // SSM_T
N_CHUNK = N_TOT // SSM_T
SSM_K = SSM_GROUP * SSM_T
FF_CHUNK = 256
D_IN_REST = Q_LORA + KV_LORA + LANES
MOD_ROWS = 8
VMEM_LIMIT = 56 * 1024 * 1024

F32 = jnp.float32
BF16 = jnp.bfloat16
HI = lax.Precision.HIGHEST


def _cparams(*sem):
    return pltpu.CompilerParams(dimension_semantics=sem, vmem_limit_bytes=VMEM_LIMIT)


def _const_spec(shape):
    nd = len(shape)
    return pl.BlockSpec(shape, lambda *_: (0,) * nd, pipeline_mode=pl.Buffered(1))


def _mod_row(i):
    return jnp.where(i < LAT_TILES, i // SEQ_TILES, BATCH)


def _pos_tile(i):
    return jnp.where(i < LAT_TILES, i % SEQ_TILES, SEQ_TILES + i - LAT_TILES)


def _rms(x, g):
    ms = jnp.mean(x * x, axis=-1, keepdims=True)
    return x * lax.rsqrt(ms + EPS) * g


ADA_TN = 1536


def _ada_kernel(c_ref, w_ref, b_ref, o_ref):
    c = c_ref[...]
    s = c * jax.nn.sigmoid(c)
    o_ref[0] = jnp.dot(s, w_ref[0], precision=HI, preferred_element_type=F32) + b_ref[0]


def _ada_call(cond, w_ada, b_ada):
    n_col = 6 * D_MODEL // ADA_TN
    return pl.pallas_call(
        _ada_kernel,
        grid=(DEPTH, n_col),
        in_specs=[
            pl.BlockSpec((MOD_ROWS, D_MODEL), lambda l, j: (0, 0)),
            pl.BlockSpec((1, D_MODEL, ADA_TN), lambda l, j: (l, 0, j)),
            pl.BlockSpec((1, 1, ADA_TN), lambda l, j: (l, 0, j)),
        ],
        out_specs=pl.BlockSpec((1, MOD_ROWS, ADA_TN), lambda l, j: (l, 0, j)),
        out_shape=jax.ShapeDtypeStruct((DEPTH, MOD_ROWS, 6 * D_MODEL), F32),
        compiler_params=_cparams("arbitrary", "arbitrary"),
        name="ada_mod",
    )(cond, w_ada, b_ada.reshape(DEPTH, 1, 6 * D_MODEL))


def _tile_rows(refs, n_x):
    if n_x == 1:
        return refs[0][...]
    return jnp.where(pl.program_id(0) < LAT_TILES, refs[0][...], refs[1][...])


def _x_specs(n_x):
    if n_x == 1:
        return [pl.BlockSpec((TM, D_MODEL), lambda i: (i, 0))]
    return [pl.BlockSpec((TM, D_MODEL), lambda i: (jnp.minimum(i, LAT_TILES - 1), 0)),
            pl.BlockSpec((TM, D_MODEL), lambda i: (jnp.maximum(i - LAT_TILES, 0), 0))]


def _inproj_kernel(*refs, n_x):
    (mod_ref, g_ref, wut_ref, win_ref, qg_ref, kvg_ref, wq1_ref, wq2_ref, wkk_ref, wv_ref,
     cq_ref, sq_ref, cs_ref, ut_ref, q_ref, k_ref, v_ref) = refs[n_x:]
    x = _tile_rows(refs, n_x)
    sh = mod_ref[0, 0:1, :]
    sc = mod_ref[0, 1:2, :]
    xm = (_rms(x, g_ref[...]) * (1.0 + sc) + sh).astype(BF16)
    ut_ref[...] = lax.dot_general(wut_ref[...], xm, (((1,), (1,)), ((), ())),
                                  preferred_element_type=F32).astype(ut_ref.dtype)
    z = jnp.dot(xm, win_ref[...], preferred_element_type=F32)
    qn = _rms(z[:, :Q_LORA], qg_ref[...]).astype(BF16)
    kvn = _rms(z[:, Q_LORA:Q_LORA + KV_LORA], kvg_ref[...]).astype(BF16)
    krr = (z[:, Q_LORA + KV_LORA:] * cs_ref[...]).astype(BF16)
    q1 = jnp.dot(qn, wq1_ref[...], preferred_element_type=F32)
    q2 = jnp.dot(qn, wq2_ref[...], preferred_element_type=F32)
    cq = cq_ref[...]
    sq = sq_ref[...]
    for h in range(N_HEADS):
        sl = slice(h * HEAD_PAD, (h + 1) * HEAD_PAD)
        q_ref[:, sl] = (q1[:, sl] * cq + q2[:, sl] * sq).astype(q_ref.dtype)
    kin = jnp.concatenate([kvn, krr], axis=-1)
    k_ref[...] = jnp.dot(kin, wkk_ref[...], preferred_element_type=F32).astype(k_ref.dtype)
    vv = jnp.dot(kvn, wv_ref[...], preferred_element_type=F32)
    lane = lax.broadcasted_iota(jnp.int32, vv.shape, 1)
    v_ref[...] = jnp.where(lane % HEAD_PAD == V_HEAD, 1.0, vv).astype(v_ref.dtype)


def _inproj_call(xs, mod, g_mix, wut, win, qg, kvg, wq1, wq2, wkk, wv, cq_t, sq_t, cs_t):
    n_tiles = N_TOT // TM
    row = lambda i: (i, 0)
    pos = lambda i: (_pos_tile(i), 0)
    return pl.pallas_call(
        functools.partial(_inproj_kernel, n_x=len(xs)),
        grid=(n_tiles,),
        in_specs=_x_specs(len(xs)) + [
            pl.BlockSpec((1, 6, D_MODEL), lambda i: (_mod_row(i), 0, 0)),
            _const_spec((1, D_MODEL)),
            _const_spec((D_SSM, D_MODEL)),
            _const_spec((D_MODEL, D_IN_REST)),
            _const_spec((1, Q_LORA)),
            _const_spec((1, KV_LORA)),
            _const_spec((Q_LORA, N_HEADS * HEAD_PAD)),
            _const_spec((Q_LORA, N_HEADS * HEAD_PAD)),
            _const_spec((2 * KV_LORA, N_HEADS * HEAD_PAD)),
            _const_spec((KV_LORA, N_HEADS * HEAD_PAD)),
            pl.BlockSpec((TM, LANES), pos),
            pl.BlockSpec((TM, LANES), pos),
            pl.BlockSpec((TM, LANES), pos),
        ],
        out_specs=[
            pl.BlockSpec((D_SSM, TM), lambda i: (0, i)),
            pl.BlockSpec((TM, N_HEADS * HEAD_PAD), row),
            pl.BlockSpec((TM, N_HEADS * HEAD_PAD), row),
            pl.BlockSpec((TM, N_HEADS * HEAD_PAD), row),
        ],
        out_shape=[
            jax.ShapeDtypeStruct((D_SSM, N_TOT), BF16),
            jax.ShapeDtypeStruct((N_TOT, N_HEADS * HEAD_PAD), BF16),
            jax.ShapeDtypeStruct((N_TOT, N_HEADS * HEAD_PAD), BF16),
            jax.ShapeDtypeStruct((N_TOT, N_HEADS * HEAD_PAD), BF16),
        ],
        compiler_params=_cparams("parallel"),
        name="in_proj",
    )(*xs, mod, g_mix, wut, win, qg, kvg, wq1, wq2, wkk, wv, cq_t, sq_t, cs_t)


def _attn_kernel(*refs, latent_pieces, tq):
    if latent_pieces:
        q_ref, k_ref, v_ref, kc_ref, vc_ref, o_ref, s_scr = refs
    else:
        q_ref, kc_ref, vc_ref, _, o_ref, s_scr = refs
    heads = [slice(hh * HEAD_PAD, (hh + 1) * HEAD_PAD) for hh in range(2)]
    qs = [q_ref[:, sl] for sl in heads]

    def put_scores(slot, k_at, width):
        for hh in range(2):
            s_scr[slot, hh, :, :width] = lax.dot_general(
                qs[hh], k_at(heads[hh]), (((1,), (1,)), ((), ())), preferred_element_type=F32)

    def consume(carry, slot, v_at, width):
        new = []
        for hh in range(2):
            m, acc = carry[hh]
            s = s_scr[slot, hh, :, :width]
            m_new = jnp.maximum(m, jnp.max(s, axis=-1, keepdims=True))
            alpha = jnp.exp2(m - m_new)
            p = jnp.exp2(s - m_new).astype(BF16)
            acc = alpha * acc + jnp.dot(p, v_at(heads[hh]), preferred_element_type=F32)
            new.append((m_new, acc))
        return tuple(new)

    def piece(kr, vr, start, size):
        return (lambda sl: kr[start:start + size, sl]), (lambda sl: vr[start:start + size, sl]), size

    pieces = [piece(kc_ref, vc_ref, 0, CTX_LEN)]
    start = 0
    for size in latent_pieces:
        pieces.append(piece(k_ref, v_ref, start, size))
        start += size
    carry = tuple((jnp.full((tq, 1), -jnp.inf, F32), jnp.zeros((tq, HEAD_PAD), F32))
                  for _ in range(2))
    put_scores(0, pieces[0][0], pieces[0][2])
    for i, (_, v_at, size) in enumerate(pieces):
        if i + 1 < len(pieces):
            put_scores((i + 1) % 2, pieces[i + 1][0], pieces[i + 1][2])
        carry = consume(carry, i % 2, v_at, size)
    outs = [acc[:, :V_HEAD] / acc[:, V_HEAD:V_HEAD + 1] for _, acc in carry]
    o_ref[...] = jnp.concatenate(outs, axis=-1).astype(o_ref.dtype)


def _attn_latent_call(q, k, v):
    qt = SEQ // TQ
    ctx0 = N_LAT // CTX_LEN
    return pl.pallas_call(
        functools.partial(_attn_kernel, latent_pieces=KEY_PIECES, tq=TQ),
        grid=(BATCH, N_HEADS // 2, qt),
        in_specs=[
            pl.BlockSpec((TQ, 2 * HEAD_PAD), lambda b, h, i: (b * qt + i, h)),
            pl.BlockSpec((SEQ, 2 * HEAD_PAD), lambda b, h, i: (b, h)),
            pl.BlockSpec((SEQ, 2 * HEAD_PAD), lambda b, h, i: (b, h)),
            pl.BlockSpec((CTX_LEN, 2 * HEAD_PAD), lambda b, h, i: (ctx0 + b, h)),
            pl.BlockSpec((CTX_LEN, 2 * HEAD_PAD), lambda b, h, i: (ctx0 + b, h)),
        ],
        out_specs=pl.BlockSpec((TQ, 2 * V_HEAD), lambda b, h, i: (b * qt + i, h)),
        out_shape=jax.ShapeDtypeStruct((N_TOT, D_ATT), BF16),
        scratch_shapes=[pltpu.VMEM((2, 2, TQ, max(KEY_PIECES)), F32)],
        compiler_params=_cparams("parallel", "parallel", "arbitrary"),
        name="attn_latent",
    )(q, k, v, k, v)


def _attn_ctx_call(q, k, v, y_att):
    ctx0 = N_LAT // CTX_LEN
    return pl.pallas_call(
        functools.partial(_attn_kernel, latent_pieces=(), tq=CTX_LEN),
        grid=(BATCH, N_HEADS // 2),
        in_specs=[
            pl.BlockSpec((CTX_LEN, 2 * HEAD_PAD), lambda b, h: (ctx0 + b, h)),
            pl.BlockSpec((CTX_LEN, 2 * HEAD_PAD), lambda b, h: (ctx0 + b, h)),
            pl.BlockSpec((CTX_LEN, 2 * HEAD_PAD), lambda b, h: (ctx0 + b, h)),
            pl.BlockSpec(memory_space=pl.ANY),
        ],
        out_specs=pl.BlockSpec((CTX_LEN, 2 * V_HEAD), lambda b, h: (ctx0 + b, h)),
        out_shape=jax.ShapeDtypeStruct((N_TOT, D_ATT), BF16),
        input_output_aliases={3: 0},
        scratch_shapes=[pltpu.VMEM((1, 2, CTX_LEN, CTX_LEN), F32)],
        compiler_params=_cparams("parallel", "parallel"),
        name="attn_ctx",
    )(q, k, v, y_att)


SSM_C = 64
SSM_HALVES = SSM_T // SSM_C
SSM_KC = SSM_GROUP * SSM_C
LAG_ROWS = 2 * SSM_C
PT_LAG, PT_INC, PT_OUT = 0, LAG_ROWS, LAG_ROWS + SSM_C
PT_ROWS = LAG_ROWS + 2 * SSM_C
NT_DIMS = (((1,), (1,)), ((), ()))


def _ssm_kernel(u_ref, pta_ref, ptb_ref, rows_ref, cc_ref, at_ref, dv_ref, y_ref,
                abt_hi_scr, abt_lo_scr, wl_scr, m_scr, ws_scr, wct_scr, s_scr, h_scr):
    H, C = SSM_GROUP, SSM_C
    half_lane = lax.broadcasted_iota(jnp.int32, (1, LANES), 1) < C

    def split(v):
        hi = v.astype(BF16)
        return hi, (v - hi.astype(F32)).astype(BF16)

    def scaled(row0, n_rows, ra, rb):
        return pta_ref[0, row0:row0 + n_rows, :] * ra + ptb_ref[0, row0:row0 + n_rows, :] * rb

    def build_tables(i, _):
        row0 = pl.multiple_of(i * C, C)
        b_re = rows_ref[0, pl.ds(i, 1), :]
        b_im = rows_ref[0, pl.ds(H + i, 1), :]
        lag_rows = pl.ds(pl.multiple_of(i * LAG_ROWS, LAG_ROWS), LAG_ROWS)
        abt_hi_scr[lag_rows, :], abt_lo_scr[lag_rows, :] = split(
            scaled(PT_LAG, LAG_ROWS, b_re, b_im))
        ws_scr[pl.ds(row0, C), :] = scaled(PT_INC, C, b_re, b_im).astype(BF16)
        c_re = rows_ref[0, pl.ds(2 * H + i, 1), :]
        c_im = rows_ref[0, pl.ds(3 * H + i, 1), :]
        wct_scr[pl.ds(row0, C), :] = scaled(PT_OUT, C, c_re, c_im).astype(BF16)
        return 0

    lax.fori_loop(0, H, build_tables, 0)
    cc_hi, cc_lo = split(cc_ref[0])
    nt = lambda a, b: lax.dot_general(a, b, NT_DIMS, preferred_element_type=F32)
    wl = nt(cc_hi, abt_hi_scr[...]) + nt(cc_hi, abt_lo_scr[...]) + nt(cc_lo, abt_hi_scr[...])
    wl_scr[0] = wl
    wl_scr[1] = jnp.concatenate(
        [pltpu.roll(wl[:, ci * LAG_ROWS:(ci + 1) * LAG_ROWS], C, 1) for ci in range(H)], axis=1)

    slot2 = lax.broadcasted_iota(jnp.int32, (1, 2 * LANES), 1)
    first_low = (slot2 < C) | (slot2 > 2 * LANES - C)
    first_high = slot2 < LANES

    def toeplitz_pair(x):
        return pltpu.roll(jnp.broadcast_to(x, (C, 2 * LANES)), 0, 1, stride=1, stride_axis=0)

    def build_toeplitz(ci, _):
        row0 = pl.multiple_of(ci * C, C)
        slots = pl.ds(pl.multiple_of(ci * LAG_ROWS, LAG_ROWS), LAG_ROWS)

        def lags(co, shifted):
            v = wl_scr[int(shifted), co:co + 1, slots]
            return jnp.concatenate([v, v], axis=1)

        for ka in range(0, H // 2, 2):
            kb = ka + 1
            low = toeplitz_pair(jnp.where(first_low, lags(2 * ka, False), lags(2 * kb, False)))
            high = toeplitz_pair(jnp.where(first_high, lags(2 * ka + 1, True),
                                           lags(2 * kb + 1, True)))
            for k, lanes in ((ka, slice(0, LANES)), (kb, slice(LANES, 2 * LANES))):
                m_scr[pl.ds(row0, C), k * LANES:(k + 1) * LANES] = jnp.where(
                    half_lane, low[:, lanes], high[:, lanes]).astype(BF16)
        return 0

    lax.fori_loop(0, H, build_toeplitz, 0)

    def chunk_operand(hf):
        cols = []
        for k in range(H // 2):
            a, b = u_ref[2 * k].astype(F32), u_ref[2 * k + 1].astype(F32)
            if hf == 0:
                cols.append(jnp.where(half_lane, a, pltpu.roll(b, C, 1)))
            else:
                cols.append(jnp.where(half_lane, pltpu.roll(a, C, 1), b))
        return jnp.concatenate(cols, axis=1).astype(BF16)

    ys = []
    for hf in range(SSM_HALVES):
        u = chunk_operand(hf)
        ys.append(jnp.dot(u, m_scr[...], preferred_element_type=F32))
        s = jnp.dot(u, ws_scr[...], preferred_element_type=F32)
        for d in range(2):
            s_d = s[:, d * LANES:(d + 1) * LANES]
            s_scr[hf, d] = s_d
            s_scr[hf, 2 + d] = pltpu.roll(s_d, SSM_STATE, 1)

    n_lat, n_ctx = SEQ // SSM_T, CTX_LEN // SSM_T
    ctx = [((N_CHUNK_LAT + c, n_ctx), hf) for c in range(n_ctx) for hf in range(SSM_HALVES)]
    lat = [((k, n_lat), hf) for k in range(n_lat) for hf in range(SSM_HALVES)]
    coef = [jnp.broadcast_to(at_ref[0, r:r + 1, :], (BATCH, 2 * SSM_STATE)) for r in range(4)]

    def advance(h, h_sw, d, rows, hf):
        a0, a1 = coef[2 * d], coef[2 * d + 1]
        return (h * a0 + h_sw * a1 + s_scr[hf, d, rows, :],
                h_sw * a0 - h * a1 + s_scr[hf, 2 + d, rows, :])

    zero = jnp.zeros((BATCH, 2 * SSM_STATE), F32)
    h_f, h_f_sw, h_r, h_r_sw = zero, zero, zero, zero
    for ((sf, stf), cf), ((sr, strd), cr) in zip(ctx + lat, ctx[::-1] + lat[::-1]):
        rows_f = pl.ds(sf, BATCH, stride=stf)
        rows_r = pl.ds(sr, BATCH, stride=strd)
        h_scr[cf, 0, rows_f, :] = h_f
        h_f, h_f_sw = advance(h_f, h_f_sw, 0, rows_f, cf)
        h_scr[cr, 1, rows_r, :] = h_r
        h_r, h_r_sw = advance(h_r, h_r_sw, 1, rows_r, cr)

    for hf in range(SSM_HALVES):
        h_in = jnp.concatenate([h_scr[hf, 0], h_scr[hf, 1]], axis=-1).astype(BF16)
        ys[hf] = ys[hf] + lax.dot_general(h_in, wct_scr[...], NT_DIMS,
                                          preferred_element_type=F32)
    for k in range(H // 2):
        y0, y1 = ys[0][:, k * LANES:(k + 1) * LANES], ys[1][:, k * LANES:(k + 1) * LANES]
        for c, yc in ((2 * k, jnp.where(half_lane, y0, pltpu.roll(y1, C, 1))),
                      (2 * k + 1, jnp.where(half_lane, pltpu.roll(y0, C, 1), y1))):
            skip = u_ref[c].astype(F32) * dv_ref[0, :, c * SSM_T:(c + 1) * SSM_T]
            y_ref[c] = (yc + skip).astype(y_ref.dtype)


def _ssm_call(u_t, pta, ptb, rows, cc, at, dv):
    g3 = lambda g: (g, 0, 0)
    return pl.pallas_call(
        _ssm_kernel,
        grid=(N_SSM_GROUPS,),
        in_specs=[
            pl.BlockSpec((SSM_GROUP, N_CHUNK, SSM_T), g3),
            pl.BlockSpec((1, PT_ROWS, 4 * SSM_STATE), g3),
            pl.BlockSpec((1, PT_ROWS, 4 * SSM_STATE), g3),
            pl.BlockSpec((1, 4 * SSM_GROUP, 4 * SSM_STATE), g3),
            pl.BlockSpec((1, SSM_GROUP, 4 * SSM_STATE), g3),
            pl.BlockSpec((1, 4, 2 * SSM_STATE), g3),
            pl.BlockSpec((1, 1, SSM_K), g3),
        ],
        out_specs=pl.BlockSpec((SSM_GROUP, N_CHUNK, SSM_T), g3),
        out_shape=jax.ShapeDtypeStruct((D_SSM, N_CHUNK, SSM_T), BF16),
        scratch_shapes=[
            pltpu.VMEM((SSM_GROUP * LAG_ROWS, 4 * SSM_STATE), BF16),
            pltpu.VMEM((SSM_GROUP * LAG_ROWS, 4 * SSM_STATE), BF16),
            pltpu.VMEM((2, SSM_GROUP, SSM_GROUP * LAG_ROWS), F32),
            pltpu.VMEM((SSM_KC, SSM_KC), BF16),
            pltpu.VMEM((SSM_KC, 4 * SSM_STATE), BF16),
            pltpu.VMEM((SSM_KC, 4 * SSM_STATE), BF16),
            pltpu.VMEM((SSM_HALVES, 4, N_CHUNK, 2 * SSM_STATE), F32),
            pltpu.VMEM((SSM_HALVES, 2, N_CHUNK, 2 * SSM_STATE), F32),
        ],
        compiler_params=_cparams("parallel"),
        name="s5_mixer",
    )(u_t, pta, ptb, rows, cc, at, dv)


def _ssm_tables(a_re, a_im, log_dt, b_re, b_im, c_re, c_im, d_skip):
    G, P, H, T = N_SSM_GROUPS, SSM_STATE, SSM_GROUP, SSM_T
    a_re, a_im = a_re.astype(F32), a_im.astype(F32)
    dt = jnp.exp(log_dt.astype(F32))[..., None]
    den = a_re * a_re + a_im * a_im
    mag1 = jnp.exp(dt * a_re)
    ab_re, ab_im = mag1 * jnp.cos(dt * a_im), mag1 * jnp.sin(dt * a_im)
    num_re = ab_re - 1.0
    f_re = (num_re * a_re + ab_im * a_im) / den
    f_im = (ab_im * a_re - num_re * a_im) / den
    b_re, b_im = b_re.astype(F32), b_im.astype(F32)
    bb_re = f_re[..., None] * b_re - f_im[..., None] * b_im
    bb_im = f_re[..., None] * b_im + f_im[..., None] * b_re
    c_re, c_im = c_re.astype(F32), c_im.astype(F32)
    la, th = dt * a_re, dt * a_im

    def powers(d, n):
        mag = jnp.exp(la[d][:, None, :] * n[None, :, None])
        ph = th[d][:, None, :] * n[None, :, None]
        return mag * jnp.cos(ph), mag * jnp.sin(ph)

    C = SSM_C
    slot = np.arange(LAG_ROWS)
    lag_f = np.where(slot < C, slot, -1)
    lag_r = np.where((-slot) % LAG_ROWS < C, (-slot) % LAG_ROWS, -1)
    s_idx = np.arange(C)
    f_all = np.concatenate([lag_f, C - 1 - s_idx, s_idx + 1])
    r_all = np.concatenate([lag_r, s_idx, C - s_idx])
    f_exp, r_exp = jnp.asarray(np.maximum(f_all, 0), F32), jnp.asarray(np.maximum(r_all, 0), F32)
    f_on, r_on = jnp.asarray(f_all >= 0, F32), jnp.asarray(r_all >= 0, F32)
    lane = jnp.arange(4 * P)
    is_fwd = (lane < 2 * P)[None, :]
    is_re = ((lane // P) % 2 == 0)[None, None, :]
    expo = jnp.where(is_fwd, f_exp[:, None], r_exp[:, None])[None]
    on = jnp.where(is_fwd, f_on[:, None], r_on[:, None])[None]
    la4 = jnp.concatenate([la[0], la[0], la[1], la[1]], axis=-1)[:, None, :]
    th4 = jnp.concatenate([th[0], th[0], th[1], th[1]], axis=-1)[:, None, :]
    mag = jnp.exp(la4 * expo) * on
    p_re, p_im = mag * jnp.cos(th4 * expo), mag * jnp.sin(th4 * expo)
    pta = jnp.where(is_re, p_re, p_im)
    ptb = jnp.where(is_re, -p_im, p_re)

    def per_channel(v):
        f, r = v[0].transpose(0, 2, 1), v[1].transpose(0, 2, 1)
        return jnp.concatenate([f, f, r, r], axis=-1)
    cf_re, cr_re = c_re[0], c_re[1]
    cf_im, cr_im = c_im[0], c_im[1]
    rows = jnp.concatenate([
        per_channel(bb_re), per_channel(bb_im),
        jnp.concatenate([cf_re, -cf_re, cr_re, -cr_re], axis=-1),
        jnp.concatenate([cf_im, -cf_im, cr_im, -cr_im], axis=-1)], axis=1)
    cc = jnp.concatenate([cf_re, -cf_im, cr_re, -cr_im], axis=-1)

    t_exp = jnp.full((1,), float(SSM_C), F32)
    (f_re_t, f_im_t), (r_re_t, r_im_t) = powers(0, t_exp), powers(1, t_exp)
    f_re_t, f_im_t, r_re_t, r_im_t = (v[:, 0, :] for v in (f_re_t, f_im_t, r_re_t, r_im_t))
    at = jnp.stack([jnp.concatenate([f_re_t, f_re_t], -1), jnp.concatenate([-f_im_t, f_im_t], -1),
                    jnp.concatenate([r_re_t, r_re_t], -1), jnp.concatenate([-r_im_t, r_im_t], -1)],
                   axis=1)
    dv = jnp.repeat(d_skip.astype(F32).reshape(G, H), T, axis=-1).reshape(G, 1, H * T)
    return pta, ptb, rows, cc, at, dv


ROUTE_E1, ROUTE_E2, ROUTE_W1, ROUTE_W2 = 0, 1, 2, 3


def _top2_route(logits):
    lane = lax.broadcasted_iota(jnp.int32, logits.shape, 1)
    lg = jnp.where(lane < N_EXPERTS, logits, -jnp.inf)
    m1 = jnp.max(lg, axis=-1, keepdims=True)
    i1 = jnp.min(jnp.where(lg == m1, lane, LANES), axis=-1, keepdims=True)
    lg2 = jnp.where(lane == i1, -jnp.inf, lg)
    m2 = jnp.max(lg2, axis=-1, keepdims=True)
    i2 = jnp.min(jnp.where(lg2 == m2, lane, LANES), axis=-1, keepdims=True)
    e2 = jnp.exp(m2 - m1)
    w1 = 1.0 / (1.0 + e2)
    rec = jnp.where(lane == ROUTE_E1, i1.astype(F32), 0.0)
    rec = jnp.where(lane == ROUTE_E2, i2.astype(F32), rec)
    rec = jnp.where(lane == ROUTE_W1, w1, rec)
    return jnp.where(lane == ROUTE_W2, e2 * w1, rec), i1, i2


def _expert_slots(i1, i2, taken):
    n = i1.shape[0]
    lane = lax.broadcasted_iota(jnp.int32, (n, LANES), 1)
    picked = jnp.where((lane == i1) | (lane == i2), 1.0, 0.0)
    earlier = (lax.broadcasted_iota(jnp.int32, (n, n), 1)
               < lax.broadcasted_iota(jnp.int32, (n, n), 0)).astype(BF16)
    rank = jnp.dot(earlier, picked.astype(BF16), preferred_element_type=F32)
    slot = rank + taken + (lane * EXPERT_CAP).astype(F32)
    d1 = jnp.sum(jnp.where(lane == i1, slot, 0.0), axis=-1, keepdims=True)
    d2 = jnp.sum(jnp.where(lane == i2, slot, 0.0), axis=-1, keepdims=True)
    dest = jnp.where(lane == 0, d1, jnp.where(lane == 1, d2, 0.0)).astype(jnp.int32)
    return dest, jnp.sum(picked, axis=0, keepdims=True)


def _pack_bf16_pairs(v):
    k = v.shape[1] // 2
    bits = pltpu.bitcast(v.astype(BF16).astype(F32), jnp.uint32)
    return (bits[:, :k] & jnp.uint32(0xFFFF0000)) | (bits[:, k:] >> 16)


def _unpack_bf16_pairs(w):
    hi = pltpu.bitcast(w & jnp.uint32(0xFFFF0000), F32)
    lo = pltpu.bitcast(w << 16, F32)
    return hi, lo


def _store_packed(ref, v):
    words = _pack_bf16_pairs(v)
    for s in range(SC_SPLIT):
        ref[s] = words[:, s * SC_ROW:(s + 1) * SC_ROW]


def _load_packed(ref):
    return _unpack_bf16_pairs(jnp.concatenate([ref[s] for s in range(SC_SPLIT)], axis=-1))


def _mix_kernel(*refs, with_router, n_x):
    rest = refs[n_x:]
    if with_router:
        (yst_ref, ya_ref, mod_ref, wglut_ref, bglu_ref, wout_ref, gffn_ref, router_ref,
         x1_ref, h2_ref, gate_ref, dest_ref, cnt_ref, taken_scr) = rest
    else:
        (yst_ref, ya_ref, mod_ref, wglut_ref, bglu_ref, wout_ref, gffn_ref,
         w1_ref, w3_ref, w2_ref, x2_ref) = rest
    zt = jax.nn.gelu(yst_ref[...].astype(F32), approximate=True)
    glt = zt * jax.nn.sigmoid(
        jnp.dot(wglut_ref[...], zt.astype(BF16), preferred_element_type=F32) + bglu_ref[...])
    mix = jnp.concatenate([glt.T.astype(BF16), ya_ref[...]], axis=-1)
    o = jnp.dot(mix, wout_ref[...], preferred_element_type=F32)
    x1 = _tile_rows(refs, n_x) + mod_ref[0, 2:3, :] * o
    h2 = _rms(x1, gffn_ref[...]) * (1.0 + mod_ref[0, 4:5, :]) + mod_ref[0, 3:4, :]
    if not with_router:
        acc = _swiglu(h2.astype(BF16), lambda sl: w1_ref[:, sl], lambda sl: w3_ref[:, sl],
                      lambda sl: w2_ref[sl, :], D_FF)
        x2_ref[...] = x1 + mod_ref[0, 5:6, :] * acc
    else:
        x1_ref[...] = x1
        _store_packed(h2_ref, h2)
        h_hi = h2.astype(BF16)
        h_lo = (h2 - h_hi.astype(F32)).astype(BF16)
        o1 = jnp.dot(h_hi, router_ref[0], preferred_element_type=F32)
        o2 = jnp.dot(h_lo, router_ref[1], preferred_element_type=F32)
        logits = o1 + pltpu.roll(o1, LANES - N_EXPERTS, 1) + o2
        gate_ref[...], i1, i2 = _top2_route(logits)

        @pl.when(pl.program_id(0) == 0)
        def _():
            taken_scr[...] = jnp.zeros_like(taken_scr)

        dest_ref[...], tile_cnt = _expert_slots(i1, i2, taken_scr[...])
        taken_scr[...] += tile_cnt
        cnt_ref[...] = jnp.broadcast_to(taken_scr[...], cnt_ref.shape)


def _mix_call(xs, y_ssm_t, y_att, mod, wglut, bglu, wout, gffn, *, router=None, ffn=None, n_tiles):
    row = lambda i: (i, 0)
    out_row = row
    with_router = router is not None
    assert with_router != (ffn is not None)
    in_specs = _x_specs(len(xs)) + [
        pl.BlockSpec((D_SSM, TM), lambda i: (0, i)),
        pl.BlockSpec((TM, D_ATT), row),
        pl.BlockSpec((1, 6, D_MODEL), lambda i: (_mod_row(i), 0, 0)),
        _const_spec((D_SSM, D_SSM)),
        _const_spec((D_SSM, 1)),
        _const_spec((D_MIX, D_MODEL)),
        _const_spec((1, D_MODEL)),
    ]
    args = [*xs, y_ssm_t, y_att, mod, wglut, bglu, wout, gffn]
    out_specs = [pl.BlockSpec((TM, D_MODEL), out_row)]
    out_shape = [jax.ShapeDtypeStruct((n_tiles * TM, D_MODEL), F32)]
    if with_router:
        in_specs.append(_const_spec((2, D_MODEL, LANES)))
        args.append(router)
        out_specs += [pl.BlockSpec((SC_SPLIT, TM, SC_ROW), lambda i: (0, i, 0)),
                      pl.BlockSpec((TM, LANES), out_row),
                      pl.BlockSpec((TM, LANES), out_row),
                      pl.BlockSpec((8, LANES), lambda i: (0, 0))]
        out_shape += [jax.ShapeDtypeStruct((SC_SPLIT, n_tiles * TM, SC_ROW), jnp.uint32),
                      jax.ShapeDtypeStruct((n_tiles * TM, LANES), F32),
                      jax.ShapeDtypeStruct((n_tiles * TM, LANES), jnp.int32),
                      jax.ShapeDtypeStruct((8, LANES), F32)]
        scratch = [pltpu.VMEM((1, LANES), F32)]
    else:
        in_specs += [_const_spec((D_MODEL, D_FF)), _const_spec((D_MODEL, D_FF)),
                     _const_spec((D_FF, D_MODEL))]
        args += list(ffn)
        scratch = []
    return pl.pallas_call(
        functools.partial(_mix_kernel, with_router=with_router, n_x=len(xs)),
        grid=(n_tiles,),
        in_specs=in_specs,
        out_specs=out_specs,
        out_shape=out_shape,
        scratch_shapes=scratch,
        compiler_params=_cparams("arbitrary" if with_router else "parallel"),
        name="mix_out",
    )(*args)


def _swiglu(h, w1_at, w3_at, w2_at, d_ff):
    acc = jnp.zeros((h.shape[0], D_MODEL), F32)
    for lo in range(0, d_ff, FF_CHUNK):
        sl = slice(lo, min(lo + FF_CHUNK, d_ff))
        a = jnp.dot(h, w1_at(sl), preferred_element_type=F32)
        b = jnp.dot(h, w3_at(sl), preferred_element_type=F32)
        g = (a * jax.nn.sigmoid(a) * b).astype(BF16)
        acc = acc + jnp.dot(g, w2_at(sl), preferred_element_type=F32)
    return acc


EXPERT_BLK = 512
EXPERT_CAP = N_LAT
N_SORTED = N_EXPERTS * EXPERT_CAP
N_EXPERT_BLKS = 2 * N_LAT // EXPERT_BLK + N_EXPERTS
PACKED = D_MODEL // 2
SC_ROW = 256
SC_SPLIT = PACKED // SC_ROW
SC_WIN = 128


def _sc_mesh():
    return plsc.VectorSubcoreMesh(core_axis_name="core", subcore_axis_name="subcore")


def _sc_scatter(x, idx_a, idx_b, n_out):
    n = x.shape[0]

    @pl.kernel(out_type=jax.ShapeDtypeStruct((n_out, SC_ROW), x.dtype), mesh=_sc_mesh(),
               scratch_types=[])
    def scatter(x_hbm, a_hbm, b_hbm, o_hbm):
        def body(x_vmem, a_vmem, b_vmem):
            pltpu.sync_copy(x_vmem, o_hbm.at[a_vmem.at[0]])
            pltpu.sync_copy(x_vmem, o_hbm.at[b_vmem.at[0]])

        pltpu.emit_pipeline(
            body, grid=(n // SC_WIN,),
            in_specs=[pl.BlockSpec((SC_WIN, SC_ROW), lambda i: (i, 0)),
                      pl.BlockSpec((1, SC_WIN), lambda i: (0, i)),
                      pl.BlockSpec((1, SC_WIN), lambda i: (0, i))],
            out_specs=[],
            core_axis_name=("core", "subcore"),
            dimension_semantics=(pltpu.PARALLEL,),
        )(x_hbm, a_hbm, b_hbm)

    return scatter(x, idx_a.reshape(1, n), idx_b.reshape(1, n))


def _sc_gather(y, idx):
    n = idx.shape[0]

    @pl.kernel(out_type=jax.ShapeDtypeStruct((n, SC_ROW), y.dtype), mesh=_sc_mesh(),
               scratch_types=[])
    def gather(y_hbm, i_hbm, o_hbm):
        def body(i_vmem, o_vmem):
            pltpu.sync_copy(y_hbm.at[i_vmem.at[0]], o_vmem)

        pltpu.emit_pipeline(
            body, grid=(n // SC_WIN,),
            in_specs=[pl.BlockSpec((1, SC_WIN), lambda i: (0, i))],
            out_specs=[pl.BlockSpec((SC_WIN, SC_ROW), lambda i: (i, 0))],
            core_axis_name=("core", "subcore"),
            dimension_semantics=(pltpu.PARALLEL,),
        )(i_hbm, o_hbm)

    return gather(y, idx.reshape(1, n))


W_CAST_ROWS = 128


def _experts_kernel(blk_expert_ref, blk_row_ref, n_used_ref, x_ref, w1_ref, w3_ref, w2_ref, o_ref,
                    w1_scr, w3_scr, w2_scr):
    del blk_row_ref
    b = pl.program_id(0)
    live = b < n_used_ref[0]
    new_expert = (b == 0) | (blk_expert_ref[b] != blk_expert_ref[jnp.maximum(b - 1, 0)])

    @pl.when(live & new_expert)
    def _():
        for src, dst in ((w1_ref, w1_scr), (w3_ref, w3_scr), (w2_ref, w2_scr)):
            for r in range(0, dst.shape[0], W_CAST_ROWS):
                dst[r:r + W_CAST_ROWS, :] = src[0, r:r + W_CAST_ROWS, :].astype(BF16)

    @pl.when(live)
    def _():
        hi, lo = _load_packed(x_ref)
        h = jnp.concatenate([hi.astype(BF16), lo.astype(BF16)], axis=-1)
        y = _swiglu(h, lambda sl: w1_scr[:, sl], lambda sl: w3_scr[:, sl],
                    lambda sl: w2_scr[sl, :], D_FF_EXPERT)
        _store_packed(o_ref, y)


def _experts_call(blk_expert, blk_row, n_used, xs, w1, w3, w2):
    row = lambda b, be, br, nu: (0, br[b], 0)
    wsel = lambda b, be, br, nu: (be[b], 0, 0)
    return pl.pallas_call(
        _experts_kernel,
        grid_spec=pltpu.PrefetchScalarGridSpec(
            num_scalar_prefetch=3,
            grid=(N_EXPERT_BLKS,),
            in_specs=[
                pl.BlockSpec((SC_SPLIT, EXPERT_BLK, SC_ROW), row),
                pl.BlockSpec((1, D_MODEL, D_FF_EXPERT), wsel),
                pl.BlockSpec((1, D_MODEL, D_FF_EXPERT), wsel),
                pl.BlockSpec((1, D_FF_EXPERT, D_MODEL), wsel),
            ],
            out_specs=pl.BlockSpec((SC_SPLIT, EXPERT_BLK, SC_ROW), row),
            scratch_shapes=[pltpu.VMEM((D_MODEL, D_FF_EXPERT), BF16),
                            pltpu.VMEM((D_MODEL, D_FF_EXPERT), BF16),
                            pltpu.VMEM((D_FF_EXPERT, D_MODEL), BF16)],
        ),
        out_shape=jax.ShapeDtypeStruct((SC_SPLIT, N_SORTED, SC_ROW), jnp.uint32),
        compiler_params=_cparams("arbitrary"),
        name="moe_experts",
    )(blk_expert, blk_row, n_used, xs, w1, w3, w2)


def _combine_kernel(x1_ref, r_ref, y_ref, mod_ref, fg_ref, o_ref):
    def expert_out(slot):
        hi, lo = _load_packed(y_ref.at[slot])
        return jnp.concatenate([hi, lo], axis=-1)
    w1 = r_ref[:, ROUTE_W1:ROUTE_W1 + 1]
    w2 = r_ref[:, ROUTE_W2:ROUTE_W2 + 1]
    y = w1 * expert_out(0) + w2 * expert_out(1)
    x2 = x1_ref[...] + mod_ref[0, 5:6, :] * y
    o_ref[...] = _rms(x2, fg_ref[...])


def _combine_call(x1, route, y_tok, mod, fg):
    n_tiles = N_LAT // TM
    row = lambda i: (i, 0)
    return pl.pallas_call(
        _combine_kernel,
        grid=(n_tiles,),
        in_specs=[
            pl.BlockSpec((TM, D_MODEL), row),
            pl.BlockSpec((TM, LANES), row),
            pl.BlockSpec((2, SC_SPLIT, TM, SC_ROW), lambda i: (0, 0, i, 0)),
            pl.BlockSpec((1, 6, D_MODEL), lambda i: (i // SEQ_TILES, 0, 0)),
            pl.BlockSpec((1, D_MODEL), lambda i: (0, 0)),
        ],
        out_specs=pl.BlockSpec((TM, D_MODEL), row),
        out_shape=jax.ShapeDtypeStruct((N_LAT, D_MODEL), F32),
        compiler_params=_cparams("parallel"),
        name="moe_combine",
    )(x1, route, y_tok, mod, fg)


def _moe_routed(h2p, x1, route, dest, cnt, mod, w1, w3, w2, fg):
    blks = (cnt[0, :N_EXPERTS].astype(jnp.int32) + (EXPERT_BLK - 1)) // EXPERT_BLK
    blk_end = jnp.cumsum(blks)
    n_used = blk_end[-1:]
    b = jnp.minimum(jnp.arange(N_EXPERT_BLKS, dtype=jnp.int32), n_used[0] - 1)
    blk_expert = jnp.sum((b[:, None] >= blk_end[None, :]).astype(jnp.int32), axis=1)
    blk_row = blk_expert * (EXPERT_CAP // EXPERT_BLK) + b - (blk_end - blks)[blk_expert]
    piece = jnp.arange(SC_SPLIT, dtype=jnp.int32)[:, None] * N_SORTED
    idx = [(piece + dest[:, slot][None, :]).reshape(SC_SPLIT * N_LAT) for slot in range(2)]
    xs = _sc_scatter(h2p.reshape(SC_SPLIT * N_LAT, SC_ROW), idx[0], idx[1], SC_SPLIT * N_SORTED)
    ys = _experts_call(blk_expert, blk_row, n_used, xs.reshape(SC_SPLIT, N_SORTED, SC_ROW),
                       w1, w3, w2)
    y_tok = _sc_gather(ys.reshape(SC_SPLIT * N_SORTED, SC_ROW), jnp.concatenate(idx))
    return _combine_call(x1, route, y_tok.reshape(2, SC_SPLIT, N_LAT, SC_ROW), mod, fg)


def _rope_partner_perm():
    perm, sign = [], []
    for j in range(QK_ROPE):
        first_half = (j % AXIS_ROPE) < ROPE_FREQS
        perm.append(j + ROPE_FREQS if first_half else j - ROPE_FREQS)
        sign.append(-1.0 if first_half else 1.0)
    return jnp.array(perm, jnp.int32), jnp.array(sign, F32)


def _rope_tables():
    t = jnp.arange(SEQ)
    row = (t // GRID_W).astype(F32)
    col = (t % GRID_W).astype(F32)
    inv_freq = ROPE_BASE ** (-2.0 * jnp.arange(ROPE_FREQS, dtype=F32) / AXIS_ROPE)
    ang = jnp.concatenate([row[:, None] * inv_freq, row[:, None] * inv_freq,
                           col[:, None] * inv_freq, col[:, None] * inv_freq], axis=1)
    cos = jnp.concatenate([jnp.cos(ang), jnp.ones((N_CTX, QK_ROPE), F32)], axis=0)
    sin = jnp.concatenate([jnp.sin(ang), jnp.zeros((N_CTX, QK_ROPE), F32)], axis=0)
    n = N_CTX + SEQ
    pad32 = jnp.zeros((n, HEAD_PAD - D_QK), F32)
    qs = ATT_SCALE * math.log2(math.e)
    cq = jnp.concatenate([jnp.full((n, QK_NOPE), qs, F32), qs * cos, pad32], axis=1)
    sq = jnp.concatenate([jnp.zeros((n, QK_NOPE), F32), qs * sin, pad32], axis=1)
    cs = jnp.concatenate([cos, sin, jnp.zeros((n, LANES - 2 * QK_ROPE), F32)], axis=1)
    return cq, sq, cs


def _layer_weights(w_in, w_uq, w_ukv):
    perm, sign = _rope_partner_perm()
    s0 = D_SSM + Q_LORA + KV_LORA
    kr_w = w_in[:, s0:s0 + QK_ROPE]
    wut = w_in[:, :D_SSM].T.astype(BF16)
    win = jnp.concatenate([w_in[:, D_SSM:s0], kr_w, kr_w[:, perm] * sign,
                           jnp.zeros((D_MODEL, LANES - 2 * QK_ROPE), F32)], axis=1).astype(BF16)
    uq = w_uq.reshape(Q_LORA, N_HEADS, D_QK)
    nope, rope = uq[..., :QK_NOPE], uq[..., QK_NOPE:]
    zpad = jnp.zeros((Q_LORA, N_HEADS, HEAD_PAD - D_QK), F32)
    wq1 = jnp.concatenate([nope, rope, zpad], axis=-1).reshape(Q_LORA, N_HEADS * HEAD_PAD).astype(BF16)
    wq2 = jnp.concatenate([jnp.zeros_like(nope), rope[..., perm] * sign, zpad], axis=-1)
    wq2 = wq2.reshape(Q_LORA, N_HEADS * HEAD_PAD).astype(BF16)
    ukv = w_ukv.reshape(KV_LORA, N_HEADS, QK_NOPE + V_HEAD)
    wk = jnp.concatenate([ukv[..., :QK_NOPE], jnp.zeros((KV_LORA, N_HEADS, HEAD_PAD - QK_NOPE), F32)],
                         axis=-1).reshape(KV_LORA, N_HEADS * HEAD_PAD)
    eye = jnp.eye(QK_ROPE, dtype=F32)
    place = jnp.concatenate([jnp.zeros((QK_ROPE, QK_NOPE), F32), eye,
                             jnp.zeros((QK_ROPE, HEAD_PAD - D_QK), F32)], axis=1)
    place = jnp.tile(place, (1, N_HEADS))
    spread = jnp.concatenate([place, place, jnp.zeros((LANES - 2 * QK_ROPE, N_HEADS * HEAD_PAD), F32)], 0)
    wkk = jnp.concatenate([wk, spread], axis=0).astype(BF16)
    wv = jnp.concatenate([ukv[..., QK_NOPE:], jnp.zeros((KV_LORA, N_HEADS, HEAD_PAD - V_HEAD), F32)],
                         axis=-1).reshape(KV_LORA, N_HEADS * HEAD_PAD).astype(BF16)
    return wut, win, wq1, wq2, wkk, wv


def kernel(x, c, ctx, c_ctx, w_ada, b_ada, norm_mix, norm_ffn, w_in, q_norm, kv_norm, w_uq, w_ukv,
           ssm_a_re, ssm_a_im, ssm_log_dt, ssm_b_re, ssm_b_im, ssm_c_re, ssm_c_im, ssm_d, w_glu,
           b_glu, w_out, ffn_w1, ffn_w3, ffn_w2, moe_router, moe_w1, moe_w3, moe_w2, final_norm):
    assert x.shape == (BATCH, SEQ, D_MODEL) and ctx.shape == (BATCH, CTX_LEN, D_MODEL)
    cond = jnp.concatenate([c, c_ctx[None, :], jnp.zeros((MOD_ROWS - BATCH - 1, D_MODEL), F32)], axis=0)
    mod_all = _ada_call(cond, w_ada, b_ada).reshape(DEPTH, MOD_ROWS, 6, D_MODEL)
    cq_t, sq_t, cs_t = _rope_tables()
    xs = (x.reshape(N_LAT, D_MODEL), ctx.reshape(N_CTX, D_MODEL))

    out = None
    for i in range(DEPTH):
        last = i == DEPTH - 1
        mod = mod_all[i]
        wut, win, wq1, wq2, wkk, wv = _layer_weights(w_in[i], w_uq[i], w_ukv[i])
        u_t, q, k, v = _inproj_call(xs, mod, norm_mix[i][None, :], wut, win, q_norm[i][None, :],
                                    kv_norm[i][None, :], wq1, wq2, wkk, wv, cq_t, sq_t, cs_t)
        tabs = _ssm_tables(ssm_a_re[i], ssm_a_im[i], ssm_log_dt[i], ssm_b_re[i], ssm_b_im[i],
                           ssm_c_re[i], ssm_c_im[i], ssm_d[i])
        y_ssm_t = _ssm_call(u_t.reshape(D_SSM, N_CHUNK, SSM_T), *tabs).reshape(D_SSM, N_TOT)
        y_att = _attn_latent_call(q, k, v)
        if last:
            n_tiles = LAT_TILES
        else:
            y_att = _attn_ctx_call(q, k, v, y_att)
            n_tiles = N_TOT // TM
        j = i // 2
        mix_args = (xs, y_ssm_t, y_att, mod, w_glu[i].T.astype(BF16), b_glu[i][:, None],
                    w_out[i].astype(BF16), norm_ffn[i][None, :])
        if i % 2 == 0:
            assert not last
            ffn = (ffn_w1[j].astype(BF16), ffn_w3[j].astype(BF16), ffn_w2[j].astype(BF16))
            xs = tuple(_mix_call(*mix_args, ffn=ffn, n_tiles=n_tiles))
        else:
            assert last
            r = moe_router[j]
            r_top = lax.bitcast_convert_type(
                lax.bitcast_convert_type(r, jnp.uint32) & jnp.uint32(0xFFFF0000), F32)
            r_hi = r_top.astype(BF16)
            r_lo = (r - r_top).astype(BF16)
            zr = jnp.zeros((D_MODEL, LANES - 2 * N_EXPERTS), BF16)
            router = jnp.stack([jnp.concatenate([r_hi, r_lo, zr], axis=1),
                                jnp.concatenate([r_hi, jnp.zeros_like(r_lo), zr], axis=1)])
            x1, h2p, route, dest, cnt = _mix_call(*mix_args, router=router, n_tiles=n_tiles)
            out = _moe_routed(h2p, x1, route, dest, cnt, mod, moe_w1[j], moe_w3[j], moe_w2[j],
                              final_norm[None, :])
    return out.reshape(BATCH, SEQ, D_MODEL)
```

```python
import functools
import math

import jax
import jax.numpy as jnp
import numpy as np
from jax import lax
from jax.experimental import pallas as pl
from jax.experimental.pallas import tpu as pltpu
from jax.experimental.pallas import tpu_sc as plsc

D_MODEL = 1024
BATCH = 4
SEQ = 8192
DEPTH = 2
GRID_W = 64
CTX_LEN = 256
D_SSM = 512
SSM_GROUP = 16
N_SSM_GROUPS = D_SSM // SSM_GROUP
SSM_STATE = 64
N_HEADS = 8
QK_NOPE = 64
QK_ROPE = 32
V_HEAD = 64
Q_LORA = 256
KV_LORA = 128
D_QK = QK_NOPE + QK_ROPE
D_ATT = N_HEADS * V_HEAD
D_MIX = D_SSM + D_ATT
AXIS_ROPE = QK_ROPE // 2
ROPE_FREQS = AXIS_ROPE // 2
ROPE_BASE = 10000.0
ATT_SCALE = 1.0 / math.sqrt(D_QK)
D_FF = 2816
N_EXPERTS = 8
D_FF_EXPERT = 1408
EPS = 1e-6

N_CTX = BATCH * CTX_LEN
N_LAT = BATCH * SEQ
N_TOT = N_CTX + N_LAT

LANES = 128
HEAD_PAD = 128
TM = 512
LAT_TILES = N_LAT // TM
SEQ_TILES = SEQ // TM
TQ = 1024
KEY_PIECES = (2048, 2048, 2048, 2048)
assert sum(KEY_PIECES) == SEQ
SSM_T = 128
N_CHUNK_LAT = N_LAT // SSM_T
N_CHUNK = N_TOT // SSM_T
SSM_K = SSM_GROUP * SSM_T
FF_CHUNK = 256
D_IN_REST = Q_LORA + KV_LORA + LANES
MOD_ROWS = 8
VMEM_LIMIT = 56 * 1024 * 1024

F32 = jnp.float32
BF16 = jnp.bfloat16
HI = lax.Precision.HIGHEST


def _cparams(*sem):
    return pltpu.CompilerParams(dimension_semantics=sem, vmem_limit_bytes=VMEM_LIMIT)


def _const_spec(shape):
    nd = len(shape)
    return pl.BlockSpec(shape, lambda *_: (0,) * nd, pipeline_mode=pl.Buffered(1))


def _mod_row(i):
    return jnp.where(i < LAT_TILES, i // SEQ_TILES, BATCH)


def _pos_tile(i):
    return jnp.where(i < LAT_TILES, i % SEQ_TILES, SEQ_TILES + i - LAT_TILES)


def _rms(x, g):
    ms = jnp.mean(x * x, axis=-1, keepdims=True)
    return x * lax.rsqrt(ms + EPS) * g


ADA_TN = 1536


def _ada_kernel(c_ref, w_ref, b_ref, o_ref):
    c = c_ref[...]
    s = c * jax.nn.sigmoid(c)
    o_ref[0] = jnp.dot(s, w_ref[0], precision=HI, preferred_element_type=F32) + b_ref[0]


def _ada_call(cond, w_ada, b_ada):
    n_col = 6 * D_MODEL // ADA_TN
    return pl.pallas_call(
        _ada_kernel,
        grid=(DEPTH, n_col),
        in_specs=[
            pl.BlockSpec((MOD_ROWS, D_MODEL), lambda l, j: (0, 0)),
            pl.BlockSpec((1, D_MODEL, ADA_TN), lambda l, j: (l, 0, j)),
            pl.BlockSpec((1, 1, ADA_TN), lambda l, j: (l, 0, j)),
        ],
        out_specs=pl.BlockSpec((1, MOD_ROWS, ADA_TN), lambda l, j: (l, 0, j)),
        out_shape=jax.ShapeDtypeStruct((DEPTH, MOD_ROWS, 6 * D_MODEL), F32),
        compiler_params=_cparams("arbitrary", "arbitrary"),
        name="ada_mod",
    )(cond, w_ada, b_ada.reshape(DEPTH, 1, 6 * D_MODEL))


def _tile_rows(refs, n_x):
    if n_x == 1:
        return refs[0][...]
    return jnp.where(pl.program_id(0) < LAT_TILES, refs[0][...], refs[1][...])


def _x_specs(n_x):
    if n_x == 1:
        return [pl.BlockSpec((TM, D_MODEL), lambda i: (i, 0))]
    return [pl.BlockSpec((TM, D_MODEL), lambda i: (jnp.minimum(i, LAT_TILES - 1), 0)),
            pl.BlockSpec((TM, D_MODEL), lambda i: (jnp.maximum(i - LAT_TILES, 0), 0))]


def _inproj_kernel(*refs, n_x):
    (mod_ref, g_ref, wut_ref, win_ref, qg_ref, kvg_ref, wq1_ref, wkk_ref, wv_ref,
     cq_ref, sq_ref, cs_ref, ut_ref, q_ref, k_ref, v_ref) = refs[n_x:]
    x = _tile_rows(refs, n_x)
    sh = mod_ref[0, 0:1, :]
    sc = mod_ref[0, 1:2, :]
    xm = (_rms(x, g_ref[...]) * (1.0 + sc) + sh).astype(BF16)
    ut_ref[...] = lax.dot_general(wut_ref[...], xm, (((1,), (1,)), ((), ())),
                                  preferred_element_type=F32).astype(ut_ref.dtype)
    z = jnp.dot(xm, win_ref[...], preferred_element_type=F32)
    qn = _rms(z[:, :Q_LORA], qg_ref[...]).astype(BF16)
    kvn = _rms(z[:, Q_LORA:Q_LORA + KV_LORA], kvg_ref[...]).astype(BF16)
    krr = (z[:, Q_LORA + KV_LORA:] * cs_ref[...]).astype(BF16)
    q1 = jnp.dot(qn, wq1_ref[...], preferred_element_type=F32)
    cq = cq_ref[...]
    sq = sq_ref[...]
    lane = lax.broadcasted_iota(jnp.int32, (1, HEAD_PAD), 1)
    first_half = (lane - QK_NOPE) % AXIS_ROPE < ROPE_FREQS
    for h in range(N_HEADS):
        sl = slice(h * HEAD_PAD, (h + 1) * HEAD_PAD)
        qh = q1[:, sl]
        partner = jnp.where(first_half, -pltpu.roll(qh, HEAD_PAD - ROPE_FREQS, 1),
                            pltpu.roll(qh, ROPE_FREQS, 1))
        q_ref[:, sl] = (qh * cq + partner * sq).astype(q_ref.dtype)
    kin = jnp.concatenate([kvn, krr], axis=-1)
    k_ref[...] = jnp.dot(kin, wkk_ref[...], preferred_element_type=F32).astype(k_ref.dtype)
    vv = jnp.dot(kvn, wv_ref[...], preferred_element_type=F32)
    lane = lax.broadcasted_iota(jnp.int32, vv.shape, 1)
    v_ref[...] = jnp.where(lane % HEAD_PAD == V_HEAD, 1.0, vv).astype(v_ref.dtype)


def _inproj_call(xs, mod, g_mix, wut, win, qg, kvg, wq1, wkk, wv, cq_t, sq_t, cs_t):
    n_tiles = N_TOT // TM
    row = lambda i: (i, 0)
    pos = lambda i: (_pos_tile(i), 0)
    return pl.pallas_call(
        functools.partial(_inproj_kernel, n_x=len(xs)),
        grid=(n_tiles,),
        in_specs=_x_specs(len(xs)) + [
            pl.BlockSpec((1, 6, D_MODEL), lambda i: (_mod_row(i), 0, 0)),
            _const_spec((1, D_MODEL)),
            _const_spec((D_SSM, D_MODEL)),
            _const_spec((D_MODEL, D_IN_REST)),
            _const_spec((1, Q_LORA)),
            _const_spec((1, KV_LORA)),
            _const_spec((Q_LORA, N_HEADS * HEAD_PAD)),
            _const_spec((2 * KV_LORA, N_HEADS * HEAD_PAD)),
            _const_spec((KV_LORA, N_HEADS * HEAD_PAD)),
            pl.BlockSpec((TM, LANES), pos),
            pl.BlockSpec((TM, LANES), pos),
            pl.BlockSpec((TM, LANES), pos),
        ],
        out_specs=[
            pl.BlockSpec((D_SSM, TM), lambda i: (0, i)),
            pl.BlockSpec((TM, N_HEADS * HEAD_PAD), row),
            pl.BlockSpec((TM, N_HEADS * HEAD_PAD), row),
            pl.BlockSpec((TM, N_HEADS * HEAD_PAD), row),
        ],
        out_shape=[
            jax.ShapeDtypeStruct((D_SSM, N_TOT), BF16),
            jax.ShapeDtypeStruct((N_TOT, N_HEADS * HEAD_PAD), BF16),
            jax.ShapeDtypeStruct((N_TOT, N_HEADS * HEAD_PAD), BF16),
            jax.ShapeDtypeStruct((N_TOT, N_HEADS * HEAD_PAD), BF16),
        ],
        compiler_params=_cparams("parallel"),
        name="in_proj",
    )(*xs, mod, g_mix, wut, win, qg, kvg, wq1, wkk, wv, cq_t, sq_t, cs_t)


def _attn_kernel(*refs, latent_pieces, tq):
    if latent_pieces:
        q_ref, k_ref, v_ref, kc_ref, vc_ref, o_ref, s_scr = refs
    else:
        q_ref, kc_ref, vc_ref, _, o_ref, s_scr = refs
    heads = [slice(hh * HEAD_PAD, (hh + 1) * HEAD_PAD) for hh in range(2)]
    qs = [q_ref[:, sl] for sl in heads]

    def put_scores(slot, k_at, width):
        for hh in range(2):
            s_scr[slot, hh, :, :width] = lax.dot_general(
                qs[hh], k_at(heads[hh]), (((1,), (1,)), ((), ())), preferred_element_type=F32)

    def consume(carry, slot, v_at, width):
        new = []
        for hh in range(2):
            m, acc = carry[hh]
            s = s_scr[slot, hh, :, :width]
            m_new = jnp.maximum(m, jnp.max(s, axis=-1, keepdims=True))
            alpha = jnp.exp2(m - m_new)
            p = jnp.exp2(s - m_new).astype(BF16)
            acc = alpha * acc + jnp.dot(p, v_at(heads[hh]), preferred_element_type=F32)
            new.append((m_new, acc))
        return tuple(new)

    def piece(kr, vr, start, size):
        return (lambda sl: kr[start:start + size, sl]), (lambda sl: vr[start:start + size, sl]), size

    pieces = [piece(kc_ref, vc_ref, 0, CTX_LEN)]
    start = 0
    for size in latent_pieces:
        pieces.append(piece(k_ref, v_ref, start, size))
        start += size
    carry = tuple((jnp.full((tq, 1), -jnp.inf, F32), jnp.zeros((tq, HEAD_PAD), F32))
                  for _ in range(2))
    put_scores(0, pieces[0][0], pieces[0][2])
    for i, (_, v_at, size) in enumerate(pieces):
        if i + 1 < len(pieces):
            put_scores((i + 1) % 2, pieces[i + 1][0], pieces[i + 1][2])
        carry = consume(carry, i % 2, v_at, size)
    outs = [acc[:, :V_HEAD] / acc[:, V_HEAD:V_HEAD + 1] for _, acc in carry]
    o_ref[...] = jnp.concatenate(outs, axis=-1).astype(o_ref.dtype)


def _attn_latent_call(q, k, v):
    qt = SEQ // TQ
    ctx0 = N_LAT // CTX_LEN
    return pl.pallas_call(
        functools.partial(_attn_kernel, latent_pieces=KEY_PIECES, tq=TQ),
        grid=(BATCH, N_HEADS // 2, qt),
        in_specs=[
            pl.BlockSpec((TQ, 2 * HEAD_PAD), lambda b, h, i: (b * qt + i, h)),
            pl.BlockSpec((SEQ, 2 * HEAD_PAD), lambda b, h, i: (b, h)),
            pl.BlockSpec((SEQ, 2 * HEAD_PAD), lambda b, h, i: (b, h)),
            pl.BlockSpec((CTX_LEN, 2 * HEAD_PAD), lambda b, h, i: (ctx0 + b, h)),
            pl.BlockSpec((CTX_LEN, 2 * HEAD_PAD), lambda b, h, i: (ctx0 + b, h)),
        ],
        out_specs=pl.BlockSpec((TQ, 2 * V_HEAD), lambda b, h, i: (b * qt + i, h)),
        out_shape=jax.ShapeDtypeStruct((N_TOT, D_ATT), BF16),
        scratch_shapes=[pltpu.VMEM((2, 2, TQ, max(KEY_PIECES)), F32)],
        compiler_params=_cparams("parallel", "parallel", "arbitrary"),
        name="attn_latent",
    )(q, k, v, k, v)


def _attn_ctx_call(q, k, v, y_att):
    ctx0 = N_LAT // CTX_LEN
    return pl.pallas_call(
        functools.partial(_attn_kernel, latent_pieces=(), tq=CTX_LEN),
        grid=(BATCH, N_HEADS // 2),
        in_specs=[
            pl.BlockSpec((CTX_LEN, 2 * HEAD_PAD), lambda b, h: (ctx0 + b, h)),
            pl.BlockSpec((CTX_LEN, 2 * HEAD_PAD), lambda b, h: (ctx0 + b, h)),
            pl.BlockSpec((CTX_LEN, 2 * HEAD_PAD), lambda b, h: (ctx0 + b, h)),
            pl.BlockSpec(memory_space=pl.ANY),
        ],
        out_specs=pl.BlockSpec((CTX_LEN, 2 * V_HEAD), lambda b, h: (ctx0 + b, h)),
        out_shape=jax.ShapeDtypeStruct((N_TOT, D_ATT), BF16),
        input_output_aliases={3: 0},
        scratch_shapes=[pltpu.VMEM((1, 2, CTX_LEN, CTX_LEN), F32)],
        compiler_params=_cparams("parallel", "parallel"),
        name="attn_ctx",
    )(q, k, v, y_att)


SSM_C = 64
SSM_HALVES = SSM_T // SSM_C
SSM_KC = SSM_GROUP * SSM_C
LAG_ROWS = 2 * SSM_C
PT_LAG, PT_INC, PT_OUT = 0, LAG_ROWS, LAG_ROWS + SSM_C
PT_ROWS = LAG_ROWS + 2 * SSM_C
NT_DIMS = (((1,), (1,)), ((), ()))


def _ssm_kernel(u_ref, pta_ref, ptb_ref, rows_ref, cc_ref, at_ref, dv_ref, y_ref,
                abt_hi_scr, abt_lo_scr, wl_scr, m_scr, ws_scr, wct_scr, s_scr, h_scr):
    H, C = SSM_GROUP, SSM_C
    half_lane = lax.broadcasted_iota(jnp.int32, (1, LANES), 1) < C

    def split(v):
        hi = v.astype(BF16)
        return hi, (v - hi.astype(F32)).astype(BF16)

    def scaled(row0, n_rows, ra, rb):
        return pta_ref[0, row0:row0 + n_rows, :] * ra + ptb_ref[0, row0:row0 + n_rows, :] * rb

    def build_tables(i, _):
        row0 = pl.multiple_of(i * C, C)
        b_re = rows_ref[0, pl.ds(i, 1), :]
        b_im = rows_ref[0, pl.ds(H + i, 1), :]
        lag_rows = pl.ds(pl.multiple_of(i * LAG_ROWS, LAG_ROWS), LAG_ROWS)
        abt_hi_scr[lag_rows, :], abt_lo_scr[lag_rows, :] = split(
            scaled(PT_LAG, LAG_ROWS, b_re, b_im))
        ws_scr[pl.ds(row0, C), :] = scaled(PT_INC, C, b_re, b_im).astype(BF16)
        c_re = rows_ref[0, pl.ds(2 * H + i, 1), :]
        c_im = rows_ref[0, pl.ds(3 * H + i, 1), :]
        wct_scr[pl.ds(row0, C), :] = scaled(PT_OUT, C, c_re, c_im).astype(BF16)
        return 0

    lax.fori_loop(0, H, build_tables, 0)
    cc_hi, cc_lo = split(cc_ref[0])
    nt = lambda a, b: lax.dot_general(a, b, NT_DIMS, preferred_element_type=F32)
    wl = nt(cc_hi, abt_hi_scr[...]) + nt(cc_hi, abt_lo_scr[...]) + nt(cc_lo, abt_hi_scr[...])
    wl_scr[0] = wl
    wl_scr[1] = jnp.concatenate(
        [pltpu.roll(wl[:, ci * LAG_ROWS:(ci + 1) * LAG_ROWS], C, 1) for ci in range(H)], axis=1)

    slot2 = lax.broadcasted_iota(jnp.int32, (1, 2 * LANES), 1)
    first_low = (slot2 < C) | (slot2 > 2 * LANES - C)
    first_high = slot2 < LANES

    def toeplitz_pair(x):
        return pltpu.roll(jnp.broadcast_to(x, (C, 2 * LANES)), 0, 1, stride=1, stride_axis=0)

    def build_toeplitz(ci, _):
        row0 = pl.multiple_of(ci * C, C)
        slots = pl.ds(pl.multiple_of(ci * LAG_ROWS, LAG_ROWS), LAG_ROWS)

        def lags(co, shifted):
            v = wl_scr[int(shifted), co:co + 1, slots]
            return jnp.concatenate([v, v], axis=1)

        for ka in range(0, H // 2, 2):
            kb = ka + 1
            low = toeplitz_pair(jnp.where(first_low, lags(2 * ka, False), lags(2 * kb, False)))
            high = toeplitz_pair(jnp.where(first_high, lags(2 * ka + 1, True),
                                           lags(2 * kb + 1, True)))
            for k, lanes in ((ka, slice(0, LANES)), (kb, slice(LANES, 2 * LANES))):
                m_scr[pl.ds(row0, C), k * LANES:(k + 1) * LANES] = jnp.where(
                    half_lane, low[:, lanes], high[:, lanes]).astype(BF16)
        return 0

    lax.fori_loop(0, H, build_toeplitz, 0)

    def chunk_operand(hf):
        cols = []
        for k in range(H // 2):
            a, b = u_ref[2 * k].astype(F32), u_ref[2 * k + 1].astype(F32)
            if hf == 0:
                cols.append(jnp.where(half_lane, a, pltpu.roll(b, C, 1)))
            else:
                cols.append(jnp.where(half_lane, pltpu.roll(a, C, 1), b))
        return jnp.concatenate(cols, axis=1).astype(BF16)

    ys = []
    for hf in range(SSM_HALVES):
        u = chunk_operand(hf)
        ys.append(jnp.dot(u, m_scr[...], preferred_element_type=F32))
        s = jnp.dot(u, ws_scr[...], preferred_element_type=F32)
        for d in range(2):
            s_d = s[:, d * LANES:(d + 1) * LANES]
            s_scr[hf, d] = s_d
            s_scr[hf, 2 + d] = pltpu.roll(s_d, SSM_STATE, 1)

    n_lat, n_ctx = SEQ // SSM_T, CTX_LEN // SSM_T
    ctx = [((N_CHUNK_LAT + c, n_ctx), hf) for c in range(n_ctx) for hf in range(SSM_HALVES)]
    lat = [((k, n_lat), hf) for k in range(n_lat) for hf in range(SSM_HALVES)]
    coef = [jnp.broadcast_to(at_ref[0, r:r + 1, :], (BATCH, 2 * SSM_STATE)) for r in range(4)]

    def advance(h, h_sw, d, rows, hf):
        a0, a1 = coef[2 * d], coef[2 * d + 1]
        return (h * a0 + h_sw * a1 + s_scr[hf, d, rows, :],
                h_sw * a0 - h * a1 + s_scr[hf, 2 + d, rows, :])

    zero = jnp.zeros((BATCH, 2 * SSM_STATE), F32)
    h_f, h_f_sw, h_r, h_r_sw = zero, zero, zero, zero
    for ((sf, stf), cf), ((sr, strd), cr) in zip(ctx + lat, ctx[::-1] + lat[::-1]):
        rows_f = pl.ds(sf, BATCH, stride=stf)
        rows_r = pl.ds(sr, BATCH, stride=strd)
        h_scr[cf, 0, rows_f, :] = h_f
        h_f, h_f_sw = advance(h_f, h_f_sw, 0, rows_f, cf)
        h_scr[cr, 1, rows_r, :] = h_r
        h_r, h_r_sw = advance(h_r, h_r_sw, 1, rows_r, cr)

    for hf in range(SSM_HALVES):
        h_in = jnp.concatenate([h_scr[hf, 0], h_scr[hf, 1]], axis=-1).astype(BF16)
        ys[hf] = ys[hf] + lax.dot_general(h_in, wct_scr[...], NT_DIMS,
                                          preferred_element_type=F32)
    for k in range(H // 2):
        y0, y1 = ys[0][:, k * LANES:(k + 1) * LANES], ys[1][:, k * LANES:(k + 1) * LANES]
        for c, yc in ((2 * k, jnp.where(half_lane, y0, pltpu.roll(y1, C, 1))),
                      (2 * k + 1, jnp.where(half_lane, pltpu.roll(y0, C, 1), y1))):
            skip = u_ref[c].astype(F32) * dv_ref[0, :, c * SSM_T:(c + 1) * SSM_T]
            y_ref[c] = (yc + skip).astype(y_ref.dtype)


def _ssm_call(u_t, pta, ptb, rows, cc, at, dv):
    g3 = lambda g: (g, 0, 0)
    return pl.pallas_call(
        _ssm_kernel,
        grid=(N_SSM_GROUPS,),
        in_specs=[
            pl.BlockSpec((SSM_GROUP, N_CHUNK, SSM_T), g3),
            pl.BlockSpec((1, PT_ROWS, 4 * SSM_STATE), g3),
            pl.BlockSpec((1, PT_ROWS, 4 * SSM_STATE), g3),
            pl.BlockSpec((1, 4 * SSM_GROUP, 4 * SSM_STATE), g3),
            pl.BlockSpec((1, SSM_GROUP, 4 * SSM_STATE), g3),
            pl.BlockSpec((1, 4, 2 * SSM_STATE), g3),
            pl.BlockSpec((1, 1, SSM_K), g3),
        ],
        out_specs=pl.BlockSpec((SSM_GROUP, N_CHUNK, SSM_T), g3),
        out_shape=jax.ShapeDtypeStruct((D_SSM, N_CHUNK, SSM_T), BF16),
        scratch_shapes=[
            pltpu.VMEM((SSM_GROUP * LAG_ROWS, 4 * SSM_STATE), BF16),
            pltpu.VMEM((SSM_GROUP * LAG_ROWS, 4 * SSM_STATE), BF16),
            pltpu.VMEM((2, SSM_GROUP, SSM_GROUP * LAG_ROWS), F32),
            pltpu.VMEM((SSM_KC, SSM_KC), BF16),
            pltpu.VMEM((SSM_KC, 4 * SSM_STATE), BF16),
            pltpu.VMEM((SSM_KC, 4 * SSM_STATE), BF16),
            pltpu.VMEM((SSM_HALVES, 4, N_CHUNK, 2 * SSM_STATE), F32),
            pltpu.VMEM((SSM_HALVES, 2, N_CHUNK, 2 * SSM_STATE), F32),
        ],
        compiler_params=_cparams("parallel"),
        name="s5_mixer",
    )(u_t, pta, ptb, rows, cc, at, dv)


def _ssm_tables(a_re, a_im, log_dt, b_re, b_im, c_re, c_im, d_skip):
    G, P, H, T = N_SSM_GROUPS, SSM_STATE, SSM_GROUP, SSM_T
    a_re, a_im = a_re.astype(F32), a_im.astype(F32)
    dt = jnp.exp(log_dt.astype(F32))[..., None]
    den = a_re * a_re + a_im * a_im
    mag1 = jnp.exp(dt * a_re)
    ab_re, ab_im = mag1 * jnp.cos(dt * a_im), mag1 * jnp.sin(dt * a_im)
    num_re = ab_re - 1.0
    f_re = (num_re * a_re + ab_im * a_im) / den
    f_im = (ab_im * a_re - num_re * a_im) / den
    b_re, b_im = b_re.astype(F32), b_im.astype(F32)
    bb_re = f_re[..., None] * b_re - f_im[..., None] * b_im
    bb_im = f_re[..., None] * b_im + f_im[..., None] * b_re
    c_re, c_im = c_re.astype(F32), c_im.astype(F32)
    la, th = dt * a_re, dt * a_im

    def powers(d, n):
        mag = jnp.exp(la[d][:, None, :] * n[None, :, None])
        ph = th[d][:, None, :] * n[None, :, None]
        return mag * jnp.cos(ph), mag * jnp.sin(ph)

    C = SSM_C
    slot = np.arange(LAG_ROWS)
    lag_f = np.where(slot < C, slot, -1)
    lag_r = np.where((-slot) % LAG_ROWS < C, (-slot) % LAG_ROWS, -1)
    s_idx = np.arange(C)
    f_all = np.concatenate([lag_f, C - 1 - s_idx, s_idx + 1])
    r_all = np.concatenate([lag_r, s_idx, C - s_idx])
    f_exp, r_exp = jnp.asarray(np.maximum(f_all, 0), F32), jnp.asarray(np.maximum(r_all, 0), F32)
    f_on, r_on = jnp.asarray(f_all >= 0, F32), jnp.asarray(r_all >= 0, F32)
    lane = jnp.arange(4 * P)
    is_fwd = (lane < 2 * P)[None, :]
    is_re = ((lane // P) % 2 == 0)[None, None, :]
    expo = jnp.where(is_fwd, f_exp[:, None], r_exp[:, None])[None]
    on = jnp.where(is_fwd, f_on[:, None], r_on[:, None])[None]
    la4 = jnp.concatenate([la[0], la[0], la[1], la[1]], axis=-1)[:, None, :]
    th4 = jnp.concatenate([th[0], th[0], th[1], th[1]], axis=-1)[:, None, :]
    mag = jnp.exp(la4 * expo) * on
    p_re, p_im = mag * jnp.cos(th4 * expo), mag * jnp.sin(th4 * expo)
    pta = jnp.where(is_re, p_re, p_im)
    ptb = jnp.where(is_re, -p_im, p_re)

    def per_channel(v):
        f, r = v[0].transpose(0, 2, 1), v[1].transpose(0, 2, 1)
        return jnp.concatenate([f, f, r, r], axis=-1)
    cf_re, cr_re = c_re[0], c_re[1]
    cf_im, cr_im = c_im[0], c_im[1]
    rows = jnp.concatenate([
        per_channel(bb_re), per_channel(bb_im),
        jnp.concatenate([cf_re, -cf_re, cr_re, -cr_re], axis=-1),
        jnp.concatenate([cf_im, -cf_im, cr_im, -cr_im], axis=-1)], axis=1)
    cc = jnp.concatenate([cf_re, -cf_im, cr_re, -cr_im], axis=-1)

    t_exp = jnp.full((1,), float(SSM_C), F32)
    (f_re_t, f_im_t), (r_re_t, r_im_t) = powers(0, t_exp), powers(1, t_exp)
    f_re_t, f_im_t, r_re_t, r_im_t = (v[:, 0, :] for v in (f_re_t, f_im_t, r_re_t, r_im_t))
    at = jnp.stack([jnp.concatenate([f_re_t, f_re_t], -1), jnp.concatenate([-f_im_t, f_im_t], -1),
                    jnp.concatenate([r_re_t, r_re_t], -1), jnp.concatenate([-r_im_t, r_im_t], -1)],
                   axis=1)
    dv = jnp.repeat(d_skip.astype(F32).reshape(G, H), T, axis=-1).reshape(G, 1, H * T)
    return pta, ptb, rows, cc, at, dv


ROUTE_E1, ROUTE_E2, ROUTE_W1, ROUTE_W2 = 0, 1, 2, 3


def _top2_route(logits):
    lane = lax.broadcasted_iota(jnp.int32, logits.shape, 1)
    lg = jnp.where(lane < N_EXPERTS, logits, -jnp.inf)
    m1 = jnp.max(lg, axis=-1, keepdims=True)
    i1 = jnp.min(jnp.where(lg == m1, lane, LANES), axis=-1, keepdims=True)
    lg2 = jnp.where(lane == i1, -jnp.inf, lg)
    m2 = jnp.max(lg2, axis=-1, keepdims=True)
    i2 = jnp.min(jnp.where(lg2 == m2, lane, LANES), axis=-1, keepdims=True)
    e2 = jnp.exp(m2 - m1)
    w1 = 1.0 / (1.0 + e2)
    rec = jnp.where(lane == ROUTE_E1, i1.astype(F32), 0.0)
    rec = jnp.where(lane == ROUTE_E2, i2.astype(F32), rec)
    rec = jnp.where(lane == ROUTE_W1, w1, rec)
    return jnp.where(lane == ROUTE_W2, e2 * w1, rec), i1, i2


def _expert_slots(i1, i2, taken):
    n = i1.shape[0]
    lane = lax.broadcasted_iota(jnp.int32, (n, LANES), 1)
    picked = jnp.where((lane == i1) | (lane == i2), 1.0, 0.0)
    earlier = (lax.broadcasted_iota(jnp.int32, (n, n), 1)
               < lax.broadcasted_iota(jnp.int32, (n, n), 0)).astype(BF16)
    rank = jnp.dot(earlier, picked.astype(BF16), preferred_element_type=F32)
    slot = rank + taken + (lane * EXPERT_CAP).astype(F32)
    d1 = jnp.sum(jnp.where(lane == i1, slot, 0.0), axis=-1, keepdims=True)
    d2 = jnp.sum(jnp.where(lane == i2, slot, 0.0), axis=-1, keepdims=True)
    dest = jnp.where(lane == 0, d1, jnp.where(lane == 1, d2, 0.0)).astype(jnp.int32)
    return dest, jnp.sum(picked, axis=0, keepdims=True)


def _pack_bf16_pairs(v):
    k = v.shape[1] // 2
    bits = pltpu.bitcast(v.astype(BF16).astype(F32), jnp.uint32)
    return (bits[:, :k] & jnp.uint32(0xFFFF0000)) | (bits[:, k:] >> 16)


def _unpack_bf16_pairs(w):
    hi = pltpu.bitcast(w & jnp.uint32(0xFFFF0000), F32)
    lo = pltpu.bitcast(w << 16, F32)
    return hi, lo


def _store_packed(ref, v):
    words = _pack_bf16_pairs(v)
    for s in range(SC_SPLIT):
        ref[s] = words[:, s * SC_ROW:(s + 1) * SC_ROW]


def _load_packed(ref):
    return _unpack_bf16_pairs(jnp.concatenate([ref[s] for s in range(SC_SPLIT)], axis=-1))


def _mix_kernel(*refs, with_router, n_x):
    rest = refs[n_x:]
    if with_router:
        (yst_ref, ya_ref, mod_ref, wglut_ref, bglu_ref, wout_ref, gffn_ref, router_ref,
         x1_ref, h2_ref, gate_ref, dest_ref, cnt_ref, taken_scr) = rest
    else:
        (yst_ref, ya_ref, mod_ref, wglut_ref, bglu_ref, wout_ref, gffn_ref,
         w1_ref, w3_ref, w2_ref, x2_ref) = rest
    zt = jax.nn.gelu(yst_ref[...].astype(F32), approximate=True)
    glt = zt * jax.nn.sigmoid(
        jnp.dot(wglut_ref[...], zt.astype(BF16), preferred_element_type=F32) + bglu_ref[...])
    mix = jnp.concatenate([glt.T.astype(BF16), ya_ref[...]], axis=-1)
    o = jnp.dot(mix, wout_ref[...], preferred_element_type=F32)
    x1 = _tile_rows(refs, n_x) + mod_ref[0, 2:3, :] * o
    h2 = _rms(x1, gffn_ref[...]) * (1.0 + mod_ref[0, 4:5, :]) + mod_ref[0, 3:4, :]
    if not with_router:
        acc = _swiglu(h2.astype(BF16), lambda sl: w1_ref[:, sl], lambda sl: w3_ref[:, sl],
                      lambda sl: w2_ref[sl, :], D_FF)
        x2_ref[...] = x1 + mod_ref[0, 5:6, :] * acc
    else:
        x1_ref[...] = x1
        _store_packed(h2_ref, h2)
        h_hi = h2.astype(BF16)
        h_lo = (h2 - h_hi.astype(F32)).astype(BF16)
        o1 = jnp.dot(h_hi, router_ref[0], preferred_element_type=F32)
        o2 = jnp.dot(h_lo, router_ref[1], preferred_element_type=F32)
        logits = o1 + pltpu.roll(o1, LANES - N_EXPERTS, 1) + o2
        gate_ref[...], i1, i2 = _top2_route(logits)

        @pl.when(pl.program_id(0) == 0)
        def _():
            taken_scr[...] = jnp.zeros_like(taken_scr)

        dest_ref[...], tile_cnt = _expert_slots(i1, i2, taken_scr[...])
        taken_scr[...] += tile_cnt
        cnt_ref[...] = jnp.broadcast_to(taken_scr[...], cnt_ref.shape)


def _mix_call(xs, y_ssm_t, y_att, mod, wglut, bglu, wout, gffn, *, router=None, ffn=None, n_tiles):
    row = lambda i: (i, 0)
    out_row = row
    with_router = router is not None
    assert with_router != (ffn is not None)
    in_specs = _x_specs(len(xs)) + [
        pl.BlockSpec((D_SSM, TM), lambda i: (0, i)),
        pl.BlockSpec((TM, D_ATT), row),
        pl.BlockSpec((1, 6, D_MODEL), lambda i: (_mod_row(i), 0, 0)),
        _const_spec((D_SSM, D_SSM)),
        _const_spec((D_SSM, 1)),
        _const_spec((D_MIX, D_MODEL)),
        _const_spec((1, D_MODEL)),
    ]
    args = [*xs, y_ssm_t, y_att, mod, wglut, bglu, wout, gffn]
    out_specs = [pl.BlockSpec((TM, D_MODEL), out_row)]
    out_shape = [jax.ShapeDtypeStruct((n_tiles * TM, D_MODEL), F32)]
    if with_router:
        in_specs.append(_const_spec((2, D_MODEL, LANES)))
        args.append(router)
        out_specs += [pl.BlockSpec((SC_SPLIT, TM, SC_ROW), lambda i: (0, i, 0)),
                      pl.BlockSpec((TM, LANES), out_row),
                      pl.BlockSpec((TM, LANES), out_row),
                      pl.BlockSpec((8, LANES), lambda i: (0, 0))]
        out_shape += [jax.ShapeDtypeStruct((SC_SPLIT, n_tiles * TM, SC_ROW), jnp.uint32),
                      jax.ShapeDtypeStruct((n_tiles * TM, LANES), F32),
                      jax.ShapeDtypeStruct((n_tiles * TM, LANES), jnp.int32),
                      jax.ShapeDtypeStruct((8, LANES), F32)]
        scratch = [pltpu.VMEM((1, LANES), F32)]
    else:
        in_specs += [_const_spec((D_MODEL, D_FF)), _const_spec((D_MODEL, D_FF)),
                     _const_spec((D_FF, D_MODEL))]
        args += list(ffn)
        scratch = []
    return pl.pallas_call(
        functools.partial(_mix_kernel, with_router=with_router, n_x=len(xs)),
        grid=(n_tiles,),
        in_specs=in_specs,
        out_specs=out_specs,
        out_shape=out_shape,
        scratch_shapes=scratch,
        compiler_params=_cparams("arbitrary" if with_router else "parallel"),
        name="mix_out",
    )(*args)


def _swiglu(h, w1_at, w3_at, w2_at, d_ff):
    acc = jnp.zeros((h.shape[0], D_MODEL), F32)
    for lo in range(0, d_ff, FF_CHUNK):
        sl = slice(lo, min(lo + FF_CHUNK, d_ff))
        a = jnp.dot(h, w1_at(sl), preferred_element_type=F32)
        b = jnp.dot(h, w3_at(sl), preferred_element_type=F32)
        g = (a * jax.nn.sigmoid(a) * b).astype(BF16)
        acc = acc + jnp.dot(g, w2_at(sl), preferred_element_type=F32)
    return acc


EXPERT_BLK = 512
EXPERT_CAP = N_LAT
N_SORTED = N_EXPERTS * EXPERT_CAP
N_EXPERT_BLKS = 2 * N_LAT // EXPERT_BLK + N_EXPERTS
PACKED = D_MODEL // 2
SC_ROW = 256
SC_SPLIT = PACKED // SC_ROW
SC_WIN = 128


def _sc_mesh():
    return plsc.VectorSubcoreMesh(core_axis_name="core", subcore_axis_name="subcore")


def _sc_scatter(x, idx_a, idx_b, n_out):
    n = x.shape[0]

    @pl.kernel(out_type=jax.ShapeDtypeStruct((n_out, SC_ROW), x.dtype), mesh=_sc_mesh(),
               scratch_types=[])
    def scatter(x_hbm, a_hbm, b_hbm, o_hbm):
        def body(x_vmem, a_vmem, b_vmem):
            pltpu.sync_copy(x_vmem, o_hbm.at[a_vmem.at[0]])
            pltpu.sync_copy(x_vmem, o_hbm.at[b_vmem.at[0]])

        pltpu.emit_pipeline(
            body, grid=(n // SC_WIN,),
            in_specs=[pl.BlockSpec((SC_WIN, SC_ROW), lambda i: (i, 0)),
                      pl.BlockSpec((1, SC_WIN), lambda i: (0, i)),
                      pl.BlockSpec((1, SC_WIN), lambda i: (0, i))],
            out_specs=[],
            core_axis_name=("core", "subcore"),
            dimension_semantics=(pltpu.PARALLEL,),
        )(x_hbm, a_hbm, b_hbm)

    return scatter(x, idx_a.reshape(1, n), idx_b.reshape(1, n))


def _sc_gather(y, idx):
    n = idx.shape[0]

    @pl.kernel(out_type=jax.ShapeDtypeStruct((n, SC_ROW), y.dtype), mesh=_sc_mesh(),
               scratch_types=[])
    def gather(y_hbm, i_hbm, o_hbm):
        def body(i_vmem, o_vmem):
            pltpu.sync_copy(y_hbm.at[i_vmem.at[0]], o_vmem)

        pltpu.emit_pipeline(
            body, grid=(n // SC_WIN,),
            in_specs=[pl.BlockSpec((1, SC_WIN), lambda i: (0, i))],
            out_specs=[pl.BlockSpec((SC_WIN, SC_ROW), lambda i: (i, 0))],
            core_axis_name=("core", "subcore"),
            dimension_semantics=(pltpu.PARALLEL,),
        )(i_hbm, o_hbm)

    return gather(y, idx.reshape(1, n))


W_CAST_ROWS = 128


def _experts_kernel(blk_expert_ref, blk_row_ref, n_used_ref, x_ref, w1_ref, w3_ref, w2_ref, o_ref,
                    w1_scr, w3_scr, w2_scr):
    del blk_row_ref
    b = pl.program_id(0)
    live = b < n_used_ref[0]
    new_expert = (b == 0) | (blk_expert_ref[b] != blk_expert_ref[jnp.maximum(b - 1, 0)])

    @pl.when(live & new_expert)
    def _():
        for src, dst in ((w1_ref, w1_scr), (w3_ref, w3_scr), (w2_ref, w2_scr)):
            for r in range(0, dst.shape[0], W_CAST_ROWS):
                dst[r:r + W_CAST_ROWS, :] = src[0, r:r + W_CAST_ROWS, :].astype(BF16)

    @pl.when(live)
    def _():
        hi, lo = _load_packed(x_ref)
        h = jnp.concatenate([hi.astype(BF16), lo.astype(BF16)], axis=-1)
        y = _swiglu(h, lambda sl: w1_scr[:, sl], lambda sl: w3_scr[:, sl],
                    lambda sl: w2_scr[sl, :], D_FF_EXPERT)
        _store_packed(o_ref, y)


def _experts_call(blk_expert, blk_row, n_used, xs, w1, w3, w2):
    row = lambda b, be, br, nu: (0, br[b], 0)
    wsel = lambda b, be, br, nu: (be[b], 0, 0)
    return pl.pallas_call(
        _experts_kernel,
        grid_spec=pltpu.PrefetchScalarGridSpec(
            num_scalar_prefetch=3,
            grid=(N_EXPERT_BLKS,),
            in_specs=[
                pl.BlockSpec((SC_SPLIT, EXPERT_BLK, SC_ROW), row),
                pl.BlockSpec((1, D_MODEL, D_FF_EXPERT), wsel),
                pl.BlockSpec((1, D_MODEL, D_FF_EXPERT), wsel),
                pl.BlockSpec((1, D_FF_EXPERT, D_MODEL), wsel),
            ],
            out_specs=pl.BlockSpec((SC_SPLIT, EXPERT_BLK, SC_ROW), row),
            scratch_shapes=[pltpu.VMEM((D_MODEL, D_FF_EXPERT), BF16),
                            pltpu.VMEM((D_MODEL, D_FF_EXPERT), BF16),
                            pltpu.VMEM((D_FF_EXPERT, D_MODEL), BF16)],
        ),
        out_shape=jax.ShapeDtypeStruct((SC_SPLIT, N_SORTED, SC_ROW), jnp.uint32),
        compiler_params=_cparams("arbitrary"),
        name="moe_experts",
    )(blk_expert, blk_row, n_used, xs, w1, w3, w2)


def _combine_kernel(x1_ref, r_ref, y_ref, mod_ref, fg_ref, o_ref):
    def expert_out(slot):
        hi, lo = _load_packed(y_ref.at[slot])
        return jnp.concatenate([hi, lo], axis=-1)
    w1 = r_ref[:, ROUTE_W1:ROUTE_W1 + 1]
    w2 = r_ref[:, ROUTE_W2:ROUTE_W2 + 1]
    y = w1 * expert_out(0) + w2 * expert_out(1)
    x2 = x1_ref[...] + mod_ref[0, 5:6, :] * y
    o_ref[...] = _rms(x2, fg_ref[...])


def _combine_call(x1, route, y_tok, mod, fg):
    n_tiles = N_LAT // TM
    row = lambda i: (i, 0)
    return pl.pallas_call(
        _combine_kernel,
        grid=(n_tiles,),
        in_specs=[
            pl.BlockSpec((TM, D_MODEL), row),
            pl.BlockSpec((TM, LANES), row),
            pl.BlockSpec((2, SC_SPLIT, TM, SC_ROW), lambda i: (0, 0, i, 0)),
            pl.BlockSpec((1, 6, D_MODEL), lambda i: (i // SEQ_TILES, 0, 0)),
            pl.BlockSpec((1, D_MODEL), lambda i: (0, 0)),
        ],
        out_specs=pl.BlockSpec((TM, D_MODEL), row),
        out_shape=jax.ShapeDtypeStruct((N_LAT, D_MODEL), F32),
        compiler_params=_cparams("parallel"),
        name="moe_combine",
    )(x1, route, y_tok, mod, fg)


def _moe_routed(h2p, x1, route, dest, cnt, mod, w1, w3, w2, fg):
    blks = (cnt[0, :N_EXPERTS].astype(jnp.int32) + (EXPERT_BLK - 1)) // EXPERT_BLK
    blk_end = jnp.cumsum(blks)
    n_used = blk_end[-1:]
    b = jnp.minimum(jnp.arange(N_EXPERT_BLKS, dtype=jnp.int32), n_used[0] - 1)
    blk_expert = jnp.sum((b[:, None] >= blk_end[None, :]).astype(jnp.int32), axis=1)
    blk_row = blk_expert * (EXPERT_CAP // EXPERT_BLK) + b - (blk_end - blks)[blk_expert]
    piece = jnp.arange(SC_SPLIT, dtype=jnp.int32)[:, None] * N_SORTED
    idx = [(piece + dest[:, slot][None, :]).reshape(SC_SPLIT * N_LAT) for slot in range(2)]
    xs = _sc_scatter(h2p.reshape(SC_SPLIT * N_LAT, SC_ROW), idx[0], idx[1], SC_SPLIT * N_SORTED)
    ys = _experts_call(blk_expert, blk_row, n_used, xs.reshape(SC_SPLIT, N_SORTED, SC_ROW),
                       w1, w3, w2)
    y_tok = _sc_gather(ys.reshape(SC_SPLIT * N_SORTED, SC_ROW), jnp.concatenate(idx))
    return _combine_call(x1, route, y_tok.reshape(2, SC_SPLIT, N_LAT, SC_ROW), mod, fg)


def _rope_partner_perm():
    perm, sign = [], []
    for j in range(QK_ROPE):
        first_half = (j % AXIS_ROPE) < ROPE_FREQS
        perm.append(j + ROPE_FREQS if first_half else j - ROPE_FREQS)
        sign.append(-1.0 if first_half else 1.0)
    return jnp.array(perm, jnp.int32), jnp.array(sign, F32)


def _rope_tables():
    t = jnp.arange(SEQ)
    row = (t // GRID_W).astype(F32)
    col = (t % GRID_W).astype(F32)
    inv_freq = ROPE_BASE ** (-2.0 * jnp.arange(ROPE_FREQS, dtype=F32) / AXIS_ROPE)
    ang = jnp.concatenate([row[:, None] * inv_freq, row[:, None] * inv_freq,
                           col[:, None] * inv_freq, col[:, None] * inv_freq], axis=1)
    cos = jnp.concatenate([jnp.cos(ang), jnp.ones((N_CTX, QK_ROPE), F32)], axis=0)
    sin = jnp.concatenate([jnp.sin(ang), jnp.zeros((N_CTX, QK_ROPE), F32)], axis=0)
    n = N_CTX + SEQ
    pad32 = jnp.zeros((n, HEAD_PAD - D_QK), F32)
    qs = ATT_SCALE * math.log2(math.e)
    cq = jnp.concatenate([jnp.full((n, QK_NOPE), qs, F32), qs * cos, pad32], axis=1)
    sq = jnp.concatenate([jnp.zeros((n, QK_NOPE), F32), qs * sin, pad32], axis=1)
    cs = jnp.concatenate([cos, sin, jnp.zeros((n, LANES - 2 * QK_ROPE), F32)], axis=1)
    return cq, sq, cs


def _layer_weights(w_in, w_uq, w_ukv):
    perm, sign = _rope_partner_perm()
    s0 = D_SSM + Q_LORA + KV_LORA
    kr_w = w_in[:, s0:s0 + QK_ROPE]
    wut = w_in[:, :D_SSM].T.astype(BF16)
    win = jnp.concatenate([w_in[:, D_SSM:s0], kr_w, kr_w[:, perm] * sign,
                           jnp.zeros((D_MODEL, LANES - 2 * QK_ROPE), F32)], axis=1).astype(BF16)
    uq = w_uq.reshape(Q_LORA, N_HEADS, D_QK)
    nope, rope = uq[..., :QK_NOPE], uq[..., QK_NOPE:]
    zpad = jnp.zeros((Q_LORA, N_HEADS, HEAD_PAD - D_QK), F32)
    wq1 = jnp.concatenate([nope, rope, zpad], axis=-1).reshape(Q_LORA, N_HEADS * HEAD_PAD).astype(BF16)
    ukv = w_ukv.reshape(KV_LORA, N_HEADS, QK_NOPE + V_HEAD)
    wk = jnp.concatenate([ukv[..., :QK_NOPE], jnp.zeros((KV_LORA, N_HEADS, HEAD_PAD - QK_NOPE), F32)],
                         axis=-1).reshape(KV_LORA, N_HEADS * HEAD_PAD)
    eye = jnp.eye(QK_ROPE, dtype=F32)
    place = jnp.concatenate([jnp.zeros((QK_ROPE, QK_NOPE), F32), eye,
                             jnp.zeros((QK_ROPE, HEAD_PAD - D_QK), F32)], axis=1)
    place = jnp.tile(place, (1, N_HEADS))
    spread = jnp.concatenate([place, place, jnp.zeros((LANES - 2 * QK_ROPE, N_HEADS * HEAD_PAD), F32)], 0)
    wkk = jnp.concatenate([wk, spread], axis=0).astype(BF16)
    wv = jnp.concatenate([ukv[..., QK_NOPE:], jnp.zeros((KV_LORA, N_HEADS, HEAD_PAD - V_HEAD), F32)],
                         axis=-1).reshape(KV_LORA, N_HEADS * HEAD_PAD).astype(BF16)
    return wut, win, wq1, wkk, wv


def kernel(x, c, ctx, c_ctx, w_ada, b_ada, norm_mix, norm_ffn, w_in, q_norm, kv_norm, w_uq, w_ukv,
           ssm_a_re, ssm_a_im, ssm_log_dt, ssm_b_re, ssm_b_im, ssm_c_re, ssm_c_im, ssm_d, w_glu,
           b_glu, w_out, ffn_w1, ffn_w3, ffn_w2, moe_router, moe_w1, moe_w3, moe_w2, final_norm):
    assert x.shape == (BATCH, SEQ, D_MODEL) and ctx.shape == (BATCH, CTX_LEN, D_MODEL)
    cond = jnp.concatenate([c, c_ctx[None, :], jnp.zeros((MOD_ROWS - BATCH - 1, D_MODEL), F32)], axis=0)
    mod_all = _ada_call(cond, w_ada, b_ada).reshape(DEPTH, MOD_ROWS, 6, D_MODEL)
    cq_t, sq_t, cs_t = _rope_tables()
    xs = (x.reshape(N_LAT, D_MODEL), ctx.reshape(N_CTX, D_MODEL))

    out = None
    for i in range(DEPTH):
        last = i == DEPTH - 1
        mod = mod_all[i]
        wut, win, wq1, wkk, wv = _layer_weights(w_in[i], w_uq[i], w_ukv[i])
        u_t, q, k, v = _inproj_call(xs, mod, norm_mix[i][None, :], wut, win, q_norm[i][None, :],
                                    kv_norm[i][None, :], wq1, wkk, wv, cq_t, sq_t, cs_t)
        tabs = _ssm_tables(ssm_a_re[i], ssm_a_im[i], ssm_log_dt[i], ssm_b_re[i], ssm_b_im[i],
                           ssm_c_re[i], ssm_c_im[i], ssm_d[i])
        y_ssm_t = _ssm_call(u_t.reshape(D_SSM, N_CHUNK, SSM_T), *tabs).reshape(D_SSM, N_TOT)
        y_att = _attn_latent_call(q, k, v)
        if last:
            n_tiles = LAT_TILES
        else:
            y_att = _attn_ctx_call(q, k, v, y_att)
            n_tiles = N_TOT // TM
        j = i // 2
        mix_args = (xs, y_ssm_t, y_att, mod, w_glu[i].T.astype(BF16), b_glu[i][:, None],
                    w_out[i].astype(BF16), norm_ffn[i][None, :])
        if i % 2 == 0:
            assert not last
            ffn = (ffn_w1[j].astype(BF16), ffn_w3[j].astype(BF16), ffn_w2[j].astype(BF16))
            xs = tuple(_mix_call(*mix_args, ffn=ffn, n_tiles=n_tiles))
        else:
            assert last
            r = moe_router[j]
            r_top = lax.bitcast_convert_type(
                lax.bitcast_convert_type(r, jnp.uint32) & jnp.uint32(0xFFFF0000), F32)
            r_hi = r_top.astype(BF16)
            r_lo = (r - r_top).astype(BF16)
            zr = jnp.zeros((D_MODEL, LANES - 2 * N_EXPERTS), BF16)
            router = jnp.stack([jnp.concatenate([r_hi, r_lo, zr], axis=1),
                                jnp.concatenate([r_hi, jnp.zeros_like(r_lo), zr], axis=1)])
            x1, h2p, route, dest, cnt = _mix_call(*mix_args, router=router, n_tiles=n_tiles)
            out = _moe_routed(h2p, x1, route, dest, cnt, mod, moe_w1[j], moe_w3[j], moe_w2[j],
                              final_norm[None, :])
    return out.reshape(BATCH, SEQ, D_MODEL)
```
